```python
import math
import jax, jax.numpy as jnp
from jax import lax
import numpy as np


D_MODEL = 1024
BATCH = 8
SEQ = 2048
DEPTH = 2

CHUNK = 64
EPS = 1e-6

GM_WIDTH = 1024
GM_GROUPS = 4
GM_BLOCK = 128

MLA_HEADS = 8
MLA_NOPE = 128
MLA_ROPE = 64
MLA_VDIM = 128
MLA_QK_DIM = MLA_NOPE + MLA_ROPE
MLA_Q_RANK = 384
MLA_KV_RANK = 256
MLA_WIDTH = MLA_HEADS * MLA_VDIM
ROPE_THETA = 10000.0
Q_BLOCK = 128

LRU_WIDTH = 1280
LRU_BLOCKS = 16
LRU_BW = LRU_WIDTH // LRU_BLOCKS
LRU_C = 8.0
CONV_W = 4

IN_SIZES = (GM_WIDTH, GM_WIDTH, GM_WIDTH, MLA_Q_RANK, MLA_KV_RANK, MLA_ROPE, MLA_WIDTH,
            LRU_WIDTH, LRU_WIDTH, D_MODEL, D_MODEL, D_MODEL)
N_IN = sum(IN_SIZES)

kernel_name = "hybrid_gmlp_mla_rglru_chunk_causal"


def _split_points():
    return [int(s) for s in np.cumsum(IN_SIZES)[:-1]]


def rmsnorm(x, g):
    xf = x.astype(jnp.float32)
    y = xf * lax.rsqrt(jnp.mean(xf * xf, axis=-1, keepdims=True) + EPS)
    return (y * g.astype(jnp.float32)).astype(x.dtype)


def layernorm(x, g, b):
    xf = x.astype(jnp.float32)
    mu = jnp.mean(xf, axis=-1, keepdims=True)
    var = jnp.mean(jnp.square(xf - mu), axis=-1, keepdims=True)
    y = (xf - mu) * lax.rsqrt(var + EPS)
    return (y * g.astype(jnp.float32) + b.astype(jnp.float32)).astype(x.dtype)


def rope(t, cos, sin):
    half = t.shape[-1] // 2
    t1, t2 = t[..., :half], t[..., half:]
    return jnp.concatenate([t1 * cos - t2 * sin, t2 * cos + t1 * sin], axis=-1)


def gmlp_spatial(u, v, ln_g, ln_b, ws, bs):
    B, S, _ = v.shape
    v = layernorm(v, ln_g, ln_b)
    vb = v.reshape(B, S // GM_BLOCK, GM_BLOCK, GM_GROUPS, GM_WIDTH // GM_GROUPS)
    idx = jnp.arange(GM_BLOCK)
    mask = (idx[None, :] // CHUNK) <= (idx[:, None] // CHUNK)
    ws_m = jnp.where(mask[None], ws, jnp.zeros_like(ws))
    sv = jnp.einsum('gij,bnjgc->bnigc', ws_m, vb) + bs.T[None, None, :, :, None]
    return u * sv.reshape(B, S, GM_WIDTH)


def mla_attention(c_q, c_kv, k_rope_in, q_norm_g, w_uq, kv_norm_g, w_ukv):
    B, S, _ = c_q.shape
    H = MLA_HEADS
    q = (rmsnorm(c_q, q_norm_g) @ w_uq).reshape(B, S, H, MLA_QK_DIM)
    kv = (rmsnorm(c_kv, kv_norm_g) @ w_ukv).reshape(B, S, H, MLA_NOPE + MLA_VDIM)
    q_nope, q_rope = q[..., :MLA_NOPE], q[..., MLA_NOPE:]
    k_nope, v = kv[..., :MLA_NOPE], kv[..., MLA_NOPE:]

    pos = jnp.arange(S, dtype=jnp.float32)
    inv_freq = ROPE_THETA ** (-jnp.arange(0, MLA_ROPE, 2, dtype=jnp.float32) / MLA_ROPE)
    ang = pos[:, None] * inv_freq[None, :]
    cos = jnp.cos(ang).astype(q.dtype)
    sin = jnp.sin(ang).astype(q.dtype)
    q_rope = rope(q_rope, cos[None, :, None, :], sin[None, :, None, :])
    k_rope = rope(k_rope_in, cos[None], sin[None])

    q = jnp.concatenate([q_nope, q_rope], axis=-1)
    k = jnp.concatenate([k_nope, jnp.broadcast_to(k_rope[:, :, None, :], (B, S, H, MLA_ROPE))], axis=-1)
    scale = 1.0 / math.sqrt(MLA_QK_DIM)
    nb = S // Q_BLOCK
    qb = q.reshape(B, nb, Q_BLOCK, H, MLA_QK_DIM).transpose(1, 0, 2, 3, 4)
    key_chunk = jnp.arange(S) // CHUNK

    def one_block(args):
        qi, bi = args
        s = jnp.einsum('bqhd,bkhd->bhqk', qi, k).astype(jnp.float32) * scale
        q_chunk = (bi * Q_BLOCK + jnp.arange(Q_BLOCK)) // CHUNK
        mask = key_chunk[None, :] <= q_chunk[:, None]
        s = jnp.where(mask[None, None], s, -1e30)
        p = jax.nn.softmax(s, axis=-1).astype(v.dtype)
        return jnp.einsum('bhqk,bkhd->bqhd', p, v)

    o = lax.map(one_block, (qb, jnp.arange(nb)))
    return o.transpose(1, 0, 2, 3, 4).reshape(B, S, MLA_WIDTH)


def rg_lru(x_c, conv_w, conv_b, w_a, b_a, w_x, b_x, lam):
    B, S, C = x_c.shape
    xc = lax.conv_general_dilated(
        x_c, conv_w[:, None, :].astype(x_c.dtype), window_strides=(1,), padding=[(CONV_W - 1, 0)],
        dimension_numbers=('NWC', 'WIO', 'NWC'), feature_group_count=C) + conv_b
    xb = xc.reshape(B, S, LRU_BLOCKS, LRU_BW)
    r = jax.nn.sigmoid(jnp.einsum('bshi,hij->bshj', xb, w_a).reshape(B, S, C) + b_a)
    i = jax.nn.sigmoid(jnp.einsum('bshi,hij->bshj', xb, w_x).reshape(B, S, C) + b_x)
    rf = r.astype(jnp.float32)
    log_a = -LRU_C * rf * jax.nn.softplus(-lam.astype(jnp.float32))
    a = jnp.exp(log_a)
    mult = jnp.sqrt(jnp.maximum(1.0 - jnp.exp(2.0 * log_a), 0.0))
    bterm = mult * (i.astype(jnp.float32) * xc.astype(jnp.float32))

    def combine(e1, e2):
        a1, b1 = e1
        a2, b2 = e2
        return a1 * a2, a2 * b1 + b2

    _, h = lax.associative_scan(combine, (a, bterm), axis=1)
    return h.astype(x_c.dtype)


def _fwd_setup_inputs(seed: int = 0) -> dict:
    key = jax.random.key(seed)
    ks = iter(jax.random.split(key, 32))
    L, D = DEPTH, D_MODEL

    def nrm(shape, scale):
        return jax.random.normal(next(ks), shape, jnp.float32) * scale

    def gain(shape):
        return 1.0 + nrm(shape, 0.02)

    a_init = jax.random.uniform(next(ks), (L, LRU_WIDTH), jnp.float32, 0.9, 0.999)
    s = a_init ** (1.0 / LRU_C)
    lam = jnp.log(s) - jnp.log1p(-s)

    return {
        "x": nrm((BATCH, SEQ, D), 1.0),
        "pre_norm_g": gain((L, D)),
        "w_in": nrm((L, D, N_IN), D ** -0.5),
        "gm_ln_g": gain((L, GM_WIDTH)),
        "gm_ln_b": nrm((L, GM_WIDTH), 0.02),
        "gm_ws": nrm((L, GM_GROUPS, GM_BLOCK, GM_BLOCK), GM_BLOCK ** -0.5),
        "gm_bs": 1.0 + nrm((L, GM_GROUPS, GM_BLOCK), 0.02),
        "mla_q_norm_g": gain((L, MLA_Q_RANK)),
        "mla_w_uq": nrm((L, MLA_Q_RANK, MLA_HEADS * MLA_QK_DIM), MLA_Q_RANK ** -0.5),
        "mla_kv_norm_g": gain((L, MLA_KV_RANK)),
        "mla_w_ukv": nrm((L, MLA_KV_RANK, MLA_HEADS * (MLA_NOPE + MLA_VDIM)), MLA_KV_RANK ** -0.5),
        "lru_conv_w": nrm((L, CONV_W, LRU_WIDTH), CONV_W ** -0.5),
        "lru_conv_b": nrm((L, LRU_WIDTH), 0.01),
        "lru_w_a": nrm((L, LRU_BLOCKS, LRU_BW, LRU_BW), LRU_BW ** -0.5),
        "lru_b_a": nrm((L, LRU_WIDTH), 0.01),
        "lru_w_x": nrm((L, LRU_BLOCKS, LRU_BW, LRU_BW), LRU_BW ** -0.5),
        "lru_b_x": nrm((L, LRU_WIDTH), 0.01),
        "lru_lambda": lam,
        "w_proj_a": nrm((L, GM_WIDTH, D), GM_WIDTH ** -0.5),
        "w_proj_b": nrm((L, MLA_WIDTH, D), MLA_WIDTH ** -0.5),
        "w_proj_c": nrm((L, LRU_WIDTH, D), LRU_WIDTH ** -0.5),
        "w_out": nrm((L, D, D), D ** -0.5),
        "post_norm_g": gain((L, D)),
    }


def _fwd_reference(x, pre_norm_g, w_in, gm_ln_g, gm_ln_b, gm_ws, gm_bs, mla_q_norm_g, mla_w_uq,
              mla_kv_norm_g, mla_w_ukv, lru_conv_w, lru_conv_b, lru_w_a, lru_b_a, lru_w_x,
              lru_b_x, lru_lambda, w_proj_a, w_proj_b, w_proj_c, w_out, post_norm_g):
    cuts = _split_points()
    for l in range(DEPTH):
        h = rmsnorm(x, pre_norm_g[l])
        proj = h @ w_in[l]
        (u, v, z_a, c_q, c_kv, k_rope, z_b, x_c, z_c,
         g_a, g_b, g_c) = jnp.split(proj, cuts, axis=-1)

        y_a = gmlp_spatial(u, v, gm_ln_g[l], gm_ln_b[l], gm_ws[l], gm_bs[l]) * jax.nn.silu(z_a)
        y_b = mla_attention(c_q, c_kv, k_rope, mla_q_norm_g[l], mla_w_uq[l],
                            mla_kv_norm_g[l], mla_w_ukv[l]) * jax.nn.silu(z_b)
        y_c = rg_lru(x_c, lru_conv_w[l], lru_conv_b[l], lru_w_a[l], lru_b_a[l],
                     lru_w_x[l], lru_b_x[l], lru_lambda[l]) * jax.nn.silu(z_c)

        merged = (jax.nn.sigmoid(g_a) * (y_a @ w_proj_a[l])
                  + jax.nn.sigmoid(g_b) * (y_b @ w_proj_b[l])
                  + jax.nn.sigmoid(g_c) * (y_c @ w_proj_c[l]))
        x = x + rmsnorm(merged @ w_out[l], post_norm_g[l])
    return x


import jax as _jax
import jax.numpy as _jnp

TWIN_FORMAT = 'train_step'
FWD_PARAMS = ['x', 'pre_norm_g', 'w_in', 'gm_ln_g', 'gm_ln_b', 'gm_ws', 'gm_bs', 'mla_q_norm_g', 'mla_w_uq', 'mla_kv_norm_g', 'mla_w_ukv', 'lru_conv_w', 'lru_conv_b', 'lru_w_a', 'lru_b_a', 'lru_w_x', 'lru_b_x', 'lru_lambda', 'w_proj_a', 'w_proj_b', 'w_proj_c', 'w_out', 'post_norm_g']
TWIN_WEIGHTS = ['pre_norm_g', 'w_in', 'gm_ln_g', 'gm_ln_b', 'gm_ws', 'gm_bs', 'mla_q_norm_g', 'mla_w_uq', 'mla_kv_norm_g', 'mla_w_ukv', 'lru_conv_w', 'lru_conv_b', 'lru_w_a', 'lru_b_a', 'lru_w_x', 'lru_b_x', 'lru_lambda', 'w_proj_a', 'w_proj_b', 'w_proj_c', 'w_out', 'post_norm_g']
TWIN_DIFF_INPUT = 'x'
TWIN_INPUTS = ['x', 'pre_norm_g', 'w_in', 'gm_ln_g', 'gm_ln_b', 'gm_ws', 'gm_bs', 'mla_q_norm_g', 'mla_w_uq', 'mla_kv_norm_g', 'mla_w_ukv', 'lru_conv_w', 'lru_conv_b', 'lru_w_a', 'lru_b_a', 'lru_w_x', 'lru_b_x', 'lru_lambda', 'w_proj_a', 'w_proj_b', 'w_proj_c', 'w_out', 'post_norm_g', 'loss_target', 'm_pre_norm_g', 'm_w_in', 'm_gm_ln_g', 'm_gm_ln_b', 'm_gm_ws', 'm_gm_bs', 'm_mla_q_norm_g', 'm_mla_w_uq', 'm_mla_kv_norm_g', 'm_mla_w_ukv', 'm_lru_conv_w', 'm_lru_conv_b', 'm_lru_w_a', 'm_lru_b_a', 'm_lru_w_x', 'm_lru_b_x', 'm_lru_lambda', 'm_w_proj_a', 'm_w_proj_b', 'm_w_proj_c', 'm_w_out', 'm_post_norm_g', 'v_pre_norm_g', 'v_w_in', 'v_gm_ln_g', 'v_gm_ln_b', 'v_gm_ws', 'v_gm_bs', 'v_mla_q_norm_g', 'v_mla_w_uq', 'v_mla_kv_norm_g', 'v_mla_w_ukv', 'v_lru_conv_w', 'v_lru_conv_b', 'v_lru_w_a', 'v_lru_b_a', 'v_lru_w_x', 'v_lru_b_x', 'v_lru_lambda', 'v_w_proj_a', 'v_w_proj_b', 'v_w_proj_c', 'v_w_out', 'v_post_norm_g']
TWIN_OUTPUTS = ['loss', 'grad_x', 'grad_pre_norm_g', 'grad_w_in', 'grad_gm_ln_g', 'grad_gm_ln_b', 'grad_gm_ws', 'grad_gm_bs', 'grad_mla_q_norm_g', 'grad_mla_w_uq', 'grad_mla_kv_norm_g', 'grad_mla_w_ukv', 'grad_lru_conv_w', 'grad_lru_conv_b', 'grad_lru_w_a', 'grad_lru_b_a', 'grad_lru_w_x', 'grad_lru_b_x', 'grad_lru_lambda', 'grad_w_proj_a', 'grad_w_proj_b', 'grad_w_proj_c', 'grad_w_out', 'grad_post_norm_g', 'delta_pre_norm_g', 'delta_w_in', 'delta_gm_ln_g', 'delta_gm_ln_b', 'delta_gm_ws', 'delta_gm_bs', 'delta_mla_q_norm_g', 'delta_mla_w_uq', 'delta_mla_kv_norm_g', 'delta_mla_w_ukv', 'delta_lru_conv_w', 'delta_lru_conv_b', 'delta_lru_w_a', 'delta_lru_b_a', 'delta_lru_w_x', 'delta_lru_b_x', 'delta_lru_lambda', 'delta_w_proj_a', 'delta_w_proj_b', 'delta_w_proj_c', 'delta_w_out', 'delta_post_norm_g', 'new_m_pre_norm_g', 'new_m_w_in', 'new_m_gm_ln_g', 'new_m_gm_ln_b', 'new_m_gm_ws', 'new_m_gm_bs', 'new_m_mla_q_norm_g', 'new_m_mla_w_uq', 'new_m_mla_kv_norm_g', 'new_m_mla_w_ukv', 'new_m_lru_conv_w', 'new_m_lru_conv_b', 'new_m_lru_w_a', 'new_m_lru_b_a', 'new_m_lru_w_x', 'new_m_lru_b_x', 'new_m_lru_lambda', 'new_m_w_proj_a', 'new_m_w_proj_b', 'new_m_w_proj_c', 'new_m_w_out', 'new_m_post_norm_g', 'new_v_pre_norm_g', 'new_v_w_in', 'new_v_gm_ln_g', 'new_v_gm_ln_b', 'new_v_gm_ws', 'new_v_gm_bs', 'new_v_mla_q_norm_g', 'new_v_mla_w_uq', 'new_v_mla_kv_norm_g', 'new_v_mla_w_ukv', 'new_v_lru_conv_w', 'new_v_lru_conv_b', 'new_v_lru_w_a', 'new_v_lru_b_a', 'new_v_lru_w_x', 'new_v_lru_b_x', 'new_v_lru_lambda', 'new_v_w_proj_a', 'new_v_w_proj_b', 'new_v_w_proj_c', 'new_v_w_out', 'new_v_post_norm_g']
TWIN_LEAF_KINDS = {'loss': 'loss', 'grad_x': 'grad_x', 'grad_pre_norm_g': 'grad_w', 'grad_w_in': 'grad_w', 'grad_gm_ln_g': 'grad_w', 'grad_gm_ln_b': 'grad_w', 'grad_gm_ws': 'grad_w', 'grad_gm_bs': 'grad_w', 'grad_mla_q_norm_g': 'grad_w', 'grad_mla_w_uq': 'grad_w', 'grad_mla_kv_norm_g': 'grad_w', 'grad_mla_w_ukv': 'grad_w', 'grad_lru_conv_w': 'grad_w', 'grad_lru_conv_b': 'grad_w', 'grad_lru_w_a': 'grad_w', 'grad_lru_b_a': 'grad_w', 'grad_lru_w_x': 'grad_w', 'grad_lru_b_x': 'grad_w', 'grad_lru_lambda': 'grad_w', 'grad_w_proj_a': 'grad_w', 'grad_w_proj_b': 'grad_w', 'grad_w_proj_c': 'grad_w', 'grad_w_out': 'grad_w', 'grad_post_norm_g': 'grad_w', 'delta_pre_norm_g': 'delta_w', 'delta_w_in': 'delta_w', 'delta_gm_ln_g': 'delta_w', 'delta_gm_ln_b': 'delta_w', 'delta_gm_ws': 'delta_w', 'delta_gm_bs': 'delta_w', 'delta_mla_q_norm_g': 'delta_w', 'delta_mla_w_uq': 'delta_w', 'delta_mla_kv_norm_g': 'delta_w', 'delta_mla_w_ukv': 'delta_w', 'delta_lru_conv_w': 'delta_w', 'delta_lru_conv_b': 'delta_w', 'delta_lru_w_a': 'delta_w', 'delta_lru_b_a': 'delta_w', 'delta_lru_w_x': 'delta_w', 'delta_lru_b_x': 'delta_w', 'delta_lru_lambda': 'delta_w', 'delta_w_proj_a': 'delta_w', 'delta_w_proj_b': 'delta_w', 'delta_w_proj_c': 'delta_w', 'delta_w_out': 'delta_w', 'delta_post_norm_g': 'delta_w', 'new_m_pre_norm_g': 'new_m', 'new_m_w_in': 'new_m', 'new_m_gm_ln_g': 'new_m', 'new_m_gm_ln_b': 'new_m', 'new_m_gm_ws': 'new_m', 'new_m_gm_bs': 'new_m', 'new_m_mla_q_norm_g': 'new_m', 'new_m_mla_w_uq': 'new_m', 'new_m_mla_kv_norm_g': 'new_m', 'new_m_mla_w_ukv': 'new_m', 'new_m_lru_conv_w': 'new_m', 'new_m_lru_conv_b': 'new_m', 'new_m_lru_w_a': 'new_m', 'new_m_lru_b_a': 'new_m', 'new_m_lru_w_x': 'new_m', 'new_m_lru_b_x': 'new_m', 'new_m_lru_lambda': 'new_m', 'new_m_w_proj_a': 'new_m', 'new_m_w_proj_b': 'new_m', 'new_m_w_proj_c': 'new_m', 'new_m_w_out': 'new_m', 'new_m_post_norm_g': 'new_m', 'new_v_pre_norm_g': 'new_v', 'new_v_w_in': 'new_v', 'new_v_gm_ln_g': 'new_v', 'new_v_gm_ln_b': 'new_v', 'new_v_gm_ws': 'new_v', 'new_v_gm_bs': 'new_v', 'new_v_mla_q_norm_g': 'new_v', 'new_v_mla_w_uq': 'new_v', 'new_v_mla_kv_norm_g': 'new_v', 'new_v_mla_w_ukv': 'new_v', 'new_v_lru_conv_w': 'new_v', 'new_v_lru_conv_b': 'new_v', 'new_v_lru_w_a': 'new_v', 'new_v_lru_b_a': 'new_v', 'new_v_lru_w_x': 'new_v', 'new_v_lru_b_x': 'new_v', 'new_v_lru_lambda': 'new_v', 'new_v_w_proj_a': 'new_v', 'new_v_w_proj_b': 'new_v', 'new_v_w_proj_c': 'new_v', 'new_v_w_out': 'new_v', 'new_v_post_norm_g': 'new_v'}


def _forward(args):
    return _fwd_reference(*[args[k] for k in FWD_PARAMS])


def _output_shape():
    out = _jax.eval_shape(lambda: _forward(_fwd_setup_inputs(0)))
    return out.shape, out.dtype

N_MICROBATCH = 1
ADAM_LR = 0.001
ADAM_B1 = 0.9
ADAM_B2 = 0.999
ADAM_EPS = 1e-08
ADAM_WD = 0.01
ADAM_STEP = 10
PER_EXAMPLE_BATCH_AXIS = {'x': 0, 'loss_target': 0}
SHARED_INPUTS = []
_WEIGHT_DTYPES = {'pre_norm_g': _jnp.float32, 'w_in': _jnp.float32, 'gm_ln_g': _jnp.float32, 'gm_ln_b': _jnp.float32, 'gm_ws': _jnp.float32, 'gm_bs': _jnp.float32, 'mla_q_norm_g': _jnp.float32, 'mla_w_uq': _jnp.float32, 'mla_kv_norm_g': _jnp.float32, 'mla_w_ukv': _jnp.float32, 'lru_conv_w': _jnp.float32, 'lru_conv_b': _jnp.float32, 'lru_w_a': _jnp.float32, 'lru_b_a': _jnp.float32, 'lru_w_x': _jnp.float32, 'lru_b_x': _jnp.float32, 'lru_lambda': _jnp.float32, 'w_proj_a': _jnp.float32, 'w_proj_b': _jnp.float32, 'w_proj_c': _jnp.float32, 'w_out': _jnp.float32, 'post_norm_g': _jnp.float32}
MOMENT_SCALE = {'pre_norm_g': 4.914428e-01, 'w_in': 1.476309e-01, 'gm_ln_g': 1.639734e-01, 'gm_ln_b': 1.688929e-01, 'gm_ws': 2.318485e-01, 'gm_bs': 2.544048e-01, 'mla_q_norm_g': 4.940534e-02, 'mla_w_uq': 2.473449e-02, 'mla_kv_norm_g': 8.571789e-02, 'mla_w_ukv': 2.961139e-02, 'lru_conv_w': 1.913273e-01, 'lru_conv_b': 3.199740e+00, 'lru_w_a': 9.490851e-02, 'lru_b_a': 5.856001e-02, 'lru_w_x': 1.756026e-01, 'lru_b_x': 5.960680e-02, 'lru_lambda': 9.113951e-02, 'w_proj_a': 2.652580e-01, 'w_proj_b': 3.334096e-02, 'w_proj_c': 2.420698e-01, 'w_out': 3.565962e-01, 'post_norm_g': 1.597121e+01}


def _to_microbatches(a, axis):
    t = _jnp.moveaxis(a, axis, 0)
    t = t.reshape((N_MICROBATCH, t.shape[0] // N_MICROBATCH) + t.shape[1:])
    return _jnp.moveaxis(t, 1, axis + 1)


def setup_inputs(seed: int = 0) -> dict:
    inp = _fwd_setup_inputs(seed)
    key = _jax.random.fold_in(_jax.random.key(seed), 7919)
    shape, _ = _output_shape()
    out = dict(inp)
    out["loss_target"] = _jax.random.normal(_jax.random.fold_in(key, 0), shape, _jnp.float32)
    for i, name in enumerate(TWIN_WEIGHTS):
        w = inp[name].astype(_jnp.float32)
        if MOMENT_SCALE is None:
            s = _jnp.sqrt(_jnp.mean(_jnp.square(w)) + 1e-30)
        else:
            s = MOMENT_SCALE[name]
        km, kv = _jax.random.split(_jax.random.fold_in(key, i + 1))
        out[name] = w
        out["m_" + name] = s * _jax.random.normal(km, w.shape, _jnp.float32)
        out["v_" + name] = (s * s) * _jax.random.uniform(kv, w.shape, _jnp.float32, 0.5, 1.5)
    if N_MICROBATCH > 1:
        for name, axis in PER_EXAMPLE_BATCH_AXIS.items():
            out[name] = _to_microbatches(out[name], axis)
    return {'x': out['x'], 'pre_norm_g': out['pre_norm_g'], 'w_in': out['w_in'], 'gm_ln_g': out['gm_ln_g'], 'gm_ln_b': out['gm_ln_b'], 'gm_ws': out['gm_ws'], 'gm_bs': out['gm_bs'], 'mla_q_norm_g': out['mla_q_norm_g'], 'mla_w_uq': out['mla_w_uq'], 'mla_kv_norm_g': out['mla_kv_norm_g'], 'mla_w_ukv': out['mla_w_ukv'], 'lru_conv_w': out['lru_conv_w'], 'lru_conv_b': out['lru_conv_b'], 'lru_w_a': out['lru_w_a'], 'lru_b_a': out['lru_b_a'], 'lru_w_x': out['lru_w_x'], 'lru_b_x': out['lru_b_x'], 'lru_lambda': out['lru_lambda'], 'w_proj_a': out['w_proj_a'], 'w_proj_b': out['w_proj_b'], 'w_proj_c': out['w_proj_c'], 'w_out': out['w_out'], 'post_norm_g': out['post_norm_g'], 'loss_target': out['loss_target'], 'm_pre_norm_g': out['m_pre_norm_g'], 'm_w_in': out['m_w_in'], 'm_gm_ln_g': out['m_gm_ln_g'], 'm_gm_ln_b': out['m_gm_ln_b'], 'm_gm_ws': out['m_gm_ws'], 'm_gm_bs': out['m_gm_bs'], 'm_mla_q_norm_g': out['m_mla_q_norm_g'], 'm_mla_w_uq': out['m_mla_w_uq'], 'm_mla_kv_norm_g': out['m_mla_kv_norm_g'], 'm_mla_w_ukv': out['m_mla_w_ukv'], 'm_lru_conv_w': out['m_lru_conv_w'], 'm_lru_conv_b': out['m_lru_conv_b'], 'm_lru_w_a': out['m_lru_w_a'], 'm_lru_b_a': out['m_lru_b_a'], 'm_lru_w_x': out['m_lru_w_x'], 'm_lru_b_x': out['m_lru_b_x'], 'm_lru_lambda': out['m_lru_lambda'], 'm_w_proj_a': out['m_w_proj_a'], 'm_w_proj_b': out['m_w_proj_b'], 'm_w_proj_c': out['m_w_proj_c'], 'm_w_out': out['m_w_out'], 'm_post_norm_g': out['m_post_norm_g'], 'v_pre_norm_g': out['v_pre_norm_g'], 'v_w_in': out['v_w_in'], 'v_gm_ln_g': out['v_gm_ln_g'], 'v_gm_ln_b': out['v_gm_ln_b'], 'v_gm_ws': out['v_gm_ws'], 'v_gm_bs': out['v_gm_bs'], 'v_mla_q_norm_g': out['v_mla_q_norm_g'], 'v_mla_w_uq': out['v_mla_w_uq'], 'v_mla_kv_norm_g': out['v_mla_kv_norm_g'], 'v_mla_w_ukv': out['v_mla_w_ukv'], 'v_lru_conv_w': out['v_lru_conv_w'], 'v_lru_conv_b': out['v_lru_conv_b'], 'v_lru_w_a': out['v_lru_w_a'], 'v_lru_b_a': out['v_lru_b_a'], 'v_lru_w_x': out['v_lru_w_x'], 'v_lru_b_x': out['v_lru_b_x'], 'v_lru_lambda': out['v_lru_lambda'], 'v_w_proj_a': out['v_w_proj_a'], 'v_w_proj_b': out['v_w_proj_b'], 'v_w_proj_c': out['v_w_proj_c'], 'v_w_out': out['v_w_out'], 'v_post_norm_g': out['v_post_norm_g']}


def _loss(weights, diff, rest, loss_target):
    with _jax.named_scope("forward"):
        args = {**rest, TWIN_DIFF_INPUT: diff, **{k: w.astype(_WEIGHT_DTYPES[k]) for k, w in weights.items()}}
        y = _forward(args)
    with _jax.named_scope("loss_head"):
        err = _jnp.square(y.astype(_jnp.float32) - loss_target)
        return 0.5 * _jnp.sum(_jnp.mean(err, axis=-1)) if err.ndim else 0.5 * err


def _adamw(w, g, m, v):
    m = ADAM_B1 * m + (1.0 - ADAM_B1) * g
    v = ADAM_B2 * v + (1.0 - ADAM_B2) * _jnp.square(g)
    m_hat = m / (1.0 - ADAM_B1 ** ADAM_STEP)
    v_hat = v / (1.0 - ADAM_B2 ** ADAM_STEP)
    delta = -ADAM_LR * (m_hat / (_jnp.sqrt(v_hat) + ADAM_EPS) + ADAM_WD * w)
    return delta, m, v


def reference(x, pre_norm_g, w_in, gm_ln_g, gm_ln_b, gm_ws, gm_bs, mla_q_norm_g, mla_w_uq, mla_kv_norm_g, mla_w_ukv, lru_conv_w, lru_conv_b, lru_w_a, lru_b_a, lru_w_x, lru_b_x, lru_lambda, w_proj_a, w_proj_b, w_proj_c, w_out, post_norm_g, loss_target, m_pre_norm_g, m_w_in, m_gm_ln_g, m_gm_ln_b, m_gm_ws, m_gm_bs, m_mla_q_norm_g, m_mla_w_uq, m_mla_kv_norm_g, m_mla_w_ukv, m_lru_conv_w, m_lru_conv_b, m_lru_w_a, m_lru_b_a, m_lru_w_x, m_lru_b_x, m_lru_lambda, m_w_proj_a, m_w_proj_b, m_w_proj_c, m_w_out, m_post_norm_g, v_pre_norm_g, v_w_in, v_gm_ln_g, v_gm_ln_b, v_gm_ws, v_gm_bs, v_mla_q_norm_g, v_mla_w_uq, v_mla_kv_norm_g, v_mla_w_ukv, v_lru_conv_w, v_lru_conv_b, v_lru_w_a, v_lru_b_a, v_lru_w_x, v_lru_b_x, v_lru_lambda, v_w_proj_a, v_w_proj_b, v_w_proj_c, v_w_out, v_post_norm_g):
    given = dict(x=x, pre_norm_g=pre_norm_g, w_in=w_in, gm_ln_g=gm_ln_g, gm_ln_b=gm_ln_b, gm_ws=gm_ws, gm_bs=gm_bs, mla_q_norm_g=mla_q_norm_g, mla_w_uq=mla_w_uq, mla_kv_norm_g=mla_kv_norm_g, mla_w_ukv=mla_w_ukv, lru_conv_w=lru_conv_w, lru_conv_b=lru_conv_b, lru_w_a=lru_w_a, lru_b_a=lru_b_a, lru_w_x=lru_w_x, lru_b_x=lru_b_x, lru_lambda=lru_lambda, w_proj_a=w_proj_a, w_proj_b=w_proj_b, w_proj_c=w_proj_c, w_out=w_out, post_norm_g=post_norm_g, loss_target=loss_target, m_pre_norm_g=m_pre_norm_g, m_w_in=m_w_in, m_gm_ln_g=m_gm_ln_g, m_gm_ln_b=m_gm_ln_b, m_gm_ws=m_gm_ws, m_gm_bs=m_gm_bs, m_mla_q_norm_g=m_mla_q_norm_g, m_mla_w_uq=m_mla_w_uq, m_mla_kv_norm_g=m_mla_kv_norm_g, m_mla_w_ukv=m_mla_w_ukv, m_lru_conv_w=m_lru_conv_w, m_lru_conv_b=m_lru_conv_b, m_lru_w_a=m_lru_w_a, m_lru_b_a=m_lru_b_a, m_lru_w_x=m_lru_w_x, m_lru_b_x=m_lru_b_x, m_lru_lambda=m_lru_lambda, m_w_proj_a=m_w_proj_a, m_w_proj_b=m_w_proj_b, m_w_proj_c=m_w_proj_c, m_w_out=m_w_out, m_post_norm_g=m_post_norm_g, v_pre_norm_g=v_pre_norm_g, v_w_in=v_w_in, v_gm_ln_g=v_gm_ln_g, v_gm_ln_b=v_gm_ln_b, v_gm_ws=v_gm_ws, v_gm_bs=v_gm_bs, v_mla_q_norm_g=v_mla_q_norm_g, v_mla_w_uq=v_mla_w_uq, v_mla_kv_norm_g=v_mla_kv_norm_g, v_mla_w_ukv=v_mla_w_ukv, v_lru_conv_w=v_lru_conv_w, v_lru_conv_b=v_lru_conv_b, v_lru_w_a=v_lru_w_a, v_lru_b_a=v_lru_b_a, v_lru_w_x=v_lru_w_x, v_lru_b_x=v_lru_b_x, v_lru_lambda=v_lru_lambda, v_w_proj_a=v_w_proj_a, v_w_proj_b=v_w_proj_b, v_w_proj_c=v_w_proj_c, v_w_out=v_w_out, v_post_norm_g=v_post_norm_g)
    weights = {n: given[n] for n in TWIN_WEIGHTS}
    shared = {n: given[n] for n in SHARED_INPUTS}
    per_example = {n: given[n] for n in ['x']}
    grad_fn = _jax.value_and_grad(_loss, argnums=(0, 1))

    def one_microbatch(ex, loss_target):
        ex = dict(ex)
        diff = ex.pop(TWIN_DIFF_INPUT)
        return grad_fn(weights, diff, {**shared, **ex}, loss_target)

    if N_MICROBATCH == 1:
        loss, (grad_w, grad_x) = one_microbatch(per_example, given["loss_target"])
    else:
        def body(carry, xs):
            loss_sum, grad_sum = carry
            l_k, (gw_k, gx_k) = one_microbatch(xs[0], xs[1])
            with _jax.named_scope("update"):
                return (loss_sum + l_k, _jax.tree.map(_jnp.add, grad_sum, gw_k)), gx_k

        init = (_jnp.zeros((), _jnp.float32), _jax.tree.map(_jnp.zeros_like, weights))
        (loss, grad_w), grad_x = _jax.lax.scan(body, init, (per_example, given["loss_target"]))
    with _jax.named_scope("update"):
        delta_w, new_m, new_v = {}, {}, {}
        for n in TWIN_WEIGHTS:
            delta_w[n], new_m[n], new_v[n] = _adamw(weights[n], grad_w[n], given["m_" + n], given["v_" + n])
    return (loss, grad_x, *[grad_w[n] for n in TWIN_WEIGHTS], *[delta_w[n] for n in TWIN_WEIGHTS],
            *[new_m[n] for n in TWIN_WEIGHTS], *[new_v[n] for n in TWIN_WEIGHTS])
```

```python
import functools
import math

import jax
import jax.numpy as jnp
from jax import lax
from jax.experimental import pallas as pl
from jax.experimental.pallas import tpu as pltpu

F32 = jnp.float32
BF16 = jnp.bfloat16

T = 2048
D = 1024
L = 2
NDEV = 8
EPS = 1e-6
CHUNK_SHIFT = 6
HEADS = 8
QK = 192
LRU_W = 1280
LRU_TILE = 640
N_IN = 10432
SHARD = N_IN // NDEV
OFF_U, OFF_V, OFF_ZA, OFF_CQ, OFF_CKV, OFF_ZB = 0, 1024, 2048, 3072, 3456, 3840
OFF_XC, OFF_ZC, OFF_GA, OFF_GB, OFF_GC = 5120, 6400, 7680, 8704, 9728
NPAD = 10752
PAD1_AT, PAD1 = 3776, 64
PAD2_AT, PAD2 = 4800, 256
WIN_PIECES = ((0, 888), (888, 280), (1168, 136))
VMEM_LIMIT = 60 * 1024 * 1024

ADAM_LR, ADAM_B1, ADAM_B2, ADAM_EPS, ADAM_WD, ADAM_STEP = 0.001, 0.9, 0.999, 1e-08, 0.01, 10

_NN = (((1,), (0,)), ((), ()))
_NT = (((1,), (1,)), ((), ()))
_TN = (((0,), (0,)), ((), ()))


def _dg(a, b, dims):
    return lax.dot_general(a.astype(BF16), b.astype(BF16), dims, preferred_element_type=F32)


@jax.custom_vjp
def dot_nn(a, b):
    return _dg(a, b, _NN)


def _nn_fwd(a, b):
    return _dg(a, b, _NN), (a, b)


def _nn_bwd(res, g):
    a, b = res
    return _dg(g, b, _NT).astype(a.dtype), _dg(a, g, _TN).astype(b.dtype)


dot_nn.defvjp(_nn_fwd, _nn_bwd)


@jax.custom_vjp
def dot_nt(a, b):
    return _dg(a, b, _NT)


def _nt_fwd(a, b):
    return _dg(a, b, _NT), (a, b)


def _nt_bwd(res, g):
    a, b = res
    return _dg(g, b, _NN).astype(a.dtype), _dg(g, a, _TN).astype(b.dtype)


dot_nt.defvjp(_nt_fwd, _nt_bwd)


def _params(sem=None):
    return pltpu.CompilerParams(dimension_semantics=sem, vmem_limit_bytes=VMEM_LIMIT)


def _sigmoid(x):
    return 1.0 / (1.0 + jnp.exp(-x))


def _silu(x):
    return x * _sigmoid(x)


def _rms(x, g):
    ms = jnp.mean(x * x, axis=-1, keepdims=True)
    return x * lax.rsqrt(ms + EPS) * g


def _acc(ref, val, first):
    @pl.when(first)
    def _():
        ref[...] = val

    @pl.when(jnp.logical_not(first))
    def _():
        ref[...] += val


def inproj_fwd(x, g, wt, l):
    tn = 256

    def body(x_ref, g_ref, w_ref, proj_ref, h_ref):
        @pl.when(pl.program_id(0) == 0)
        def _():
            h_ref[...] = _rms(x_ref[...], g_ref[...]).astype(BF16)

        proj_ref[...] = lax.dot_general(h_ref[...], w_ref[...].astype(BF16), _NT, preferred_element_type=F32)

    return pl.pallas_call(
        body, grid=(NPAD // tn,),
        in_specs=[pl.BlockSpec((T, D), lambda j: (0, 0)), pl.BlockSpec((1, D), lambda j: (0, 0)),
                  pl.BlockSpec((None, tn, D), lambda j: (l, j, 0))],
        out_specs=[pl.BlockSpec((T, tn), lambda j: (0, j)), pl.BlockSpec((T, D), lambda j: (0, 0))],
        out_shape=[jax.ShapeDtypeStruct((T, NPAD), F32), jax.ShapeDtypeStruct((T, D), BF16)],
        name=f"inproj_fwd_l{l}", compiler_params=_params(("arbitrary",)))(x, g, wt)


def inproj_bwd(dproj, h, wt, l, gbuf=None):
    tn = 256

    def body(*refs):
        dp_ref, h_ref, w_ref = refs[:3]
        dwt_ref, dh_ref = refs[-2:]
        dp = dp_ref[...]
        dwt_ref[...] = lax.dot_general(dp, h_ref[...], _TN, preferred_element_type=F32)
        contrib = lax.dot_general(dp, w_ref[...].astype(BF16), _NN, preferred_element_type=F32)
        _acc(dh_ref, contrib, pl.program_id(0) == 0)

    in_specs = [pl.BlockSpec((T, tn), lambda j: (0, j)), pl.BlockSpec((T, D), lambda j: (0, 0)),
                pl.BlockSpec((None, tn, D), lambda j: (l, j, 0))]
    args = [dproj, h, wt]
    aliases = {}
    if gbuf is not None:
        in_specs.append(pl.BlockSpec(memory_space=pl.ANY))
        args.append(gbuf)
        aliases = {3: 0}
    return pl.pallas_call(
        body, grid=(NPAD // tn,), in_specs=in_specs,
        out_specs=[pl.BlockSpec((None, tn, D), lambda j: (l, j, 0)), pl.BlockSpec((T, D), lambda j: (0, 0))],
        out_shape=[jax.ShapeDtypeStruct((L, NPAD, D), F32), jax.ShapeDtypeStruct((T, D), F32)],
        input_output_aliases=aliases,
        name=f"inproj_bwd_l{l}", compiler_params=_params(("arbitrary",)))(*args)


def prenorm_bwd(x, g, dh, dxn, l):
    tm = 256

    def body(x_ref, g_ref, dh_ref, dxn_ref, dx_ref, dg_ref):
        _, vjp = jax.vjp(_rms, x_ref[...], g_ref[...])
        dx, dg = vjp(dh_ref[...])
        dx_ref[...] = dx + dxn_ref[...]
        _acc(dg_ref, dg, pl.program_id(0) == 0)

    tok = pl.BlockSpec((tm, D), lambda i: (i, 0))
    vec = pl.BlockSpec((1, D), lambda i: (0, 0))
    return pl.pallas_call(
        body, grid=(T // tm,), in_specs=[tok, vec, tok, tok], out_specs=[tok, vec],
        out_shape=[jax.ShapeDtypeStruct((T, D), F32), jax.ShapeDtypeStruct((1, D), F32)],
        name=f"prenorm_bwd_l{l}", compiler_params=_params(("arbitrary",)))(x, g, dh, dxn)


def _gmlp_tile(u, v, z, ln_g, ln_b, ws, bs):
    mu = jnp.mean(v, axis=-1, keepdims=True)
    vc = v - mu
    var = jnp.mean(vc * vc, axis=-1, keepdims=True)
    vn = vc * lax.rsqrt(var + EPS) * ln_g + ln_b
    qi = lax.broadcasted_iota(jnp.int32, (128, 128), 0) >> CHUNK_SHIFT
    kj = lax.broadcasted_iota(jnp.int32, (128, 128), 1) >> CHUNK_SHIFT
    mask = kj <= qi
    outs = []
    for g in range(4):
        wm = jnp.where(mask, ws[g], 0.0)
        outs.append(dot_nn(wm, vn[:, 256 * g:256 * (g + 1)]) + bs[g])
    sv = jnp.concatenate(outs, axis=1)
    return u * sv * _silu(z)


def _gmlp_specs():
    blk = lambda c: pl.BlockSpec((128, 1024), lambda n, c=c: (n, c))
    vec = pl.BlockSpec((1, 1024), lambda n: (0, 0))
    return [blk(0), blk(1), blk(2), vec, vec,
            pl.BlockSpec((4, 128, 128), lambda n: (0, 0, 0)), pl.BlockSpec((4, 128, 1), lambda n: (0, 0, 0))]


def gmlp_fwd(proj, ln_g, ln_b, ws, bs, l):
    def body(u_ref, v_ref, z_ref, g_ref, b_ref, ws_ref, bs_ref, y_ref):
        y_ref[...] = _gmlp_tile(u_ref[...], v_ref[...], z_ref[...], g_ref[...], b_ref[...],
                                [ws_ref[g] for g in range(4)], [bs_ref[g] for g in range(4)])

    return pl.pallas_call(
        body, grid=(T // 128,), in_specs=_gmlp_specs(),
        out_specs=pl.BlockSpec((128, 1024), lambda n: (n, 0)),
        out_shape=jax.ShapeDtypeStruct((T, 1024), F32),
        name=f"gmlp_fwd_l{l}", compiler_params=_params(("arbitrary",)))(proj, proj, proj, ln_g, ln_b, ws, bs)


def gmlp_bwd(proj, ln_g, ln_b, ws, bs, dy, l):
    def body(u_ref, v_ref, z_ref, g_ref, b_ref, ws_ref, bs_ref, dy_ref, dseg_ref, dg_ref, db_ref, dws_ref, dbs_ref):
        first = pl.program_id(0) == 0
        _, vjp = jax.vjp(_gmlp_tile, u_ref[...], v_ref[...], z_ref[...], g_ref[...], b_ref[...],
                         [ws_ref[g] for g in range(4)], [bs_ref[g] for g in range(4)])
        du, dv, dz, dg, db, dws, dbs = vjp(dy_ref[...])
        dseg_ref[:, 0:1024] = du.astype(BF16)
        dseg_ref[:, 1024:2048] = dv.astype(BF16)
        dseg_ref[:, 2048:3072] = dz.astype(BF16)
        _acc(dg_ref, dg, first)
        _acc(db_ref, db, first)
        for g in range(4):
            _acc(dws_ref.at[g], dws[g], first)
            _acc(dbs_ref.at[g], dbs[g], first)

    vec = pl.BlockSpec((1, 1024), lambda n: (0, 0))
    return pl.pallas_call(
        body, grid=(T // 128,), in_specs=_gmlp_specs() + [pl.BlockSpec((128, 1024), lambda n: (n, 0))],
        out_specs=[pl.BlockSpec((128, 3072), lambda n: (n, 0)), vec, vec,
                   pl.BlockSpec((4, 128, 128), lambda n: (0, 0, 0)), pl.BlockSpec((4, 128, 1), lambda n: (0, 0, 0))],
        out_shape=[jax.ShapeDtypeStruct((T, 3072), BF16), jax.ShapeDtypeStruct((1, 1024), F32),
                   jax.ShapeDtypeStruct((1, 1024), F32), jax.ShapeDtypeStruct((4, 128, 128), F32),
                   jax.ShapeDtypeStruct((4, 128, 1), F32)],
        name=f"gmlp_bwd_l{l}", compiler_params=_params(("arbitrary",)))(proj, proj, proj, ln_g, ln_b, ws, bs, dy)


QKV_TM = 256


def _qkv_tile(cq, ckvr, qg, kvg, wq, wkv, ctab, stab):
    tm = cq.shape[0]
    cqn = _rms(cq, qg)
    lane = lax.broadcasted_iota(jnp.int32, ckvr.shape, 1)
    iskv = lane < 256
    ms = jnp.sum(jnp.where(iskv, ckvr * ckvr, 0.0), axis=-1, keepdims=True) * (1.0 / 256)
    lm = jnp.where(iskv, ckvr * lax.rsqrt(ms + EPS) * kvg, ckvr)
    r = lax.broadcasted_iota(jnp.int32, (64, 128), 0)
    c = lax.broadcasted_iota(jnp.int32, (64, 128), 1)
    eye = jnp.where(c == r, 1.0, 0.0)
    eye_sw = jnp.where(c == ((r + 32) & 63), 1.0, 0.0)
    z64 = jnp.zeros((64, 256), F32)
    z128 = jnp.zeros((128, 128), F32)
    rk_rope = jnp.concatenate([z64, eye], axis=1)
    rk_sw = jnp.concatenate([jnp.zeros((128, 384), F32), jnp.concatenate([z64, eye_sw], axis=1)], axis=0)
    k_sw = dot_nt(lm, rk_sw) * stab
    qs, ks, vs = [], [], []
    for h in range(HEADS):
        wn, w1, w2 = wq[h]
        wk, wv = wkv[h]
        wq_h = jnp.concatenate([wn, w1, w2], axis=0)
        wq_sw = jnp.concatenate([jnp.zeros((128, 384), F32), w2, w1], axis=0)
        qs.append(dot_nt(cqn, wq_h) * ctab + dot_nt(cqn, wq_sw) * stab)
        rk_h = jnp.concatenate([jnp.concatenate([wk, z128], axis=1), rk_rope], axis=0)
        ks.append(dot_nt(lm, rk_h) * ctab + k_sw)
        vs.append(dot_nt(lm, jnp.concatenate([wv, z128], axis=1)))
    return qs, ks, vs


def _qkv_in_specs():
    tm = QKV_TM
    return [pl.BlockSpec((tm, 384), lambda i: (i, OFF_CQ // 384)), pl.BlockSpec((tm, 384), lambda i: (i, OFF_CKV // 384)),
            pl.BlockSpec((1, 384), lambda i: (0, 0)), pl.BlockSpec((1, 384), lambda i: (0, 0)),
            pl.BlockSpec((HEADS, 192, 384), lambda i: (0, 0, 0)), pl.BlockSpec((HEADS, 256, 256), lambda i: (0, 0, 0)),
            pl.BlockSpec((tm, 192), lambda i: (i, 0)), pl.BlockSpec((tm, 192), lambda i: (i, 0))]


def _qkv_weights(wq_ref, wkv_ref):
    wq = [(wq_ref[h, 0:128, :], wq_ref[h, 128:160, :], wq_ref[h, 160:192, :]) for h in range(HEADS)]
    wkv = [(wkv_ref[h, 0:128, :], wkv_ref[h, 128:256, :]) for h in range(HEADS)]
    return wq, wkv


def qkv_fwd(proj, qg, kvg, wq, wkv, ctab, stab, l):
    tm = QKV_TM

    def body(cq_ref, ckvr_ref, qg_ref, kvg_ref, wq_ref, wkv_ref, c_ref, s_ref, q_ref, k_ref, v_ref):
        wq_l, wkv_l = _qkv_weights(wq_ref, wkv_ref)
        qs, ks, vs = _qkv_tile(cq_ref[...], ckvr_ref[...], qg_ref[...], kvg_ref[...], wq_l, wkv_l, c_ref[...], s_ref[...])
        for h in range(HEADS):
            q_ref[h] = qs[h]
            k_ref[h] = ks[h]
            v_ref[h] = vs[h]

    return pl.pallas_call(
        body, grid=(T // tm,), in_specs=_qkv_in_specs(),
        out_specs=[pl.BlockSpec((HEADS, tm, QK), lambda i: (0, i, 0)), pl.BlockSpec((HEADS, tm, QK), lambda i: (0, i, 0)),
                   pl.BlockSpec((HEADS, tm, 128), lambda i: (0, i, 0))],
        out_shape=[jax.ShapeDtypeStruct((HEADS, T, QK), F32), jax.ShapeDtypeStruct((HEADS, T, QK), F32),
                   jax.ShapeDtypeStruct((HEADS, T, 128), F32)],
        name=f"qkv_fwd_l{l}", compiler_params=_params(("arbitrary",)))(proj, proj, qg, kvg, wq, wkv, ctab, stab)


def qkv_bwd(proj, qg, kvg, wq, wkv, ctab, stab, dq, dk, dv, l):
    tm = QKV_TM

    def body(cq_ref, ckvr_ref, qg_ref, kvg_ref, wq_ref, wkv_ref, c_ref, s_ref, dq_ref, dk_ref, dv_ref,
             dseg_ref, dqg_ref, dkvg_ref, dwq_ref, dwkv_ref):
        first = pl.program_id(0) == 0
        wq_l, wkv_l = _qkv_weights(wq_ref, wkv_ref)
        c_tab, s_tab = c_ref[...], s_ref[...]
        fn = lambda cq, ckvr, qg_, kvg_, wq_, wkv_: _qkv_tile(cq, ckvr, qg_, kvg_, wq_, wkv_, c_tab, s_tab)
        _, vjp = jax.vjp(fn, cq_ref[...], ckvr_ref[...], qg_ref[...], kvg_ref[...], wq_l, wkv_l)
        cts = ([dq_ref[h] for h in range(HEADS)], [dk_ref[h] for h in range(HEADS)], [dv_ref[h] for h in range(HEADS)])
        dcq, dckvr, dqg, dkvg, dwq, dwkv = vjp(cts)
        dseg_ref[:, 0:384] = dcq.astype(BF16)
        dseg_ref[:, 384:768] = dckvr.astype(BF16)
        _acc(dqg_ref, dqg, first)
        _acc(dkvg_ref, dkvg, first)
        for h in range(HEADS):
            _acc(dwq_ref.at[h, 0:128, :], dwq[h][0], first)
            _acc(dwq_ref.at[h, 128:160, :], dwq[h][1], first)
            _acc(dwq_ref.at[h, 160:192, :], dwq[h][2], first)
            _acc(dwkv_ref.at[h, 0:128, :], dwkv[h][0], first)
            _acc(dwkv_ref.at[h, 128:256, :], dwkv[h][1], first)

    hq = pl.BlockSpec((HEADS, tm, QK), lambda i: (0, i, 0))
    return pl.pallas_call(
        body, grid=(T // tm,),
        in_specs=_qkv_in_specs() + [hq, hq, pl.BlockSpec((HEADS, tm, 128), lambda i: (0, i, 0))],
        out_specs=[pl.BlockSpec((tm, 768), lambda i: (i, 0)), pl.BlockSpec((1, 384), lambda i: (0, 0)),
                   pl.BlockSpec((1, 384), lambda i: (0, 0)), pl.BlockSpec((HEADS, 192, 384), lambda i: (0, 0, 0)),
                   pl.BlockSpec((HEADS, 256, 256), lambda i: (0, 0, 0))],
        out_shape=[jax.ShapeDtypeStruct((T, 768), BF16), jax.ShapeDtypeStruct((1, 384), F32),
                   jax.ShapeDtypeStruct((1, 384), F32), jax.ShapeDtypeStruct((HEADS, 192, 384), F32),
                   jax.ShapeDtypeStruct((HEADS, 256, 256), F32)],
        name=f"qkv_bwd_l{l}", compiler_params=_params(("arbitrary",)))(
            proj, proj, qg, kvg, wq, wkv, ctab, stab, dq, dk, dv)


ATT_TQ = 256


def _attn_tile(q, k, v, zb, q0):
    s = dot_nt(q, k) * (1.0 / math.sqrt(QK))
    qc = (q0 + lax.broadcasted_iota(jnp.int32, s.shape, 0)) >> CHUNK_SHIFT
    kc = lax.broadcasted_iota(jnp.int32, s.shape, 1) >> CHUNK_SHIFT
    s = jnp.where(kc <= qc, s, -1e30)
    m = lax.stop_gradient(jnp.max(s, axis=-1, keepdims=True))
    p = jnp.exp(s - m)
    p = p / jnp.sum(p, axis=-1, keepdims=True)
    return dot_nn(p, v) * _silu(zb)


def _attn_in_specs():
    tq = ATT_TQ
    return [pl.BlockSpec((None, tq, QK), lambda h, i: (h, i, 0)), pl.BlockSpec((None, T, QK), lambda h, i: (h, 0, 0)),
            pl.BlockSpec((None, T, 128), lambda h, i: (h, 0, 0)),
            pl.BlockSpec((tq, 128), lambda h, i: (i, OFF_ZB // 128 + h))]


def attn_fwd(q, k, v, proj, l):
    tq = ATT_TQ

    def body(q_ref, k_ref, v_ref, z_ref, y_ref):
        y_ref[...] = _attn_tile(q_ref[...], k_ref[...], v_ref[...], z_ref[...], pl.program_id(1) * tq)

    return pl.pallas_call(
        body, grid=(HEADS, T // tq), in_specs=_attn_in_specs(),
        out_specs=pl.BlockSpec((tq, 128), lambda h, i: (i, h)),
        out_shape=jax.ShapeDtypeStruct((T, 1024), F32),
        name=f"attn_fwd_l{l}", compiler_params=_params(("arbitrary", "arbitrary")))(q, k, v, proj)


def attn_bwd(q, k, v, proj, dy, l):
    tq = ATT_TQ

    def body(q_ref, k_ref, v_ref, z_ref, dy_ref, dq_ref, dk_ref, dv_ref, dz_ref):
        q0 = pl.program_id(1) * tq
        fn = lambda q_, k_, v_, z_: _attn_tile(q_, k_, v_, z_, q0)
        _, vjp = jax.vjp(fn, q_ref[...], k_ref[...], v_ref[...], z_ref[...])
        dq, dk, dv, dz = vjp(dy_ref[...])
        dq_ref[...] = dq
        dz_ref[...] = dz.astype(BF16)
        first = pl.program_id(1) == 0
        _acc(dk_ref, dk, first)
        _acc(dv_ref, dv, first)

    return pl.pallas_call(
        body, grid=(HEADS, T // tq),
        in_specs=_attn_in_specs() + [pl.BlockSpec((tq, 128), lambda h, i: (i, h))],
        out_specs=[pl.BlockSpec((None, tq, QK), lambda h, i: (h, i, 0)), pl.BlockSpec((None, T, QK), lambda h, i: (h, 0, 0)),
                   pl.BlockSpec((None, T, 128), lambda h, i: (h, 0, 0)), pl.BlockSpec((tq, 128), lambda h, i: (i, h))],
        out_shape=[jax.ShapeDtypeStruct((HEADS, T, QK), F32), jax.ShapeDtypeStruct((HEADS, T, QK), F32),
                   jax.ShapeDtypeStruct((HEADS, T, 128), F32), jax.ShapeDtypeStruct((T, 1024), BF16)],
        name=f"attn_bwd_l{l}", compiler_params=_params(("arbitrary", "arbitrary")))(q, k, v, proj, dy)


LRU_TT = 256


def _lru_gates(xc, wa, wx, ba, bx, lam):
    r = _sigmoid(dot_nn(xc, wa) + ba)
    i = _sigmoid(dot_nn(xc, wx) + bx)
    sp = jnp.maximum(-lam, 0.0) + jnp.log1p(jnp.exp(-jnp.abs(lam)))
    log_a = -8.0 * r * sp
    a = jnp.exp(log_a)
    mult = jnp.sqrt(jnp.maximum(1.0 - jnp.exp(2.0 * log_a), 0.0))
    return a, mult * (i * xc)


def _shift_down(x, s, halo):
    xs = pltpu.roll(x, s, 0)
    row = lax.broadcasted_iota(jnp.int32, halo.shape, 0)
    top = jnp.where(row < s, pltpu.roll(halo, s, 0), xs[0:8])
    return jnp.concatenate([top, xs[8:]], axis=0)


def _shift_up(x, s, halo):
    n = x.shape[0]
    xs = pltpu.roll(x, n - s, 0)
    row = lax.broadcasted_iota(jnp.int32, halo.shape, 0)
    bot = jnp.where(row >= 8 - s, pltpu.roll(halo, 8 - s, 0), xs[n - 8:n])
    return jnp.concatenate([xs[:n - 8], bot], axis=0)


def _conv(x, halo, w_ref, b):
    return (w_ref[3:4, :] * x + w_ref[2:3, :] * _shift_down(x, 1, halo) + w_ref[1:2, :] * _shift_down(x, 2, halo)
            + w_ref[0:1, :] * _shift_down(x, 3, halo) + b)


def _scan(a, b, reverse):
    n = a.shape[0]
    row = lax.broadcasted_iota(jnp.int32, a.shape, 0)
    d = 1
    while d < n:
        if reverse:
            keep = row < n - d
            a_sh = jnp.where(keep, pltpu.roll(a, n - d, 0), 1.0)
            b_sh = jnp.where(keep, pltpu.roll(b, n - d, 0), 0.0)
        else:
            keep = row >= d
            a_sh = jnp.where(keep, pltpu.roll(a, d, 0), 1.0)
            b_sh = jnp.where(keep, pltpu.roll(b, d, 0), 0.0)
        b = a * b_sh + b
        a = a * a_sh
        d *= 2
    return a, b


def _lru_param_specs(time_map):
    ct = LRU_TILE
    vec = pl.BlockSpec((1, ct), lambda n, i: (0, n))
    return [pl.BlockSpec((4, ct), lambda n, i: (0, n)), vec,
            pl.BlockSpec((None, ct, ct), lambda n, i: (n, 0, 0)), pl.BlockSpec((None, ct, ct), lambda n, i: (n, 0, 0)),
            vec, vec, vec]


def lru_fwd(proj, conv_w, conv_b, wa, wx, ba, bx, lam, l):
    tt, ct = LRU_TT, LRU_TILE

    def body(x_ref, z_ref, cw_ref, cb_ref, wa_ref, wx_ref, ba_ref, bx_ref, lam_ref, h_ref, y_ref, halo, hcar):
        @pl.when(pl.program_id(1) == 0)
        def _():
            halo[...] = jnp.zeros_like(halo)
            hcar[...] = jnp.zeros_like(hcar)

        x = x_ref[...]
        xc = _conv(x, halo[...], cw_ref, cb_ref[...])
        halo[...] = x[tt - 8:tt]
        a, b = _lru_gates(xc, wa_ref[...], wx_ref[...], ba_ref[...], bx_ref[...], lam_ref[...])
        a_cum, b_cum = _scan(a, b, False)
        h = a_cum * hcar[...] + b_cum
        h_ref[...] = h
        hcar[...] = h_ref[tt - 1:tt, :]
        y_ref[...] = h * _silu(z_ref[...])

    seq = pl.BlockSpec((tt, ct), lambda n, i: (i, n))
    return pl.pallas_call(
        body, grid=(LRU_W // ct, T // tt),
        in_specs=[pl.BlockSpec((tt, ct), lambda n, i: (i, OFF_XC // ct + n)),
                  pl.BlockSpec((tt, ct), lambda n, i: (i, OFF_ZC // ct + n))] + _lru_param_specs(None),
        out_specs=[seq, seq],
        out_shape=[jax.ShapeDtypeStruct((T, LRU_W), F32), jax.ShapeDtypeStruct((T, LRU_W), F32)],
        scratch_shapes=[pltpu.VMEM((8, ct), F32), pltpu.VMEM((1, ct), F32)],
        name=f"lru_fwd_l{l}", compiler_params=_params(("arbitrary", "arbitrary")))(
            proj, proj, conv_w, conv_b, wa, wx, ba, bx, lam)


def lru_bwd(proj, hseq, dy, conv_w, conv_b, wa, wx, ba, bx, lam, l):
    tt, ct = LRU_TT, LRU_TILE
    nt = T // tt
    rev = lambda i: nt - 1 - i
    prev8 = lambda i: jnp.maximum(rev(i) * (tt // 8) - 1, 0)

    def body(x_ref, xh_ref, z_ref, h_ref, hh_ref, dy_ref, cw_ref, cb_ref, wa_ref, wx_ref, ba_ref, bx_ref, lam_ref,
             dx_ref, dz_ref, dcw_ref, dcb_ref, dwa_ref, dwx_ref, dba_ref, dbx_ref, dlam_ref, gcar, dhalo):
        i = pl.program_id(1)
        first = i == 0

        @pl.when(first)
        def _():
            gcar[...] = jnp.zeros_like(gcar)
            dhalo[...] = jnp.zeros_like(dhalo)

        at_start = rev(i) == 0
        x = x_ref[...]
        xhalo = jnp.where(at_start, 0.0, xh_ref[...])
        sh = [x, _shift_down(x, 1, xhalo), _shift_down(x, 2, xhalo), _shift_down(x, 3, xhalo)]
        xc = (cw_ref[3:4, :] * sh[0] + cw_ref[2:3, :] * sh[1] + cw_ref[1:2, :] * sh[2] + cw_ref[0:1, :] * sh[3]
              + cb_ref[...])
        (a, b), vjp = jax.vjp(_lru_gates, xc, wa_ref[...], wx_ref[...], ba_ref[...], bx_ref[...], lam_ref[...])
        hs = h_ref[...]
        hprev = _shift_down(hs, 1, jnp.where(at_start, 0.0, hh_ref[...]))
        z = z_ref[...]
        sg = _sigmoid(z)
        dy = dy_ref[...]
        dz_ref[...] = (dy * hs * (sg * (1.0 + z * (1.0 - sg)))).astype(BF16)
        dh = dy * (z * sg)
        row = lax.broadcasted_iota(jnp.int32, a.shape, 0)
        a_next = jnp.where(row < tt - 1, pltpu.roll(a, tt - 1, 0), 1.0)
        a_cum, b_cum = _scan(a_next, dh, True)
        g = a_cum * gcar[...] + b_cum
        dxc, dwa, dwx, dba, dbx, dlam = vjp((g * hprev, g))
        dx = (cw_ref[3:4, :] * dxc + cw_ref[2:3, :] * _shift_up(dxc, 1, dhalo[...])
              + cw_ref[1:2, :] * _shift_up(dxc, 2, dhalo[...]) + cw_ref[0:1, :] * _shift_up(dxc, 3, dhalo[...]))
        dx_ref[...] = dx.astype(BF16)
        dhalo[...] = dxc[0:8]
        ag = a * g
        gcar[...] = ag[0:1]
        dcw = jnp.concatenate([jnp.sum(dxc * sh[3 - j], axis=0, keepdims=True) for j in range(4)], axis=0)
        _acc(dcw_ref, dcw, first)
        _acc(dcb_ref, jnp.sum(dxc, axis=0, keepdims=True), first)
        _acc(dwa_ref, dwa, first)
        _acc(dwx_ref, dwx, first)
        _acc(dba_ref, dba, first)
        _acc(dbx_ref, dbx, first)
        _acc(dlam_ref, dlam, first)

    xcol = OFF_XC // ct
    zcol = OFF_ZC // ct
    vec = pl.BlockSpec((1, ct), lambda n, i: (0, n))
    mat = pl.BlockSpec((None, ct, ct), lambda n, i: (n, 0, 0))
    seq = pl.BlockSpec((tt, ct), lambda n, i: (rev(i), n))
    return pl.pallas_call(
        body, grid=(LRU_W // ct, nt),
        in_specs=[pl.BlockSpec((tt, ct), lambda n, i: (rev(i), xcol + n)),
                  pl.BlockSpec((8, ct), lambda n, i: (prev8(i), xcol + n)),
                  pl.BlockSpec((tt, ct), lambda n, i: (rev(i), zcol + n)),
                  seq, pl.BlockSpec((8, ct), lambda n, i: (prev8(i), n)), seq] + _lru_param_specs(None),
        out_specs=[seq, seq, pl.BlockSpec((4, ct), lambda n, i: (0, n)), vec, mat, mat, vec, vec, vec],
        out_shape=[jax.ShapeDtypeStruct((T, LRU_W), BF16), jax.ShapeDtypeStruct((T, LRU_W), BF16),
                   jax.ShapeDtypeStruct((4, LRU_W), F32), jax.ShapeDtypeStruct((1, LRU_W), F32),
                   jax.ShapeDtypeStruct((2, ct, ct), F32), jax.ShapeDtypeStruct((2, ct, ct), F32),
                   jax.ShapeDtypeStruct((1, LRU_W), F32), jax.ShapeDtypeStruct((1, LRU_W), F32),
                   jax.ShapeDtypeStruct((1, LRU_W), F32)],
        scratch_shapes=[pltpu.VMEM((1, ct), F32), pltpu.VMEM((8, ct), F32)],
        name=f"lru_bwd_l{l}", compiler_params=_params(("arbitrary", "arbitrary")))(
            proj, proj, proj, hseq, hseq, dy, conv_w, conv_b, wa, wx, ba, bx, lam)


def proj_fwd(y, w, l, tag):
    tm = 512
    k = y.shape[1]

    def body(y_ref, w_ref, o_ref):
        o_ref[...] = _dg(y_ref[...], w_ref[...], _NN)

    return pl.pallas_call(
        body, grid=(T // tm,),
        in_specs=[pl.BlockSpec((tm, k), lambda i: (i, 0)), pl.BlockSpec((None, k, D), lambda i: (l, 0, 0))],
        out_specs=pl.BlockSpec((tm, D), lambda i: (i, 0)), out_shape=jax.ShapeDtypeStruct((T, D), F32),
        name=f"proj_{tag}_fwd_l{l}", compiler_params=_params(("arbitrary",)))(y, w)


def proj_bwd(y, dp, w, l, tag):
    tm = 512
    k = y.shape[1]

    def body(y_ref, dp_ref, w_ref, dy_ref, dw_ref):
        dp = dp_ref[...]
        dy_ref[...] = _dg(dp, w_ref[...], _NT)
        _acc(dw_ref, _dg(y_ref[...], dp, _TN), pl.program_id(0) == 0)

    return pl.pallas_call(
        body, grid=(T // tm,),
        in_specs=[pl.BlockSpec((tm, k), lambda i: (i, 0)), pl.BlockSpec((tm, D), lambda i: (i, 0)),
                  pl.BlockSpec((None, k, D), lambda i: (l, 0, 0))],
        out_specs=[pl.BlockSpec((tm, k), lambda i: (i, 0)), pl.BlockSpec((k, D), lambda i: (0, 0))],
        out_shape=[jax.ShapeDtypeStruct((T, k), F32), jax.ShapeDtypeStruct((k, D), F32)],
        name=f"proj_{tag}_bwd_l{l}", compiler_params=_params(("arbitrary",)))(y, dp, w)


OUT_TM = 256


def _out_tile(pa, pb, pc, ga, gb, gc, wout, post_g):
    merged = _sigmoid(ga) * pa + _sigmoid(gb) * pb + _sigmoid(gc) * pc
    return _rms(dot_nn(merged, wout), post_g)


def _out_in_specs(l):
    tm = OUT_TM
    tok = pl.BlockSpec((tm, D), lambda i: (i, 0))
    gate = lambda off: pl.BlockSpec((tm, 512), lambda i, off=off: (i, off // 512))
    return [tok, tok, tok, gate(OFF_GA), gate(OFF_GA + 512), gate(OFF_GB), gate(OFF_GB + 512), gate(OFF_GC),
            gate(OFF_GC + 512), pl.BlockSpec((None, D, D), lambda i: (l, 0, 0)), pl.BlockSpec((1, D), lambda i: (0, 0))]


def _gates(refs):
    return [jnp.concatenate([refs[2 * j][...], refs[2 * j + 1][...]], axis=1) for j in range(3)]


def out_fwd(x, pa, pb, pc, proj, wout, post_g, l):
    tm = OUT_TM

    def body(pa_ref, pb_ref, pc_ref, g0, g1, g2, g3, g4, g5, w_ref, pg_ref, x_ref, o_ref):
        ga, gb, gc = _gates([g0, g1, g2, g3, g4, g5])
        o_ref[...] = x_ref[...] + _out_tile(pa_ref[...], pb_ref[...], pc_ref[...], ga, gb, gc, w_ref[...], pg_ref[...])

    tok = pl.BlockSpec((tm, D), lambda i: (i, 0))
    return pl.pallas_call(
        body, grid=(T // tm,), in_specs=_out_in_specs(l) + [tok], out_specs=tok,
        out_shape=jax.ShapeDtypeStruct((T, D), F32),
        name=f"out_fwd_l{l}", compiler_params=_params(("arbitrary",)))(
            pa, pb, pc, proj, proj, proj, proj, proj, proj, wout, post_g, x)


def out_bwd(pa, pb, pc, proj, wout, post_g, dxn, l):
    tm = OUT_TM

    def body(pa_ref, pb_ref, pc_ref, g0, g1, g2, g3, g4, g5, w_ref, pg_ref, dxn_ref,
             dpa_ref, dpb_ref, dpc_ref, dg_ref, dw_ref, dpg_ref):
        first = pl.program_id(0) == 0
        ga, gb, gc = _gates([g0, g1, g2, g3, g4, g5])
        _, vjp = jax.vjp(_out_tile, pa_ref[...], pb_ref[...], pc_ref[...], ga, gb, gc, w_ref[...], pg_ref[...])
        dpa, dpb, dpc, dga, dgb, dgc, dw, dpg = vjp(dxn_ref[...])
        dpa_ref[...] = dpa.astype(BF16)
        dpb_ref[...] = dpb.astype(BF16)
        dpc_ref[...] = dpc.astype(BF16)
        dg_ref[:, 0:1024] = dga.astype(BF16)
        dg_ref[:, 1024:2048] = dgb.astype(BF16)
        dg_ref[:, 2048:3072] = dgc.astype(BF16)
        _acc(dw_ref, dw, first)
        _acc(dpg_ref, dpg, first)

    tok = pl.BlockSpec((tm, D), lambda i: (i, 0))
    return pl.pallas_call(
        body, grid=(T // tm,), in_specs=_out_in_specs(l) + [tok],
        out_specs=[tok, tok, tok, pl.BlockSpec((tm, 3072), lambda i: (i, 0)), pl.BlockSpec((D, D), lambda i: (0, 0)),
                   pl.BlockSpec((1, D), lambda i: (0, 0))],
        out_shape=[jax.ShapeDtypeStruct((T, D), BF16)] * 3 + [jax.ShapeDtypeStruct((T, 3072), BF16),
                                                            jax.ShapeDtypeStruct((D, D), F32), jax.ShapeDtypeStruct((1, D), F32)],
        name=f"out_bwd_l{l}", compiler_params=_params(("arbitrary",)))(
            pa, pb, pc, proj, proj, proj, proj, proj, proj, wout, post_g, dxn)


def loss_head(y, target):
    tm = 256

    def body(y_ref, t_ref, loss_ref, dy_ref):
        e = y_ref[...] - t_ref[...]
        dy_ref[...] = e * (1.0 / D)
        val = 0.5 * jnp.sum(jnp.mean(e * e, axis=-1, keepdims=True), axis=0, keepdims=True)
        _acc(loss_ref, jnp.broadcast_to(val, (8, 128)), pl.program_id(0) == 0)

    tok = pl.BlockSpec((tm, D), lambda i: (i, 0))
    total, dy = pl.pallas_call(
        body, grid=(T // tm,), in_specs=[tok, tok],
        out_specs=[pl.BlockSpec((8, 128), lambda i: (0, 0)), tok],
        out_shape=[jax.ShapeDtypeStruct((8, 128), F32), jax.ShapeDtypeStruct((T, D), F32)],
        name="loss_head", compiler_params=_params(("arbitrary",)))(y, target)
    return total[0, 0], dy


def _rope_tables():
    pos = jnp.arange(T, dtype=F32)
    inv_freq = 10000.0 ** (-jnp.arange(0, 64, 2, dtype=F32) / 64)
    ang = pos[:, None] * inv_freq[None, :]
    cos, sin = jnp.cos(ang), jnp.sin(ang)
    ctab = jnp.concatenate([jnp.ones((T, 128), F32), cos, cos], axis=1)
    stab = jnp.concatenate([jnp.zeros((T, 128), F32), -sin, sin], axis=1)
    return ctab, stab


def _block_diag(w):
    w5 = w.reshape(L, 2, 8, 80, 80)
    eye = jnp.eye(8, dtype=w.dtype)
    return jnp.einsum("lnbij,bc->lnbicj", w5, eye).reshape(L, 2, LRU_TILE, LRU_TILE)


def _block_diag_t(dw):
    dw5 = dw.reshape(2, 8, 80, 8, 80)
    return jnp.einsum("nbicj,bc->nbij", dw5, jnp.eye(8, dtype=dw.dtype)).reshape(16, 80, 80)


def _layer_fwd(x, l, w, tabs):
    row = lambda a: a[l][None]
    proj, h = inproj_fwd(x, row(w["pre_norm_g"]), w["w_in_t"], l)
    ya = gmlp_fwd(proj, row(w["gm_ln_g"]), row(w["gm_ln_b"]), w["gm_ws"][l], w["gm_bs"][l][..., None], l)
    q, k, v = qkv_fwd(proj, row(w["mla_q_norm_g"]), row(w["kv_g384"]), w["wq"][l], w["wkv"][l], tabs[0], tabs[1], l)
    yb = attn_fwd(q, k, v, proj, l)
    hseq, yc = lru_fwd(proj, w["lru_conv_w"][l], row(w["lru_conv_b"]), w["wa_dense"][l], w["wx_dense"][l],
                       row(w["lru_b_a"]), row(w["lru_b_x"]), row(w["lru_lambda"]), l)
    pa = proj_fwd(ya, w["w_proj_a"], l, "a")
    pb = proj_fwd(yb, w["w_proj_b"], l, "b")
    pc = proj_fwd(yc, w["w_proj_c"], l, "c")
    xn = out_fwd(x, pa, pb, pc, proj, w["w_out"], row(w["post_norm_g"]), l)
    return xn, (x, proj, h, ya, q, k, v, yb, hseq, yc, pa, pb, pc)


def _layer_bwd(dxn, l, w, tabs, saved, gbuf):
    x, proj, h, ya, q, k, v, yb, hseq, yc, pa, pb, pc = saved
    row = lambda a: a[l][None]
    g = {}
    dpa, dpb, dpc, dgates, g["w_out"], dpost = out_bwd(pa, pb, pc, proj, w["w_out"], row(w["post_norm_g"]), dxn, l)
    g["post_norm_g"] = dpost[0]
    dya, g["w_proj_a"] = proj_bwd(ya, dpa, w["w_proj_a"], l, "a")
    dyb, g["w_proj_b"] = proj_bwd(yb, dpb, w["w_proj_b"], l, "b")
    dyc, g["w_proj_c"] = proj_bwd(yc, dpc, w["w_proj_c"], l, "c")
    dseg_a, dln_g, dln_b, g["gm_ws"], dbs = gmlp_bwd(proj, row(w["gm_ln_g"]), row(w["gm_ln_b"]), w["gm_ws"][l],
                                                    w["gm_bs"][l][..., None], dya, l)
    g["gm_ln_g"], g["gm_ln_b"], g["gm_bs"] = dln_g[0], dln_b[0], dbs[..., 0]
    dq, dk, dv, dzb = attn_bwd(q, k, v, proj, dyb, l)
    dseg_q, dqg, dkvg, g["wq"], g["wkv"] = qkv_bwd(proj, row(w["mla_q_norm_g"]), row(w["kv_g384"]), w["wq"][l],
                                                   w["wkv"][l], tabs[0], tabs[1], dq, dk, dv, l)
    g["mla_q_norm_g"], g["mla_kv_norm_g"] = dqg[0], dkvg[0, :256]
    dxc, dzc, g["lru_conv_w"], dcb, dwa, dwx, dba, dbx, dlam = lru_bwd(
        proj, hseq, dyc, w["lru_conv_w"][l], row(w["lru_conv_b"]), w["wa_dense"][l], w["wx_dense"][l],
        row(w["lru_b_a"]), row(w["lru_b_x"]), row(w["lru_lambda"]), l)
    g["lru_conv_b"], g["lru_b_a"], g["lru_b_x"], g["lru_lambda"] = dcb[0], dba[0], dbx[0], dlam[0]
    g["lru_w_a"], g["lru_w_x"] = _block_diag_t(dwa), _block_diag_t(dwx)
    dproj = jnp.concatenate([dseg_a, dseg_q, dzb, jnp.zeros((T, PAD2), dzb.dtype), dxc, dzc, dgates], axis=1)
    gbuf, dh = inproj_bwd(dproj, h, w["w_in_t"], l, gbuf)
    dx, dpre = prenorm_bwd(x, row(w["pre_norm_g"]), dh, dxn, l)
    g["pre_norm_g"] = dpre[0]
    return dx, gbuf, g


def _local_step(x, target, w):
    tabs = _rope_tables()
    saved = []
    for l in range(L):
        x, s = _layer_fwd(x, l, w, tabs)
        saved.append(s)
    loss, dx = loss_head(x, target)
    gbuf = None
    grads = [None] * L
    for l in reversed(range(L)):
        dx, gbuf, grads[l] = _layer_bwd(dx, l, w, tabs, saved[l], gbuf)
    return loss, dx, gbuf, grads


MESH = pl.DeviceIdType.MESH
ANY = pl.BlockSpec(memory_space=pl.ANY)
FLIPS = ((1, 0), (0, 1), (1, 1))


def _win_off(k, s):
    g = SHARD * k + s
    return g + jnp.where(g >= PAD1_AT, PAD1, 0) + jnp.where(g >= PAD2_AT, PAD2, 0)


def _plain_off(rows):
    return lambda k, s: rows * k + s


class Spec:
    def __init__(self, rows, cols, full_rows, pieces=None, off=None, layers=L):
        self.rows, self.cols, self.full_rows, self.layers = rows, cols, full_rows, layers
        self.pieces = pieces or ((0, rows),)
        self.off = off or _plain_off(rows)


WEIGHT_SPECS = {
    "w_in_t": Spec(SHARD, D, NPAD, WIN_PIECES, _win_off),
    "wq": Spec(192, 384, 1536),
    "wkv": Spec(256, 256, 2048),
    "conv": Spec(160, 128, 1280),
    "w_proj_a": Spec(128, D, 1024),
    "w_proj_b": Spec(128, D, 1024),
    "w_proj_c": Spec(160, D, 1280),
    "w_out": Spec(128, D, 1024),
}
REP_ROWS = 72
REP_SPEC = Spec(REP_ROWS, D, REP_ROWS * NDEV, layers=1)


def _coords():
    return lax.axis_index("x"), lax.axis_index("y"), lax.axis_index("c")


def _rows(ref, start, n):
    if not isinstance(start, int):
        start = pl.multiple_of(start, 8)
    return ref.at[:, pl.ds(start, n), :]


def _col_tile(cols):
    return 256 if cols % 256 == 0 else cols


def _n_pieces(specs):
    return sum(len(sp.pieces) for sp in specs)


def gather_send(shards, specs, tag):
    ns, npc = len(specs), _n_pieces(specs)
    has_gaps = [sp.off is _win_off for sp in specs]
    zeros = jnp.zeros((L, PAD2, D), F32)

    def body(*refs):
        srcs, zsrc, fulls = refs[:ns], refs[ns], refs[ns + 1:2 * ns + 1]
        lsem, zsem, ssem, rsem = refs[2 * ns + 1:]
        x, y, c = _coords()
        me = 4 * x + 2 * y + c
        targets = [(x, y, 1 - c)] + [(x ^ fx, y ^ fy, c) for fx, fy in FLIPS]
        target_k = [4 * tx + 2 * ty + tc for tx, ty, tc in targets]
        waits = []
        p = 0
        for src, full, sp, gaps in zip(srcs, fulls, specs, has_gaps):
            if gaps:
                for gi, (at, n) in enumerate(((PAD1_AT, PAD1), (PAD2_AT + PAD1, PAD2))):
                    zc = pltpu.make_async_copy(zsrc.at[:, pl.ds(0, n), :], full.at[:, pl.ds(at, n), :], zsem.at[gi])
                    zc.start()
                    waits.append(zc.wait)
            for s, n in sp.pieces:
                src_rows = _rows(src, s, n)
                mine = _rows(full, sp.off(me, s), n)
                lc = pltpu.make_async_copy(src_rows, mine, lsem.at[p])
                lc.start()
                waits.append(lc.wait)
                for t, tgt in enumerate(targets):
                    send = pltpu.make_async_remote_copy(src_rows, mine, ssem.at[t, p], rsem.at[t, p],
                                                        device_id=tgt, device_id_type=MESH)
                    send.start()
                    waits.append(send.wait_send)
                    theirs = _rows(full, sp.off(target_k[t], s), n)
                    waits.append(pltpu.make_async_remote_copy(src_rows, theirs, ssem.at[t, p], rsem.at[t, p],
                                                              device_id=tgt, device_id_type=MESH).wait_recv)
                p += 1
        for w in waits:
            w()

    return pl.pallas_call(
        body, in_specs=[ANY] * (ns + 1), out_specs=[ANY] * ns,
        out_shape=[jax.ShapeDtypeStruct((sp.layers, sp.full_rows, sp.cols), F32) for sp in specs],
        scratch_shapes=[pltpu.SemaphoreType.DMA((npc,)), pltpu.SemaphoreType.DMA((2,)),
                        pltpu.SemaphoreType.DMA((4, npc)), pltpu.SemaphoreType.DMA((4, npc))],
        name=f"gather_send_{tag}", compiler_params=pltpu.CompilerParams(has_side_effects=True))(*shards, zeros)


def gather_forward(fulls, specs, tag):
    ns, npc = len(specs), _n_pieces(specs)

    def body(*refs):
        bufs = refs[ns:2 * ns]
        ssem, rsem = refs[2 * ns:]
        x, y, c = _coords()
        sibling = (x, y, 1 - c)
        waits = []
        p = 0
        for buf, sp in zip(bufs, specs):
            for s, n in sp.pieces:
                for t, (fx, fy) in enumerate(FLIPS):
                    chip = 4 * (x ^ fx) + 2 * (y ^ fy)
                    here = _rows(buf, sp.off(chip + c, s), n)
                    send = pltpu.make_async_remote_copy(here, here, ssem.at[t, p], rsem.at[t, p],
                                                        device_id=sibling, device_id_type=MESH)
                    send.start()
                    waits.append(send.wait_send)
                    there = _rows(buf, sp.off(chip + 1 - c, s), n)
                    waits.append(pltpu.make_async_remote_copy(here, there, ssem.at[t, p], rsem.at[t, p],
                                                              device_id=sibling, device_id_type=MESH).wait_recv)
                p += 1
        for w in waits:
            w()

    return pl.pallas_call(
        body, in_specs=[ANY] * ns, out_specs=[ANY] * ns,
        out_shape=[jax.ShapeDtypeStruct(f.shape, f.dtype) for f in fulls],
        input_output_aliases={i: i for i in range(ns)},
        scratch_shapes=[pltpu.SemaphoreType.DMA((3, npc)), pltpu.SemaphoreType.DMA((3, npc))],
        name=f"gather_forward_{tag}", compiler_params=pltpu.CompilerParams(has_side_effects=True))(*fulls)


def all_gather(shards, specs, tag):
    return gather_forward(gather_send(shards, specs, tag), specs, tag)


def reduce_pair(grads, specs, tag):
    ns, npc = len(specs), _n_pieces(specs)

    def body(*refs):
        srcs, mines, theirs = refs[:ns], refs[ns:2 * ns], refs[2 * ns:3 * ns]
        lsem, ssem, rsem = refs[3 * ns:]
        x, y, c = _coords()
        sibling = (x, y, 1 - c)
        waits = []
        p = 0
        for src, mine, their, sp in zip(srcs, mines, theirs, specs):
            for s, n in sp.pieces:
                for j in range(4):
                    lc = pltpu.make_async_copy(_rows(src, sp.off(2 * j + c, s), n), _rows(mine.at[j], s, n), lsem.at[j, p])
                    lc.start()
                    waits.append(lc.wait)
                    send = pltpu.make_async_remote_copy(_rows(src, sp.off(2 * j + 1 - c, s), n), _rows(their.at[j], s, n),
                                                        ssem.at[j, p], rsem.at[j, p], device_id=sibling, device_id_type=MESH)
                    send.start()
                    waits.append(send.wait)
                p += 1
        for w in waits:
            w()

    shapes = [jax.ShapeDtypeStruct((4, sp.layers, sp.rows, sp.cols), F32) for sp in specs]
    outs = pl.pallas_call(
        body, in_specs=[ANY] * ns, out_specs=[ANY] * (2 * ns), out_shape=shapes + shapes,
        scratch_shapes=[pltpu.SemaphoreType.DMA((4, npc)), pltpu.SemaphoreType.DMA((4, npc)),
                        pltpu.SemaphoreType.DMA((4, npc))],
        name=f"reduce_pair_{tag}", compiler_params=pltpu.CompilerParams(has_side_effects=True))(*grads)
    return outs[:ns], outs[ns:]


def add_pair(a, b, tag):
    _, layers, rows, cols = a.shape

    def body(a_ref, b_ref, o_ref):
        o_ref[...] = a_ref[...] + b_ref[...]

    tc = _col_tile(cols)
    blk = pl.BlockSpec((None, None, rows, tc), lambda j, l, n: (j, l, 0, n))
    return pl.pallas_call(
        body, grid=(4, layers, cols // tc), in_specs=[blk, blk], out_specs=blk,
        out_shape=jax.ShapeDtypeStruct(a.shape, F32),
        name=f"add_pair_{tag}", compiler_params=_params(("arbitrary",) * 3))(a, b)


def reduce_chips(parts, specs, tag):
    ns = len(specs)

    def body(*refs):
        srcs, dsts = refs[:ns], refs[ns:2 * ns]
        lsem, ssem, rsem = refs[2 * ns:]
        x, y, c = _coords()
        waits = []
        for i, (src, dst) in enumerate(zip(srcs, dsts)):
            lc = pltpu.make_async_copy(src.at[2 * x + y], dst.at[3], lsem.at[i])
            lc.start()
            waits.append(lc.wait)
            for t, (fx, fy) in enumerate(FLIPS):
                tx, ty = x ^ fx, y ^ fy
                send = pltpu.make_async_remote_copy(src.at[2 * tx + ty], dst.at[t], ssem.at[t, i], rsem.at[t, i],
                                                    device_id=(tx, ty, c), device_id_type=MESH)
                send.start()
                waits.append(send.wait)
        for w in waits:
            w()

    return pl.pallas_call(
        body, in_specs=[ANY] * ns, out_specs=[ANY] * ns,
        out_shape=[jax.ShapeDtypeStruct(p.shape, F32) for p in parts],
        scratch_shapes=[pltpu.SemaphoreType.DMA((ns,)), pltpu.SemaphoreType.DMA((3, ns)), pltpu.SemaphoreType.DMA((3, ns))],
        name=f"reduce_chips_{tag}", compiler_params=pltpu.CompilerParams(has_side_effects=True))(*parts)


def sum_slots(r, tag):
    _, layers, rows, cols = r.shape

    def body(r_ref, o_ref):
        o_ref[...] = ((r_ref[3] + r_ref[0]) + r_ref[1]) + r_ref[2]

    tc = _col_tile(cols)
    return pl.pallas_call(
        body, grid=(layers, cols // tc), in_specs=[pl.BlockSpec((4, None, rows, tc), lambda l, n: (0, l, 0, n))],
        out_specs=pl.BlockSpec((None, rows, tc), lambda l, n: (l, 0, n)),
        out_shape=jax.ShapeDtypeStruct((layers, rows, cols), F32),
        name=f"sum_slots_{tag}", compiler_params=_params(("arbitrary",) * 2))(r)


def reduce_scatter(grads, specs, names, tag):
    mine, theirs = reduce_pair(grads, specs, tag)
    parts = [add_pair(a, b, f"{tag}_{n}") for a, b, n in zip(mine, theirs, names)]
    slots = reduce_chips(parts, specs, tag)
    return [sum_slots(r, f"{tag}_{n}") for r, n in zip(slots, names)]


def adamw(w, g, m, v, name):
    shape = w.shape
    cols = shape[-1]
    rows = math.prod(shape[:-1])
    tr = rows
    while tr * cols * 4 > (1 << 20) and tr % 16 == 0:
        tr //= 2
    c1 = 1.0 - ADAM_B1 ** ADAM_STEP
    c2 = 1.0 - ADAM_B2 ** ADAM_STEP

    def body(w_ref, g_ref, m_ref, v_ref, d_ref, nm_ref, nv_ref):
        gv = g_ref[...]
        m2 = ADAM_B1 * m_ref[...] + (1.0 - ADAM_B1) * gv
        v2 = ADAM_B2 * v_ref[...] + (1.0 - ADAM_B2) * (gv * gv)
        nm_ref[...] = m2
        nv_ref[...] = v2
        d_ref[...] = -ADAM_LR * ((m2 / c1) / (jnp.sqrt(v2 / c2) + ADAM_EPS) + ADAM_WD * w_ref[...])

    blk = pl.BlockSpec((tr, cols), lambda i: (i, 0))
    outs = pl.pallas_call(
        body, grid=(rows // tr,), in_specs=[blk] * 4, out_specs=[blk] * 3,
        out_shape=[jax.ShapeDtypeStruct((rows, cols), F32)] * 3,
        name=f"adamw_{name}", compiler_params=_params(("arbitrary",)))(
            *[a.reshape(rows, cols) for a in (w, g, m, v)])
    return [o.reshape(shape) for o in outs]


WEIGHTS = ("pre_norm_g", "w_in", "gm_ln_g", "gm_ln_b", "gm_ws", "gm_bs", "mla_q_norm_g", "mla_w_uq", "mla_kv_norm_g",
           "mla_w_ukv", "lru_conv_w", "lru_conv_b", "lru_w_a", "lru_b_a", "lru_w_x", "lru_b_x", "lru_lambda",
           "w_proj_a", "w_proj_b", "w_proj_c", "w_out", "post_norm_g")
SHARDED = ("w_in", "mla_w_uq", "mla_w_ukv", "lru_conv_w", "w_proj_a", "w_proj_b", "w_proj_c", "w_out")
REPLICATED = tuple(n for n in WEIGHTS if n not in SHARDED)


def _step(x, target, wts, ms, vs):
    t12 = lambda a: jnp.swapaxes(a, 1, 2)
    names = list(WEIGHT_SPECS)
    specs = [WEIGHT_SPECS[n] for n in names]
    conv_t = jnp.pad(t12(wts["lru_conv_w"]), ((0, 0), (0, 0), (0, 124)))
    shards = [t12(wts["w_in"]), t12(wts["mla_w_uq"]), t12(wts["mla_w_ukv"]), conv_t,
              wts["w_proj_a"], wts["w_proj_b"], wts["w_proj_c"], wts["w_out"]]
    full = dict(zip(names, all_gather(shards, specs, "w")))

    w = {n: wts[n] for n in REPLICATED}
    w["w_in_t"] = full["w_in_t"]
    w["wq"] = full["wq"].reshape(L, HEADS, 192, 384)
    w["wkv"] = full["wkv"].reshape(L, HEADS, 256, 256)
    w["lru_conv_w"] = t12(full["conv"][:, :, :4])
    for n in ("w_proj_a", "w_proj_b", "w_proj_c", "w_out"):
        w[n] = full[n]
    w["kv_g384"] = jnp.concatenate([wts["mla_kv_norm_g"], jnp.ones((L, 128), F32)], axis=1)
    w["wa_dense"] = _block_diag(wts["lru_w_a"])
    w["wx_dense"] = _block_diag(wts["lru_w_x"])

    loss, dx, gbuf, grads = _local_step(x, target, w)
    stack = lambda n: jnp.stack([grads[l][n] for l in range(L)])

    local = [gbuf, stack("wq").reshape(L, 1536, 384), stack("wkv").reshape(L, 2048, 256),
             jnp.pad(t12(stack("lru_conv_w")), ((0, 0), (0, 0), (0, 124))),
             stack("w_proj_a"), stack("w_proj_b"), stack("w_proj_c"), stack("w_out")]
    rep_flat = jnp.concatenate([stack(n).reshape(-1) for n in REPLICATED])
    rep_flat = jnp.pad(rep_flat, (0, REP_ROWS * NDEV * D - rep_flat.shape[0])).reshape(1, REP_ROWS * NDEV, D)
    summed = reduce_scatter(local + [rep_flat], specs + [REP_SPEC], names + ["rep"], "g")
    s = dict(zip(names, summed[:-1]))
    rep_full = all_gather([summed[-1]], [REP_SPEC], "rep")[0].reshape(-1)

    g = {"w_in": t12(s["w_in_t"]), "mla_w_uq": t12(s["wq"]), "mla_w_ukv": t12(s["wkv"]),
         "lru_conv_w": t12(s["conv"][:, :, :4])}
    for n in ("w_proj_a", "w_proj_b", "w_proj_c", "w_out"):
        g[n] = s[n]
    at = 0
    for n in REPLICATED:
        size = math.prod(wts[n].shape)
        g[n] = rep_full[at:at + size].reshape(wts[n].shape)
        at += size

    loss = lax.psum(loss, ("x", "y", "c"))
    upd = {n: adamw(wts[n], g[n], ms[n], vs[n], n) for n in WEIGHTS}
    return (loss, dx[None], *[g[n] for n in WEIGHTS], *[upd[n][0] for n in WEIGHTS],
            *[upd[n][1] for n in WEIGHTS], *[upd[n][2] for n in WEIGHTS])


def kernel(x, pre_norm_g, w_in, gm_ln_g, gm_ln_b, gm_ws, gm_bs, mla_q_norm_g, mla_w_uq, mla_kv_norm_g, mla_w_ukv, lru_conv_w, lru_conv_b, lru_w_a, lru_b_a, lru_w_x, lru_b_x, lru_lambda, w_proj_a, w_proj_b, w_proj_c, w_out, post_norm_g, loss_target, m_pre_norm_g, m_w_in, m_gm_ln_g, m_gm_ln_b, m_gm_ws, m_gm_bs, m_mla_q_norm_g, m_mla_w_uq, m_mla_kv_norm_g, m_mla_w_ukv, m_lru_conv_w, m_lru_conv_b, m_lru_w_a, m_lru_b_a, m_lru_w_x, m_lru_b_x, m_lru_lambda, m_w_proj_a, m_w_proj_b, m_w_proj_c, m_w_out, m_post_norm_g, v_pre_norm_g, v_w_in, v_gm_ln_g, v_gm_ln_b, v_gm_ws, v_gm_bs, v_mla_q_norm_g, v_mla_w_uq, v_mla_kv_norm_g, v_mla_w_ukv, v_lru_conv_w, v_lru_conv_b, v_lru_w_a, v_lru_b_a, v_lru_w_x, v_lru_b_x, v_lru_lambda, v_w_proj_a, v_w_proj_b, v_w_proj_c, v_w_out, v_post_norm_g):
    wts = dict(zip(WEIGHTS, (pre_norm_g, w_in, gm_ln_g, gm_ln_b, gm_ws, gm_bs, mla_q_norm_g, mla_w_uq, mla_kv_norm_g,
                             mla_w_ukv, lru_conv_w, lru_conv_b, lru_w_a, lru_b_a, lru_w_x, lru_b_x, lru_lambda,
                             w_proj_a, w_proj_b, w_proj_c, w_out, post_norm_g)))
    ms = dict(zip(WEIGHTS, (m_pre_norm_g, m_w_in, m_gm_ln_g, m_gm_ln_b, m_gm_ws, m_gm_bs, m_mla_q_norm_g, m_mla_w_uq,
                            m_mla_kv_norm_g, m_mla_w_ukv, m_lru_conv_w, m_lru_conv_b, m_lru_w_a, m_lru_b_a, m_lru_w_x,
                            m_lru_b_x, m_lru_lambda, m_w_proj_a, m_w_proj_b, m_w_proj_c, m_w_out, m_post_norm_g)))
    vs = dict(zip(WEIGHTS, (v_pre_norm_g, v_w_in, v_gm_ln_g, v_gm_ln_b, v_gm_ws, v_gm_bs, v_mla_q_norm_g, v_mla_w_uq,
                            v_mla_kv_norm_g, v_mla_w_ukv, v_lru_conv_w, v_lru_conv_b, v_lru_w_a, v_lru_b_a, v_lru_w_x,
                            v_lru_b_x, v_lru_lambda, v_w_proj_a, v_w_proj_b, v_w_proj_c, v_w_out, v_post_norm_g)))
    return _step(x[0], loss_target[0], wts, ms, vs)
```

```python
import functools
import math

import jax
import jax.numpy as jnp
from jax import lax
from jax.experimental import pallas as pl
from jax.experimental.pallas import tpu as pltpu

F32 = jnp.float32
BF16 = jnp.bfloat16

T = 2048
D = 1024
L = 2
NDEV = 8
EPS = 1e-6
CHUNK_SHIFT = 6
HEADS = 8
QK = 192
LRU_W = 1280
LRU_TILE = 640
N_IN = 10432
SHARD = N_IN // NDEV
OFF_U, OFF_V, OFF_ZA, OFF_CQ, OFF_CKV, OFF_ZB = 0, 1024, 2048, 3072, 3456, 3840
OFF_XC, OFF_ZC, OFF_GA, OFF_GB, OFF_GC = 5120, 6400, 7680, 8704, 9728
NPAD = 10752
PAD1_AT, PAD1 = 3776, 64
PAD2_AT, PAD2 = 4800, 256
WIN_PIECES = ((0, 888), (888, 280), (1168, 136))
VMEM_LIMIT = 60 * 1024 * 1024

ADAM_LR, ADAM_B1, ADAM_B2, ADAM_EPS, ADAM_WD, ADAM_STEP = 0.001, 0.9, 0.999, 1e-08, 0.01, 10

_NN = (((1,), (0,)), ((), ()))
_NT = (((1,), (1,)), ((), ()))
_TN = (((0,), (0,)), ((), ()))


def _dg(a, b, dims):
    return lax.dot_general(a.astype(BF16), b.astype(BF16), dims, preferred_element_type=F32)


@jax.custom_vjp
def dot_nn(a, b):
    return _dg(a, b, _NN)


def _nn_fwd(a, b):
    return _dg(a, b, _NN), (a, b)


def _nn_bwd(res, g):
    a, b = res
    return _dg(g, b, _NT).astype(a.dtype), _dg(a, g, _TN).astype(b.dtype)


dot_nn.defvjp(_nn_fwd, _nn_bwd)


@jax.custom_vjp
def dot_nt(a, b):
    return _dg(a, b, _NT)


def _nt_fwd(a, b):
    return _dg(a, b, _NT), (a, b)


def _nt_bwd(res, g):
    a, b = res
    return _dg(g, b, _NN).astype(a.dtype), _dg(g, a, _TN).astype(b.dtype)


dot_nt.defvjp(_nt_fwd, _nt_bwd)


def _params(sem=None):
    return pltpu.CompilerParams(dimension_semantics=sem, vmem_limit_bytes=VMEM_LIMIT)


def _sigmoid(x):
    return 1.0 / (1.0 + jnp.exp(-x))


def _silu(x):
    return x * _sigmoid(x)


def _rms(x, g):
    ms = jnp.mean(x * x, axis=-1, keepdims=True)
    return x * lax.rsqrt(ms + EPS) * g


def _acc(ref, val, first):
    @pl.when(first)
    def _():
        ref[...] = val

    @pl.when(jnp.logical_not(first))
    def _():
        ref[...] += val


def inproj_fwd(x, g, wt, l):
    tn = 256

    def body(x_ref, g_ref, w_ref, proj_ref, h_ref):
        @pl.when(pl.program_id(0) == 0)
        def _():
            h_ref[...] = _rms(x_ref[...], g_ref[...]).astype(BF16)

        proj_ref[...] = lax.dot_general(h_ref[...], w_ref[...].astype(BF16), _NT, preferred_element_type=F32)

    return pl.pallas_call(
        body, grid=(NPAD // tn,),
        in_specs=[pl.BlockSpec((T, D), lambda j: (0, 0)), pl.BlockSpec((1, D), lambda j: (0, 0)),
                  pl.BlockSpec((None, tn, D), lambda j: (l, j, 0))],
        out_specs=[pl.BlockSpec((T, tn), lambda j: (0, j)), pl.BlockSpec((T, D), lambda j: (0, 0))],
        out_shape=[jax.ShapeDtypeStruct((T, NPAD), F32), jax.ShapeDtypeStruct((T, D), BF16)],
        name=f"inproj_fwd_l{l}", compiler_params=_params(("arbitrary",)))(x, g, wt)


def inproj_bwd(dproj, h, wt, l, gbuf=None):
    tn = 256

    def body(*refs):
        dp_ref, h_ref, w_ref = refs[:3]
        dwt_ref, dh_ref = refs[-2:]
        dp = dp_ref[...]
        dwt_ref[...] = lax.dot_general(dp, h_ref[...], _TN, preferred_element_type=F32)
        contrib = lax.dot_general(dp, w_ref[...].astype(BF16), _NN, preferred_element_type=F32)
        _acc(dh_ref, contrib, pl.program_id(0) == 0)

    in_specs = [pl.BlockSpec((T, tn), lambda j: (0, j)), pl.BlockSpec((T, D), lambda j: (0, 0)),
                pl.BlockSpec((None, tn, D), lambda j: (l, j, 0))]
    args = [dproj, h, wt]
    aliases = {}
    if gbuf is not None:
        in_specs.append(pl.BlockSpec(memory_space=pl.ANY))
        args.append(gbuf)
        aliases = {3: 0}
    return pl.pallas_call(
        body, grid=(NPAD // tn,), in_specs=in_specs,
        out_specs=[pl.BlockSpec((None, tn, D), lambda j: (l, j, 0)), pl.BlockSpec((T, D), lambda j: (0, 0))],
        out_shape=[jax.ShapeDtypeStruct((L, NPAD, D), F32), jax.ShapeDtypeStruct((T, D), F32)],
        input_output_aliases=aliases,
        name=f"inproj_bwd_l{l}", compiler_params=_params(("arbitrary",)))(*args)


def prenorm_bwd(x, g, dh, dxn, l):
    tm = 256

    def body(x_ref, g_ref, dh_ref, dxn_ref, dx_ref, dg_ref):
        _, vjp = jax.vjp(_rms, x_ref[...], g_ref[...])
        dx, dg = vjp(dh_ref[...])
        dx_ref[...] = dx + dxn_ref[...]
        _acc(dg_ref, dg, pl.program_id(0) == 0)

    tok = pl.BlockSpec((tm, D), lambda i: (i, 0))
    vec = pl.BlockSpec((1, D), lambda i: (0, 0))
    return pl.pallas_call(
        body, grid=(T // tm,), in_specs=[tok, vec, tok, tok], out_specs=[tok, vec],
        out_shape=[jax.ShapeDtypeStruct((T, D), F32), jax.ShapeDtypeStruct((1, D), F32)],
        name=f"prenorm_bwd_l{l}", compiler_params=_params(("arbitrary",)))(x, g, dh, dxn)


def _gmlp_tile(u, v, z, ln_g, ln_b, ws, bs):
    mu = jnp.mean(v, axis=-1, keepdims=True)
    vc = v - mu
    var = jnp.mean(vc * vc, axis=-1, keepdims=True)
    vn = vc * lax.rsqrt(var + EPS) * ln_g + ln_b
    qi = lax.broadcasted_iota(jnp.int32, (128, 128), 0) >> CHUNK_SHIFT
    kj = lax.broadcasted_iota(jnp.int32, (128, 128), 1) >> CHUNK_SHIFT
    mask = kj <= qi
    outs = []
    for g in range(4):
        wm = jnp.where(mask, ws[g], 0.0)
        outs.append(dot_nn(wm, vn[:, 256 * g:256 * (g + 1)]) + bs[g])
    sv = jnp.concatenate(outs, axis=1)
    return u * sv * _silu(z)


def _gmlp_specs():
    blk = lambda c: pl.BlockSpec((128, 1024), lambda n, c=c: (n, c))
    vec = pl.BlockSpec((1, 1024), lambda n: (0, 0))
    return [blk(0), blk(1), blk(2), vec, vec,
            pl.BlockSpec((4, 128, 128), lambda n: (0, 0, 0)), pl.BlockSpec((4, 128, 1), lambda n: (0, 0, 0))]


def gmlp_fwd(proj, ln_g, ln_b, ws, bs, l):
    def body(u_ref, v_ref, z_ref, g_ref, b_ref, ws_ref, bs_ref, y_ref):
        y_ref[...] = _gmlp_tile(u_ref[...], v_ref[...], z_ref[...], g_ref[...], b_ref[...],
                                [ws_ref[g] for g in range(4)], [bs_ref[g] for g in range(4)])

    return pl.pallas_call(
        body, grid=(T // 128,), in_specs=_gmlp_specs(),
        out_specs=pl.BlockSpec((128, 1024), lambda n: (n, 0)),
        out_shape=jax.ShapeDtypeStruct((T, 1024), F32),
        name=f"gmlp_fwd_l{l}", compiler_params=_params(("arbitrary",)))(proj, proj, proj, ln_g, ln_b, ws, bs)


def gmlp_bwd(proj, ln_g, ln_b, ws, bs, dy, l):
    def body(u_ref, v_ref, z_ref, g_ref, b_ref, ws_ref, bs_ref, dy_ref, dseg_ref, dg_ref, db_ref, dws_ref, dbs_ref):
        first = pl.program_id(0) == 0
        _, vjp = jax.vjp(_gmlp_tile, u_ref[...], v_ref[...], z_ref[...], g_ref[...], b_ref[...],
                         [ws_ref[g] for g in range(4)], [bs_ref[g] for g in range(4)])
        du, dv, dz, dg, db, dws, dbs = vjp(dy_ref[...])
        dseg_ref[:, 0:1024] = du.astype(BF16)
        dseg_ref[:, 1024:2048] = dv.astype(BF16)
        dseg_ref[:, 2048:3072] = dz.astype(BF16)
        _acc(dg_ref, dg, first)
        _acc(db_ref, db, first)
        for g in range(4):
            _acc(dws_ref.at[g], dws[g], first)
            _acc(dbs_ref.at[g], dbs[g], first)

    vec = pl.BlockSpec((1, 1024), lambda n: (0, 0))
    return pl.pallas_call(
        body, grid=(T // 128,), in_specs=_gmlp_specs() + [pl.BlockSpec((128, 1024), lambda n: (n, 0))],
        out_specs=[pl.BlockSpec((128, 3072), lambda n: (n, 0)), vec, vec,
                   pl.BlockSpec((4, 128, 128), lambda n: (0, 0, 0)), pl.BlockSpec((4, 128, 1), lambda n: (0, 0, 0))],
        out_shape=[jax.ShapeDtypeStruct((T, 3072), BF16), jax.ShapeDtypeStruct((1, 1024), F32),
                   jax.ShapeDtypeStruct((1, 1024), F32), jax.ShapeDtypeStruct((4, 128, 128), F32),
                   jax.ShapeDtypeStruct((4, 128, 1), F32)],
        name=f"gmlp_bwd_l{l}", compiler_params=_params(("arbitrary",)))(proj, proj, proj, ln_g, ln_b, ws, bs, dy)


QKV_TM = 256


def _qkv_tile(cq, ckvr, qg, kvg, wq, wkv, ctab, stab):
    tm = cq.shape[0]
    cqn = _rms(cq, qg)
    lane = lax.broadcasted_iota(jnp.int32, ckvr.shape, 1)
    iskv = lane < 256
    ms = jnp.sum(jnp.where(iskv, ckvr * ckvr, 0.0), axis=-1, keepdims=True) * (1.0 / 256)
    lm = jnp.where(iskv, ckvr * lax.rsqrt(ms + EPS) * kvg, ckvr)
    r = lax.broadcasted_iota(jnp.int32, (64, 128), 0)
    c = lax.broadcasted_iota(jnp.int32, (64, 128), 1)
    eye = jnp.where(c == r, 1.0, 0.0)
    eye_sw = jnp.where(c == ((r + 32) & 63), 1.0, 0.0)
    z64 = jnp.zeros((64, 256), F32)
    z128 = jnp.zeros((128, 128), F32)
    rk_rope = jnp.concatenate([z64, eye], axis=1)
    rk_sw = jnp.concatenate([jnp.zeros((128, 384), F32), jnp.concatenate([z64, eye_sw], axis=1)], axis=0)
    k_sw = dot_nt(lm, rk_sw) * stab
    qs, ks, vs = [], [], []
    for h in range(HEADS):
        wn, w1, w2 = wq[h]
        wk, wv = wkv[h]
        wq_h = jnp.concatenate([wn, w1, w2], axis=0)
        wq_sw = jnp.concatenate([jnp.zeros((128, 384), F32), w2, w1], axis=0)
        qs.append(dot_nt(cqn, wq_h) * ctab + dot_nt(cqn, wq_sw) * stab)
        rk_h = jnp.concatenate([jnp.concatenate([wk, z128], axis=1), rk_rope], axis=0)
        ks.append(dot_nt(lm, rk_h) * ctab + k_sw)
        vs.append(dot_nt(lm, jnp.concatenate([wv, z128], axis=1)))
    return qs, ks, vs


def _qkv_in_specs():
    tm = QKV_TM
    return [pl.BlockSpec((tm, 384), lambda i: (i, OFF_CQ // 384)), pl.BlockSpec((tm, 384), lambda i: (i, OFF_CKV // 384)),
            pl.BlockSpec((1, 384), lambda i: (0, 0)), pl.BlockSpec((1, 384), lambda i: (0, 0)),
            pl.BlockSpec((HEADS, 192, 384), lambda i: (0, 0, 0)), pl.BlockSpec((HEADS, 256, 256), lambda i: (0, 0, 0)),
            pl.BlockSpec((tm, 192), lambda i: (i, 0)), pl.BlockSpec((tm, 192), lambda i: (i, 0))]


def _qkv_weights(wq_ref, wkv_ref):
    wq = [(wq_ref[h, 0:128, :], wq_ref[h, 128:160, :], wq_ref[h, 160:192, :]) for h in range(HEADS)]
    wkv = [(wkv_ref[h, 0:128, :].astype(F32), wkv_ref[h, 128:256, :].astype(F32)) for h in range(HEADS)]
    return wq, wkv


def qkv_fwd(proj, qg, kvg, wq, wkv, ctab, stab, l):
    tm = QKV_TM

    def body(cq_ref, ckvr_ref, qg_ref, kvg_ref, wq_ref, wkv_ref, c_ref, s_ref, q_ref, k_ref, v_ref):
        wq_l, wkv_l = _qkv_weights(wq_ref, wkv_ref)
        qs, ks, vs = _qkv_tile(cq_ref[...], ckvr_ref[...], qg_ref[...], kvg_ref[...], wq_l, wkv_l, c_ref[...], s_ref[...])
        for h in range(HEADS):
            q_ref[h] = qs[h]
            k_ref[h] = ks[h]
            v_ref[h] = vs[h]

    return pl.pallas_call(
        body, grid=(T // tm,), in_specs=_qkv_in_specs(),
        out_specs=[pl.BlockSpec((HEADS, tm, QK), lambda i: (0, i, 0)), pl.BlockSpec((HEADS, tm, QK), lambda i: (0, i, 0)),
                   pl.BlockSpec((HEADS, tm, 128), lambda i: (0, i, 0))],
        out_shape=[jax.ShapeDtypeStruct((HEADS, T, QK), F32), jax.ShapeDtypeStruct((HEADS, T, QK), F32),
                   jax.ShapeDtypeStruct((HEADS, T, 128), F32)],
        name=f"qkv_fwd_l{l}", compiler_params=_params(("arbitrary",)))(proj, proj, qg, kvg, wq, wkv, ctab, stab)


def qkv_bwd(proj, qg, kvg, wq, wkv, ctab, stab, dq, dk, dv, l):
    tm = QKV_TM

    def body(cq_ref, ckvr_ref, qg_ref, kvg_ref, wq_ref, wkv_ref, c_ref, s_ref, dq_ref, dk_ref, dv_ref,
             dseg_ref, dqg_ref, dkvg_ref, dwq_ref, dwkv_ref):
        first = pl.program_id(0) == 0
        wq_l, wkv_l = _qkv_weights(wq_ref, wkv_ref)
        c_tab, s_tab = c_ref[...], s_ref[...]
        fn = lambda cq, ckvr, qg_, kvg_, wq_, wkv_: _qkv_tile(cq, ckvr, qg_, kvg_, wq_, wkv_, c_tab, s_tab)
        _, vjp = jax.vjp(fn, cq_ref[...], ckvr_ref[...], qg_ref[...], kvg_ref[...], wq_l, wkv_l)
        cts = ([dq_ref[h] for h in range(HEADS)], [dk_ref[h] for h in range(HEADS)], [dv_ref[h] for h in range(HEADS)])
        dcq, dckvr, dqg, dkvg, dwq, dwkv = vjp(cts)
        dseg_ref[:, 0:384] = dcq.astype(BF16)
        dseg_ref[:, 384:768] = dckvr.astype(BF16)
        _acc(dqg_ref, dqg, first)
        _acc(dkvg_ref, dkvg, first)
        for h in range(HEADS):
            _acc(dwq_ref.at[h, 0:128, :], dwq[h][0], first)
            _acc(dwq_ref.at[h, 128:160, :], dwq[h][1], first)
            _acc(dwq_ref.at[h, 160:192, :], dwq[h][2], first)
            _acc(dwkv_ref.at[h, 0:128, :], dwkv[h][0], first)
            _acc(dwkv_ref.at[h, 128:256, :], dwkv[h][1], first)

    hq = pl.BlockSpec((HEADS, tm, QK), lambda i: (0, i, 0))
    return pl.pallas_call(
        body, grid=(T // tm,),
        in_specs=_qkv_in_specs() + [hq, hq, pl.BlockSpec((HEADS, tm, 128), lambda i: (0, i, 0))],
        out_specs=[pl.BlockSpec((tm, 768), lambda i: (i, 0)), pl.BlockSpec((1, 384), lambda i: (0, 0)),
                   pl.BlockSpec((1, 384), lambda i: (0, 0)), pl.BlockSpec((HEADS, 192, 384), lambda i: (0, 0, 0)),
                   pl.BlockSpec((HEADS, 256, 256), lambda i: (0, 0, 0))],
        out_shape=[jax.ShapeDtypeStruct((T, 768), BF16), jax.ShapeDtypeStruct((1, 384), F32),
                   jax.ShapeDtypeStruct((1, 384), F32), jax.ShapeDtypeStruct((HEADS, 192, 384), F32),
                   jax.ShapeDtypeStruct((HEADS, 256, 256), F32)],
        name=f"qkv_bwd_l{l}", compiler_params=_params(("arbitrary",)))(
            proj, proj, qg, kvg, wq, wkv, ctab, stab, dq, dk, dv)


ATT_TQ = 256


def _attn_tile(q, k, v, zb, q0):
    s = dot_nt(q, k) * (1.0 / math.sqrt(QK))
    qc = (q0 + lax.broadcasted_iota(jnp.int32, s.shape, 0)) >> CHUNK_SHIFT
    kc = lax.broadcasted_iota(jnp.int32, s.shape, 1) >> CHUNK_SHIFT
    s = jnp.where(kc <= qc, s, -1e30)
    m = lax.stop_gradient(jnp.max(s, axis=-1, keepdims=True))
    p = jnp.exp(s - m)
    p = p / jnp.sum(p, axis=-1, keepdims=True)
    return dot_nn(p, v) * _silu(zb)


def _attn_in_specs():
    tq = ATT_TQ
    return [pl.BlockSpec((None, tq, QK), lambda h, i: (h, i, 0)), pl.BlockSpec((None, T, QK), lambda h, i: (h, 0, 0)),
            pl.BlockSpec((None, T, 128), lambda h, i: (h, 0, 0)),
            pl.BlockSpec((tq, 128), lambda h, i: (i, OFF_ZB // 128 + h))]


def attn_fwd(q, k, v, proj, l):
    tq = ATT_TQ

    def body(q_ref, k_ref, v_ref, z_ref, y_ref):
        for g in range(T // tq):
            @pl.when(pl.program_id(1) == g)
            def _(g=g):
                kmax = tq * (g + 1)
                y_ref[...] = _attn_tile(q_ref[...], k_ref[0:kmax, :], v_ref[0:kmax, :], z_ref[...], g * tq)

    return pl.pallas_call(
        body, grid=(HEADS, T // tq), in_specs=_attn_in_specs(),
        out_specs=pl.BlockSpec((tq, 128), lambda h, i: (i, h)),
        out_shape=jax.ShapeDtypeStruct((T, 1024), F32),
        name=f"attn_fwd_l{l}", compiler_params=_params(("arbitrary", "arbitrary")))(q, k, v, proj)


def attn_bwd(q, k, v, proj, dy, l):
    tq = ATT_TQ

    def body(q_ref, k_ref, v_ref, z_ref, dy_ref, dq_ref, dk_ref, dv_ref, dz_ref):
        @pl.when(pl.program_id(1) == 0)
        def _():
            dk_ref[...] = jnp.zeros_like(dk_ref)
            dv_ref[...] = jnp.zeros_like(dv_ref)

        for g in range(T // tq):
            @pl.when(pl.program_id(1) == g)
            def _(g=g):
                kmax = tq * (g + 1)
                fn = lambda q_, k_, v_, z_: _attn_tile(q_, k_, v_, z_, g * tq)
                _, vjp = jax.vjp(fn, q_ref[...], k_ref[0:kmax, :], v_ref[0:kmax, :], z_ref[...])
                dq, dk, dv, dz = vjp(dy_ref[...])
                dq_ref[...] = dq
                dz_ref[...] = dz.astype(BF16)
                dk_ref[0:kmax, :] += dk
                dv_ref[0:kmax, :] += dv

    return pl.pallas_call(
        body, grid=(HEADS, T // tq),
        in_specs=_attn_in_specs() + [pl.BlockSpec((tq, 128), lambda h, i: (i, h))],
        out_specs=[pl.BlockSpec((None, tq, QK), lambda h, i: (h, i, 0)), pl.BlockSpec((None, T, QK), lambda h, i: (h, 0, 0)),
                   pl.BlockSpec((None, T, 128), lambda h, i: (h, 0, 0)), pl.BlockSpec((tq, 128), lambda h, i: (i, h))],
        out_shape=[jax.ShapeDtypeStruct((HEADS, T, QK), F32), jax.ShapeDtypeStruct((HEADS, T, QK), F32),
                   jax.ShapeDtypeStruct((HEADS, T, 128), F32), jax.ShapeDtypeStruct((T, 1024), BF16)],
        name=f"attn_bwd_l{l}", compiler_params=_params(("arbitrary", "arbitrary")))(q, k, v, proj, dy)


LRU_TT = 256


def _lru_gates(xc, wa, wx, ba, bx, lam):
    r = _sigmoid(dot_nn(xc, wa) + ba)
    i = _sigmoid(dot_nn(xc, wx) + bx)
    sp = jnp.maximum(-lam, 0.0) + jnp.log1p(jnp.exp(-jnp.abs(lam)))
    log_a = -8.0 * r * sp
    a = jnp.exp(log_a)
    mult = jnp.sqrt(jnp.maximum(1.0 - jnp.exp(2.0 * log_a), 0.0))
    return a, mult * (i * xc)


def _shift_down(x, s, halo):
    xs = pltpu.roll(x, s, 0)
    row = lax.broadcasted_iota(jnp.int32, halo.shape, 0)
    top = jnp.where(row < s, pltpu.roll(halo, s, 0), xs[0:8])
    return jnp.concatenate([top, xs[8:]], axis=0)


def _shift_up(x, s, halo):
    n = x.shape[0]
    xs = pltpu.roll(x, n - s, 0)
    row = lax.broadcasted_iota(jnp.int32, halo.shape, 0)
    bot = jnp.where(row >= 8 - s, pltpu.roll(halo, 8 - s, 0), xs[n - 8:n])
    return jnp.concatenate([xs[:n - 8], bot], axis=0)


def _conv(x, halo, w_ref, b):
    return (w_ref[3:4, :] * x + w_ref[2:3, :] * _shift_down(x, 1, halo) + w_ref[1:2, :] * _shift_down(x, 2, halo)
            + w_ref[0:1, :] * _shift_down(x, 3, halo) + b)


def _scan(a, b, reverse):
    n = a.shape[0]
    row = lax.broadcasted_iota(jnp.int32, a.shape, 0)
    d = 1
    while d < n:
        if reverse:
            keep = row < n - d
            a_sh = jnp.where(keep, pltpu.roll(a, n - d, 0), 1.0)
            b_sh = jnp.where(keep, pltpu.roll(b, n - d, 0), 0.0)
        else:
            keep = row >= d
            a_sh = jnp.where(keep, pltpu.roll(a, d, 0), 1.0)
            b_sh = jnp.where(keep, pltpu.roll(b, d, 0), 0.0)
        b = a * b_sh + b
        a = a * a_sh
        d *= 2
    return a, b


def _lru_param_specs(time_map):
    ct = LRU_TILE
    vec = pl.BlockSpec((1, ct), lambda n, i: (0, n))
    return [pl.BlockSpec((4, ct), lambda n, i: (0, n)), vec,
            pl.BlockSpec((None, ct, ct), lambda n, i: (n, 0, 0)), pl.BlockSpec((None, ct, ct), lambda n, i: (n, 0, 0)),
            vec, vec, vec]


def lru_fwd(proj, conv_w, conv_b, wa, wx, ba, bx, lam, l):
    tt, ct = LRU_TT, LRU_TILE

    def body(x_ref, z_ref, cw_ref, cb_ref, wa_ref, wx_ref, ba_ref, bx_ref, lam_ref, h_ref, y_ref, halo, hcar):
        @pl.when(pl.program_id(1) == 0)
        def _():
            halo[...] = jnp.zeros_like(halo)
            hcar[...] = jnp.zeros_like(hcar)

        x = x_ref[...]
        xc = _conv(x, halo[...], cw_ref, cb_ref[...])
        halo[...] = x[tt - 8:tt]
        a, b = _lru_gates(xc, wa_ref[...], wx_ref[...], ba_ref[...], bx_ref[...], lam_ref[...])
        a_cum, b_cum = _scan(a, b, False)
        h = a_cum * hcar[...] + b_cum
        h_ref[...] = h
        hcar[...] = h_ref[tt - 1:tt, :]
        y_ref[...] = h * _silu(z_ref[...])

    seq = pl.BlockSpec((tt, ct), lambda n, i: (i, n))
    return pl.pallas_call(
        body, grid=(LRU_W // ct, T // tt),
        in_specs=[pl.BlockSpec((tt, ct), lambda n, i: (i, OFF_XC // ct + n)),
                  pl.BlockSpec((tt, ct), lambda n, i: (i, OFF_ZC // ct + n))] + _lru_param_specs(None),
        out_specs=[seq, seq],
        out_shape=[jax.ShapeDtypeStruct((T, LRU_W), F32), jax.ShapeDtypeStruct((T, LRU_W), F32)],
        scratch_shapes=[pltpu.VMEM((8, ct), F32), pltpu.VMEM((1, ct), F32)],
        name=f"lru_fwd_l{l}", compiler_params=_params(("arbitrary", "arbitrary")))(
            proj, proj, conv_w, conv_b, wa, wx, ba, bx, lam)


def lru_bwd(proj, hseq, dy, conv_w, conv_b, wa, wx, ba, bx, lam, l):
    tt, ct = LRU_TT, LRU_TILE
    nt = T // tt
    rev = lambda i: nt - 1 - i
    prev8 = lambda i: jnp.maximum(rev(i) * (tt // 8) - 1, 0)

    def body(x_ref, xh_ref, z_ref, h_ref, hh_ref, dy_ref, cw_ref, cb_ref, wa_ref, wx_ref, ba_ref, bx_ref, lam_ref,
             dx_ref, dz_ref, dcw_ref, dcb_ref, dwa_ref, dwx_ref, dba_ref, dbx_ref, dlam_ref, gcar, dhalo):
        i = pl.program_id(1)
        first = i == 0

        @pl.when(first)
        def _():
            gcar[...] = jnp.zeros_like(gcar)
            dhalo[...] = jnp.zeros_like(dhalo)

        at_start = rev(i) == 0
        x = x_ref[...]
        xhalo = jnp.where(at_start, 0.0, xh_ref[...])
        sh = [x, _shift_down(x, 1, xhalo), _shift_down(x, 2, xhalo), _shift_down(x, 3, xhalo)]
        xc = (cw_ref[3:4, :] * sh[0] + cw_ref[2:3, :] * sh[1] + cw_ref[1:2, :] * sh[2] + cw_ref[0:1, :] * sh[3]
              + cb_ref[...])
        (a, b), vjp = jax.vjp(_lru_gates, xc, wa_ref[...], wx_ref[...], ba_ref[...], bx_ref[...], lam_ref[...])
        hs = h_ref[...]
        hprev = _shift_down(hs, 1, jnp.where(at_start, 0.0, hh_ref[...]))
        z = z_ref[...]
        sg = _sigmoid(z)
        dy = dy_ref[...]
        dz_ref[...] = (dy * hs * (sg * (1.0 + z * (1.0 - sg)))).astype(BF16)
        dh = dy * (z * sg)
        row = lax.broadcasted_iota(jnp.int32, a.shape, 0)
        a_next = jnp.where(row < tt - 1, pltpu.roll(a, tt - 1, 0), 1.0)
        a_cum, b_cum = _scan(a_next, dh, True)
        g = a_cum * gcar[...] + b_cum
        dxc, dwa, dwx, dba, dbx, dlam = vjp((g * hprev, g))
        dx = (cw_ref[3:4, :] * dxc + cw_ref[2:3, :] * _shift_up(dxc, 1, dhalo[...])
              + cw_ref[1:2, :] * _shift_up(dxc, 2, dhalo[...]) + cw_ref[0:1, :] * _shift_up(dxc, 3, dhalo[...]))
        dx_ref[...] = dx.astype(BF16)
        dhalo[...] = dxc[0:8]
        ag = a * g
        gcar[...] = ag[0:1]
        dcw = jnp.concatenate([jnp.sum(dxc * sh[3 - j], axis=0, keepdims=True) for j in range(4)], axis=0)
        _acc(dcw_ref, dcw, first)
        _acc(dcb_ref, jnp.sum(dxc, axis=0, keepdims=True), first)
        _acc(dwa_ref, dwa, first)
        _acc(dwx_ref, dwx, first)
        _acc(dba_ref, dba, first)
        _acc(dbx_ref, dbx, first)
        _acc(dlam_ref, dlam, first)

    xcol = OFF_XC // ct
    zcol = OFF_ZC // ct
    vec = pl.BlockSpec((1, ct), lambda n, i: (0, n))
    mat = pl.BlockSpec((None, ct, ct), lambda n, i: (n, 0, 0))
    seq = pl.BlockSpec((tt, ct), lambda n, i: (rev(i), n))
    return pl.pallas_call(
        body, grid=(LRU_W // ct, nt),
        in_specs=[pl.BlockSpec((tt, ct), lambda n, i: (rev(i), xcol + n)),
                  pl.BlockSpec((8, ct), lambda n, i: (prev8(i), xcol + n)),
                  pl.BlockSpec((tt, ct), lambda n, i: (rev(i), zcol + n)),
                  seq, pl.BlockSpec((8, ct), lambda n, i: (prev8(i), n)), seq] + _lru_param_specs(None),
        out_specs=[seq, seq, pl.BlockSpec((4, ct), lambda n, i: (0, n)), vec, mat, mat, vec, vec, vec],
        out_shape=[jax.ShapeDtypeStruct((T, LRU_W), BF16), jax.ShapeDtypeStruct((T, LRU_W), BF16),
                   jax.ShapeDtypeStruct((4, LRU_W), F32), jax.ShapeDtypeStruct((1, LRU_W), F32),
                   jax.ShapeDtypeStruct((2, ct, ct), F32), jax.ShapeDtypeStruct((2, ct, ct), F32),
                   jax.ShapeDtypeStruct((1, LRU_W), F32), jax.ShapeDtypeStruct((1, LRU_W), F32),
                   jax.ShapeDtypeStruct((1, LRU_W), F32)],
        scratch_shapes=[pltpu.VMEM((1, ct), F32), pltpu.VMEM((8, ct), F32)],
        name=f"lru_bwd_l{l}", compiler_params=_params(("arbitrary", "arbitrary")))(
            proj, proj, proj, hseq, hseq, dy, conv_w, conv_b, wa, wx, ba, bx, lam)


def proj_fwd(y, w, l, tag):
    tm = 512
    k = y.shape[1]

    def body(y_ref, w_ref, o_ref):
        o_ref[...] = _dg(y_ref[...], w_ref[...], _NN)

    return pl.pallas_call(
        body, grid=(T // tm,),
        in_specs=[pl.BlockSpec((tm, k), lambda i: (i, 0)), pl.BlockSpec((None, k, D), lambda i: (l, 0, 0))],
        out_specs=pl.BlockSpec((tm, D), lambda i: (i, 0)), out_shape=jax.ShapeDtypeStruct((T, D), F32),
        name=f"proj_{tag}_fwd_l{l}", compiler_params=_params(("arbitrary",)))(y, w)


def proj_bwd(y, dp, w, l, tag):
    tm = 512
    k = y.shape[1]

    def body(y_ref, dp_ref, w_ref, dy_ref, dw_ref):
        dp = dp_ref[...]
        dy_ref[...] = _dg(dp, w_ref[...], _NT)
        _acc(dw_ref, _dg(y_ref[...], dp, _TN), pl.program_id(0) == 0)

    return pl.pallas_call(
        body, grid=(T // tm,),
        in_specs=[pl.BlockSpec((tm, k), lambda i: (i, 0)), pl.BlockSpec((tm, D), lambda i: (i, 0)),
                  pl.BlockSpec((None, k, D), lambda i: (l, 0, 0))],
        out_specs=[pl.BlockSpec((tm, k), lambda i: (i, 0)), pl.BlockSpec((k, D), lambda i: (0, 0))],
        out_shape=[jax.ShapeDtypeStruct((T, k), F32), jax.ShapeDtypeStruct((k, D), F32)],
        name=f"proj_{tag}_bwd_l{l}", compiler_params=_params(("arbitrary",)))(y, dp, w)


OUT_TM = 256


def _out_tile(pa, pb, pc, ga, gb, gc, wout, post_g):
    merged = _sigmoid(ga) * pa + _sigmoid(gb) * pb + _sigmoid(gc) * pc
    return _rms(dot_nn(merged, wout), post_g)


def _out_in_specs(l):
    tm = OUT_TM
    tok = pl.BlockSpec((tm, D), lambda i: (i, 0))
    gate = lambda off: pl.BlockSpec((tm, 512), lambda i, off=off: (i, off // 512))
    return [tok, tok, tok, gate(OFF_GA), gate(OFF_GA + 512), gate(OFF_GB), gate(OFF_GB + 512), gate(OFF_GC),
            gate(OFF_GC + 512), pl.BlockSpec((None, D, D), lambda i: (l, 0, 0)), pl.BlockSpec((1, D), lambda i: (0, 0))]


def _gates(refs):
    return [jnp.concatenate([refs[2 * j][...], refs[2 * j + 1][...]], axis=1) for j in range(3)]


def out_fwd(x, pa, pb, pc, proj, wout, post_g, l):
    tm = OUT_TM

    def body(pa_ref, pb_ref, pc_ref, g0, g1, g2, g3, g4, g5, w_ref, pg_ref, x_ref, o_ref):
        ga, gb, gc = _gates([g0, g1, g2, g3, g4, g5])
        o_ref[...] = x_ref[...] + _out_tile(pa_ref[...], pb_ref[...], pc_ref[...], ga, gb, gc, w_ref[...], pg_ref[...])

    tok = pl.BlockSpec((tm, D), lambda i: (i, 0))
    return pl.pallas_call(
        body, grid=(T // tm,), in_specs=_out_in_specs(l) + [tok], out_specs=tok,
        out_shape=jax.ShapeDtypeStruct((T, D), F32),
        name=f"out_fwd_l{l}", compiler_params=_params(("arbitrary",)))(
            pa, pb, pc, proj, proj, proj, proj, proj, proj, wout, post_g, x)


def out_bwd(pa, pb, pc, proj, wout, post_g, dxn, l):
    tm = OUT_TM

    def body(pa_ref, pb_ref, pc_ref, g0, g1, g2, g3, g4, g5, w_ref, pg_ref, dxn_ref,
             dpa_ref, dpb_ref, dpc_ref, dg_ref, dw_ref, dpg_ref):
        first = pl.program_id(0) == 0
        ga, gb, gc = _gates([g0, g1, g2, g3, g4, g5])
        _, vjp = jax.vjp(_out_tile, pa_ref[...], pb_ref[...], pc_ref[...], ga, gb, gc, w_ref[...].astype(F32), pg_ref[...])
        dpa, dpb, dpc, dga, dgb, dgc, dw, dpg = vjp(dxn_ref[...])
        dpa_ref[...] = dpa.astype(BF16)
        dpb_ref[...] = dpb.astype(BF16)
        dpc_ref[...] = dpc.astype(BF16)
        dg_ref[:, 0:1024] = dga.astype(BF16)
        dg_ref[:, 1024:2048] = dgb.astype(BF16)
        dg_ref[:, 2048:3072] = dgc.astype(BF16)
        _acc(dw_ref, dw, first)
        _acc(dpg_ref, dpg, first)

    tok = pl.BlockSpec((tm, D), lambda i: (i, 0))
    return pl.pallas_call(
        body, grid=(T // tm,), in_specs=_out_in_specs(l) + [tok],
        out_specs=[tok, tok, tok, pl.BlockSpec((tm, 3072), lambda i: (i, 0)), pl.BlockSpec((D, D), lambda i: (0, 0)),
                   pl.BlockSpec((1, D), lambda i: (0, 0))],
        out_shape=[jax.ShapeDtypeStruct((T, D), BF16)] * 3 + [jax.ShapeDtypeStruct((T, 3072), BF16),
                                                            jax.ShapeDtypeStruct((D, D), F32), jax.ShapeDtypeStruct((1, D), F32)],
        name=f"out_bwd_l{l}", compiler_params=_params(("arbitrary",)))(
            pa, pb, pc, proj, proj, proj, proj, proj, proj, wout, post_g, dxn)


def loss_head(y, target):
    tm = 256

    def body(y_ref, t_ref, loss_ref, dy_ref):
        e = y_ref[...] - t_ref[...]
        dy_ref[...] = e * (1.0 / D)
        val = 0.5 * jnp.sum(jnp.mean(e * e, axis=-1, keepdims=True), axis=0, keepdims=True)
        _acc(loss_ref, jnp.broadcast_to(val, (8, 128)), pl.program_id(0) == 0)

    tok = pl.BlockSpec((tm, D), lambda i: (i, 0))
    total, dy = pl.pallas_call(
        body, grid=(T // tm,), in_specs=[tok, tok],
        out_specs=[pl.BlockSpec((8, 128), lambda i: (0, 0)), tok],
        out_shape=[jax.ShapeDtypeStruct((8, 128), F32), jax.ShapeDtypeStruct((T, D), F32)],
        name="loss_head", compiler_params=_params(("arbitrary",)))(y, target)
    return total[0, 0], dy


def _rope_tables():
    pos = jnp.arange(T, dtype=F32)
    inv_freq = 10000.0 ** (-jnp.arange(0, 64, 2, dtype=F32) / 64)
    ang = pos[:, None] * inv_freq[None, :]
    cos, sin = jnp.cos(ang), jnp.sin(ang)
    ctab = jnp.concatenate([jnp.ones((T, 128), F32), cos, cos], axis=1)
    stab = jnp.concatenate([jnp.zeros((T, 128), F32), -sin, sin], axis=1)
    return ctab, stab


def _block_diag(w):
    w5 = w.reshape(L, 2, 8, 80, 80)
    eye = jnp.eye(8, dtype=w.dtype)
    return jnp.einsum("lnbij,bc->lnbicj", w5, eye).reshape(L, 2, LRU_TILE, LRU_TILE)


def _block_diag_t(dw):
    dw5 = dw.reshape(2, 8, 80, 8, 80)
    return jnp.einsum("nbicj,bc->nbij", dw5, jnp.eye(8, dtype=dw.dtype)).reshape(16, 80, 80)


def _layer_fwd(x, l, w, tabs):
    row = lambda a: a[l][None]
    proj, h = inproj_fwd(x, row(w["pre_norm_g"]), w["w_in_t"], l)
    ya = gmlp_fwd(proj, row(w["gm_ln_g"]), row(w["gm_ln_b"]), w["gm_ws"][l], w["gm_bs"][l][..., None], l)
    q, k, v = qkv_fwd(proj, row(w["mla_q_norm_g"]), row(w["kv_g384"]), w["wq"][l], w["wkv"][l], tabs[0], tabs[1], l)
    yb = attn_fwd(q, k, v, proj, l)
    hseq, yc = lru_fwd(proj, w["lru_conv_w"][l], row(w["lru_conv_b"]), w["wa_dense"][l], w["wx_dense"][l],
                       row(w["lru_b_a"]), row(w["lru_b_x"]), row(w["lru_lambda"]), l)
    pa = proj_fwd(ya, w["w_proj_a"], l, "a")
    pb = proj_fwd(yb, w["w_proj_b"], l, "b")
    pc = proj_fwd(yc, w["w_proj_c"], l, "c")
    xn = out_fwd(x, pa, pb, pc, proj, w["w_out"], row(w["post_norm_g"]), l)
    return xn, (x, proj, h, ya, q, k, v, yb, hseq, yc, pa, pb, pc)


def _layer_bwd(dxn, l, w, tabs, saved, gbuf):
    x, proj, h, ya, q, k, v, yb, hseq, yc, pa, pb, pc = saved
    row = lambda a: a[l][None]
    g = {}
    dpa, dpb, dpc, dgates, g["w_out"], dpost = out_bwd(pa, pb, pc, proj, w["w_out"], row(w["post_norm_g"]), dxn, l)
    g["post_norm_g"] = dpost[0]
    dya, g["w_proj_a"] = proj_bwd(ya, dpa, w["w_proj_a"], l, "a")
    dyb, g["w_proj_b"] = proj_bwd(yb, dpb, w["w_proj_b"], l, "b")
    dyc, g["w_proj_c"] = proj_bwd(yc, dpc, w["w_proj_c"], l, "c")
    dseg_a, dln_g, dln_b, g["gm_ws"], dbs = gmlp_bwd(proj, row(w["gm_ln_g"]), row(w["gm_ln_b"]), w["gm_ws"][l],
                                                    w["gm_bs"][l][..., None], dya, l)
    g["gm_ln_g"], g["gm_ln_b"], g["gm_bs"] = dln_g[0], dln_b[0], dbs[..., 0]
    dq, dk, dv, dzb = attn_bwd(q, k, v, proj, dyb, l)
    dseg_q, dqg, dkvg, g["wq"], g["wkv"] = qkv_bwd(proj, row(w["mla_q_norm_g"]), row(w["kv_g384"]), w["wq"][l],
                                                   w["wkv"][l], tabs[0], tabs[1], dq, dk, dv, l)
    g["mla_q_norm_g"], g["mla_kv_norm_g"] = dqg[0], dkvg[0, :256]
    dxc, dzc, g["lru_conv_w"], dcb, dwa, dwx, dba, dbx, dlam = lru_bwd(
        proj, hseq, dyc, w["lru_conv_w"][l], row(w["lru_conv_b"]), w["wa_dense"][l], w["wx_dense"][l],
        row(w["lru_b_a"]), row(w["lru_b_x"]), row(w["lru_lambda"]), l)
    g["lru_conv_b"], g["lru_b_a"], g["lru_b_x"], g["lru_lambda"] = dcb[0], dba[0], dbx[0], dlam[0]
    g["lru_w_a"], g["lru_w_x"] = _block_diag_t(dwa), _block_diag_t(dwx)
    dproj = jnp.concatenate([dseg_a, dseg_q, dzb, jnp.zeros((T, PAD2), dzb.dtype), dxc, dzc, dgates], axis=1)
    gbuf, dh = inproj_bwd(dproj, h, w["w_in_t"], l, gbuf)
    dx, dpre = prenorm_bwd(x, row(w["pre_norm_g"]), dh, dxn, l)
    g["pre_norm_g"] = dpre[0]
    return dx, gbuf, g


def _local_step(x, target, w):
    tabs = _rope_tables()
    saved = []
    for l in range(L):
        x, s = _layer_fwd(x, l, w, tabs)
        saved.append(s)
    loss, dx = loss_head(x, target)
    gbuf = None
    grads = [None] * L
    for l in reversed(range(L)):
        dx, gbuf, grads[l] = _layer_bwd(dx, l, w, tabs, saved[l], gbuf)
    return loss, dx, gbuf, grads


MESH = pl.DeviceIdType.MESH
ANY = pl.BlockSpec(memory_space=pl.ANY)
FLIPS = ((1, 0), (0, 1), (1, 1))


def _win_off(k, s):
    g = SHARD * k + s
    return g + jnp.where(g >= PAD1_AT, PAD1, 0) + jnp.where(g >= PAD2_AT, PAD2, 0)


def _plain_off(rows):
    return lambda k, s: rows * k + s


class Spec:
    def __init__(self, rows, cols, full_rows, pieces=None, off=None, layers=L, packed=None):
        self.rows, self.cols, self.full_rows, self.layers = rows, cols, full_rows, layers
        self.pieces = pieces or ((0, rows),)
        self.off = off or _plain_off(rows)
        self.packed = cols % 256 == 0 if packed is None else packed
        self.wcols = cols // 2 if self.packed else cols

    def to_words(self, a):
        return _pack(a) if self.packed else a

    def from_words(self, p):
        return _unpack(p) if self.packed else p


def _pack(a):
    half = a.shape[-1] // 2
    lo = lax.bitcast_convert_type(a[:, :half].astype(jnp.bfloat16).astype(F32), jnp.uint32)
    hi = lax.bitcast_convert_type(a[:, half:].astype(jnp.bfloat16).astype(F32), jnp.uint32)
    return lax.bitcast_convert_type((lo >> 16) | (hi & jnp.uint32(0xFFFF0000)), F32)


def _unpack(p):
    w = lax.bitcast_convert_type(p, jnp.uint32)
    lo = lax.bitcast_convert_type(w << 16, F32)
    hi = lax.bitcast_convert_type(w & jnp.uint32(0xFFFF0000), F32)
    return jnp.concatenate([lo, hi], axis=-1)


WEIGHT_SPECS = {
    "w_in_t": Spec(SHARD, D, NPAD, WIN_PIECES, _win_off),
    "wq": Spec(192, 384, 1536),
    "wkv": Spec(256, 256, 2048),
    "conv": Spec(160, 128, 1280),
    "w_proj_a": Spec(128, D, 1024),
    "w_proj_b": Spec(128, D, 1024),
    "w_proj_c": Spec(160, D, 1280),
    "w_out": Spec(128, D, 1024),
}
REP_ROWS = 72
REP_SPEC = Spec(REP_ROWS, D, REP_ROWS * NDEV, layers=1, packed=False)


def _coords():
    return lax.axis_index("x"), lax.axis_index("y"), lax.axis_index("c")


def _rows(ref, start, n):
    if not isinstance(start, int):
        start = pl.multiple_of(start, 8)
    return ref.at[:, pl.ds(start, n), :]


def _col_tile(cols):
    return 256 if cols % 256 == 0 else cols


def _n_pieces(specs):
    return sum(len(sp.pieces) for sp in specs)


def pack_place(shard, sp, tag):
    gaps = ((PAD1_AT, PAD1), (PAD2_AT + PAD1, PAD2)) if sp.off is _win_off else ()
    npc = len(sp.pieces)

    def body(s_ref, words_ref, full_ref, buf, zbuf, sem):
        l = pl.program_id(0)
        x, y, c = _coords()
        me = 4 * x + 2 * y + c
        words = sp.to_words(s_ref[...])
        words_ref[...] = words
        buf[...] = words
        copies = [pltpu.make_async_copy(buf.at[pl.ds(s, n), :],
                                        full_ref.at[l, pl.ds(pl.multiple_of(sp.off(me, s), 8), n), :], sem.at[i])
                  for i, (s, n) in enumerate(sp.pieces)]
        if gaps:
            zbuf[...] = jnp.zeros_like(zbuf)
            copies += [pltpu.make_async_copy(zbuf.at[pl.ds(0, n), :], full_ref.at[l, pl.ds(at, n), :], sem.at[npc + i])
                       for i, (at, n) in enumerate(gaps)]
        for cp in copies:
            cp.start()
        for cp in copies:
            cp.wait()

    return pl.pallas_call(
        body, grid=(sp.layers,), in_specs=[pl.BlockSpec((None, sp.rows, sp.cols), lambda l: (l, 0, 0))],
        out_specs=[pl.BlockSpec((None, sp.rows, sp.wcols), lambda l: (l, 0, 0)), ANY],
        out_shape=[jax.ShapeDtypeStruct((sp.layers, sp.rows, sp.wcols), F32),
                   jax.ShapeDtypeStruct((sp.layers, sp.full_rows, sp.wcols), F32)],
        scratch_shapes=[pltpu.VMEM((sp.rows, sp.wcols), F32), pltpu.VMEM((PAD2 if gaps else 8, sp.wcols), F32),
                        pltpu.SemaphoreType.DMA((npc + len(gaps),))],
        name=f"pack_place_{tag}", compiler_params=_params(("arbitrary",)))(shard)


def gather_send(words, fulls, specs, tag):
    ns, npc = len(specs), _n_pieces(specs)

    def body(*refs):
        srcs, bufs = refs[:ns], refs[2 * ns:3 * ns]
        ssem, rsem = refs[3 * ns:]
        x, y, c = _coords()
        me = 4 * x + 2 * y + c
        targets = [(x, y, 1 - c)] + [(x ^ fx, y ^ fy, c) for fx, fy in FLIPS]
        target_k = [4 * tx + 2 * ty + tc for tx, ty, tc in targets]
        waits = []
        p = 0
        for src, buf, sp in zip(srcs, bufs, specs):
            for s, n in sp.pieces:
                src_rows = _rows(src, s, n)
                mine = _rows(buf, sp.off(me, s), n)
                for t, tgt in enumerate(targets):
                    send = pltpu.make_async_remote_copy(src_rows, mine, ssem.at[t, p], rsem.at[t, p],
                                                        device_id=tgt, device_id_type=MESH)
                    send.start()
                    waits.append(send.wait_send)
                    theirs = _rows(buf, sp.off(target_k[t], s), n)
                    waits.append(pltpu.make_async_remote_copy(src_rows, theirs, ssem.at[t, p], rsem.at[t, p],
                                                              device_id=tgt, device_id_type=MESH).wait_recv)
                p += 1
        for w in waits:
            w()

    return pl.pallas_call(
        body, in_specs=[ANY] * (2 * ns), out_specs=[ANY] * ns,
        out_shape=[jax.ShapeDtypeStruct(f.shape, f.dtype) for f in fulls],
        input_output_aliases={ns + i: i for i in range(ns)},
        scratch_shapes=[pltpu.SemaphoreType.DMA((4, npc)), pltpu.SemaphoreType.DMA((4, npc))],
        name=f"gather_send_{tag}", compiler_params=pltpu.CompilerParams(has_side_effects=True))(*words, *fulls)


def unpack_weights(p, sp, tag):
    tr = 256 if sp.full_rows % 256 == 0 else sp.full_rows

    def body(p_ref, o_ref):
        o_ref[...] = _unpack(p_ref[...]).astype(jnp.bfloat16)

    return pl.pallas_call(
        body, grid=(sp.layers, sp.full_rows // tr),
        in_specs=[pl.BlockSpec((None, tr, sp.wcols), lambda l, i: (l, i, 0))],
        out_specs=pl.BlockSpec((None, tr, sp.cols), lambda l, i: (l, i, 0)),
        out_shape=jax.ShapeDtypeStruct((sp.layers, sp.full_rows, sp.cols), jnp.bfloat16),
        name=f"unpack_{tag}", compiler_params=_params(("arbitrary", "arbitrary")))(p)


def gather_forward(fulls, specs, tag):
    ns, npc = len(specs), _n_pieces(specs)

    def body(*refs):
        bufs = refs[ns:2 * ns]
        ssem, rsem = refs[2 * ns:]
        x, y, c = _coords()
        sibling = (x, y, 1 - c)
        waits = []
        p = 0
        for buf, sp in zip(bufs, specs):
            for s, n in sp.pieces:
                for t, (fx, fy) in enumerate(FLIPS):
                    chip = 4 * (x ^ fx) + 2 * (y ^ fy)
                    here = _rows(buf, sp.off(chip + c, s), n)
                    send = pltpu.make_async_remote_copy(here, here, ssem.at[t, p], rsem.at[t, p],
                                                        device_id=sibling, device_id_type=MESH)
                    send.start()
                    waits.append(send.wait_send)
                    there = _rows(buf, sp.off(chip + 1 - c, s), n)
                    waits.append(pltpu.make_async_remote_copy(here, there, ssem.at[t, p], rsem.at[t, p],
                                                              device_id=sibling, device_id_type=MESH).wait_recv)
                p += 1
        for w in waits:
            w()

    return pl.pallas_call(
        body, in_specs=[ANY] * ns, out_specs=[ANY] * ns,
        out_shape=[jax.ShapeDtypeStruct(f.shape, f.dtype) for f in fulls],
        input_output_aliases={i: i for i in range(ns)},
        scratch_shapes=[pltpu.SemaphoreType.DMA((3, npc)), pltpu.SemaphoreType.DMA((3, npc))],
        name=f"gather_forward_{tag}", compiler_params=pltpu.CompilerParams(has_side_effects=True))(*fulls)


def all_gather(shards, specs, names, tag):
    placed = [pack_place(s, sp, f"{tag}_{n}") for s, sp, n in zip(shards, specs, names)]
    fulls = gather_send([p[0] for p in placed], [p[1] for p in placed], specs, tag)
    return gather_forward(fulls, specs, tag)


def reduce_pair(grads, specs, tag):
    ns, npc = len(specs), _n_pieces(specs)

    def body(*refs):
        srcs, theirs = refs[:ns], refs[ns:2 * ns]
        ssem, rsem = refs[2 * ns:]
        x, y, c = _coords()
        sibling = (x, y, 1 - c)
        waits = []
        p = 0
        for src, their, sp in zip(srcs, theirs, specs):
            for s, n in sp.pieces:
                for j in range(4):
                    send = pltpu.make_async_remote_copy(_rows(src, sp.off(2 * j + 1 - c, s), n), _rows(their.at[j], s, n),
                                                        ssem.at[j, p], rsem.at[j, p], device_id=sibling, device_id_type=MESH)
                    send.start()
                    waits.append(send.wait)
                p += 1
        for w in waits:
            w()

    return pl.pallas_call(
        body, in_specs=[ANY] * ns, out_specs=[ANY] * ns,
        out_shape=[jax.ShapeDtypeStruct((4, sp.layers, sp.rows, sp.cols), F32) for sp in specs],
        scratch_shapes=[pltpu.SemaphoreType.DMA((4, npc)), pltpu.SemaphoreType.DMA((4, npc))],
        name=f"reduce_pair_{tag}", compiler_params=pltpu.CompilerParams(has_side_effects=True))(*grads)


def pair_sum(g, r1, sp, tag):
    npc = len(sp.pieces)

    def body(g_ref, r_ref, own_ref, words_ref, gbuf, sem):
        l, j = pl.program_id(0), pl.program_id(1)
        x, y, c = _coords()
        copies = [pltpu.make_async_copy(g_ref.at[l, pl.ds(pl.multiple_of(sp.off(2 * j + c, s), 8), n), :],
                                        gbuf.at[pl.ds(s, n), :], sem.at[i]) for i, (s, n) in enumerate(sp.pieces)]
        for cp in copies:
            cp.start()
        for cp in copies:
            cp.wait()
        p = gbuf[...] + r_ref[...]
        words_ref[...] = sp.to_words(p)

        @pl.when(j == 2 * x + y)
        def _():
            own_ref[...] = p

    return pl.pallas_call(
        body, grid=(sp.layers, 4),
        in_specs=[ANY, pl.BlockSpec((None, None, sp.rows, sp.cols), lambda l, j: (j, l, 0, 0))],
        out_specs=[pl.BlockSpec((None, sp.rows, sp.cols), lambda l, j: (l, 0, 0)),
                   pl.BlockSpec((None, None, sp.rows, sp.wcols), lambda l, j: (j, l, 0, 0))],
        out_shape=[jax.ShapeDtypeStruct((sp.layers, sp.rows, sp.cols), F32),
                   jax.ShapeDtypeStruct((4, sp.layers, sp.rows, sp.wcols), F32)],
        scratch_shapes=[pltpu.VMEM((sp.rows, sp.cols), F32), pltpu.SemaphoreType.DMA((npc,))],
        name=f"pair_sum_{tag}", compiler_params=_params(("arbitrary", "arbitrary")))(g, r1)


def reduce_chips(words, specs, tag):
    ns = len(specs)

    def body(*refs):
        srcs, dsts = refs[:ns], refs[ns:2 * ns]
        ssem, rsem = refs[2 * ns:]
        x, y, c = _coords()
        waits = []
        for i, (src, dst) in enumerate(zip(srcs, dsts)):
            for t, (fx, fy) in enumerate(FLIPS):
                tx, ty = x ^ fx, y ^ fy
                send = pltpu.make_async_remote_copy(src.at[2 * tx + ty], dst.at[t], ssem.at[t, i], rsem.at[t, i],
                                                    device_id=(tx, ty, c), device_id_type=MESH)
                send.start()
                waits.append(send.wait)
        for w in waits:
            w()

    return pl.pallas_call(
        body, in_specs=[ANY] * ns, out_specs=[ANY] * ns,
        out_shape=[jax.ShapeDtypeStruct((3,) + w.shape[1:], F32) for w in words],
        scratch_shapes=[pltpu.SemaphoreType.DMA((3, ns)), pltpu.SemaphoreType.DMA((3, ns))],
        name=f"reduce_chips_{tag}", compiler_params=pltpu.CompilerParams(has_side_effects=True))(*words)


def sum_chips(own, r2, sp, tag):
    def body(own_ref, r_ref, o_ref):
        o_ref[...] = ((own_ref[...] + sp.from_words(r_ref[0])) + sp.from_words(r_ref[1])) + sp.from_words(r_ref[2])

    blk = pl.BlockSpec((None, sp.rows, sp.cols), lambda l: (l, 0, 0))
    return pl.pallas_call(
        body, grid=(sp.layers,), in_specs=[blk, pl.BlockSpec((3, None, sp.rows, sp.wcols), lambda l: (0, l, 0, 0))],
        out_specs=blk, out_shape=jax.ShapeDtypeStruct((sp.layers, sp.rows, sp.cols), F32),
        name=f"sum_chips_{tag}", compiler_params=_params(("arbitrary",)))(own, r2)


def reduce_scatter(grads, specs, names, tag):
    theirs = reduce_pair(grads, specs, tag)
    sums = [pair_sum(g, r1, sp, f"{tag}_{n}") for g, r1, sp, n in zip(grads, theirs, specs, names)]
    r2 = reduce_chips([s[1] for s in sums], specs, tag)
    return [sum_chips(s[0], r, sp, f"{tag}_{n}") for s, r, sp, n in zip(sums, r2, specs, names)]


def adamw(w, g, m, v, name):
    shape = w.shape
    cols = shape[-1]
    rows = math.prod(shape[:-1])
    tr = rows
    while tr * cols * 4 > (1 << 20) and tr % 16 == 0:
        tr //= 2
    c1 = 1.0 - ADAM_B1 ** ADAM_STEP
    c2 = 1.0 - ADAM_B2 ** ADAM_STEP

    def body(w_ref, g_ref, m_ref, v_ref, d_ref, nm_ref, nv_ref):
        gv = g_ref[...]
        m2 = ADAM_B1 * m_ref[...] + (1.0 - ADAM_B1) * gv
        v2 = ADAM_B2 * v_ref[...] + (1.0 - ADAM_B2) * (gv * gv)
        nm_ref[...] = m2
        nv_ref[...] = v2
        d_ref[...] = -ADAM_LR * ((m2 / c1) / (jnp.sqrt(v2 / c2) + ADAM_EPS) + ADAM_WD * w_ref[...])

    blk = pl.BlockSpec((tr, cols), lambda i: (i, 0))
    outs = pl.pallas_call(
        body, grid=(rows // tr,), in_specs=[blk] * 4, out_specs=[blk] * 3,
        out_shape=[jax.ShapeDtypeStruct((rows, cols), F32)] * 3,
        name=f"adamw_{name}", compiler_params=_params(("arbitrary",)))(
            *[a.reshape(rows, cols) for a in (w, g, m, v)])
    return [o.reshape(shape) for o in outs]


WEIGHTS = ("pre_norm_g", "w_in", "gm_ln_g", "gm_ln_b", "gm_ws", "gm_bs", "mla_q_norm_g", "mla_w_uq", "mla_kv_norm_g",
           "mla_w_ukv", "lru_conv_w", "lru_conv_b", "lru_w_a", "lru_b_a", "lru_w_x", "lru_b_x", "lru_lambda",
           "w_proj_a", "w_proj_b", "w_proj_c", "w_out", "post_norm_g")
SHARDED = ("w_in", "mla_w_uq", "mla_w_ukv", "lru_conv_w", "w_proj_a", "w_proj_b", "w_proj_c", "w_out")
REPLICATED = tuple(n for n in WEIGHTS if n not in SHARDED)


def _step(x, target, wts, ms, vs):
    t12 = lambda a: jnp.swapaxes(a, 1, 2)
    names = list(WEIGHT_SPECS)
    specs = [WEIGHT_SPECS[n] for n in names]
    conv_t = jnp.pad(t12(wts["lru_conv_w"]), ((0, 0), (0, 0), (0, 124)))
    shards = [t12(wts["w_in"]), t12(wts["mla_w_uq"]), t12(wts["mla_w_ukv"]), conv_t,
              wts["w_proj_a"], wts["w_proj_b"], wts["w_proj_c"], wts["w_out"]]
    words = all_gather(shards, specs, names, "w")
    full = {n: unpack_weights(p, sp, n) if sp.packed else p for n, p, sp in zip(names, words, specs)}

    w = {n: wts[n] for n in REPLICATED}
    w["w_in_t"] = full["w_in_t"]
    w["wq"] = full["wq"].reshape(L, HEADS, 192, 384)
    w["wkv"] = full["wkv"].reshape(L, HEADS, 256, 256)
    w["lru_conv_w"] = t12(full["conv"][:, :, :4])
    for n in ("w_proj_a", "w_proj_b", "w_proj_c", "w_out"):
        w[n] = full[n]
    w["kv_g384"] = jnp.concatenate([wts["mla_kv_norm_g"], jnp.ones((L, 128), F32)], axis=1)
    w["wa_dense"] = _block_diag(wts["lru_w_a"])
    w["wx_dense"] = _block_diag(wts["lru_w_x"])

    loss, dx, gbuf, grads = _local_step(x, target, w)
    stack = lambda n: jnp.stack([grads[l][n] for l in range(L)])

    local = [gbuf, stack("wq").reshape(L, 1536, 384), stack("wkv").reshape(L, 2048, 256),
             jnp.pad(t12(stack("lru_conv_w")), ((0, 0), (0, 0), (0, 124))),
             stack("w_proj_a"), stack("w_proj_b"), stack("w_proj_c"), stack("w_out")]
    rep_flat = jnp.concatenate([stack(n).reshape(-1) for n in REPLICATED])
    rep_flat = jnp.pad(rep_flat, (0, REP_ROWS * NDEV * D - rep_flat.shape[0])).reshape(1, REP_ROWS * NDEV, D)
    summed = reduce_scatter(local + [rep_flat], specs + [REP_SPEC], names + ["rep"], "g")
    s = dict(zip(names, summed[:-1]))
    rep_full = all_gather([summed[-1]], [REP_SPEC], ["rep"], "rep")[0].reshape(-1)

    g = {"w_in": t12(s["w_in_t"]), "mla_w_uq": t12(s["wq"]), "mla_w_ukv": t12(s["wkv"]),
         "lru_conv_w": t12(s["conv"][:, :, :4])}
    for n in ("w_proj_a", "w_proj_b", "w_proj_c", "w_out"):
        g[n] = s[n]
    at = 0
    for n in REPLICATED:
        size = math.prod(wts[n].shape)
        g[n] = rep_full[at:at + size].reshape(wts[n].shape)
        at += size

    loss = lax.psum(loss, ("x", "y", "c"))
    upd = {n: adamw(wts[n], g[n], ms[n], vs[n], n) for n in WEIGHTS}
    return (loss, dx[None], *[g[n] for n in WEIGHTS], *[upd[n][0] for n in WEIGHTS],
            *[upd[n][1] for n in WEIGHTS], *[upd[n][2] for n in WEIGHTS])


def kernel(x, pre_norm_g, w_in, gm_ln_g, gm_ln_b, gm_ws, gm_bs, mla_q_norm_g, mla_w_uq, mla_kv_norm_g, mla_w_ukv, lru_conv_w, lru_conv_b, lru_w_a, lru_b_a, lru_w_x, lru_b_x, lru_lambda, w_proj_a, w_proj_b, w_proj_c, w_out, post_norm_g, loss_target, m_pre_norm_g, m_w_in, m_gm_ln_g, m_gm_ln_b, m_gm_ws, m_gm_bs, m_mla_q_norm_g, m_mla_w_uq, m_mla_kv_norm_g, m_mla_w_ukv, m_lru_conv_w, m_lru_conv_b, m_lru_w_a, m_lru_b_a, m_lru_w_x, m_lru_b_x, m_lru_lambda, m_w_proj_a, m_w_proj_b, m_w_proj_c, m_w_out, m_post_norm_g, v_pre_norm_g, v_w_in, v_gm_ln_g, v_gm_ln_b, v_gm_ws, v_gm_bs, v_mla_q_norm_g, v_mla_w_uq, v_mla_kv_norm_g, v_mla_w_ukv, v_lru_conv_w, v_lru_conv_b, v_lru_w_a, v_lru_b_a, v_lru_w_x, v_lru_b_x, v_lru_lambda, v_w_proj_a, v_w_proj_b, v_w_proj_c, v_w_out, v_post_norm_g):
    wts = dict(zip(WEIGHTS, (pre_norm_g, w_in, gm_ln_g, gm_ln_b, gm_ws, gm_bs, mla_q_norm_g, mla_w_uq, mla_kv_norm_g,
                             mla_w_ukv, lru_conv_w, lru_conv_b, lru_w_a, lru_b_a, lru_w_x, lru_b_x, lru_lambda,
                             w_proj_a, w_proj_b, w_proj_c, w_out, post_norm_g)))
    ms = dict(zip(WEIGHTS, (m_pre_norm_g, m_w_in, m_gm_ln_g, m_gm_ln_b, m_gm_ws, m_gm_bs, m_mla_q_norm_g, m_mla_w_uq,
                            m_mla_kv_norm_g, m_mla_w_ukv, m_lru_conv_w, m_lru_conv_b, m_lru_w_a, m_lru_b_a, m_lru_w_x,
                            m_lru_b_x, m_lru_lambda, m_w_proj_a, m_w_proj_b, m_w_proj_c, m_w_out, m_post_norm_g)))
    vs = dict(zip(WEIGHTS, (v_pre_norm_g, v_w_in, v_gm_ln_g, v_gm_ln_b, v_gm_ws, v_gm_bs, v_mla_q_norm_g, v_mla_w_uq,
                            v_mla_kv_norm_g, v_mla_w_ukv, v_lru_conv_w, v_lru_conv_b, v_lru_w_a, v_lru_b_a, v_lru_w_x,
                            v_lru_b_x, v_lru_lambda, v_w_proj_a, v_w_proj_b, v_w_proj_c, v_w_out, v_post_norm_g)))
    return _step(x[0], loss_target[0], wts, ms, vs)
```

```python
import functools
import math

import jax
import jax.numpy as jnp
from jax import lax
from jax.experimental import pallas as pl
from jax.experimental.pallas import tpu as pltpu

F32 = jnp.float32
BF16 = jnp.bfloat16

T = 2048
D = 1024
L = 2
NDEV = 8
EPS = 1e-6
CHUNK_SHIFT = 6
HEADS = 8
QK = 192
LRU_W = 1280
LRU_TILE = 640
N_IN = 10432
SHARD = N_IN // NDEV
OFF_U, OFF_V, OFF_ZA, OFF_CQ, OFF_CKV, OFF_ZB = 0, 1024, 2048, 3072, 3456, 3840
OFF_XC, OFF_ZC, OFF_GA, OFF_GB, OFF_GC = 5120, 6400, 7680, 8704, 9728
NPAD = 10752
PAD1_AT, PAD1 = 3776, 64
PAD2_AT, PAD2 = 4800, 256
WIN_PIECES = ((0, 888), (888, 280), (1168, 136))
VMEM_LIMIT = 60 * 1024 * 1024

ADAM_LR, ADAM_B1, ADAM_B2, ADAM_EPS, ADAM_WD, ADAM_STEP = 0.001, 0.9, 0.999, 1e-08, 0.01, 10

_NN = (((1,), (0,)), ((), ()))
_NT = (((1,), (1,)), ((), ()))
_TN = (((0,), (0,)), ((), ()))


def _dg(a, b, dims):
    return lax.dot_general(a.astype(BF16), b.astype(BF16), dims, preferred_element_type=F32)


@jax.custom_vjp
def dot_nn(a, b):
    return _dg(a, b, _NN)


def _nn_fwd(a, b):
    return _dg(a, b, _NN), (a, b)


def _nn_bwd(res, g):
    a, b = res
    return _dg(g, b, _NT).astype(a.dtype), _dg(a, g, _TN).astype(b.dtype)


dot_nn.defvjp(_nn_fwd, _nn_bwd)


@jax.custom_vjp
def dot_nt(a, b):
    return _dg(a, b, _NT)


def _nt_fwd(a, b):
    return _dg(a, b, _NT), (a, b)


def _nt_bwd(res, g):
    a, b = res
    return _dg(g, b, _NN).astype(a.dtype), _dg(g, a, _TN).astype(b.dtype)


dot_nt.defvjp(_nt_fwd, _nt_bwd)


def _params(sem=None):
    return pltpu.CompilerParams(dimension_semantics=sem, vmem_limit_bytes=VMEM_LIMIT)


def _sigmoid(x):
    return 1.0 / (1.0 + jnp.exp(-x))


def _silu(x):
    return x * _sigmoid(x)


def _rms(x, g):
    ms = jnp.mean(x * x, axis=-1, keepdims=True)
    return x * lax.rsqrt(ms + EPS) * g


def _acc(ref, val, first):
    @pl.when(first)
    def _():
        ref[...] = val

    @pl.when(jnp.logical_not(first))
    def _():
        ref[...] += val


ANY = pl.BlockSpec(memory_space=pl.ANY)


def inproj_fwd(x, g, wt, l, dep=None):
    tn = 256

    def body(x_ref, g_ref, w_ref, *rest):
        proj_ref, h_ref = rest[-2:]

        @pl.when(pl.program_id(0) == 0)
        def _():
            h_ref[...] = _rms(x_ref[...], g_ref[...]).astype(BF16)

        proj_ref[...] = lax.dot_general(h_ref[...], w_ref[...].astype(BF16), _NT, preferred_element_type=F32)

    deps = [] if dep is None else [dep]
    return pl.pallas_call(
        body, grid=(NPAD // tn,),
        in_specs=[pl.BlockSpec((T, D), lambda j: (0, 0)), pl.BlockSpec((1, D), lambda j: (0, 0)),
                  pl.BlockSpec((tn, D), lambda j: (j, 0))] + [ANY] * len(deps),
        out_specs=[pl.BlockSpec((T, tn), lambda j: (0, j)), pl.BlockSpec((T, D), lambda j: (0, 0))],
        out_shape=[jax.ShapeDtypeStruct((T, NPAD), F32), jax.ShapeDtypeStruct((T, D), BF16)],
        name=f"inproj_fwd_l{l}", compiler_params=_params(("arbitrary",)))(x, g, wt, *deps)


def inproj_bwd(dproj, h, wt, l):
    tn = 256

    def body(dp_ref, h_ref, w_ref, dwt_ref, dh_ref):
        dp = dp_ref[...]
        dwt_ref[...] = lax.dot_general(dp, h_ref[...], _TN, preferred_element_type=F32)
        contrib = lax.dot_general(dp, w_ref[...].astype(BF16), _NN, preferred_element_type=F32)
        _acc(dh_ref, contrib, pl.program_id(0) == 0)

    return pl.pallas_call(
        body, grid=(NPAD // tn,),
        in_specs=[pl.BlockSpec((T, tn), lambda j: (0, j)), pl.BlockSpec((T, D), lambda j: (0, 0)),
                  pl.BlockSpec((tn, D), lambda j: (j, 0))],
        out_specs=[pl.BlockSpec((None, tn, D), lambda j: (0, j, 0)), pl.BlockSpec((T, D), lambda j: (0, 0))],
        out_shape=[jax.ShapeDtypeStruct((1, NPAD, D), F32), jax.ShapeDtypeStruct((T, D), F32)],
        name=f"inproj_bwd_l{l}", compiler_params=_params(("arbitrary",)))(dproj, h, wt)


def prenorm_bwd(x, g, dh, dxn, l):
    tm = 256

    def body(x_ref, g_ref, dh_ref, dxn_ref, dx_ref, dg_ref):
        _, vjp = jax.vjp(_rms, x_ref[...], g_ref[...])
        dx, dg = vjp(dh_ref[...])
        dx_ref[...] = dx + dxn_ref[...]
        _acc(dg_ref, dg, pl.program_id(0) == 0)

    tok = pl.BlockSpec((tm, D), lambda i: (i, 0))
    vec = pl.BlockSpec((1, D), lambda i: (0, 0))
    return pl.pallas_call(
        body, grid=(T // tm,), in_specs=[tok, vec, tok, tok], out_specs=[tok, vec],
        out_shape=[jax.ShapeDtypeStruct((T, D), F32), jax.ShapeDtypeStruct((1, D), F32)],
        name=f"prenorm_bwd_l{l}", compiler_params=_params(("arbitrary",)))(x, g, dh, dxn)


def _gmlp_tile(u, v, z, ln_g, ln_b, ws, bs):
    mu = jnp.mean(v, axis=-1, keepdims=True)
    vc = v - mu
    var = jnp.mean(vc * vc, axis=-1, keepdims=True)
    vn = vc * lax.rsqrt(var + EPS) * ln_g + ln_b
    qi = lax.broadcasted_iota(jnp.int32, (128, 128), 0) >> CHUNK_SHIFT
    kj = lax.broadcasted_iota(jnp.int32, (128, 128), 1) >> CHUNK_SHIFT
    mask = kj <= qi
    outs = []
    for g in range(4):
        wm = jnp.where(mask, ws[g], 0.0)
        outs.append(dot_nn(wm, vn[:, 256 * g:256 * (g + 1)]) + bs[g])
    sv = jnp.concatenate(outs, axis=1)
    return u * sv * _silu(z)


def _gmlp_specs():
    blk = lambda c: pl.BlockSpec((128, 1024), lambda n, c=c: (n, c))
    vec = pl.BlockSpec((1, 1024), lambda n: (0, 0))
    return [blk(0), blk(1), blk(2), vec, vec,
            pl.BlockSpec((4, 128, 128), lambda n: (0, 0, 0)), pl.BlockSpec((4, 128, 1), lambda n: (0, 0, 0))]


def gmlp_fwd(proj, ln_g, ln_b, ws, bs, l):
    def body(u_ref, v_ref, z_ref, g_ref, b_ref, ws_ref, bs_ref, y_ref):
        y_ref[...] = _gmlp_tile(u_ref[...], v_ref[...], z_ref[...], g_ref[...], b_ref[...],
                                [ws_ref[g] for g in range(4)], [bs_ref[g] for g in range(4)])

    return pl.pallas_call(
        body, grid=(T // 128,), in_specs=_gmlp_specs(),
        out_specs=pl.BlockSpec((128, 1024), lambda n: (n, 0)),
        out_shape=jax.ShapeDtypeStruct((T, 1024), F32),
        name=f"gmlp_fwd_l{l}", compiler_params=_params(("arbitrary",)))(proj, proj, proj, ln_g, ln_b, ws, bs)


def gmlp_bwd(proj, ln_g, ln_b, ws, bs, dy, l):
    def body(u_ref, v_ref, z_ref, g_ref, b_ref, ws_ref, bs_ref, dy_ref, dseg_ref, dg_ref, db_ref, dws_ref, dbs_ref):
        first = pl.program_id(0) == 0
        _, vjp = jax.vjp(_gmlp_tile, u_ref[...], v_ref[...], z_ref[...], g_ref[...], b_ref[...],
                         [ws_ref[g] for g in range(4)], [bs_ref[g] for g in range(4)])
        du, dv, dz, dg, db, dws, dbs = vjp(dy_ref[...])
        dseg_ref[:, 0:1024] = du.astype(BF16)
        dseg_ref[:, 1024:2048] = dv.astype(BF16)
        dseg_ref[:, 2048:3072] = dz.astype(BF16)
        _acc(dg_ref, dg, first)
        _acc(db_ref, db, first)
        for g in range(4):
            _acc(dws_ref.at[g], dws[g], first)
            _acc(dbs_ref.at[g], dbs[g], first)

    vec = pl.BlockSpec((1, 1024), lambda n: (0, 0))
    return pl.pallas_call(
        body, grid=(T // 128,), in_specs=_gmlp_specs() + [pl.BlockSpec((128, 1024), lambda n: (n, 0))],
        out_specs=[pl.BlockSpec((128, 3072), lambda n: (n, 0)), vec, vec,
                   pl.BlockSpec((4, 128, 128), lambda n: (0, 0, 0)), pl.BlockSpec((4, 128, 1), lambda n: (0, 0, 0))],
        out_shape=[jax.ShapeDtypeStruct((T, 3072), BF16), jax.ShapeDtypeStruct((1, 1024), F32),
                   jax.ShapeDtypeStruct((1, 1024), F32), jax.ShapeDtypeStruct((4, 128, 128), F32),
                   jax.ShapeDtypeStruct((4, 128, 1), F32)],
        name=f"gmlp_bwd_l{l}", compiler_params=_params(("arbitrary",)))(proj, proj, proj, ln_g, ln_b, ws, bs, dy)


QKV_TM = 256


def _qkv_tile(cq, ckvr, qg, kvg, wq, wkv, ctab, stab):
    tm = cq.shape[0]
    cqn = _rms(cq, qg)
    lane = lax.broadcasted_iota(jnp.int32, ckvr.shape, 1)
    iskv = lane < 256
    ms = jnp.sum(jnp.where(iskv, ckvr * ckvr, 0.0), axis=-1, keepdims=True) * (1.0 / 256)
    lm = jnp.where(iskv, ckvr * lax.rsqrt(ms + EPS) * kvg, ckvr)
    r = lax.broadcasted_iota(jnp.int32, (64, 128), 0)
    c = lax.broadcasted_iota(jnp.int32, (64, 128), 1)
    eye = jnp.where(c == r, 1.0, 0.0)
    eye_sw = jnp.where(c == ((r + 32) & 63), 1.0, 0.0)
    z64 = jnp.zeros((64, 256), F32)
    z128 = jnp.zeros((128, 128), F32)
    rk_rope = jnp.concatenate([z64, eye], axis=1)
    rk_sw = jnp.concatenate([jnp.zeros((128, 384), F32), jnp.concatenate([z64, eye_sw], axis=1)], axis=0)
    k_sw = dot_nt(lm, rk_sw) * stab
    qs, ks, vs = [], [], []
    for h in range(HEADS):
        wn, w1, w2 = wq[h]
        wk, wv = wkv[h]
        wq_h = jnp.concatenate([wn, w1, w2], axis=0)
        wq_sw = jnp.concatenate([jnp.zeros((128, 384), F32), w2, w1], axis=0)
        qs.append(dot_nt(cqn, wq_h) * ctab + dot_nt(cqn, wq_sw) * stab)
        rk_h = jnp.concatenate([jnp.concatenate([wk, z128], axis=1), rk_rope], axis=0)
        ks.append(dot_nt(lm, rk_h) * ctab + k_sw)
        vs.append(dot_nt(lm, jnp.concatenate([wv, z128], axis=1)))
    return qs, ks, vs


def _qkv_in_specs():
    tm = QKV_TM
    return [pl.BlockSpec((tm, 384), lambda i: (i, OFF_CQ // 384)), pl.BlockSpec((tm, 384), lambda i: (i, OFF_CKV // 384)),
            pl.BlockSpec((1, 384), lambda i: (0, 0)), pl.BlockSpec((1, 384), lambda i: (0, 0)),
            pl.BlockSpec((HEADS, 192, 384), lambda i: (0, 0, 0)), pl.BlockSpec((HEADS, 256, 256), lambda i: (0, 0, 0)),
            pl.BlockSpec((tm, 192), lambda i: (i, 0)), pl.BlockSpec((tm, 192), lambda i: (i, 0))]


def _qkv_weights(wq_ref, wkv_ref):
    wq = [(wq_ref[h, 0:128, :], wq_ref[h, 128:160, :], wq_ref[h, 160:192, :]) for h in range(HEADS)]
    wkv = [(wkv_ref[h, 0:128, :].astype(F32), wkv_ref[h, 128:256, :].astype(F32)) for h in range(HEADS)]
    return wq, wkv


def qkv_fwd(proj, qg, kvg, wq, wkv, ctab, stab, l):
    tm = QKV_TM

    def body(cq_ref, ckvr_ref, qg_ref, kvg_ref, wq_ref, wkv_ref, c_ref, s_ref, q_ref, k_ref, v_ref):
        wq_l, wkv_l = _qkv_weights(wq_ref, wkv_ref)
        qs, ks, vs = _qkv_tile(cq_ref[...], ckvr_ref[...], qg_ref[...], kvg_ref[...], wq_l, wkv_l, c_ref[...], s_ref[...])
        for h in range(HEADS):
            q_ref[h] = qs[h]
            k_ref[h] = ks[h]
            v_ref[h] = vs[h]

    return pl.pallas_call(
        body, grid=(T // tm,), in_specs=_qkv_in_specs(),
        out_specs=[pl.BlockSpec((HEADS, tm, QK), lambda i: (0, i, 0)), pl.BlockSpec((HEADS, tm, QK), lambda i: (0, i, 0)),
                   pl.BlockSpec((HEADS, tm, 128), lambda i: (0, i, 0))],
        out_shape=[jax.ShapeDtypeStruct((HEADS, T, QK), F32), jax.ShapeDtypeStruct((HEADS, T, QK), F32),
                   jax.ShapeDtypeStruct((HEADS, T, 128), F32)],
        name=f"qkv_fwd_l{l}", compiler_params=_params(("arbitrary",)))(proj, proj, qg, kvg, wq, wkv, ctab, stab)


def qkv_bwd(proj, qg, kvg, wq, wkv, ctab, stab, dq, dk, dv, l):
    tm = QKV_TM

    def body(cq_ref, ckvr_ref, qg_ref, kvg_ref, wq_ref, wkv_ref, c_ref, s_ref, dq_ref, dk_ref, dv_ref,
             dseg_ref, dqg_ref, dkvg_ref, dwq_ref, dwkv_ref):
        first = pl.program_id(0) == 0
        wq_l, wkv_l = _qkv_weights(wq_ref, wkv_ref)
        c_tab, s_tab = c_ref[...], s_ref[...]
        fn = lambda cq, ckvr, qg_, kvg_, wq_, wkv_: _qkv_tile(cq, ckvr, qg_, kvg_, wq_, wkv_, c_tab, s_tab)
        _, vjp = jax.vjp(fn, cq_ref[...], ckvr_ref[...], qg_ref[...], kvg_ref[...], wq_l, wkv_l)
        cts = ([dq_ref[h] for h in range(HEADS)], [dk_ref[h] for h in range(HEADS)], [dv_ref[h] for h in range(HEADS)])
        dcq, dckvr, dqg, dkvg, dwq, dwkv = vjp(cts)
        dseg_ref[:, 0:384] = dcq.astype(BF16)
        dseg_ref[:, 384:768] = dckvr.astype(BF16)
        _acc(dqg_ref, dqg, first)
        _acc(dkvg_ref, dkvg, first)
        for h in range(HEADS):
            _acc(dwq_ref.at[h, 0:128, :], dwq[h][0], first)
            _acc(dwq_ref.at[h, 128:160, :], dwq[h][1], first)
            _acc(dwq_ref.at[h, 160:192, :], dwq[h][2], first)
            _acc(dwkv_ref.at[h, 0:128, :], dwkv[h][0], first)
            _acc(dwkv_ref.at[h, 128:256, :], dwkv[h][1], first)

    hq = pl.BlockSpec((HEADS, tm, QK), lambda i: (0, i, 0))
    return pl.pallas_call(
        body, grid=(T // tm,),
        in_specs=_qkv_in_specs() + [hq, hq, pl.BlockSpec((HEADS, tm, 128), lambda i: (0, i, 0))],
        out_specs=[pl.BlockSpec((tm, 768), lambda i: (i, 0)), pl.BlockSpec((1, 384), lambda i: (0, 0)),
                   pl.BlockSpec((1, 384), lambda i: (0, 0)), pl.BlockSpec((HEADS, 192, 384), lambda i: (0, 0, 0)),
                   pl.BlockSpec((HEADS, 256, 256), lambda i: (0, 0, 0))],
        out_shape=[jax.ShapeDtypeStruct((T, 768), BF16), jax.ShapeDtypeStruct((1, 384), F32),
                   jax.ShapeDtypeStruct((1, 384), F32), jax.ShapeDtypeStruct((HEADS, 192, 384), F32),
                   jax.ShapeDtypeStruct((HEADS, 256, 256), F32)],
        name=f"qkv_bwd_l{l}", compiler_params=_params(("arbitrary",)))(
            proj, proj, qg, kvg, wq, wkv, ctab, stab, dq, dk, dv)


ATT_TQ = 256


def _attn_tile(q, k, v, zb, q0):
    s = dot_nt(q, k) * (1.0 / math.sqrt(QK))
    qc = (q0 + lax.broadcasted_iota(jnp.int32, s.shape, 0)) >> CHUNK_SHIFT
    kc = lax.broadcasted_iota(jnp.int32, s.shape, 1) >> CHUNK_SHIFT
    s = jnp.where(kc <= qc, s, -1e30)
    m = lax.stop_gradient(jnp.max(s, axis=-1, keepdims=True))
    p = jnp.exp(s - m)
    p = p / jnp.sum(p, axis=-1, keepdims=True)
    return dot_nn(p, v) * _silu(zb)


def _attn_in_specs():
    tq = ATT_TQ
    return [pl.BlockSpec((None, tq, QK), lambda h, i: (h, i, 0)), pl.BlockSpec((None, T, QK), lambda h, i: (h, 0, 0)),
            pl.BlockSpec((None, T, 128), lambda h, i: (h, 0, 0)),
            pl.BlockSpec((tq, 128), lambda h, i: (i, OFF_ZB // 128 + h))]


def attn_fwd(q, k, v, proj, l):
    tq = ATT_TQ

    def body(q_ref, k_ref, v_ref, z_ref, y_ref):
        for g in range(T // tq):
            @pl.when(pl.program_id(1) == g)
            def _(g=g):
                kmax = tq * (g + 1)
                y_ref[...] = _attn_tile(q_ref[...], k_ref[0:kmax, :], v_ref[0:kmax, :], z_ref[...], g * tq)

    return pl.pallas_call(
        body, grid=(HEADS, T // tq), in_specs=_attn_in_specs(),
        out_specs=pl.BlockSpec((tq, 128), lambda h, i: (i, h)),
        out_shape=jax.ShapeDtypeStruct((T, 1024), F32),
        name=f"attn_fwd_l{l}", compiler_params=_params(("arbitrary", "arbitrary")))(q, k, v, proj)


def attn_bwd(q, k, v, proj, dy, l):
    tq = ATT_TQ

    def body(q_ref, k_ref, v_ref, z_ref, dy_ref, dq_ref, dk_ref, dv_ref, dz_ref):
        @pl.when(pl.program_id(1) == 0)
        def _():
            dk_ref[...] = jnp.zeros_like(dk_ref)
            dv_ref[...] = jnp.zeros_like(dv_ref)

        for g in range(T // tq):
            @pl.when(pl.program_id(1) == g)
            def _(g=g):
                kmax = tq * (g + 1)
                fn = lambda q_, k_, v_, z_: _attn_tile(q_, k_, v_, z_, g * tq)
                _, vjp = jax.vjp(fn, q_ref[...], k_ref[0:kmax, :], v_ref[0:kmax, :], z_ref[...])
                dq, dk, dv, dz = vjp(dy_ref[...])
                dq_ref[...] = dq
                dz_ref[...] = dz.astype(BF16)
                dk_ref[0:kmax, :] += dk
                dv_ref[0:kmax, :] += dv

    return pl.pallas_call(
        body, grid=(HEADS, T // tq),
        in_specs=_attn_in_specs() + [pl.BlockSpec((tq, 128), lambda h, i: (i, h))],
        out_specs=[pl.BlockSpec((None, tq, QK), lambda h, i: (h, i, 0)), pl.BlockSpec((None, T, QK), lambda h, i: (h, 0, 0)),
                   pl.BlockSpec((None, T, 128), lambda h, i: (h, 0, 0)), pl.BlockSpec((tq, 128), lambda h, i: (i, h))],
        out_shape=[jax.ShapeDtypeStruct((HEADS, T, QK), F32), jax.ShapeDtypeStruct((HEADS, T, QK), F32),
                   jax.ShapeDtypeStruct((HEADS, T, 128), F32), jax.ShapeDtypeStruct((T, 1024), BF16)],
        name=f"attn_bwd_l{l}", compiler_params=_params(("arbitrary", "arbitrary")))(q, k, v, proj, dy)


LRU_TT = 256


def _lru_gates(xc, wa, wx, ba, bx, lam):
    r = _sigmoid(dot_nn(xc, wa) + ba)
    i = _sigmoid(dot_nn(xc, wx) + bx)
    sp = jnp.maximum(-lam, 0.0) + jnp.log1p(jnp.exp(-jnp.abs(lam)))
    log_a = -8.0 * r * sp
    a = jnp.exp(log_a)
    mult = jnp.sqrt(jnp.maximum(1.0 - jnp.exp(2.0 * log_a), 0.0))
    return a, mult * (i * xc)


def _shift_down(x, s, halo):
    xs = pltpu.roll(x, s, 0)
    row = lax.broadcasted_iota(jnp.int32, halo.shape, 0)
    top = jnp.where(row < s, pltpu.roll(halo, s, 0), xs[0:8])
    return jnp.concatenate([top, xs[8:]], axis=0)


def _shift_up(x, s, halo):
    n = x.shape[0]
    xs = pltpu.roll(x, n - s, 0)
    row = lax.broadcasted_iota(jnp.int32, halo.shape, 0)
    bot = jnp.where(row >= 8 - s, pltpu.roll(halo, 8 - s, 0), xs[n - 8:n])
    return jnp.concatenate([xs[:n - 8], bot], axis=0)


def _conv(x, halo, w_ref, b):
    return (w_ref[3:4, :] * x + w_ref[2:3, :] * _shift_down(x, 1, halo) + w_ref[1:2, :] * _shift_down(x, 2, halo)
            + w_ref[0:1, :] * _shift_down(x, 3, halo) + b)


def _scan(a, b, reverse):
    n = a.shape[0]
    row = lax.broadcasted_iota(jnp.int32, a.shape, 0)
    d = 1
    while d < n:
        if reverse:
            keep = row < n - d
            a_sh = jnp.where(keep, pltpu.roll(a, n - d, 0), 1.0)
            b_sh = jnp.where(keep, pltpu.roll(b, n - d, 0), 0.0)
        else:
            keep = row >= d
            a_sh = jnp.where(keep, pltpu.roll(a, d, 0), 1.0)
            b_sh = jnp.where(keep, pltpu.roll(b, d, 0), 0.0)
        b = a * b_sh + b
        a = a * a_sh
        d *= 2
    return a, b


def _lru_param_specs(time_map):
    ct = LRU_TILE
    vec = pl.BlockSpec((1, ct), lambda n, i: (0, n))
    return [pl.BlockSpec((4, ct), lambda n, i: (0, n)), vec,
            pl.BlockSpec((None, ct, ct), lambda n, i: (n, 0, 0)), pl.BlockSpec((None, ct, ct), lambda n, i: (n, 0, 0)),
            vec, vec, vec]


def lru_fwd(proj, conv_w, conv_b, wa, wx, ba, bx, lam, l):
    tt, ct = LRU_TT, LRU_TILE

    def body(x_ref, z_ref, cw_ref, cb_ref, wa_ref, wx_ref, ba_ref, bx_ref, lam_ref, h_ref, y_ref, halo, hcar):
        @pl.when(pl.program_id(1) == 0)
        def _():
            halo[...] = jnp.zeros_like(halo)
            hcar[...] = jnp.zeros_like(hcar)

        x = x_ref[...]
        xc = _conv(x, halo[...], cw_ref, cb_ref[...])
        halo[...] = x[tt - 8:tt]
        a, b = _lru_gates(xc, wa_ref[...], wx_ref[...], ba_ref[...], bx_ref[...], lam_ref[...])
        a_cum, b_cum = _scan(a, b, False)
        h = a_cum * hcar[...] + b_cum
        h_ref[...] = h
        hcar[...] = h_ref[tt - 1:tt, :]
        y_ref[...] = h * _silu(z_ref[...])

    seq = pl.BlockSpec((tt, ct), lambda n, i: (i, n))
    return pl.pallas_call(
        body, grid=(LRU_W // ct, T // tt),
        in_specs=[pl.BlockSpec((tt, ct), lambda n, i: (i, OFF_XC // ct + n)),
                  pl.BlockSpec((tt, ct), lambda n, i: (i, OFF_ZC // ct + n))] + _lru_param_specs(None),
        out_specs=[seq, seq],
        out_shape=[jax.ShapeDtypeStruct((T, LRU_W), F32), jax.ShapeDtypeStruct((T, LRU_W), F32)],
        scratch_shapes=[pltpu.VMEM((8, ct), F32), pltpu.VMEM((1, ct), F32)],
        name=f"lru_fwd_l{l}", compiler_params=_params(("arbitrary", "arbitrary")))(
            proj, proj, conv_w, conv_b, wa, wx, ba, bx, lam)


def lru_bwd(proj, hseq, dy, conv_w, conv_b, wa, wx, ba, bx, lam, l):
    tt, ct = LRU_TT, LRU_TILE
    nt = T // tt
    rev = lambda i: nt - 1 - i
    prev8 = lambda i: jnp.maximum(rev(i) * (tt // 8) - 1, 0)

    def body(x_ref, xh_ref, z_ref, h_ref, hh_ref, dy_ref, cw_ref, cb_ref, wa_ref, wx_ref, ba_ref, bx_ref, lam_ref,
             dx_ref, dz_ref, dcw_ref, dcb_ref, dwa_ref, dwx_ref, dba_ref, dbx_ref, dlam_ref, gcar, dhalo):
        i = pl.program_id(1)
        first = i == 0

        @pl.when(first)
        def _():
            gcar[...] = jnp.zeros_like(gcar)
            dhalo[...] = jnp.zeros_like(dhalo)

        at_start = rev(i) == 0
        x = x_ref[...]
        xhalo = jnp.where(at_start, 0.0, xh_ref[...])
        sh = [x, _shift_down(x, 1, xhalo), _shift_down(x, 2, xhalo), _shift_down(x, 3, xhalo)]
        xc = (cw_ref[3:4, :] * sh[0] + cw_ref[2:3, :] * sh[1] + cw_ref[1:2, :] * sh[2] + cw_ref[0:1, :] * sh[3]
              + cb_ref[...])
        (a, b), vjp = jax.vjp(_lru_gates, xc, wa_ref[...], wx_ref[...], ba_ref[...], bx_ref[...], lam_ref[...])
        hs = h_ref[...]
        hprev = _shift_down(hs, 1, jnp.where(at_start, 0.0, hh_ref[...]))
        z = z_ref[...]
        sg = _sigmoid(z)
        dy = dy_ref[...]
        dz_ref[...] = (dy * hs * (sg * (1.0 + z * (1.0 - sg)))).astype(BF16)
        dh = dy * (z * sg)
        row = lax.broadcasted_iota(jnp.int32, a.shape, 0)
        a_next = jnp.where(row < tt - 1, pltpu.roll(a, tt - 1, 0), 1.0)
        a_cum, b_cum = _scan(a_next, dh, True)
        g = a_cum * gcar[...] + b_cum
        dxc, dwa, dwx, dba, dbx, dlam = vjp((g * hprev, g))
        dx = (cw_ref[3:4, :] * dxc + cw_ref[2:3, :] * _shift_up(dxc, 1, dhalo[...])
              + cw_ref[1:2, :] * _shift_up(dxc, 2, dhalo[...]) + cw_ref[0:1, :] * _shift_up(dxc, 3, dhalo[...]))
        dx_ref[...] = dx.astype(BF16)
        dhalo[...] = dxc[0:8]
        ag = a * g
        gcar[...] = ag[0:1]
        dcw = jnp.concatenate([jnp.sum(dxc * sh[3 - j], axis=0, keepdims=True) for j in range(4)], axis=0)
        _acc(dcw_ref, dcw, first)
        _acc(dcb_ref, jnp.sum(dxc, axis=0, keepdims=True), first)
        _acc(dwa_ref, dwa, first)
        _acc(dwx_ref, dwx, first)
        _acc(dba_ref, dba, first)
        _acc(dbx_ref, dbx, first)
        _acc(dlam_ref, dlam, first)

    xcol = OFF_XC // ct
    zcol = OFF_ZC // ct
    vec = pl.BlockSpec((1, ct), lambda n, i: (0, n))
    mat = pl.BlockSpec((None, ct, ct), lambda n, i: (n, 0, 0))
    seq = pl.BlockSpec((tt, ct), lambda n, i: (rev(i), n))
    return pl.pallas_call(
        body, grid=(LRU_W // ct, nt),
        in_specs=[pl.BlockSpec((tt, ct), lambda n, i: (rev(i), xcol + n)),
                  pl.BlockSpec((8, ct), lambda n, i: (prev8(i), xcol + n)),
                  pl.BlockSpec((tt, ct), lambda n, i: (rev(i), zcol + n)),
                  seq, pl.BlockSpec((8, ct), lambda n, i: (prev8(i), n)), seq] + _lru_param_specs(None),
        out_specs=[seq, seq, pl.BlockSpec((4, ct), lambda n, i: (0, n)), vec, mat, mat, vec, vec, vec],
        out_shape=[jax.ShapeDtypeStruct((T, LRU_W), BF16), jax.ShapeDtypeStruct((T, LRU_W), BF16),
                   jax.ShapeDtypeStruct((4, LRU_W), F32), jax.ShapeDtypeStruct((1, LRU_W), F32),
                   jax.ShapeDtypeStruct((2, ct, ct), F32), jax.ShapeDtypeStruct((2, ct, ct), F32),
                   jax.ShapeDtypeStruct((1, LRU_W), F32), jax.ShapeDtypeStruct((1, LRU_W), F32),
                   jax.ShapeDtypeStruct((1, LRU_W), F32)],
        scratch_shapes=[pltpu.VMEM((1, ct), F32), pltpu.VMEM((8, ct), F32)],
        name=f"lru_bwd_l{l}", compiler_params=_params(("arbitrary", "arbitrary")))(
            proj, proj, proj, hseq, hseq, dy, conv_w, conv_b, wa, wx, ba, bx, lam)


def proj_fwd(y, w, l, tag):
    tm = 512
    k = y.shape[1]

    def body(y_ref, w_ref, o_ref):
        o_ref[...] = _dg(y_ref[...], w_ref[...], _NN)

    return pl.pallas_call(
        body, grid=(T // tm,),
        in_specs=[pl.BlockSpec((tm, k), lambda i: (i, 0)), pl.BlockSpec((k, D), lambda i: (0, 0))],
        out_specs=pl.BlockSpec((tm, D), lambda i: (i, 0)), out_shape=jax.ShapeDtypeStruct((T, D), F32),
        name=f"proj_{tag}_fwd_l{l}", compiler_params=_params(("arbitrary",)))(y, w)


def proj_bwd(y, dp, w, l, tag):
    tm = 512
    k = y.shape[1]

    def body(y_ref, dp_ref, w_ref, dy_ref, dw_ref):
        dp = dp_ref[...]
        dy_ref[...] = _dg(dp, w_ref[...], _NT)
        _acc(dw_ref, _dg(y_ref[...], dp, _TN), pl.program_id(0) == 0)

    return pl.pallas_call(
        body, grid=(T // tm,),
        in_specs=[pl.BlockSpec((tm, k), lambda i: (i, 0)), pl.BlockSpec((tm, D), lambda i: (i, 0)),
                  pl.BlockSpec((k, D), lambda i: (0, 0))],
        out_specs=[pl.BlockSpec((tm, k), lambda i: (i, 0)), pl.BlockSpec((None, k, D), lambda i: (0, 0, 0))],
        out_shape=[jax.ShapeDtypeStruct((T, k), F32), jax.ShapeDtypeStruct((1, k, D), F32)],
        name=f"proj_{tag}_bwd_l{l}", compiler_params=_params(("arbitrary",)))(y, dp, w)


OUT_TM = 256


def _out_tile(pa, pb, pc, ga, gb, gc, wout, post_g):
    merged = _sigmoid(ga) * pa + _sigmoid(gb) * pb + _sigmoid(gc) * pc
    return _rms(dot_nn(merged, wout), post_g)


def _out_in_specs():
    tm = OUT_TM
    tok = pl.BlockSpec((tm, D), lambda i: (i, 0))
    gate = lambda off: pl.BlockSpec((tm, 512), lambda i, off=off: (i, off // 512))
    return [tok, tok, tok, gate(OFF_GA), gate(OFF_GA + 512), gate(OFF_GB), gate(OFF_GB + 512), gate(OFF_GC),
            gate(OFF_GC + 512), pl.BlockSpec((D, D), lambda i: (0, 0)), pl.BlockSpec((1, D), lambda i: (0, 0))]


def _gates(refs):
    return [jnp.concatenate([refs[2 * j][...], refs[2 * j + 1][...]], axis=1) for j in range(3)]


def out_fwd(x, pa, pb, pc, proj, wout, post_g, l):
    tm = OUT_TM

    def body(pa_ref, pb_ref, pc_ref, g0, g1, g2, g3, g4, g5, w_ref, pg_ref, x_ref, o_ref):
        ga, gb, gc = _gates([g0, g1, g2, g3, g4, g5])
        o_ref[...] = x_ref[...] + _out_tile(pa_ref[...], pb_ref[...], pc_ref[...], ga, gb, gc, w_ref[...], pg_ref[...])

    tok = pl.BlockSpec((tm, D), lambda i: (i, 0))
    return pl.pallas_call(
        body, grid=(T // tm,), in_specs=_out_in_specs() + [tok], out_specs=tok,
        out_shape=jax.ShapeDtypeStruct((T, D), F32),
        name=f"out_fwd_l{l}", compiler_params=_params(("arbitrary",)))(
            pa, pb, pc, proj, proj, proj, proj, proj, proj, wout, post_g, x)


def out_bwd(pa, pb, pc, proj, wout, post_g, dxn, l, dep=None):
    tm = OUT_TM

    def body(pa_ref, pb_ref, pc_ref, g0, g1, g2, g3, g4, g5, w_ref, pg_ref, dxn_ref, *rest):
        dpa_ref, dpb_ref, dpc_ref, dg_ref, dw_ref, dpg_ref = rest[-6:]
        first = pl.program_id(0) == 0
        ga, gb, gc = _gates([g0, g1, g2, g3, g4, g5])
        _, vjp = jax.vjp(_out_tile, pa_ref[...], pb_ref[...], pc_ref[...], ga, gb, gc, w_ref[...].astype(F32), pg_ref[...])
        dpa, dpb, dpc, dga, dgb, dgc, dw, dpg = vjp(dxn_ref[...])
        dpa_ref[...] = dpa.astype(BF16)
        dpb_ref[...] = dpb.astype(BF16)
        dpc_ref[...] = dpc.astype(BF16)
        dg_ref[:, 0:1024] = dga.astype(BF16)
        dg_ref[:, 1024:2048] = dgb.astype(BF16)
        dg_ref[:, 2048:3072] = dgc.astype(BF16)
        _acc(dw_ref, dw, first)
        _acc(dpg_ref, dpg, first)

    tok = pl.BlockSpec((tm, D), lambda i: (i, 0))
    deps = [] if dep is None else [dep]
    return pl.pallas_call(
        body, grid=(T // tm,), in_specs=_out_in_specs() + [tok] + [ANY] * len(deps),
        out_specs=[tok, tok, tok, pl.BlockSpec((tm, 3072), lambda i: (i, 0)),
                   pl.BlockSpec((None, D, D), lambda i: (0, 0, 0)), pl.BlockSpec((1, D), lambda i: (0, 0))],
        out_shape=[jax.ShapeDtypeStruct((T, D), BF16)] * 3 + [jax.ShapeDtypeStruct((T, 3072), BF16),
                                                            jax.ShapeDtypeStruct((1, D, D), F32), jax.ShapeDtypeStruct((1, D), F32)],
        name=f"out_bwd_l{l}", compiler_params=_params(("arbitrary",)))(
            pa, pb, pc, proj, proj, proj, proj, proj, proj, wout, post_g, dxn, *deps)


def loss_head(y, target):
    tm = 256

    def body(y_ref, t_ref, loss_ref, dy_ref):
        e = y_ref[...] - t_ref[...]
        dy_ref[...] = e * (1.0 / D)
        val = 0.5 * jnp.sum(jnp.mean(e * e, axis=-1, keepdims=True), axis=0, keepdims=True)
        _acc(loss_ref, jnp.broadcast_to(val, (8, 128)), pl.program_id(0) == 0)

    tok = pl.BlockSpec((tm, D), lambda i: (i, 0))
    total, dy = pl.pallas_call(
        body, grid=(T // tm,), in_specs=[tok, tok],
        out_specs=[pl.BlockSpec((8, 128), lambda i: (0, 0)), tok],
        out_shape=[jax.ShapeDtypeStruct((8, 128), F32), jax.ShapeDtypeStruct((T, D), F32)],
        name="loss_head", compiler_params=_params(("arbitrary",)))(y, target)
    return total[0, 0], dy


def _rope_tables():
    pos = jnp.arange(T, dtype=F32)
    inv_freq = 10000.0 ** (-jnp.arange(0, 64, 2, dtype=F32) / 64)
    ang = pos[:, None] * inv_freq[None, :]
    cos, sin = jnp.cos(ang), jnp.sin(ang)
    ctab = jnp.concatenate([jnp.ones((T, 128), F32), cos, cos], axis=1)
    stab = jnp.concatenate([jnp.zeros((T, 128), F32), -sin, sin], axis=1)
    return ctab, stab


def _block_diag(w):
    w5 = w.reshape(L, 2, 8, 80, 80)
    eye = jnp.eye(8, dtype=w.dtype)
    return jnp.einsum("lnbij,bc->lnbicj", w5, eye).reshape(L, 2, LRU_TILE, LRU_TILE)


def _block_diag_t(dw):
    dw5 = dw.reshape(2, 8, 80, 8, 80)
    return jnp.einsum("nbicj,bc->nbij", dw5, jnp.eye(8, dtype=dw.dtype)).reshape(16, 80, 80)


def _layer_fwd(x, l, w, gw, tabs, dep=None):
    row = lambda a: a[l][None]
    proj, h = inproj_fwd(x, row(w["pre_norm_g"]), gw["w_in_t"], l, dep)
    ya = gmlp_fwd(proj, row(w["gm_ln_g"]), row(w["gm_ln_b"]), w["gm_ws"][l], w["gm_bs"][l][..., None], l)
    q, k, v = qkv_fwd(proj, row(w["mla_q_norm_g"]), row(w["kv_g384"]), gw["wq"], gw["wkv"], tabs[0], tabs[1], l)
    yb = attn_fwd(q, k, v, proj, l)
    hseq, yc = lru_fwd(proj, gw["conv"], row(w["lru_conv_b"]), w["wa_dense"][l], w["wx_dense"][l],
                       row(w["lru_b_a"]), row(w["lru_b_x"]), row(w["lru_lambda"]), l)
    pa = proj_fwd(ya, gw["w_proj_a"], l, "a")
    pb = proj_fwd(yb, gw["w_proj_b"], l, "b")
    pc = proj_fwd(yc, gw["w_proj_c"], l, "c")
    xn = out_fwd(x, pa, pb, pc, proj, gw["w_out"], row(w["post_norm_g"]), l)
    return xn, (x, proj, h, ya, q, k, v, yb, hseq, yc, pa, pb, pc)


def _layer_bwd(dxn, l, w, gw, tabs, saved, dep=None):
    x, proj, h, ya, q, k, v, yb, hseq, yc, pa, pb, pc = saved
    row = lambda a: a[l][None]
    g, gg = {}, {}
    dpa, dpb, dpc, dgates, gg["w_out"], dpost = out_bwd(pa, pb, pc, proj, gw["w_out"], row(w["post_norm_g"]), dxn, l, dep)
    g["post_norm_g"] = dpost[0]
    dya, gg["w_proj_a"] = proj_bwd(ya, dpa, gw["w_proj_a"], l, "a")
    dyb, gg["w_proj_b"] = proj_bwd(yb, dpb, gw["w_proj_b"], l, "b")
    dyc, gg["w_proj_c"] = proj_bwd(yc, dpc, gw["w_proj_c"], l, "c")
    dseg_a, dln_g, dln_b, g["gm_ws"], dbs = gmlp_bwd(proj, row(w["gm_ln_g"]), row(w["gm_ln_b"]), w["gm_ws"][l],
                                                    w["gm_bs"][l][..., None], dya, l)
    g["gm_ln_g"], g["gm_ln_b"], g["gm_bs"] = dln_g[0], dln_b[0], dbs[..., 0]
    dq, dk, dv, dzb = attn_bwd(q, k, v, proj, dyb, l)
    dseg_q, dqg, dkvg, dwq, dwkv = qkv_bwd(proj, row(w["mla_q_norm_g"]), row(w["kv_g384"]), gw["wq"], gw["wkv"],
                                           tabs[0], tabs[1], dq, dk, dv, l)
    gg["wq"], gg["wkv"] = dwq.reshape(1, 1536, 384), dwkv.reshape(1, 2048, 256)
    g["mla_q_norm_g"], g["mla_kv_norm_g"] = dqg[0], dkvg[0, :256]
    dxc, dzc, dcw, dcb, dwa, dwx, dba, dbx, dlam = lru_bwd(
        proj, hseq, dyc, gw["conv"], row(w["lru_conv_b"]), w["wa_dense"][l], w["wx_dense"][l],
        row(w["lru_b_a"]), row(w["lru_b_x"]), row(w["lru_lambda"]), l)
    gg["conv"] = jnp.pad(dcw.T, ((0, 0), (0, 124)))[None]
    g["lru_conv_b"], g["lru_b_a"], g["lru_b_x"], g["lru_lambda"] = dcb[0], dba[0], dbx[0], dlam[0]
    g["lru_w_a"], g["lru_w_x"] = _block_diag_t(dwa), _block_diag_t(dwx)
    dproj = jnp.concatenate([dseg_a, dseg_q, dzb, jnp.zeros((T, PAD2), dzb.dtype), dxc, dzc, dgates], axis=1)
    gg["w_in_t"], dh = inproj_bwd(dproj, h, gw["w_in_t"], l)
    dx, dpre = prenorm_bwd(x, row(w["pre_norm_g"]), dh, dxn, l)
    g["pre_norm_g"] = dpre[0]
    return dx, gg, g


MESH = pl.DeviceIdType.MESH
HBM = pl.BlockSpec(memory_space=pltpu.HBM)
SEM = pl.BlockSpec(memory_space=pltpu.SEMAPHORE)
EFFECT = pltpu.SideEffectType.DATAFLOW_SIDE_EFFECTING
FLIPS = ((1, 0), (0, 1), (1, 1))


def _win_off(k, s):
    g = SHARD * k + s
    return g + jnp.where(g >= PAD1_AT, PAD1, 0) + jnp.where(g >= PAD2_AT, PAD2, 0)


def _plain_off(rows):
    return lambda k, s: rows * k + s


class Spec:
    def __init__(self, rows, cols, full_rows, pieces=None, off=None, layers=1, packed=None):
        self.rows, self.cols, self.full_rows, self.layers = rows, cols, full_rows, layers
        self.pieces = pieces or ((0, rows),)
        self.off = off or _plain_off(rows)
        self.packed = cols % 256 == 0 if packed is None else packed
        self.wcols = cols // 2 if self.packed else cols

    def to_words(self, a):
        return _pack(a) if self.packed else a

    def from_words(self, p):
        return _unpack(p) if self.packed else p


def _pack(a):
    half = a.shape[-1] // 2
    lo = lax.bitcast_convert_type(a[:, :half].astype(jnp.bfloat16).astype(F32), jnp.uint32)
    hi = lax.bitcast_convert_type(a[:, half:].astype(jnp.bfloat16).astype(F32), jnp.uint32)
    return lax.bitcast_convert_type((lo >> 16) | (hi & jnp.uint32(0xFFFF0000)), F32)


def _unpack(p):
    w = lax.bitcast_convert_type(p, jnp.uint32)
    lo = lax.bitcast_convert_type(w << 16, F32)
    hi = lax.bitcast_convert_type(w & jnp.uint32(0xFFFF0000), F32)
    return jnp.concatenate([lo, hi], axis=-1)


WEIGHT_SPECS = {
    "w_in_t": Spec(SHARD, D, NPAD, WIN_PIECES, _win_off),
    "wq": Spec(192, 384, 1536),
    "wkv": Spec(256, 256, 2048),
    "conv": Spec(160, 128, 1280),
    "w_proj_a": Spec(128, D, 1024),
    "w_proj_b": Spec(128, D, 1024),
    "w_proj_c": Spec(160, D, 1280),
    "w_out": Spec(128, D, 1024),
}
REP_ROWS = 72
REP_SPEC = Spec(REP_ROWS, D, REP_ROWS * NDEV, packed=False)


def _coords():
    return lax.axis_index("x"), lax.axis_index("y"), lax.axis_index("c")


def _rows(ref, start, n):
    if not isinstance(start, int):
        start = pl.multiple_of(start, 8)
    return ref.at[:, pl.ds(start, n), :]


def _col_tile(cols):
    return 256 if cols % 256 == 0 else cols


def _n_pieces(specs):
    return sum(len(sp.pieces) for sp in specs)


def pack_place(shard, sp, layer, tag):
    gaps = ((PAD1_AT, PAD1), (PAD2_AT + PAD1, PAD2)) if sp.off is _win_off else ()
    npc = len(sp.pieces)

    def body(s_ref, words_ref, full_ref, buf, zbuf, sem):
        l = 0
        x, y, c = _coords()
        me = 4 * x + 2 * y + c
        words = sp.to_words(s_ref[...])
        words_ref[...] = words
        buf[...] = words
        copies = [pltpu.make_async_copy(buf.at[pl.ds(s, n), :],
                                        full_ref.at[l, pl.ds(pl.multiple_of(sp.off(me, s), 8), n), :], sem.at[i])
                  for i, (s, n) in enumerate(sp.pieces)]
        if gaps:
            zbuf[...] = jnp.zeros_like(zbuf)
            copies += [pltpu.make_async_copy(zbuf.at[pl.ds(0, n), :], full_ref.at[l, pl.ds(at, n), :], sem.at[npc + i])
                       for i, (at, n) in enumerate(gaps)]
        for cp in copies:
            cp.start()
        for cp in copies:
            cp.wait()

    return pl.pallas_call(
        body, grid=(1,), in_specs=[pl.BlockSpec((None, sp.rows, sp.cols), lambda i: (layer, 0, 0))],
        out_specs=[pl.BlockSpec((None, sp.rows, sp.wcols), lambda i: (0, 0, 0)), ANY],
        out_shape=[jax.ShapeDtypeStruct((sp.layers, sp.rows, sp.wcols), F32),
                   jax.ShapeDtypeStruct((sp.layers, sp.full_rows, sp.wcols), F32)],
        scratch_shapes=[pltpu.VMEM((sp.rows, sp.wcols), F32), pltpu.VMEM((PAD2 if gaps else 8, sp.wcols), F32),
                        pltpu.SemaphoreType.DMA((npc + len(gaps),))],
        name=f"pack_place_{tag}", compiler_params=_params(("arbitrary",)))(shard)


def _gather_copies(srcs, bufs, specs, ssem, rsem, landing):
    x, y, c = _coords()
    me = 4 * x + 2 * y + c
    targets = [(x, y, 1 - c)] + [(x ^ fx, y ^ fy, c) for fx, fy in FLIPS]
    copies = []
    p = 0
    for src, buf, sp in zip(srcs, bufs, specs):
        for s, n in sp.pieces:
            for t, (tx, ty, tc) in enumerate(targets):
                owner = 4 * tx + 2 * ty + tc if landing else me
                copies.append(pltpu.make_async_remote_copy(_rows(src, s, n), _rows(buf, sp.off(owner, s), n),
                                                           ssem.at[4 * p + t], rsem.at[4 * p + t],
                                                           device_id=(tx, ty, tc), device_id_type=MESH))
            p += 1
    return copies


def gather_send(words, fulls, specs, tag):
    ns, npc = len(specs), _n_pieces(specs)

    def body(*refs):
        srcs, bufs, sems = refs[:ns], refs[2 * ns:3 * ns], refs[3 * ns:]
        for cp in _gather_copies(srcs, bufs, specs, *sems, False):
            cp.start()
        for cp in _gather_copies(srcs, bufs, specs, *sems, False):
            cp.wait_send()
        for cp in _gather_copies(srcs, bufs, specs, *sems, True):
            cp.wait_recv()

    return pl.pallas_call(
        body, in_specs=[ANY] * (2 * ns), out_specs=[ANY] * ns,
        out_shape=[jax.ShapeDtypeStruct(f.shape, f.dtype) for f in fulls],
        input_output_aliases={ns + i: i for i in range(ns)},
        scratch_shapes=[pltpu.SemaphoreType.DMA((4 * npc,)), pltpu.SemaphoreType.DMA((4 * npc,))],
        name=f"gather_send_{tag}", compiler_params=pltpu.CompilerParams(has_side_effects=True))(*words, *fulls)


def _in_hbm(arrays):
    return [pltpu.with_memory_space_constraint(a, pltpu.HBM) for a in arrays]


def gather_start(words, fulls, specs, dep, tag):
    ns, npc = len(specs), _n_pieces(specs)

    def body(*refs):
        ssem, rsem = refs[2 * ns + 1:2 * ns + 3]
        for cp in _gather_copies(refs[:ns], refs[ns:2 * ns], specs, ssem, rsem, False):
            cp.start()
        refs[-1][...] = jnp.zeros_like(refs[-1])

    outs = pl.pallas_call(
        body, in_specs=[HBM] * (2 * ns) + [ANY],
        out_specs=[SEM, SEM] + [HBM] * (2 * ns) + [pl.BlockSpec(memory_space=pltpu.VMEM)],
        out_shape=[pltpu.SemaphoreType.DMA((4 * npc,)), pltpu.SemaphoreType.DMA((4 * npc,))]
        + [pltpu.HBM(a.shape, a.dtype) for a in list(words) + list(fulls)] + [jax.ShapeDtypeStruct((8, 128), F32)],
        input_output_aliases={i: 2 + i for i in range(2 * ns)},
        name=f"gather_start_{tag}", compiler_params=pltpu.CompilerParams(has_side_effects=EFFECT))(
            *_in_hbm(list(words) + list(fulls)), dep)
    return outs[0], outs[1], outs[2:2 + ns], outs[2 + ns:2 + 2 * ns], outs[-1]


def gather_wait(ssem, rsem, words, fulls, specs, after, tag):
    ns = len(specs)

    def body(*refs):
        srcs, bufs, ssem, rsem = refs[:ns], refs[ns:2 * ns], refs[2 * ns], refs[2 * ns + 1]
        for cp in _gather_copies(srcs, bufs, specs, ssem, rsem, False):
            cp.wait_send()
        for cp in _gather_copies(srcs, bufs, specs, ssem, rsem, True):
            cp.wait_recv()

    outs = pl.pallas_call(
        body, in_specs=[HBM] * (2 * ns) + [SEM, SEM, ANY], out_specs=[HBM] * (2 * ns),
        out_shape=[pltpu.HBM(a.shape, a.dtype) for a in list(words) + list(fulls)],
        input_output_aliases={i: i for i in range(2 * ns)},
        name=f"gather_wait_{tag}", compiler_params=pltpu.CompilerParams(has_side_effects=EFFECT))(
            *words, *fulls, ssem, rsem, after)
    return outs[ns:]


def unpack_weights(p, sp, tag):
    tr = 256 if sp.full_rows % 256 == 0 else sp.full_rows

    def body(p_ref, o_ref):
        o_ref[...] = _unpack(p_ref[...]).astype(jnp.bfloat16)

    return pl.pallas_call(
        body, grid=(sp.full_rows // tr,),
        in_specs=[pl.BlockSpec((None, tr, sp.wcols), lambda i: (0, i, 0))],
        out_specs=pl.BlockSpec((tr, sp.cols), lambda i: (i, 0)),
        out_shape=jax.ShapeDtypeStruct((sp.full_rows, sp.cols), jnp.bfloat16),
        name=f"unpack_{tag}", compiler_params=_params(("arbitrary",)))(p)


def gather_forward(fulls, specs, tag):
    ns, npc = len(specs), _n_pieces(specs)

    def body(*refs):
        bufs = refs[ns:2 * ns]
        ssem, rsem = refs[2 * ns:]
        x, y, c = _coords()
        sibling = (x, y, 1 - c)
        waits = []
        p = 0
        for buf, sp in zip(bufs, specs):
            for s, n in sp.pieces:
                for t, (fx, fy) in enumerate(FLIPS):
                    chip = 4 * (x ^ fx) + 2 * (y ^ fy)
                    here = _rows(buf, sp.off(chip + c, s), n)
                    send = pltpu.make_async_remote_copy(here, here, ssem.at[t, p], rsem.at[t, p],
                                                        device_id=sibling, device_id_type=MESH)
                    send.start()
                    waits.append(send.wait_send)
                    there = _rows(buf, sp.off(chip + 1 - c, s), n)
                    waits.append(pltpu.make_async_remote_copy(here, there, ssem.at[t, p], rsem.at[t, p],
                                                              device_id=sibling, device_id_type=MESH).wait_recv)
                p += 1
        for w in waits:
            w()

    return pl.pallas_call(
        body, in_specs=[ANY] * ns, out_specs=[ANY] * ns,
        out_shape=[jax.ShapeDtypeStruct(f.shape, f.dtype) for f in fulls],
        input_output_aliases={i: i for i in range(ns)},
        scratch_shapes=[pltpu.SemaphoreType.DMA((3, npc)), pltpu.SemaphoreType.DMA((3, npc))],
        name=f"gather_forward_{tag}", compiler_params=pltpu.CompilerParams(has_side_effects=True))(*fulls)


def all_gather(shards, layer, specs, names, tag):
    placed = [pack_place(s, sp, layer, f"{tag}_{n}") for s, sp, n in zip(shards, specs, names)]
    fulls = gather_send([p[0] for p in placed], [p[1] for p in placed], specs, tag)
    return gather_forward(fulls, specs, tag)


def reduce_pair(grads, specs, tag):
    ns, npc = len(specs), _n_pieces(specs)

    def body(*refs):
        srcs, theirs = refs[:ns], refs[ns:2 * ns]
        ssem, rsem = refs[2 * ns:]
        x, y, c = _coords()
        sibling = (x, y, 1 - c)
        waits = []
        p = 0
        for src, their, sp in zip(srcs, theirs, specs):
            for s, n in sp.pieces:
                for j in range(4):
                    send = pltpu.make_async_remote_copy(_rows(src, sp.off(2 * j + 1 - c, s), n), _rows(their.at[j], s, n),
                                                        ssem.at[j, p], rsem.at[j, p], device_id=sibling, device_id_type=MESH)
                    send.start()
                    waits.append(send.wait)
                p += 1
        for w in waits:
            w()

    return pl.pallas_call(
        body, in_specs=[ANY] * ns, out_specs=[ANY] * ns,
        out_shape=[jax.ShapeDtypeStruct((4, sp.layers, sp.rows, sp.cols), F32) for sp in specs],
        scratch_shapes=[pltpu.SemaphoreType.DMA((4, npc)), pltpu.SemaphoreType.DMA((4, npc))],
        name=f"reduce_pair_{tag}", compiler_params=pltpu.CompilerParams(has_side_effects=True))(*grads)


def pair_sum(g, r1, sp, tag):
    npc = len(sp.pieces)

    def body(g_ref, r_ref, own_ref, words_ref, gbuf, sem):
        l, j = pl.program_id(0), pl.program_id(1)
        x, y, c = _coords()
        copies = [pltpu.make_async_copy(g_ref.at[l, pl.ds(pl.multiple_of(sp.off(2 * j + c, s), 8), n), :],
                                        gbuf.at[pl.ds(s, n), :], sem.at[i]) for i, (s, n) in enumerate(sp.pieces)]
        for cp in copies:
            cp.start()
        for cp in copies:
            cp.wait()
        p = gbuf[...] + r_ref[...]
        words_ref[...] = sp.to_words(p)

        @pl.when(j == 2 * x + y)
        def _():
            own_ref[...] = p

    return pl.pallas_call(
        body, grid=(sp.layers, 4),
        in_specs=[ANY, pl.BlockSpec((None, None, sp.rows, sp.cols), lambda l, j: (j, l, 0, 0))],
        out_specs=[pl.BlockSpec((None, sp.rows, sp.cols), lambda l, j: (l, 0, 0)),
                   pl.BlockSpec((None, None, sp.rows, sp.wcols), lambda l, j: (j, l, 0, 0))],
        out_shape=[jax.ShapeDtypeStruct((sp.layers, sp.rows, sp.cols), F32),
                   jax.ShapeDtypeStruct((4, sp.layers, sp.rows, sp.wcols), F32)],
        scratch_shapes=[pltpu.VMEM((sp.rows, sp.cols), F32), pltpu.SemaphoreType.DMA((npc,))],
        name=f"pair_sum_{tag}", compiler_params=_params(("arbitrary", "arbitrary")))(g, r1)


def _chip_copies(srcs, dsts, ssem, rsem):
    x, y, c = _coords()
    copies = []
    for i, (src, dst) in enumerate(zip(srcs, dsts)):
        for t, (fx, fy) in enumerate(FLIPS):
            tx, ty = x ^ fx, y ^ fy
            copies.append(pltpu.make_async_remote_copy(src.at[2 * tx + ty], dst.at[t], ssem.at[3 * i + t], rsem.at[3 * i + t],
                                                       device_id=(tx, ty, c), device_id_type=MESH))
    return copies


def _slot_shapes(words):
    return [(3,) + w.shape[1:] for w in words]


def reduce_chips(words, specs, tag):
    ns = len(specs)

    def body(*refs):
        copies = _chip_copies(refs[:ns], refs[ns:2 * ns], *refs[2 * ns:])
        for cp in copies:
            cp.start()
        for cp in copies:
            cp.wait()

    return pl.pallas_call(
        body, in_specs=[ANY] * ns, out_specs=[ANY] * ns,
        out_shape=[jax.ShapeDtypeStruct(s, F32) for s in _slot_shapes(words)],
        scratch_shapes=[pltpu.SemaphoreType.DMA((3 * ns,)), pltpu.SemaphoreType.DMA((3 * ns,))],
        name=f"reduce_chips_{tag}", compiler_params=pltpu.CompilerParams(has_side_effects=True))(*words)


def chips_start(words, specs, tag):
    ns = len(specs)
    slots = [lax.empty(s, F32) for s in _slot_shapes(words)]

    def body(*refs):
        ssem, rsem = refs[2 * ns:2 * ns + 2]
        for cp in _chip_copies(refs[:ns], refs[ns:2 * ns], ssem, rsem):
            cp.start()
        refs[-1][...] = jnp.zeros_like(refs[-1])

    outs = pl.pallas_call(
        body, in_specs=[HBM] * (2 * ns),
        out_specs=[SEM, SEM] + [HBM] * (2 * ns) + [pl.BlockSpec(memory_space=pltpu.VMEM)],
        out_shape=[pltpu.SemaphoreType.DMA((3 * ns,)), pltpu.SemaphoreType.DMA((3 * ns,))]
        + [pltpu.HBM(a.shape, a.dtype) for a in list(words) + slots] + [jax.ShapeDtypeStruct((8, 128), F32)],
        input_output_aliases={i: 2 + i for i in range(2 * ns)},
        name=f"chips_start_{tag}", compiler_params=pltpu.CompilerParams(has_side_effects=EFFECT))(
            *_in_hbm(list(words) + slots))
    return outs[0], outs[1], outs[2:2 + ns], outs[2 + ns:2 + 2 * ns], outs[-1]


def chips_wait(ssem, rsem, words, slots, specs, after, tag):
    ns = len(specs)

    def body(*refs):
        for cp in _chip_copies(refs[:ns], refs[ns:2 * ns], refs[2 * ns], refs[2 * ns + 1]):
            cp.wait_send()
            cp.wait_recv()

    outs = pl.pallas_call(
        body, in_specs=[HBM] * (2 * ns) + [SEM, SEM, ANY], out_specs=[HBM] * (2 * ns),
        out_shape=[pltpu.HBM(a.shape, a.dtype) for a in list(words) + list(slots)],
        input_output_aliases={i: i for i in range(2 * ns)},
        name=f"chips_wait_{tag}", compiler_params=pltpu.CompilerParams(has_side_effects=EFFECT))(
            *words, *slots, ssem, rsem, after)
    return outs[ns:]


def sum_chips(own, r2, sp, tag):
    def body(own_ref, r_ref, o_ref):
        o_ref[...] = ((own_ref[...] + sp.from_words(r_ref[0])) + sp.from_words(r_ref[1])) + sp.from_words(r_ref[2])

    blk = pl.BlockSpec((None, sp.rows, sp.cols), lambda l: (l, 0, 0))
    return pl.pallas_call(
        body, grid=(sp.layers,), in_specs=[blk, pl.BlockSpec((3, None, sp.rows, sp.wcols), lambda l: (0, l, 0, 0))],
        out_specs=blk, out_shape=jax.ShapeDtypeStruct((sp.layers, sp.rows, sp.cols), F32),
        name=f"sum_chips_{tag}", compiler_params=_params(("arbitrary",)))(own, r2)


def reduce_scatter_start(grads, specs, names, tag):
    theirs = reduce_pair(grads, specs, tag)
    sums = [pair_sum(g, r1, sp, f"{tag}_{n}") for g, r1, sp, n in zip(grads, theirs, specs, names)]
    ssem, rsem, words, slots, token = chips_start([s[1] for s in sums], specs, tag)
    return (ssem, rsem, words, slots, [s[0] for s in sums]), token


def reduce_scatter_finish(state, after, specs, names, tag):
    ssem, rsem, words, slots, own = state
    r2 = chips_wait(ssem, rsem, words, slots, specs, after, tag)
    return [sum_chips(o, r, sp, f"{tag}_{n}") for o, r, sp, n in zip(own, r2, specs, names)]


def reduce_scatter(grads, specs, names, tag):
    theirs = reduce_pair(grads, specs, tag)
    sums = [pair_sum(g, r1, sp, f"{tag}_{n}") for g, r1, sp, n in zip(grads, theirs, specs, names)]
    r2 = reduce_chips([s[1] for s in sums], specs, tag)
    return [sum_chips(s[0], r, sp, f"{tag}_{n}") for s, r, sp, n in zip(sums, r2, specs, names)]


def _adamw_math(w, g, m, v):
    c1 = 1.0 - ADAM_B1 ** ADAM_STEP
    c2 = 1.0 - ADAM_B2 ** ADAM_STEP
    m2 = ADAM_B1 * m + (1.0 - ADAM_B1) * g
    v2 = ADAM_B2 * v + (1.0 - ADAM_B2) * (g * g)
    return -ADAM_LR * ((m2 / c1) / (jnp.sqrt(v2 / c2) + ADAM_EPS) + ADAM_WD * w), m2, v2


def adamw(w, g, m, v, name):
    shape = w.shape
    cols = shape[-1]
    rows = math.prod(shape[:-1])
    tr = rows
    while tr * cols * 4 > (1 << 20) and tr % 16 == 0:
        tr //= 2

    def body(w_ref, g_ref, m_ref, v_ref, d_ref, nm_ref, nv_ref):
        d_ref[...], nm_ref[...], nv_ref[...] = _adamw_math(w_ref[...], g_ref[...], m_ref[...], v_ref[...])

    blk = pl.BlockSpec((tr, cols), lambda i: (i, 0))
    outs = pl.pallas_call(
        body, grid=(rows // tr,), in_specs=[blk] * 4, out_specs=[blk] * 3,
        out_shape=[jax.ShapeDtypeStruct((rows, cols), F32)] * 3,
        name=f"adamw_{name}", compiler_params=_params(("arbitrary",)))(
            *[a.reshape(rows, cols) for a in (w, g, m, v)])
    return [o.reshape(shape) for o in outs]


def adamw_layers(w, gs, m, v, name):
    _, rows, cols = w.shape
    tc = _col_tile(cols)

    def body(w_ref, g0_ref, g1_ref, m_ref, v_ref, g_ref, d_ref, nm_ref, nv_ref):
        g = jnp.where(pl.program_id(0) == 0, g0_ref[...], g1_ref[...])
        g_ref[...] = g
        d_ref[...], nm_ref[...], nv_ref[...] = _adamw_math(w_ref[...], g, m_ref[...], v_ref[...])

    blk = pl.BlockSpec((None, rows, tc), lambda l, n: (l, 0, n))
    one = pl.BlockSpec((None, rows, tc), lambda l, n: (0, 0, n))
    return pl.pallas_call(
        body, grid=(L, cols // tc), in_specs=[blk, one, one, blk, blk], out_specs=[blk] * 4,
        out_shape=[jax.ShapeDtypeStruct(w.shape, F32)] * 4,
        name=f"adamw_{name}", compiler_params=_params(("arbitrary", "arbitrary")))(w, gs[0], gs[1], m, v)


WEIGHTS = ("pre_norm_g", "w_in", "gm_ln_g", "gm_ln_b", "gm_ws", "gm_bs", "mla_q_norm_g", "mla_w_uq", "mla_kv_norm_g",
           "mla_w_ukv", "lru_conv_w", "lru_conv_b", "lru_w_a", "lru_b_a", "lru_w_x", "lru_b_x", "lru_lambda",
           "w_proj_a", "w_proj_b", "w_proj_c", "w_out", "post_norm_g")
SHARDED = ("w_in", "mla_w_uq", "mla_w_ukv", "lru_conv_w", "w_proj_a", "w_proj_b", "w_proj_c", "w_out")
REPLICATED = tuple(n for n in WEIGHTS if n not in SHARDED)


def _step(x, target, wts, ms, vs):
    t12 = lambda a: jnp.swapaxes(a, 1, 2)
    names = list(WEIGHT_SPECS)
    specs = [WEIGHT_SPECS[n] for n in names]
    tabs = _rope_tables()
    own = {"w_in_t": t12(wts["w_in"]), "wq": t12(wts["mla_w_uq"]), "wkv": t12(wts["mla_w_ukv"]),
           "conv": jnp.pad(t12(wts["lru_conv_w"]), ((0, 0), (0, 0), (0, 124))),
           "w_proj_a": wts["w_proj_a"], "w_proj_b": wts["w_proj_b"], "w_proj_c": wts["w_proj_c"], "w_out": wts["w_out"]}
    shards = [own[n] for n in names]

    w = {n: wts[n] for n in REPLICATED}
    w["kv_g384"] = jnp.concatenate([wts["mla_kv_norm_g"], jnp.ones((L, 128), F32)], axis=1)
    w["wa_dense"] = _block_diag(wts["lru_w_a"])
    w["wx_dense"] = _block_diag(wts["lru_w_x"])

    def layer_weights(words, l):
        gw = {n: unpack_weights(p, sp, f"{n}_l{l}") if sp.packed else p[0] for n, p, sp in zip(names, words, specs)}
        gw["wq"] = gw["wq"].reshape(HEADS, 192, 384)
        gw["wkv"] = gw["wkv"].reshape(HEADS, 256, 256)
        gw["conv"] = gw["conv"][:, :4].T
        return gw

    words0 = all_gather(shards, 0, specs, names, "w0")
    placed1 = [pack_place(s, sp, 1, f"w1_{n}") for s, sp, n in zip(shards, specs, names)]
    ssem, rsem, wthru, fthru, token = gather_start([p[0] for p in placed1], [p[1] for p in placed1], specs, words0[0], "w1")
    gw0 = layer_weights(words0, 0)
    x1, saved0 = _layer_fwd(x, 0, w, gw0, tabs, dep=token)
    words1 = gather_forward(gather_wait(ssem, rsem, wthru, fthru, specs, x1, "w1"), specs, "w1")
    gw1 = layer_weights(words1, 1)
    x2, saved1 = _layer_fwd(x1, 1, w, gw1, tabs)
    loss, dx2 = loss_head(x2, target)

    dx1, gg1, g1 = _layer_bwd(dx2, 1, w, gw1, tabs, saved1)
    state1, token1 = reduce_scatter_start([gg1[n] for n in names], specs, names, "g1")
    dx0, gg0, g0 = _layer_bwd(dx1, 0, w, gw0, tabs, saved0, dep=token1)
    s1 = dict(zip(names, reduce_scatter_finish(state1, dx0, specs, names, "g1")))

    rep_flat = jnp.concatenate([jnp.stack([g0[n], g1[n]]).reshape(-1) for n in REPLICATED])
    rep_flat = jnp.pad(rep_flat, (0, REP_ROWS * NDEV * D - rep_flat.shape[0])).reshape(1, REP_ROWS * NDEV, D)
    summed0 = reduce_scatter([gg0[n] for n in names] + [rep_flat], specs + [REP_SPEC], names + ["rep"], "g0")
    s0 = dict(zip(names, summed0[:-1]))
    rep_full = all_gather([summed0[-1]], 0, [REP_SPEC], ["rep"], "rep")[0].reshape(-1)

    out = {}
    for n, key in (("w_in", "w_in_t"), ("mla_w_uq", "wq"), ("mla_w_ukv", "wkv")):
        out[n] = [t12(r) for r in adamw_layers(own[key], [s0[key], s1[key]], t12(ms[n]), t12(vs[n]), n)]
    for n in ("w_proj_a", "w_proj_b", "w_proj_c", "w_out"):
        out[n] = adamw_layers(wts[n], [s0[n], s1[n]], ms[n], vs[n], n)
    g_conv = t12(jnp.concatenate([s0["conv"], s1["conv"]])[:, :, :4])
    out["lru_conv_w"] = [g_conv] + adamw(wts["lru_conv_w"], g_conv, ms["lru_conv_w"], vs["lru_conv_w"], "lru_conv_w")
    at = 0
    for n in REPLICATED:
        size = math.prod(wts[n].shape)
        g = rep_full[at:at + size].reshape(wts[n].shape)
        out[n] = [g] + adamw(wts[n], g, ms[n], vs[n], n)
        at += size

    loss = lax.psum(loss, ("x", "y", "c"))
    return (loss, dx0[None], *[out[n][k] for k in range(4) for n in WEIGHTS])


def kernel(x, pre_norm_g, w_in, gm_ln_g, gm_ln_b, gm_ws, gm_bs, mla_q_norm_g, mla_w_uq, mla_kv_norm_g, mla_w_ukv, lru_conv_w, lru_conv_b, lru_w_a, lru_b_a, lru_w_x, lru_b_x, lru_lambda, w_proj_a, w_proj_b, w_proj_c, w_out, post_norm_g, loss_target, m_pre_norm_g, m_w_in, m_gm_ln_g, m_gm_ln_b, m_gm_ws, m_gm_bs, m_mla_q_norm_g, m_mla_w_uq, m_mla_kv_norm_g, m_mla_w_ukv, m_lru_conv_w, m_lru_conv_b, m_lru_w_a, m_lru_b_a, m_lru_w_x, m_lru_b_x, m_lru_lambda, m_w_proj_a, m_w_proj_b, m_w_proj_c, m_w_out, m_post_norm_g, v_pre_norm_g, v_w_in, v_gm_ln_g, v_gm_ln_b, v_gm_ws, v_gm_bs, v_mla_q_norm_g, v_mla_w_uq, v_mla_kv_norm_g, v_mla_w_ukv, v_lru_conv_w, v_lru_conv_b, v_lru_w_a, v_lru_b_a, v_lru_w_x, v_lru_b_x, v_lru_lambda, v_w_proj_a, v_w_proj_b, v_w_proj_c, v_w_out, v_post_norm_g):
    wts = dict(zip(WEIGHTS, (pre_norm_g, w_in, gm_ln_g, gm_ln_b, gm_ws, gm_bs, mla_q_norm_g, mla_w_uq, mla_kv_norm_g,
                             mla_w_ukv, lru_conv_w, lru_conv_b, lru_w_a, lru_b_a, lru_w_x, lru_b_x, lru_lambda,
                             w_proj_a, w_proj_b, w_proj_c, w_out, post_norm_g)))
    ms = dict(zip(WEIGHTS, (m_pre_norm_g, m_w_in, m_gm_ln_g, m_gm_ln_b, m_gm_ws, m_gm_bs, m_mla_q_norm_g, m_mla_w_uq,
                            m_mla_kv_norm_g, m_mla_w_ukv, m_lru_conv_w, m_lru_conv_b, m_lru_w_a, m_lru_b_a, m_lru_w_x,
                            m_lru_b_x, m_lru_lambda, m_w_proj_a, m_w_proj_b, m_w_proj_c, m_w_out, m_post_norm_g)))
    vs = dict(zip(WEIGHTS, (v_pre_norm_g, v_w_in, v_gm_ln_g, v_gm_ln_b, v_gm_ws, v_gm_bs, v_mla_q_norm_g, v_mla_w_uq,
                            v_mla_kv_norm_g, v_mla_w_ukv, v_lru_conv_w, v_lru_conv_b, v_lru_w_a, v_lru_b_a, v_lru_w_x,
                            v_lru_b_x, v_lru_lambda, v_w_proj_a, v_w_proj_b, v_w_proj_c, v_w_out, v_post_norm_g)))
    return _step(x[0], loss_target[0], wts, ms, vs)
```

```python
import functools
import math

import jax
import jax.numpy as jnp
from jax import lax
from jax.experimental import pallas as pl
from jax.experimental.pallas import tpu as pltpu

F32 = jnp.float32
BF16 = jnp.bfloat16

T = 2048
D = 1024
L = 2
NDEV = 8
EPS = 1e-6
CHUNK_SHIFT = 6
HEADS = 8
QK = 192
LRU_W = 1280
LRU_TILE = 640
N_IN = 10432
SHARD = N_IN // NDEV
OFF_U, OFF_V, OFF_ZA, OFF_CQ, OFF_CKV, OFF_ZB = 0, 1024, 2048, 3072, 3456, 3840
OFF_XC, OFF_ZC, OFF_GA, OFF_GB, OFF_GC = 5120, 6400, 7680, 8704, 9728
NPAD = 10752
PAD1_AT, PAD1 = 3776, 64
PAD2_AT, PAD2 = 4800, 256
WIN_PIECES = ((0, 888), (888, 280), (1168, 136))
VMEM_LIMIT = 60 * 1024 * 1024

ADAM_LR, ADAM_B1, ADAM_B2, ADAM_EPS, ADAM_WD, ADAM_STEP = 0.001, 0.9, 0.999, 1e-08, 0.01, 10

_NN = (((1,), (0,)), ((), ()))
_NT = (((1,), (1,)), ((), ()))
_TN = (((0,), (0,)), ((), ()))


def _dg(a, b, dims):
    return lax.dot_general(a.astype(BF16), b.astype(BF16), dims, preferred_element_type=F32)


@jax.custom_vjp
def dot_nn(a, b):
    return _dg(a, b, _NN)


def _nn_fwd(a, b):
    return _dg(a, b, _NN), (a, b)


def _nn_bwd(res, g):
    a, b = res
    return _dg(g, b, _NT).astype(a.dtype), _dg(a, g, _TN).astype(b.dtype)


dot_nn.defvjp(_nn_fwd, _nn_bwd)


@jax.custom_vjp
def dot_nt(a, b):
    return _dg(a, b, _NT)


def _nt_fwd(a, b):
    return _dg(a, b, _NT), (a, b)


def _nt_bwd(res, g):
    a, b = res
    return _dg(g, b, _NN).astype(a.dtype), _dg(g, a, _TN).astype(b.dtype)


dot_nt.defvjp(_nt_fwd, _nt_bwd)


def _params(sem=None):
    return pltpu.CompilerParams(dimension_semantics=sem, vmem_limit_bytes=VMEM_LIMIT)


def _sigmoid(x):
    return 1.0 / (1.0 + jnp.exp(-x))


def _silu(x):
    return x * _sigmoid(x)


def _rms(x, g):
    ms = jnp.mean(x * x, axis=-1, keepdims=True)
    return x * lax.rsqrt(ms + EPS) * g


def _acc(ref, val, first):
    @pl.when(first)
    def _():
        ref[...] = val

    @pl.when(jnp.logical_not(first))
    def _():
        ref[...] += val


ANY = pl.BlockSpec(memory_space=pl.ANY)


INPROJ_TN = 512


def inproj_fwd(x, g, wt, l, dep=None):
    tn = INPROJ_TN

    def body(x_ref, g_ref, w_ref, *rest):
        proj_ref, h_ref = rest[-2:]

        @pl.when(pl.program_id(0) == 0)
        def _():
            h_ref[...] = _rms(x_ref[...], g_ref[...]).astype(BF16)

        proj_ref[...] = lax.dot_general(h_ref[...], _unpack(w_ref[...]).astype(BF16), _NT, preferred_element_type=F32)

    deps = [] if dep is None else [dep]
    return pl.pallas_call(
        body, grid=(NPAD // tn,),
        in_specs=[pl.BlockSpec((T, D), lambda j: (0, 0)), pl.BlockSpec((1, D), lambda j: (0, 0)),
                  pl.BlockSpec((None, tn, D // 2), lambda j: (0, j, 0))] + [ANY] * len(deps),
        out_specs=[pl.BlockSpec((T, tn), lambda j: (0, j)), pl.BlockSpec((T, D), lambda j: (0, 0))],
        out_shape=[jax.ShapeDtypeStruct((T, NPAD), F32), jax.ShapeDtypeStruct((T, D), BF16)],
        name=f"inproj_fwd_l{l}", compiler_params=_params(("arbitrary",)))(x, g, wt, *deps)


def inproj_bwd(dproj, h, wt, l):
    tn = INPROJ_TN

    def body(dp_ref, h_ref, w_ref, dwt_ref, dh_ref):
        dp = dp_ref[...]
        dwt_ref[...] = lax.dot_general(dp, h_ref[...], _TN, preferred_element_type=F32)
        contrib = lax.dot_general(dp, _unpack(w_ref[...]).astype(BF16), _NN, preferred_element_type=F32)
        _acc(dh_ref, contrib, pl.program_id(0) == 0)

    return pl.pallas_call(
        body, grid=(NPAD // tn,),
        in_specs=[pl.BlockSpec((T, tn), lambda j: (0, j)), pl.BlockSpec((T, D), lambda j: (0, 0)),
                  pl.BlockSpec((None, tn, D // 2), lambda j: (0, j, 0))],
        out_specs=[pl.BlockSpec((None, tn, D), lambda j: (0, j, 0)), pl.BlockSpec((T, D), lambda j: (0, 0))],
        out_shape=[jax.ShapeDtypeStruct((1, NPAD, D), F32), jax.ShapeDtypeStruct((T, D), F32)],
        name=f"inproj_bwd_l{l}", compiler_params=_params(("arbitrary",)))(dproj, h, wt)


def prenorm_bwd(x, g, dh, dxn, l):
    tm = 256

    def body(x_ref, g_ref, dh_ref, dxn_ref, dx_ref, dg_ref):
        _, vjp = jax.vjp(_rms, x_ref[...], g_ref[...])
        dx, dg = vjp(dh_ref[...])
        dx_ref[...] = dx + dxn_ref[...]
        _acc(dg_ref, dg, pl.program_id(0) == 0)

    tok = pl.BlockSpec((tm, D), lambda i: (i, 0))
    vec = pl.BlockSpec((1, D), lambda i: (0, 0))
    return pl.pallas_call(
        body, grid=(T // tm,), in_specs=[tok, vec, tok, tok], out_specs=[tok, vec],
        out_shape=[jax.ShapeDtypeStruct((T, D), F32), jax.ShapeDtypeStruct((1, D), F32)],
        name=f"prenorm_bwd_l{l}", compiler_params=_params(("arbitrary",)))(x, g, dh, dxn)


def _gmlp_tile(u, v, z, ln_g, ln_b, ws, bs):
    mu = jnp.mean(v, axis=-1, keepdims=True)
    vc = v - mu
    var = jnp.mean(vc * vc, axis=-1, keepdims=True)
    vn = vc * lax.rsqrt(var + EPS) * ln_g + ln_b
    qi = lax.broadcasted_iota(jnp.int32, (128, 128), 0) >> CHUNK_SHIFT
    kj = lax.broadcasted_iota(jnp.int32, (128, 128), 1) >> CHUNK_SHIFT
    mask = kj <= qi
    outs = []
    for g in range(4):
        wm = jnp.where(mask, ws[g], 0.0)
        outs.append(dot_nn(wm, vn[:, 256 * g:256 * (g + 1)]) + bs[g])
    sv = jnp.concatenate(outs, axis=1)
    return u * sv * _silu(z)


def _gmlp_specs():
    blk = lambda c: pl.BlockSpec((128, 1024), lambda n, c=c: (n, c))
    vec = pl.BlockSpec((1, 1024), lambda n: (0, 0))
    return [blk(0), blk(1), blk(2), vec, vec,
            pl.BlockSpec((4, 128, 128), lambda n: (0, 0, 0)), pl.BlockSpec((4, 128, 1), lambda n: (0, 0, 0))]


def gmlp_fwd(proj, ln_g, ln_b, ws, bs, l):
    def body(u_ref, v_ref, z_ref, g_ref, b_ref, ws_ref, bs_ref, y_ref):
        y_ref[...] = _gmlp_tile(u_ref[...], v_ref[...], z_ref[...], g_ref[...], b_ref[...],
                                [ws_ref[g] for g in range(4)], [bs_ref[g] for g in range(4)])

    return pl.pallas_call(
        body, grid=(T // 128,), in_specs=_gmlp_specs(),
        out_specs=pl.BlockSpec((128, 1024), lambda n: (n, 0)),
        out_shape=jax.ShapeDtypeStruct((T, 1024), F32),
        name=f"gmlp_fwd_l{l}", compiler_params=_params(("arbitrary",)))(proj, proj, proj, ln_g, ln_b, ws, bs)


def gmlp_bwd(proj, ln_g, ln_b, ws, bs, dy, l):
    def body(u_ref, v_ref, z_ref, g_ref, b_ref, ws_ref, bs_ref, dy_ref, dseg_ref, dg_ref, db_ref, dws_ref, dbs_ref):
        first = pl.program_id(0) == 0
        _, vjp = jax.vjp(_gmlp_tile, u_ref[...], v_ref[...], z_ref[...], g_ref[...], b_ref[...],
                         [ws_ref[g] for g in range(4)], [bs_ref[g] for g in range(4)])
        du, dv, dz, dg, db, dws, dbs = vjp(dy_ref[...])
        dseg_ref[:, 0:1024] = du.astype(BF16)
        dseg_ref[:, 1024:2048] = dv.astype(BF16)
        dseg_ref[:, 2048:3072] = dz.astype(BF16)
        _acc(dg_ref, dg, first)
        _acc(db_ref, db, first)
        for g in range(4):
            _acc(dws_ref.at[g], dws[g], first)
            _acc(dbs_ref.at[g], dbs[g], first)

    vec = pl.BlockSpec((1, 1024), lambda n: (0, 0))
    return pl.pallas_call(
        body, grid=(T // 128,), in_specs=_gmlp_specs() + [pl.BlockSpec((128, 1024), lambda n: (n, 0))],
        out_specs=[pl.BlockSpec((128, 3072), lambda n: (n, 0)), vec, vec,
                   pl.BlockSpec((4, 128, 128), lambda n: (0, 0, 0)), pl.BlockSpec((4, 128, 1), lambda n: (0, 0, 0))],
        out_shape=[jax.ShapeDtypeStruct((T, 3072), BF16), jax.ShapeDtypeStruct((1, 1024), F32),
                   jax.ShapeDtypeStruct((1, 1024), F32), jax.ShapeDtypeStruct((4, 128, 128), F32),
                   jax.ShapeDtypeStruct((4, 128, 1), F32)],
        name=f"gmlp_bwd_l{l}", compiler_params=_params(("arbitrary",)))(proj, proj, proj, ln_g, ln_b, ws, bs, dy)


QKV_TM = 256


def _qkv_tile(cq, ckvr, qg, kvg, wq, wkv, ctab, stab):
    tm = cq.shape[0]
    cqn = _rms(cq, qg)
    lane = lax.broadcasted_iota(jnp.int32, ckvr.shape, 1)
    iskv = lane < 256
    ms = jnp.sum(jnp.where(iskv, ckvr * ckvr, 0.0), axis=-1, keepdims=True) * (1.0 / 256)
    lm = jnp.where(iskv, ckvr * lax.rsqrt(ms + EPS) * kvg, ckvr)
    r = lax.broadcasted_iota(jnp.int32, (64, 128), 0)
    c = lax.broadcasted_iota(jnp.int32, (64, 128), 1)
    eye = jnp.where(c == r, 1.0, 0.0)
    eye_sw = jnp.where(c == ((r + 32) & 63), 1.0, 0.0)
    z64 = jnp.zeros((64, 256), F32)
    z128 = jnp.zeros((128, 128), F32)
    rk_rope = jnp.concatenate([z64, eye], axis=1)
    rk_sw = jnp.concatenate([jnp.zeros((128, 384), F32), jnp.concatenate([z64, eye_sw], axis=1)], axis=0)
    k_sw = dot_nt(lm, rk_sw) * stab
    qs, ks, vs = [], [], []
    for h in range(HEADS):
        wn, w1, w2 = wq[h]
        wk, wv = wkv[h]
        wq_h = jnp.concatenate([wn, w1, w2], axis=0)
        wq_sw = jnp.concatenate([jnp.zeros((128, 384), F32), w2, w1], axis=0)
        qs.append(dot_nt(cqn, wq_h) * ctab + dot_nt(cqn, wq_sw) * stab)
        rk_h = jnp.concatenate([jnp.concatenate([wk, z128], axis=1), rk_rope], axis=0)
        ks.append(dot_nt(lm, rk_h) * ctab + k_sw)
        vs.append(dot_nt(lm, jnp.concatenate([wv, z128], axis=1)))
    return qs, ks, vs


def _qkv_in_specs():
    tm = QKV_TM
    return [pl.BlockSpec((tm, 384), lambda i: (i, OFF_CQ // 384)), pl.BlockSpec((tm, 384), lambda i: (i, OFF_CKV // 384)),
            pl.BlockSpec((1, 384), lambda i: (0, 0)), pl.BlockSpec((1, 384), lambda i: (0, 0)),
            pl.BlockSpec((HEADS, 192, 384), lambda i: (0, 0, 0)), pl.BlockSpec((HEADS, 256, 128), lambda i: (0, 0, 0)),
            pl.BlockSpec((tm, 192), lambda i: (i, 0)), pl.BlockSpec((tm, 192), lambda i: (i, 0))]


def _qkv_weights(wq_ref, wkv_ref):
    wq = [(wq_ref[h, 0:128, :], wq_ref[h, 128:160, :], wq_ref[h, 160:192, :]) for h in range(HEADS)]
    wkv = [(_unpack(wkv_ref[h, 0:128, :]), _unpack(wkv_ref[h, 128:256, :])) for h in range(HEADS)]
    return wq, wkv


def qkv_fwd(proj, qg, kvg, wq, wkv, ctab, stab, l):
    tm = QKV_TM

    def body(cq_ref, ckvr_ref, qg_ref, kvg_ref, wq_ref, wkv_ref, c_ref, s_ref, q_ref, k_ref, v_ref):
        wq_l, wkv_l = _qkv_weights(wq_ref, wkv_ref)
        qs, ks, vs = _qkv_tile(cq_ref[...], ckvr_ref[...], qg_ref[...], kvg_ref[...], wq_l, wkv_l, c_ref[...], s_ref[...])
        for h in range(HEADS):
            q_ref[h] = qs[h]
            k_ref[h] = ks[h]
            v_ref[h] = vs[h]

    return pl.pallas_call(
        body, grid=(T // tm,), in_specs=_qkv_in_specs(),
        out_specs=[pl.BlockSpec((HEADS, tm, QK), lambda i: (0, i, 0)), pl.BlockSpec((HEADS, tm, QK), lambda i: (0, i, 0)),
                   pl.BlockSpec((HEADS, tm, 128), lambda i: (0, i, 0))],
        out_shape=[jax.ShapeDtypeStruct((HEADS, T, QK), F32), jax.ShapeDtypeStruct((HEADS, T, QK), F32),
                   jax.ShapeDtypeStruct((HEADS, T, 128), F32)],
        name=f"qkv_fwd_l{l}", compiler_params=_params(("arbitrary",)))(proj, proj, qg, kvg, wq, wkv, ctab, stab)


def qkv_bwd(proj, qg, kvg, wq, wkv, ctab, stab, dq, dk, dv, l):
    tm = QKV_TM

    def body(cq_ref, ckvr_ref, qg_ref, kvg_ref, wq_ref, wkv_ref, c_ref, s_ref, dq_ref, dk_ref, dv_ref,
             dseg_ref, dqg_ref, dkvg_ref, dwq_ref, dwkv_ref):
        first = pl.program_id(0) == 0
        wq_l, wkv_l = _qkv_weights(wq_ref, wkv_ref)
        c_tab, s_tab = c_ref[...], s_ref[...]
        fn = lambda cq, ckvr, qg_, kvg_, wq_, wkv_: _qkv_tile(cq, ckvr, qg_, kvg_, wq_, wkv_, c_tab, s_tab)
        _, vjp = jax.vjp(fn, cq_ref[...], ckvr_ref[...], qg_ref[...], kvg_ref[...], wq_l, wkv_l)
        cts = ([dq_ref[h] for h in range(HEADS)], [dk_ref[h] for h in range(HEADS)], [dv_ref[h] for h in range(HEADS)])
        dcq, dckvr, dqg, dkvg, dwq, dwkv = vjp(cts)
        dseg_ref[:, 0:384] = dcq.astype(BF16)
        dseg_ref[:, 384:768] = dckvr.astype(BF16)
        _acc(dqg_ref, dqg, first)
        _acc(dkvg_ref, dkvg, first)
        for h in range(HEADS):
            _acc(dwq_ref.at[h, 0:128, :], dwq[h][0], first)
            _acc(dwq_ref.at[h, 128:160, :], dwq[h][1], first)
            _acc(dwq_ref.at[h, 160:192, :], dwq[h][2], first)
            _acc(dwkv_ref.at[h, 0:128, :], dwkv[h][0], first)
            _acc(dwkv_ref.at[h, 128:256, :], dwkv[h][1], first)

    hq = pl.BlockSpec((HEADS, tm, QK), lambda i: (0, i, 0))
    return pl.pallas_call(
        body, grid=(T // tm,),
        in_specs=_qkv_in_specs() + [hq, hq, pl.BlockSpec((HEADS, tm, 128), lambda i: (0, i, 0))],
        out_specs=[pl.BlockSpec((tm, 768), lambda i: (i, 0)), pl.BlockSpec((1, 384), lambda i: (0, 0)),
                   pl.BlockSpec((1, 384), lambda i: (0, 0)), pl.BlockSpec((HEADS, 192, 384), lambda i: (0, 0, 0)),
                   pl.BlockSpec((HEADS, 256, 256), lambda i: (0, 0, 0))],
        out_shape=[jax.ShapeDtypeStruct((T, 768), BF16), jax.ShapeDtypeStruct((1, 384), F32),
                   jax.ShapeDtypeStruct((1, 384), F32), jax.ShapeDtypeStruct((HEADS, 192, 384), F32),
                   jax.ShapeDtypeStruct((HEADS, 256, 256), F32)],
        name=f"qkv_bwd_l{l}", compiler_params=_params(("arbitrary",)))(
            proj, proj, qg, kvg, wq, wkv, ctab, stab, dq, dk, dv)


ATT_TQ = 256


def _attn_tile(q, kv_past, k, v, zb):
    scale = 1.0 / math.sqrt(QK)
    s = dot_nt(q, k) * scale
    qc = lax.broadcasted_iota(jnp.int32, s.shape, 0) >> CHUNK_SHIFT
    kc = lax.broadcasted_iota(jnp.int32, s.shape, 1) >> CHUNK_SHIFT
    s = jnp.where(kc <= qc, s, -1e30)
    m = jnp.max(s, axis=-1, keepdims=True)
    if kv_past is not None:
        sp = dot_nt(q, kv_past[0]) * scale
        m = jnp.maximum(m, jnp.max(sp, axis=-1, keepdims=True))
    m = lax.stop_gradient(m)
    p = jnp.exp(s - m)
    denom = jnp.sum(p, axis=-1, keepdims=True)
    o = dot_nn(p, v)
    if kv_past is not None:
        pp = jnp.exp(sp - m)
        denom = denom + jnp.sum(pp, axis=-1, keepdims=True)
        o = o + dot_nn(pp, kv_past[1])
    return o * (1.0 / denom) * _silu(zb)


def _attn_operands(k_ref, v_ref, g, tq):
    n = tq * g
    past = (k_ref[0:n, :], v_ref[0:n, :]) if g else None
    return past, k_ref[n:n + tq, :], v_ref[n:n + tq, :]


def _attn_in_specs():
    tq = ATT_TQ
    return [pl.BlockSpec((None, tq, QK), lambda h, i: (h, i, 0)), pl.BlockSpec((None, T, QK), lambda h, i: (h, 0, 0)),
            pl.BlockSpec((None, T, 128), lambda h, i: (h, 0, 0)),
            pl.BlockSpec((tq, 128), lambda h, i: (i, OFF_ZB // 128 + h))]


def attn_fwd(q, k, v, proj, l):
    tq = ATT_TQ

    def body(q_ref, k_ref, v_ref, z_ref, y_ref):
        for g in range(T // tq):
            @pl.when(pl.program_id(1) == g)
            def _(g=g):
                past, k, v = _attn_operands(k_ref, v_ref, g, tq)
                y_ref[...] = _attn_tile(q_ref[...], past, k, v, z_ref[...])

    return pl.pallas_call(
        body, grid=(HEADS, T // tq), in_specs=_attn_in_specs(),
        out_specs=pl.BlockSpec((tq, 128), lambda h, i: (i, h)),
        out_shape=jax.ShapeDtypeStruct((T, 1024), F32),
        name=f"attn_fwd_l{l}", compiler_params=_params(("arbitrary", "arbitrary")))(q, k, v, proj)


def attn_bwd(q, k, v, proj, dy, l):
    tq = ATT_TQ

    def body(q_ref, k_ref, v_ref, z_ref, dy_ref, dq_ref, dk_ref, dv_ref, dz_ref):
        @pl.when(pl.program_id(1) == 0)
        def _():
            dk_ref[...] = jnp.zeros_like(dk_ref)
            dv_ref[...] = jnp.zeros_like(dv_ref)

        for g in range(T // tq):
            @pl.when(pl.program_id(1) == g)
            def _(g=g):
                n = tq * g
                past, k, v = _attn_operands(k_ref, v_ref, g, tq)
                _, vjp = jax.vjp(_attn_tile, q_ref[...], past, k, v, z_ref[...])
                dq, dpast, dk, dv, dz = vjp(dy_ref[...])
                dq_ref[...] = dq
                dz_ref[...] = dz.astype(BF16)
                dk_ref[n:n + tq, :] += dk
                dv_ref[n:n + tq, :] += dv
                if g:
                    dk_ref[0:n, :] += dpast[0]
                    dv_ref[0:n, :] += dpast[1]

    return pl.pallas_call(
        body, grid=(HEADS, T // tq),
        in_specs=_attn_in_specs() + [pl.BlockSpec((tq, 128), lambda h, i: (i, h))],
        out_specs=[pl.BlockSpec((None, tq, QK), lambda h, i: (h, i, 0)), pl.BlockSpec((None, T, QK), lambda h, i: (h, 0, 0)),
                   pl.BlockSpec((None, T, 128), lambda h, i: (h, 0, 0)), pl.BlockSpec((tq, 128), lambda h, i: (i, h))],
        out_shape=[jax.ShapeDtypeStruct((HEADS, T, QK), F32), jax.ShapeDtypeStruct((HEADS, T, QK), F32),
                   jax.ShapeDtypeStruct((HEADS, T, 128), F32), jax.ShapeDtypeStruct((T, 1024), BF16)],
        name=f"attn_bwd_l{l}", compiler_params=_params(("arbitrary", "arbitrary")))(q, k, v, proj, dy)


LRU_TT = 256


def _lru_gates(xc, wa, wx, ba, bx, lam):
    r = _sigmoid(dot_nn(xc, wa) + ba)
    i = _sigmoid(dot_nn(xc, wx) + bx)
    sp = jnp.maximum(-lam, 0.0) + jnp.log1p(jnp.exp(-jnp.abs(lam)))
    log_a = -8.0 * r * sp
    a = jnp.exp(log_a)
    mult = jnp.sqrt(jnp.maximum(1.0 - jnp.exp(2.0 * log_a), 0.0))
    return a, mult * (i * xc)


def _shift_down(x, s, halo):
    xs = pltpu.roll(x, s, 0)
    row = lax.broadcasted_iota(jnp.int32, halo.shape, 0)
    top = jnp.where(row < s, pltpu.roll(halo, s, 0), xs[0:8])
    return jnp.concatenate([top, xs[8:]], axis=0)


def _shift_up(x, s, halo):
    n = x.shape[0]
    xs = pltpu.roll(x, n - s, 0)
    row = lax.broadcasted_iota(jnp.int32, halo.shape, 0)
    bot = jnp.where(row >= 8 - s, pltpu.roll(halo, 8 - s, 0), xs[n - 8:n])
    return jnp.concatenate([xs[:n - 8], bot], axis=0)


def _conv(x, halo, w_ref, b):
    return (w_ref[3:4, :] * x + w_ref[2:3, :] * _shift_down(x, 1, halo) + w_ref[1:2, :] * _shift_down(x, 2, halo)
            + w_ref[0:1, :] * _shift_down(x, 3, halo) + b)


def _scan(a, b, reverse):
    n = a.shape[0]
    row = lax.broadcasted_iota(jnp.int32, a.shape, 0)
    d = 1
    while d < n:
        if reverse:
            keep = row < n - d
            a_sh = jnp.where(keep, pltpu.roll(a, n - d, 0), 1.0)
            b_sh = jnp.where(keep, pltpu.roll(b, n - d, 0), 0.0)
        else:
            keep = row >= d
            a_sh = jnp.where(keep, pltpu.roll(a, d, 0), 1.0)
            b_sh = jnp.where(keep, pltpu.roll(b, d, 0), 0.0)
        b = a * b_sh + b
        a = a * a_sh
        d *= 2
    return a, b


def _lru_param_specs(time_map):
    ct = LRU_TILE
    vec = pl.BlockSpec((1, ct), lambda n, i: (0, n))
    return [pl.BlockSpec((4, ct), lambda n, i: (0, n)), vec,
            pl.BlockSpec((None, ct, ct), lambda n, i: (n, 0, 0)), pl.BlockSpec((None, ct, ct), lambda n, i: (n, 0, 0)),
            vec, vec, vec]


def lru_fwd(proj, conv_w, conv_b, wa, wx, ba, bx, lam, l):
    tt, ct = LRU_TT, LRU_TILE

    def body(x_ref, z_ref, cw_ref, cb_ref, wa_ref, wx_ref, ba_ref, bx_ref, lam_ref, h_ref, y_ref, halo, hcar):
        @pl.when(pl.program_id(1) == 0)
        def _():
            halo[...] = jnp.zeros_like(halo)
            hcar[...] = jnp.zeros_like(hcar)

        x = x_ref[...]
        xc = _conv(x, halo[...], cw_ref, cb_ref[...])
        halo[...] = x[tt - 8:tt]
        a, b = _lru_gates(xc, wa_ref[...], wx_ref[...], ba_ref[...], bx_ref[...], lam_ref[...])
        a_cum, b_cum = _scan(a, b, False)
        h = a_cum * hcar[...] + b_cum
        h_ref[...] = h
        hcar[...] = h_ref[tt - 1:tt, :]
        y_ref[...] = h * _silu(z_ref[...])

    seq = pl.BlockSpec((tt, ct), lambda n, i: (i, n))
    return pl.pallas_call(
        body, grid=(LRU_W // ct, T // tt),
        in_specs=[pl.BlockSpec((tt, ct), lambda n, i: (i, OFF_XC // ct + n)),
                  pl.BlockSpec((tt, ct), lambda n, i: (i, OFF_ZC // ct + n))] + _lru_param_specs(None),
        out_specs=[seq, seq],
        out_shape=[jax.ShapeDtypeStruct((T, LRU_W), F32), jax.ShapeDtypeStruct((T, LRU_W), F32)],
        scratch_shapes=[pltpu.VMEM((8, ct), F32), pltpu.VMEM((1, ct), F32)],
        name=f"lru_fwd_l{l}", compiler_params=_params(("arbitrary", "arbitrary")))(
            proj, proj, conv_w, conv_b, wa, wx, ba, bx, lam)


def lru_bwd(proj, hseq, dy, conv_w, conv_b, wa, wx, ba, bx, lam, l):
    tt, ct = LRU_TT, LRU_TILE
    nt = T // tt
    rev = lambda i: nt - 1 - i
    prev8 = lambda i: jnp.maximum(rev(i) * (tt // 8) - 1, 0)

    def body(x_ref, xh_ref, z_ref, h_ref, hh_ref, dy_ref, cw_ref, cb_ref, wa_ref, wx_ref, ba_ref, bx_ref, lam_ref,
             dx_ref, dz_ref, dcw_ref, dcb_ref, dwa_ref, dwx_ref, dba_ref, dbx_ref, dlam_ref, gcar, dhalo):
        i = pl.program_id(1)
        first = i == 0

        @pl.when(first)
        def _():
            gcar[...] = jnp.zeros_like(gcar)
            dhalo[...] = jnp.zeros_like(dhalo)

        at_start = rev(i) == 0
        x = x_ref[...]
        xhalo = jnp.where(at_start, 0.0, xh_ref[...])
        sh = [x, _shift_down(x, 1, xhalo), _shift_down(x, 2, xhalo), _shift_down(x, 3, xhalo)]
        xc = (cw_ref[3:4, :] * sh[0] + cw_ref[2:3, :] * sh[1] + cw_ref[1:2, :] * sh[2] + cw_ref[0:1, :] * sh[3]
              + cb_ref[...])
        (a, b), vjp = jax.vjp(_lru_gates, xc, wa_ref[...], wx_ref[...], ba_ref[...], bx_ref[...], lam_ref[...])
        hs = h_ref[...]
        hprev = _shift_down(hs, 1, jnp.where(at_start, 0.0, hh_ref[...]))
        z = z_ref[...]
        sg = _sigmoid(z)
        dy = dy_ref[...]
        dz_ref[...] = (dy * hs * (sg * (1.0 + z * (1.0 - sg)))).astype(BF16)
        dh = dy * (z * sg)
        row = lax.broadcasted_iota(jnp.int32, a.shape, 0)
        a_next = jnp.where(row < tt - 1, pltpu.roll(a, tt - 1, 0), 1.0)
        a_cum, b_cum = _scan(a_next, dh, True)
        g = a_cum * gcar[...] + b_cum
        dxc, dwa, dwx, dba, dbx, dlam = vjp((g * hprev, g))
        dx = (cw_ref[3:4, :] * dxc + cw_ref[2:3, :] * _shift_up(dxc, 1, dhalo[...])
              + cw_ref[1:2, :] * _shift_up(dxc, 2, dhalo[...]) + cw_ref[0:1, :] * _shift_up(dxc, 3, dhalo[...]))
        dx_ref[...] = dx.astype(BF16)
        dhalo[...] = dxc[0:8]
        ag = a * g
        gcar[...] = ag[0:1]
        dcw = jnp.concatenate([jnp.sum(dxc * sh[3 - j], axis=0, keepdims=True) for j in range(4)], axis=0)
        _acc(dcw_ref, dcw, first)
        _acc(dcb_ref, jnp.sum(dxc, axis=0, keepdims=True), first)
        _acc(dwa_ref, dwa, first)
        _acc(dwx_ref, dwx, first)
        _acc(dba_ref, dba, first)
        _acc(dbx_ref, dbx, first)
        _acc(dlam_ref, dlam, first)

    xcol = OFF_XC // ct
    zcol = OFF_ZC // ct
    vec = pl.BlockSpec((1, ct), lambda n, i: (0, n))
    mat = pl.BlockSpec((None, ct, ct), lambda n, i: (n, 0, 0))
    seq = pl.BlockSpec((tt, ct), lambda n, i: (rev(i), n))
    return pl.pallas_call(
        body, grid=(LRU_W // ct, nt),
        in_specs=[pl.BlockSpec((tt, ct), lambda n, i: (rev(i), xcol + n)),
                  pl.BlockSpec((8, ct), lambda n, i: (prev8(i), xcol + n)),
                  pl.BlockSpec((tt, ct), lambda n, i: (rev(i), zcol + n)),
                  seq, pl.BlockSpec((8, ct), lambda n, i: (prev8(i), n)), seq] + _lru_param_specs(None),
        out_specs=[seq, seq, pl.BlockSpec((4, ct), lambda n, i: (0, n)), vec, mat, mat, vec, vec, vec],
        out_shape=[jax.ShapeDtypeStruct((T, LRU_W), BF16), jax.ShapeDtypeStruct((T, LRU_W), BF16),
                   jax.ShapeDtypeStruct((4, LRU_W), F32), jax.ShapeDtypeStruct((1, LRU_W), F32),
                   jax.ShapeDtypeStruct((2, ct, ct), F32), jax.ShapeDtypeStruct((2, ct, ct), F32),
                   jax.ShapeDtypeStruct((1, LRU_W), F32), jax.ShapeDtypeStruct((1, LRU_W), F32),
                   jax.ShapeDtypeStruct((1, LRU_W), F32)],
        scratch_shapes=[pltpu.VMEM((1, ct), F32), pltpu.VMEM((8, ct), F32)],
        name=f"lru_bwd_l{l}", compiler_params=_params(("arbitrary", "arbitrary")))(
            proj, proj, proj, hseq, hseq, dy, conv_w, conv_b, wa, wx, ba, bx, lam)


def proj_fwd(y, w, l, tag):
    tm = 512
    k = y.shape[1]

    def body(y_ref, w_ref, o_ref):
        o_ref[...] = _dg(y_ref[...], _unpack(w_ref[...]), _NN)

    return pl.pallas_call(
        body, grid=(T // tm,),
        in_specs=[pl.BlockSpec((tm, k), lambda i: (i, 0)), pl.BlockSpec((None, k, D // 2), lambda i: (0, 0, 0))],
        out_specs=pl.BlockSpec((tm, D), lambda i: (i, 0)), out_shape=jax.ShapeDtypeStruct((T, D), F32),
        name=f"proj_{tag}_fwd_l{l}", compiler_params=_params(("arbitrary",)))(y, w)


def proj_bwd(y, dp, w, l, tag):
    tm = 512
    k = y.shape[1]

    def body(y_ref, dp_ref, w_ref, dy_ref, dw_ref):
        dp = dp_ref[...]
        dy_ref[...] = _dg(dp, _unpack(w_ref[...]), _NT)
        _acc(dw_ref, _dg(y_ref[...], dp, _TN), pl.program_id(0) == 0)

    return pl.pallas_call(
        body, grid=(T // tm,),
        in_specs=[pl.BlockSpec((tm, k), lambda i: (i, 0)), pl.BlockSpec((tm, D), lambda i: (i, 0)),
                  pl.BlockSpec((None, k, D // 2), lambda i: (0, 0, 0))],
        out_specs=[pl.BlockSpec((tm, k), lambda i: (i, 0)), pl.BlockSpec((None, k, D), lambda i: (0, 0, 0))],
        out_shape=[jax.ShapeDtypeStruct((T, k), F32), jax.ShapeDtypeStruct((1, k, D), F32)],
        name=f"proj_{tag}_bwd_l{l}", compiler_params=_params(("arbitrary",)))(y, dp, w)


OUT_TM = 256


def _out_tile(pa, pb, pc, ga, gb, gc, wout, post_g):
    merged = _sigmoid(ga) * pa + _sigmoid(gb) * pb + _sigmoid(gc) * pc
    return _rms(dot_nn(merged, wout), post_g)


def _out_in_specs():
    tm = OUT_TM
    tok = pl.BlockSpec((tm, D), lambda i: (i, 0))
    gate = lambda off: pl.BlockSpec((tm, 512), lambda i, off=off: (i, off // 512))
    return [tok, tok, tok, gate(OFF_GA), gate(OFF_GA + 512), gate(OFF_GB), gate(OFF_GB + 512), gate(OFF_GC),
            gate(OFF_GC + 512), pl.BlockSpec((None, D, D // 2), lambda i: (0, 0, 0)), pl.BlockSpec((1, D), lambda i: (0, 0))]


def _gates(refs):
    return [jnp.concatenate([refs[2 * j][...], refs[2 * j + 1][...]], axis=1) for j in range(3)]


def out_fwd(x, pa, pb, pc, proj, wout, post_g, l):
    tm = OUT_TM

    def body(pa_ref, pb_ref, pc_ref, g0, g1, g2, g3, g4, g5, w_ref, pg_ref, x_ref, o_ref):
        ga, gb, gc = _gates([g0, g1, g2, g3, g4, g5])
        o_ref[...] = x_ref[...] + _out_tile(pa_ref[...], pb_ref[...], pc_ref[...], ga, gb, gc, _unpack(w_ref[...]),
                                            pg_ref[...])

    tok = pl.BlockSpec((tm, D), lambda i: (i, 0))
    return pl.pallas_call(
        body, grid=(T // tm,), in_specs=_out_in_specs() + [tok], out_specs=tok,
        out_shape=jax.ShapeDtypeStruct((T, D), F32),
        name=f"out_fwd_l{l}", compiler_params=_params(("arbitrary",)))(
            pa, pb, pc, proj, proj, proj, proj, proj, proj, wout, post_g, x)


def out_bwd(pa, pb, pc, proj, wout, post_g, dxn, l, dep=None):
    tm = OUT_TM

    def body(pa_ref, pb_ref, pc_ref, g0, g1, g2, g3, g4, g5, w_ref, pg_ref, dxn_ref, *rest):
        dpa_ref, dpb_ref, dpc_ref, dg_ref, dw_ref, dpg_ref = rest[-6:]
        first = pl.program_id(0) == 0
        ga, gb, gc = _gates([g0, g1, g2, g3, g4, g5])
        _, vjp = jax.vjp(_out_tile, pa_ref[...], pb_ref[...], pc_ref[...], ga, gb, gc, _unpack(w_ref[...]), pg_ref[...])
        dpa, dpb, dpc, dga, dgb, dgc, dw, dpg = vjp(dxn_ref[...])
        dpa_ref[...] = dpa.astype(BF16)
        dpb_ref[...] = dpb.astype(BF16)
        dpc_ref[...] = dpc.astype(BF16)
        dg_ref[:, 0:1024] = dga.astype(BF16)
        dg_ref[:, 1024:2048] = dgb.astype(BF16)
        dg_ref[:, 2048:3072] = dgc.astype(BF16)
        _acc(dw_ref, dw, first)
        _acc(dpg_ref, dpg, first)

    tok = pl.BlockSpec((tm, D), lambda i: (i, 0))
    deps = [] if dep is None else [dep]
    return pl.pallas_call(
        body, grid=(T // tm,), in_specs=_out_in_specs() + [tok] + [ANY] * len(deps),
        out_specs=[tok, tok, tok, pl.BlockSpec((tm, 3072), lambda i: (i, 0)),
                   pl.BlockSpec((None, D, D), lambda i: (0, 0, 0)), pl.BlockSpec((1, D), lambda i: (0, 0))],
        out_shape=[jax.ShapeDtypeStruct((T, D), BF16)] * 3 + [jax.ShapeDtypeStruct((T, 3072), BF16),
                                                            jax.ShapeDtypeStruct((1, D, D), F32), jax.ShapeDtypeStruct((1, D), F32)],
        name=f"out_bwd_l{l}", compiler_params=_params(("arbitrary",)))(
            pa, pb, pc, proj, proj, proj, proj, proj, proj, wout, post_g, dxn, *deps)


def loss_head(y, target):
    tm = 256

    def body(y_ref, t_ref, loss_ref, dy_ref):
        e = y_ref[...] - t_ref[...]
        dy_ref[...] = e * (1.0 / D)
        val = 0.5 * jnp.sum(jnp.mean(e * e, axis=-1, keepdims=True), axis=0, keepdims=True)
        _acc(loss_ref, jnp.broadcast_to(val, (8, 128)), pl.program_id(0) == 0)

    tok = pl.BlockSpec((tm, D), lambda i: (i, 0))
    total, dy = pl.pallas_call(
        body, grid=(T // tm,), in_specs=[tok, tok],
        out_specs=[pl.BlockSpec((8, 128), lambda i: (0, 0)), tok],
        out_shape=[jax.ShapeDtypeStruct((8, 128), F32), jax.ShapeDtypeStruct((T, D), F32)],
        name="loss_head", compiler_params=_params(("arbitrary",)))(y, target)
    return total[0, 0], dy


def _rope_tables():
    pos = jnp.arange(T, dtype=F32)
    inv_freq = 10000.0 ** (-jnp.arange(0, 64, 2, dtype=F32) / 64)
    ang = pos[:, None] * inv_freq[None, :]
    cos, sin = jnp.cos(ang), jnp.sin(ang)
    ctab = jnp.concatenate([jnp.ones((T, 128), F32), cos, cos], axis=1)
    stab = jnp.concatenate([jnp.zeros((T, 128), F32), -sin, sin], axis=1)
    return ctab, stab


def _block_diag(w):
    w5 = w.reshape(L, 2, 8, 80, 80)
    eye = jnp.eye(8, dtype=w.dtype)
    return jnp.einsum("lnbij,bc->lnbicj", w5, eye).reshape(L, 2, LRU_TILE, LRU_TILE)


def _block_diag_t(dw):
    dw5 = dw.reshape(2, 8, 80, 8, 80)
    return jnp.einsum("nbicj,bc->nbij", dw5, jnp.eye(8, dtype=dw.dtype)).reshape(16, 80, 80)


def _layer_fwd(x, l, w, gw, tabs, dep=None):
    row = lambda a: a[l][None]
    proj, h = inproj_fwd(x, row(w["pre_norm_g"]), gw["w_in_t"], l, dep)
    ya = gmlp_fwd(proj, row(w["gm_ln_g"]), row(w["gm_ln_b"]), w["gm_ws"][l], w["gm_bs"][l][..., None], l)
    q, k, v = qkv_fwd(proj, row(w["mla_q_norm_g"]), row(w["kv_g384"]), gw["wq"], gw["wkv"], tabs[0], tabs[1], l)
    yb = attn_fwd(q, k, v, proj, l)
    hseq, yc = lru_fwd(proj, gw["conv"], row(w["lru_conv_b"]), w["wa_dense"][l], w["wx_dense"][l],
                       row(w["lru_b_a"]), row(w["lru_b_x"]), row(w["lru_lambda"]), l)
    pa = proj_fwd(ya, gw["w_proj_a"], l, "a")
    pb = proj_fwd(yb, gw["w_proj_b"], l, "b")
    pc = proj_fwd(yc, gw["w_proj_c"], l, "c")
    xn = out_fwd(x, pa, pb, pc, proj, gw["w_out"], row(w["post_norm_g"]), l)
    return xn, (x, proj, h, ya, q, k, v, yb, hseq, yc, pa, pb, pc)


def _layer_bwd(dxn, l, w, gw, tabs, saved, dep=None):
    x, proj, h, ya, q, k, v, yb, hseq, yc, pa, pb, pc = saved
    row = lambda a: a[l][None]
    g, gg = {}, {}
    dpa, dpb, dpc, dgates, gg["w_out"], dpost = out_bwd(pa, pb, pc, proj, gw["w_out"], row(w["post_norm_g"]), dxn, l, dep)
    g["post_norm_g"] = dpost[0]
    dya, gg["w_proj_a"] = proj_bwd(ya, dpa, gw["w_proj_a"], l, "a")
    dyb, gg["w_proj_b"] = proj_bwd(yb, dpb, gw["w_proj_b"], l, "b")
    dyc, gg["w_proj_c"] = proj_bwd(yc, dpc, gw["w_proj_c"], l, "c")
    dseg_a, dln_g, dln_b, g["gm_ws"], dbs = gmlp_bwd(proj, row(w["gm_ln_g"]), row(w["gm_ln_b"]), w["gm_ws"][l],
                                                    w["gm_bs"][l][..., None], dya, l)
    g["gm_ln_g"], g["gm_ln_b"], g["gm_bs"] = dln_g[0], dln_b[0], dbs[..., 0]
    dq, dk, dv, dzb = attn_bwd(q, k, v, proj, dyb, l)
    dseg_q, dqg, dkvg, dwq, dwkv = qkv_bwd(proj, row(w["mla_q_norm_g"]), row(w["kv_g384"]), gw["wq"], gw["wkv"],
                                           tabs[0], tabs[1], dq, dk, dv, l)
    gg["wq"], gg["wkv"] = dwq.reshape(1, 1536, 384), dwkv.reshape(1, 2048, 256)
    g["mla_q_norm_g"], g["mla_kv_norm_g"] = dqg[0], dkvg[0, :256]
    dxc, dzc, dcw, dcb, dwa, dwx, dba, dbx, dlam = lru_bwd(
        proj, hseq, dyc, gw["conv"], row(w["lru_conv_b"]), w["wa_dense"][l], w["wx_dense"][l],
        row(w["lru_b_a"]), row(w["lru_b_x"]), row(w["lru_lambda"]), l)
    gg["conv"] = jnp.pad(dcw.T, ((0, 0), (0, 124)))[None]
    g["lru_conv_b"], g["lru_b_a"], g["lru_b_x"], g["lru_lambda"] = dcb[0], dba[0], dbx[0], dlam[0]
    g["lru_w_a"], g["lru_w_x"] = _block_diag_t(dwa), _block_diag_t(dwx)
    dproj = jnp.concatenate([dseg_a, dseg_q, dzb, jnp.zeros((T, PAD2), dzb.dtype), dxc, dzc, dgates], axis=1)
    gg["w_in_t"], dh = inproj_bwd(dproj, h, gw["w_in_t"], l)
    dx, dpre = prenorm_bwd(x, row(w["pre_norm_g"]), dh, dxn, l)
    g["pre_norm_g"] = dpre[0]
    return dx, gg, g


MESH = pl.DeviceIdType.MESH
HBM = pl.BlockSpec(memory_space=pltpu.HBM)
SEM = pl.BlockSpec(memory_space=pltpu.SEMAPHORE)
EFFECT = pltpu.SideEffectType.DATAFLOW_SIDE_EFFECTING
FLIPS = ((1, 0), (0, 1), (1, 1))


def _win_off(k, s):
    g = SHARD * k + s
    return g + jnp.where(g >= PAD1_AT, PAD1, 0) + jnp.where(g >= PAD2_AT, PAD2, 0)


def _plain_off(rows):
    return lambda k, s: rows * k + s


class Spec:
    def __init__(self, rows, cols, full_rows, pieces=None, off=None, layers=1, packed=None):
        self.rows, self.cols, self.full_rows, self.layers = rows, cols, full_rows, layers
        self.pieces = pieces or ((0, rows),)
        self.off = off or _plain_off(rows)
        self.packed = cols % 256 == 0 if packed is None else packed
        self.wcols = cols // 2 if self.packed else cols

    def to_words(self, a):
        return _pack(a) if self.packed else a

    def from_words(self, p):
        return _unpack(p) if self.packed else p


def _pack(a):
    bits = lambda v: lax.bitcast_convert_type(v.astype(jnp.bfloat16).astype(F32), jnp.uint32)
    words = [(bits(a[:, g:g + 128]) >> 16) | (bits(a[:, g + 128:g + 256]) & jnp.uint32(0xFFFF0000))
             for g in range(0, a.shape[-1], 256)]
    return lax.bitcast_convert_type(jnp.concatenate(words, axis=-1) if len(words) > 1 else words[0], F32)


def _unpack(p):
    w = lax.bitcast_convert_type(p, jnp.uint32)
    lo = lax.bitcast_convert_type(w << 16, F32)
    hi = lax.bitcast_convert_type(w & jnp.uint32(0xFFFF0000), F32)
    return jnp.concatenate([h[:, g:g + 128] for g in range(0, p.shape[-1], 128) for h in (lo, hi)], axis=-1)


WEIGHT_SPECS = {
    "w_in_t": Spec(SHARD, D, NPAD, WIN_PIECES, _win_off),
    "wq": Spec(192, 384, 1536),
    "wkv": Spec(256, 256, 2048),
    "conv": Spec(160, 128, 1280),
    "w_proj_a": Spec(128, D, 1024),
    "w_proj_b": Spec(128, D, 1024),
    "w_proj_c": Spec(160, D, 1280),
    "w_out": Spec(128, D, 1024),
}
REP_ROWS = 72
REP_SPEC = Spec(REP_ROWS, D, REP_ROWS * NDEV, packed=False)


def _coords():
    return lax.axis_index("x"), lax.axis_index("y"), lax.axis_index("c")


def _rows(ref, start, n):
    if not isinstance(start, int):
        start = pl.multiple_of(start, 8)
    return ref.at[:, pl.ds(start, n), :]


def _col_tile(cols):
    return 256 if cols % 256 == 0 else cols


def _n_pieces(specs):
    return sum(len(sp.pieces) for sp in specs)


def pack_place(shard, sp, layer, tag):
    gaps = ((PAD1_AT, PAD1), (PAD2_AT + PAD1, PAD2)) if sp.off is _win_off else ()
    npc = len(sp.pieces)

    def body(s_ref, words_ref, full_ref, buf, zbuf, sem):
        l = 0
        x, y, c = _coords()
        me = 4 * x + 2 * y + c
        words = sp.to_words(s_ref[...])
        words_ref[...] = words
        buf[...] = words
        copies = [pltpu.make_async_copy(buf.at[pl.ds(s, n), :],
                                        full_ref.at[l, pl.ds(pl.multiple_of(sp.off(me, s), 8), n), :], sem.at[i])
                  for i, (s, n) in enumerate(sp.pieces)]
        if gaps:
            zbuf[...] = jnp.zeros_like(zbuf)
            copies += [pltpu.make_async_copy(zbuf.at[pl.ds(0, n), :], full_ref.at[l, pl.ds(at, n), :], sem.at[npc + i])
                       for i, (at, n) in enumerate(gaps)]
        for cp in copies:
            cp.start()
        for cp in copies:
            cp.wait()

    return pl.pallas_call(
        body, grid=(1,), in_specs=[pl.BlockSpec((None, sp.rows, sp.cols), lambda i: (layer, 0, 0))],
        out_specs=[pl.BlockSpec((None, sp.rows, sp.wcols), lambda i: (0, 0, 0)), ANY],
        out_shape=[jax.ShapeDtypeStruct((sp.layers, sp.rows, sp.wcols), F32),
                   jax.ShapeDtypeStruct((sp.layers, sp.full_rows, sp.wcols), F32)],
        scratch_shapes=[pltpu.VMEM((sp.rows, sp.wcols), F32), pltpu.VMEM((PAD2 if gaps else 8, sp.wcols), F32),
                        pltpu.SemaphoreType.DMA((npc + len(gaps),))],
        name=f"pack_place_{tag}", compiler_params=_params(("arbitrary",)))(shard)


def _gather_copies(srcs, bufs, specs, ssem, rsem, landing):
    x, y, c = _coords()
    me = 4 * x + 2 * y + c
    targets = [(x, y, 1 - c)] + [(x ^ fx, y ^ fy, c) for fx, fy in FLIPS]
    copies = []
    p = 0
    for src, buf, sp in zip(srcs, bufs, specs):
        for s, n in sp.pieces:
            for t, (tx, ty, tc) in enumerate(targets):
                owner = 4 * tx + 2 * ty + tc if landing else me
                copies.append(pltpu.make_async_remote_copy(_rows(src, s, n), _rows(buf, sp.off(owner, s), n),
                                                           ssem.at[4 * p + t], rsem.at[4 * p + t],
                                                           device_id=(tx, ty, tc), device_id_type=MESH))
            p += 1
    return copies


def gather_send(words, fulls, specs, tag):
    ns, npc = len(specs), _n_pieces(specs)

    def body(*refs):
        srcs, bufs, sems = refs[:ns], refs[2 * ns:3 * ns], refs[3 * ns:]
        for cp in _gather_copies(srcs, bufs, specs, *sems, False):
            cp.start()
        for cp in _gather_copies(srcs, bufs, specs, *sems, False):
            cp.wait_send()
        for cp in _gather_copies(srcs, bufs, specs, *sems, True):
            cp.wait_recv()

    return pl.pallas_call(
        body, in_specs=[ANY] * (2 * ns), out_specs=[ANY] * ns,
        out_shape=[jax.ShapeDtypeStruct(f.shape, f.dtype) for f in fulls],
        input_output_aliases={ns + i: i for i in range(ns)},
        scratch_shapes=[pltpu.SemaphoreType.DMA((4 * npc,)), pltpu.SemaphoreType.DMA((4 * npc,))],
        name=f"gather_send_{tag}", compiler_params=pltpu.CompilerParams(has_side_effects=True))(*words, *fulls)


def _in_hbm(arrays):
    return [pltpu.with_memory_space_constraint(a, pltpu.HBM) for a in arrays]


def gather_start(words, fulls, specs, dep, tag):
    ns, npc = len(specs), _n_pieces(specs)

    def body(*refs):
        ssem, rsem = refs[2 * ns + 1:2 * ns + 3]
        for cp in _gather_copies(refs[:ns], refs[ns:2 * ns], specs, ssem, rsem, False):
            cp.start()
        refs[-1][...] = jnp.zeros_like(refs[-1])

    outs = pl.pallas_call(
        body, in_specs=[HBM] * (2 * ns) + [ANY],
        out_specs=[SEM, SEM] + [HBM] * (2 * ns) + [pl.BlockSpec(memory_space=pltpu.VMEM)],
        out_shape=[pltpu.SemaphoreType.DMA((4 * npc,)), pltpu.SemaphoreType.DMA((4 * npc,))]
        + [pltpu.HBM(a.shape, a.dtype) for a in list(words) + list(fulls)] + [jax.ShapeDtypeStruct((8, 128), F32)],
        input_output_aliases={i: 2 + i for i in range(2 * ns)},
        name=f"gather_start_{tag}", compiler_params=pltpu.CompilerParams(has_side_effects=EFFECT))(
            *_in_hbm(list(words) + list(fulls)), dep)
    return outs[0], outs[1], outs[2:2 + ns], outs[2 + ns:2 + 2 * ns], outs[-1]


def gather_wait(ssem, rsem, words, fulls, specs, after, tag):
    ns = len(specs)

    def body(*refs):
        srcs, bufs, ssem, rsem = refs[:ns], refs[ns:2 * ns], refs[2 * ns], refs[2 * ns + 1]
        for cp in _gather_copies(srcs, bufs, specs, ssem, rsem, False):
            cp.wait_send()
        for cp in _gather_copies(srcs, bufs, specs, ssem, rsem, True):
            cp.wait_recv()

    outs = pl.pallas_call(
        body, in_specs=[HBM] * (2 * ns) + [SEM, SEM, ANY], out_specs=[HBM] * (2 * ns),
        out_shape=[pltpu.HBM(a.shape, a.dtype) for a in list(words) + list(fulls)],
        input_output_aliases={i: i for i in range(2 * ns)},
        name=f"gather_wait_{tag}", compiler_params=pltpu.CompilerParams(has_side_effects=EFFECT))(
            *words, *fulls, ssem, rsem, after)
    return outs[ns:]


def gather_forward(fulls, specs, tag):
    ns, npc = len(specs), _n_pieces(specs)

    def body(*refs):
        bufs = refs[ns:2 * ns]
        ssem, rsem = refs[2 * ns:]
        x, y, c = _coords()
        sibling = (x, y, 1 - c)
        waits = []
        p = 0
        for buf, sp in zip(bufs, specs):
            for s, n in sp.pieces:
                for t, (fx, fy) in enumerate(FLIPS):
                    chip = 4 * (x ^ fx) + 2 * (y ^ fy)
                    here = _rows(buf, sp.off(chip + c, s), n)
                    send = pltpu.make_async_remote_copy(here, here, ssem.at[t, p], rsem.at[t, p],
                                                        device_id=sibling, device_id_type=MESH)
                    send.start()
                    waits.append(send.wait_send)
                    there = _rows(buf, sp.off(chip + 1 - c, s), n)
                    waits.append(pltpu.make_async_remote_copy(here, there, ssem.at[t, p], rsem.at[t, p],
                                                              device_id=sibling, device_id_type=MESH).wait_recv)
                p += 1
        for w in waits:
            w()

    return pl.pallas_call(
        body, in_specs=[ANY] * ns, out_specs=[ANY] * ns,
        out_shape=[jax.ShapeDtypeStruct(f.shape, f.dtype) for f in fulls],
        input_output_aliases={i: i for i in range(ns)},
        scratch_shapes=[pltpu.SemaphoreType.DMA((3, npc)), pltpu.SemaphoreType.DMA((3, npc))],
        name=f"gather_forward_{tag}", compiler_params=pltpu.CompilerParams(has_side_effects=True))(*fulls)


def all_gather(shards, layer, specs, names, tag):
    placed = [pack_place(s, sp, layer, f"{tag}_{n}") for s, sp, n in zip(shards, specs, names)]
    fulls = gather_send([p[0] for p in placed], [p[1] for p in placed], specs, tag)
    return gather_forward(fulls, specs, tag)


def reduce_pair(grads, specs, tag):
    ns, npc = len(specs), _n_pieces(specs)

    def body(*refs):
        srcs, theirs = refs[:ns], refs[ns:2 * ns]
        ssem, rsem = refs[2 * ns:]
        x, y, c = _coords()
        sibling = (x, y, 1 - c)
        waits = []
        p = 0
        for src, their, sp in zip(srcs, theirs, specs):
            for s, n in sp.pieces:
                for j in range(4):
                    send = pltpu.make_async_remote_copy(_rows(src, sp.off(2 * j + 1 - c, s), n), _rows(their.at[j], s, n),
                                                        ssem.at[j, p], rsem.at[j, p], device_id=sibling, device_id_type=MESH)
                    send.start()
                    waits.append(send.wait)
                p += 1
        for w in waits:
            w()

    return pl.pallas_call(
        body, in_specs=[ANY] * ns, out_specs=[ANY] * ns,
        out_shape=[jax.ShapeDtypeStruct((4, sp.layers, sp.rows, sp.cols), F32) for sp in specs],
        scratch_shapes=[pltpu.SemaphoreType.DMA((4, npc)), pltpu.SemaphoreType.DMA((4, npc))],
        name=f"reduce_pair_{tag}", compiler_params=pltpu.CompilerParams(has_side_effects=True))(*grads)


def pair_sum(g, r1, sp, tag):
    npc = len(sp.pieces)

    def body(g_ref, r_ref, own_ref, words_ref, gbuf, sem):
        l, j = pl.program_id(0), pl.program_id(1)
        x, y, c = _coords()
        copies = [pltpu.make_async_copy(g_ref.at[l, pl.ds(pl.multiple_of(sp.off(2 * j + c, s), 8), n), :],
                                        gbuf.at[pl.ds(s, n), :], sem.at[i]) for i, (s, n) in enumerate(sp.pieces)]
        for cp in copies:
            cp.start()
        for cp in copies:
            cp.wait()
        p = gbuf[...] + r_ref[...]
        words_ref[...] = sp.to_words(p)

        @pl.when(j == 2 * x + y)
        def _():
            own_ref[...] = p

    return pl.pallas_call(
        body, grid=(sp.layers, 4),
        in_specs=[ANY, pl.BlockSpec((None, None, sp.rows, sp.cols), lambda l, j: (j, l, 0, 0))],
        out_specs=[pl.BlockSpec((None, sp.rows, sp.cols), lambda l, j: (l, 0, 0)),
                   pl.BlockSpec((None, None, sp.rows, sp.wcols), lambda l, j: (j, l, 0, 0))],
        out_shape=[jax.ShapeDtypeStruct((sp.layers, sp.rows, sp.cols), F32),
                   jax.ShapeDtypeStruct((4, sp.layers, sp.rows, sp.wcols), F32)],
        scratch_shapes=[pltpu.VMEM((sp.rows, sp.cols), F32), pltpu.SemaphoreType.DMA((npc,))],
        name=f"pair_sum_{tag}", compiler_params=_params(("arbitrary", "arbitrary")))(g, r1)


def _chip_copies(srcs, dsts, ssem, rsem):
    x, y, c = _coords()
    copies = []
    for i, (src, dst) in enumerate(zip(srcs, dsts)):
        for t, (fx, fy) in enumerate(FLIPS):
            tx, ty = x ^ fx, y ^ fy
            copies.append(pltpu.make_async_remote_copy(src.at[2 * tx + ty], dst.at[t], ssem.at[3 * i + t], rsem.at[3 * i + t],
                                                       device_id=(tx, ty, c), device_id_type=MESH))
    return copies


def _slot_shapes(words):
    return [(3,) + w.shape[1:] for w in words]


def reduce_chips(words, specs, tag):
    ns = len(specs)

    def body(*refs):
        copies = _chip_copies(refs[:ns], refs[ns:2 * ns], *refs[2 * ns:])
        for cp in copies:
            cp.start()
        for cp in copies:
            cp.wait()

    return pl.pallas_call(
        body, in_specs=[ANY] * ns, out_specs=[ANY] * ns,
        out_shape=[jax.ShapeDtypeStruct(s, F32) for s in _slot_shapes(words)],
        scratch_shapes=[pltpu.SemaphoreType.DMA((3 * ns,)), pltpu.SemaphoreType.DMA((3 * ns,))],
        name=f"reduce_chips_{tag}", compiler_params=pltpu.CompilerParams(has_side_effects=True))(*words)


def chips_start(words, specs, tag):
    ns = len(specs)
    slots = [lax.empty(s, F32) for s in _slot_shapes(words)]

    def body(*refs):
        ssem, rsem = refs[2 * ns:2 * ns + 2]
        for cp in _chip_copies(refs[:ns], refs[ns:2 * ns], ssem, rsem):
            cp.start()
        refs[-1][...] = jnp.zeros_like(refs[-1])

    outs = pl.pallas_call(
        body, in_specs=[HBM] * (2 * ns),
        out_specs=[SEM, SEM] + [HBM] * (2 * ns) + [pl.BlockSpec(memory_space=pltpu.VMEM)],
        out_shape=[pltpu.SemaphoreType.DMA((3 * ns,)), pltpu.SemaphoreType.DMA((3 * ns,))]
        + [pltpu.HBM(a.shape, a.dtype) for a in list(words) + slots] + [jax.ShapeDtypeStruct((8, 128), F32)],
        input_output_aliases={i: 2 + i for i in range(2 * ns)},
        name=f"chips_start_{tag}", compiler_params=pltpu.CompilerParams(has_side_effects=EFFECT))(
            *_in_hbm(list(words) + slots))
    return outs[0], outs[1], outs[2:2 + ns], outs[2 + ns:2 + 2 * ns], outs[-1]


def chips_wait(ssem, rsem, words, slots, specs, after, tag):
    ns = len(specs)

    def body(*refs):
        for cp in _chip_copies(refs[:ns], refs[ns:2 * ns], refs[2 * ns], refs[2 * ns + 1]):
            cp.wait_send()
            cp.wait_recv()

    outs = pl.pallas_call(
        body, in_specs=[HBM] * (2 * ns) + [SEM, SEM, ANY], out_specs=[HBM] * (2 * ns),
        out_shape=[pltpu.HBM(a.shape, a.dtype) for a in list(words) + list(slots)],
        input_output_aliases={i: i for i in range(2 * ns)},
        name=f"chips_wait_{tag}", compiler_params=pltpu.CompilerParams(has_side_effects=EFFECT))(
            *words, *slots, ssem, rsem, after)
    return outs[ns:]


def sum_chips(own, r2, sp, tag):
    def body(own_ref, r_ref, o_ref):
        o_ref[...] = ((own_ref[...] + sp.from_words(r_ref[0])) + sp.from_words(r_ref[1])) + sp.from_words(r_ref[2])

    blk = pl.BlockSpec((None, sp.rows, sp.cols), lambda l: (l, 0, 0))
    return pl.pallas_call(
        body, grid=(sp.layers,), in_specs=[blk, pl.BlockSpec((3, None, sp.rows, sp.wcols), lambda l: (0, l, 0, 0))],
        out_specs=blk, out_shape=jax.ShapeDtypeStruct((sp.layers, sp.rows, sp.cols), F32),
        name=f"sum_chips_{tag}", compiler_params=_params(("arbitrary",)))(own, r2)


def reduce_scatter_start(grads, specs, names, tag):
    theirs = reduce_pair(grads, specs, tag)
    sums = [pair_sum(g, r1, sp, f"{tag}_{n}") for g, r1, sp, n in zip(grads, theirs, specs, names)]
    ssem, rsem, words, slots, token = chips_start([s[1] for s in sums], specs, tag)
    return (ssem, rsem, words, slots, [s[0] for s in sums]), token


def reduce_scatter_finish(state, after, specs, tag):
    ssem, rsem, words, slots, own = state
    return list(zip(own, chips_wait(ssem, rsem, words, slots, specs, after, tag)))


def reduce_scatter(grads, specs, names, tag):
    theirs = reduce_pair(grads, specs, tag)
    sums = [pair_sum(g, r1, sp, f"{tag}_{n}") for g, r1, sp, n in zip(grads, theirs, specs, names)]
    return list(zip([s[0] for s in sums], reduce_chips([s[1] for s in sums], specs, tag)))


def _adamw_math(w, g, m, v):
    c1 = 1.0 - ADAM_B1 ** ADAM_STEP
    c2 = 1.0 - ADAM_B2 ** ADAM_STEP
    m2 = ADAM_B1 * m + (1.0 - ADAM_B1) * g
    v2 = ADAM_B2 * v + (1.0 - ADAM_B2) * (g * g)
    return -ADAM_LR * ((m2 / c1) / (jnp.sqrt(v2 / c2) + ADAM_EPS) + ADAM_WD * w), m2, v2


def adamw(w, g, m, v, name):
    shape = w.shape
    cols = shape[-1]
    rows = math.prod(shape[:-1])
    tr = rows
    while tr * cols * 4 > (1 << 20) and tr % 16 == 0:
        tr //= 2

    def body(w_ref, g_ref, m_ref, v_ref, d_ref, nm_ref, nv_ref):
        d_ref[...], nm_ref[...], nv_ref[...] = _adamw_math(w_ref[...], g_ref[...], m_ref[...], v_ref[...])

    blk = pl.BlockSpec((tr, cols), lambda i: (i, 0))
    outs = pl.pallas_call(
        body, grid=(rows // tr,), in_specs=[blk] * 4, out_specs=[blk] * 3,
        out_shape=[jax.ShapeDtypeStruct((rows, cols), F32)] * 3,
        name=f"adamw_{name}", compiler_params=_params(("arbitrary",)))(
            *[a.reshape(rows, cols) for a in (w, g, m, v)])
    return [o.reshape(shape) for o in outs]


def adamw_layers(w, sums, m, v, sp, name):
    _, rows, cols = w.shape
    tc = _col_tile(cols)
    twc = tc // 2 if sp.packed else tc

    def body(w_ref, own0, r0, own1, r1, m_ref, v_ref, g_ref, d_ref, nm_ref, nv_ref):
        first = pl.program_id(0) == 0
        own = jnp.where(first, own0[...], own1[...])
        r = [sp.from_words(jnp.where(first, r0[t], r1[t])) for t in range(3)]
        g = ((own + r[0]) + r[1]) + r[2]
        g_ref[...] = g
        d_ref[...], nm_ref[...], nv_ref[...] = _adamw_math(w_ref[...], g, m_ref[...], v_ref[...])

    blk = pl.BlockSpec((None, rows, tc), lambda l, n: (l, 0, n))
    own = pl.BlockSpec((None, rows, tc), lambda l, n: (0, 0, n))
    slots = pl.BlockSpec((3, None, rows, twc), lambda l, n: (0, 0, 0, n))
    return pl.pallas_call(
        body, grid=(L, cols // tc), in_specs=[blk, own, slots, own, slots, blk, blk], out_specs=[blk] * 4,
        out_shape=[jax.ShapeDtypeStruct(w.shape, F32)] * 4,
        name=f"adamw_{name}", compiler_params=_params(("arbitrary", "arbitrary")))(
            w, sums[0][0], sums[0][1], sums[1][0], sums[1][1], m, v)


WEIGHTS = ("pre_norm_g", "w_in", "gm_ln_g", "gm_ln_b", "gm_ws", "gm_bs", "mla_q_norm_g", "mla_w_uq", "mla_kv_norm_g",
           "mla_w_ukv", "lru_conv_w", "lru_conv_b", "lru_w_a", "lru_b_a", "lru_w_x", "lru_b_x", "lru_lambda",
           "w_proj_a", "w_proj_b", "w_proj_c", "w_out", "post_norm_g")
SHARDED = ("w_in", "mla_w_uq", "mla_w_ukv", "lru_conv_w", "w_proj_a", "w_proj_b", "w_proj_c", "w_out")
REPLICATED = tuple(n for n in WEIGHTS if n not in SHARDED)


def _step(x, target, wts, ms, vs):
    t12 = lambda a: jnp.swapaxes(a, 1, 2)
    names = list(WEIGHT_SPECS)
    specs = [WEIGHT_SPECS[n] for n in names]
    tabs = _rope_tables()
    own = {"w_in_t": t12(wts["w_in"]), "wq": t12(wts["mla_w_uq"]), "wkv": t12(wts["mla_w_ukv"]),
           "conv": jnp.pad(t12(wts["lru_conv_w"]), ((0, 0), (0, 0), (0, 124))),
           "w_proj_a": wts["w_proj_a"], "w_proj_b": wts["w_proj_b"], "w_proj_c": wts["w_proj_c"], "w_out": wts["w_out"]}
    shards = [own[n] for n in names]

    w = {n: wts[n] for n in REPLICATED}
    w["kv_g384"] = jnp.concatenate([wts["mla_kv_norm_g"], jnp.ones((L, 128), F32)], axis=1)
    w["wa_dense"] = _block_diag(wts["lru_w_a"])
    w["wx_dense"] = _block_diag(wts["lru_w_x"])

    def layer_weights(words):
        gw = dict(zip(names, words))
        gw["wq"] = gw["wq"].reshape(HEADS, 192, 384)
        gw["wkv"] = gw["wkv"].reshape(HEADS, 256, 128)
        gw["conv"] = gw["conv"][0, :, :4].T
        return gw

    words0 = all_gather(shards, 0, specs, names, "w0")
    placed1 = [pack_place(s, sp, 1, f"w1_{n}") for s, sp, n in zip(shards, specs, names)]
    ssem, rsem, wthru, fthru, token = gather_start([p[0] for p in placed1], [p[1] for p in placed1], specs, words0[0], "w1")
    gw0 = layer_weights(words0)
    x1, saved0 = _layer_fwd(x, 0, w, gw0, tabs, dep=token)
    words1 = gather_forward(gather_wait(ssem, rsem, wthru, fthru, specs, x1, "w1"), specs, "w1")
    gw1 = layer_weights(words1)
    x2, saved1 = _layer_fwd(x1, 1, w, gw1, tabs)
    loss, dx2 = loss_head(x2, target)

    dx1, gg1, g1 = _layer_bwd(dx2, 1, w, gw1, tabs, saved1)
    state1, token1 = reduce_scatter_start([gg1[n] for n in names], specs, names, "g1")
    dx0, gg0, g0 = _layer_bwd(dx1, 0, w, gw0, tabs, saved0, dep=token1)
    s1 = dict(zip(names, reduce_scatter_finish(state1, dx0, specs, "g1")))

    rep_flat = jnp.concatenate([jnp.stack([g0[n], g1[n]]).reshape(-1) for n in REPLICATED])
    rep_flat = jnp.pad(rep_flat, (0, REP_ROWS * NDEV * D - rep_flat.shape[0])).reshape(1, REP_ROWS * NDEV, D)
    summed0 = reduce_scatter([gg0[n] for n in names] + [rep_flat], specs + [REP_SPEC], names + ["rep"], "g0")
    s0 = dict(zip(names, summed0[:-1]))
    rep_sum = sum_chips(*summed0[-1], REP_SPEC, "rep")
    rep_full = all_gather([rep_sum], 0, [REP_SPEC], ["rep"], "rep")[0].reshape(-1)

    out = {}
    for n, key in (("w_in", "w_in_t"), ("mla_w_uq", "wq"), ("mla_w_ukv", "wkv")):
        res = adamw_layers(own[key], [s0[key], s1[key]], t12(ms[n]), t12(vs[n]), WEIGHT_SPECS[key], n)
        out[n] = [t12(r) for r in res]
    for n in ("w_proj_a", "w_proj_b", "w_proj_c", "w_out"):
        out[n] = adamw_layers(wts[n], [s0[n], s1[n]], ms[n], vs[n], WEIGHT_SPECS[n], n)
    conv_sp = WEIGHT_SPECS["conv"]
    g_conv = t12(jnp.concatenate([sum_chips(*s0["conv"], conv_sp, "conv0"), sum_chips(*s1["conv"], conv_sp, "conv1")])[:, :, :4])
    out["lru_conv_w"] = [g_conv] + adamw(wts["lru_conv_w"], g_conv, ms["lru_conv_w"], vs["lru_conv_w"], "lru_conv_w")
    at = 0
    for n in REPLICATED:
        size = math.prod(wts[n].shape)
        g = rep_full[at:at + size].reshape(wts[n].shape)
        out[n] = [g] + adamw(wts[n], g, ms[n], vs[n], n)
        at += size

    loss = lax.psum(loss, ("x", "y", "c"))
    return (loss, dx0[None], *[out[n][k] for k in range(4) for n in WEIGHTS])


def kernel(x, pre_norm_g, w_in, gm_ln_g, gm_ln_b, gm_ws, gm_bs, mla_q_norm_g, mla_w_uq, mla_kv_norm_g, mla_w_ukv, lru_conv_w, lru_conv_b, lru_w_a, lru_b_a, lru_w_x, lru_b_x, lru_lambda, w_proj_a, w_proj_b, w_proj_c, w_out, post_norm_g, loss_target, m_pre_norm_g, m_w_in, m_gm_ln_g, m_gm_ln_b, m_gm_ws, m_gm_bs, m_mla_q_norm_g, m_mla_w_uq, m_mla_kv_norm_g, m_mla_w_ukv, m_lru_conv_w, m_lru_conv_b, m_lru_w_a, m_lru_b_a, m_lru_w_x, m_lru_b_x, m_lru_lambda, m_w_proj_a, m_w_proj_b, m_w_proj_c, m_w_out, m_post_norm_g, v_pre_norm_g, v_w_in, v_gm_ln_g, v_gm_ln_b, v_gm_ws, v_gm_bs, v_mla_q_norm_g, v_mla_w_uq, v_mla_kv_norm_g, v_mla_w_ukv, v_lru_conv_w, v_lru_conv_b, v_lru_w_a, v_lru_b_a, v_lru_w_x, v_lru_b_x, v_lru_lambda, v_w_proj_a, v_w_proj_b, v_w_proj_c, v_w_out, v_post_norm_g):
    wts = dict(zip(WEIGHTS, (pre_norm_g, w_in, gm_ln_g, gm_ln_b, gm_ws, gm_bs, mla_q_norm_g, mla_w_uq, mla_kv_norm_g,
                             mla_w_ukv, lru_conv_w, lru_conv_b, lru_w_a, lru_b_a, lru_w_x, lru_b_x, lru_lambda,
                             w_proj_a, w_proj_b, w_proj_c, w_out, post_norm_g)))
    ms = dict(zip(WEIGHTS, (m_pre_norm_g, m_w_in, m_gm_ln_g, m_gm_ln_b, m_gm_ws, m_gm_bs, m_mla_q_norm_g, m_mla_w_uq,
                            m_mla_kv_norm_g, m_mla_w_ukv, m_lru_conv_w, m_lru_conv_b, m_lru_w_a, m_lru_b_a, m_lru_w_x,
                            m_lru_b_x, m_lru_lambda, m_w_proj_a, m_w_proj_b, m_w_proj_c, m_w_out, m_post_norm_g)))
    vs = dict(zip(WEIGHTS, (v_pre_norm_g, v_w_in, v_gm_ln_g, v_gm_ln_b, v_gm_ws, v_gm_bs, v_mla_q_norm_g, v_mla_w_uq,
                            v_mla_kv_norm_g, v_mla_w_ukv, v_lru_conv_w, v_lru_conv_b, v_lru_w_a, v_lru_b_a, v_lru_w_x,
                            v_lru_b_x, v_lru_lambda, v_w_proj_a, v_w_proj_b, v_w_proj_c, v_w_out, v_post_norm_g)))
    return _step(x[0], loss_target[0], wts, ms, vs)
```

```python
import functools
import math

import jax
import jax.numpy as jnp
from jax import lax
from jax.experimental import pallas as pl
from jax.experimental.pallas import tpu as pltpu

F32 = jnp.float32
BF16 = jnp.bfloat16

T = 2048
D = 1024
L = 2
NDEV = 8
EPS = 1e-6
CHUNK_SHIFT = 6
HEADS = 8
QK = 192
LRU_W = 1280
LRU_TILE = 640
N_IN = 10432
SHARD = N_IN // NDEV
OFF_U, OFF_V, OFF_ZA, OFF_CQ, OFF_CKV, OFF_ZB = 0, 1024, 2048, 3072, 3456, 3840
OFF_XC, OFF_ZC, OFF_GA, OFF_GB, OFF_GC = 5120, 6400, 7680, 8704, 9728
NPAD = 10752
PAD1_AT, PAD1 = 3776, 64
PAD2_AT, PAD2 = 4800, 256
WIN_PIECES = ((0, 888), (888, 280), (1168, 136))
VMEM_LIMIT = 60 * 1024 * 1024

ADAM_LR, ADAM_B1, ADAM_B2, ADAM_EPS, ADAM_WD, ADAM_STEP = 0.001, 0.9, 0.999, 1e-08, 0.01, 10

_NN = (((1,), (0,)), ((), ()))
_NT = (((1,), (1,)), ((), ()))
_TN = (((0,), (0,)), ((), ()))


def _dg(a, b, dims):
    return lax.dot_general(a.astype(BF16), b.astype(BF16), dims, preferred_element_type=F32)


@jax.custom_vjp
def dot_nn(a, b):
    return _dg(a, b, _NN)


def _nn_fwd(a, b):
    return _dg(a, b, _NN), (a, b)


def _nn_bwd(res, g):
    a, b = res
    return _dg(g, b, _NT).astype(a.dtype), _dg(a, g, _TN).astype(b.dtype)


dot_nn.defvjp(_nn_fwd, _nn_bwd)


@jax.custom_vjp
def dot_nt(a, b):
    return _dg(a, b, _NT)


def _nt_fwd(a, b):
    return _dg(a, b, _NT), (a, b)


def _nt_bwd(res, g):
    a, b = res
    return _dg(g, b, _NN).astype(a.dtype), _dg(g, a, _TN).astype(b.dtype)


dot_nt.defvjp(_nt_fwd, _nt_bwd)


def _params(sem=None):
    return pltpu.CompilerParams(dimension_semantics=sem, vmem_limit_bytes=VMEM_LIMIT)


def _sigmoid(x):
    return 1.0 / (1.0 + jnp.exp(-x))


def _silu(x):
    return x * _sigmoid(x)


def _rms(x, g):
    ms = jnp.mean(x * x, axis=-1, keepdims=True)
    return x * lax.rsqrt(ms + EPS) * g


def _acc(ref, val, first):
    @pl.when(first)
    def _():
        ref[...] = val

    @pl.when(jnp.logical_not(first))
    def _():
        ref[...] += val


ANY = pl.BlockSpec(memory_space=pl.ANY)


INPROJ_TN = 512


def inproj_fwd(x, g, wt, l, dep=None):
    tn = INPROJ_TN

    def body(x_ref, g_ref, w_ref, *rest):
        proj_ref, h_ref = rest[-2:]

        @pl.when(pl.program_id(0) == 0)
        def _():
            h_ref[...] = _rms(x_ref[...], g_ref[...]).astype(BF16)

        proj_ref[...] = lax.dot_general(h_ref[...], _unpack(w_ref[...]).astype(BF16), _NT, preferred_element_type=F32)

    deps = [] if dep is None else [dep]
    return pl.pallas_call(
        body, grid=(NPAD // tn,),
        in_specs=[pl.BlockSpec((T, D), lambda j: (0, 0)), pl.BlockSpec((1, D), lambda j: (0, 0)),
                  pl.BlockSpec((None, tn, D // 2), lambda j: (0, j, 0))] + [ANY] * len(deps),
        out_specs=[pl.BlockSpec((T, tn), lambda j: (0, j)), pl.BlockSpec((T, D), lambda j: (0, 0))],
        out_shape=[jax.ShapeDtypeStruct((T, NPAD), F32), jax.ShapeDtypeStruct((T, D), BF16)],
        name=f"inproj_fwd_l{l}", compiler_params=_params(("arbitrary",)))(x, g, wt, *deps)


def inproj_bwd(dproj, h, wt, l, dep=None):
    tn = INPROJ_TN
    deps = [] if dep is None else [dep]

    def body(dp_ref, h_ref, w_ref, *rest):
        dwt_ref, dh_ref = rest[-2:]
        dp = dp_ref[...]
        dwt_ref[...] = lax.dot_general(dp, h_ref[...], _TN, preferred_element_type=F32)
        contrib = lax.dot_general(dp, _unpack(w_ref[...]).astype(BF16), _NN, preferred_element_type=F32)
        _acc(dh_ref, contrib, pl.program_id(0) == 0)

    return pl.pallas_call(
        body, grid=(NPAD // tn,),
        in_specs=[pl.BlockSpec((T, tn), lambda j: (0, j)), pl.BlockSpec((T, D), lambda j: (0, 0)),
                  pl.BlockSpec((None, tn, D // 2), lambda j: (0, j, 0))] + [ANY] * len(deps),
        out_specs=[pl.BlockSpec((None, tn, D), lambda j: (0, j, 0)), pl.BlockSpec((T, D), lambda j: (0, 0))],
        out_shape=[jax.ShapeDtypeStruct((1, NPAD, D), F32), jax.ShapeDtypeStruct((T, D), F32)],
        name=f"inproj_bwd_l{l}", compiler_params=_params(("arbitrary",)))(dproj, h, wt, *deps)


def prenorm_bwd(x, g, dh, dxn, l):
    tm = 256

    def body(x_ref, g_ref, dh_ref, dxn_ref, dx_ref, dg_ref):
        _, vjp = jax.vjp(_rms, x_ref[...], g_ref[...])
        dx, dg = vjp(dh_ref[...])
        dx_ref[...] = dx + dxn_ref[...]
        _acc(dg_ref, dg, pl.program_id(0) == 0)

    tok = pl.BlockSpec((tm, D), lambda i: (i, 0))
    vec = pl.BlockSpec((1, D), lambda i: (0, 0))
    return pl.pallas_call(
        body, grid=(T // tm,), in_specs=[tok, vec, tok, tok], out_specs=[tok, vec],
        out_shape=[jax.ShapeDtypeStruct((T, D), F32), jax.ShapeDtypeStruct((1, D), F32)],
        name=f"prenorm_bwd_l{l}", compiler_params=_params(("arbitrary",)))(x, g, dh, dxn)


def _gmlp_tile(u, v, z, ln_g, ln_b, ws, bs):
    mu = jnp.mean(v, axis=-1, keepdims=True)
    vc = v - mu
    var = jnp.mean(vc * vc, axis=-1, keepdims=True)
    vn = vc * lax.rsqrt(var + EPS) * ln_g + ln_b
    qi = lax.broadcasted_iota(jnp.int32, (128, 128), 0) >> CHUNK_SHIFT
    kj = lax.broadcasted_iota(jnp.int32, (128, 128), 1) >> CHUNK_SHIFT
    mask = kj <= qi
    outs = []
    for g in range(4):
        wm = jnp.where(mask, ws[g], 0.0)
        outs.append(dot_nn(wm, vn[:, 256 * g:256 * (g + 1)]) + bs[g])
    sv = jnp.concatenate(outs, axis=1)
    return u * sv * _silu(z)


def _gmlp_specs():
    blk = lambda c: pl.BlockSpec((128, 1024), lambda n, c=c: (n, c))
    vec = pl.BlockSpec((1, 1024), lambda n: (0, 0))
    return [blk(0), blk(1), blk(2), vec, vec,
            pl.BlockSpec((4, 128, 128), lambda n: (0, 0, 0)), pl.BlockSpec((4, 128, 1), lambda n: (0, 0, 0))]


def gmlp_fwd(proj, ln_g, ln_b, ws, bs, l):
    def body(u_ref, v_ref, z_ref, g_ref, b_ref, ws_ref, bs_ref, y_ref):
        y_ref[...] = _gmlp_tile(u_ref[...], v_ref[...], z_ref[...], g_ref[...], b_ref[...],
                                [ws_ref[g] for g in range(4)], [bs_ref[g] for g in range(4)])

    return pl.pallas_call(
        body, grid=(T // 128,), in_specs=_gmlp_specs(),
        out_specs=pl.BlockSpec((128, 1024), lambda n: (n, 0)),
        out_shape=jax.ShapeDtypeStruct((T, 1024), F32),
        name=f"gmlp_fwd_l{l}", compiler_params=_params(("arbitrary",)))(proj, proj, proj, ln_g, ln_b, ws, bs)


def gmlp_bwd(proj, ln_g, ln_b, ws, bs, dy, l):
    def body(u_ref, v_ref, z_ref, g_ref, b_ref, ws_ref, bs_ref, dy_ref, dseg_ref, dg_ref, db_ref, dws_ref, dbs_ref):
        first = pl.program_id(0) == 0
        _, vjp = jax.vjp(_gmlp_tile, u_ref[...], v_ref[...], z_ref[...], g_ref[...], b_ref[...],
                         [ws_ref[g] for g in range(4)], [bs_ref[g] for g in range(4)])
        du, dv, dz, dg, db, dws, dbs = vjp(dy_ref[...])
        dseg_ref[:, 0:1024] = du.astype(BF16)
        dseg_ref[:, 1024:2048] = dv.astype(BF16)
        dseg_ref[:, 2048:3072] = dz.astype(BF16)
        _acc(dg_ref, dg, first)
        _acc(db_ref, db, first)
        for g in range(4):
            _acc(dws_ref.at[g], dws[g], first)
            _acc(dbs_ref.at[g], dbs[g], first)

    vec = pl.BlockSpec((1, 1024), lambda n: (0, 0))
    return pl.pallas_call(
        body, grid=(T // 128,), in_specs=_gmlp_specs() + [pl.BlockSpec((128, 1024), lambda n: (n, 0))],
        out_specs=[pl.BlockSpec((128, 3072), lambda n: (n, 0)), vec, vec,
                   pl.BlockSpec((4, 128, 128), lambda n: (0, 0, 0)), pl.BlockSpec((4, 128, 1), lambda n: (0, 0, 0))],
        out_shape=[jax.ShapeDtypeStruct((T, 3072), BF16), jax.ShapeDtypeStruct((1, 1024), F32),
                   jax.ShapeDtypeStruct((1, 1024), F32), jax.ShapeDtypeStruct((4, 128, 128), F32),
                   jax.ShapeDtypeStruct((4, 128, 1), F32)],
        name=f"gmlp_bwd_l{l}", compiler_params=_params(("arbitrary",)))(proj, proj, proj, ln_g, ln_b, ws, bs, dy)


QKV_TM = 256


def _qkv_tile(cq, ckvr, qg, kvg, wq, wkv, ctab, stab):
    tm = cq.shape[0]
    cqn = _rms(cq, qg)
    lane = lax.broadcasted_iota(jnp.int32, ckvr.shape, 1)
    iskv = lane < 256
    ms = jnp.sum(jnp.where(iskv, ckvr * ckvr, 0.0), axis=-1, keepdims=True) * (1.0 / 256)
    lm = jnp.where(iskv, ckvr * lax.rsqrt(ms + EPS) * kvg, ckvr)
    r = lax.broadcasted_iota(jnp.int32, (64, 128), 0)
    c = lax.broadcasted_iota(jnp.int32, (64, 128), 1)
    eye = jnp.where(c == r, 1.0, 0.0)
    eye_sw = jnp.where(c == ((r + 32) & 63), 1.0, 0.0)
    z64 = jnp.zeros((64, 256), F32)
    z128 = jnp.zeros((128, 128), F32)
    rk_rope = jnp.concatenate([z64, eye], axis=1)
    rk_sw = jnp.concatenate([jnp.zeros((128, 384), F32), jnp.concatenate([z64, eye_sw], axis=1)], axis=0)
    k_sw = dot_nt(lm, rk_sw) * stab
    qs, ks, vs = [], [], []
    for h in range(HEADS):
        wn, w1, w2 = wq[h]
        wk, wv = wkv[h]
        wq_h = jnp.concatenate([wn, w1, w2], axis=0)
        wq_sw = jnp.concatenate([jnp.zeros((128, 384), F32), w2, w1], axis=0)
        qs.append(dot_nt(cqn, wq_h) * ctab + dot_nt(cqn, wq_sw) * stab)
        rk_h = jnp.concatenate([jnp.concatenate([wk, z128], axis=1), rk_rope], axis=0)
        ks.append(dot_nt(lm, rk_h) * ctab + k_sw)
        vs.append(dot_nt(lm, jnp.concatenate([wv, z128], axis=1)))
    return qs, ks, vs


def _qkv_in_specs():
    tm = QKV_TM
    return [pl.BlockSpec((tm, 384), lambda i: (i, OFF_CQ // 384)), pl.BlockSpec((tm, 384), lambda i: (i, OFF_CKV // 384)),
            pl.BlockSpec((1, 384), lambda i: (0, 0)), pl.BlockSpec((1, 384), lambda i: (0, 0)),
            pl.BlockSpec((HEADS, 192, 384), lambda i: (0, 0, 0)), pl.BlockSpec((HEADS, 256, 128), lambda i: (0, 0, 0)),
            pl.BlockSpec((tm, 192), lambda i: (i, 0)), pl.BlockSpec((tm, 192), lambda i: (i, 0))]


def _qkv_weights(wq_ref, wkv_ref):
    wq = [(wq_ref[h, 0:128, :], wq_ref[h, 128:160, :], wq_ref[h, 160:192, :]) for h in range(HEADS)]
    wkv = [(_unpack(wkv_ref[h, 0:128, :]), _unpack(wkv_ref[h, 128:256, :])) for h in range(HEADS)]
    return wq, wkv


def qkv_fwd(proj, qg, kvg, wq, wkv, ctab, stab, l, dep=None):
    tm = QKV_TM
    deps = [] if dep is None else [dep]

    def body(cq_ref, ckvr_ref, qg_ref, kvg_ref, wq_ref, wkv_ref, c_ref, s_ref, *rest):
        q_ref, k_ref, v_ref = rest[-3:]
        wq_l, wkv_l = _qkv_weights(wq_ref, wkv_ref)
        qs, ks, vs = _qkv_tile(cq_ref[...], ckvr_ref[...], qg_ref[...], kvg_ref[...], wq_l, wkv_l, c_ref[...], s_ref[...])
        for h in range(HEADS):
            q_ref[h] = qs[h]
            k_ref[h] = ks[h]
            v_ref[h] = vs[h]

    return pl.pallas_call(
        body, grid=(T // tm,), in_specs=_qkv_in_specs() + [ANY] * len(deps),
        out_specs=[pl.BlockSpec((HEADS, tm, QK), lambda i: (0, i, 0)), pl.BlockSpec((HEADS, tm, QK), lambda i: (0, i, 0)),
                   pl.BlockSpec((HEADS, tm, 128), lambda i: (0, i, 0))],
        out_shape=[jax.ShapeDtypeStruct((HEADS, T, QK), F32), jax.ShapeDtypeStruct((HEADS, T, QK), F32),
                   jax.ShapeDtypeStruct((HEADS, T, 128), F32)],
        name=f"qkv_fwd_l{l}", compiler_params=_params(("arbitrary",)))(proj, proj, qg, kvg, wq, wkv, ctab, stab, *deps)


def qkv_bwd(proj, qg, kvg, wq, wkv, ctab, stab, dq, dk, dv, l):
    tm = QKV_TM

    def body(cq_ref, ckvr_ref, qg_ref, kvg_ref, wq_ref, wkv_ref, c_ref, s_ref, dq_ref, dk_ref, dv_ref,
             dseg_ref, dqg_ref, dkvg_ref, dwq_ref, dwkv_ref):
        first = pl.program_id(0) == 0
        wq_l, wkv_l = _qkv_weights(wq_ref, wkv_ref)
        c_tab, s_tab = c_ref[...], s_ref[...]
        fn = lambda cq, ckvr, qg_, kvg_, wq_, wkv_: _qkv_tile(cq, ckvr, qg_, kvg_, wq_, wkv_, c_tab, s_tab)
        _, vjp = jax.vjp(fn, cq_ref[...], ckvr_ref[...], qg_ref[...], kvg_ref[...], wq_l, wkv_l)
        cts = ([dq_ref[h] for h in range(HEADS)], [dk_ref[h] for h in range(HEADS)], [dv_ref[h] for h in range(HEADS)])
        dcq, dckvr, dqg, dkvg, dwq, dwkv = vjp(cts)
        dseg_ref[:, 0:384] = dcq.astype(BF16)
        dseg_ref[:, 384:768] = dckvr.astype(BF16)
        _acc(dqg_ref, dqg, first)
        _acc(dkvg_ref, dkvg, first)
        for h in range(HEADS):
            _acc(dwq_ref.at[h, 0:128, :], dwq[h][0], first)
            _acc(dwq_ref.at[h, 128:160, :], dwq[h][1], first)
            _acc(dwq_ref.at[h, 160:192, :], dwq[h][2], first)
            _acc(dwkv_ref.at[h, 0:128, :], dwkv[h][0], first)
            _acc(dwkv_ref.at[h, 128:256, :], dwkv[h][1], first)

    hq = pl.BlockSpec((HEADS, tm, QK), lambda i: (0, i, 0))
    return pl.pallas_call(
        body, grid=(T // tm,),
        in_specs=_qkv_in_specs() + [hq, hq, pl.BlockSpec((HEADS, tm, 128), lambda i: (0, i, 0))],
        out_specs=[pl.BlockSpec((tm, 768), lambda i: (i, 0)), pl.BlockSpec((1, 384), lambda i: (0, 0)),
                   pl.BlockSpec((1, 384), lambda i: (0, 0)), pl.BlockSpec((HEADS, 192, 384), lambda i: (0, 0, 0)),
                   pl.BlockSpec((HEADS, 256, 256), lambda i: (0, 0, 0))],
        out_shape=[jax.ShapeDtypeStruct((T, 768), BF16), jax.ShapeDtypeStruct((1, 384), F32),
                   jax.ShapeDtypeStruct((1, 384), F32), jax.ShapeDtypeStruct((HEADS, 192, 384), F32),
                   jax.ShapeDtypeStruct((HEADS, 256, 256), F32)],
        name=f"qkv_bwd_l{l}", compiler_params=_params(("arbitrary",)))(
            proj, proj, qg, kvg, wq, wkv, ctab, stab, dq, dk, dv)


ATT_TQ = 256


def _attn_tile(q, kv_past, k, v, zb):
    scale = 1.0 / math.sqrt(QK)
    s = dot_nt(q, k) * scale
    qc = lax.broadcasted_iota(jnp.int32, s.shape, 0) >> CHUNK_SHIFT
    kc = lax.broadcasted_iota(jnp.int32, s.shape, 1) >> CHUNK_SHIFT
    s = jnp.where(kc <= qc, s, -1e30)
    m = jnp.max(s, axis=-1, keepdims=True)
    if kv_past is not None:
        sp = dot_nt(q, kv_past[0]) * scale
        m = jnp.maximum(m, jnp.max(sp, axis=-1, keepdims=True))
    m = lax.stop_gradient(m)
    p = jnp.exp(s - m)
    denom = jnp.sum(p, axis=-1, keepdims=True)
    o = dot_nn(p, v)
    if kv_past is not None:
        pp = jnp.exp(sp - m)
        denom = denom + jnp.sum(pp, axis=-1, keepdims=True)
        o = o + dot_nn(pp, kv_past[1])
    return o * (1.0 / denom) * _silu(zb)


def _attn_operands(k_ref, v_ref, g, tq):
    n = tq * g
    past = (k_ref[0:n, :], v_ref[0:n, :]) if g else None
    return past, k_ref[n:n + tq, :], v_ref[n:n + tq, :]


def _attn_in_specs():
    tq = ATT_TQ
    return [pl.BlockSpec((None, tq, QK), lambda h, i: (h, i, 0)), pl.BlockSpec((None, T, QK), lambda h, i: (h, 0, 0)),
            pl.BlockSpec((None, T, 128), lambda h, i: (h, 0, 0)),
            pl.BlockSpec((tq, 128), lambda h, i: (i, OFF_ZB // 128 + h))]


def attn_fwd(q, k, v, proj, l):
    tq = ATT_TQ

    def body(q_ref, k_ref, v_ref, z_ref, y_ref):
        for g in range(T // tq):
            @pl.when(pl.program_id(1) == g)
            def _(g=g):
                past, k, v = _attn_operands(k_ref, v_ref, g, tq)
                y_ref[...] = _attn_tile(q_ref[...], past, k, v, z_ref[...])

    return pl.pallas_call(
        body, grid=(HEADS, T // tq), in_specs=_attn_in_specs(),
        out_specs=pl.BlockSpec((tq, 128), lambda h, i: (i, h)),
        out_shape=jax.ShapeDtypeStruct((T, 1024), F32),
        name=f"attn_fwd_l{l}", compiler_params=_params(("arbitrary", "arbitrary")))(q, k, v, proj)


def attn_bwd(q, k, v, proj, dy, l):
    tq = ATT_TQ

    def body(q_ref, k_ref, v_ref, z_ref, dy_ref, dq_ref, dk_ref, dv_ref, dz_ref):
        @pl.when(pl.program_id(1) == 0)
        def _():
            dk_ref[...] = jnp.zeros_like(dk_ref)
            dv_ref[...] = jnp.zeros_like(dv_ref)

        for g in range(T // tq):
            @pl.when(pl.program_id(1) == g)
            def _(g=g):
                n = tq * g
                past, k, v = _attn_operands(k_ref, v_ref, g, tq)
                _, vjp = jax.vjp(_attn_tile, q_ref[...], past, k, v, z_ref[...])
                dq, dpast, dk, dv, dz = vjp(dy_ref[...])
                dq_ref[...] = dq
                dz_ref[...] = dz.astype(BF16)
                dk_ref[n:n + tq, :] += dk
                dv_ref[n:n + tq, :] += dv
                if g:
                    dk_ref[0:n, :] += dpast[0]
                    dv_ref[0:n, :] += dpast[1]

    return pl.pallas_call(
        body, grid=(HEADS, T // tq),
        in_specs=_attn_in_specs() + [pl.BlockSpec((tq, 128), lambda h, i: (i, h))],
        out_specs=[pl.BlockSpec((None, tq, QK), lambda h, i: (h, i, 0)), pl.BlockSpec((None, T, QK), lambda h, i: (h, 0, 0)),
                   pl.BlockSpec((None, T, 128), lambda h, i: (h, 0, 0)), pl.BlockSpec((tq, 128), lambda h, i: (i, h))],
        out_shape=[jax.ShapeDtypeStruct((HEADS, T, QK), F32), jax.ShapeDtypeStruct((HEADS, T, QK), F32),
                   jax.ShapeDtypeStruct((HEADS, T, 128), F32), jax.ShapeDtypeStruct((T, 1024), BF16)],
        name=f"attn_bwd_l{l}", compiler_params=_params(("arbitrary", "arbitrary")))(q, k, v, proj, dy)


LRU_TT = 256


def _lru_gates(xc, wa, wx, ba, bx, lam):
    r = _sigmoid(dot_nn(xc, wa) + ba)
    i = _sigmoid(dot_nn(xc, wx) + bx)
    sp = jnp.maximum(-lam, 0.0) + jnp.log1p(jnp.exp(-jnp.abs(lam)))
    log_a = -8.0 * r * sp
    a = jnp.exp(log_a)
    mult = jnp.sqrt(jnp.maximum(1.0 - jnp.exp(2.0 * log_a), 0.0))
    return a, mult * (i * xc)


def _shift_down(x, s, halo):
    xs = pltpu.roll(x, s, 0)
    row = lax.broadcasted_iota(jnp.int32, halo.shape, 0)
    top = jnp.where(row < s, pltpu.roll(halo, s, 0), xs[0:8])
    return jnp.concatenate([top, xs[8:]], axis=0)


def _shift_up(x, s, halo):
    n = x.shape[0]
    xs = pltpu.roll(x, n - s, 0)
    row = lax.broadcasted_iota(jnp.int32, halo.shape, 0)
    bot = jnp.where(row >= 8 - s, pltpu.roll(halo, 8 - s, 0), xs[n - 8:n])
    return jnp.concatenate([xs[:n - 8], bot], axis=0)


def _conv(x, halo, w_ref, b):
    return (w_ref[3:4, :] * x + w_ref[2:3, :] * _shift_down(x, 1, halo) + w_ref[1:2, :] * _shift_down(x, 2, halo)
            + w_ref[0:1, :] * _shift_down(x, 3, halo) + b)


def _scan(a, b, reverse):
    n = a.shape[0]
    row = lax.broadcasted_iota(jnp.int32, a.shape, 0)
    d = 1
    while d < n:
        if reverse:
            keep = row < n - d
            a_sh = jnp.where(keep, pltpu.roll(a, n - d, 0), 1.0)
            b_sh = jnp.where(keep, pltpu.roll(b, n - d, 0), 0.0)
        else:
            keep = row >= d
            a_sh = jnp.where(keep, pltpu.roll(a, d, 0), 1.0)
            b_sh = jnp.where(keep, pltpu.roll(b, d, 0), 0.0)
        b = a * b_sh + b
        a = a * a_sh
        d *= 2
    return a, b


def _lru_param_specs(l):
    ct = LRU_TILE
    vec = pl.BlockSpec((1, ct), lambda n, i: (0, n))
    mat = pl.BlockSpec((None, None, ct, ct), lambda n, i: (l, n, 0, 0))
    return [pl.BlockSpec((4, ct), lambda n, i: (0, n)), vec, mat, mat, vec, vec, vec]


def lru_fwd(proj, conv_w, conv_b, wa, wx, ba, bx, lam, l):
    tt, ct = LRU_TT, LRU_TILE

    def body(x_ref, z_ref, cw_ref, cb_ref, wa_ref, wx_ref, ba_ref, bx_ref, lam_ref, h_ref, y_ref, halo, hcar):
        @pl.when(pl.program_id(1) == 0)
        def _():
            halo[...] = jnp.zeros_like(halo)
            hcar[...] = jnp.zeros_like(hcar)

        x = x_ref[...]
        xc = _conv(x, halo[...], cw_ref, cb_ref[...])
        halo[...] = x[tt - 8:tt]
        a, b = _lru_gates(xc, wa_ref[...], wx_ref[...], ba_ref[...], bx_ref[...], lam_ref[...])
        a_cum, b_cum = _scan(a, b, False)
        h = a_cum * hcar[...] + b_cum
        h_ref[...] = h
        hcar[...] = h_ref[tt - 1:tt, :]
        y_ref[...] = h * _silu(z_ref[...])

    seq = pl.BlockSpec((tt, ct), lambda n, i: (i, n))
    return pl.pallas_call(
        body, grid=(LRU_W // ct, T // tt),
        in_specs=[pl.BlockSpec((tt, ct), lambda n, i: (i, OFF_XC // ct + n)),
                  pl.BlockSpec((tt, ct), lambda n, i: (i, OFF_ZC // ct + n))] + _lru_param_specs(l),
        out_specs=[seq, seq],
        out_shape=[jax.ShapeDtypeStruct((T, LRU_W), F32), jax.ShapeDtypeStruct((T, LRU_W), F32)],
        scratch_shapes=[pltpu.VMEM((8, ct), F32), pltpu.VMEM((1, ct), F32)],
        name=f"lru_fwd_l{l}", compiler_params=_params(("arbitrary", "arbitrary")))(
            proj, proj, conv_w, conv_b, wa, wx, ba, bx, lam)


def lru_bwd(proj, hseq, dy, conv_w, conv_b, wa, wx, ba, bx, lam, l):
    tt, ct = LRU_TT, LRU_TILE
    nt = T // tt
    rev = lambda i: nt - 1 - i
    prev8 = lambda i: jnp.maximum(rev(i) * (tt // 8) - 1, 0)

    def body(x_ref, xh_ref, z_ref, h_ref, hh_ref, dy_ref, cw_ref, cb_ref, wa_ref, wx_ref, ba_ref, bx_ref, lam_ref,
             dx_ref, dz_ref, dcw_ref, dcb_ref, dwa_ref, dwx_ref, dba_ref, dbx_ref, dlam_ref, gcar, dhalo):
        i = pl.program_id(1)
        first = i == 0

        @pl.when(first)
        def _():
            gcar[...] = jnp.zeros_like(gcar)
            dhalo[...] = jnp.zeros_like(dhalo)

        at_start = rev(i) == 0
        x = x_ref[...]
        xhalo = jnp.where(at_start, 0.0, xh_ref[...])
        sh = [x, _shift_down(x, 1, xhalo), _shift_down(x, 2, xhalo), _shift_down(x, 3, xhalo)]
        xc = (cw_ref[3:4, :] * sh[0] + cw_ref[2:3, :] * sh[1] + cw_ref[1:2, :] * sh[2] + cw_ref[0:1, :] * sh[3]
              + cb_ref[...])
        (a, b), vjp = jax.vjp(_lru_gates, xc, wa_ref[...], wx_ref[...], ba_ref[...], bx_ref[...], lam_ref[...])
        hs = h_ref[...]
        hprev = _shift_down(hs, 1, jnp.where(at_start, 0.0, hh_ref[...]))
        z = z_ref[...]
        sg = _sigmoid(z)
        dy = dy_ref[...]
        dz_ref[...] = (dy * hs * (sg * (1.0 + z * (1.0 - sg)))).astype(BF16)
        dh = dy * (z * sg)
        row = lax.broadcasted_iota(jnp.int32, a.shape, 0)
        a_next = jnp.where(row < tt - 1, pltpu.roll(a, tt - 1, 0), 1.0)
        a_cum, b_cum = _scan(a_next, dh, True)
        g = a_cum * gcar[...] + b_cum
        dxc, dwa, dwx, dba, dbx, dlam = vjp((g * hprev, g))
        dx = (cw_ref[3:4, :] * dxc + cw_ref[2:3, :] * _shift_up(dxc, 1, dhalo[...])
              + cw_ref[1:2, :] * _shift_up(dxc, 2, dhalo[...]) + cw_ref[0:1, :] * _shift_up(dxc, 3, dhalo[...]))
        dx_ref[...] = dx.astype(BF16)
        dhalo[...] = dxc[0:8]
        ag = a * g
        gcar[...] = ag[0:1]
        dcw = jnp.concatenate([jnp.sum(dxc * sh[3 - j], axis=0, keepdims=True) for j in range(4)], axis=0)
        _acc(dcw_ref, dcw, first)
        _acc(dcb_ref, jnp.sum(dxc, axis=0, keepdims=True), first)
        _acc(dwa_ref, dwa, first)
        _acc(dwx_ref, dwx, first)
        _acc(dba_ref, dba, first)
        _acc(dbx_ref, dbx, first)
        _acc(dlam_ref, dlam, first)

    xcol = OFF_XC // ct
    zcol = OFF_ZC // ct
    vec = pl.BlockSpec((1, ct), lambda n, i: (0, n))
    mat = pl.BlockSpec((None, ct, ct), lambda n, i: (n, 0, 0))
    seq = pl.BlockSpec((tt, ct), lambda n, i: (rev(i), n))
    return pl.pallas_call(
        body, grid=(LRU_W // ct, nt),
        in_specs=[pl.BlockSpec((tt, ct), lambda n, i: (rev(i), xcol + n)),
                  pl.BlockSpec((8, ct), lambda n, i: (prev8(i), xcol + n)),
                  pl.BlockSpec((tt, ct), lambda n, i: (rev(i), zcol + n)),
                  seq, pl.BlockSpec((8, ct), lambda n, i: (prev8(i), n)), seq] + _lru_param_specs(l),
        out_specs=[seq, seq, pl.BlockSpec((4, ct), lambda n, i: (0, n)), vec, mat, mat, vec, vec, vec],
        out_shape=[jax.ShapeDtypeStruct((T, LRU_W), BF16), jax.ShapeDtypeStruct((T, LRU_W), BF16),
                   jax.ShapeDtypeStruct((4, LRU_W), F32), jax.ShapeDtypeStruct((1, LRU_W), F32),
                   jax.ShapeDtypeStruct((2, ct, ct), F32), jax.ShapeDtypeStruct((2, ct, ct), F32),
                   jax.ShapeDtypeStruct((1, LRU_W), F32), jax.ShapeDtypeStruct((1, LRU_W), F32),
                   jax.ShapeDtypeStruct((1, LRU_W), F32)],
        scratch_shapes=[pltpu.VMEM((1, ct), F32), pltpu.VMEM((8, ct), F32)],
        name=f"lru_bwd_l{l}", compiler_params=_params(("arbitrary", "arbitrary")))(
            proj, proj, proj, hseq, hseq, dy, conv_w, conv_b, wa, wx, ba, bx, lam)


def proj_fwd(y, w, l, tag):
    tm = 512
    k = y.shape[1]

    def body(y_ref, w_ref, o_ref):
        o_ref[...] = _dg(y_ref[...], _unpack(w_ref[...]), _NN)

    return pl.pallas_call(
        body, grid=(T // tm,),
        in_specs=[pl.BlockSpec((tm, k), lambda i: (i, 0)), pl.BlockSpec((None, k, D // 2), lambda i: (0, 0, 0))],
        out_specs=pl.BlockSpec((tm, D), lambda i: (i, 0)), out_shape=jax.ShapeDtypeStruct((T, D), F32),
        name=f"proj_{tag}_fwd_l{l}", compiler_params=_params(("arbitrary",)))(y, w)


def proj_bwd(y, dp, w, l, tag):
    tm = 512
    k = y.shape[1]

    def body(y_ref, dp_ref, w_ref, dy_ref, dw_ref):
        dp = dp_ref[...]
        dy_ref[...] = _dg(dp, _unpack(w_ref[...]), _NT)
        _acc(dw_ref, _dg(y_ref[...], dp, _TN), pl.program_id(0) == 0)

    return pl.pallas_call(
        body, grid=(T // tm,),
        in_specs=[pl.BlockSpec((tm, k), lambda i: (i, 0)), pl.BlockSpec((tm, D), lambda i: (i, 0)),
                  pl.BlockSpec((None, k, D // 2), lambda i: (0, 0, 0))],
        out_specs=[pl.BlockSpec((tm, k), lambda i: (i, 0)), pl.BlockSpec((None, k, D), lambda i: (0, 0, 0))],
        out_shape=[jax.ShapeDtypeStruct((T, k), F32), jax.ShapeDtypeStruct((1, k, D), F32)],
        name=f"proj_{tag}_bwd_l{l}", compiler_params=_params(("arbitrary",)))(y, dp, w)


OUT_TM = 256


def _out_tile(pa, pb, pc, ga, gb, gc, wout, post_g):
    merged = _sigmoid(ga) * pa + _sigmoid(gb) * pb + _sigmoid(gc) * pc
    return _rms(dot_nn(merged, wout), post_g)


def _out_in_specs():
    tm = OUT_TM
    tok = pl.BlockSpec((tm, D), lambda i: (i, 0))
    gate = lambda off: pl.BlockSpec((tm, 512), lambda i, off=off: (i, off // 512))
    return [tok, tok, tok, gate(OFF_GA), gate(OFF_GA + 512), gate(OFF_GB), gate(OFF_GB + 512), gate(OFF_GC),
            gate(OFF_GC + 512), pl.BlockSpec((None, D, D // 2), lambda i: (0, 0, 0)), pl.BlockSpec((1, D), lambda i: (0, 0))]


def _gates(refs):
    return [jnp.concatenate([refs[2 * j][...], refs[2 * j + 1][...]], axis=1) for j in range(3)]


def out_fwd(x, pa, pb, pc, proj, wout, post_g, l):
    tm = OUT_TM

    def body(pa_ref, pb_ref, pc_ref, g0, g1, g2, g3, g4, g5, w_ref, pg_ref, x_ref, o_ref):
        ga, gb, gc = _gates([g0, g1, g2, g3, g4, g5])
        o_ref[...] = x_ref[...] + _out_tile(pa_ref[...], pb_ref[...], pc_ref[...], ga, gb, gc, _unpack(w_ref[...]),
                                            pg_ref[...])

    tok = pl.BlockSpec((tm, D), lambda i: (i, 0))
    return pl.pallas_call(
        body, grid=(T // tm,), in_specs=_out_in_specs() + [tok], out_specs=tok,
        out_shape=jax.ShapeDtypeStruct((T, D), F32),
        name=f"out_fwd_l{l}", compiler_params=_params(("arbitrary",)))(
            pa, pb, pc, proj, proj, proj, proj, proj, proj, wout, post_g, x)


def out_bwd(pa, pb, pc, proj, wout, post_g, dxn, l, dep=None):
    tm = OUT_TM

    def body(pa_ref, pb_ref, pc_ref, g0, g1, g2, g3, g4, g5, w_ref, pg_ref, dxn_ref, *rest):
        dpa_ref, dpb_ref, dpc_ref, dg_ref, dw_ref, dpg_ref = rest[-6:]
        first = pl.program_id(0) == 0
        ga, gb, gc = _gates([g0, g1, g2, g3, g4, g5])
        _, vjp = jax.vjp(_out_tile, pa_ref[...], pb_ref[...], pc_ref[...], ga, gb, gc, _unpack(w_ref[...]), pg_ref[...])
        dpa, dpb, dpc, dga, dgb, dgc, dw, dpg = vjp(dxn_ref[...])
        dpa_ref[...] = dpa.astype(BF16)
        dpb_ref[...] = dpb.astype(BF16)
        dpc_ref[...] = dpc.astype(BF16)
        dg_ref[:, 0:1024] = dga.astype(BF16)
        dg_ref[:, 1024:2048] = dgb.astype(BF16)
        dg_ref[:, 2048:3072] = dgc.astype(BF16)
        _acc(dw_ref, dw, first)
        _acc(dpg_ref, dpg, first)

    tok = pl.BlockSpec((tm, D), lambda i: (i, 0))
    deps = [] if dep is None else [dep]
    return pl.pallas_call(
        body, grid=(T // tm,), in_specs=_out_in_specs() + [tok] + [ANY] * len(deps),
        out_specs=[tok, tok, tok, pl.BlockSpec((tm, 3072), lambda i: (i, 0)),
                   pl.BlockSpec((None, D, D), lambda i: (0, 0, 0)), pl.BlockSpec((1, D), lambda i: (0, 0))],
        out_shape=[jax.ShapeDtypeStruct((T, D), BF16)] * 3 + [jax.ShapeDtypeStruct((T, 3072), BF16),
                                                            jax.ShapeDtypeStruct((1, D, D), F32), jax.ShapeDtypeStruct((1, D), F32)],
        name=f"out_bwd_l{l}", compiler_params=_params(("arbitrary",)))(
            pa, pb, pc, proj, proj, proj, proj, proj, proj, wout, post_g, dxn, *deps)


def loss_head(y, target):
    tm = 256

    def body(y_ref, t_ref, loss_ref, dy_ref):
        e = y_ref[...] - t_ref[...]
        dy_ref[...] = e * (1.0 / D)
        val = 0.5 * jnp.sum(jnp.mean(e * e, axis=-1, keepdims=True), axis=0, keepdims=True)
        _acc(loss_ref, jnp.broadcast_to(val, (8, 128)), pl.program_id(0) == 0)

    tok = pl.BlockSpec((tm, D), lambda i: (i, 0))
    total, dy = pl.pallas_call(
        body, grid=(T // tm,), in_specs=[tok, tok],
        out_specs=[pl.BlockSpec((8, 128), lambda i: (0, 0)), tok],
        out_shape=[jax.ShapeDtypeStruct((8, 128), F32), jax.ShapeDtypeStruct((T, D), F32)],
        name="loss_head", compiler_params=_params(("arbitrary",)))(y, target)
    return total[0, 0], dy


def _rope_tables():
    pos = jnp.arange(T, dtype=F32)
    inv_freq = 10000.0 ** (-jnp.arange(0, 64, 2, dtype=F32) / 64)
    ang = pos[:, None] * inv_freq[None, :]
    cos, sin = jnp.cos(ang), jnp.sin(ang)
    ctab = jnp.concatenate([jnp.ones((T, 128), F32), cos, cos], axis=1)
    stab = jnp.concatenate([jnp.zeros((T, 128), F32), -sin, sin], axis=1)
    return ctab, stab


def _block_diag(w):
    w5 = w.reshape(L, 2, 8, 80, 80)
    eye = jnp.eye(8, dtype=w.dtype)
    return jnp.einsum("lnbij,bc->lnbicj", w5, eye).reshape(L, 2, LRU_TILE, LRU_TILE)


def _block_diag_t(dw):
    dw5 = dw.reshape(2, 8, 80, 8, 80)
    return jnp.einsum("nbicj,bc->nbij", dw5, jnp.eye(8, dtype=dw.dtype)).reshape(16, 80, 80)


def _layer_fwd(x, l, w, gw, tabs, dep=None, mid=None):
    row = lambda a: a[l][None]
    proj, h = inproj_fwd(x, row(w["pre_norm_g"]), gw["w_in_t"], l, dep)
    ya = gmlp_fwd(proj, row(w["gm_ln_g"]), row(w["gm_ln_b"]), w["gm_ws"][l], w["gm_bs"][l][..., None], l)
    dep2 = None
    if mid is not None:
        gw, dep2 = mid(ya)
    q, k, v = qkv_fwd(proj, row(w["mla_q_norm_g"]), row(w["kv_g384"]), gw["wq"], gw["wkv"], tabs[0], tabs[1], l, dep2)
    yb = attn_fwd(q, k, v, proj, l)
    hseq, yc = lru_fwd(proj, gw["conv"], row(w["lru_conv_b"]), w["wa_dense"], w["wx_dense"],
                       row(w["lru_b_a"]), row(w["lru_b_x"]), row(w["lru_lambda"]), l)
    pa = proj_fwd(ya, gw["w_proj_a"], l, "a")
    pb = proj_fwd(yb, gw["w_proj_b"], l, "b")
    pc = proj_fwd(yc, gw["w_proj_c"], l, "c")
    xn = out_fwd(x, pa, pb, pc, proj, gw["w_out"], row(w["post_norm_g"]), l)
    return xn, (x, proj, h, ya, q, k, v, yb, hseq, yc, pa, pb, pc)


def _layer_bwd(dxn, l, w, gw, tabs, saved, dep=None, mid=None):
    x, proj, h, ya, q, k, v, yb, hseq, yc, pa, pb, pc = saved
    row = lambda a: a[l][None]
    g, gg = {}, {}
    dpa, dpb, dpc, dgates, gg["w_out"], dpost = out_bwd(pa, pb, pc, proj, gw["w_out"], row(w["post_norm_g"]), dxn, l, dep)
    g["post_norm_g"] = dpost[0]
    dya, gg["w_proj_a"] = proj_bwd(ya, dpa, gw["w_proj_a"], l, "a")
    dyb, gg["w_proj_b"] = proj_bwd(yb, dpb, gw["w_proj_b"], l, "b")
    dyc, gg["w_proj_c"] = proj_bwd(yc, dpc, gw["w_proj_c"], l, "c")
    dseg_a, dln_g, dln_b, g["gm_ws"], dbs = gmlp_bwd(proj, row(w["gm_ln_g"]), row(w["gm_ln_b"]), w["gm_ws"][l],
                                                    w["gm_bs"][l][..., None], dya, l)
    g["gm_ln_g"], g["gm_ln_b"], g["gm_bs"] = dln_g[0], dln_b[0], dbs[..., 0]
    dq, dk, dv, dzb = attn_bwd(q, k, v, proj, dyb, l)
    dseg_q, dqg, dkvg, dwq, dwkv = qkv_bwd(proj, row(w["mla_q_norm_g"]), row(w["kv_g384"]), gw["wq"], gw["wkv"],
                                           tabs[0], tabs[1], dq, dk, dv, l)
    gg["wq"], gg["wkv"] = dwq.reshape(1, 1536, 384), dwkv.reshape(1, 2048, 256)
    g["mla_q_norm_g"], g["mla_kv_norm_g"] = dqg[0], dkvg[0, :256]
    dxc, dzc, dcw, dcb, dwa, dwx, dba, dbx, dlam = lru_bwd(
        proj, hseq, dyc, gw["conv"], row(w["lru_conv_b"]), w["wa_dense"], w["wx_dense"],
        row(w["lru_b_a"]), row(w["lru_b_x"]), row(w["lru_lambda"]), l)
    gg["conv"] = jnp.pad(dcw.T, ((0, 0), (0, 124)))[None]
    g["lru_conv_b"], g["lru_b_a"], g["lru_b_x"], g["lru_lambda"] = dcb[0], dba[0], dbx[0], dlam[0]
    g["lru_w_a"], g["lru_w_x"] = _block_diag_t(dwa), _block_diag_t(dwx)
    dproj = jnp.concatenate([dseg_a, dseg_q, dzb, jnp.zeros((T, PAD2), dzb.dtype), dxc, dzc, dgates], axis=1)
    dep2 = mid(gg, dproj) if mid is not None else None
    gg["w_in_t"], dh = inproj_bwd(dproj, h, gw["w_in_t"], l, dep2)
    dx, dpre = prenorm_bwd(x, row(w["pre_norm_g"]), dh, dxn, l)
    g["pre_norm_g"] = dpre[0]
    return dx, gg, g


MESH = pl.DeviceIdType.MESH
HBM = pl.BlockSpec(memory_space=pltpu.HBM)
SEM = pl.BlockSpec(memory_space=pltpu.SEMAPHORE)
EFFECT = pltpu.SideEffectType.DATAFLOW_SIDE_EFFECTING
FLIPS = ((1, 0), (0, 1), (1, 1))


def _win_off(k, s):
    g = SHARD * k + s
    return g + jnp.where(g >= PAD1_AT, PAD1, 0) + jnp.where(g >= PAD2_AT, PAD2, 0)


def _plain_off(rows):
    return lambda k, s: rows * k + s


class Spec:
    def __init__(self, rows, cols, full_rows, pieces=None, off=None, layers=1, packed=None):
        self.rows, self.cols, self.full_rows, self.layers = rows, cols, full_rows, layers
        self.pieces = pieces or ((0, rows),)
        self.off = off or _plain_off(rows)
        self.packed = cols % 256 == 0 if packed is None else packed
        self.wcols = cols // 2 if self.packed else cols

    def to_words(self, a):
        return _pack(a) if self.packed else a

    def from_words(self, p):
        return _unpack(p) if self.packed else p


def _pack(a):
    bits = lambda v: lax.bitcast_convert_type(v.astype(jnp.bfloat16).astype(F32), jnp.uint32)
    words = [(bits(a[:, g:g + 128]) >> 16) | (bits(a[:, g + 128:g + 256]) & jnp.uint32(0xFFFF0000))
             for g in range(0, a.shape[-1], 256)]
    return lax.bitcast_convert_type(jnp.concatenate(words, axis=-1) if len(words) > 1 else words[0], F32)


def _unpack(p):
    w = lax.bitcast_convert_type(p, jnp.uint32)
    lo = lax.bitcast_convert_type(w << 16, F32)
    hi = lax.bitcast_convert_type(w & jnp.uint32(0xFFFF0000), F32)
    return jnp.concatenate([h[:, g:g + 128] for g in range(0, p.shape[-1], 128) for h in (lo, hi)], axis=-1)


WEIGHT_SPECS = {
    "w_in_t": Spec(SHARD, D, NPAD, WIN_PIECES, _win_off),
    "wq": Spec(192, 384, 1536),
    "wkv": Spec(256, 256, 2048),
    "conv": Spec(160, 128, 1280),
    "w_proj_a": Spec(128, D, 1024),
    "w_proj_b": Spec(128, D, 1024),
    "w_proj_c": Spec(160, D, 1280),
    "w_out": Spec(128, D, 1024),
}
REP_ROWS = 72
REP_SPEC = Spec(REP_ROWS, D, REP_ROWS * NDEV, packed=False)


def _coords():
    return lax.axis_index("x"), lax.axis_index("y"), lax.axis_index("c")


def _rows(ref, start, n):
    if not isinstance(start, int):
        start = pl.multiple_of(start, 8)
    return ref.at[:, pl.ds(start, n), :]


def _col_tile(cols):
    return 256 if cols % 256 == 0 else cols


def _n_pieces(specs):
    return sum(len(sp.pieces) for sp in specs)


def pack_place(shard, sp, layer, tag):
    gaps = ((PAD1_AT, PAD1), (PAD2_AT + PAD1, PAD2)) if sp.off is _win_off else ()
    npc = len(sp.pieces)

    def body(s_ref, words_ref, full_ref, buf, zbuf, sem):
        l = 0
        x, y, c = _coords()
        me = 4 * x + 2 * y + c
        words = sp.to_words(s_ref[...])
        words_ref[...] = words
        buf[...] = words
        copies = [pltpu.make_async_copy(buf.at[pl.ds(s, n), :],
                                        full_ref.at[l, pl.ds(pl.multiple_of(sp.off(me, s), 8), n), :], sem.at[i])
                  for i, (s, n) in enumerate(sp.pieces)]
        if gaps:
            zbuf[...] = jnp.zeros_like(zbuf)
            copies += [pltpu.make_async_copy(zbuf.at[pl.ds(0, n), :], full_ref.at[l, pl.ds(at, n), :], sem.at[npc + i])
                       for i, (at, n) in enumerate(gaps)]
        for cp in copies:
            cp.start()
        for cp in copies:
            cp.wait()

    return pl.pallas_call(
        body, grid=(1,), in_specs=[pl.BlockSpec((None, sp.rows, sp.cols), lambda i: (layer, 0, 0))],
        out_specs=[pl.BlockSpec((None, sp.rows, sp.wcols), lambda i: (0, 0, 0)), ANY],
        out_shape=[jax.ShapeDtypeStruct((sp.layers, sp.rows, sp.wcols), F32),
                   jax.ShapeDtypeStruct((sp.layers, sp.full_rows, sp.wcols), F32)],
        scratch_shapes=[pltpu.VMEM((sp.rows, sp.wcols), F32), pltpu.VMEM((PAD2 if gaps else 8, sp.wcols), F32),
                        pltpu.SemaphoreType.DMA((npc + len(gaps),))],
        name=f"pack_place_{tag}", compiler_params=_params(("arbitrary",)))(shard)


def _gather_copies(srcs, bufs, specs, ssem, rsem, landing):
    x, y, c = _coords()
    me = 4 * x + 2 * y + c
    targets = [(x, y, 1 - c)] + [(x ^ fx, y ^ fy, c) for fx, fy in FLIPS]
    copies = []
    p = 0
    for src, buf, sp in zip(srcs, bufs, specs):
        for s, n in sp.pieces:
            for t, (tx, ty, tc) in enumerate(targets):
                owner = 4 * tx + 2 * ty + tc if landing else me
                copies.append(pltpu.make_async_remote_copy(_rows(src, s, n), _rows(buf, sp.off(owner, s), n),
                                                           ssem.at[4 * p + t], rsem.at[4 * p + t],
                                                           device_id=(tx, ty, tc), device_id_type=MESH))
            p += 1
    return copies


def gather_send(words, fulls, specs, tag):
    ns, npc = len(specs), _n_pieces(specs)

    def body(*refs):
        srcs, bufs, sems = refs[:ns], refs[2 * ns:3 * ns], refs[3 * ns:]
        for cp in _gather_copies(srcs, bufs, specs, *sems, False):
            cp.start()
        for cp in _gather_copies(srcs, bufs, specs, *sems, False):
            cp.wait_send()
        for cp in _gather_copies(srcs, bufs, specs, *sems, True):
            cp.wait_recv()

    return pl.pallas_call(
        body, in_specs=[ANY] * (2 * ns), out_specs=[ANY] * ns,
        out_shape=[jax.ShapeDtypeStruct(f.shape, f.dtype) for f in fulls],
        input_output_aliases={ns + i: i for i in range(ns)},
        scratch_shapes=[pltpu.SemaphoreType.DMA((4 * npc,)), pltpu.SemaphoreType.DMA((4 * npc,))],
        name=f"gather_send_{tag}", compiler_params=pltpu.CompilerParams(has_side_effects=True))(*words, *fulls)


def _in_hbm(arrays):
    return [pltpu.with_memory_space_constraint(a, pltpu.HBM) for a in arrays]


def gather_start(words, fulls, specs, dep, tag):
    ns, npc = len(specs), _n_pieces(specs)

    def body(*refs):
        ssem, rsem = refs[2 * ns + 1:2 * ns + 3]
        for cp in _gather_copies(refs[:ns], refs[ns:2 * ns], specs, ssem, rsem, False):
            cp.start()
        refs[-1][...] = jnp.zeros_like(refs[-1])

    outs = pl.pallas_call(
        body, in_specs=[HBM] * (2 * ns) + [ANY],
        out_specs=[SEM, SEM] + [HBM] * (2 * ns) + [pl.BlockSpec(memory_space=pltpu.VMEM)],
        out_shape=[pltpu.SemaphoreType.DMA((4 * npc,)), pltpu.SemaphoreType.DMA((4 * npc,))]
        + [pltpu.HBM(a.shape, a.dtype) for a in list(words) + list(fulls)] + [jax.ShapeDtypeStruct((8, 128), F32)],
        input_output_aliases={i: 2 + i for i in range(2 * ns)},
        name=f"gather_start_{tag}", compiler_params=pltpu.CompilerParams(has_side_effects=EFFECT))(
            *_in_hbm(list(words) + list(fulls)), dep)
    return outs[0], outs[1], outs[2:2 + ns], outs[2 + ns:2 + 2 * ns], outs[-1]


def gather_wait(ssem, rsem, words, fulls, specs, after, tag):
    ns = len(specs)

    def body(*refs):
        srcs, bufs, ssem, rsem = refs[:ns], refs[ns:2 * ns], refs[2 * ns], refs[2 * ns + 1]
        for cp in _gather_copies(srcs, bufs, specs, ssem, rsem, False):
            cp.wait_send()
        for cp in _gather_copies(srcs, bufs, specs, ssem, rsem, True):
            cp.wait_recv()

    outs = pl.pallas_call(
        body, in_specs=[HBM] * (2 * ns) + [SEM, SEM, ANY], out_specs=[HBM] * (2 * ns),
        out_shape=[pltpu.HBM(a.shape, a.dtype) for a in list(words) + list(fulls)],
        input_output_aliases={i: i for i in range(2 * ns)},
        name=f"gather_wait_{tag}", compiler_params=pltpu.CompilerParams(has_side_effects=EFFECT))(
            *words, *fulls, ssem, rsem, after)
    return outs[ns:]


def gather_forward(fulls, specs, tag):
    ns, npc = len(specs), _n_pieces(specs)

    def body(*refs):
        bufs = refs[ns:2 * ns]
        ssem, rsem = refs[2 * ns:]
        x, y, c = _coords()
        sibling = (x, y, 1 - c)
        waits = []
        p = 0
        for buf, sp in zip(bufs, specs):
            for s, n in sp.pieces:
                for t, (fx, fy) in enumerate(FLIPS):
                    chip = 4 * (x ^ fx) + 2 * (y ^ fy)
                    here = _rows(buf, sp.off(chip + c, s), n)
                    send = pltpu.make_async_remote_copy(here, here, ssem.at[t, p], rsem.at[t, p],
                                                        device_id=sibling, device_id_type=MESH)
                    send.start()
                    waits.append(send.wait_send)
                    there = _rows(buf, sp.off(chip + 1 - c, s), n)
                    waits.append(pltpu.make_async_remote_copy(here, there, ssem.at[t, p], rsem.at[t, p],
                                                              device_id=sibling, device_id_type=MESH).wait_recv)
                p += 1
        for w in waits:
            w()

    return pl.pallas_call(
        body, in_specs=[ANY] * ns, out_specs=[ANY] * ns,
        out_shape=[jax.ShapeDtypeStruct(f.shape, f.dtype) for f in fulls],
        input_output_aliases={i: i for i in range(ns)},
        scratch_shapes=[pltpu.SemaphoreType.DMA((3, npc)), pltpu.SemaphoreType.DMA((3, npc))],
        name=f"gather_forward_{tag}", compiler_params=pltpu.CompilerParams(has_side_effects=True))(*fulls)


def all_gather(shards, layer, specs, names, tag):
    placed = [pack_place(s, sp, layer, f"{tag}_{n}") for s, sp, n in zip(shards, specs, names)]
    fulls = gather_send([p[0] for p in placed], [p[1] for p in placed], specs, tag)
    return gather_forward(fulls, specs, tag)


def reduce_pair(grads, specs, tag, dep=None):
    ns, npc = len(specs), _n_pieces(specs)
    deps = [] if dep is None else [dep]

    def body(*refs):
        srcs, theirs = refs[:ns], refs[ns + len(deps):2 * ns + len(deps)]
        ssem, rsem = refs[2 * ns + len(deps):]
        x, y, c = _coords()
        sibling = (x, y, 1 - c)
        waits = []
        p = 0
        for src, their, sp in zip(srcs, theirs, specs):
            for s, n in sp.pieces:
                for j in range(4):
                    send = pltpu.make_async_remote_copy(_rows(src, sp.off(2 * j + 1 - c, s), n), _rows(their.at[j], s, n),
                                                        ssem.at[j, p], rsem.at[j, p], device_id=sibling, device_id_type=MESH)
                    send.start()
                    waits.append(send.wait)
                p += 1
        for w in waits:
            w()

    return pl.pallas_call(
        body, in_specs=[ANY] * (ns + len(deps)), out_specs=[ANY] * ns,
        out_shape=[jax.ShapeDtypeStruct((4, sp.layers, sp.rows, sp.cols), F32) for sp in specs],
        scratch_shapes=[pltpu.SemaphoreType.DMA((4, npc)), pltpu.SemaphoreType.DMA((4, npc))],
        name=f"reduce_pair_{tag}", compiler_params=pltpu.CompilerParams(has_side_effects=True))(*grads, *deps)


def pair_sum(g, r1, sp, tag):
    npc = len(sp.pieces)
    fetch_all = 4 * sp.rows * sp.cols * 4 <= (8 << 20)

    def body(g_ref, r_ref, own_ref, words_ref, gbuf, sem):
        l, j = pl.program_id(0), pl.program_id(1)
        x, y, c = _coords()

        def fetch(chip, slot):
            copies = [pltpu.make_async_copy(g_ref.at[l, pl.ds(pl.multiple_of(sp.off(2 * chip + c, s), 8), n), :],
                                            gbuf.at[slot, pl.ds(s, n), :], sem.at[slot, i])
                      for i, (s, n) in enumerate(sp.pieces)]
            for cp in copies:
                cp.start()
            return copies

        if fetch_all:
            @pl.when(j == 0)
            def _():
                for cp in [cp for chip in range(4) for cp in fetch(chip, chip)]:
                    cp.wait()

            mine = gbuf[j]
        else:
            for cp in fetch(j, 0):
                cp.wait()
            mine = gbuf[0]
        p = mine + r_ref[...]
        words_ref[...] = sp.to_words(p)

        @pl.when(j == 2 * x + y)
        def _():
            own_ref[...] = p

    return pl.pallas_call(
        body, grid=(sp.layers, 4),
        in_specs=[ANY, pl.BlockSpec((None, None, sp.rows, sp.cols), lambda l, j: (j, l, 0, 0))],
        out_specs=[pl.BlockSpec((None, sp.rows, sp.cols), lambda l, j: (l, 0, 0)),
                   pl.BlockSpec((None, None, sp.rows, sp.wcols), lambda l, j: (j, l, 0, 0))],
        out_shape=[jax.ShapeDtypeStruct((sp.layers, sp.rows, sp.cols), F32),
                   jax.ShapeDtypeStruct((4, sp.layers, sp.rows, sp.wcols), F32)],
        scratch_shapes=[pltpu.VMEM((4 if fetch_all else 1, sp.rows, sp.cols), F32), pltpu.SemaphoreType.DMA((4, npc))],
        name=f"pair_sum_{tag}", compiler_params=_params(("arbitrary", "arbitrary")))(g, r1)


def _chip_copies(srcs, dsts, ssem, rsem):
    x, y, c = _coords()
    copies = []
    for i, (src, dst) in enumerate(zip(srcs, dsts)):
        for t, (fx, fy) in enumerate(FLIPS):
            tx, ty = x ^ fx, y ^ fy
            copies.append(pltpu.make_async_remote_copy(src.at[2 * tx + ty], dst.at[t], ssem.at[3 * i + t], rsem.at[3 * i + t],
                                                       device_id=(tx, ty, c), device_id_type=MESH))
    return copies


def _slot_shapes(words):
    return [(3,) + w.shape[1:] for w in words]


def reduce_chips(words, specs, tag):
    ns = len(specs)

    def body(*refs):
        copies = _chip_copies(refs[:ns], refs[ns:2 * ns], *refs[2 * ns:])
        for cp in copies:
            cp.start()
        for cp in copies:
            cp.wait()

    return pl.pallas_call(
        body, in_specs=[ANY] * ns, out_specs=[ANY] * ns,
        out_shape=[jax.ShapeDtypeStruct(s, F32) for s in _slot_shapes(words)],
        scratch_shapes=[pltpu.SemaphoreType.DMA((3 * ns,)), pltpu.SemaphoreType.DMA((3 * ns,))],
        name=f"reduce_chips_{tag}", compiler_params=pltpu.CompilerParams(has_side_effects=True))(*words)


def chips_start(words, specs, tag):
    ns = len(specs)
    slots = [lax.empty(s, F32) for s in _slot_shapes(words)]

    def body(*refs):
        ssem, rsem = refs[2 * ns:2 * ns + 2]
        for cp in _chip_copies(refs[:ns], refs[ns:2 * ns], ssem, rsem):
            cp.start()
        refs[-1][...] = jnp.zeros_like(refs[-1])

    outs = pl.pallas_call(
        body, in_specs=[HBM] * (2 * ns),
        out_specs=[SEM, SEM] + [HBM] * (2 * ns) + [pl.BlockSpec(memory_space=pltpu.VMEM)],
        out_shape=[pltpu.SemaphoreType.DMA((3 * ns,)), pltpu.SemaphoreType.DMA((3 * ns,))]
        + [pltpu.HBM(a.shape, a.dtype) for a in list(words) + slots] + [jax.ShapeDtypeStruct((8, 128), F32)],
        input_output_aliases={i: 2 + i for i in range(2 * ns)},
        name=f"chips_start_{tag}", compiler_params=pltpu.CompilerParams(has_side_effects=EFFECT))(
            *_in_hbm(list(words) + slots))
    return outs[0], outs[1], outs[2:2 + ns], outs[2 + ns:2 + 2 * ns], outs[-1]


def chips_wait(ssem, rsem, words, slots, specs, after, tag):
    ns = len(specs)

    def body(*refs):
        for cp in _chip_copies(refs[:ns], refs[ns:2 * ns], refs[2 * ns], refs[2 * ns + 1]):
            cp.wait_send()
            cp.wait_recv()

    outs = pl.pallas_call(
        body, in_specs=[HBM] * (2 * ns) + [SEM, SEM, ANY], out_specs=[HBM] * (2 * ns),
        out_shape=[pltpu.HBM(a.shape, a.dtype) for a in list(words) + list(slots)],
        input_output_aliases={i: i for i in range(2 * ns)},
        name=f"chips_wait_{tag}", compiler_params=pltpu.CompilerParams(has_side_effects=EFFECT))(
            *words, *slots, ssem, rsem, after)
    return outs[ns:]


def sum_chips(own, r2, sp, tag):
    def body(own_ref, r_ref, o_ref):
        o_ref[...] = ((own_ref[...] + sp.from_words(r_ref[0])) + sp.from_words(r_ref[1])) + sp.from_words(r_ref[2])

    blk = pl.BlockSpec((None, sp.rows, sp.cols), lambda l: (l, 0, 0))
    return pl.pallas_call(
        body, grid=(sp.layers,), in_specs=[blk, pl.BlockSpec((3, None, sp.rows, sp.wcols), lambda l: (0, l, 0, 0))],
        out_specs=blk, out_shape=jax.ShapeDtypeStruct((sp.layers, sp.rows, sp.cols), F32),
        name=f"sum_chips_{tag}", compiler_params=_params(("arbitrary",)))(own, r2)


def reduce_scatter_start(grads, specs, names, dep, tag):
    theirs = reduce_pair(grads, specs, tag, dep)
    sums = [pair_sum(g, r1, sp, f"{tag}_{n}") for g, r1, sp, n in zip(grads, theirs, specs, names)]
    ssem, rsem, words, slots, token = chips_start([s[1] for s in sums], specs, tag)
    return (ssem, rsem, words, slots, [s[0] for s in sums]), token


def reduce_scatter_finish(state, after, specs, tag):
    ssem, rsem, words, slots, own = state
    return list(zip(own, chips_wait(ssem, rsem, words, slots, specs, after, tag)))


def reduce_scatter(grads, specs, names, tag):
    theirs = reduce_pair(grads, specs, tag)
    sums = [pair_sum(g, r1, sp, f"{tag}_{n}") for g, r1, sp, n in zip(grads, theirs, specs, names)]
    return list(zip([s[0] for s in sums], reduce_chips([s[1] for s in sums], specs, tag)))


def _adamw_math(w, g, m, v):
    c1 = 1.0 - ADAM_B1 ** ADAM_STEP
    c2 = 1.0 - ADAM_B2 ** ADAM_STEP
    m2 = ADAM_B1 * m + (1.0 - ADAM_B1) * g
    v2 = ADAM_B2 * v + (1.0 - ADAM_B2) * (g * g)
    return -ADAM_LR * ((m2 / c1) / (jnp.sqrt(v2 / c2) + ADAM_EPS) + ADAM_WD * w), m2, v2


def adamw(w, g, m, v, name):
    shape = w.shape
    cols = shape[-1]
    rows = math.prod(shape[:-1])
    tr = rows
    while tr * cols * 4 > (1 << 20) and tr % 16 == 0:
        tr //= 2

    def body(w_ref, g_ref, m_ref, v_ref, d_ref, nm_ref, nv_ref):
        d_ref[...], nm_ref[...], nv_ref[...] = _adamw_math(w_ref[...], g_ref[...], m_ref[...], v_ref[...])

    blk = pl.BlockSpec((tr, cols), lambda i: (i, 0))
    outs = pl.pallas_call(
        body, grid=(rows // tr,), in_specs=[blk] * 4, out_specs=[blk] * 3,
        out_shape=[jax.ShapeDtypeStruct((rows, cols), F32)] * 3,
        name=f"adamw_{name}", compiler_params=_params(("arbitrary",)))(
            *[a.reshape(rows, cols) for a in (w, g, m, v)])
    return [o.reshape(shape) for o in outs]


def adamw_layers(w, sums, m, v, sp, name):
    _, rows, cols = w.shape
    tc = _col_tile(cols)
    twc = tc // 2 if sp.packed else tc

    def body(w_ref, own0, r0, own1, r1, m_ref, v_ref, g_ref, d_ref, nm_ref, nv_ref):
        first = pl.program_id(0) == 0
        own = jnp.where(first, own0[...], own1[...])
        r = [sp.from_words(jnp.where(first, r0[t], r1[t])) for t in range(3)]
        g = ((own + r[0]) + r[1]) + r[2]
        g_ref[...] = g
        d_ref[...], nm_ref[...], nv_ref[...] = _adamw_math(w_ref[...], g, m_ref[...], v_ref[...])

    blk = pl.BlockSpec((None, rows, tc), lambda l, n: (l, 0, n))
    own = pl.BlockSpec((None, rows, tc), lambda l, n: (0, 0, n))
    slots = pl.BlockSpec((3, None, rows, twc), lambda l, n: (0, 0, 0, n))
    return pl.pallas_call(
        body, grid=(L, cols // tc), in_specs=[blk, own, slots, own, slots, blk, blk], out_specs=[blk] * 4,
        out_shape=[jax.ShapeDtypeStruct(w.shape, F32)] * 4,
        name=f"adamw_{name}", compiler_params=_params(("arbitrary", "arbitrary")))(
            w, sums[0][0], sums[0][1], sums[1][0], sums[1][1], m, v)


WEIGHTS = ("pre_norm_g", "w_in", "gm_ln_g", "gm_ln_b", "gm_ws", "gm_bs", "mla_q_norm_g", "mla_w_uq", "mla_kv_norm_g",
           "mla_w_ukv", "lru_conv_w", "lru_conv_b", "lru_w_a", "lru_b_a", "lru_w_x", "lru_b_x", "lru_lambda",
           "w_proj_a", "w_proj_b", "w_proj_c", "w_out", "post_norm_g")
SHARDED = ("w_in", "mla_w_uq", "mla_w_ukv", "lru_conv_w", "w_proj_a", "w_proj_b", "w_proj_c", "w_out")
REPLICATED = tuple(n for n in WEIGHTS if n not in SHARDED)


def _step(x, target, wts, ms, vs):
    t12 = lambda a: jnp.swapaxes(a, 1, 2)
    names = list(WEIGHT_SPECS)
    specs = [WEIGHT_SPECS[n] for n in names]
    tabs = _rope_tables()
    own = {"w_in_t": t12(wts["w_in"]), "wq": t12(wts["mla_w_uq"]), "wkv": t12(wts["mla_w_ukv"]),
           "conv": jnp.pad(t12(wts["lru_conv_w"]), ((0, 0), (0, 0), (0, 124))),
           "w_proj_a": wts["w_proj_a"], "w_proj_b": wts["w_proj_b"], "w_proj_c": wts["w_proj_c"], "w_out": wts["w_out"]}
    first, rest = ["w_in_t"], [n for n in names if n != "w_in_t"]
    sfirst, srest = [WEIGHT_SPECS[n] for n in first], [WEIGHT_SPECS[n] for n in rest]

    w = {n: wts[n] for n in REPLICATED}
    w["kv_g384"] = jnp.concatenate([wts["mla_kv_norm_g"], jnp.ones((L, 128), F32)], axis=1)
    w["wa_dense"] = _block_diag(wts["lru_w_a"])
    w["wx_dense"] = _block_diag(wts["lru_w_x"])

    def layer_weights(ns, words):
        gw = dict(zip(ns, words))
        gw["wq"] = gw["wq"].reshape(HEADS, 192, 384)
        gw["wkv"] = gw["wkv"].reshape(HEADS, 256, 128)
        gw["conv"] = gw["conv"][0, :, :4].T
        return gw

    placed = [{n: pack_place(own[n], WEIGHT_SPECS[n], l, f"w{l}_{n}") for n in names} for l in range(L)]
    words_of = lambda l, ns: [placed[l][n][0] for n in ns]
    bufs_of = lambda l, ns: [placed[l][n][1] for n in ns]
    later = {}

    win0 = gather_forward(gather_send(words_of(0, first), bufs_of(0, first), sfirst, "w0a"), sfirst, "w0a")
    ssem_b, rsem_b, wthru_b, fthru_b, token_b = gather_start(words_of(0, rest), bufs_of(0, rest), srest, win0[0], "w0b")

    def fwd0_mid(ya):
        rest0 = gather_forward(gather_wait(ssem_b, rsem_b, wthru_b, fthru_b, srest, ya, "w0b"), srest, "w0b")
        later["w1"] = gather_start(words_of(1, names), bufs_of(1, names), specs, rest0[0], "w1")
        later["gw0"] = layer_weights(first + rest, list(win0) + list(rest0))
        return later["gw0"], later["w1"][4]

    x1, saved0 = _layer_fwd(x, 0, w, {"w_in_t": win0[0]}, tabs, dep=token_b, mid=fwd0_mid)
    ssem1, rsem1, wthru1, fthru1, _ = later["w1"]
    words1 = gather_forward(gather_wait(ssem1, rsem1, wthru1, fthru1, specs, x1, "w1"), specs, "w1")
    gw0, gw1 = later["gw0"], layer_weights(names, words1)
    x2, saved1 = _layer_fwd(x1, 1, w, gw1, tabs)
    loss, dx2 = loss_head(x2, target)

    dx1, gg1, g1 = _layer_bwd(dx2, 1, w, gw1, tabs, saved1)
    state1, token1 = reduce_scatter_start([gg1[n] for n in names], specs, names, None, "g1")

    def bwd0_mid(gg, last):
        later["s1"] = reduce_scatter_finish(state1, last, specs, "g1")
        later["g0b"], token = reduce_scatter_start([gg[n] for n in rest], srest, rest, later["s1"][0][1], "g0b")
        return token

    dx0, gg0, g0 = _layer_bwd(dx1, 0, w, gw0, tabs, saved0, dep=token1, mid=bwd0_mid)
    s1 = dict(zip(names, later["s1"]))
    s0 = dict(zip(rest, reduce_scatter_finish(later["g0b"], dx0, srest, "g0b")))
    rep_flat = jnp.concatenate([jnp.stack([g0[n], g1[n]]).reshape(-1) for n in REPLICATED])
    rep_flat = jnp.pad(rep_flat, (0, REP_ROWS * NDEV * D - rep_flat.shape[0])).reshape(1, REP_ROWS * NDEV, D)
    s0["w_in_t"], rep_parts = reduce_scatter([gg0["w_in_t"], rep_flat], sfirst + [REP_SPEC], first + ["rep"], "g0a")
    rep_sum = sum_chips(*rep_parts, REP_SPEC, "rep")
    rep_full = all_gather([rep_sum], 0, [REP_SPEC], ["rep"], "rep")[0].reshape(-1)

    out = {}
    for n, key in (("w_in", "w_in_t"), ("mla_w_uq", "wq"), ("mla_w_ukv", "wkv")):
        res = adamw_layers(own[key], [s0[key], s1[key]], t12(ms[n]), t12(vs[n]), WEIGHT_SPECS[key], n)
        out[n] = [t12(r) for r in res]
    for n in ("w_proj_a", "w_proj_b", "w_proj_c", "w_out"):
        out[n] = adamw_layers(wts[n], [s0[n], s1[n]], ms[n], vs[n], WEIGHT_SPECS[n], n)
    conv_sp = WEIGHT_SPECS["conv"]
    g_conv = t12(jnp.concatenate([sum_chips(*s0["conv"], conv_sp, "conv0"), sum_chips(*s1["conv"], conv_sp, "conv1")])[:, :, :4])
    out["lru_conv_w"] = [g_conv] + adamw(wts["lru_conv_w"], g_conv, ms["lru_conv_w"], vs["lru_conv_w"], "lru_conv_w")
    at = 0
    for n in REPLICATED:
        size = math.prod(wts[n].shape)
        g = rep_full[at:at + size].reshape(wts[n].shape)
        out[n] = [g] + adamw(wts[n], g, ms[n], vs[n], n)
        at += size

    loss = lax.psum(loss, ("x", "y", "c"))
    return (loss, dx0[None], *[out[n][k] for k in range(4) for n in WEIGHTS])


def kernel(x, pre_norm_g, w_in, gm_ln_g, gm_ln_b, gm_ws, gm_bs, mla_q_norm_g, mla_w_uq, mla_kv_norm_g, mla_w_ukv, lru_conv_w, lru_conv_b, lru_w_a, lru_b_a, lru_w_x, lru_b_x, lru_lambda, w_proj_a, w_proj_b, w_proj_c, w_out, post_norm_g, loss_target, m_pre_norm_g, m_w_in, m_gm_ln_g, m_gm_ln_b, m_gm_ws, m_gm_bs, m_mla_q_norm_g, m_mla_w_uq, m_mla_kv_norm_g, m_mla_w_ukv, m_lru_conv_w, m_lru_conv_b, m_lru_w_a, m_lru_b_a, m_lru_w_x, m_lru_b_x, m_lru_lambda, m_w_proj_a, m_w_proj_b, m_w_proj_c, m_w_out, m_post_norm_g, v_pre_norm_g, v_w_in, v_gm_ln_g, v_gm_ln_b, v_gm_ws, v_gm_bs, v_mla_q_norm_g, v_mla_w_uq, v_mla_kv_norm_g, v_mla_w_ukv, v_lru_conv_w, v_lru_conv_b, v_lru_w_a, v_lru_b_a, v_lru_w_x, v_lru_b_x, v_lru_lambda, v_w_proj_a, v_w_proj_b, v_w_proj_c, v_w_out, v_post_norm_g):
    wts = dict(zip(WEIGHTS, (pre_norm_g, w_in, gm_ln_g, gm_ln_b, gm_ws, gm_bs, mla_q_norm_g, mla_w_uq, mla_kv_norm_g,
                             mla_w_ukv, lru_conv_w, lru_conv_b, lru_w_a, lru_b_a, lru_w_x, lru_b_x, lru_lambda,
                             w_proj_a, w_proj_b, w_proj_c, w_out, post_norm_g)))
    ms = dict(zip(WEIGHTS, (m_pre_norm_g, m_w_in, m_gm_ln_g, m_gm_ln_b, m_gm_ws, m_gm_bs, m_mla_q_norm_g, m_mla_w_uq,
                            m_mla_kv_norm_g, m_mla_w_ukv, m_lru_conv_w, m_lru_conv_b, m_lru_w_a, m_lru_b_a, m_lru_w_x,
                            m_lru_b_x, m_lru_lambda, m_w_proj_a, m_w_proj_b, m_w_proj_c, m_w_out, m_post_norm_g)))
    vs = dict(zip(WEIGHTS, (v_pre_norm_g, v_w_in, v_gm_ln_g, v_gm_ln_b, v_gm_ws, v_gm_bs, v_mla_q_norm_g, v_mla_w_uq,
                            v_mla_kv_norm_g, v_mla_w_ukv, v_lru_conv_w, v_lru_conv_b, v_lru_w_a, v_lru_b_a, v_lru_w_x,
                            v_lru_b_x, v_lru_lambda, v_w_proj_a, v_w_proj_b, v_w_proj_c, v_w_out, v_post_norm_g)))
    return _step(x[0], loss_target[0], wts, ms, vs)
```

```python
import functools
import math

import jax
import jax.numpy as jnp
from jax import lax
from jax.experimental import pallas as pl
from jax.experimental.pallas import tpu as pltpu

F32 = jnp.float32
BF16 = jnp.bfloat16

T = 2048
D = 1024
L = 2
NDEV = 8
EPS = 1e-6
CHUNK_SHIFT = 6
HEADS = 8
QK = 192
LRU_W = 1280
LRU_TILE = 640
N_IN = 10432
SHARD = N_IN // NDEV
OFF_U, OFF_V, OFF_ZA, OFF_CQ, OFF_CKV, OFF_ZB = 0, 1024, 2048, 3072, 3456, 3840
OFF_XC, OFF_ZC, OFF_GA, OFF_GB, OFF_GC = 5120, 6400, 7680, 8704, 9728
NPAD = 10752
PAD1_AT, PAD1 = 3776, 64
PAD2_AT, PAD2 = 4800, 256
WIN_PIECES = ((0, 888), (888, 280), (1168, 136))
VMEM_LIMIT = 60 * 1024 * 1024

ADAM_LR, ADAM_B1, ADAM_B2, ADAM_EPS, ADAM_WD, ADAM_STEP = 0.001, 0.9, 0.999, 1e-08, 0.01, 10

_NN = (((1,), (0,)), ((), ()))
_NT = (((1,), (1,)), ((), ()))
_TN = (((0,), (0,)), ((), ()))


def _dg(a, b, dims):
    return lax.dot_general(a.astype(BF16), b.astype(BF16), dims, preferred_element_type=F32)


@jax.custom_vjp
def dot_nn(a, b):
    return _dg(a, b, _NN)


def _nn_fwd(a, b):
    return _dg(a, b, _NN), (a, b)


def _nn_bwd(res, g):
    a, b = res
    return _dg(g, b, _NT).astype(a.dtype), _dg(a, g, _TN).astype(b.dtype)


dot_nn.defvjp(_nn_fwd, _nn_bwd)


@jax.custom_vjp
def dot_nt(a, b):
    return _dg(a, b, _NT)


def _nt_fwd(a, b):
    return _dg(a, b, _NT), (a, b)


def _nt_bwd(res, g):
    a, b = res
    return _dg(g, b, _NN).astype(a.dtype), _dg(g, a, _TN).astype(b.dtype)


dot_nt.defvjp(_nt_fwd, _nt_bwd)


def _params(sem=None):
    return pltpu.CompilerParams(dimension_semantics=sem, vmem_limit_bytes=VMEM_LIMIT)


def _sigmoid(x):
    return 1.0 / (1.0 + jnp.exp(-x))


def _silu(x):
    return x * _sigmoid(x)


def _rms(x, g):
    ms = jnp.mean(x * x, axis=-1, keepdims=True)
    return x * lax.rsqrt(ms + EPS) * g


def _acc(ref, val, first):
    @pl.when(first)
    def _():
        ref[...] = val

    @pl.when(jnp.logical_not(first))
    def _():
        ref[...] += val


ANY = pl.BlockSpec(memory_space=pl.ANY)


INPROJ_TN = 512


def inproj_fwd(x, g, wt, l, dep=None):
    tn = INPROJ_TN

    def body(x_ref, g_ref, w_ref, *rest):
        proj_ref, h_ref = rest[-2:]

        @pl.when(pl.program_id(0) == 0)
        def _():
            h_ref[...] = _rms(x_ref[...], g_ref[...]).astype(BF16)

        proj_ref[...] = lax.dot_general(h_ref[...], _unpack(w_ref[...]).astype(BF16), _NT, preferred_element_type=F32)

    deps = [] if dep is None else [dep]
    return pl.pallas_call(
        body, grid=(NPAD // tn,),
        in_specs=[pl.BlockSpec((T, D), lambda j: (0, 0)), pl.BlockSpec((1, D), lambda j: (0, 0)),
                  pl.BlockSpec((None, tn, D // 2), lambda j: (0, j, 0))] + [ANY] * len(deps),
        out_specs=[pl.BlockSpec((T, tn), lambda j: (0, j)), pl.BlockSpec((T, D), lambda j: (0, 0))],
        out_shape=[jax.ShapeDtypeStruct((T, NPAD), F32), jax.ShapeDtypeStruct((T, D), BF16)],
        name=f"inproj_fwd_l{l}", compiler_params=_params(("arbitrary",)))(x, g, wt, *deps)


def inproj_bwd(dproj, h, wt, l, dep=None):
    tn = INPROJ_TN
    deps = [] if dep is None else [dep]

    def body(dp_ref, h_ref, w_ref, *rest):
        dwt_ref, dh_ref = rest[-2:]
        dp = dp_ref[...]
        dwt_ref[...] = lax.dot_general(dp, h_ref[...], _TN, preferred_element_type=F32)
        contrib = lax.dot_general(dp, _unpack(w_ref[...]).astype(BF16), _NN, preferred_element_type=F32)
        _acc(dh_ref, contrib, pl.program_id(0) == 0)

    return pl.pallas_call(
        body, grid=(NPAD // tn,),
        in_specs=[pl.BlockSpec((T, tn), lambda j: (0, j)), pl.BlockSpec((T, D), lambda j: (0, 0)),
                  pl.BlockSpec((None, tn, D // 2), lambda j: (0, j, 0))] + [ANY] * len(deps),
        out_specs=[pl.BlockSpec((None, tn, D), lambda j: (0, j, 0)), pl.BlockSpec((T, D), lambda j: (0, 0))],
        out_shape=[jax.ShapeDtypeStruct((1, NPAD, D), F32), jax.ShapeDtypeStruct((T, D), F32)],
        name=f"inproj_bwd_l{l}", compiler_params=_params(("arbitrary",)))(dproj, h, wt, *deps)


def prenorm_bwd(x, g, dh, dxn, l):
    tm = 256

    def body(x_ref, g_ref, dh_ref, dxn_ref, dx_ref, dg_ref):
        _, vjp = jax.vjp(_rms, x_ref[...], g_ref[...])
        dx, dg = vjp(dh_ref[...])
        dx_ref[...] = dx + dxn_ref[...]
        _acc(dg_ref, dg, pl.program_id(0) == 0)

    tok = pl.BlockSpec((tm, D), lambda i: (i, 0))
    vec = pl.BlockSpec((1, D), lambda i: (0, 0))
    return pl.pallas_call(
        body, grid=(T // tm,), in_specs=[tok, vec, tok, tok], out_specs=[tok, vec],
        out_shape=[jax.ShapeDtypeStruct((T, D), F32), jax.ShapeDtypeStruct((1, D), F32)],
        name=f"prenorm_bwd_l{l}", compiler_params=_params(("arbitrary",)))(x, g, dh, dxn)


def _gmlp_tile(u, v, z, ln_g, ln_b, ws, bs):
    mu = jnp.mean(v, axis=-1, keepdims=True)
    vc = v - mu
    var = jnp.mean(vc * vc, axis=-1, keepdims=True)
    vn = vc * lax.rsqrt(var + EPS) * ln_g + ln_b
    qi = lax.broadcasted_iota(jnp.int32, (128, 128), 0) >> CHUNK_SHIFT
    kj = lax.broadcasted_iota(jnp.int32, (128, 128), 1) >> CHUNK_SHIFT
    mask = kj <= qi
    outs = []
    for g in range(4):
        wm = jnp.where(mask, ws[g], 0.0)
        outs.append(dot_nn(wm, vn[:, 256 * g:256 * (g + 1)]) + bs[g])
    sv = jnp.concatenate(outs, axis=1)
    return u * sv * _silu(z)


def _gmlp_specs():
    blk = lambda c: pl.BlockSpec((128, 1024), lambda n, c=c: (n, c))
    vec = pl.BlockSpec((1, 1024), lambda n: (0, 0))
    return [blk(0), blk(1), blk(2), vec, vec,
            pl.BlockSpec((4, 128, 128), lambda n: (0, 0, 0)), pl.BlockSpec((4, 128, 1), lambda n: (0, 0, 0))]


def gmlp_fwd(proj, ln_g, ln_b, ws, bs, l):
    def body(u_ref, v_ref, z_ref, g_ref, b_ref, ws_ref, bs_ref, y_ref):
        y_ref[...] = _gmlp_tile(u_ref[...], v_ref[...], z_ref[...], g_ref[...], b_ref[...],
                                [ws_ref[g] for g in range(4)], [bs_ref[g] for g in range(4)])

    return pl.pallas_call(
        body, grid=(T // 128,), in_specs=_gmlp_specs(),
        out_specs=pl.BlockSpec((128, 1024), lambda n: (n, 0)),
        out_shape=jax.ShapeDtypeStruct((T, 1024), F32),
        name=f"gmlp_fwd_l{l}", compiler_params=_params(("arbitrary",)))(proj, proj, proj, ln_g, ln_b, ws, bs)


def gmlp_bwd(proj, ln_g, ln_b, ws, bs, dy, l):
    def body(u_ref, v_ref, z_ref, g_ref, b_ref, ws_ref, bs_ref, dy_ref, dseg_ref, dg_ref, db_ref, dws_ref, dbs_ref):
        first = pl.program_id(0) == 0
        _, vjp = jax.vjp(_gmlp_tile, u_ref[...], v_ref[...], z_ref[...], g_ref[...], b_ref[...],
                         [ws_ref[g] for g in range(4)], [bs_ref[g] for g in range(4)])
        du, dv, dz, dg, db, dws, dbs = vjp(dy_ref[...])
        dseg_ref[:, 0:1024] = du.astype(BF16)
        dseg_ref[:, 1024:2048] = dv.astype(BF16)
        dseg_ref[:, 2048:3072] = dz.astype(BF16)
        _acc(dg_ref, dg, first)
        _acc(db_ref, db, first)
        for g in range(4):
            _acc(dws_ref.at[g], dws[g], first)
            _acc(dbs_ref.at[g], dbs[g], first)

    vec = pl.BlockSpec((1, 1024), lambda n: (0, 0))
    return pl.pallas_call(
        body, grid=(T // 128,), in_specs=_gmlp_specs() + [pl.BlockSpec((128, 1024), lambda n: (n, 0))],
        out_specs=[pl.BlockSpec((128, 3072), lambda n: (n, 0)), vec, vec,
                   pl.BlockSpec((4, 128, 128), lambda n: (0, 0, 0)), pl.BlockSpec((4, 128, 1), lambda n: (0, 0, 0))],
        out_shape=[jax.ShapeDtypeStruct((T, 3072), BF16), jax.ShapeDtypeStruct((1, 1024), F32),
                   jax.ShapeDtypeStruct((1, 1024), F32), jax.ShapeDtypeStruct((4, 128, 128), F32),
                   jax.ShapeDtypeStruct((4, 128, 1), F32)],
        name=f"gmlp_bwd_l{l}", compiler_params=_params(("arbitrary",)))(proj, proj, proj, ln_g, ln_b, ws, bs, dy)


QKV_TM = 256


def _qkv_tile(cq, ckvr, qg, kvg, wq, wkv, ctab, stab):
    tm = cq.shape[0]
    cqn = _rms(cq, qg)
    lane = lax.broadcasted_iota(jnp.int32, ckvr.shape, 1)
    iskv = lane < 256
    ms = jnp.sum(jnp.where(iskv, ckvr * ckvr, 0.0), axis=-1, keepdims=True) * (1.0 / 256)
    lm = jnp.where(iskv, ckvr * lax.rsqrt(ms + EPS) * kvg, ckvr)
    r = lax.broadcasted_iota(jnp.int32, (64, 128), 0)
    c = lax.broadcasted_iota(jnp.int32, (64, 128), 1)
    eye = jnp.where(c == r, 1.0, 0.0)
    eye_sw = jnp.where(c == ((r + 32) & 63), 1.0, 0.0)
    z64 = jnp.zeros((64, 256), F32)
    z128 = jnp.zeros((128, 128), F32)
    rk_rope = jnp.concatenate([z64, eye], axis=1)
    rk_sw = jnp.concatenate([jnp.zeros((128, 384), F32), jnp.concatenate([z64, eye_sw], axis=1)], axis=0)
    k_sw = dot_nt(lm, rk_sw) * stab
    qs, ks, vs = [], [], []
    for h in range(HEADS):
        wn, w1, w2 = wq[h]
        wk, wv = wkv[h]
        wq_h = jnp.concatenate([wn, w1, w2], axis=0)
        wq_sw = jnp.concatenate([jnp.zeros((128, 384), F32), w2, w1], axis=0)
        qs.append(dot_nt(cqn, wq_h) * ctab + dot_nt(cqn, wq_sw) * stab)
        rk_h = jnp.concatenate([jnp.concatenate([wk, z128], axis=1), rk_rope], axis=0)
        ks.append(dot_nt(lm, rk_h) * ctab + k_sw)
        vs.append(dot_nt(lm, jnp.concatenate([wv, z128], axis=1)))
    return qs, ks, vs


def _qkv_in_specs():
    tm = QKV_TM
    return [pl.BlockSpec((tm, 384), lambda i: (i, OFF_CQ // 384)), pl.BlockSpec((tm, 384), lambda i: (i, OFF_CKV // 384)),
            pl.BlockSpec((1, 384), lambda i: (0, 0)), pl.BlockSpec((1, 384), lambda i: (0, 0)),
            pl.BlockSpec((HEADS, 192, 384), lambda i: (0, 0, 0)), pl.BlockSpec((HEADS, 256, 128), lambda i: (0, 0, 0)),
            pl.BlockSpec((tm, 192), lambda i: (i, 0)), pl.BlockSpec((tm, 192), lambda i: (i, 0))]


def _qkv_weights(wq_ref, wkv_ref):
    wq = [(wq_ref[h, 0:128, :], wq_ref[h, 128:160, :], wq_ref[h, 160:192, :]) for h in range(HEADS)]
    wkv = [(_unpack(wkv_ref[h, 0:128, :]), _unpack(wkv_ref[h, 128:256, :])) for h in range(HEADS)]
    return wq, wkv


def qkv_fwd(proj, qg, kvg, wq, wkv, ctab, stab, l, dep=None):
    tm = QKV_TM
    deps = [] if dep is None else [dep]

    def body(cq_ref, ckvr_ref, qg_ref, kvg_ref, wq_ref, wkv_ref, c_ref, s_ref, *rest):
        q_ref, k_ref, v_ref = rest[-3:]
        wq_l, wkv_l = _qkv_weights(wq_ref, wkv_ref)
        qs, ks, vs = _qkv_tile(cq_ref[...], ckvr_ref[...], qg_ref[...], kvg_ref[...], wq_l, wkv_l, c_ref[...], s_ref[...])
        for h in range(HEADS):
            q_ref[h] = qs[h]
            k_ref[h] = ks[h]
            v_ref[h] = vs[h]

    return pl.pallas_call(
        body, grid=(T // tm,), in_specs=_qkv_in_specs() + [ANY] * len(deps),
        out_specs=[pl.BlockSpec((HEADS, tm, QK), lambda i: (0, i, 0)), pl.BlockSpec((HEADS, tm, QK), lambda i: (0, i, 0)),
                   pl.BlockSpec((HEADS, tm, 128), lambda i: (0, i, 0))],
        out_shape=[jax.ShapeDtypeStruct((HEADS, T, QK), F32), jax.ShapeDtypeStruct((HEADS, T, QK), F32),
                   jax.ShapeDtypeStruct((HEADS, T, 128), F32)],
        name=f"qkv_fwd_l{l}", compiler_params=_params(("arbitrary",)))(proj, proj, qg, kvg, wq, wkv, ctab, stab, *deps)


def qkv_bwd(proj, qg, kvg, wq, wkv, ctab, stab, dq, dk, dv, l):
    tm = QKV_TM

    def body(cq_ref, ckvr_ref, qg_ref, kvg_ref, wq_ref, wkv_ref, c_ref, s_ref, dq_ref, dk_ref, dv_ref,
             dseg_ref, dqg_ref, dkvg_ref, dwq_ref, dwkv_ref):
        first = pl.program_id(0) == 0
        wq_l, wkv_l = _qkv_weights(wq_ref, wkv_ref)
        c_tab, s_tab = c_ref[...], s_ref[...]
        fn = lambda cq, ckvr, qg_, kvg_, wq_, wkv_: _qkv_tile(cq, ckvr, qg_, kvg_, wq_, wkv_, c_tab, s_tab)
        _, vjp = jax.vjp(fn, cq_ref[...], ckvr_ref[...], qg_ref[...], kvg_ref[...], wq_l, wkv_l)
        cts = ([dq_ref[h] for h in range(HEADS)], [dk_ref[h] for h in range(HEADS)], [dv_ref[h] for h in range(HEADS)])
        dcq, dckvr, dqg, dkvg, dwq, dwkv = vjp(cts)
        dseg_ref[:, 0:384] = dcq.astype(BF16)
        dseg_ref[:, 384:768] = dckvr.astype(BF16)
        _acc(dqg_ref, dqg, first)
        _acc(dkvg_ref, dkvg, first)
        for h in range(HEADS):
            _acc(dwq_ref.at[h, 0:128, :], dwq[h][0], first)
            _acc(dwq_ref.at[h, 128:160, :], dwq[h][1], first)
            _acc(dwq_ref.at[h, 160:192, :], dwq[h][2], first)
            _acc(dwkv_ref.at[h, 0:128, :], dwkv[h][0], first)
            _acc(dwkv_ref.at[h, 128:256, :], dwkv[h][1], first)

    hq = pl.BlockSpec((HEADS, tm, QK), lambda i: (0, i, 0))
    return pl.pallas_call(
        body, grid=(T // tm,),
        in_specs=_qkv_in_specs() + [hq, hq, pl.BlockSpec((HEADS, tm, 128), lambda i: (0, i, 0))],
        out_specs=[pl.BlockSpec((tm, 768), lambda i: (i, 0)), pl.BlockSpec((1, 384), lambda i: (0, 0)),
                   pl.BlockSpec((1, 384), lambda i: (0, 0)), pl.BlockSpec((HEADS, 192, 384), lambda i: (0, 0, 0)),
                   pl.BlockSpec((HEADS, 256, 256), lambda i: (0, 0, 0))],
        out_shape=[jax.ShapeDtypeStruct((T, 768), BF16), jax.ShapeDtypeStruct((1, 384), F32),
                   jax.ShapeDtypeStruct((1, 384), F32), jax.ShapeDtypeStruct((HEADS, 192, 384), F32),
                   jax.ShapeDtypeStruct((HEADS, 256, 256), F32)],
        name=f"qkv_bwd_l{l}", compiler_params=_params(("arbitrary",)))(
            proj, proj, qg, kvg, wq, wkv, ctab, stab, dq, dk, dv)


ATT_TQ_FWD = 256
ATT_TQ_BWD = 512


def _attn_tile(q, kv_past, k, v, zb):
    q = q * (1.0 / math.sqrt(QK))
    s = dot_nt(q, k)
    qc = lax.broadcasted_iota(jnp.int32, s.shape, 0) >> CHUNK_SHIFT
    kc = lax.broadcasted_iota(jnp.int32, s.shape, 1) >> CHUNK_SHIFT
    s = jnp.where(kc <= qc, s, -1e30)
    m = jnp.max(s, axis=-1, keepdims=True)
    if kv_past is not None:
        sp = dot_nt(q, kv_past[0])
        m = jnp.maximum(m, jnp.max(sp, axis=-1, keepdims=True))
    m = lax.stop_gradient(m)
    p = jnp.exp(s - m)
    denom = jnp.sum(p, axis=-1, keepdims=True)
    o = dot_nn(p, v)
    if kv_past is not None:
        pp = jnp.exp(sp - m)
        denom = denom + jnp.sum(pp, axis=-1, keepdims=True)
        o = o + dot_nn(pp, kv_past[1])
    return o * (1.0 / denom) * _silu(zb)


def _attn_operands(k_ref, v_ref, g, tq):
    n = tq * g
    past = (k_ref[0:n, :], v_ref[0:n, :]) if g else None
    return past, k_ref[n:n + tq, :], v_ref[n:n + tq, :]


def _attn_in_specs(tq):
    return [pl.BlockSpec((None, tq, QK), lambda h, i: (h, i, 0)), pl.BlockSpec((None, T, QK), lambda h, i: (h, 0, 0)),
            pl.BlockSpec((None, T, 128), lambda h, i: (h, 0, 0)),
            pl.BlockSpec((tq, 128), lambda h, i: (i, OFF_ZB // 128 + h))]


def attn_fwd(q, k, v, proj, l):
    tq = ATT_TQ_FWD

    def body(q_ref, k_ref, v_ref, z_ref, y_ref):
        for g in range(T // tq):
            @pl.when(pl.program_id(1) == g)
            def _(g=g):
                past, k, v = _attn_operands(k_ref, v_ref, g, tq)
                y_ref[...] = _attn_tile(q_ref[...], past, k, v, z_ref[...])

    return pl.pallas_call(
        body, grid=(HEADS, T // tq), in_specs=_attn_in_specs(tq),
        out_specs=pl.BlockSpec((tq, 128), lambda h, i: (i, h)),
        out_shape=jax.ShapeDtypeStruct((T, 1024), F32),
        name=f"attn_fwd_l{l}", compiler_params=_params(("arbitrary", "arbitrary")))(q, k, v, proj)


def attn_bwd(q, k, v, proj, dy, l):
    tq = ATT_TQ_BWD

    def body(q_ref, k_ref, v_ref, z_ref, dy_ref, dq_ref, dk_ref, dv_ref, dz_ref):
        @pl.when(pl.program_id(1) == 0)
        def _():
            dk_ref[...] = jnp.zeros_like(dk_ref)
            dv_ref[...] = jnp.zeros_like(dv_ref)

        for g in range(T // tq):
            @pl.when(pl.program_id(1) == g)
            def _(g=g):
                n = tq * g
                past, k, v = _attn_operands(k_ref, v_ref, g, tq)
                _, vjp = jax.vjp(_attn_tile, q_ref[...], past, k, v, z_ref[...])
                dq, dpast, dk, dv, dz = vjp(dy_ref[...])
                dq_ref[...] = dq
                dz_ref[...] = dz.astype(BF16)
                dk_ref[n:n + tq, :] += dk
                dv_ref[n:n + tq, :] += dv
                if g:
                    dk_ref[0:n, :] += dpast[0]
                    dv_ref[0:n, :] += dpast[1]

    return pl.pallas_call(
        body, grid=(HEADS, T // tq),
        in_specs=_attn_in_specs(tq) + [pl.BlockSpec((tq, 128), lambda h, i: (i, h))],
        out_specs=[pl.BlockSpec((None, tq, QK), lambda h, i: (h, i, 0)), pl.BlockSpec((None, T, QK), lambda h, i: (h, 0, 0)),
                   pl.BlockSpec((None, T, 128), lambda h, i: (h, 0, 0)), pl.BlockSpec((tq, 128), lambda h, i: (i, h))],
        out_shape=[jax.ShapeDtypeStruct((HEADS, T, QK), F32), jax.ShapeDtypeStruct((HEADS, T, QK), F32),
                   jax.ShapeDtypeStruct((HEADS, T, 128), F32), jax.ShapeDtypeStruct((T, 1024), BF16)],
        name=f"attn_bwd_l{l}", compiler_params=_params(("arbitrary", "arbitrary")))(q, k, v, proj, dy)


LRU_TT = 256


def _lru_gates(xc, wa, wx, ba, bx, lam):
    r = _sigmoid(dot_nn(xc, wa) + ba)
    i = _sigmoid(dot_nn(xc, wx) + bx)
    sp = jnp.maximum(-lam, 0.0) + jnp.log1p(jnp.exp(-jnp.abs(lam)))
    log_a = -8.0 * r * sp
    a = jnp.exp(log_a)
    mult = jnp.sqrt(jnp.maximum(1.0 - jnp.exp(2.0 * log_a), 0.0))
    return a, mult * (i * xc)


def _shift_down(x, s, halo):
    n, c = x.shape
    r = pltpu.roll(x.reshape(n // 8, 8, c), s, 1)
    before = jnp.concatenate([pltpu.roll(halo, s, 0)[None], r[:-1]], axis=0)
    sub = lax.broadcasted_iota(jnp.int32, r.shape, 1)
    return jnp.where(sub >= s, r, before).reshape(n, c)


def _shift_up(x, s, halo):
    n, c = x.shape
    r = pltpu.roll(x.reshape(n // 8, 8, c), 8 - s, 1)
    after = jnp.concatenate([r[1:], pltpu.roll(halo, 8 - s, 0)[None]], axis=0)
    sub = lax.broadcasted_iota(jnp.int32, r.shape, 1)
    return jnp.where(sub < 8 - s, r, after).reshape(n, c)


def _conv(x, halo, w_ref, b):
    return (w_ref[3:4, :] * x + w_ref[2:3, :] * _shift_down(x, 1, halo) + w_ref[1:2, :] * _shift_down(x, 2, halo)
            + w_ref[0:1, :] * _shift_down(x, 3, halo) + b)


def _scan(a, b, reverse, carry):
    n, c = a.shape
    a, b = a.reshape(n // 8, 8, c), b.reshape(n // 8, 8, c)
    sub = lax.broadcasted_iota(jnp.int32, a.shape, 1)
    for d in (1, 2, 4):
        keep = sub < 8 - d if reverse else sub >= d
        shift = 8 - d if reverse else d
        a_sh = jnp.where(keep, pltpu.roll(a, shift, 1), 1.0)
        b_sh = jnp.where(keep, pltpu.roll(b, shift, 1), 0.0)
        b = a * b_sh + b
        a = a * a_sh
    a, b = a.reshape(n, c), b.reshape(n, c)
    groups = [None] * (n // 8)
    for g in (reversed(range(n // 8)) if reverse else range(n // 8)):
        h = a[8 * g:8 * g + 8] * carry + b[8 * g:8 * g + 8]
        groups[g] = h
        carry = h[0:1] if reverse else h[7:8]
    return jnp.concatenate(groups, axis=0), carry


def _lru_param_specs(l):
    ct = LRU_TILE
    vec = pl.BlockSpec((1, ct), lambda n, i: (0, n))
    mat = pl.BlockSpec((None, None, ct, ct), lambda n, i: (l, n, 0, 0))
    return [pl.BlockSpec((4, ct), lambda n, i: (0, n)), vec, mat, mat, vec, vec, vec]


def lru_fwd(proj, conv_w, conv_b, wa, wx, ba, bx, lam, l):
    tt, ct = LRU_TT, LRU_TILE

    def body(x_ref, z_ref, cw_ref, cb_ref, wa_ref, wx_ref, ba_ref, bx_ref, lam_ref, h_ref, y_ref, halo, hcar):
        @pl.when(pl.program_id(1) == 0)
        def _():
            halo[...] = jnp.zeros_like(halo)
            hcar[...] = jnp.zeros_like(hcar)

        x = x_ref[...]
        xc = _conv(x, halo[...], cw_ref, cb_ref[...])
        halo[...] = x[tt - 8:tt]
        a, b = _lru_gates(xc, wa_ref[...], wx_ref[...], ba_ref[...], bx_ref[...], lam_ref[...])
        h, hcar[...] = _scan(a, b, False, hcar[...])
        h_ref[...] = h
        y_ref[...] = h * _silu(z_ref[...])

    seq = pl.BlockSpec((tt, ct), lambda n, i: (i, n))
    return pl.pallas_call(
        body, grid=(LRU_W // ct, T // tt),
        in_specs=[pl.BlockSpec((tt, ct), lambda n, i: (i, OFF_XC // ct + n)),
                  pl.BlockSpec((tt, ct), lambda n, i: (i, OFF_ZC // ct + n))] + _lru_param_specs(l),
        out_specs=[seq, seq],
        out_shape=[jax.ShapeDtypeStruct((T, LRU_W), F32), jax.ShapeDtypeStruct((T, LRU_W), F32)],
        scratch_shapes=[pltpu.VMEM((8, ct), F32), pltpu.VMEM((1, ct), F32)],
        name=f"lru_fwd_l{l}", compiler_params=_params(("arbitrary", "arbitrary")))(
            proj, proj, conv_w, conv_b, wa, wx, ba, bx, lam)


def lru_bwd(proj, hseq, dy, conv_w, conv_b, wa, wx, ba, bx, lam, l):
    tt, ct = LRU_TT, LRU_TILE
    nt = T // tt
    rev = lambda i: nt - 1 - i
    prev8 = lambda i: jnp.maximum(rev(i) * (tt // 8) - 1, 0)

    def body(x_ref, xh_ref, z_ref, h_ref, hh_ref, dy_ref, cw_ref, cb_ref, wa_ref, wx_ref, ba_ref, bx_ref, lam_ref,
             dx_ref, dz_ref, dcw_ref, dcb_ref, dwa_ref, dwx_ref, dba_ref, dbx_ref, dlam_ref, gcar, dhalo):
        i = pl.program_id(1)
        first = i == 0

        @pl.when(first)
        def _():
            gcar[...] = jnp.zeros_like(gcar)
            dhalo[...] = jnp.zeros_like(dhalo)

        at_start = rev(i) == 0
        x = x_ref[...]
        xhalo = jnp.where(at_start, 0.0, xh_ref[...])
        sh = [x, _shift_down(x, 1, xhalo), _shift_down(x, 2, xhalo), _shift_down(x, 3, xhalo)]
        xc = (cw_ref[3:4, :] * sh[0] + cw_ref[2:3, :] * sh[1] + cw_ref[1:2, :] * sh[2] + cw_ref[0:1, :] * sh[3]
              + cb_ref[...])
        (a, b), vjp = jax.vjp(_lru_gates, xc, wa_ref[...], wx_ref[...], ba_ref[...], bx_ref[...], lam_ref[...])
        hs = h_ref[...]
        hprev = _shift_down(hs, 1, jnp.where(at_start, 0.0, hh_ref[...]))
        z = z_ref[...]
        sg = _sigmoid(z)
        dy = dy_ref[...]
        dz_ref[...] = (dy * hs * (sg * (1.0 + z * (1.0 - sg)))).astype(BF16)
        dh = dy * (z * sg)
        a_next = _shift_up(a, 1, jnp.ones((8, ct), F32))
        g, _ = _scan(a_next, dh, True, gcar[...])
        dxc, dwa, dwx, dba, dbx, dlam = vjp((g * hprev, g))
        dx = (cw_ref[3:4, :] * dxc + cw_ref[2:3, :] * _shift_up(dxc, 1, dhalo[...])
              + cw_ref[1:2, :] * _shift_up(dxc, 2, dhalo[...]) + cw_ref[0:1, :] * _shift_up(dxc, 3, dhalo[...]))
        dx_ref[...] = dx.astype(BF16)
        dhalo[...] = dxc[0:8]
        ag = a * g
        gcar[...] = ag[0:1]
        dcw = jnp.concatenate([jnp.sum(dxc * sh[3 - j], axis=0, keepdims=True) for j in range(4)], axis=0)
        _acc(dcw_ref, dcw, first)
        _acc(dcb_ref, jnp.sum(dxc, axis=0, keepdims=True), first)
        _acc(dwa_ref, dwa, first)
        _acc(dwx_ref, dwx, first)
        _acc(dba_ref, dba, first)
        _acc(dbx_ref, dbx, first)
        _acc(dlam_ref, dlam, first)

    xcol = OFF_XC // ct
    zcol = OFF_ZC // ct
    vec = pl.BlockSpec((1, ct), lambda n, i: (0, n))
    mat = pl.BlockSpec((None, ct, ct), lambda n, i: (n, 0, 0))
    seq = pl.BlockSpec((tt, ct), lambda n, i: (rev(i), n))
    return pl.pallas_call(
        body, grid=(LRU_W // ct, nt),
        in_specs=[pl.BlockSpec((tt, ct), lambda n, i: (rev(i), xcol + n)),
                  pl.BlockSpec((8, ct), lambda n, i: (prev8(i), xcol + n)),
                  pl.BlockSpec((tt, ct), lambda n, i: (rev(i), zcol + n)),
                  seq, pl.BlockSpec((8, ct), lambda n, i: (prev8(i), n)), seq] + _lru_param_specs(l),
        out_specs=[seq, seq, pl.BlockSpec((4, ct), lambda n, i: (0, n)), vec, mat, mat, vec, vec, vec],
        out_shape=[jax.ShapeDtypeStruct((T, LRU_W), BF16), jax.ShapeDtypeStruct((T, LRU_W), BF16),
                   jax.ShapeDtypeStruct((4, LRU_W), F32), jax.ShapeDtypeStruct((1, LRU_W), F32),
                   jax.ShapeDtypeStruct((2, ct, ct), F32), jax.ShapeDtypeStruct((2, ct, ct), F32),
                   jax.ShapeDtypeStruct((1, LRU_W), F32), jax.ShapeDtypeStruct((1, LRU_W), F32),
                   jax.ShapeDtypeStruct((1, LRU_W), F32)],
        scratch_shapes=[pltpu.VMEM((1, ct), F32), pltpu.VMEM((8, ct), F32)],
        name=f"lru_bwd_l{l}", compiler_params=_params(("arbitrary", "arbitrary")))(
            proj, proj, proj, hseq, hseq, dy, conv_w, conv_b, wa, wx, ba, bx, lam)


def proj_bwd(y, dp, w, l, tag):
    tm = 512
    k = y.shape[1]

    def body(y_ref, dp_ref, w_ref, dy_ref, dw_ref):
        dp = dp_ref[...]
        dy_ref[...] = _dg(dp, _unpack(w_ref[...]), _NT)
        _acc(dw_ref, _dg(y_ref[...], dp, _TN), pl.program_id(0) == 0)

    return pl.pallas_call(
        body, grid=(T // tm,),
        in_specs=[pl.BlockSpec((tm, k), lambda i: (i, 0)), pl.BlockSpec((tm, D), lambda i: (i, 0)),
                  pl.BlockSpec((None, k, D // 2), lambda i: (0, 0, 0))],
        out_specs=[pl.BlockSpec((tm, k), lambda i: (i, 0)), pl.BlockSpec((None, k, D), lambda i: (0, 0, 0))],
        out_shape=[jax.ShapeDtypeStruct((T, k), F32), jax.ShapeDtypeStruct((1, k, D), F32)],
        name=f"proj_{tag}_bwd_l{l}", compiler_params=_params(("arbitrary",)))(y, dp, w)


OUT_TM = 256


def _out_tile(pa, pb, pc, ga, gb, gc, wout, post_g):
    merged = _sigmoid(ga) * pa + _sigmoid(gb) * pb + _sigmoid(gc) * pc
    return _rms(dot_nn(merged, wout), post_g)


def _out_in_specs():
    tm = OUT_TM
    tok = pl.BlockSpec((tm, D), lambda i: (i, 0))
    gate = lambda off: pl.BlockSpec((tm, 512), lambda i, off=off: (i, off // 512))
    return [tok, tok, tok, gate(OFF_GA), gate(OFF_GA + 512), gate(OFF_GB), gate(OFF_GB + 512), gate(OFF_GC),
            gate(OFF_GC + 512), pl.BlockSpec((None, D, D // 2), lambda i: (0, 0, 0)), pl.BlockSpec((1, D), lambda i: (0, 0))]


def _gates(refs):
    return [jnp.concatenate([refs[2 * j][...], refs[2 * j + 1][...]], axis=1) for j in range(3)]


def out_fwd(x, ya, yb, yc, proj, wpa, wpb, wpc, wout, post_g, l):
    tm = OUT_TM

    def body(ya_ref, yb_ref, yc_ref, g0, g1, g2, g3, g4, g5, wo_ref, pg_ref, x_ref, wa_ref, wb_ref, wc_ref,
             o_ref, pa_ref, pb_ref, pc_ref, wa, wb, wc, wo):
        @pl.when(pl.program_id(0) == 0)
        def _():
            for dst, src in ((wa, wa_ref), (wb, wb_ref), (wc, wc_ref), (wo, wo_ref)):
                dst[...] = _unpack(src[...]).astype(BF16)

        pa = _dg(ya_ref[...], wa[...], _NN)
        pb = _dg(yb_ref[...], wb[...], _NN)
        pc = _dg(yc_ref[...], wc[...], _NN)
        ga, gb, gc = _gates([g0, g1, g2, g3, g4, g5])
        o_ref[...] = x_ref[...] + _out_tile(pa, pb, pc, ga, gb, gc, wo[...], pg_ref[...])
        pa_ref[...] = pa.astype(BF16)
        pb_ref[...] = pb.astype(BF16)
        pc_ref[...] = pc.astype(BF16)

    tok = pl.BlockSpec((tm, D), lambda i: (i, 0))
    words = lambda k: pl.BlockSpec((None, k, D // 2), lambda i: (0, 0, 0))
    specs = _out_in_specs()
    specs[2] = pl.BlockSpec((tm, LRU_W), lambda i: (i, 0))
    return pl.pallas_call(
        body, grid=(T // tm,), in_specs=specs + [tok, words(D), words(D), words(LRU_W)], out_specs=[tok] * 4,
        out_shape=[jax.ShapeDtypeStruct((T, D), F32)] + [jax.ShapeDtypeStruct((T, D), BF16)] * 3,
        scratch_shapes=[pltpu.VMEM((D, D), BF16), pltpu.VMEM((D, D), BF16), pltpu.VMEM((LRU_W, D), BF16),
                        pltpu.VMEM((D, D), BF16)],
        name=f"out_fwd_l{l}", compiler_params=_params(("arbitrary",)))(
            ya, yb, yc, proj, proj, proj, proj, proj, proj, wout, post_g, x, wpa, wpb, wpc)


def out_bwd(pa, pb, pc, proj, wout, post_g, dxn, l, dep=None):
    tm = OUT_TM

    def body(pa_ref, pb_ref, pc_ref, g0, g1, g2, g3, g4, g5, w_ref, pg_ref, dxn_ref, *rest):
        dpa_ref, dpb_ref, dpc_ref, dg_ref, dw_ref, dpg_ref = rest[-6:]
        first = pl.program_id(0) == 0
        ga, gb, gc = _gates([g0, g1, g2, g3, g4, g5])
        _, vjp = jax.vjp(_out_tile, pa_ref[...], pb_ref[...], pc_ref[...], ga, gb, gc, _unpack(w_ref[...]), pg_ref[...])
        dpa, dpb, dpc, dga, dgb, dgc, dw, dpg = vjp(dxn_ref[...])
        dpa_ref[...] = dpa.astype(BF16)
        dpb_ref[...] = dpb.astype(BF16)
        dpc_ref[...] = dpc.astype(BF16)
        dg_ref[:, 0:1024] = dga.astype(BF16)
        dg_ref[:, 1024:2048] = dgb.astype(BF16)
        dg_ref[:, 2048:3072] = dgc.astype(BF16)
        _acc(dw_ref, dw, first)
        _acc(dpg_ref, dpg, first)

    tok = pl.BlockSpec((tm, D), lambda i: (i, 0))
    deps = [] if dep is None else [dep]
    return pl.pallas_call(
        body, grid=(T // tm,), in_specs=_out_in_specs() + [tok] + [ANY] * len(deps),
        out_specs=[tok, tok, tok, pl.BlockSpec((tm, 3072), lambda i: (i, 0)),
                   pl.BlockSpec((None, D, D), lambda i: (0, 0, 0)), pl.BlockSpec((1, D), lambda i: (0, 0))],
        out_shape=[jax.ShapeDtypeStruct((T, D), BF16)] * 3 + [jax.ShapeDtypeStruct((T, 3072), BF16),
                                                            jax.ShapeDtypeStruct((1, D, D), F32), jax.ShapeDtypeStruct((1, D), F32)],
        name=f"out_bwd_l{l}", compiler_params=_params(("arbitrary",)))(
            pa, pb, pc, proj, proj, proj, proj, proj, proj, wout, post_g, dxn, *deps)


def loss_head(y, target):
    tm = 256

    def body(y_ref, t_ref, loss_ref, dy_ref):
        e = y_ref[...] - t_ref[...]
        dy_ref[...] = e * (1.0 / D)
        val = 0.5 * jnp.sum(jnp.mean(e * e, axis=-1, keepdims=True), axis=0, keepdims=True)
        _acc(loss_ref, jnp.broadcast_to(val, (8, 128)), pl.program_id(0) == 0)

    tok = pl.BlockSpec((tm, D), lambda i: (i, 0))
    total, dy = pl.pallas_call(
        body, grid=(T // tm,), in_specs=[tok, tok],
        out_specs=[pl.BlockSpec((8, 128), lambda i: (0, 0)), tok],
        out_shape=[jax.ShapeDtypeStruct((8, 128), F32), jax.ShapeDtypeStruct((T, D), F32)],
        name="loss_head", compiler_params=_params(("arbitrary",)))(y, target)
    return total[0, 0], dy


def _rope_tables():
    pos = jnp.arange(T, dtype=F32)
    inv_freq = 10000.0 ** (-jnp.arange(0, 64, 2, dtype=F32) / 64)
    ang = pos[:, None] * inv_freq[None, :]
    cos, sin = jnp.cos(ang), jnp.sin(ang)
    ctab = jnp.concatenate([jnp.ones((T, 128), F32), cos, cos], axis=1)
    stab = jnp.concatenate([jnp.zeros((T, 128), F32), -sin, sin], axis=1)
    return ctab, stab


def _block_diag(w):
    w5 = w.reshape(L, 2, 8, 80, 80)
    eye = jnp.eye(8, dtype=w.dtype)
    return jnp.einsum("lnbij,bc->lnbicj", w5, eye).reshape(L, 2, LRU_TILE, LRU_TILE)


def _block_diag_t(dw):
    dw5 = dw.reshape(2, 8, 80, 8, 80)
    return jnp.einsum("nbicj,bc->nbij", dw5, jnp.eye(8, dtype=dw.dtype)).reshape(16, 80, 80)


def _layer_fwd(x, l, w, gw, tabs, dep=None, mid=None):
    row = lambda a: a[l][None]
    proj, h = inproj_fwd(x, row(w["pre_norm_g"]), gw["w_in_t"], l, dep)
    ya = gmlp_fwd(proj, row(w["gm_ln_g"]), row(w["gm_ln_b"]), w["gm_ws"][l], w["gm_bs"][l][..., None], l)
    dep2 = None
    if mid is not None:
        gw, dep2 = mid(ya)
    q, k, v = qkv_fwd(proj, row(w["mla_q_norm_g"]), row(w["kv_g384"]), gw["wq"], gw["wkv"], tabs[0], tabs[1], l, dep2)
    yb = attn_fwd(q, k, v, proj, l)
    hseq, yc = lru_fwd(proj, gw["conv"], row(w["lru_conv_b"]), w["wa_dense"], w["wx_dense"],
                       row(w["lru_b_a"]), row(w["lru_b_x"]), row(w["lru_lambda"]), l)
    xn, pa, pb, pc = out_fwd(x, ya, yb, yc, proj, gw["w_proj_a"], gw["w_proj_b"], gw["w_proj_c"], gw["w_out"],
                             row(w["post_norm_g"]), l)
    return xn, (x, proj, h, ya, q, k, v, yb, hseq, yc, pa, pb, pc)


def _layer_bwd(dxn, l, w, gw, tabs, saved, dep=None, mid=None):
    x, proj, h, ya, q, k, v, yb, hseq, yc, pa, pb, pc = saved
    row = lambda a: a[l][None]
    g, gg = {}, {}
    dpa, dpb, dpc, dgates, gg["w_out"], dpost = out_bwd(pa, pb, pc, proj, gw["w_out"], row(w["post_norm_g"]), dxn, l, dep)
    g["post_norm_g"] = dpost[0]
    dya, gg["w_proj_a"] = proj_bwd(ya, dpa, gw["w_proj_a"], l, "a")
    dyb, gg["w_proj_b"] = proj_bwd(yb, dpb, gw["w_proj_b"], l, "b")
    dyc, gg["w_proj_c"] = proj_bwd(yc, dpc, gw["w_proj_c"], l, "c")
    dseg_a, dln_g, dln_b, g["gm_ws"], dbs = gmlp_bwd(proj, row(w["gm_ln_g"]), row(w["gm_ln_b"]), w["gm_ws"][l],
                                                    w["gm_bs"][l][..., None], dya, l)
    g["gm_ln_g"], g["gm_ln_b"], g["gm_bs"] = dln_g[0], dln_b[0], dbs[..., 0]
    dq, dk, dv, dzb = attn_bwd(q, k, v, proj, dyb, l)
    dseg_q, dqg, dkvg, dwq, dwkv = qkv_bwd(proj, row(w["mla_q_norm_g"]), row(w["kv_g384"]), gw["wq"], gw["wkv"],
                                           tabs[0], tabs[1], dq, dk, dv, l)
    gg["wq"], gg["wkv"] = dwq.reshape(1, 1536, 384), dwkv.reshape(1, 2048, 256)
    g["mla_q_norm_g"], g["mla_kv_norm_g"] = dqg[0], dkvg[0, :256]
    dxc, dzc, dcw, dcb, dwa, dwx, dba, dbx, dlam = lru_bwd(
        proj, hseq, dyc, gw["conv"], row(w["lru_conv_b"]), w["wa_dense"], w["wx_dense"],
        row(w["lru_b_a"]), row(w["lru_b_x"]), row(w["lru_lambda"]), l)
    gg["conv"] = jnp.pad(dcw.T, ((0, 0), (0, 124)))[None]
    g["lru_conv_b"], g["lru_b_a"], g["lru_b_x"], g["lru_lambda"] = dcb[0], dba[0], dbx[0], dlam[0]
    g["lru_w_a"], g["lru_w_x"] = _block_diag_t(dwa), _block_diag_t(dwx)
    dproj = jnp.concatenate([dseg_a, dseg_q, dzb, jnp.zeros((T, PAD2), dzb.dtype), dxc, dzc, dgates], axis=1)
    dep2 = mid(gg, dproj) if mid is not None else None
    gg["w_in_t"], dh = inproj_bwd(dproj, h, gw["w_in_t"], l, dep2)
    dx, dpre = prenorm_bwd(x, row(w["pre_norm_g"]), dh, dxn, l)
    g["pre_norm_g"] = dpre[0]
    return dx, gg, g


MESH = pl.DeviceIdType.MESH
HBM = pl.BlockSpec(memory_space=pltpu.HBM)
SEM = pl.BlockSpec(memory_space=pltpu.SEMAPHORE)
EFFECT = pltpu.SideEffectType.DATAFLOW_SIDE_EFFECTING
FLIPS = ((1, 0), (0, 1), (1, 1))


def _win_off(k, s):
    g = SHARD * k + s
    return g + jnp.where(g >= PAD1_AT, PAD1, 0) + jnp.where(g >= PAD2_AT, PAD2, 0)


def _plain_off(rows):
    return lambda k, s: rows * k + s


class Spec:
    def __init__(self, rows, cols, full_rows, pieces=None, off=None, layers=1, packed=None):
        self.rows, self.cols, self.full_rows, self.layers = rows, cols, full_rows, layers
        self.pieces = pieces or ((0, rows),)
        self.off = off or _plain_off(rows)
        self.packed = cols % 256 == 0 if packed is None else packed
        self.wcols = cols // 2 if self.packed else cols

    def to_words(self, a):
        return _pack(a) if self.packed else a

    def from_words(self, p):
        return _unpack(p) if self.packed else p


def _pack(a):
    bits = lambda v: lax.bitcast_convert_type(v.astype(jnp.bfloat16).astype(F32), jnp.uint32)
    words = [(bits(a[:, g:g + 128]) >> 16) | (bits(a[:, g + 128:g + 256]) & jnp.uint32(0xFFFF0000))
             for g in range(0, a.shape[-1], 256)]
    return lax.bitcast_convert_type(jnp.concatenate(words, axis=-1) if len(words) > 1 else words[0], F32)


def _unpack(p):
    w = lax.bitcast_convert_type(p, jnp.uint32)
    lo = lax.bitcast_convert_type(w << 16, F32)
    hi = lax.bitcast_convert_type(w & jnp.uint32(0xFFFF0000), F32)
    return jnp.concatenate([h[:, g:g + 128] for g in range(0, p.shape[-1], 128) for h in (lo, hi)], axis=-1)


WEIGHT_SPECS = {
    "w_in_t": Spec(SHARD, D, NPAD, WIN_PIECES, _win_off),
    "wq": Spec(192, 384, 1536),
    "wkv": Spec(256, 256, 2048),
    "conv": Spec(160, 128, 1280),
    "w_proj_a": Spec(128, D, 1024),
    "w_proj_b": Spec(128, D, 1024),
    "w_proj_c": Spec(160, D, 1280),
    "w_out": Spec(128, D, 1024),
}
REP_ROWS = 72
REP_SPEC = Spec(REP_ROWS, D, REP_ROWS * NDEV, packed=False)


def _coords():
    return lax.axis_index("x"), lax.axis_index("y"), lax.axis_index("c")


def _rows(ref, start, n):
    if not isinstance(start, int):
        start = pl.multiple_of(start, 8)
    return ref.at[:, pl.ds(start, n), :]


def _col_tile(cols):
    return 256 if cols % 256 == 0 else cols


def _n_pieces(specs):
    return sum(len(sp.pieces) for sp in specs)


def pack_place(shard, sp, layer, tag):
    gaps = ((PAD1_AT, PAD1), (PAD2_AT + PAD1, PAD2)) if sp.off is _win_off else ()
    npc = len(sp.pieces)

    def body(s_ref, words_ref, full_ref, buf, zbuf, sem):
        l = 0
        x, y, c = _coords()
        me = 4 * x + 2 * y + c
        words = sp.to_words(s_ref[...])
        words_ref[...] = words
        buf[...] = words
        copies = [pltpu.make_async_copy(buf.at[pl.ds(s, n), :],
                                        full_ref.at[l, pl.ds(pl.multiple_of(sp.off(me, s), 8), n), :], sem.at[i])
                  for i, (s, n) in enumerate(sp.pieces)]
        if gaps:
            zbuf[...] = jnp.zeros_like(zbuf)
            copies += [pltpu.make_async_copy(zbuf.at[pl.ds(0, n), :], full_ref.at[l, pl.ds(at, n), :], sem.at[npc + i])
                       for i, (at, n) in enumerate(gaps)]
        for cp in copies:
            cp.start()
        for cp in copies:
            cp.wait()

    return pl.pallas_call(
        body, grid=(1,), in_specs=[pl.BlockSpec((None, sp.rows, sp.cols), lambda i: (layer, 0, 0))],
        out_specs=[pl.BlockSpec((None, sp.rows, sp.wcols), lambda i: (0, 0, 0)), ANY],
        out_shape=[jax.ShapeDtypeStruct((sp.layers, sp.rows, sp.wcols), F32),
                   jax.ShapeDtypeStruct((sp.layers, sp.full_rows, sp.wcols), F32)],
        scratch_shapes=[pltpu.VMEM((sp.rows, sp.wcols), F32), pltpu.VMEM((PAD2 if gaps else 8, sp.wcols), F32),
                        pltpu.SemaphoreType.DMA((npc + len(gaps),))],
        name=f"pack_place_{tag}", compiler_params=_params(("arbitrary",)))(shard)


def _gather_copies(srcs, bufs, specs, ssem, rsem, landing):
    x, y, c = _coords()
    me = 4 * x + 2 * y + c
    targets = [(x, y, 1 - c)] + [(x ^ fx, y ^ fy, c) for fx, fy in FLIPS]
    copies = []
    p = 0
    for src, buf, sp in zip(srcs, bufs, specs):
        for s, n in sp.pieces:
            for t, (tx, ty, tc) in enumerate(targets):
                owner = 4 * tx + 2 * ty + tc if landing else me
                copies.append(pltpu.make_async_remote_copy(_rows(src, s, n), _rows(buf, sp.off(owner, s), n),
                                                           ssem.at[4 * p + t], rsem.at[4 * p + t],
                                                           device_id=(tx, ty, tc), device_id_type=MESH))
            p += 1
    return copies


def gather_send(words, fulls, specs, tag):
    ns, npc = len(specs), _n_pieces(specs)

    def body(*refs):
        srcs, bufs, sems = refs[:ns], refs[2 * ns:3 * ns], refs[3 * ns:]
        for cp in _gather_copies(srcs, bufs, specs, *sems, False):
            cp.start()
        for cp in _gather_copies(srcs, bufs, specs, *sems, False):
            cp.wait_send()
        for cp in _gather_copies(srcs, bufs, specs, *sems, True):
            cp.wait_recv()

    return pl.pallas_call(
        body, in_specs=[ANY] * (2 * ns), out_specs=[ANY] * ns,
        out_shape=[jax.ShapeDtypeStruct(f.shape, f.dtype) for f in fulls],
        input_output_aliases={ns + i: i for i in range(ns)},
        scratch_shapes=[pltpu.SemaphoreType.DMA((4 * npc,)), pltpu.SemaphoreType.DMA((4 * npc,))],
        name=f"gather_send_{tag}", compiler_params=pltpu.CompilerParams(has_side_effects=True))(*words, *fulls)


def _in_hbm(arrays):
    return [pltpu.with_memory_space_constraint(a, pltpu.HBM) for a in arrays]


def gather_start(words, fulls, specs, dep, tag):
    ns, npc = len(specs), _n_pieces(specs)

    def body(*refs):
        ssem, rsem = refs[2 * ns + 1:2 * ns + 3]
        for cp in _gather_copies(refs[:ns], refs[ns:2 * ns], specs, ssem, rsem, False):
            cp.start()
        refs[-1][...] = jnp.zeros_like(refs[-1])

    outs = pl.pallas_call(
        body, in_specs=[HBM] * (2 * ns) + [ANY],
        out_specs=[SEM, SEM] + [HBM] * (2 * ns) + [pl.BlockSpec(memory_space=pltpu.VMEM)],
        out_shape=[pltpu.SemaphoreType.DMA((4 * npc,)), pltpu.SemaphoreType.DMA((4 * npc,))]
        + [pltpu.HBM(a.shape, a.dtype) for a in list(words) + list(fulls)] + [jax.ShapeDtypeStruct((8, 128), F32)],
        input_output_aliases={i: 2 + i for i in range(2 * ns)},
        name=f"gather_start_{tag}", compiler_params=pltpu.CompilerParams(has_side_effects=EFFECT))(
            *_in_hbm(list(words) + list(fulls)), dep)
    return outs[0], outs[1], outs[2:2 + ns], outs[2 + ns:2 + 2 * ns], outs[-1]


def gather_wait(ssem, rsem, words, fulls, specs, after, tag):
    ns = len(specs)

    def body(*refs):
        srcs, bufs, ssem, rsem = refs[:ns], refs[ns:2 * ns], refs[2 * ns], refs[2 * ns + 1]
        for cp in _gather_copies(srcs, bufs, specs, ssem, rsem, False):
            cp.wait_send()
        for cp in _gather_copies(srcs, bufs, specs, ssem, rsem, True):
            cp.wait_recv()

    outs = pl.pallas_call(
        body, in_specs=[HBM] * (2 * ns) + [SEM, SEM, ANY], out_specs=[HBM] * (2 * ns),
        out_shape=[pltpu.HBM(a.shape, a.dtype) for a in list(words) + list(fulls)],
        input_output_aliases={i: i for i in range(2 * ns)},
        name=f"gather_wait_{tag}", compiler_params=pltpu.CompilerParams(has_side_effects=EFFECT))(
            *words, *fulls, ssem, rsem, after)
    return outs[ns:]


def gather_forward(fulls, specs, tag):
    ns, npc = len(specs), _n_pieces(specs)

    def body(*refs):
        bufs = refs[ns:2 * ns]
        ssem, rsem = refs[2 * ns:]
        x, y, c = _coords()
        sibling = (x, y, 1 - c)
        waits = []
        p = 0
        for buf, sp in zip(bufs, specs):
            for s, n in sp.pieces:
                for t, (fx, fy) in enumerate(FLIPS):
                    chip = 4 * (x ^ fx) + 2 * (y ^ fy)
                    here = _rows(buf, sp.off(chip + c, s), n)
                    send = pltpu.make_async_remote_copy(here, here, ssem.at[t, p], rsem.at[t, p],
                                                        device_id=sibling, device_id_type=MESH)
                    send.start()
                    waits.append(send.wait_send)
                    there = _rows(buf, sp.off(chip + 1 - c, s), n)
                    waits.append(pltpu.make_async_remote_copy(here, there, ssem.at[t, p], rsem.at[t, p],
                                                              device_id=sibling, device_id_type=MESH).wait_recv)
                p += 1
        for w in waits:
            w()

    return pl.pallas_call(
        body, in_specs=[ANY] * ns, out_specs=[ANY] * ns,
        out_shape=[jax.ShapeDtypeStruct(f.shape, f.dtype) for f in fulls],
        input_output_aliases={i: i for i in range(ns)},
        scratch_shapes=[pltpu.SemaphoreType.DMA((3, npc)), pltpu.SemaphoreType.DMA((3, npc))],
        name=f"gather_forward_{tag}", compiler_params=pltpu.CompilerParams(has_side_effects=True))(*fulls)


def all_gather(shards, layer, specs, names, tag):
    placed = [pack_place(s, sp, layer, f"{tag}_{n}") for s, sp, n in zip(shards, specs, names)]
    fulls = gather_send([p[0] for p in placed], [p[1] for p in placed], specs, tag)
    return gather_forward(fulls, specs, tag)


def reduce_pair(grads, specs, tag, dep=None):
    ns, npc = len(specs), _n_pieces(specs)
    deps = [] if dep is None else [dep]

    def body(*refs):
        srcs, theirs = refs[:ns], refs[ns + len(deps):2 * ns + len(deps)]
        ssem, rsem = refs[2 * ns + len(deps):]
        x, y, c = _coords()
        sibling = (x, y, 1 - c)
        waits = []
        p = 0
        for src, their, sp in zip(srcs, theirs, specs):
            for s, n in sp.pieces:
                for j in range(4):
                    send = pltpu.make_async_remote_copy(_rows(src, sp.off(2 * j + 1 - c, s), n), _rows(their.at[j], s, n),
                                                        ssem.at[j, p], rsem.at[j, p], device_id=sibling, device_id_type=MESH)
                    send.start()
                    waits.append(send.wait)
                p += 1
        for w in waits:
            w()

    return pl.pallas_call(
        body, in_specs=[ANY] * (ns + len(deps)), out_specs=[ANY] * ns,
        out_shape=[jax.ShapeDtypeStruct((4, sp.layers, sp.rows, sp.cols), F32) for sp in specs],
        scratch_shapes=[pltpu.SemaphoreType.DMA((4, npc)), pltpu.SemaphoreType.DMA((4, npc))],
        name=f"reduce_pair_{tag}", compiler_params=pltpu.CompilerParams(has_side_effects=True))(*grads, *deps)


def pair_sum(g, r1, sp, tag):
    npc = len(sp.pieces)
    fetch_all = 4 * sp.rows * sp.cols * 4 <= (8 << 20)

    def body(g_ref, r_ref, own_ref, words_ref, gbuf, sem):
        l, j = pl.program_id(0), pl.program_id(1)
        x, y, c = _coords()

        def fetch(chip, slot):
            copies = [pltpu.make_async_copy(g_ref.at[l, pl.ds(pl.multiple_of(sp.off(2 * chip + c, s), 8), n), :],
                                            gbuf.at[slot, pl.ds(s, n), :], sem.at[slot, i])
                      for i, (s, n) in enumerate(sp.pieces)]
            for cp in copies:
                cp.start()
            return copies

        if fetch_all:
            @pl.when(j == 0)
            def _():
                for cp in [cp for chip in range(4) for cp in fetch(chip, chip)]:
                    cp.wait()

            mine = gbuf[j]
        else:
            for cp in fetch(j, 0):
                cp.wait()
            mine = gbuf[0]
        p = mine + r_ref[...]
        words_ref[...] = sp.to_words(p)

        @pl.when(j == 2 * x + y)
        def _():
            own_ref[...] = p

    return pl.pallas_call(
        body, grid=(sp.layers, 4),
        in_specs=[ANY, pl.BlockSpec((None, None, sp.rows, sp.cols), lambda l, j: (j, l, 0, 0))],
        out_specs=[pl.BlockSpec((None, sp.rows, sp.cols), lambda l, j: (l, 0, 0)),
                   pl.BlockSpec((None, None, sp.rows, sp.wcols), lambda l, j: (j, l, 0, 0))],
        out_shape=[jax.ShapeDtypeStruct((sp.layers, sp.rows, sp.cols), F32),
                   jax.ShapeDtypeStruct((4, sp.layers, sp.rows, sp.wcols), F32)],
        scratch_shapes=[pltpu.VMEM((4 if fetch_all else 1, sp.rows, sp.cols), F32), pltpu.SemaphoreType.DMA((4, npc))],
        name=f"pair_sum_{tag}", compiler_params=_params(("arbitrary", "arbitrary")))(g, r1)


def _chip_copies(srcs, dsts, ssem, rsem):
    x, y, c = _coords()
    copies = []
    for i, (src, dst) in enumerate(zip(srcs, dsts)):
        for t, (fx, fy) in enumerate(FLIPS):
            tx, ty = x ^ fx, y ^ fy
            copies.append(pltpu.make_async_remote_copy(src.at[2 * tx + ty], dst.at[t], ssem.at[3 * i + t], rsem.at[3 * i + t],
                                                       device_id=(tx, ty, c), device_id_type=MESH))
    return copies


def _slot_shapes(words):
    return [(3,) + w.shape[1:] for w in words]


def reduce_chips(words, specs, tag):
    ns = len(specs)

    def body(*refs):
        copies = _chip_copies(refs[:ns], refs[ns:2 * ns], *refs[2 * ns:])
        for cp in copies:
            cp.start()
        for cp in copies:
            cp.wait()

    return pl.pallas_call(
        body, in_specs=[ANY] * ns, out_specs=[ANY] * ns,
        out_shape=[jax.ShapeDtypeStruct(s, F32) for s in _slot_shapes(words)],
        scratch_shapes=[pltpu.SemaphoreType.DMA((3 * ns,)), pltpu.SemaphoreType.DMA((3 * ns,))],
        name=f"reduce_chips_{tag}", compiler_params=pltpu.CompilerParams(has_side_effects=True))(*words)


def chips_start(words, specs, tag):
    ns = len(specs)
    slots = [lax.empty(s, F32) for s in _slot_shapes(words)]

    def body(*refs):
        ssem, rsem = refs[2 * ns:2 * ns + 2]
        for cp in _chip_copies(refs[:ns], refs[ns:2 * ns], ssem, rsem):
            cp.start()
        refs[-1][...] = jnp.zeros_like(refs[-1])

    outs = pl.pallas_call(
        body, in_specs=[HBM] * (2 * ns),
        out_specs=[SEM, SEM] + [HBM] * (2 * ns) + [pl.BlockSpec(memory_space=pltpu.VMEM)],
        out_shape=[pltpu.SemaphoreType.DMA((3 * ns,)), pltpu.SemaphoreType.DMA((3 * ns,))]
        + [pltpu.HBM(a.shape, a.dtype) for a in list(words) + slots] + [jax.ShapeDtypeStruct((8, 128), F32)],
        input_output_aliases={i: 2 + i for i in range(2 * ns)},
        name=f"chips_start_{tag}", compiler_params=pltpu.CompilerParams(has_side_effects=EFFECT))(
            *_in_hbm(list(words) + slots))
    return outs[0], outs[1], outs[2:2 + ns], outs[2 + ns:2 + 2 * ns], outs[-1]


def chips_wait(ssem, rsem, words, slots, specs, after, tag):
    ns = len(specs)

    def body(*refs):
        for cp in _chip_copies(refs[:ns], refs[ns:2 * ns], refs[2 * ns], refs[2 * ns + 1]):
            cp.wait_send()
            cp.wait_recv()

    outs = pl.pallas_call(
        body, in_specs=[HBM] * (2 * ns) + [SEM, SEM, ANY], out_specs=[HBM] * (2 * ns),
        out_shape=[pltpu.HBM(a.shape, a.dtype) for a in list(words) + list(slots)],
        input_output_aliases={i: i for i in range(2 * ns)},
        name=f"chips_wait_{tag}", compiler_params=pltpu.CompilerParams(has_side_effects=EFFECT))(
            *words, *slots, ssem, rsem, after)
    return outs[ns:]


def sum_chips(own, r2, sp, tag):
    def body(own_ref, r_ref, o_ref):
        o_ref[...] = ((own_ref[...] + sp.from_words(r_ref[0])) + sp.from_words(r_ref[1])) + sp.from_words(r_ref[2])

    blk = pl.BlockSpec((None, sp.rows, sp.cols), lambda l: (l, 0, 0))
    return pl.pallas_call(
        body, grid=(sp.layers,), in_specs=[blk, pl.BlockSpec((3, None, sp.rows, sp.wcols), lambda l: (0, l, 0, 0))],
        out_specs=blk, out_shape=jax.ShapeDtypeStruct((sp.layers, sp.rows, sp.cols), F32),
        name=f"sum_chips_{tag}", compiler_params=_params(("arbitrary",)))(own, r2)


def reduce_scatter_start(grads, specs, names, dep, tag):
    theirs = reduce_pair(grads, specs, tag, dep)
    sums = [pair_sum(g, r1, sp, f"{tag}_{n}") for g, r1, sp, n in zip(grads, theirs, specs, names)]
    ssem, rsem, words, slots, token = chips_start([s[1] for s in sums], specs, tag)
    return (ssem, rsem, words, slots, [s[0] for s in sums]), token


def reduce_scatter_finish(state, after, specs, tag):
    ssem, rsem, words, slots, own = state
    return list(zip(own, chips_wait(ssem, rsem, words, slots, specs, after, tag)))


def reduce_scatter(grads, specs, names, tag):
    theirs = reduce_pair(grads, specs, tag)
    sums = [pair_sum(g, r1, sp, f"{tag}_{n}") for g, r1, sp, n in zip(grads, theirs, specs, names)]
    return list(zip([s[0] for s in sums], reduce_chips([s[1] for s in sums], specs, tag)))


def _adamw_math(w, g, m, v):
    c1 = 1.0 - ADAM_B1 ** ADAM_STEP
    c2 = 1.0 - ADAM_B2 ** ADAM_STEP
    m2 = ADAM_B1 * m + (1.0 - ADAM_B1) * g
    v2 = ADAM_B2 * v + (1.0 - ADAM_B2) * (g * g)
    return -ADAM_LR * ((m2 / c1) / (jnp.sqrt(v2 / c2) + ADAM_EPS) + ADAM_WD * w), m2, v2


def adamw(w, g, m, v, name):
    shape = w.shape
    cols = shape[-1]
    rows = math.prod(shape[:-1])
    tr = rows
    while tr * cols * 4 > (1 << 20) and tr % 16 == 0:
        tr //= 2

    def body(w_ref, g_ref, m_ref, v_ref, d_ref, nm_ref, nv_ref):
        d_ref[...], nm_ref[...], nv_ref[...] = _adamw_math(w_ref[...], g_ref[...], m_ref[...], v_ref[...])

    blk = pl.BlockSpec((tr, cols), lambda i: (i, 0))
    outs = pl.pallas_call(
        body, grid=(rows // tr,), in_specs=[blk] * 4, out_specs=[blk] * 3,
        out_shape=[jax.ShapeDtypeStruct((rows, cols), F32)] * 3,
        name=f"adamw_{name}", compiler_params=_params(("arbitrary",)))(
            *[a.reshape(rows, cols) for a in (w, g, m, v)])
    return [o.reshape(shape) for o in outs]


def adamw_layers(w, sums, m, v, sp, name):
    _, rows, cols = w.shape
    tc = _col_tile(cols)
    twc = tc // 2 if sp.packed else tc

    def body(w_ref, own0, r0, own1, r1, m_ref, v_ref, g_ref, d_ref, nm_ref, nv_ref):
        first = pl.program_id(0) == 0
        own = jnp.where(first, own0[...], own1[...])
        r = [sp.from_words(jnp.where(first, r0[t], r1[t])) for t in range(3)]
        g = ((own + r[0]) + r[1]) + r[2]
        g_ref[...] = g
        d_ref[...], nm_ref[...], nv_ref[...] = _adamw_math(w_ref[...], g, m_ref[...], v_ref[...])

    blk = pl.BlockSpec((None, rows, tc), lambda l, n: (l, 0, n))
    own = pl.BlockSpec((None, rows, tc), lambda l, n: (0, 0, n))
    slots = pl.BlockSpec((3, None, rows, twc), lambda l, n: (0, 0, 0, n))
    return pl.pallas_call(
        body, grid=(L, cols // tc), in_specs=[blk, own, slots, own, slots, blk, blk], out_specs=[blk] * 4,
        out_shape=[jax.ShapeDtypeStruct(w.shape, F32)] * 4,
        name=f"adamw_{name}", compiler_params=_params(("arbitrary", "arbitrary")))(
            w, sums[0][0], sums[0][1], sums[1][0], sums[1][1], m, v)


WEIGHTS = ("pre_norm_g", "w_in", "gm_ln_g", "gm_ln_b", "gm_ws", "gm_bs", "mla_q_norm_g", "mla_w_uq", "mla_kv_norm_g",
           "mla_w_ukv", "lru_conv_w", "lru_conv_b", "lru_w_a", "lru_b_a", "lru_w_x", "lru_b_x", "lru_lambda",
           "w_proj_a", "w_proj_b", "w_proj_c", "w_out", "post_norm_g")
SHARDED = ("w_in", "mla_w_uq", "mla_w_ukv", "lru_conv_w", "w_proj_a", "w_proj_b", "w_proj_c", "w_out")
REPLICATED = tuple(n for n in WEIGHTS if n not in SHARDED)


def _step(x, target, wts, ms, vs):
    t12 = lambda a: jnp.swapaxes(a, 1, 2)
    names = list(WEIGHT_SPECS)
    specs = [WEIGHT_SPECS[n] for n in names]
    tabs = _rope_tables()
    own = {"w_in_t": t12(wts["w_in"]), "wq": t12(wts["mla_w_uq"]), "wkv": t12(wts["mla_w_ukv"]),
           "conv": jnp.pad(t12(wts["lru_conv_w"]), ((0, 0), (0, 0), (0, 124))),
           "w_proj_a": wts["w_proj_a"], "w_proj_b": wts["w_proj_b"], "w_proj_c": wts["w_proj_c"], "w_out": wts["w_out"]}
    first, rest = ["w_in_t"], [n for n in names if n != "w_in_t"]
    sfirst, srest = [WEIGHT_SPECS[n] for n in first], [WEIGHT_SPECS[n] for n in rest]

    w = {n: wts[n] for n in REPLICATED}
    w["kv_g384"] = jnp.concatenate([wts["mla_kv_norm_g"], jnp.ones((L, 128), F32)], axis=1)
    w["wa_dense"] = _block_diag(wts["lru_w_a"])
    w["wx_dense"] = _block_diag(wts["lru_w_x"])

    def layer_weights(ns, words):
        gw = dict(zip(ns, words))
        gw["wq"] = gw["wq"].reshape(HEADS, 192, 384)
        gw["wkv"] = gw["wkv"].reshape(HEADS, 256, 128)
        gw["conv"] = gw["conv"][0, :, :4].T
        return gw

    placed = [{n: pack_place(own[n], WEIGHT_SPECS[n], l, f"w{l}_{n}") for n in names} for l in range(L)]
    words_of = lambda l, ns: [placed[l][n][0] for n in ns]
    bufs_of = lambda l, ns: [placed[l][n][1] for n in ns]
    later = {}

    win0 = gather_forward(gather_send(words_of(0, first), bufs_of(0, first), sfirst, "w0a"), sfirst, "w0a")
    ssem_b, rsem_b, wthru_b, fthru_b, token_b = gather_start(words_of(0, rest), bufs_of(0, rest), srest, win0[0], "w0b")

    def fwd0_mid(ya):
        rest0 = gather_forward(gather_wait(ssem_b, rsem_b, wthru_b, fthru_b, srest, ya, "w0b"), srest, "w0b")
        later["w1"] = gather_start(words_of(1, names), bufs_of(1, names), specs, rest0[0], "w1")
        later["gw0"] = layer_weights(first + rest, list(win0) + list(rest0))
        return later["gw0"], later["w1"][4]

    x1, saved0 = _layer_fwd(x, 0, w, {"w_in_t": win0[0]}, tabs, dep=token_b, mid=fwd0_mid)
    ssem1, rsem1, wthru1, fthru1, _ = later["w1"]
    words1 = gather_forward(gather_wait(ssem1, rsem1, wthru1, fthru1, specs, x1, "w1"), specs, "w1")
    gw0, gw1 = later["gw0"], layer_weights(names, words1)
    x2, saved1 = _layer_fwd(x1, 1, w, gw1, tabs)
    loss, dx2 = loss_head(x2, target)

    dx1, gg1, g1 = _layer_bwd(dx2, 1, w, gw1, tabs, saved1)
    state1, token1 = reduce_scatter_start([gg1[n] for n in names], specs, names, None, "g1")

    def bwd0_mid(gg, last):
        later["s1"] = reduce_scatter_finish(state1, last, specs, "g1")
        later["g0b"], token = reduce_scatter_start([gg[n] for n in rest], srest, rest, later["s1"][0][1], "g0b")
        return token

    dx0, gg0, g0 = _layer_bwd(dx1, 0, w, gw0, tabs, saved0, dep=token1, mid=bwd0_mid)
    s1 = dict(zip(names, later["s1"]))
    s0 = dict(zip(rest, reduce_scatter_finish(later["g0b"], dx0, srest, "g0b")))
    rep_flat = jnp.concatenate([jnp.stack([g0[n], g1[n]]).reshape(-1) for n in REPLICATED])
    rep_flat = jnp.pad(rep_flat, (0, REP_ROWS * NDEV * D - rep_flat.shape[0])).reshape(1, REP_ROWS * NDEV, D)
    s0["w_in_t"], rep_parts = reduce_scatter([gg0["w_in_t"], rep_flat], sfirst + [REP_SPEC], first + ["rep"], "g0a")
    rep_sum = sum_chips(*rep_parts, REP_SPEC, "rep")
    rep_full = all_gather([rep_sum], 0, [REP_SPEC], ["rep"], "rep")[0].reshape(-1)

    out = {}
    for n, key in (("w_in", "w_in_t"), ("mla_w_uq", "wq"), ("mla_w_ukv", "wkv")):
        res = adamw_layers(own[key], [s0[key], s1[key]], t12(ms[n]), t12(vs[n]), WEIGHT_SPECS[key], n)
        out[n] = [t12(r) for r in res]
    for n in ("w_proj_a", "w_proj_b", "w_proj_c", "w_out"):
        out[n] = adamw_layers(wts[n], [s0[n], s1[n]], ms[n], vs[n], WEIGHT_SPECS[n], n)
    conv_sp = WEIGHT_SPECS["conv"]
    g_conv = t12(jnp.concatenate([sum_chips(*s0["conv"], conv_sp, "conv0"), sum_chips(*s1["conv"], conv_sp, "conv1")])[:, :, :4])
    out["lru_conv_w"] = [g_conv] + adamw(wts["lru_conv_w"], g_conv, ms["lru_conv_w"], vs["lru_conv_w"], "lru_conv_w")
    at = 0
    for n in REPLICATED:
        size = math.prod(wts[n].shape)
        g = rep_full[at:at + size].reshape(wts[n].shape)
        out[n] = [g] + adamw(wts[n], g, ms[n], vs[n], n)
        at += size

    loss = lax.psum(loss, ("x", "y", "c"))
    return (loss, dx0[None], *[out[n][k] for k in range(4) for n in WEIGHTS])


def kernel(x, pre_norm_g, w_in, gm_ln_g, gm_ln_b, gm_ws, gm_bs, mla_q_norm_g, mla_w_uq, mla_kv_norm_g, mla_w_ukv, lru_conv_w, lru_conv_b, lru_w_a, lru_b_a, lru_w_x, lru_b_x, lru_lambda, w_proj_a, w_proj_b, w_proj_c, w_out, post_norm_g, loss_target, m_pre_norm_g, m_w_in, m_gm_ln_g, m_gm_ln_b, m_gm_ws, m_gm_bs, m_mla_q_norm_g, m_mla_w_uq, m_mla_kv_norm_g, m_mla_w_ukv, m_lru_conv_w, m_lru_conv_b, m_lru_w_a, m_lru_b_a, m_lru_w_x, m_lru_b_x, m_lru_lambda, m_w_proj_a, m_w_proj_b, m_w_proj_c, m_w_out, m_post_norm_g, v_pre_norm_g, v_w_in, v_gm_ln_g, v_gm_ln_b, v_gm_ws, v_gm_bs, v_mla_q_norm_g, v_mla_w_uq, v_mla_kv_norm_g, v_mla_w_ukv, v_lru_conv_w, v_lru_conv_b, v_lru_w_a, v_lru_b_a, v_lru_w_x, v_lru_b_x, v_lru_lambda, v_w_proj_a, v_w_proj_b, v_w_proj_c, v_w_out, v_post_norm_g):
    wts = dict(zip(WEIGHTS, (pre_norm_g, w_in, gm_ln_g, gm_ln_b, gm_ws, gm_bs, mla_q_norm_g, mla_w_uq, mla_kv_norm_g,
                             mla_w_ukv, lru_conv_w, lru_conv_b, lru_w_a, lru_b_a, lru_w_x, lru_b_x, lru_lambda,
                             w_proj_a, w_proj_b, w_proj_c, w_out, post_norm_g)))
    ms = dict(zip(WEIGHTS, (m_pre_norm_g, m_w_in, m_gm_ln_g, m_gm_ln_b, m_gm_ws, m_gm_bs, m_mla_q_norm_g, m_mla_w_uq,
                            m_mla_kv_norm_g, m_mla_w_ukv, m_lru_conv_w, m_lru_conv_b, m_lru_w_a, m_lru_b_a, m_lru_w_x,
                            m_lru_b_x, m_lru_lambda, m_w_proj_a, m_w_proj_b, m_w_proj_c, m_w_out, m_post_norm_g)))
    vs = dict(zip(WEIGHTS, (v_pre_norm_g, v_w_in, v_gm_ln_g, v_gm_ln_b, v_gm_ws, v_gm_bs, v_mla_q_norm_g, v_mla_w_uq,
                            v_mla_kv_norm_g, v_mla_w_ukv, v_lru_conv_w, v_lru_conv_b, v_lru_w_a, v_lru_b_a, v_lru_w_x,
                            v_lru_b_x, v_lru_lambda, v_w_proj_a, v_w_proj_b, v_w_proj_c, v_w_out, v_post_norm_g)))
    return _step(x[0], loss_target[0], wts, ms, vs)
```

```python
import functools
import math

import jax
import jax.numpy as jnp
from jax import lax
from jax.experimental import pallas as pl
from jax.experimental.pallas import tpu as pltpu

F32 = jnp.float32
BF16 = jnp.bfloat16

T = 2048
D = 1024
L = 2
NDEV = 8
EPS = 1e-6
CHUNK_SHIFT = 6
HEADS = 8
QK = 192
LRU_W = 1280
LRU_TILE = 640
N_IN = 10432
SHARD = N_IN // NDEV
OFF_U, OFF_V, OFF_ZA, OFF_CQ, OFF_CKV, OFF_ZB = 0, 1024, 2048, 3072, 3456, 3840
OFF_XC, OFF_ZC, OFF_GA, OFF_GB, OFF_GC = 5120, 6400, 7680, 8704, 9728
NPAD = 10752
PAD1_AT, PAD1 = 3776, 64
PAD2_AT, PAD2 = 4800, 256
WIN_PIECES = ((0, 888), (888, 280), (1168, 136))
VMEM_LIMIT = 60 * 1024 * 1024

ADAM_LR, ADAM_B1, ADAM_B2, ADAM_EPS, ADAM_WD, ADAM_STEP = 0.001, 0.9, 0.999, 1e-08, 0.01, 10

_NN = (((1,), (0,)), ((), ()))
_NT = (((1,), (1,)), ((), ()))
_TN = (((0,), (0,)), ((), ()))


def _dg(a, b, dims):
    return lax.dot_general(a.astype(BF16), b.astype(BF16), dims, preferred_element_type=F32)


@jax.custom_vjp
def dot_nn(a, b):
    return _dg(a, b, _NN)


def _nn_fwd(a, b):
    return _dg(a, b, _NN), (a, b)


def _nn_bwd(res, g):
    a, b = res
    return _dg(g, b, _NT).astype(a.dtype), _dg(a, g, _TN).astype(b.dtype)


dot_nn.defvjp(_nn_fwd, _nn_bwd)


@jax.custom_vjp
def dot_nt(a, b):
    return _dg(a, b, _NT)


def _nt_fwd(a, b):
    return _dg(a, b, _NT), (a, b)


def _nt_bwd(res, g):
    a, b = res
    return _dg(g, b, _NN).astype(a.dtype), _dg(g, a, _TN).astype(b.dtype)


dot_nt.defvjp(_nt_fwd, _nt_bwd)


def _params(sem=None):
    return pltpu.CompilerParams(dimension_semantics=sem, vmem_limit_bytes=VMEM_LIMIT)


def _sigmoid(x):
    return 1.0 / (1.0 + jnp.exp(-x))


def _silu(x):
    return x * _sigmoid(x)


def _rms(x, g):
    ms = jnp.mean(x * x, axis=-1, keepdims=True)
    return x * lax.rsqrt(ms + EPS) * g


def _acc(ref, val, first):
    @pl.when(first)
    def _():
        ref[...] = val

    @pl.when(jnp.logical_not(first))
    def _():
        ref[...] += val


ANY = pl.BlockSpec(memory_space=pl.ANY)


INPROJ_TN = 512


def inproj_fwd(x, g, wt, l, dep=None):
    tn = INPROJ_TN

    def body(x_ref, g_ref, w_ref, *rest):
        proj_ref, h_ref = rest[-2:]

        @pl.when(pl.program_id(0) == 0)
        def _():
            h_ref[...] = _rms(x_ref[...], g_ref[...]).astype(BF16)

        proj_ref[...] = lax.dot_general(h_ref[...], _unpack(w_ref[...]).astype(BF16), _NT, preferred_element_type=F32)

    deps = [] if dep is None else [dep]
    return pl.pallas_call(
        body, grid=(NPAD // tn,),
        in_specs=[pl.BlockSpec((T, D), lambda j: (0, 0)), pl.BlockSpec((1, D), lambda j: (0, 0)),
                  pl.BlockSpec((None, tn, D // 2), lambda j: (0, j, 0))] + [ANY] * len(deps),
        out_specs=[pl.BlockSpec((T, tn), lambda j: (0, j)), pl.BlockSpec((T, D), lambda j: (0, 0))],
        out_shape=[jax.ShapeDtypeStruct((T, NPAD), F32), jax.ShapeDtypeStruct((T, D), BF16)],
        name=f"inproj_fwd_l{l}", compiler_params=_params(("arbitrary",)))(x, g, wt, *deps)


def inproj_bwd(dproj, h, wt, l, dep=None):
    tn = INPROJ_TN
    deps = [] if dep is None else [dep]

    def body(dp_ref, h_ref, w_ref, *rest):
        dwt_ref, dh_ref = rest[-2:]
        dp = dp_ref[...]
        dwt_ref[...] = lax.dot_general(dp, h_ref[...], _TN, preferred_element_type=F32)
        contrib = lax.dot_general(dp, _unpack(w_ref[...]).astype(BF16), _NN, preferred_element_type=F32)
        _acc(dh_ref, contrib, pl.program_id(0) == 0)

    return pl.pallas_call(
        body, grid=(NPAD // tn,),
        in_specs=[pl.BlockSpec((T, tn), lambda j: (0, j)), pl.BlockSpec((T, D), lambda j: (0, 0)),
                  pl.BlockSpec((None, tn, D // 2), lambda j: (0, j, 0))] + [ANY] * len(deps),
        out_specs=[pl.BlockSpec((None, tn, D), lambda j: (0, j, 0)), pl.BlockSpec((T, D), lambda j: (0, 0))],
        out_shape=[jax.ShapeDtypeStruct((1, NPAD, D), F32), jax.ShapeDtypeStruct((T, D), F32)],
        name=f"inproj_bwd_l{l}", compiler_params=_params(("arbitrary",)))(dproj, h, wt, *deps)


def prenorm_bwd(x, g, dh, dxn, l):
    tm = 256

    def body(x_ref, g_ref, dh_ref, dxn_ref, dx_ref, dg_ref):
        _, vjp = jax.vjp(_rms, x_ref[...], g_ref[...])
        dx, dg = vjp(dh_ref[...])
        dx_ref[...] = dx + dxn_ref[...]
        _acc(dg_ref, dg, pl.program_id(0) == 0)

    tok = pl.BlockSpec((tm, D), lambda i: (i, 0))
    vec = pl.BlockSpec((1, D), lambda i: (0, 0))
    return pl.pallas_call(
        body, grid=(T // tm,), in_specs=[tok, vec, tok, tok], out_specs=[tok, vec],
        out_shape=[jax.ShapeDtypeStruct((T, D), F32), jax.ShapeDtypeStruct((1, D), F32)],
        name=f"prenorm_bwd_l{l}", compiler_params=_params(("arbitrary",)))(x, g, dh, dxn)


def _gmlp_tile(u, v, z, ln_g, ln_b, ws, bs):
    mu = jnp.mean(v, axis=-1, keepdims=True)
    vc = v - mu
    var = jnp.mean(vc * vc, axis=-1, keepdims=True)
    vn = vc * lax.rsqrt(var + EPS) * ln_g + ln_b
    qi = lax.broadcasted_iota(jnp.int32, (128, 128), 0) >> CHUNK_SHIFT
    kj = lax.broadcasted_iota(jnp.int32, (128, 128), 1) >> CHUNK_SHIFT
    mask = kj <= qi
    outs = []
    for g in range(4):
        wm = jnp.where(mask, ws[g], 0.0)
        outs.append(dot_nn(wm, vn[:, 256 * g:256 * (g + 1)]) + bs[g])
    sv = jnp.concatenate(outs, axis=1)
    return u * sv * _silu(z)


def _gmlp_specs():
    blk = lambda c: pl.BlockSpec((128, 1024), lambda n, c=c: (n, c))
    vec = pl.BlockSpec((1, 1024), lambda n: (0, 0))
    return [blk(0), blk(1), blk(2), vec, vec,
            pl.BlockSpec((4, 128, 128), lambda n: (0, 0, 0)), pl.BlockSpec((4, 128, 1), lambda n: (0, 0, 0))]


def gmlp_fwd(proj, ln_g, ln_b, ws, bs, l):
    def body(u_ref, v_ref, z_ref, g_ref, b_ref, ws_ref, bs_ref, y_ref):
        y_ref[...] = _gmlp_tile(u_ref[...], v_ref[...], z_ref[...], g_ref[...], b_ref[...],
                                [ws_ref[g] for g in range(4)], [bs_ref[g] for g in range(4)])

    return pl.pallas_call(
        body, grid=(T // 128,), in_specs=_gmlp_specs(),
        out_specs=pl.BlockSpec((128, 1024), lambda n: (n, 0)),
        out_shape=jax.ShapeDtypeStruct((T, 1024), F32),
        name=f"gmlp_fwd_l{l}", compiler_params=_params(("arbitrary",)))(proj, proj, proj, ln_g, ln_b, ws, bs)


def gmlp_bwd(proj, ln_g, ln_b, ws, bs, dy, l):
    def body(u_ref, v_ref, z_ref, g_ref, b_ref, ws_ref, bs_ref, dy_ref, dseg_ref, dg_ref, db_ref, dws_ref, dbs_ref):
        first = pl.program_id(0) == 0
        _, vjp = jax.vjp(_gmlp_tile, u_ref[...], v_ref[...], z_ref[...], g_ref[...], b_ref[...],
                         [ws_ref[g] for g in range(4)], [bs_ref[g] for g in range(4)])
        du, dv, dz, dg, db, dws, dbs = vjp(dy_ref[...])
        dseg_ref[:, 0:1024] = du.astype(BF16)
        dseg_ref[:, 1024:2048] = dv.astype(BF16)
        dseg_ref[:, 2048:3072] = dz.astype(BF16)
        _acc(dg_ref, dg, first)
        _acc(db_ref, db, first)
        for g in range(4):
            _acc(dws_ref.at[g], dws[g], first)
            _acc(dbs_ref.at[g], dbs[g], first)

    vec = pl.BlockSpec((1, 1024), lambda n: (0, 0))
    return pl.pallas_call(
        body, grid=(T // 128,), in_specs=_gmlp_specs() + [pl.BlockSpec((128, 1024), lambda n: (n, 0))],
        out_specs=[pl.BlockSpec((128, 3072), lambda n: (n, 0)), vec, vec,
                   pl.BlockSpec((4, 128, 128), lambda n: (0, 0, 0)), pl.BlockSpec((4, 128, 1), lambda n: (0, 0, 0))],
        out_shape=[jax.ShapeDtypeStruct((T, 3072), BF16), jax.ShapeDtypeStruct((1, 1024), F32),
                   jax.ShapeDtypeStruct((1, 1024), F32), jax.ShapeDtypeStruct((4, 128, 128), F32),
                   jax.ShapeDtypeStruct((4, 128, 1), F32)],
        name=f"gmlp_bwd_l{l}", compiler_params=_params(("arbitrary",)))(proj, proj, proj, ln_g, ln_b, ws, bs, dy)


QKV_TM = 256


def _qkv_tile(cq, ckvr, qg, kvg, wq, wkv, ctab, stab):
    tm = cq.shape[0]
    cqn = _rms(cq, qg)
    lane = lax.broadcasted_iota(jnp.int32, ckvr.shape, 1)
    iskv = lane < 256
    ms = jnp.sum(jnp.where(iskv, ckvr * ckvr, 0.0), axis=-1, keepdims=True) * (1.0 / 256)
    lm = jnp.where(iskv, ckvr * lax.rsqrt(ms + EPS) * kvg, ckvr)
    r = lax.broadcasted_iota(jnp.int32, (64, 128), 0)
    c = lax.broadcasted_iota(jnp.int32, (64, 128), 1)
    eye = jnp.where(c == r, 1.0, 0.0)
    eye_sw = jnp.where(c == ((r + 32) & 63), 1.0, 0.0)
    z64 = jnp.zeros((64, 256), F32)
    z128 = jnp.zeros((128, 128), F32)
    rk_rope = jnp.concatenate([z64, eye], axis=1)
    rk_sw = jnp.concatenate([jnp.zeros((128, 384), F32), jnp.concatenate([z64, eye_sw], axis=1)], axis=0)
    k_sw = dot_nt(lm, rk_sw) * stab
    qs, ks, vs = [], [], []
    for h in range(HEADS):
        wn, w1, w2 = wq[h]
        wk, wv = wkv[h]
        wq_h = jnp.concatenate([wn, w1, w2], axis=0)
        wq_sw = jnp.concatenate([jnp.zeros((128, 384), F32), w2, w1], axis=0)
        qs.append(dot_nt(cqn, wq_h) * ctab + dot_nt(cqn, wq_sw) * stab)
        rk_h = jnp.concatenate([jnp.concatenate([wk, z128], axis=1), rk_rope], axis=0)
        ks.append(dot_nt(lm, rk_h) * ctab + k_sw)
        vs.append(dot_nt(lm, jnp.concatenate([wv, z128], axis=1)))
    return qs, ks, vs


def _qkv_in_specs():
    tm = QKV_TM
    return [pl.BlockSpec((tm, 384), lambda i: (i, OFF_CQ // 384)), pl.BlockSpec((tm, 384), lambda i: (i, OFF_CKV // 384)),
            pl.BlockSpec((1, 384), lambda i: (0, 0)), pl.BlockSpec((1, 384), lambda i: (0, 0)),
            pl.BlockSpec((HEADS, 192, 384), lambda i: (0, 0, 0)), pl.BlockSpec((HEADS, 256, 128), lambda i: (0, 0, 0)),
            pl.BlockSpec((tm, 192), lambda i: (i, 0)), pl.BlockSpec((tm, 192), lambda i: (i, 0))]


def _qkv_weights(wq_ref, wkv_ref):
    wq = [(wq_ref[h, 0:128, :], wq_ref[h, 128:160, :], wq_ref[h, 160:192, :]) for h in range(HEADS)]
    wkv = [(_unpack(wkv_ref[h, 0:128, :]), _unpack(wkv_ref[h, 128:256, :])) for h in range(HEADS)]
    return wq, wkv


def qkv_fwd(proj, qg, kvg, wq, wkv, ctab, stab, l, dep=None):
    tm = QKV_TM
    deps = [] if dep is None else [dep]

    def body(cq_ref, ckvr_ref, qg_ref, kvg_ref, wq_ref, wkv_ref, c_ref, s_ref, *rest):
        q_ref, k_ref, v_ref = rest[-3:]
        wq_l, wkv_l = _qkv_weights(wq_ref, wkv_ref)
        qs, ks, vs = _qkv_tile(cq_ref[...], ckvr_ref[...], qg_ref[...], kvg_ref[...], wq_l, wkv_l, c_ref[...], s_ref[...])
        for h in range(HEADS):
            q_ref[h] = qs[h]
            k_ref[h] = ks[h]
            v_ref[h] = vs[h]

    return pl.pallas_call(
        body, grid=(T // tm,), in_specs=_qkv_in_specs() + [ANY] * len(deps),
        out_specs=[pl.BlockSpec((HEADS, tm, QK), lambda i: (0, i, 0)), pl.BlockSpec((HEADS, tm, QK), lambda i: (0, i, 0)),
                   pl.BlockSpec((HEADS, tm, 128), lambda i: (0, i, 0))],
        out_shape=[jax.ShapeDtypeStruct((HEADS, T, QK), F32), jax.ShapeDtypeStruct((HEADS, T, QK), F32),
                   jax.ShapeDtypeStruct((HEADS, T, 128), F32)],
        name=f"qkv_fwd_l{l}", compiler_params=_params(("arbitrary",)))(proj, proj, qg, kvg, wq, wkv, ctab, stab, *deps)


def qkv_bwd(proj, qg, kvg, wq, wkv, ctab, stab, dq, dk, dv, l):
    tm = QKV_TM

    def body(cq_ref, ckvr_ref, qg_ref, kvg_ref, wq_ref, wkv_ref, c_ref, s_ref, dq_ref, dk_ref, dv_ref,
             dseg_ref, dqg_ref, dkvg_ref, dwq_ref, dwkv_ref):
        first = pl.program_id(0) == 0
        wq_l, wkv_l = _qkv_weights(wq_ref, wkv_ref)
        c_tab, s_tab = c_ref[...], s_ref[...]
        fn = lambda cq, ckvr, qg_, kvg_, wq_, wkv_: _qkv_tile(cq, ckvr, qg_, kvg_, wq_, wkv_, c_tab, s_tab)
        _, vjp = jax.vjp(fn, cq_ref[...], ckvr_ref[...], qg_ref[...], kvg_ref[...], wq_l, wkv_l)
        cts = ([dq_ref[h] for h in range(HEADS)], [dk_ref[h] for h in range(HEADS)], [dv_ref[h] for h in range(HEADS)])
        dcq, dckvr, dqg, dkvg, dwq, dwkv = vjp(cts)
        dseg_ref[:, 0:384] = dcq.astype(BF16)
        dseg_ref[:, 384:768] = dckvr.astype(BF16)
        _acc(dqg_ref, dqg, first)
        _acc(dkvg_ref, dkvg, first)
        for h in range(HEADS):
            _acc(dwq_ref.at[h, 0:128, :], dwq[h][0], first)
            _acc(dwq_ref.at[h, 128:160, :], dwq[h][1], first)
            _acc(dwq_ref.at[h, 160:192, :], dwq[h][2], first)
            _acc(dwkv_ref.at[h, 0:128, :], dwkv[h][0], first)
            _acc(dwkv_ref.at[h, 128:256, :], dwkv[h][1], first)

    hq = pl.BlockSpec((HEADS, tm, QK), lambda i: (0, i, 0))
    return pl.pallas_call(
        body, grid=(T // tm,),
        in_specs=_qkv_in_specs() + [hq, hq, pl.BlockSpec((HEADS, tm, 128), lambda i: (0, i, 0))],
        out_specs=[pl.BlockSpec((tm, 768), lambda i: (i, 0)), pl.BlockSpec((1, 384), lambda i: (0, 0)),
                   pl.BlockSpec((1, 384), lambda i: (0, 0)), pl.BlockSpec((HEADS, 192, 384), lambda i: (0, 0, 0)),
                   pl.BlockSpec((HEADS, 256, 256), lambda i: (0, 0, 0))],
        out_shape=[jax.ShapeDtypeStruct((T, 768), BF16), jax.ShapeDtypeStruct((1, 384), F32),
                   jax.ShapeDtypeStruct((1, 384), F32), jax.ShapeDtypeStruct((HEADS, 192, 384), F32),
                   jax.ShapeDtypeStruct((HEADS, 256, 256), F32)],
        name=f"qkv_bwd_l{l}", compiler_params=_params(("arbitrary",)))(
            proj, proj, qg, kvg, wq, wkv, ctab, stab, dq, dk, dv)


ATT_TQ_FWD = 256
ATT_TQ_BWD = 512


def _attn_tile(q, kv_past, k, v, zb):
    q = q * (1.0 / math.sqrt(QK))
    s = dot_nt(q, k)
    qc = lax.broadcasted_iota(jnp.int32, s.shape, 0) >> CHUNK_SHIFT
    kc = lax.broadcasted_iota(jnp.int32, s.shape, 1) >> CHUNK_SHIFT
    s = jnp.where(kc <= qc, s, -1e30)
    m = jnp.max(s, axis=-1, keepdims=True)
    if kv_past is not None:
        sp = dot_nt(q, kv_past[0])
        m = jnp.maximum(m, jnp.max(sp, axis=-1, keepdims=True))
    m = lax.stop_gradient(m)
    p = jnp.exp(s - m)
    denom = jnp.sum(p, axis=-1, keepdims=True)
    o = dot_nn(p, v)
    if kv_past is not None:
        pp = jnp.exp(sp - m)
        denom = denom + jnp.sum(pp, axis=-1, keepdims=True)
        o = o + dot_nn(pp, kv_past[1])
    return o * (1.0 / denom) * _silu(zb)


def _attn_operands(k_ref, v_ref, g, tq):
    n = tq * g
    past = (k_ref[0:n, :], v_ref[0:n, :]) if g else None
    return past, k_ref[n:n + tq, :], v_ref[n:n + tq, :]


def _attn_in_specs(tq):
    return [pl.BlockSpec((None, tq, QK), lambda h, i: (h, i, 0)), pl.BlockSpec((None, T, QK), lambda h, i: (h, 0, 0)),
            pl.BlockSpec((None, T, 128), lambda h, i: (h, 0, 0)),
            pl.BlockSpec((tq, 128), lambda h, i: (i, OFF_ZB // 128 + h))]


def attn_fwd(q, k, v, proj, l):
    tq = ATT_TQ_FWD

    def body(q_ref, k_ref, v_ref, z_ref, y_ref):
        for g in range(T // tq):
            @pl.when(pl.program_id(1) == g)
            def _(g=g):
                past, k, v = _attn_operands(k_ref, v_ref, g, tq)
                y_ref[...] = _attn_tile(q_ref[...], past, k, v, z_ref[...])

    return pl.pallas_call(
        body, grid=(HEADS, T // tq), in_specs=_attn_in_specs(tq),
        out_specs=pl.BlockSpec((tq, 128), lambda h, i: (i, h)),
        out_shape=jax.ShapeDtypeStruct((T, 1024), F32),
        name=f"attn_fwd_l{l}", compiler_params=_params(("arbitrary", "arbitrary")))(q, k, v, proj)


def attn_bwd(q, k, v, proj, dy, l):
    tq = ATT_TQ_BWD

    def body(q_ref, k_ref, v_ref, z_ref, dy_ref, dq_ref, dk_ref, dv_ref, dz_ref):
        @pl.when(pl.program_id(1) == 0)
        def _():
            dk_ref[...] = jnp.zeros_like(dk_ref)
            dv_ref[...] = jnp.zeros_like(dv_ref)

        for g in range(T // tq):
            @pl.when(pl.program_id(1) == g)
            def _(g=g):
                n = tq * g
                past, k, v = _attn_operands(k_ref, v_ref, g, tq)
                _, vjp = jax.vjp(_attn_tile, q_ref[...], past, k, v, z_ref[...])
                dq, dpast, dk, dv, dz = vjp(dy_ref[...])
                dq_ref[...] = dq
                dz_ref[...] = dz.astype(BF16)
                dk_ref[n:n + tq, :] += dk
                dv_ref[n:n + tq, :] += dv
                if g:
                    dk_ref[0:n, :] += dpast[0]
                    dv_ref[0:n, :] += dpast[1]

    return pl.pallas_call(
        body, grid=(HEADS, T // tq),
        in_specs=_attn_in_specs(tq) + [pl.BlockSpec((tq, 128), lambda h, i: (i, h))],
        out_specs=[pl.BlockSpec((None, tq, QK), lambda h, i: (h, i, 0)), pl.BlockSpec((None, T, QK), lambda h, i: (h, 0, 0)),
                   pl.BlockSpec((None, T, 128), lambda h, i: (h, 0, 0)), pl.BlockSpec((tq, 128), lambda h, i: (i, h))],
        out_shape=[jax.ShapeDtypeStruct((HEADS, T, QK), F32), jax.ShapeDtypeStruct((HEADS, T, QK), F32),
                   jax.ShapeDtypeStruct((HEADS, T, 128), F32), jax.ShapeDtypeStruct((T, 1024), BF16)],
        name=f"attn_bwd_l{l}", compiler_params=_params(("arbitrary", "arbitrary")))(q, k, v, proj, dy)


LRU_TT = 256


def _lru_gates(xc, wa, wx, ba, bx, lam):
    r = _sigmoid(dot_nn(xc, wa) + ba)
    i = _sigmoid(dot_nn(xc, wx) + bx)
    sp = jnp.maximum(-lam, 0.0) + jnp.log1p(jnp.exp(-jnp.abs(lam)))
    log_a = -8.0 * r * sp
    a = jnp.exp(log_a)
    mult = jnp.sqrt(jnp.maximum(1.0 - jnp.exp(2.0 * log_a), 0.0))
    return a, mult * (i * xc)


def _shift_down(x, s, halo):
    n, c = x.shape
    r = pltpu.roll(x.reshape(n // 8, 8, c), s, 1)
    before = jnp.concatenate([pltpu.roll(halo, s, 0)[None], r[:-1]], axis=0)
    sub = lax.broadcasted_iota(jnp.int32, r.shape, 1)
    return jnp.where(sub >= s, r, before).reshape(n, c)


def _shift_up(x, s, halo):
    n, c = x.shape
    r = pltpu.roll(x.reshape(n // 8, 8, c), 8 - s, 1)
    after = jnp.concatenate([r[1:], pltpu.roll(halo, 8 - s, 0)[None]], axis=0)
    sub = lax.broadcasted_iota(jnp.int32, r.shape, 1)
    return jnp.where(sub < 8 - s, r, after).reshape(n, c)


def _conv(x, halo, w_ref, b):
    return (w_ref[3:4, :] * x + w_ref[2:3, :] * _shift_down(x, 1, halo) + w_ref[1:2, :] * _shift_down(x, 2, halo)
            + w_ref[0:1, :] * _shift_down(x, 3, halo) + b)


def _scan(a, b, reverse, carry):
    n, c = a.shape
    a, b = a.reshape(n // 8, 8, c), b.reshape(n // 8, 8, c)
    sub = lax.broadcasted_iota(jnp.int32, a.shape, 1)
    for d in (1, 2, 4):
        keep = sub < 8 - d if reverse else sub >= d
        shift = 8 - d if reverse else d
        a_sh = jnp.where(keep, pltpu.roll(a, shift, 1), 1.0)
        b_sh = jnp.where(keep, pltpu.roll(b, shift, 1), 0.0)
        b = a * b_sh + b
        a = a * a_sh
    a, b = a.reshape(n, c), b.reshape(n, c)
    groups = [None] * (n // 8)
    for g in (reversed(range(n // 8)) if reverse else range(n // 8)):
        h = a[8 * g:8 * g + 8] * carry + b[8 * g:8 * g + 8]
        groups[g] = h
        carry = h[0:1] if reverse else h[7:8]
    return jnp.concatenate(groups, axis=0), carry


def _lru_param_specs(l):
    ct = LRU_TILE
    vec = pl.BlockSpec((1, ct), lambda n, i: (0, n))
    mat = pl.BlockSpec((None, None, ct, ct), lambda n, i: (l, n, 0, 0))
    return [pl.BlockSpec((4, ct), lambda n, i: (0, n)), vec, mat, mat, vec, vec, vec]


def lru_fwd(proj, conv_w, conv_b, wa, wx, ba, bx, lam, l):
    tt, ct = LRU_TT, LRU_TILE

    def body(x_ref, z_ref, cw_ref, cb_ref, wa_ref, wx_ref, ba_ref, bx_ref, lam_ref, h_ref, y_ref, halo, hcar):
        @pl.when(pl.program_id(1) == 0)
        def _():
            halo[...] = jnp.zeros_like(halo)
            hcar[...] = jnp.zeros_like(hcar)

        x = x_ref[...]
        xc = _conv(x, halo[...], cw_ref, cb_ref[...])
        halo[...] = x[tt - 8:tt]
        a, b = _lru_gates(xc, wa_ref[...], wx_ref[...], ba_ref[...], bx_ref[...], lam_ref[...])
        h, hcar[...] = _scan(a, b, False, hcar[...])
        h_ref[...] = h
        y_ref[...] = h * _silu(z_ref[...])

    seq = pl.BlockSpec((tt, ct), lambda n, i: (i, n))
    return pl.pallas_call(
        body, grid=(LRU_W // ct, T // tt),
        in_specs=[pl.BlockSpec((tt, ct), lambda n, i: (i, OFF_XC // ct + n)),
                  pl.BlockSpec((tt, ct), lambda n, i: (i, OFF_ZC // ct + n))] + _lru_param_specs(l),
        out_specs=[seq, seq],
        out_shape=[jax.ShapeDtypeStruct((T, LRU_W), F32), jax.ShapeDtypeStruct((T, LRU_W), F32)],
        scratch_shapes=[pltpu.VMEM((8, ct), F32), pltpu.VMEM((1, ct), F32)],
        name=f"lru_fwd_l{l}", compiler_params=_params(("arbitrary", "arbitrary")))(
            proj, proj, conv_w, conv_b, wa, wx, ba, bx, lam)


def lru_bwd(proj, hseq, dy, conv_w, conv_b, wa, wx, ba, bx, lam, l):
    tt, ct = LRU_TT, LRU_TILE
    nt = T // tt
    rev = lambda i: nt - 1 - i
    prev8 = lambda i: jnp.maximum(rev(i) * (tt // 8) - 1, 0)

    def body(x_ref, xh_ref, z_ref, h_ref, hh_ref, dy_ref, cw_ref, cb_ref, wa_ref, wx_ref, ba_ref, bx_ref, lam_ref,
             dx_ref, dz_ref, dcw_ref, dcb_ref, dwa_ref, dwx_ref, dba_ref, dbx_ref, dlam_ref, gcar, dhalo):
        i = pl.program_id(1)
        first = i == 0

        @pl.when(first)
        def _():
            gcar[...] = jnp.zeros_like(gcar)
            dhalo[...] = jnp.zeros_like(dhalo)

        at_start = rev(i) == 0
        x = x_ref[...]
        xhalo = jnp.where(at_start, 0.0, xh_ref[...])
        sh = [x, _shift_down(x, 1, xhalo), _shift_down(x, 2, xhalo), _shift_down(x, 3, xhalo)]
        xc = (cw_ref[3:4, :] * sh[0] + cw_ref[2:3, :] * sh[1] + cw_ref[1:2, :] * sh[2] + cw_ref[0:1, :] * sh[3]
              + cb_ref[...])
        (a, b), vjp = jax.vjp(_lru_gates, xc, wa_ref[...], wx_ref[...], ba_ref[...], bx_ref[...], lam_ref[...])
        hs = h_ref[...]
        hprev = _shift_down(hs, 1, jnp.where(at_start, 0.0, hh_ref[...]))
        z = z_ref[...]
        sg = _sigmoid(z)
        dy = dy_ref[...]
        dz_ref[...] = (dy * hs * (sg * (1.0 + z * (1.0 - sg)))).astype(BF16)
        dh = dy * (z * sg)
        a_next = _shift_up(a, 1, jnp.ones((8, ct), F32))
        g, _ = _scan(a_next, dh, True, gcar[...])
        dxc, dwa, dwx, dba, dbx, dlam = vjp((g * hprev, g))
        dx = (cw_ref[3:4, :] * dxc + cw_ref[2:3, :] * _shift_up(dxc, 1, dhalo[...])
              + cw_ref[1:2, :] * _shift_up(dxc, 2, dhalo[...]) + cw_ref[0:1, :] * _shift_up(dxc, 3, dhalo[...]))
        dx_ref[...] = dx.astype(BF16)
        dhalo[...] = dxc[0:8]
        ag = a * g
        gcar[...] = ag[0:1]
        dcw = jnp.concatenate([jnp.sum(dxc * sh[3 - j], axis=0, keepdims=True) for j in range(4)], axis=0)
        _acc(dcw_ref, dcw, first)
        _acc(dcb_ref, jnp.sum(dxc, axis=0, keepdims=True), first)
        _acc(dwa_ref, dwa, first)
        _acc(dwx_ref, dwx, first)
        _acc(dba_ref, dba, first)
        _acc(dbx_ref, dbx, first)
        _acc(dlam_ref, dlam, first)

    xcol = OFF_XC // ct
    zcol = OFF_ZC // ct
    vec = pl.BlockSpec((1, ct), lambda n, i: (0, n))
    mat = pl.BlockSpec((None, ct, ct), lambda n, i: (n, 0, 0))
    seq = pl.BlockSpec((tt, ct), lambda n, i: (rev(i), n))
    return pl.pallas_call(
        body, grid=(LRU_W // ct, nt),
        in_specs=[pl.BlockSpec((tt, ct), lambda n, i: (rev(i), xcol + n)),
                  pl.BlockSpec((8, ct), lambda n, i: (prev8(i), xcol + n)),
                  pl.BlockSpec((tt, ct), lambda n, i: (rev(i), zcol + n)),
                  seq, pl.BlockSpec((8, ct), lambda n, i: (prev8(i), n)), seq] + _lru_param_specs(l),
        out_specs=[seq, seq, pl.BlockSpec((4, ct), lambda n, i: (0, n)), vec, mat, mat, vec, vec, vec],
        out_shape=[jax.ShapeDtypeStruct((T, LRU_W), BF16), jax.ShapeDtypeStruct((T, LRU_W), BF16),
                   jax.ShapeDtypeStruct((4, LRU_W), F32), jax.ShapeDtypeStruct((1, LRU_W), F32),
                   jax.ShapeDtypeStruct((2, ct, ct), F32), jax.ShapeDtypeStruct((2, ct, ct), F32),
                   jax.ShapeDtypeStruct((1, LRU_W), F32), jax.ShapeDtypeStruct((1, LRU_W), F32),
                   jax.ShapeDtypeStruct((1, LRU_W), F32)],
        scratch_shapes=[pltpu.VMEM((1, ct), F32), pltpu.VMEM((8, ct), F32)],
        name=f"lru_bwd_l{l}", compiler_params=_params(("arbitrary", "arbitrary")))(
            proj, proj, proj, hseq, hseq, dy, conv_w, conv_b, wa, wx, ba, bx, lam)


def proj_bwd(y, dp, w, l, tag, dep=None):
    tm = 512
    k = y.shape[1]
    deps = [] if dep is None else [dep]

    def body(y_ref, dp_ref, w_ref, *rest):
        dy_ref, dw_ref = rest[-2:]
        dp = dp_ref[...]
        dy_ref[...] = _dg(dp, _unpack(w_ref[...]), _NT)
        _acc(dw_ref, _dg(y_ref[...], dp, _TN), pl.program_id(0) == 0)

    return pl.pallas_call(
        body, grid=(T // tm,),
        in_specs=[pl.BlockSpec((tm, k), lambda i: (i, 0)), pl.BlockSpec((tm, D), lambda i: (i, 0)),
                  pl.BlockSpec((None, k, D // 2), lambda i: (0, 0, 0))] + [ANY] * len(deps),
        out_specs=[pl.BlockSpec((tm, k), lambda i: (i, 0)), pl.BlockSpec((None, k, D), lambda i: (0, 0, 0))],
        out_shape=[jax.ShapeDtypeStruct((T, k), F32), jax.ShapeDtypeStruct((1, k, D), F32)],
        name=f"proj_{tag}_bwd_l{l}", compiler_params=_params(("arbitrary",)))(y, dp, w, *deps)


OUT_TM = 256


def _out_tile(pa, pb, pc, ga, gb, gc, wout, post_g):
    merged = _sigmoid(ga) * pa + _sigmoid(gb) * pb + _sigmoid(gc) * pc
    return _rms(dot_nn(merged, wout), post_g)


def _out_in_specs():
    tm = OUT_TM
    tok = pl.BlockSpec((tm, D), lambda i: (i, 0))
    gate = lambda off: pl.BlockSpec((tm, 512), lambda i, off=off: (i, off // 512))
    return [tok, tok, tok, gate(OFF_GA), gate(OFF_GA + 512), gate(OFF_GB), gate(OFF_GB + 512), gate(OFF_GC),
            gate(OFF_GC + 512), pl.BlockSpec((None, D, D // 2), lambda i: (0, 0, 0)), pl.BlockSpec((1, D), lambda i: (0, 0))]


def _gates(refs):
    return [jnp.concatenate([refs[2 * j][...], refs[2 * j + 1][...]], axis=1) for j in range(3)]


def out_fwd(x, ya, yb, yc, proj, wpa, wpb, wpc, wout, post_g, l):
    tm = OUT_TM

    def body(ya_ref, yb_ref, yc_ref, g0, g1, g2, g3, g4, g5, wo_ref, pg_ref, x_ref, wa_ref, wb_ref, wc_ref,
             o_ref, pa_ref, pb_ref, pc_ref, wa, wb, wc, wo):
        @pl.when(pl.program_id(0) == 0)
        def _():
            for dst, src in ((wa, wa_ref), (wb, wb_ref), (wc, wc_ref), (wo, wo_ref)):
                dst[...] = _unpack(src[...]).astype(BF16)

        pa = _dg(ya_ref[...], wa[...], _NN)
        pb = _dg(yb_ref[...], wb[...], _NN)
        pc = _dg(yc_ref[...], wc[...], _NN)
        ga, gb, gc = _gates([g0, g1, g2, g3, g4, g5])
        o_ref[...] = x_ref[...] + _out_tile(pa, pb, pc, ga, gb, gc, wo[...], pg_ref[...])
        pa_ref[...] = pa.astype(BF16)
        pb_ref[...] = pb.astype(BF16)
        pc_ref[...] = pc.astype(BF16)

    tok = pl.BlockSpec((tm, D), lambda i: (i, 0))
    words = lambda k: pl.BlockSpec((None, k, D // 2), lambda i: (0, 0, 0))
    specs = _out_in_specs()
    specs[2] = pl.BlockSpec((tm, LRU_W), lambda i: (i, 0))
    return pl.pallas_call(
        body, grid=(T // tm,), in_specs=specs + [tok, words(D), words(D), words(LRU_W)], out_specs=[tok] * 4,
        out_shape=[jax.ShapeDtypeStruct((T, D), F32)] + [jax.ShapeDtypeStruct((T, D), BF16)] * 3,
        scratch_shapes=[pltpu.VMEM((D, D), BF16), pltpu.VMEM((D, D), BF16), pltpu.VMEM((LRU_W, D), BF16),
                        pltpu.VMEM((D, D), BF16)],
        name=f"out_fwd_l{l}", compiler_params=_params(("arbitrary",)))(
            ya, yb, yc, proj, proj, proj, proj, proj, proj, wout, post_g, x, wpa, wpb, wpc)


def out_bwd(pa, pb, pc, proj, wout, post_g, dxn, l, dep=None):
    tm = OUT_TM

    def body(pa_ref, pb_ref, pc_ref, g0, g1, g2, g3, g4, g5, w_ref, pg_ref, dxn_ref, *rest):
        dpa_ref, dpb_ref, dpc_ref, dg_ref, dw_ref, dpg_ref = rest[-6:]
        first = pl.program_id(0) == 0
        ga, gb, gc = _gates([g0, g1, g2, g3, g4, g5])
        _, vjp = jax.vjp(_out_tile, pa_ref[...], pb_ref[...], pc_ref[...], ga, gb, gc, _unpack(w_ref[...]), pg_ref[...])
        dpa, dpb, dpc, dga, dgb, dgc, dw, dpg = vjp(dxn_ref[...])
        dpa_ref[...] = dpa.astype(BF16)
        dpb_ref[...] = dpb.astype(BF16)
        dpc_ref[...] = dpc.astype(BF16)
        dg_ref[:, 0:1024] = dga.astype(BF16)
        dg_ref[:, 1024:2048] = dgb.astype(BF16)
        dg_ref[:, 2048:3072] = dgc.astype(BF16)
        _acc(dw_ref, dw, first)
        _acc(dpg_ref, dpg, first)

    tok = pl.BlockSpec((tm, D), lambda i: (i, 0))
    deps = [] if dep is None else [dep]
    return pl.pallas_call(
        body, grid=(T // tm,), in_specs=_out_in_specs() + [tok] + [ANY] * len(deps),
        out_specs=[tok, tok, tok, pl.BlockSpec((tm, 3072), lambda i: (i, 0)),
                   pl.BlockSpec((None, D, D), lambda i: (0, 0, 0)), pl.BlockSpec((1, D), lambda i: (0, 0))],
        out_shape=[jax.ShapeDtypeStruct((T, D), BF16)] * 3 + [jax.ShapeDtypeStruct((T, 3072), BF16),
                                                            jax.ShapeDtypeStruct((1, D, D), F32), jax.ShapeDtypeStruct((1, D), F32)],
        name=f"out_bwd_l{l}", compiler_params=_params(("arbitrary",)))(
            pa, pb, pc, proj, proj, proj, proj, proj, proj, wout, post_g, dxn, *deps)


def loss_head(y, target):
    tm = 256

    def body(y_ref, t_ref, loss_ref, dy_ref):
        e = y_ref[...] - t_ref[...]
        dy_ref[...] = e * (1.0 / D)
        val = 0.5 * jnp.sum(jnp.mean(e * e, axis=-1, keepdims=True), axis=0, keepdims=True)
        _acc(loss_ref, jnp.broadcast_to(val, (8, 128)), pl.program_id(0) == 0)

    tok = pl.BlockSpec((tm, D), lambda i: (i, 0))
    total, dy = pl.pallas_call(
        body, grid=(T // tm,), in_specs=[tok, tok],
        out_specs=[pl.BlockSpec((8, 128), lambda i: (0, 0)), tok],
        out_shape=[jax.ShapeDtypeStruct((8, 128), F32), jax.ShapeDtypeStruct((T, D), F32)],
        name="loss_head", compiler_params=_params(("arbitrary",)))(y, target)
    return total[0, 0], dy


def _rope_tables():
    pos = jnp.arange(T, dtype=F32)
    inv_freq = 10000.0 ** (-jnp.arange(0, 64, 2, dtype=F32) / 64)
    ang = pos[:, None] * inv_freq[None, :]
    cos, sin = jnp.cos(ang), jnp.sin(ang)
    ctab = jnp.concatenate([jnp.ones((T, 128), F32), cos, cos], axis=1)
    stab = jnp.concatenate([jnp.zeros((T, 128), F32), -sin, sin], axis=1)
    return ctab, stab


def _block_diag(w):
    w5 = w.reshape(L, 2, 8, 80, 80)
    eye = jnp.eye(8, dtype=w.dtype)
    return jnp.einsum("lnbij,bc->lnbicj", w5, eye).reshape(L, 2, LRU_TILE, LRU_TILE)


def _block_diag_t(dw):
    dw5 = dw.reshape(2, 8, 80, 8, 80)
    return jnp.einsum("nbicj,bc->nbij", dw5, jnp.eye(8, dtype=dw.dtype)).reshape(16, 80, 80)


def _layer_fwd(x, l, w, gw, tabs, dep=None, mid=None):
    row = lambda a: a[l][None]
    proj, h = inproj_fwd(x, row(w["pre_norm_g"]), gw["w_in_t"], l, dep)
    ya = gmlp_fwd(proj, row(w["gm_ln_g"]), row(w["gm_ln_b"]), w["gm_ws"][l], w["gm_bs"][l][..., None], l)
    dep2 = None
    if mid is not None:
        gw, dep2 = mid(ya)
    q, k, v = qkv_fwd(proj, row(w["mla_q_norm_g"]), row(w["kv_g384"]), gw["wq"], gw["wkv"], tabs[0], tabs[1], l, dep2)
    yb = attn_fwd(q, k, v, proj, l)
    hseq, yc = lru_fwd(proj, gw["conv"], row(w["lru_conv_b"]), w["wa_dense"], w["wx_dense"],
                       row(w["lru_b_a"]), row(w["lru_b_x"]), row(w["lru_lambda"]), l)
    xn, pa, pb, pc = out_fwd(x, ya, yb, yc, proj, gw["w_proj_a"], gw["w_proj_b"], gw["w_proj_c"], gw["w_out"],
                             row(w["post_norm_g"]), l)
    return xn, (x, proj, h, ya, q, k, v, yb, hseq, yc, pa, pb, pc)


def _layer_bwd(dxn, l, w, gw, tabs, saved, dep=None, early=None, mid=None):
    x, proj, h, ya, q, k, v, yb, hseq, yc, pa, pb, pc = saved
    row = lambda a: a[l][None]
    g, gg = {}, {}
    dpa, dpb, dpc, dgates, gg["w_out"], dpost = out_bwd(pa, pb, pc, proj, gw["w_out"], row(w["post_norm_g"]), dxn, l, dep)
    g["post_norm_g"] = dpost[0]
    dep1 = early(dgates) if early is not None else None
    dya, gg["w_proj_a"] = proj_bwd(ya, dpa, gw["w_proj_a"], l, "a", dep1)
    dyb, gg["w_proj_b"] = proj_bwd(yb, dpb, gw["w_proj_b"], l, "b")
    dyc, gg["w_proj_c"] = proj_bwd(yc, dpc, gw["w_proj_c"], l, "c")
    dseg_a, dln_g, dln_b, g["gm_ws"], dbs = gmlp_bwd(proj, row(w["gm_ln_g"]), row(w["gm_ln_b"]), w["gm_ws"][l],
                                                    w["gm_bs"][l][..., None], dya, l)
    g["gm_ln_g"], g["gm_ln_b"], g["gm_bs"] = dln_g[0], dln_b[0], dbs[..., 0]
    dq, dk, dv, dzb = attn_bwd(q, k, v, proj, dyb, l)
    dseg_q, dqg, dkvg, dwq, dwkv = qkv_bwd(proj, row(w["mla_q_norm_g"]), row(w["kv_g384"]), gw["wq"], gw["wkv"],
                                           tabs[0], tabs[1], dq, dk, dv, l)
    gg["wq"], gg["wkv"] = dwq.reshape(1, 1536, 384), dwkv.reshape(1, 2048, 256)
    g["mla_q_norm_g"], g["mla_kv_norm_g"] = dqg[0], dkvg[0, :256]
    dxc, dzc, dcw, dcb, dwa, dwx, dba, dbx, dlam = lru_bwd(
        proj, hseq, dyc, gw["conv"], row(w["lru_conv_b"]), w["wa_dense"], w["wx_dense"],
        row(w["lru_b_a"]), row(w["lru_b_x"]), row(w["lru_lambda"]), l)
    gg["conv"] = jnp.pad(dcw.T, ((0, 0), (0, 124)))[None]
    g["lru_conv_b"], g["lru_b_a"], g["lru_b_x"], g["lru_lambda"] = dcb[0], dba[0], dbx[0], dlam[0]
    g["lru_w_a"], g["lru_w_x"] = _block_diag_t(dwa), _block_diag_t(dwx)
    dproj = jnp.concatenate([dseg_a, dseg_q, dzb, jnp.zeros((T, PAD2), dzb.dtype), dxc, dzc, dgates], axis=1)
    dep2 = mid(gg, dproj) if mid is not None else None
    gg["w_in_t"], dh = inproj_bwd(dproj, h, gw["w_in_t"], l, dep2)
    dx, dpre = prenorm_bwd(x, row(w["pre_norm_g"]), dh, dxn, l)
    g["pre_norm_g"] = dpre[0]
    return dx, gg, g


MESH = pl.DeviceIdType.MESH
HBM = pl.BlockSpec(memory_space=pltpu.HBM)
SEM = pl.BlockSpec(memory_space=pltpu.SEMAPHORE)
EFFECT = pltpu.SideEffectType.DATAFLOW_SIDE_EFFECTING
FLIPS = ((1, 0), (0, 1), (1, 1))


def _win_off(k, s):
    g = SHARD * k + s
    return g + jnp.where(g >= PAD1_AT, PAD1, 0) + jnp.where(g >= PAD2_AT, PAD2, 0)


def _plain_off(rows):
    return lambda k, s: rows * k + s


class Spec:
    def __init__(self, rows, cols, full_rows, pieces=None, off=None, layers=1, packed=None):
        self.rows, self.cols, self.full_rows, self.layers = rows, cols, full_rows, layers
        self.pieces = pieces or ((0, rows),)
        self.off = off or _plain_off(rows)
        self.packed = cols % 256 == 0 if packed is None else packed
        self.wcols = cols // 2 if self.packed else cols

    def to_words(self, a):
        return _pack(a) if self.packed else a

    def from_words(self, p):
        return _unpack(p) if self.packed else p


def _pack(a):
    bits = lambda v: lax.bitcast_convert_type(v.astype(jnp.bfloat16).astype(F32), jnp.uint32)
    words = [(bits(a[:, g:g + 128]) >> 16) | (bits(a[:, g + 128:g + 256]) & jnp.uint32(0xFFFF0000))
             for g in range(0, a.shape[-1], 256)]
    return lax.bitcast_convert_type(jnp.concatenate(words, axis=-1) if len(words) > 1 else words[0], F32)


def _unpack(p):
    w = lax.bitcast_convert_type(p, jnp.uint32)
    lo = lax.bitcast_convert_type(w << 16, F32)
    hi = lax.bitcast_convert_type(w & jnp.uint32(0xFFFF0000), F32)
    return jnp.concatenate([h[:, g:g + 128] for g in range(0, p.shape[-1], 128) for h in (lo, hi)], axis=-1)


WEIGHT_SPECS = {
    "w_in_t": Spec(SHARD, D, NPAD, WIN_PIECES, _win_off),
    "wq": Spec(192, 384, 1536),
    "wkv": Spec(256, 256, 2048),
    "conv": Spec(160, 128, 1280),
    "w_proj_a": Spec(128, D, 1024),
    "w_proj_b": Spec(128, D, 1024),
    "w_proj_c": Spec(160, D, 1280),
    "w_out": Spec(128, D, 1024),
}
REP_ROWS = 72
REP_SPEC = Spec(REP_ROWS, D, REP_ROWS * NDEV, packed=False)


def _coords():
    return lax.axis_index("x"), lax.axis_index("y"), lax.axis_index("c")


def _rows(ref, start, n):
    if not isinstance(start, int):
        start = pl.multiple_of(start, 8)
    return ref.at[:, pl.ds(start, n), :]


def _col_tile(cols):
    return 256 if cols % 256 == 0 else cols


def _n_pieces(specs):
    return sum(len(sp.pieces) for sp in specs)


def pack_place(shard, sp, layer, tag, dep=None):
    gaps = ((PAD1_AT, PAD1), (PAD2_AT + PAD1, PAD2)) if sp.off is _win_off else ()
    npc = len(sp.pieces)
    deps = [] if dep is None else [dep]

    def body(s_ref, *rest):
        words_ref, full_ref, buf, zbuf, sem = rest[-5:]
        l = 0
        x, y, c = _coords()
        me = 4 * x + 2 * y + c
        words = sp.to_words(s_ref[...])
        words_ref[...] = words
        buf[...] = words
        copies = [pltpu.make_async_copy(buf.at[pl.ds(s, n), :],
                                        full_ref.at[l, pl.ds(pl.multiple_of(sp.off(me, s), 8), n), :], sem.at[i])
                  for i, (s, n) in enumerate(sp.pieces)]
        if gaps:
            zbuf[...] = jnp.zeros_like(zbuf)
            copies += [pltpu.make_async_copy(zbuf.at[pl.ds(0, n), :], full_ref.at[l, pl.ds(at, n), :], sem.at[npc + i])
                       for i, (at, n) in enumerate(gaps)]
        for cp in copies:
            cp.start()
        for cp in copies:
            cp.wait()

    return pl.pallas_call(
        body, grid=(1,), in_specs=[pl.BlockSpec((None, sp.rows, sp.cols), lambda i: (layer, 0, 0))] + [ANY] * len(deps),
        out_specs=[pl.BlockSpec((None, sp.rows, sp.wcols), lambda i: (0, 0, 0)), ANY],
        out_shape=[jax.ShapeDtypeStruct((sp.layers, sp.rows, sp.wcols), F32),
                   jax.ShapeDtypeStruct((sp.layers, sp.full_rows, sp.wcols), F32)],
        scratch_shapes=[pltpu.VMEM((sp.rows, sp.wcols), F32), pltpu.VMEM((PAD2 if gaps else 8, sp.wcols), F32),
                        pltpu.SemaphoreType.DMA((npc + len(gaps),))],
        name=f"pack_place_{tag}", compiler_params=_params(("arbitrary",)))(shard, *deps)


def _gather_copies(srcs, bufs, specs, ssem, rsem, landing):
    x, y, c = _coords()
    me = 4 * x + 2 * y + c
    targets = [(x, y, 1 - c)] + [(x ^ fx, y ^ fy, c) for fx, fy in FLIPS]
    copies = []
    p = 0
    for src, buf, sp in zip(srcs, bufs, specs):
        for s, n in sp.pieces:
            for t, (tx, ty, tc) in enumerate(targets):
                owner = 4 * tx + 2 * ty + tc if landing else me
                copies.append(pltpu.make_async_remote_copy(_rows(src, s, n), _rows(buf, sp.off(owner, s), n),
                                                           ssem.at[4 * p + t], rsem.at[4 * p + t],
                                                           device_id=(tx, ty, tc), device_id_type=MESH))
            p += 1
    return copies


def gather_send(words, fulls, specs, tag):
    ns, npc = len(specs), _n_pieces(specs)

    def body(*refs):
        srcs, bufs, sems = refs[:ns], refs[2 * ns:3 * ns], refs[3 * ns:]
        for cp in _gather_copies(srcs, bufs, specs, *sems, False):
            cp.start()
        for cp in _gather_copies(srcs, bufs, specs, *sems, False):
            cp.wait_send()
        for cp in _gather_copies(srcs, bufs, specs, *sems, True):
            cp.wait_recv()

    return pl.pallas_call(
        body, in_specs=[ANY] * (2 * ns), out_specs=[ANY] * ns,
        out_shape=[jax.ShapeDtypeStruct(f.shape, f.dtype) for f in fulls],
        input_output_aliases={ns + i: i for i in range(ns)},
        scratch_shapes=[pltpu.SemaphoreType.DMA((4 * npc,)), pltpu.SemaphoreType.DMA((4 * npc,))],
        name=f"gather_send_{tag}", compiler_params=pltpu.CompilerParams(has_side_effects=True))(*words, *fulls)


def _in_hbm(arrays):
    return [pltpu.with_memory_space_constraint(a, pltpu.HBM) for a in arrays]


def gather_start(words, fulls, specs, dep, tag):
    ns, npc = len(specs), _n_pieces(specs)
    deps = [] if dep is None else [dep]

    def body(*refs):
        ssem, rsem = refs[2 * ns + len(deps):2 * ns + len(deps) + 2]
        for cp in _gather_copies(refs[:ns], refs[ns:2 * ns], specs, ssem, rsem, False):
            cp.start()
        refs[-1][...] = jnp.zeros_like(refs[-1])

    outs = pl.pallas_call(
        body, in_specs=[HBM] * (2 * ns) + [ANY] * len(deps),
        out_specs=[SEM, SEM] + [HBM] * (2 * ns) + [pl.BlockSpec(memory_space=pltpu.VMEM)],
        out_shape=[pltpu.SemaphoreType.DMA((4 * npc,)), pltpu.SemaphoreType.DMA((4 * npc,))]
        + [pltpu.HBM(a.shape, a.dtype) for a in list(words) + list(fulls)] + [jax.ShapeDtypeStruct((8, 128), F32)],
        input_output_aliases={i: 2 + i for i in range(2 * ns)},
        name=f"gather_start_{tag}", compiler_params=pltpu.CompilerParams(has_side_effects=EFFECT))(
            *_in_hbm(list(words) + list(fulls)), *deps)
    return outs[0], outs[1], outs[2:2 + ns], outs[2 + ns:2 + 2 * ns], outs[-1]


def gather_wait(ssem, rsem, words, fulls, specs, after, tag):
    ns = len(specs)

    def body(*refs):
        srcs, bufs, ssem, rsem = refs[:ns], refs[ns:2 * ns], refs[2 * ns], refs[2 * ns + 1]
        for cp in _gather_copies(srcs, bufs, specs, ssem, rsem, False):
            cp.wait_send()
        for cp in _gather_copies(srcs, bufs, specs, ssem, rsem, True):
            cp.wait_recv()

    outs = pl.pallas_call(
        body, in_specs=[HBM] * (2 * ns) + [SEM, SEM, ANY], out_specs=[HBM] * (2 * ns),
        out_shape=[pltpu.HBM(a.shape, a.dtype) for a in list(words) + list(fulls)],
        input_output_aliases={i: i for i in range(2 * ns)},
        name=f"gather_wait_{tag}", compiler_params=pltpu.CompilerParams(has_side_effects=EFFECT))(
            *words, *fulls, ssem, rsem, after)
    return outs[ns:]


def gather_forward(fulls, specs, tag):
    ns, npc = len(specs), _n_pieces(specs)

    def body(*refs):
        bufs = refs[ns:2 * ns]
        ssem, rsem = refs[2 * ns:]
        x, y, c = _coords()
        sibling = (x, y, 1 - c)
        waits = []
        p = 0
        for buf, sp in zip(bufs, specs):
            for s, n in sp.pieces:
                for t, (fx, fy) in enumerate(FLIPS):
                    chip = 4 * (x ^ fx) + 2 * (y ^ fy)
                    here = _rows(buf, sp.off(chip + c, s), n)
                    send = pltpu.make_async_remote_copy(here, here, ssem.at[t, p], rsem.at[t, p],
                                                        device_id=sibling, device_id_type=MESH)
                    send.start()
                    waits.append(send.wait_send)
                    there = _rows(buf, sp.off(chip + 1 - c, s), n)
                    waits.append(pltpu.make_async_remote_copy(here, there, ssem.at[t, p], rsem.at[t, p],
                                                              device_id=sibling, device_id_type=MESH).wait_recv)
                p += 1
        for w in waits:
            w()

    return pl.pallas_call(
        body, in_specs=[ANY] * ns, out_specs=[ANY] * ns,
        out_shape=[jax.ShapeDtypeStruct(f.shape, f.dtype) for f in fulls],
        input_output_aliases={i: i for i in range(ns)},
        scratch_shapes=[pltpu.SemaphoreType.DMA((3, npc)), pltpu.SemaphoreType.DMA((3, npc))],
        name=f"gather_forward_{tag}", compiler_params=pltpu.CompilerParams(has_side_effects=True))(*fulls)


def all_gather(shards, layer, specs, names, tag):
    placed = [pack_place(s, sp, layer, f"{tag}_{n}") for s, sp, n in zip(shards, specs, names)]
    fulls = gather_send([p[0] for p in placed], [p[1] for p in placed], specs, tag)
    return gather_forward(fulls, specs, tag)


def _pair_copies(srcs, theirs, specs, ssem, rsem):
    x, y, c = _coords()
    copies = []
    p = 0
    for src, their, sp in zip(srcs, theirs, specs):
        for s, n in sp.pieces:
            for j in range(4):
                copies.append(pltpu.make_async_remote_copy(_rows(src, sp.off(2 * j + 1 - c, s), n), _rows(their.at[j], s, n),
                                                           ssem.at[4 * p + j], rsem.at[4 * p + j],
                                                           device_id=(x, y, 1 - c), device_id_type=MESH))
            p += 1
    return copies


def _pair_shapes(specs):
    return [(4, sp.layers, sp.rows, sp.cols) for sp in specs]


def reduce_pair(grads, specs, tag, dep=None):
    ns, npc = len(specs), _n_pieces(specs)
    deps = [] if dep is None else [dep]

    def body(*refs):
        copies = _pair_copies(refs[:ns], refs[ns + len(deps):2 * ns + len(deps)], specs, *refs[2 * ns + len(deps):])
        for cp in copies:
            cp.start()
        for cp in copies:
            cp.wait()

    return pl.pallas_call(
        body, in_specs=[ANY] * (ns + len(deps)), out_specs=[ANY] * ns,
        out_shape=[jax.ShapeDtypeStruct(s, F32) for s in _pair_shapes(specs)],
        scratch_shapes=[pltpu.SemaphoreType.DMA((4 * npc,)), pltpu.SemaphoreType.DMA((4 * npc,))],
        name=f"reduce_pair_{tag}", compiler_params=pltpu.CompilerParams(has_side_effects=True))(*grads, *deps)


def pair_start(grads, specs, dep, tag):
    ns, npc = len(specs), _n_pieces(specs)
    slots = [lax.empty(s, F32) for s in _pair_shapes(specs)]
    deps = [] if dep is None else [dep]

    def body(*refs):
        ssem, rsem = refs[2 * ns + len(deps):2 * ns + len(deps) + 2]
        for cp in _pair_copies(refs[:ns], refs[ns:2 * ns], specs, ssem, rsem):
            cp.start()
        refs[-1][...] = jnp.zeros_like(refs[-1])

    outs = pl.pallas_call(
        body, in_specs=[HBM] * (2 * ns) + [ANY] * len(deps),
        out_specs=[SEM, SEM] + [HBM] * (2 * ns) + [pl.BlockSpec(memory_space=pltpu.VMEM)],
        out_shape=[pltpu.SemaphoreType.DMA((4 * npc,)), pltpu.SemaphoreType.DMA((4 * npc,))]
        + [pltpu.HBM(a.shape, a.dtype) for a in list(grads) + slots] + [jax.ShapeDtypeStruct((8, 128), F32)],
        input_output_aliases={i: 2 + i for i in range(2 * ns)},
        name=f"pair_start_{tag}", compiler_params=pltpu.CompilerParams(has_side_effects=EFFECT))(
            *_in_hbm(list(grads) + slots), *deps)
    return outs[0], outs[1], outs[2:2 + ns], outs[2 + ns:2 + 2 * ns], outs[-1]


def pair_wait(ssem, rsem, grads, slots, specs, after, tag):
    ns = len(specs)

    def body(*refs):
        for cp in _pair_copies(refs[:ns], refs[ns:2 * ns], specs, refs[2 * ns], refs[2 * ns + 1]):
            cp.wait_send()
            cp.wait_recv()

    outs = pl.pallas_call(
        body, in_specs=[HBM] * (2 * ns) + [SEM, SEM, ANY], out_specs=[HBM] * (2 * ns),
        out_shape=[pltpu.HBM(a.shape, a.dtype) for a in list(grads) + list(slots)],
        input_output_aliases={i: i for i in range(2 * ns)},
        name=f"pair_wait_{tag}", compiler_params=pltpu.CompilerParams(has_side_effects=EFFECT))(
            *grads, *slots, ssem, rsem, after)
    return outs[:ns], outs[ns:]


def pair_sum(g, r1, sp, tag):
    npc = len(sp.pieces)
    fetch_all = 4 * sp.rows * sp.cols * 4 <= (8 << 20)

    def body(g_ref, r_ref, own_ref, words_ref, gbuf, sem):
        l, j = pl.program_id(0), pl.program_id(1)
        x, y, c = _coords()

        def fetch(chip, slot):
            copies = [pltpu.make_async_copy(g_ref.at[l, pl.ds(pl.multiple_of(sp.off(2 * chip + c, s), 8), n), :],
                                            gbuf.at[slot, pl.ds(s, n), :], sem.at[slot, i])
                      for i, (s, n) in enumerate(sp.pieces)]
            for cp in copies:
                cp.start()
            return copies

        if fetch_all:
            @pl.when(j == 0)
            def _():
                for cp in [cp for chip in range(4) for cp in fetch(chip, chip)]:
                    cp.wait()

            mine = gbuf[j]
        else:
            for cp in fetch(j, 0):
                cp.wait()
            mine = gbuf[0]
        p = mine + r_ref[...]
        words_ref[...] = sp.to_words(p)

        @pl.when(j == 2 * x + y)
        def _():
            own_ref[...] = p

    return pl.pallas_call(
        body, grid=(sp.layers, 4),
        in_specs=[ANY, pl.BlockSpec((None, None, sp.rows, sp.cols), lambda l, j: (j, l, 0, 0))],
        out_specs=[pl.BlockSpec((None, sp.rows, sp.cols), lambda l, j: (l, 0, 0)),
                   pl.BlockSpec((None, None, sp.rows, sp.wcols), lambda l, j: (j, l, 0, 0))],
        out_shape=[jax.ShapeDtypeStruct((sp.layers, sp.rows, sp.cols), F32),
                   jax.ShapeDtypeStruct((4, sp.layers, sp.rows, sp.wcols), F32)],
        scratch_shapes=[pltpu.VMEM((4 if fetch_all else 1, sp.rows, sp.cols), F32), pltpu.SemaphoreType.DMA((4, npc))],
        name=f"pair_sum_{tag}", compiler_params=_params(("arbitrary", "arbitrary")))(g, r1)


def _chip_copies(srcs, dsts, ssem, rsem):
    x, y, c = _coords()
    copies = []
    for i, (src, dst) in enumerate(zip(srcs, dsts)):
        for t, (fx, fy) in enumerate(FLIPS):
            tx, ty = x ^ fx, y ^ fy
            copies.append(pltpu.make_async_remote_copy(src.at[2 * tx + ty], dst.at[t], ssem.at[3 * i + t], rsem.at[3 * i + t],
                                                       device_id=(tx, ty, c), device_id_type=MESH))
    return copies


def _slot_shapes(words):
    return [(3,) + w.shape[1:] for w in words]


def reduce_chips(words, specs, tag):
    ns = len(specs)

    def body(*refs):
        copies = _chip_copies(refs[:ns], refs[ns:2 * ns], *refs[2 * ns:])
        for cp in copies:
            cp.start()
        for cp in copies:
            cp.wait()

    return pl.pallas_call(
        body, in_specs=[ANY] * ns, out_specs=[ANY] * ns,
        out_shape=[jax.ShapeDtypeStruct(s, F32) for s in _slot_shapes(words)],
        scratch_shapes=[pltpu.SemaphoreType.DMA((3 * ns,)), pltpu.SemaphoreType.DMA((3 * ns,))],
        name=f"reduce_chips_{tag}", compiler_params=pltpu.CompilerParams(has_side_effects=True))(*words)


def chips_start(words, specs, tag):
    ns = len(specs)
    slots = [lax.empty(s, F32) for s in _slot_shapes(words)]

    def body(*refs):
        ssem, rsem = refs[2 * ns:2 * ns + 2]
        for cp in _chip_copies(refs[:ns], refs[ns:2 * ns], ssem, rsem):
            cp.start()
        refs[-1][...] = jnp.zeros_like(refs[-1])

    outs = pl.pallas_call(
        body, in_specs=[HBM] * (2 * ns),
        out_specs=[SEM, SEM] + [HBM] * (2 * ns) + [pl.BlockSpec(memory_space=pltpu.VMEM)],
        out_shape=[pltpu.SemaphoreType.DMA((3 * ns,)), pltpu.SemaphoreType.DMA((3 * ns,))]
        + [pltpu.HBM(a.shape, a.dtype) for a in list(words) + slots] + [jax.ShapeDtypeStruct((8, 128), F32)],
        input_output_aliases={i: 2 + i for i in range(2 * ns)},
        name=f"chips_start_{tag}", compiler_params=pltpu.CompilerParams(has_side_effects=EFFECT))(
            *_in_hbm(list(words) + slots))
    return outs[0], outs[1], outs[2:2 + ns], outs[2 + ns:2 + 2 * ns], outs[-1]


def chips_wait(ssem, rsem, words, slots, specs, after, tag):
    ns = len(specs)

    def body(*refs):
        for cp in _chip_copies(refs[:ns], refs[ns:2 * ns], refs[2 * ns], refs[2 * ns + 1]):
            cp.wait_send()
            cp.wait_recv()

    outs = pl.pallas_call(
        body, in_specs=[HBM] * (2 * ns) + [SEM, SEM, ANY], out_specs=[HBM] * (2 * ns),
        out_shape=[pltpu.HBM(a.shape, a.dtype) for a in list(words) + list(slots)],
        input_output_aliases={i: i for i in range(2 * ns)},
        name=f"chips_wait_{tag}", compiler_params=pltpu.CompilerParams(has_side_effects=EFFECT))(
            *words, *slots, ssem, rsem, after)
    return outs[ns:]


def sum_chips(own, r2, sp, tag):
    def body(own_ref, r_ref, o_ref):
        o_ref[...] = ((own_ref[...] + sp.from_words(r_ref[0])) + sp.from_words(r_ref[1])) + sp.from_words(r_ref[2])

    blk = pl.BlockSpec((None, sp.rows, sp.cols), lambda l: (l, 0, 0))
    return pl.pallas_call(
        body, grid=(sp.layers,), in_specs=[blk, pl.BlockSpec((3, None, sp.rows, sp.wcols), lambda l: (0, l, 0, 0))],
        out_specs=blk, out_shape=jax.ShapeDtypeStruct((sp.layers, sp.rows, sp.cols), F32),
        name=f"sum_chips_{tag}", compiler_params=_params(("arbitrary",)))(own, r2)


def reduce_scatter_start(grads, specs, names, dep, tag):
    theirs = reduce_pair(grads, specs, tag, dep)
    sums = [pair_sum(g, r1, sp, f"{tag}_{n}") for g, r1, sp, n in zip(grads, theirs, specs, names)]
    ssem, rsem, words, slots, token = chips_start([s[1] for s in sums], specs, tag)
    return (ssem, rsem, words, slots, [s[0] for s in sums]), token


def reduce_scatter_finish(state, after, specs, tag):
    ssem, rsem, words, slots, own = state
    return list(zip(own, chips_wait(ssem, rsem, words, slots, specs, after, tag)))


def reduce_scatter(grads, specs, names, tag):
    theirs = reduce_pair(grads, specs, tag)
    sums = [pair_sum(g, r1, sp, f"{tag}_{n}") for g, r1, sp, n in zip(grads, theirs, specs, names)]
    return list(zip([s[0] for s in sums], reduce_chips([s[1] for s in sums], specs, tag)))


def _adamw_math(w, g, m, v):
    c1 = 1.0 - ADAM_B1 ** ADAM_STEP
    c2 = 1.0 - ADAM_B2 ** ADAM_STEP
    m2 = ADAM_B1 * m + (1.0 - ADAM_B1) * g
    v2 = ADAM_B2 * v + (1.0 - ADAM_B2) * (g * g)
    return -ADAM_LR * ((m2 / c1) / (jnp.sqrt(v2 / c2) + ADAM_EPS) + ADAM_WD * w), m2, v2


def adamw(w, g, m, v, name):
    shape = w.shape
    cols = shape[-1]
    rows = math.prod(shape[:-1])
    tr = rows
    while tr * cols * 4 > (1 << 20) and tr % 16 == 0:
        tr //= 2

    def body(w_ref, g_ref, m_ref, v_ref, d_ref, nm_ref, nv_ref):
        d_ref[...], nm_ref[...], nv_ref[...] = _adamw_math(w_ref[...], g_ref[...], m_ref[...], v_ref[...])

    blk = pl.BlockSpec((tr, cols), lambda i: (i, 0))
    outs = pl.pallas_call(
        body, grid=(rows // tr,), in_specs=[blk] * 4, out_specs=[blk] * 3,
        out_shape=[jax.ShapeDtypeStruct((rows, cols), F32)] * 3,
        name=f"adamw_{name}", compiler_params=_params(("arbitrary",)))(
            *[a.reshape(rows, cols) for a in (w, g, m, v)])
    return [o.reshape(shape) for o in outs]


def adamw_layer(w, sums, m, v, sp, l, prev, dep, name):
    _, rows, cols = w.shape
    tc = _col_tile(cols)
    twc = tc // 2 if sp.packed else tc
    extra = ([] if prev is None else list(prev)) + ([] if dep is None else [dep])

    def body(w_ref, own_ref, r_ref, m_ref, v_ref, *rest):
        g_ref, d_ref, nm_ref, nv_ref = rest[-4:]
        g = ((own_ref[...] + sp.from_words(r_ref[0])) + sp.from_words(r_ref[1])) + sp.from_words(r_ref[2])
        g_ref[...] = g
        d_ref[...], nm_ref[...], nv_ref[...] = _adamw_math(w_ref[...], g, m_ref[...], v_ref[...])

    blk = pl.BlockSpec((None, rows, tc), lambda n: (l, 0, n))
    return pl.pallas_call(
        body, grid=(cols // tc,),
        in_specs=[blk, pl.BlockSpec((None, rows, tc), lambda n: (0, 0, n)),
                  pl.BlockSpec((3, None, rows, twc), lambda n: (0, 0, 0, n)), blk, blk] + [ANY] * len(extra),
        out_specs=[blk] * 4, out_shape=[jax.ShapeDtypeStruct(w.shape, F32)] * 4,
        input_output_aliases={} if prev is None else {5 + i: i for i in range(4)},
        name=f"adamw_{name}_l{l}", compiler_params=_params(("arbitrary",)))(w, sums[0], sums[1], m, v, *extra)


WEIGHTS = ("pre_norm_g", "w_in", "gm_ln_g", "gm_ln_b", "gm_ws", "gm_bs", "mla_q_norm_g", "mla_w_uq", "mla_kv_norm_g",
           "mla_w_ukv", "lru_conv_w", "lru_conv_b", "lru_w_a", "lru_b_a", "lru_w_x", "lru_b_x", "lru_lambda",
           "w_proj_a", "w_proj_b", "w_proj_c", "w_out", "post_norm_g")
SHARDED = ("w_in", "mla_w_uq", "mla_w_ukv", "lru_conv_w", "w_proj_a", "w_proj_b", "w_proj_c", "w_out")
REPLICATED = tuple(n for n in WEIGHTS if n not in SHARDED)


def _step(x, target, wts, ms, vs):
    t12 = lambda a: jnp.swapaxes(a, 1, 2)
    names = list(WEIGHT_SPECS)
    specs = [WEIGHT_SPECS[n] for n in names]
    tabs = _rope_tables()
    own = {"w_in_t": t12(wts["w_in"]), "wq": t12(wts["mla_w_uq"]), "wkv": t12(wts["mla_w_ukv"]),
           "conv": jnp.pad(t12(wts["lru_conv_w"]), ((0, 0), (0, 0), (0, 124))),
           "w_proj_a": wts["w_proj_a"], "w_proj_b": wts["w_proj_b"], "w_proj_c": wts["w_proj_c"], "w_out": wts["w_out"]}
    first, rest = ["w_in_t"], [n for n in names if n != "w_in_t"]
    sfirst, srest = [WEIGHT_SPECS[n] for n in first], [WEIGHT_SPECS[n] for n in rest]

    w = {n: wts[n] for n in REPLICATED}
    w["kv_g384"] = jnp.concatenate([wts["mla_kv_norm_g"], jnp.ones((L, 128), F32)], axis=1)
    w["wa_dense"] = _block_diag(wts["lru_w_a"])
    w["wx_dense"] = _block_diag(wts["lru_w_x"])

    def layer_weights(ns, words):
        gw = dict(zip(ns, words))
        gw["wq"] = gw["wq"].reshape(HEADS, 192, 384)
        gw["wkv"] = gw["wkv"].reshape(HEADS, 256, 128)
        gw["conv"] = gw["conv"][0, :, :4].T
        return gw

    place = lambda l, dep: {n: pack_place(own[n], WEIGHT_SPECS[n], l, f"w{l}_{n}", dep) for n in names}
    placed = [place(0, None)]
    words_of = lambda l, ns: [placed[l][n][0] for n in ns]
    bufs_of = lambda l, ns: [placed[l][n][1] for n in ns]
    later = {}

    ssem_a, rsem_a, wthru_a, fthru_a, token_a = gather_start(words_of(0, first), bufs_of(0, first), sfirst, None, "w0a")
    placed.append(place(1, token_a))
    win0 = gather_forward(gather_wait(ssem_a, rsem_a, wthru_a, fthru_a, sfirst, placed[1]["w_in_t"][0], "w0a"), sfirst, "w0a")
    ssem_b, rsem_b, wthru_b, fthru_b, token_b = gather_start(words_of(0, rest), bufs_of(0, rest), srest, win0[0], "w0b")

    def fwd0_mid(ya):
        rest0 = gather_forward(gather_wait(ssem_b, rsem_b, wthru_b, fthru_b, srest, ya, "w0b"), srest, "w0b")
        later["w1"] = gather_start(words_of(1, names), bufs_of(1, names), specs, rest0[0], "w1")
        later["gw0"] = layer_weights(first + rest, list(win0) + list(rest0))
        return later["gw0"], later["w1"][4]

    x1, saved0 = _layer_fwd(x, 0, w, {"w_in_t": win0[0]}, tabs, dep=token_b, mid=fwd0_mid)
    ssem1, rsem1, wthru1, fthru1, _ = later["w1"]
    words1 = gather_forward(gather_wait(ssem1, rsem1, wthru1, fthru1, specs, x1, "w1"), specs, "w1")
    gw0, gw1 = later["gw0"], layer_weights(names, words1)
    x2, saved1 = _layer_fwd(x1, 1, w, gw1, tabs)
    loss, dx2 = loss_head(x2, target)

    def bwd1_mid(gg, last):
        later["p1b"] = pair_start([gg[n] for n in rest], srest, last, "g1b")
        return later["p1b"][4]

    dx1, gg1, g1 = _layer_bwd(dx2, 1, w, gw1, tabs, saved1, mid=bwd1_mid)
    grads1b, theirs1b = pair_wait(*later["p1b"][:4], srest, dx1, "g1b")
    p1a = pair_start([gg1["w_in_t"]], sfirst, theirs1b[0], "g1a")

    def bwd0_early(last):
        grads1a, theirs1a = pair_wait(*p1a[:4], sfirst, last, "g1a")
        mine = dict(zip(first + rest, list(grads1a) + list(grads1b)))
        theirs = dict(zip(first + rest, list(theirs1a) + list(theirs1b)))
        sums = [pair_sum(mine[n], theirs[n], WEIGHT_SPECS[n], f"g1_{n}") for n in names]
        ssem, rsem, words, slots, token = chips_start([s[1] for s in sums], specs, "g1")
        later["g1"] = (ssem, rsem, words, slots, [s[0] for s in sums])
        return token

    def bwd0_mid(gg, last):
        later["s1"] = reduce_scatter_finish(later["g1"], last, specs, "g1")
        later["g0b"], token = reduce_scatter_start([gg[n] for n in rest], srest, rest, later["s1"][0][1], "g0b")
        return token

    dx0, gg0, g0 = _layer_bwd(dx1, 0, w, gw0, tabs, saved0, dep=p1a[4], early=bwd0_early, mid=bwd0_mid)
    s1 = dict(zip(names, later["s1"]))
    s0 = dict(zip(rest, reduce_scatter_finish(later["g0b"], dx0, srest, "g0b")))
    rep_flat = jnp.concatenate([jnp.stack([g0[n], g1[n]]).reshape(-1) for n in REPLICATED])
    rep_flat = jnp.pad(rep_flat, (0, REP_ROWS * NDEV * D - rep_flat.shape[0])).reshape(1, REP_ROWS * NDEV, D)
    state_a, token_g = reduce_scatter_start([gg0["w_in_t"], rep_flat], sfirst + [REP_SPEC], first + ["rep"], None, "g0a")

    keys = {"w_in": "w_in_t", "mla_w_uq": "wq", "mla_w_ukv": "wkv",
            "w_proj_a": "w_proj_a", "w_proj_b": "w_proj_b", "w_proj_c": "w_proj_c", "w_out": "w_out"}
    transposed = ("w_in", "mla_w_uq", "mla_w_ukv")
    state_of = lambda n: [own[keys[n]], t12(ms[n]), t12(vs[n])] if n in transposed else [wts[n], ms[n], vs[n]]

    def update(n, l, sums, prev, dep):
        wl, ml, vl = state_of(n)
        return adamw_layer(wl, sums[keys[n]], ml, vl, WEIGHT_SPECS[keys[n]], l, prev, dep, n)

    upd = {n: update(n, 1, s1, None, token_g) for n in keys}
    for n in keys:
        if n != "w_in":
            upd[n] = update(n, 0, s0, upd[n], None)
    s0["w_in_t"], rep_parts = reduce_scatter_finish(state_a, upd["w_out"][0], sfirst + [REP_SPEC], "g0a")
    upd["w_in"] = update("w_in", 0, s0, upd["w_in"], None)
    rep_sum = sum_chips(*rep_parts, REP_SPEC, "rep")
    rep_full = all_gather([rep_sum], 0, [REP_SPEC], ["rep"], "rep")[0].reshape(-1)

    out = {n: [t12(r) for r in upd[n]] if n in transposed else upd[n] for n in keys}
    conv_sp = WEIGHT_SPECS["conv"]
    g_conv = t12(jnp.concatenate([sum_chips(*s0["conv"], conv_sp, "conv0"), sum_chips(*s1["conv"], conv_sp, "conv1")])[:, :, :4])
    out["lru_conv_w"] = [g_conv] + adamw(wts["lru_conv_w"], g_conv, ms["lru_conv_w"], vs["lru_conv_w"], "lru_conv_w")
    at = 0
    for n in REPLICATED:
        size = math.prod(wts[n].shape)
        g = rep_full[at:at + size].reshape(wts[n].shape)
        out[n] = [g] + adamw(wts[n], g, ms[n], vs[n], n)
        at += size

    loss = lax.psum(loss, ("x", "y", "c"))
    return (loss, dx0[None], *[out[n][k] for k in range(4) for n in WEIGHTS])


def kernel(x, pre_norm_g, w_in, gm_ln_g, gm_ln_b, gm_ws, gm_bs, mla_q_norm_g, mla_w_uq, mla_kv_norm_g, mla_w_ukv, lru_conv_w, lru_conv_b, lru_w_a, lru_b_a, lru_w_x, lru_b_x, lru_lambda, w_proj_a, w_proj_b, w_proj_c, w_out, post_norm_g, loss_target, m_pre_norm_g, m_w_in, m_gm_ln_g, m_gm_ln_b, m_gm_ws, m_gm_bs, m_mla_q_norm_g, m_mla_w_uq, m_mla_kv_norm_g, m_mla_w_ukv, m_lru_conv_w, m_lru_conv_b, m_lru_w_a, m_lru_b_a, m_lru_w_x, m_lru_b_x, m_lru_lambda, m_w_proj_a, m_w_proj_b, m_w_proj_c, m_w_out, m_post_norm_g, v_pre_norm_g, v_w_in, v_gm_ln_g, v_gm_ln_b, v_gm_ws, v_gm_bs, v_mla_q_norm_g, v_mla_w_uq, v_mla_kv_norm_g, v_mla_w_ukv, v_lru_conv_w, v_lru_conv_b, v_lru_w_a, v_lru_b_a, v_lru_w_x, v_lru_b_x, v_lru_lambda, v_w_proj_a, v_w_proj_b, v_w_proj_c, v_w_out, v_post_norm_g):
    wts = dict(zip(WEIGHTS, (pre_norm_g, w_in, gm_ln_g, gm_ln_b, gm_ws, gm_bs, mla_q_norm_g, mla_w_uq, mla_kv_norm_g,
                             mla_w_ukv, lru_conv_w, lru_conv_b, lru_w_a, lru_b_a, lru_w_x, lru_b_x, lru_lambda,
                             w_proj_a, w_proj_b, w_proj_c, w_out, post_norm_g)))
    ms = dict(zip(WEIGHTS, (m_pre_norm_g, m_w_in, m_gm_ln_g, m_gm_ln_b, m_gm_ws, m_gm_bs, m_mla_q_norm_g, m_mla_w_uq,
                            m_mla_kv_norm_g, m_mla_w_ukv, m_lru_conv_w, m_lru_conv_b, m_lru_w_a, m_lru_b_a, m_lru_w_x,
                            m_lru_b_x, m_lru_lambda, m_w_proj_a, m_w_proj_b, m_w_proj_c, m_w_out, m_post_norm_g)))
    vs = dict(zip(WEIGHTS, (v_pre_norm_g, v_w_in, v_gm_ln_g, v_gm_ln_b, v_gm_ws, v_gm_bs, v_mla_q_norm_g, v_mla_w_uq,
                            v_mla_kv_norm_g, v_mla_w_ukv, v_lru_conv_w, v_lru_conv_b, v_lru_w_a, v_lru_b_a, v_lru_w_x,
                            v_lru_b_x, v_lru_lambda, v_w_proj_a, v_w_proj_b, v_w_proj_c, v_w_out, v_post_norm_g)))
    return _step(x[0], loss_target[0], wts, ms, vs)
```

```python
import functools
import math

import jax
import jax.numpy as jnp
from jax import lax
from jax.experimental import pallas as pl
from jax.experimental.pallas import tpu as pltpu

F32 = jnp.float32
BF16 = jnp.bfloat16

T = 2048
D = 1024
L = 2
NDEV = 8
EPS = 1e-6
CHUNK_SHIFT = 6
HEADS = 8
QK = 192
LRU_W = 1280
LRU_TILE = 640
N_IN = 10432
SHARD = N_IN // NDEV
OFF_U, OFF_V, OFF_ZA, OFF_CQ, OFF_CKV, OFF_ZB = 0, 1024, 2048, 3072, 3456, 3840
OFF_XC, OFF_ZC, OFF_GA, OFF_GB, OFF_GC = 5120, 6400, 7680, 8704, 9728
NPAD = 10752
PAD1_AT, PAD1 = 3776, 64
PAD2_AT, PAD2 = 4800, 256
WIN_PIECES = ((0, 888), (888, 280), (1168, 136))
VMEM_LIMIT = 60 * 1024 * 1024

ADAM_LR, ADAM_B1, ADAM_B2, ADAM_EPS, ADAM_WD, ADAM_STEP = 0.001, 0.9, 0.999, 1e-08, 0.01, 10

_NN = (((1,), (0,)), ((), ()))
_NT = (((1,), (1,)), ((), ()))
_TN = (((0,), (0,)), ((), ()))


def _dg(a, b, dims):
    return lax.dot_general(a.astype(BF16), b.astype(BF16), dims, preferred_element_type=F32)


@jax.custom_vjp
def dot_nn(a, b):
    return _dg(a, b, _NN)


def _nn_fwd(a, b):
    return _dg(a, b, _NN), (a, b)


def _nn_bwd(res, g):
    a, b = res
    return _dg(g, b, _NT).astype(a.dtype), _dg(a, g, _TN).astype(b.dtype)


dot_nn.defvjp(_nn_fwd, _nn_bwd)


@jax.custom_vjp
def dot_nt(a, b):
    return _dg(a, b, _NT)


def _nt_fwd(a, b):
    return _dg(a, b, _NT), (a, b)


def _nt_bwd(res, g):
    a, b = res
    return _dg(g, b, _NN).astype(a.dtype), _dg(g, a, _TN).astype(b.dtype)


dot_nt.defvjp(_nt_fwd, _nt_bwd)


def _params(sem=None):
    return pltpu.CompilerParams(dimension_semantics=sem, vmem_limit_bytes=VMEM_LIMIT)


def _sigmoid(x):
    return 1.0 / (1.0 + jnp.exp(-x))


def _silu(x):
    return x * _sigmoid(x)


def _rms(x, g):
    ms = jnp.mean(x * x, axis=-1, keepdims=True)
    return x * lax.rsqrt(ms + EPS) * g


def _acc(ref, val, first):
    @pl.when(first)
    def _():
        ref[...] = val

    @pl.when(jnp.logical_not(first))
    def _():
        ref[...] += val


ANY = pl.BlockSpec(memory_space=pl.ANY)


INPROJ_TN = 768


def inproj_fwd(x, g, wt, l, dep=None):
    tn = INPROJ_TN

    def body(x_ref, g_ref, w_ref, *rest):
        proj_ref, h_ref = rest[-2:]

        @pl.when(pl.program_id(0) == 0)
        def _():
            h_ref[...] = _rms(x_ref[...], g_ref[...]).astype(BF16)

        proj_ref[...] = lax.dot_general(h_ref[...], _unpack(w_ref[...]).astype(BF16), _NT, preferred_element_type=F32)

    deps = [] if dep is None else [dep]
    return pl.pallas_call(
        body, grid=(NPAD // tn,),
        in_specs=[pl.BlockSpec((T, D), lambda j: (0, 0)), pl.BlockSpec((1, D), lambda j: (0, 0)),
                  pl.BlockSpec((None, tn, D // 2), lambda j: (0, j, 0))] + [ANY] * len(deps),
        out_specs=[pl.BlockSpec((T, tn), lambda j: (0, j)), pl.BlockSpec((T, D), lambda j: (0, 0))],
        out_shape=[jax.ShapeDtypeStruct((T, NPAD), F32), jax.ShapeDtypeStruct((T, D), BF16)],
        name=f"inproj_fwd_l{l}", compiler_params=_params(("arbitrary",)))(x, g, wt, *deps)


def inproj_bwd(dproj, h, wt, l, dep=None):
    tn = INPROJ_TN
    deps = [] if dep is None else [dep]

    def body(dp_ref, h_ref, w_ref, *rest):
        dwt_ref, dh_ref = rest[-2:]
        dp = dp_ref[...]
        dwt_ref[...] = lax.dot_general(dp, h_ref[...], _TN, preferred_element_type=F32)
        contrib = lax.dot_general(dp, _unpack(w_ref[...]).astype(BF16), _NN, preferred_element_type=F32)
        _acc(dh_ref, contrib, pl.program_id(0) == 0)

    return pl.pallas_call(
        body, grid=(NPAD // tn,),
        in_specs=[pl.BlockSpec((T, tn), lambda j: (0, j)), pl.BlockSpec((T, D), lambda j: (0, 0)),
                  pl.BlockSpec((None, tn, D // 2), lambda j: (0, j, 0))] + [ANY] * len(deps),
        out_specs=[pl.BlockSpec((None, tn, D), lambda j: (0, j, 0)), pl.BlockSpec((T, D), lambda j: (0, 0))],
        out_shape=[jax.ShapeDtypeStruct((1, NPAD, D), F32), jax.ShapeDtypeStruct((T, D), F32)],
        name=f"inproj_bwd_l{l}", compiler_params=_params(("arbitrary",)))(dproj, h, wt, *deps)


def prenorm_bwd(x, g, dh, dxn, l):
    tm = 256

    def body(x_ref, g_ref, dh_ref, dxn_ref, dx_ref, dg_ref):
        _, vjp = jax.vjp(_rms, x_ref[...], g_ref[...])
        dx, dg = vjp(dh_ref[...])
        dx_ref[...] = dx + dxn_ref[...]
        _acc(dg_ref, dg, pl.program_id(0) == 0)

    tok = pl.BlockSpec((tm, D), lambda i: (i, 0))
    vec = pl.BlockSpec((1, D), lambda i: (0, 0))
    return pl.pallas_call(
        body, grid=(T // tm,), in_specs=[tok, vec, tok, tok], out_specs=[tok, vec],
        out_shape=[jax.ShapeDtypeStruct((T, D), F32), jax.ShapeDtypeStruct((1, D), F32)],
        name=f"prenorm_bwd_l{l}", compiler_params=_params(("arbitrary",)))(x, g, dh, dxn)


def _gmlp_tile(u, v, z, ln_g, ln_b, ws, bs):
    mu = jnp.mean(v, axis=-1, keepdims=True)
    vc = v - mu
    var = jnp.mean(vc * vc, axis=-1, keepdims=True)
    vn = vc * lax.rsqrt(var + EPS) * ln_g + ln_b
    qi = lax.broadcasted_iota(jnp.int32, (128, 128), 0) >> CHUNK_SHIFT
    kj = lax.broadcasted_iota(jnp.int32, (128, 128), 1) >> CHUNK_SHIFT
    mask = kj <= qi
    outs = []
    for g in range(4):
        wm = jnp.where(mask, ws[g], 0.0)
        outs.append(dot_nn(wm, vn[:, 256 * g:256 * (g + 1)]) + bs[g])
    sv = jnp.concatenate(outs, axis=1)
    return u * sv * _silu(z)


def _gmlp_specs():
    blk = lambda c: pl.BlockSpec((128, 1024), lambda n, c=c: (n, c))
    vec = pl.BlockSpec((1, 1024), lambda n: (0, 0))
    return [blk(0), blk(1), blk(2), vec, vec,
            pl.BlockSpec((4, 128, 128), lambda n: (0, 0, 0)), pl.BlockSpec((4, 128, 1), lambda n: (0, 0, 0))]


def gmlp_fwd(proj, ln_g, ln_b, ws, bs, l):
    def body(u_ref, v_ref, z_ref, g_ref, b_ref, ws_ref, bs_ref, y_ref):
        y_ref[...] = _gmlp_tile(u_ref[...], v_ref[...], z_ref[...], g_ref[...], b_ref[...],
                                [ws_ref[g] for g in range(4)], [bs_ref[g] for g in range(4)])

    return pl.pallas_call(
        body, grid=(T // 128,), in_specs=_gmlp_specs(),
        out_specs=pl.BlockSpec((128, 1024), lambda n: (n, 0)),
        out_shape=jax.ShapeDtypeStruct((T, 1024), F32),
        name=f"gmlp_fwd_l{l}", compiler_params=_params(("arbitrary",)))(proj, proj, proj, ln_g, ln_b, ws, bs)


def gmlp_bwd(proj, ln_g, ln_b, ws, bs, dy, l):
    def body(u_ref, v_ref, z_ref, g_ref, b_ref, ws_ref, bs_ref, dy_ref, dseg_ref, dg_ref, db_ref, dws_ref, dbs_ref):
        first = pl.program_id(0) == 0
        _, vjp = jax.vjp(_gmlp_tile, u_ref[...], v_ref[...], z_ref[...], g_ref[...], b_ref[...],
                         [ws_ref[g] for g in range(4)], [bs_ref[g] for g in range(4)])
        du, dv, dz, dg, db, dws, dbs = vjp(dy_ref[...])
        dseg_ref[:, 0:1024] = du.astype(BF16)
        dseg_ref[:, 1024:2048] = dv.astype(BF16)
        dseg_ref[:, 2048:3072] = dz.astype(BF16)
        _acc(dg_ref, dg, first)
        _acc(db_ref, db, first)
        for g in range(4):
            _acc(dws_ref.at[g], dws[g], first)
            _acc(dbs_ref.at[g], dbs[g], first)

    vec = pl.BlockSpec((1, 1024), lambda n: (0, 0))
    return pl.pallas_call(
        body, grid=(T // 128,), in_specs=_gmlp_specs() + [pl.BlockSpec((128, 1024), lambda n: (n, 0))],
        out_specs=[pl.BlockSpec((128, 3072), lambda n: (n, 0)), vec, vec,
                   pl.BlockSpec((4, 128, 128), lambda n: (0, 0, 0)), pl.BlockSpec((4, 128, 1), lambda n: (0, 0, 0))],
        out_shape=[jax.ShapeDtypeStruct((T, 3072), BF16), jax.ShapeDtypeStruct((1, 1024), F32),
                   jax.ShapeDtypeStruct((1, 1024), F32), jax.ShapeDtypeStruct((4, 128, 128), F32),
                   jax.ShapeDtypeStruct((4, 128, 1), F32)],
        name=f"gmlp_bwd_l{l}", compiler_params=_params(("arbitrary",)))(proj, proj, proj, ln_g, ln_b, ws, bs, dy)


QKV_TM = 256


def _qkv_tile(cq, ckvr, qg, kvg, wq, wkv, ctab, stab):
    tm = cq.shape[0]
    cqn = _rms(cq, qg)
    lane = lax.broadcasted_iota(jnp.int32, ckvr.shape, 1)
    iskv = lane < 256
    ms = jnp.sum(jnp.where(iskv, ckvr * ckvr, 0.0), axis=-1, keepdims=True) * (1.0 / 256)
    lm = jnp.where(iskv, ckvr * lax.rsqrt(ms + EPS) * kvg, ckvr)
    r = lax.broadcasted_iota(jnp.int32, (64, 128), 0)
    c = lax.broadcasted_iota(jnp.int32, (64, 128), 1)
    eye = jnp.where(c == r, 1.0, 0.0)
    eye_sw = jnp.where(c == ((r + 32) & 63), 1.0, 0.0)
    z64 = jnp.zeros((64, 256), F32)
    z128 = jnp.zeros((128, 128), F32)
    rk_rope = jnp.concatenate([z64, eye], axis=1)
    rk_sw = jnp.concatenate([jnp.zeros((128, 384), F32), jnp.concatenate([z64, eye_sw], axis=1)], axis=0)
    k_sw = dot_nt(lm, rk_sw) * stab
    qs, ks, vs = [], [], []
    for h in range(HEADS):
        wn, w1, w2 = wq[h]
        wk, wv = wkv[h]
        wq_h = jnp.concatenate([wn, w1, w2], axis=0)
        wq_sw = jnp.concatenate([jnp.zeros((128, 384), F32), w2, w1], axis=0)
        qs.append(dot_nt(cqn, wq_h) * ctab + dot_nt(cqn, wq_sw) * stab)
        rk_h = jnp.concatenate([jnp.concatenate([wk, z128], axis=1), rk_rope], axis=0)
        ks.append(dot_nt(lm, rk_h) * ctab + k_sw)
        vs.append(dot_nt(lm, jnp.concatenate([wv, z128], axis=1)))
    return qs, ks, vs


def _qkv_in_specs():
    tm = QKV_TM
    return [pl.BlockSpec((tm, 384), lambda i: (i, OFF_CQ // 384)), pl.BlockSpec((tm, 384), lambda i: (i, OFF_CKV // 384)),
            pl.BlockSpec((1, 384), lambda i: (0, 0)), pl.BlockSpec((1, 384), lambda i: (0, 0)),
            pl.BlockSpec((HEADS, 192, 384), lambda i: (0, 0, 0)), pl.BlockSpec((HEADS, 256, 128), lambda i: (0, 0, 0)),
            pl.BlockSpec((tm, 192), lambda i: (i, 0)), pl.BlockSpec((tm, 192), lambda i: (i, 0))]


def _qkv_weights(wq_ref, wkv_ref):
    wq = [(wq_ref[h, 0:128, :], wq_ref[h, 128:160, :], wq_ref[h, 160:192, :]) for h in range(HEADS)]
    wkv = [(_unpack(wkv_ref[h, 0:128, :]), _unpack(wkv_ref[h, 128:256, :])) for h in range(HEADS)]
    return wq, wkv


def qkv_fwd(proj, qg, kvg, wq, wkv, ctab, stab, l, dep=None):
    tm = QKV_TM
    deps = [] if dep is None else [dep]

    def body(cq_ref, ckvr_ref, qg_ref, kvg_ref, wq_ref, wkv_ref, c_ref, s_ref, *rest):
        q_ref, k_ref, v_ref = rest[-3:]
        wq_l, wkv_l = _qkv_weights(wq_ref, wkv_ref)
        qs, ks, vs = _qkv_tile(cq_ref[...], ckvr_ref[...], qg_ref[...], kvg_ref[...], wq_l, wkv_l, c_ref[...], s_ref[...])
        for h in range(HEADS):
            q_ref[h] = qs[h]
            k_ref[h] = ks[h]
            v_ref[h] = vs[h]

    return pl.pallas_call(
        body, grid=(T // tm,), in_specs=_qkv_in_specs() + [ANY] * len(deps),
        out_specs=[pl.BlockSpec((HEADS, tm, QK), lambda i: (0, i, 0)), pl.BlockSpec((HEADS, tm, QK), lambda i: (0, i, 0)),
                   pl.BlockSpec((HEADS, tm, 128), lambda i: (0, i, 0))],
        out_shape=[jax.ShapeDtypeStruct((HEADS, T, QK), F32), jax.ShapeDtypeStruct((HEADS, T, QK), F32),
                   jax.ShapeDtypeStruct((HEADS, T, 128), F32)],
        name=f"qkv_fwd_l{l}", compiler_params=_params(("arbitrary",)))(proj, proj, qg, kvg, wq, wkv, ctab, stab, *deps)


def qkv_bwd(proj, qg, kvg, wq, wkv, ctab, stab, dq, dk, dv, l):
    tm = QKV_TM

    def body(cq_ref, ckvr_ref, qg_ref, kvg_ref, wq_ref, wkv_ref, c_ref, s_ref, dq_ref, dk_ref, dv_ref,
             dseg_ref, dqg_ref, dkvg_ref, dwq_ref, dwkv_ref):
        first = pl.program_id(0) == 0
        wq_l, wkv_l = _qkv_weights(wq_ref, wkv_ref)
        c_tab, s_tab = c_ref[...], s_ref[...]
        fn = lambda cq, ckvr, qg_, kvg_, wq_, wkv_: _qkv_tile(cq, ckvr, qg_, kvg_, wq_, wkv_, c_tab, s_tab)
        _, vjp = jax.vjp(fn, cq_ref[...], ckvr_ref[...], qg_ref[...], kvg_ref[...], wq_l, wkv_l)
        cts = ([dq_ref[h] for h in range(HEADS)], [dk_ref[h] for h in range(HEADS)], [dv_ref[h] for h in range(HEADS)])
        dcq, dckvr, dqg, dkvg, dwq, dwkv = vjp(cts)
        dseg_ref[:, 0:384] = dcq.astype(BF16)
        dseg_ref[:, 384:768] = dckvr.astype(BF16)
        _acc(dqg_ref, dqg, first)
        _acc(dkvg_ref, dkvg, first)
        for h in range(HEADS):
            _acc(dwq_ref.at[h, 0:128, :], dwq[h][0], first)
            _acc(dwq_ref.at[h, 128:160, :], dwq[h][1], first)
            _acc(dwq_ref.at[h, 160:192, :], dwq[h][2], first)
            _acc(dwkv_ref.at[h, 0:128, :], dwkv[h][0], first)
            _acc(dwkv_ref.at[h, 128:256, :], dwkv[h][1], first)

    hq = pl.BlockSpec((HEADS, tm, QK), lambda i: (0, i, 0))
    return pl.pallas_call(
        body, grid=(T // tm,),
        in_specs=_qkv_in_specs() + [hq, hq, pl.BlockSpec((HEADS, tm, 128), lambda i: (0, i, 0))],
        out_specs=[pl.BlockSpec((tm, 768), lambda i: (i, 0)), pl.BlockSpec((1, 384), lambda i: (0, 0)),
                   pl.BlockSpec((1, 384), lambda i: (0, 0)), pl.BlockSpec((HEADS, 192, 384), lambda i: (0, 0, 0)),
                   pl.BlockSpec((HEADS, 256, 256), lambda i: (0, 0, 0))],
        out_shape=[jax.ShapeDtypeStruct((T, 768), BF16), jax.ShapeDtypeStruct((1, 384), F32),
                   jax.ShapeDtypeStruct((1, 384), F32), jax.ShapeDtypeStruct((HEADS, 192, 384), F32),
                   jax.ShapeDtypeStruct((HEADS, 256, 256), F32)],
        name=f"qkv_bwd_l{l}", compiler_params=_params(("arbitrary",)))(
            proj, proj, qg, kvg, wq, wkv, ctab, stab, dq, dk, dv)


ATT_TQ_FWD = 256
ATT_TQ_BWD = 512


def _attn_tile(q, kv_past, k, v, zb):
    q = q * (1.0 / math.sqrt(QK))
    s = dot_nt(q, k)
    qc = lax.broadcasted_iota(jnp.int32, s.shape, 0) >> CHUNK_SHIFT
    kc = lax.broadcasted_iota(jnp.int32, s.shape, 1) >> CHUNK_SHIFT
    s = jnp.where(kc <= qc, s, -1e30)
    m = jnp.max(s, axis=-1, keepdims=True)
    if kv_past is not None:
        sp = dot_nt(q, kv_past[0])
        m = jnp.maximum(m, jnp.max(sp, axis=-1, keepdims=True))
    m = lax.stop_gradient(m)
    p = jnp.exp(s - m)
    denom = jnp.sum(p, axis=-1, keepdims=True)
    o = dot_nn(p, v)
    if kv_past is not None:
        pp = jnp.exp(sp - m)
        denom = denom + jnp.sum(pp, axis=-1, keepdims=True)
        o = o + dot_nn(pp, kv_past[1])
    return o * (1.0 / denom) * _silu(zb)


def _attn_operands(k_ref, v_ref, g, tq):
    n = tq * g
    past = (k_ref[0:n, :], v_ref[0:n, :]) if g else None
    return past, k_ref[n:n + tq, :], v_ref[n:n + tq, :]


def _attn_in_specs(tq):
    return [pl.BlockSpec((None, tq, QK), lambda h, i: (h, i, 0)), pl.BlockSpec((None, T, QK), lambda h, i: (h, 0, 0)),
            pl.BlockSpec((None, T, 128), lambda h, i: (h, 0, 0)),
            pl.BlockSpec((tq, 128), lambda h, i: (i, OFF_ZB // 128 + h))]


def attn_fwd(q, k, v, proj, l):
    tq = ATT_TQ_FWD

    def body(q_ref, k_ref, v_ref, z_ref, y_ref):
        for g in range(T // tq):
            @pl.when(pl.program_id(1) == g)
            def _(g=g):
                past, k, v = _attn_operands(k_ref, v_ref, g, tq)
                y_ref[...] = _attn_tile(q_ref[...], past, k, v, z_ref[...])

    return pl.pallas_call(
        body, grid=(HEADS, T // tq), in_specs=_attn_in_specs(tq),
        out_specs=pl.BlockSpec((tq, 128), lambda h, i: (i, h)),
        out_shape=jax.ShapeDtypeStruct((T, 1024), F32),
        name=f"attn_fwd_l{l}", compiler_params=_params(("arbitrary", "arbitrary")))(q, k, v, proj)


def attn_bwd(q, k, v, proj, dy, l):
    tq = ATT_TQ_BWD

    def body(q_ref, k_ref, v_ref, z_ref, dy_ref, dq_ref, dk_ref, dv_ref, dz_ref):
        @pl.when(pl.program_id(1) == 0)
        def _():
            dk_ref[...] = jnp.zeros_like(dk_ref)
            dv_ref[...] = jnp.zeros_like(dv_ref)

        for g in range(T // tq):
            @pl.when(pl.program_id(1) == g)
            def _(g=g):
                n = tq * g
                past, k, v = _attn_operands(k_ref, v_ref, g, tq)
                _, vjp = jax.vjp(_attn_tile, q_ref[...], past, k, v, z_ref[...])
                dq, dpast, dk, dv, dz = vjp(dy_ref[...])
                dq_ref[...] = dq
                dz_ref[...] = dz.astype(BF16)
                dk_ref[n:n + tq, :] += dk
                dv_ref[n:n + tq, :] += dv
                if g:
                    dk_ref[0:n, :] += dpast[0]
                    dv_ref[0:n, :] += dpast[1]

    return pl.pallas_call(
        body, grid=(HEADS, T // tq),
        in_specs=_attn_in_specs(tq) + [pl.BlockSpec((tq, 128), lambda h, i: (i, h))],
        out_specs=[pl.BlockSpec((None, tq, QK), lambda h, i: (h, i, 0)), pl.BlockSpec((None, T, QK), lambda h, i: (h, 0, 0)),
                   pl.BlockSpec((None, T, 128), lambda h, i: (h, 0, 0)), pl.BlockSpec((tq, 128), lambda h, i: (i, h))],
        out_shape=[jax.ShapeDtypeStruct((HEADS, T, QK), F32), jax.ShapeDtypeStruct((HEADS, T, QK), F32),
                   jax.ShapeDtypeStruct((HEADS, T, 128), F32), jax.ShapeDtypeStruct((T, 1024), BF16)],
        name=f"attn_bwd_l{l}", compiler_params=_params(("arbitrary", "arbitrary")))(q, k, v, proj, dy)


LRU_TT = 256


def _lru_gates(xc, wa, wx, ba, bx, lam):
    r = _sigmoid(dot_nn(xc, wa) + ba)
    i = _sigmoid(dot_nn(xc, wx) + bx)
    sp = jnp.maximum(-lam, 0.0) + jnp.log1p(jnp.exp(-jnp.abs(lam)))
    log_a = -8.0 * r * sp
    a = jnp.exp(log_a)
    mult = jnp.sqrt(jnp.maximum(1.0 - jnp.exp(2.0 * log_a), 0.0))
    return a, mult * (i * xc)


def _shift_down(x, s, halo):
    n, c = x.shape
    r = pltpu.roll(x.reshape(n // 8, 8, c), s, 1)
    before = jnp.concatenate([pltpu.roll(halo, s, 0)[None], r[:-1]], axis=0)
    sub = lax.broadcasted_iota(jnp.int32, r.shape, 1)
    return jnp.where(sub >= s, r, before).reshape(n, c)


def _shift_up(x, s, halo):
    n, c = x.shape
    r = pltpu.roll(x.reshape(n // 8, 8, c), 8 - s, 1)
    after = jnp.concatenate([r[1:], pltpu.roll(halo, 8 - s, 0)[None]], axis=0)
    sub = lax.broadcasted_iota(jnp.int32, r.shape, 1)
    return jnp.where(sub < 8 - s, r, after).reshape(n, c)


def _conv(x, halo, w_ref, b):
    return (w_ref[3:4, :] * x + w_ref[2:3, :] * _shift_down(x, 1, halo) + w_ref[1:2, :] * _shift_down(x, 2, halo)
            + w_ref[0:1, :] * _shift_down(x, 3, halo) + b)


def _scan(a, b, reverse, carry):
    n, c = a.shape
    a, b = a.reshape(n // 8, 8, c), b.reshape(n // 8, 8, c)
    sub = lax.broadcasted_iota(jnp.int32, a.shape, 1)
    for d in (1, 2, 4):
        keep = sub < 8 - d if reverse else sub >= d
        shift = 8 - d if reverse else d
        a_sh = jnp.where(keep, pltpu.roll(a, shift, 1), 1.0)
        b_sh = jnp.where(keep, pltpu.roll(b, shift, 1), 0.0)
        b = a * b_sh + b
        a = a * a_sh
    a, b = a.reshape(n, c), b.reshape(n, c)
    groups = [None] * (n // 8)
    for g in (reversed(range(n // 8)) if reverse else range(n // 8)):
        h = a[8 * g:8 * g + 8] * carry + b[8 * g:8 * g + 8]
        groups[g] = h
        carry = h[0:1] if reverse else h[7:8]
    return jnp.concatenate(groups, axis=0), carry


def _lru_param_specs(l):
    ct = LRU_TILE
    vec = pl.BlockSpec((1, ct), lambda n, i: (0, n))
    mat = pl.BlockSpec((None, None, ct, ct), lambda n, i: (l, n, 0, 0))
    return [pl.BlockSpec((4, ct), lambda n, i: (0, n)), vec, mat, mat, vec, vec, vec]


def lru_fwd(proj, conv_w, conv_b, wa, wx, ba, bx, lam, l):
    tt, ct = LRU_TT, LRU_TILE

    def body(x_ref, z_ref, cw_ref, cb_ref, wa_ref, wx_ref, ba_ref, bx_ref, lam_ref, h_ref, y_ref, halo, hcar):
        @pl.when(pl.program_id(1) == 0)
        def _():
            halo[...] = jnp.zeros_like(halo)
            hcar[...] = jnp.zeros_like(hcar)

        x = x_ref[...]
        xc = _conv(x, halo[...], cw_ref, cb_ref[...])
        halo[...] = x[tt - 8:tt]
        a, b = _lru_gates(xc, wa_ref[...], wx_ref[...], ba_ref[...], bx_ref[...], lam_ref[...])
        h, hcar[...] = _scan(a, b, False, hcar[...])
        h_ref[...] = h
        y_ref[...] = h * _silu(z_ref[...])

    seq = pl.BlockSpec((tt, ct), lambda n, i: (i, n))
    return pl.pallas_call(
        body, grid=(LRU_W // ct, T // tt),
        in_specs=[pl.BlockSpec((tt, ct), lambda n, i: (i, OFF_XC // ct + n)),
                  pl.BlockSpec((tt, ct), lambda n, i: (i, OFF_ZC // ct + n))] + _lru_param_specs(l),
        out_specs=[seq, seq],
        out_shape=[jax.ShapeDtypeStruct((T, LRU_W), F32), jax.ShapeDtypeStruct((T, LRU_W), F32)],
        scratch_shapes=[pltpu.VMEM((8, ct), F32), pltpu.VMEM((1, ct), F32)],
        name=f"lru_fwd_l{l}", compiler_params=_params(("arbitrary", "arbitrary")))(
            proj, proj, conv_w, conv_b, wa, wx, ba, bx, lam)


def lru_bwd(proj, hseq, dy, conv_w, conv_b, wa, wx, ba, bx, lam, l):
    tt, ct = LRU_TT, LRU_TILE
    nt = T // tt
    rev = lambda i: nt - 1 - i
    prev8 = lambda i: jnp.maximum(rev(i) * (tt // 8) - 1, 0)

    def body(x_ref, xh_ref, z_ref, h_ref, hh_ref, dy_ref, cw_ref, cb_ref, wa_ref, wx_ref, ba_ref, bx_ref, lam_ref,
             dx_ref, dz_ref, dcw_ref, dcb_ref, dwa_ref, dwx_ref, dba_ref, dbx_ref, dlam_ref, gcar, dhalo):
        i = pl.program_id(1)
        first = i == 0

        @pl.when(first)
        def _():
            gcar[...] = jnp.zeros_like(gcar)
            dhalo[...] = jnp.zeros_like(dhalo)

        at_start = rev(i) == 0
        x = x_ref[...]
        xhalo = jnp.where(at_start, 0.0, xh_ref[...])
        sh = [x, _shift_down(x, 1, xhalo), _shift_down(x, 2, xhalo), _shift_down(x, 3, xhalo)]
        xc = (cw_ref[3:4, :] * sh[0] + cw_ref[2:3, :] * sh[1] + cw_ref[1:2, :] * sh[2] + cw_ref[0:1, :] * sh[3]
              + cb_ref[...])
        (a, b), vjp = jax.vjp(_lru_gates, xc, wa_ref[...], wx_ref[...], ba_ref[...], bx_ref[...], lam_ref[...])
        hs = h_ref[...]
        hprev = _shift_down(hs, 1, jnp.where(at_start, 0.0, hh_ref[...]))
        z = z_ref[...]
        sg = _sigmoid(z)
        dy = dy_ref[...]
        dz_ref[...] = (dy * hs * (sg * (1.0 + z * (1.0 - sg)))).astype(BF16)
        dh = dy * (z * sg)
        a_next = _shift_up(a, 1, jnp.ones((8, ct), F32))
        g, _ = _scan(a_next, dh, True, gcar[...])
        dxc, dwa, dwx, dba, dbx, dlam = vjp((g * hprev, g))
        dx = (cw_ref[3:4, :] * dxc + cw_ref[2:3, :] * _shift_up(dxc, 1, dhalo[...])
              + cw_ref[1:2, :] * _shift_up(dxc, 2, dhalo[...]) + cw_ref[0:1, :] * _shift_up(dxc, 3, dhalo[...]))
        dx_ref[...] = dx.astype(BF16)
        dhalo[...] = dxc[0:8]
        ag = a * g
        gcar[...] = ag[0:1]
        dcw = jnp.concatenate([jnp.sum(dxc * sh[3 - j], axis=0, keepdims=True) for j in range(4)], axis=0)
        _acc(dcw_ref, dcw, first)
        _acc(dcb_ref, jnp.sum(dxc, axis=0, keepdims=True), first)
        _acc(dwa_ref, dwa, first)
        _acc(dwx_ref, dwx, first)
        _acc(dba_ref, dba, first)
        _acc(dbx_ref, dbx, first)
        _acc(dlam_ref, dlam, first)

    xcol = OFF_XC // ct
    zcol = OFF_ZC // ct
    vec = pl.BlockSpec((1, ct), lambda n, i: (0, n))
    mat = pl.BlockSpec((None, ct, ct), lambda n, i: (n, 0, 0))
    seq = pl.BlockSpec((tt, ct), lambda n, i: (rev(i), n))
    return pl.pallas_call(
        body, grid=(LRU_W // ct, nt),
        in_specs=[pl.BlockSpec((tt, ct), lambda n, i: (rev(i), xcol + n)),
                  pl.BlockSpec((8, ct), lambda n, i: (prev8(i), xcol + n)),
                  pl.BlockSpec((tt, ct), lambda n, i: (rev(i), zcol + n)),
                  seq, pl.BlockSpec((8, ct), lambda n, i: (prev8(i), n)), seq] + _lru_param_specs(l),
        out_specs=[seq, seq, pl.BlockSpec((4, ct), lambda n, i: (0, n)), vec, mat, mat, vec, vec, vec],
        out_shape=[jax.ShapeDtypeStruct((T, LRU_W), BF16), jax.ShapeDtypeStruct((T, LRU_W), BF16),
                   jax.ShapeDtypeStruct((4, LRU_W), F32), jax.ShapeDtypeStruct((1, LRU_W), F32),
                   jax.ShapeDtypeStruct((2, ct, ct), F32), jax.ShapeDtypeStruct((2, ct, ct), F32),
                   jax.ShapeDtypeStruct((1, LRU_W), F32), jax.ShapeDtypeStruct((1, LRU_W), F32),
                   jax.ShapeDtypeStruct((1, LRU_W), F32)],
        scratch_shapes=[pltpu.VMEM((1, ct), F32), pltpu.VMEM((8, ct), F32)],
        name=f"lru_bwd_l{l}", compiler_params=_params(("arbitrary", "arbitrary")))(
            proj, proj, proj, hseq, hseq, dy, conv_w, conv_b, wa, wx, ba, bx, lam)


def proj_bwd(y, dp, w, l, tag, dep=None):
    tm = 512
    k = y.shape[1]
    deps = [] if dep is None else [dep]

    def body(y_ref, dp_ref, w_ref, *rest):
        dy_ref, dw_ref = rest[-2:]
        dp = dp_ref[...]
        dy_ref[...] = _dg(dp, _unpack(w_ref[...]), _NT)
        _acc(dw_ref, _dg(y_ref[...], dp, _TN), pl.program_id(0) == 0)

    return pl.pallas_call(
        body, grid=(T // tm,),
        in_specs=[pl.BlockSpec((tm, k), lambda i: (i, 0)), pl.BlockSpec((tm, D), lambda i: (i, 0)),
                  pl.BlockSpec((None, k, D // 2), lambda i: (0, 0, 0))] + [ANY] * len(deps),
        out_specs=[pl.BlockSpec((tm, k), lambda i: (i, 0)), pl.BlockSpec((None, k, D), lambda i: (0, 0, 0))],
        out_shape=[jax.ShapeDtypeStruct((T, k), F32), jax.ShapeDtypeStruct((1, k, D), F32)],
        name=f"proj_{tag}_bwd_l{l}", compiler_params=_params(("arbitrary",)))(y, dp, w, *deps)


OUT_TM = 256


def _out_tile(pa, pb, pc, ga, gb, gc, wout, post_g):
    merged = _sigmoid(ga) * pa + _sigmoid(gb) * pb + _sigmoid(gc) * pc
    return _rms(dot_nn(merged, wout), post_g)


def _out_in_specs():
    tm = OUT_TM
    tok = pl.BlockSpec((tm, D), lambda i: (i, 0))
    gate = lambda off: pl.BlockSpec((tm, 512), lambda i, off=off: (i, off // 512))
    return [tok, tok, tok, gate(OFF_GA), gate(OFF_GA + 512), gate(OFF_GB), gate(OFF_GB + 512), gate(OFF_GC),
            gate(OFF_GC + 512), pl.BlockSpec((None, D, D // 2), lambda i: (0, 0, 0)), pl.BlockSpec((1, D), lambda i: (0, 0))]


def _gates(refs):
    return [jnp.concatenate([refs[2 * j][...], refs[2 * j + 1][...]], axis=1) for j in range(3)]


def out_fwd(x, ya, yb, yc, proj, wpa, wpb, wpc, wout, post_g, l):
    tm = OUT_TM

    def body(ya_ref, yb_ref, yc_ref, g0, g1, g2, g3, g4, g5, wo_ref, pg_ref, x_ref, wa_ref, wb_ref, wc_ref,
             o_ref, pa_ref, pb_ref, pc_ref, wa, wb, wc, wo):
        @pl.when(pl.program_id(0) == 0)
        def _():
            for dst, src in ((wa, wa_ref), (wb, wb_ref), (wc, wc_ref), (wo, wo_ref)):
                dst[...] = _unpack(src[...]).astype(BF16)

        pa = _dg(ya_ref[...], wa[...], _NN)
        pb = _dg(yb_ref[...], wb[...], _NN)
        pc = _dg(yc_ref[...], wc[...], _NN)
        ga, gb, gc = _gates([g0, g1, g2, g3, g4, g5])
        o_ref[...] = x_ref[...] + _out_tile(pa, pb, pc, ga, gb, gc, wo[...], pg_ref[...])
        pa_ref[...] = pa.astype(BF16)
        pb_ref[...] = pb.astype(BF16)
        pc_ref[...] = pc.astype(BF16)

    tok = pl.BlockSpec((tm, D), lambda i: (i, 0))
    words = lambda k: pl.BlockSpec((None, k, D // 2), lambda i: (0, 0, 0))
    specs = _out_in_specs()
    specs[2] = pl.BlockSpec((tm, LRU_W), lambda i: (i, 0))
    return pl.pallas_call(
        body, grid=(T // tm,), in_specs=specs + [tok, words(D), words(D), words(LRU_W)], out_specs=[tok] * 4,
        out_shape=[jax.ShapeDtypeStruct((T, D), F32)] + [jax.ShapeDtypeStruct((T, D), BF16)] * 3,
        scratch_shapes=[pltpu.VMEM((D, D), BF16), pltpu.VMEM((D, D), BF16), pltpu.VMEM((LRU_W, D), BF16),
                        pltpu.VMEM((D, D), BF16)],
        name=f"out_fwd_l{l}", compiler_params=_params(("arbitrary",)))(
            ya, yb, yc, proj, proj, proj, proj, proj, proj, wout, post_g, x, wpa, wpb, wpc)


def out_bwd(pa, pb, pc, proj, wout, post_g, dxn, l, dep=None):
    tm = OUT_TM

    def body(pa_ref, pb_ref, pc_ref, g0, g1, g2, g3, g4, g5, w_ref, pg_ref, dxn_ref, *rest):
        dpa_ref, dpb_ref, dpc_ref, dg_ref, dw_ref, dpg_ref = rest[-6:]
        first = pl.program_id(0) == 0
        ga, gb, gc = _gates([g0, g1, g2, g3, g4, g5])
        _, vjp = jax.vjp(_out_tile, pa_ref[...], pb_ref[...], pc_ref[...], ga, gb, gc, _unpack(w_ref[...]), pg_ref[...])
        dpa, dpb, dpc, dga, dgb, dgc, dw, dpg = vjp(dxn_ref[...])
        dpa_ref[...] = dpa.astype(BF16)
        dpb_ref[...] = dpb.astype(BF16)
        dpc_ref[...] = dpc.astype(BF16)
        dg_ref[:, 0:1024] = dga.astype(BF16)
        dg_ref[:, 1024:2048] = dgb.astype(BF16)
        dg_ref[:, 2048:3072] = dgc.astype(BF16)
        _acc(dw_ref, dw, first)
        _acc(dpg_ref, dpg, first)

    tok = pl.BlockSpec((tm, D), lambda i: (i, 0))
    deps = [] if dep is None else [dep]
    return pl.pallas_call(
        body, grid=(T // tm,), in_specs=_out_in_specs() + [tok] + [ANY] * len(deps),
        out_specs=[tok, tok, tok, pl.BlockSpec((tm, 3072), lambda i: (i, 0)),
                   pl.BlockSpec((None, D, D), lambda i: (0, 0, 0)), pl.BlockSpec((1, D), lambda i: (0, 0))],
        out_shape=[jax.ShapeDtypeStruct((T, D), BF16)] * 3 + [jax.ShapeDtypeStruct((T, 3072), BF16),
                                                            jax.ShapeDtypeStruct((1, D, D), F32), jax.ShapeDtypeStruct((1, D), F32)],
        name=f"out_bwd_l{l}", compiler_params=_params(("arbitrary",)))(
            pa, pb, pc, proj, proj, proj, proj, proj, proj, wout, post_g, dxn, *deps)


def loss_head(y, target):
    tm = 256

    def body(y_ref, t_ref, loss_ref, dy_ref):
        e = y_ref[...] - t_ref[...]
        dy_ref[...] = e * (1.0 / D)
        val = 0.5 * jnp.sum(jnp.mean(e * e, axis=-1, keepdims=True), axis=0, keepdims=True)
        _acc(loss_ref, jnp.broadcast_to(val, (8, 128)), pl.program_id(0) == 0)

    tok = pl.BlockSpec((tm, D), lambda i: (i, 0))
    total, dy = pl.pallas_call(
        body, grid=(T // tm,), in_specs=[tok, tok],
        out_specs=[pl.BlockSpec((8, 128), lambda i: (0, 0)), tok],
        out_shape=[jax.ShapeDtypeStruct((8, 128), F32), jax.ShapeDtypeStruct((T, D), F32)],
        name="loss_head", compiler_params=_params(("arbitrary",)))(y, target)
    return total[0, 0], dy


def _rope_tables():
    pos = jnp.arange(T, dtype=F32)
    inv_freq = 10000.0 ** (-jnp.arange(0, 64, 2, dtype=F32) / 64)
    ang = pos[:, None] * inv_freq[None, :]
    cos, sin = jnp.cos(ang), jnp.sin(ang)
    ctab = jnp.concatenate([jnp.ones((T, 128), F32), cos, cos], axis=1)
    stab = jnp.concatenate([jnp.zeros((T, 128), F32), -sin, sin], axis=1)
    return ctab, stab


def _block_diag(w):
    w5 = w.reshape(L, 2, 8, 80, 80)
    eye = jnp.eye(8, dtype=w.dtype)
    return jnp.einsum("lnbij,bc->lnbicj", w5, eye).reshape(L, 2, LRU_TILE, LRU_TILE)


def _block_diag_t(dw):
    dw5 = dw.reshape(2, 8, 80, 8, 80)
    return jnp.einsum("nbicj,bc->nbij", dw5, jnp.eye(8, dtype=dw.dtype)).reshape(16, 80, 80)


def _layer_fwd(x, l, w, gw, tabs, dep=None, mid=None):
    row = lambda a: a[l][None]
    proj, h = inproj_fwd(x, row(w["pre_norm_g"]), gw["w_in_t"], l, dep)
    ya = gmlp_fwd(proj, row(w["gm_ln_g"]), row(w["gm_ln_b"]), w["gm_ws"][l], w["gm_bs"][l][..., None], l)
    dep2 = None
    if mid is not None:
        gw, dep2 = mid(ya)
    q, k, v = qkv_fwd(proj, row(w["mla_q_norm_g"]), row(w["kv_g384"]), gw["wq"], gw["wkv"], tabs[0], tabs[1], l, dep2)
    yb = attn_fwd(q, k, v, proj, l)
    hseq, yc = lru_fwd(proj, gw["conv"], row(w["lru_conv_b"]), w["wa_dense"], w["wx_dense"],
                       row(w["lru_b_a"]), row(w["lru_b_x"]), row(w["lru_lambda"]), l)
    xn, pa, pb, pc = out_fwd(x, ya, yb, yc, proj, gw["w_proj_a"], gw["w_proj_b"], gw["w_proj_c"], gw["w_out"],
                             row(w["post_norm_g"]), l)
    return xn, (x, proj, h, ya, q, k, v, yb, hseq, yc, pa, pb, pc)


def _layer_bwd(dxn, l, w, gw, tabs, saved, dep=None, early=None, mid=None):
    x, proj, h, ya, q, k, v, yb, hseq, yc, pa, pb, pc = saved
    row = lambda a: a[l][None]
    g, gg = {}, {}
    dpa, dpb, dpc, dgates, gg["w_out"], dpost = out_bwd(pa, pb, pc, proj, gw["w_out"], row(w["post_norm_g"]), dxn, l, dep)
    g["post_norm_g"] = dpost[0]
    dep1 = early(dgates) if early is not None else None
    dya, gg["w_proj_a"] = proj_bwd(ya, dpa, gw["w_proj_a"], l, "a", dep1)
    dyb, gg["w_proj_b"] = proj_bwd(yb, dpb, gw["w_proj_b"], l, "b")
    dyc, gg["w_proj_c"] = proj_bwd(yc, dpc, gw["w_proj_c"], l, "c")
    dseg_a, dln_g, dln_b, g["gm_ws"], dbs = gmlp_bwd(proj, row(w["gm_ln_g"]), row(w["gm_ln_b"]), w["gm_ws"][l],
                                                    w["gm_bs"][l][..., None], dya, l)
    g["gm_ln_g"], g["gm_ln_b"], g["gm_bs"] = dln_g[0], dln_b[0], dbs[..., 0]
    dq, dk, dv, dzb = attn_bwd(q, k, v, proj, dyb, l)
    dseg_q, dqg, dkvg, dwq, dwkv = qkv_bwd(proj, row(w["mla_q_norm_g"]), row(w["kv_g384"]), gw["wq"], gw["wkv"],
                                           tabs[0], tabs[1], dq, dk, dv, l)
    gg["wq"], gg["wkv"] = dwq.reshape(1, 1536, 384), dwkv.reshape(1, 2048, 256)
    g["mla_q_norm_g"], g["mla_kv_norm_g"] = dqg[0], dkvg[0, :256]
    dxc, dzc, dcw, dcb, dwa, dwx, dba, dbx, dlam = lru_bwd(
        proj, hseq, dyc, gw["conv"], row(w["lru_conv_b"]), w["wa_dense"], w["wx_dense"],
        row(w["lru_b_a"]), row(w["lru_b_x"]), row(w["lru_lambda"]), l)
    gg["conv"] = jnp.pad(dcw.T, ((0, 0), (0, 124)))[None]
    g["lru_conv_b"], g["lru_b_a"], g["lru_b_x"], g["lru_lambda"] = dcb[0], dba[0], dbx[0], dlam[0]
    g["lru_w_a"], g["lru_w_x"] = _block_diag_t(dwa), _block_diag_t(dwx)
    dproj = jnp.concatenate([dseg_a, dseg_q, dzb, jnp.zeros((T, PAD2), dzb.dtype), dxc, dzc, dgates], axis=1)
    dep2 = mid(gg, dproj) if mid is not None else None
    gg["w_in_t"], dh = inproj_bwd(dproj, h, gw["w_in_t"], l, dep2)
    dx, dpre = prenorm_bwd(x, row(w["pre_norm_g"]), dh, dxn, l)
    g["pre_norm_g"] = dpre[0]
    return dx, gg, g


MESH = pl.DeviceIdType.MESH
HBM = pl.BlockSpec(memory_space=pltpu.HBM)
SEM = pl.BlockSpec(memory_space=pltpu.SEMAPHORE)
EFFECT = pltpu.SideEffectType.DATAFLOW_SIDE_EFFECTING
FLIPS = ((1, 0), (0, 1), (1, 1))


def _win_off(k, s):
    g = SHARD * k + s
    return g + jnp.where(g >= PAD1_AT, PAD1, 0) + jnp.where(g >= PAD2_AT, PAD2, 0)


def _plain_off(rows):
    return lambda k, s: rows * k + s


class Spec:
    def __init__(self, rows, cols, full_rows, pieces=None, off=None, layers=1, packed=None):
        self.rows, self.cols, self.full_rows, self.layers = rows, cols, full_rows, layers
        self.pieces = pieces or ((0, rows),)
        self.off = off or _plain_off(rows)
        self.packed = cols % 256 == 0 if packed is None else packed
        self.wcols = cols // 2 if self.packed else cols

    def to_words(self, a):
        return _pack(a) if self.packed else a

    def from_words(self, p):
        return _unpack(p) if self.packed else p


def _pack(a):
    bits = lambda v: lax.bitcast_convert_type(v.astype(jnp.bfloat16).astype(F32), jnp.uint32)
    words = [(bits(a[:, g:g + 128]) >> 16) | (bits(a[:, g + 128:g + 256]) & jnp.uint32(0xFFFF0000))
             for g in range(0, a.shape[-1], 256)]
    return lax.bitcast_convert_type(jnp.concatenate(words, axis=-1) if len(words) > 1 else words[0], F32)


def _unpack(p):
    w = lax.bitcast_convert_type(p, jnp.uint32)
    lo = lax.bitcast_convert_type(w << 16, F32)
    hi = lax.bitcast_convert_type(w & jnp.uint32(0xFFFF0000), F32)
    return jnp.concatenate([h[:, g:g + 128] for g in range(0, p.shape[-1], 128) for h in (lo, hi)], axis=-1)


WEIGHT_SPECS = {
    "w_in_t": Spec(SHARD, D, NPAD, WIN_PIECES, _win_off),
    "wq": Spec(192, 384, 1536),
    "wkv": Spec(256, 256, 2048),
    "conv": Spec(160, 128, 1280),
    "w_proj_a": Spec(128, D, 1024),
    "w_proj_b": Spec(128, D, 1024),
    "w_proj_c": Spec(160, D, 1280),
    "w_out": Spec(128, D, 1024),
}
REP_ROWS = 72
REP_SPEC = Spec(REP_ROWS, D, REP_ROWS * NDEV, packed=False)


def _coords():
    return lax.axis_index("x"), lax.axis_index("y"), lax.axis_index("c")


def _rows(ref, start, n):
    if not isinstance(start, int):
        start = pl.multiple_of(start, 8)
    return ref.at[:, pl.ds(start, n), :]


def _col_tile(cols):
    return 256 if cols % 256 == 0 else cols


def _n_pieces(specs):
    return sum(len(sp.pieces) for sp in specs)


def pack_place(shard, sp, layer, tag, dep=None):
    gaps = ((PAD1_AT, PAD1), (PAD2_AT + PAD1, PAD2)) if sp.off is _win_off else ()
    npc = len(sp.pieces)
    deps = [] if dep is None else [dep]

    def body(s_ref, *rest):
        words_ref, full_ref, buf, zbuf, sem = rest[-5:]
        l = 0
        x, y, c = _coords()
        me = 4 * x + 2 * y + c
        words = sp.to_words(s_ref[...])
        words_ref[...] = words
        buf[...] = words
        copies = [pltpu.make_async_copy(buf.at[pl.ds(s, n), :],
                                        full_ref.at[l, pl.ds(pl.multiple_of(sp.off(me, s), 8), n), :], sem.at[i])
                  for i, (s, n) in enumerate(sp.pieces)]
        if gaps:
            zbuf[...] = jnp.zeros_like(zbuf)
            copies += [pltpu.make_async_copy(zbuf.at[pl.ds(0, n), :], full_ref.at[l, pl.ds(at, n), :], sem.at[npc + i])
                       for i, (at, n) in enumerate(gaps)]
        for cp in copies:
            cp.start()
        for cp in copies:
            cp.wait()

    return pl.pallas_call(
        body, grid=(1,), in_specs=[pl.BlockSpec((None, sp.rows, sp.cols), lambda i: (layer, 0, 0))] + [ANY] * len(deps),
        out_specs=[pl.BlockSpec((None, sp.rows, sp.wcols), lambda i: (0, 0, 0)), ANY],
        out_shape=[jax.ShapeDtypeStruct((sp.layers, sp.rows, sp.wcols), F32),
                   jax.ShapeDtypeStruct((sp.layers, sp.full_rows, sp.wcols), F32)],
        scratch_shapes=[pltpu.VMEM((sp.rows, sp.wcols), F32), pltpu.VMEM((PAD2 if gaps else 8, sp.wcols), F32),
                        pltpu.SemaphoreType.DMA((npc + len(gaps),))],
        name=f"pack_place_{tag}", compiler_params=_params(("arbitrary",)))(shard, *deps)


def _gather_copies(srcs, bufs, specs, ssem, rsem, landing):
    x, y, c = _coords()
    me = 4 * x + 2 * y + c
    targets = [(x, y, 1 - c)] + [(x ^ fx, y ^ fy, c) for fx, fy in FLIPS]
    copies = []
    p = 0
    for src, buf, sp in zip(srcs, bufs, specs):
        for s, n in sp.pieces:
            for t, (tx, ty, tc) in enumerate(targets):
                owner = 4 * tx + 2 * ty + tc if landing else me
                copies.append(pltpu.make_async_remote_copy(_rows(src, s, n), _rows(buf, sp.off(owner, s), n),
                                                           ssem.at[4 * p + t], rsem.at[4 * p + t],
                                                           device_id=(tx, ty, tc), device_id_type=MESH))
            p += 1
    return copies


def gather_send(words, fulls, specs, tag):
    ns, npc = len(specs), _n_pieces(specs)

    def body(*refs):
        srcs, bufs, sems = refs[:ns], refs[2 * ns:3 * ns], refs[3 * ns:]
        for cp in _gather_copies(srcs, bufs, specs, *sems, False):
            cp.start()
        for cp in _gather_copies(srcs, bufs, specs, *sems, False):
            cp.wait_send()
        for cp in _gather_copies(srcs, bufs, specs, *sems, True):
            cp.wait_recv()

    return pl.pallas_call(
        body, in_specs=[ANY] * (2 * ns), out_specs=[ANY] * ns,
        out_shape=[jax.ShapeDtypeStruct(f.shape, f.dtype) for f in fulls],
        input_output_aliases={ns + i: i for i in range(ns)},
        scratch_shapes=[pltpu.SemaphoreType.DMA((4 * npc,)), pltpu.SemaphoreType.DMA((4 * npc,))],
        name=f"gather_send_{tag}", compiler_params=pltpu.CompilerParams(has_side_effects=True))(*words, *fulls)


def _in_hbm(arrays):
    return [pltpu.with_memory_space_constraint(a, pltpu.HBM) for a in arrays]


def gather_start(words, fulls, specs, dep, tag):
    ns, npc = len(specs), _n_pieces(specs)
    deps = [] if dep is None else [dep]

    def body(*refs):
        ssem, rsem = refs[2 * ns + len(deps):2 * ns + len(deps) + 2]
        for cp in _gather_copies(refs[:ns], refs[ns:2 * ns], specs, ssem, rsem, False):
            cp.start()
        refs[-1][...] = jnp.zeros_like(refs[-1])

    outs = pl.pallas_call(
        body, in_specs=[HBM] * (2 * ns) + [ANY] * len(deps),
        out_specs=[SEM, SEM] + [HBM] * (2 * ns) + [pl.BlockSpec(memory_space=pltpu.VMEM)],
        out_shape=[pltpu.SemaphoreType.DMA((4 * npc,)), pltpu.SemaphoreType.DMA((4 * npc,))]
        + [pltpu.HBM(a.shape, a.dtype) for a in list(words) + list(fulls)] + [jax.ShapeDtypeStruct((8, 128), F32)],
        input_output_aliases={i: 2 + i for i in range(2 * ns)},
        name=f"gather_start_{tag}", compiler_params=pltpu.CompilerParams(has_side_effects=EFFECT))(
            *_in_hbm(list(words) + list(fulls)), *deps)
    return outs[0], outs[1], outs[2:2 + ns], outs[2 + ns:2 + 2 * ns], outs[-1]


def gather_wait(ssem, rsem, words, fulls, specs, after, tag):
    ns = len(specs)

    def body(*refs):
        srcs, bufs, ssem, rsem = refs[:ns], refs[ns:2 * ns], refs[2 * ns], refs[2 * ns + 1]
        for cp in _gather_copies(srcs, bufs, specs, ssem, rsem, False):
            cp.wait_send()
        for cp in _gather_copies(srcs, bufs, specs, ssem, rsem, True):
            cp.wait_recv()

    outs = pl.pallas_call(
        body, in_specs=[HBM] * (2 * ns) + [SEM, SEM, ANY], out_specs=[HBM] * (2 * ns),
        out_shape=[pltpu.HBM(a.shape, a.dtype) for a in list(words) + list(fulls)],
        input_output_aliases={i: i for i in range(2 * ns)},
        name=f"gather_wait_{tag}", compiler_params=pltpu.CompilerParams(has_side_effects=EFFECT))(
            *words, *fulls, ssem, rsem, after)
    return outs[ns:]


def gather_forward(fulls, specs, tag):
    ns, npc = len(specs), _n_pieces(specs)

    def body(*refs):
        bufs = refs[ns:2 * ns]
        ssem, rsem = refs[2 * ns:]
        x, y, c = _coords()
        sibling = (x, y, 1 - c)
        waits = []
        p = 0
        for buf, sp in zip(bufs, specs):
            for s, n in sp.pieces:
                for t, (fx, fy) in enumerate(FLIPS):
                    chip = 4 * (x ^ fx) + 2 * (y ^ fy)
                    here = _rows(buf, sp.off(chip + c, s), n)
                    send = pltpu.make_async_remote_copy(here, here, ssem.at[t, p], rsem.at[t, p],
                                                        device_id=sibling, device_id_type=MESH)
                    send.start()
                    waits.append(send.wait_send)
                    there = _rows(buf, sp.off(chip + 1 - c, s), n)
                    waits.append(pltpu.make_async_remote_copy(here, there, ssem.at[t, p], rsem.at[t, p],
                                                              device_id=sibling, device_id_type=MESH).wait_recv)
                p += 1
        for w in waits:
            w()

    return pl.pallas_call(
        body, in_specs=[ANY] * ns, out_specs=[ANY] * ns,
        out_shape=[jax.ShapeDtypeStruct(f.shape, f.dtype) for f in fulls],
        input_output_aliases={i: i for i in range(ns)},
        scratch_shapes=[pltpu.SemaphoreType.DMA((3, npc)), pltpu.SemaphoreType.DMA((3, npc))],
        name=f"gather_forward_{tag}", compiler_params=pltpu.CompilerParams(has_side_effects=True))(*fulls)


def all_gather(shards, layer, specs, names, tag):
    placed = [pack_place(s, sp, layer, f"{tag}_{n}") for s, sp, n in zip(shards, specs, names)]
    fulls = gather_send([p[0] for p in placed], [p[1] for p in placed], specs, tag)
    return gather_forward(fulls, specs, tag)


def _pair_copies(srcs, theirs, specs, ssem, rsem):
    x, y, c = _coords()
    copies = []
    p = 0
    for src, their, sp in zip(srcs, theirs, specs):
        for s, n in sp.pieces:
            for j in range(4):
                copies.append(pltpu.make_async_remote_copy(_rows(src, sp.off(2 * j + 1 - c, s), n), _rows(their.at[j], s, n),
                                                           ssem.at[4 * p + j], rsem.at[4 * p + j],
                                                           device_id=(x, y, 1 - c), device_id_type=MESH))
            p += 1
    return copies


def _pair_shapes(specs):
    return [(4, sp.layers, sp.rows, sp.cols) for sp in specs]


def reduce_pair(grads, specs, tag, dep=None):
    ns, npc = len(specs), _n_pieces(specs)
    deps = [] if dep is None else [dep]

    def body(*refs):
        copies = _pair_copies(refs[:ns], refs[ns + len(deps):2 * ns + len(deps)], specs, *refs[2 * ns + len(deps):])
        for cp in copies:
            cp.start()
        for cp in copies:
            cp.wait()

    return pl.pallas_call(
        body, in_specs=[ANY] * (ns + len(deps)), out_specs=[ANY] * ns,
        out_shape=[jax.ShapeDtypeStruct(s, F32) for s in _pair_shapes(specs)],
        scratch_shapes=[pltpu.SemaphoreType.DMA((4 * npc,)), pltpu.SemaphoreType.DMA((4 * npc,))],
        name=f"reduce_pair_{tag}", compiler_params=pltpu.CompilerParams(has_side_effects=True))(*grads, *deps)


def pair_start(grads, specs, dep, tag):
    ns, npc = len(specs), _n_pieces(specs)
    slots = [lax.empty(s, F32) for s in _pair_shapes(specs)]
    deps = [] if dep is None else [dep]

    def body(*refs):
        ssem, rsem = refs[2 * ns + len(deps):2 * ns + len(deps) + 2]
        for cp in _pair_copies(refs[:ns], refs[ns:2 * ns], specs, ssem, rsem):
            cp.start()
        refs[-1][...] = jnp.zeros_like(refs[-1])

    outs = pl.pallas_call(
        body, in_specs=[HBM] * (2 * ns) + [ANY] * len(deps),
        out_specs=[SEM, SEM] + [HBM] * (2 * ns) + [pl.BlockSpec(memory_space=pltpu.VMEM)],
        out_shape=[pltpu.SemaphoreType.DMA((4 * npc,)), pltpu.SemaphoreType.DMA((4 * npc,))]
        + [pltpu.HBM(a.shape, a.dtype) for a in list(grads) + slots] + [jax.ShapeDtypeStruct((8, 128), F32)],
        input_output_aliases={i: 2 + i for i in range(2 * ns)},
        name=f"pair_start_{tag}", compiler_params=pltpu.CompilerParams(has_side_effects=EFFECT))(
            *_in_hbm(list(grads) + slots), *deps)
    return outs[0], outs[1], outs[2:2 + ns], outs[2 + ns:2 + 2 * ns], outs[-1]


def pair_wait(ssem, rsem, grads, slots, specs, after, tag):
    ns = len(specs)

    def body(*refs):
        for cp in _pair_copies(refs[:ns], refs[ns:2 * ns], specs, refs[2 * ns], refs[2 * ns + 1]):
            cp.wait_send()
            cp.wait_recv()

    outs = pl.pallas_call(
        body, in_specs=[HBM] * (2 * ns) + [SEM, SEM, ANY], out_specs=[HBM] * (2 * ns),
        out_shape=[pltpu.HBM(a.shape, a.dtype) for a in list(grads) + list(slots)],
        input_output_aliases={i: i for i in range(2 * ns)},
        name=f"pair_wait_{tag}", compiler_params=pltpu.CompilerParams(has_side_effects=EFFECT))(
            *grads, *slots, ssem, rsem, after)
    return outs[:ns], outs[ns:]


def pair_sum(g, r1, sp, tag):
    npc = len(sp.pieces)
    fetch_all = 4 * sp.rows * sp.cols * 4 <= (8 << 20)

    def body(g_ref, r_ref, own_ref, words_ref, gbuf, sem):
        l, j = pl.program_id(0), pl.program_id(1)
        x, y, c = _coords()

        def fetch(chip, slot):
            copies = [pltpu.make_async_copy(g_ref.at[l, pl.ds(pl.multiple_of(sp.off(2 * chip + c, s), 8), n), :],
                                            gbuf.at[slot, pl.ds(s, n), :], sem.at[slot, i])
                      for i, (s, n) in enumerate(sp.pieces)]
            for cp in copies:
                cp.start()
            return copies

        if fetch_all:
            @pl.when(j == 0)
            def _():
                for cp in [cp for chip in range(4) for cp in fetch(chip, chip)]:
                    cp.wait()

            mine = gbuf[j]
        else:
            for cp in fetch(j, 0):
                cp.wait()
            mine = gbuf[0]
        p = mine + r_ref[...]
        words_ref[...] = sp.to_words(p)

        @pl.when(j == 2 * x + y)
        def _():
            own_ref[...] = p

    return pl.pallas_call(
        body, grid=(sp.layers, 4),
        in_specs=[ANY, pl.BlockSpec((None, None, sp.rows, sp.cols), lambda l, j: (j, l, 0, 0))],
        out_specs=[pl.BlockSpec((None, sp.rows, sp.cols), lambda l, j: (l, 0, 0)),
                   pl.BlockSpec((None, None, sp.rows, sp.wcols), lambda l, j: (j, l, 0, 0))],
        out_shape=[jax.ShapeDtypeStruct((sp.layers, sp.rows, sp.cols), F32),
                   jax.ShapeDtypeStruct((4, sp.layers, sp.rows, sp.wcols), F32)],
        scratch_shapes=[pltpu.VMEM((4 if fetch_all else 1, sp.rows, sp.cols), F32), pltpu.SemaphoreType.DMA((4, npc))],
        name=f"pair_sum_{tag}", compiler_params=_params(("arbitrary", "arbitrary")))(g, r1)


def _chip_copies(srcs, dsts, ssem, rsem):
    x, y, c = _coords()
    copies = []
    for i, (src, dst) in enumerate(zip(srcs, dsts)):
        for t, (fx, fy) in enumerate(FLIPS):
            tx, ty = x ^ fx, y ^ fy
            copies.append(pltpu.make_async_remote_copy(src.at[2 * tx + ty], dst.at[t], ssem.at[3 * i + t], rsem.at[3 * i + t],
                                                       device_id=(tx, ty, c), device_id_type=MESH))
    return copies


def _slot_shapes(words):
    return [(3,) + w.shape[1:] for w in words]


def reduce_chips(words, specs, tag):
    ns = len(specs)

    def body(*refs):
        copies = _chip_copies(refs[:ns], refs[ns:2 * ns], *refs[2 * ns:])
        for cp in copies:
            cp.start()
        for cp in copies:
            cp.wait()

    return pl.pallas_call(
        body, in_specs=[ANY] * ns, out_specs=[ANY] * ns,
        out_shape=[jax.ShapeDtypeStruct(s, F32) for s in _slot_shapes(words)],
        scratch_shapes=[pltpu.SemaphoreType.DMA((3 * ns,)), pltpu.SemaphoreType.DMA((3 * ns,))],
        name=f"reduce_chips_{tag}", compiler_params=pltpu.CompilerParams(has_side_effects=True))(*words)


def chips_start(words, specs, tag):
    ns = len(specs)
    slots = [lax.empty(s, F32) for s in _slot_shapes(words)]

    def body(*refs):
        ssem, rsem = refs[2 * ns:2 * ns + 2]
        for cp in _chip_copies(refs[:ns], refs[ns:2 * ns], ssem, rsem):
            cp.start()
        refs[-1][...] = jnp.zeros_like(refs[-1])

    outs = pl.pallas_call(
        body, in_specs=[HBM] * (2 * ns),
        out_specs=[SEM, SEM] + [HBM] * (2 * ns) + [pl.BlockSpec(memory_space=pltpu.VMEM)],
        out_shape=[pltpu.SemaphoreType.DMA((3 * ns,)), pltpu.SemaphoreType.DMA((3 * ns,))]
        + [pltpu.HBM(a.shape, a.dtype) for a in list(words) + slots] + [jax.ShapeDtypeStruct((8, 128), F32)],
        input_output_aliases={i: 2 + i for i in range(2 * ns)},
        name=f"chips_start_{tag}", compiler_params=pltpu.CompilerParams(has_side_effects=EFFECT))(
            *_in_hbm(list(words) + slots))
    return outs[0], outs[1], outs[2:2 + ns], outs[2 + ns:2 + 2 * ns], outs[-1]


def chips_wait(ssem, rsem, words, slots, specs, after, tag):
    ns = len(specs)

    def body(*refs):
        for cp in _chip_copies(refs[:ns], refs[ns:2 * ns], refs[2 * ns], refs[2 * ns + 1]):
            cp.wait_send()
            cp.wait_recv()

    outs = pl.pallas_call(
        body, in_specs=[HBM] * (2 * ns) + [SEM, SEM, ANY], out_specs=[HBM] * (2 * ns),
        out_shape=[pltpu.HBM(a.shape, a.dtype) for a in list(words) + list(slots)],
        input_output_aliases={i: i for i in range(2 * ns)},
        name=f"chips_wait_{tag}", compiler_params=pltpu.CompilerParams(has_side_effects=EFFECT))(
            *words, *slots, ssem, rsem, after)
    return outs[ns:]


def sum_chips(own, r2, sp, tag):
    def body(own_ref, r_ref, o_ref):
        o_ref[...] = ((own_ref[...] + sp.from_words(r_ref[0])) + sp.from_words(r_ref[1])) + sp.from_words(r_ref[2])

    blk = pl.BlockSpec((None, sp.rows, sp.cols), lambda l: (l, 0, 0))
    return pl.pallas_call(
        body, grid=(sp.layers,), in_specs=[blk, pl.BlockSpec((3, None, sp.rows, sp.wcols), lambda l: (0, l, 0, 0))],
        out_specs=blk, out_shape=jax.ShapeDtypeStruct((sp.layers, sp.rows, sp.cols), F32),
        name=f"sum_chips_{tag}", compiler_params=_params(("arbitrary",)))(own, r2)


def reduce_scatter_start(grads, specs, names, dep, tag):
    theirs = reduce_pair(grads, specs, tag, dep)
    sums = [pair_sum(g, r1, sp, f"{tag}_{n}") for g, r1, sp, n in zip(grads, theirs, specs, names)]
    ssem, rsem, words, slots, token = chips_start([s[1] for s in sums], specs, tag)
    return (ssem, rsem, words, slots, [s[0] for s in sums]), token


def reduce_scatter_finish(state, after, specs, tag):
    ssem, rsem, words, slots, own = state
    return list(zip(own, chips_wait(ssem, rsem, words, slots, specs, after, tag)))


def reduce_scatter(grads, specs, names, tag):
    theirs = reduce_pair(grads, specs, tag)
    sums = [pair_sum(g, r1, sp, f"{tag}_{n}") for g, r1, sp, n in zip(grads, theirs, specs, names)]
    return list(zip([s[0] for s in sums], reduce_chips([s[1] for s in sums], specs, tag)))


def _adamw_math(w, g, m, v):
    c1 = 1.0 - ADAM_B1 ** ADAM_STEP
    c2 = 1.0 - ADAM_B2 ** ADAM_STEP
    m2 = ADAM_B1 * m + (1.0 - ADAM_B1) * g
    v2 = ADAM_B2 * v + (1.0 - ADAM_B2) * (g * g)
    return -ADAM_LR * ((m2 / c1) / (jnp.sqrt(v2 / c2) + ADAM_EPS) + ADAM_WD * w), m2, v2


def adamw(w, g, m, v, name):
    shape = w.shape
    cols = shape[-1]
    rows = math.prod(shape[:-1])
    tr = rows
    while tr * cols * 4 > (1 << 20) and tr % 16 == 0:
        tr //= 2

    def body(w_ref, g_ref, m_ref, v_ref, d_ref, nm_ref, nv_ref):
        d_ref[...], nm_ref[...], nv_ref[...] = _adamw_math(w_ref[...], g_ref[...], m_ref[...], v_ref[...])

    blk = pl.BlockSpec((tr, cols), lambda i: (i, 0))
    outs = pl.pallas_call(
        body, grid=(rows // tr,), in_specs=[blk] * 4, out_specs=[blk] * 3,
        out_shape=[jax.ShapeDtypeStruct((rows, cols), F32)] * 3,
        name=f"adamw_{name}", compiler_params=_params(("arbitrary",)))(
            *[a.reshape(rows, cols) for a in (w, g, m, v)])
    return [o.reshape(shape) for o in outs]


def adamw_layer(w, sums, m, v, sp, l, prev, dep, name):
    _, rows, cols = w.shape
    tc = _col_tile(cols)
    twc = tc // 2 if sp.packed else tc
    extra = ([] if prev is None else list(prev)) + ([] if dep is None else [dep])

    def body(w_ref, own_ref, r_ref, m_ref, v_ref, *rest):
        g_ref, d_ref, nm_ref, nv_ref = rest[-4:]
        g = ((own_ref[...] + sp.from_words(r_ref[0])) + sp.from_words(r_ref[1])) + sp.from_words(r_ref[2])
        g_ref[...] = g
        d_ref[...], nm_ref[...], nv_ref[...] = _adamw_math(w_ref[...], g, m_ref[...], v_ref[...])

    blk = pl.BlockSpec((None, rows, tc), lambda n: (l, 0, n))
    return pl.pallas_call(
        body, grid=(cols // tc,),
        in_specs=[blk, pl.BlockSpec((None, rows, tc), lambda n: (0, 0, n)),
                  pl.BlockSpec((3, None, rows, twc), lambda n: (0, 0, 0, n)), blk, blk] + [ANY] * len(extra),
        out_specs=[blk] * 4, out_shape=[jax.ShapeDtypeStruct(w.shape, F32)] * 4,
        input_output_aliases={} if prev is None else {5 + i: i for i in range(4)},
        name=f"adamw_{name}_l{l}", compiler_params=_params(("arbitrary",)))(w, sums[0], sums[1], m, v, *extra)


WEIGHTS = ("pre_norm_g", "w_in", "gm_ln_g", "gm_ln_b", "gm_ws", "gm_bs", "mla_q_norm_g", "mla_w_uq", "mla_kv_norm_g",
           "mla_w_ukv", "lru_conv_w", "lru_conv_b", "lru_w_a", "lru_b_a", "lru_w_x", "lru_b_x", "lru_lambda",
           "w_proj_a", "w_proj_b", "w_proj_c", "w_out", "post_norm_g")
SHARDED = ("w_in", "mla_w_uq", "mla_w_ukv", "lru_conv_w", "w_proj_a", "w_proj_b", "w_proj_c", "w_out")
REPLICATED = tuple(n for n in WEIGHTS if n not in SHARDED)


def _step(x, target, wts, ms, vs):
    t12 = lambda a: jnp.swapaxes(a, 1, 2)
    names = list(WEIGHT_SPECS)
    specs = [WEIGHT_SPECS[n] for n in names]
    tabs = _rope_tables()
    own = {"w_in_t": t12(wts["w_in"]), "wq": t12(wts["mla_w_uq"]), "wkv": t12(wts["mla_w_ukv"]),
           "conv": jnp.pad(t12(wts["lru_conv_w"]), ((0, 0), (0, 0), (0, 124))),
           "w_proj_a": wts["w_proj_a"], "w_proj_b": wts["w_proj_b"], "w_proj_c": wts["w_proj_c"], "w_out": wts["w_out"]}
    first, rest = ["w_in_t"], [n for n in names if n != "w_in_t"]
    sfirst, srest = [WEIGHT_SPECS[n] for n in first], [WEIGHT_SPECS[n] for n in rest]

    w = {n: wts[n] for n in REPLICATED}
    w["kv_g384"] = jnp.concatenate([wts["mla_kv_norm_g"], jnp.ones((L, 128), F32)], axis=1)
    w["wa_dense"] = _block_diag(wts["lru_w_a"])
    w["wx_dense"] = _block_diag(wts["lru_w_x"])

    def layer_weights(ns, words):
        gw = dict(zip(ns, words))
        gw["wq"] = gw["wq"].reshape(HEADS, 192, 384)
        gw["wkv"] = gw["wkv"].reshape(HEADS, 256, 128)
        gw["conv"] = gw["conv"][0, :, :4].T
        return gw

    place = lambda l, dep: {n: pack_place(own[n], WEIGHT_SPECS[n], l, f"w{l}_{n}", dep) for n in names}
    placed = [place(0, None)]
    words_of = lambda l, ns: [placed[l][n][0] for n in ns]
    bufs_of = lambda l, ns: [placed[l][n][1] for n in ns]
    later = {}

    ssem_a, rsem_a, wthru_a, fthru_a, token_a = gather_start(words_of(0, first), bufs_of(0, first), sfirst, None, "w0a")
    placed.append(place(1, token_a))
    win0 = gather_forward(gather_wait(ssem_a, rsem_a, wthru_a, fthru_a, sfirst, placed[1]["w_in_t"][0], "w0a"), sfirst, "w0a")
    ssem_b, rsem_b, wthru_b, fthru_b, token_b = gather_start(words_of(0, rest), bufs_of(0, rest), srest, win0[0], "w0b")

    def fwd0_mid(ya):
        rest0 = gather_forward(gather_wait(ssem_b, rsem_b, wthru_b, fthru_b, srest, ya, "w0b"), srest, "w0b")
        later["w1"] = gather_start(words_of(1, names), bufs_of(1, names), specs, rest0[0], "w1")
        later["gw0"] = layer_weights(first + rest, list(win0) + list(rest0))
        return later["gw0"], later["w1"][4]

    x1, saved0 = _layer_fwd(x, 0, w, {"w_in_t": win0[0]}, tabs, dep=token_b, mid=fwd0_mid)
    ssem1, rsem1, wthru1, fthru1, _ = later["w1"]
    words1 = gather_forward(gather_wait(ssem1, rsem1, wthru1, fthru1, specs, x1, "w1"), specs, "w1")
    gw0, gw1 = later["gw0"], layer_weights(names, words1)
    x2, saved1 = _layer_fwd(x1, 1, w, gw1, tabs)
    loss, dx2 = loss_head(x2, target)

    def bwd1_mid(gg, last):
        later["p1b"] = pair_start([gg[n] for n in rest], srest, last, "g1b")
        return later["p1b"][4]

    dx1, gg1, g1 = _layer_bwd(dx2, 1, w, gw1, tabs, saved1, mid=bwd1_mid)
    grads1b, theirs1b = pair_wait(*later["p1b"][:4], srest, dx1, "g1b")
    p1a = pair_start([gg1["w_in_t"]], sfirst, theirs1b[0], "g1a")

    def bwd0_early(last):
        grads1a, theirs1a = pair_wait(*p1a[:4], sfirst, last, "g1a")
        mine = dict(zip(first + rest, list(grads1a) + list(grads1b)))
        theirs = dict(zip(first + rest, list(theirs1a) + list(theirs1b)))
        sums = [pair_sum(mine[n], theirs[n], WEIGHT_SPECS[n], f"g1_{n}") for n in names]
        ssem, rsem, words, slots, token = chips_start([s[1] for s in sums], specs, "g1")
        later["g1"] = (ssem, rsem, words, slots, [s[0] for s in sums])
        return token

    def bwd0_mid(gg, last):
        later["g0b"], token = reduce_scatter_start([gg[n] for n in rest], srest, rest, last, "g0b")
        return token

    dx0, gg0, g0 = _layer_bwd(dx1, 0, w, gw0, tabs, saved0, dep=p1a[4], early=bwd0_early, mid=bwd0_mid)
    s1 = dict(zip(names, reduce_scatter_finish(later["g1"], dx0, specs, "g1")))
    s0 = dict(zip(rest, reduce_scatter_finish(later["g0b"], dx0, srest, "g0b")))
    rep_flat = jnp.concatenate([jnp.stack([g0[n], g1[n]]).reshape(-1) for n in REPLICATED])
    rep_flat = jnp.pad(rep_flat, (0, REP_ROWS * NDEV * D - rep_flat.shape[0])).reshape(1, REP_ROWS * NDEV, D)
    state_a, token_g = reduce_scatter_start([gg0["w_in_t"], rep_flat], sfirst + [REP_SPEC], first + ["rep"], None, "g0a")

    keys = {"w_in": "w_in_t", "mla_w_uq": "wq", "mla_w_ukv": "wkv",
            "w_proj_a": "w_proj_a", "w_proj_b": "w_proj_b", "w_proj_c": "w_proj_c", "w_out": "w_out"}
    transposed = ("w_in", "mla_w_uq", "mla_w_ukv")
    state_of = lambda n: [own[keys[n]], t12(ms[n]), t12(vs[n])] if n in transposed else [wts[n], ms[n], vs[n]]

    def update(n, l, sums, prev, dep):
        wl, ml, vl = state_of(n)
        return adamw_layer(wl, sums[keys[n]], ml, vl, WEIGHT_SPECS[keys[n]], l, prev, dep, n)

    upd = {n: update(n, 1, s1, None, token_g) for n in keys}
    for n in keys:
        if n != "w_in":
            upd[n] = update(n, 0, s0, upd[n], None)
    s0["w_in_t"], rep_parts = reduce_scatter_finish(state_a, upd["w_out"][0], sfirst + [REP_SPEC], "g0a")
    upd["w_in"] = update("w_in", 0, s0, upd["w_in"], None)
    rep_sum = sum_chips(*rep_parts, REP_SPEC, "rep")
    rep_full = all_gather([rep_sum], 0, [REP_SPEC], ["rep"], "rep")[0].reshape(-1)

    out = {n: [t12(r) for r in upd[n]] if n in transposed else upd[n] for n in keys}
    conv_sp = WEIGHT_SPECS["conv"]
    g_conv = t12(jnp.concatenate([sum_chips(*s0["conv"], conv_sp, "conv0"), sum_chips(*s1["conv"], conv_sp, "conv1")])[:, :, :4])
    out["lru_conv_w"] = [g_conv] + adamw(wts["lru_conv_w"], g_conv, ms["lru_conv_w"], vs["lru_conv_w"], "lru_conv_w")
    at = 0
    for n in REPLICATED:
        size = math.prod(wts[n].shape)
        g = rep_full[at:at + size].reshape(wts[n].shape)
        out[n] = [g] + adamw(wts[n], g, ms[n], vs[n], n)
        at += size

    loss = lax.psum(loss, ("x", "y", "c"))
    return (loss, dx0[None], *[out[n][k] for k in range(4) for n in WEIGHTS])


def kernel(x, pre_norm_g, w_in, gm_ln_g, gm_ln_b, gm_ws, gm_bs, mla_q_norm_g, mla_w_uq, mla_kv_norm_g, mla_w_ukv, lru_conv_w, lru_conv_b, lru_w_a, lru_b_a, lru_w_x, lru_b_x, lru_lambda, w_proj_a, w_proj_b, w_proj_c, w_out, post_norm_g, loss_target, m_pre_norm_g, m_w_in, m_gm_ln_g, m_gm_ln_b, m_gm_ws, m_gm_bs, m_mla_q_norm_g, m_mla_w_uq, m_mla_kv_norm_g, m_mla_w_ukv, m_lru_conv_w, m_lru_conv_b, m_lru_w_a, m_lru_b_a, m_lru_w_x, m_lru_b_x, m_lru_lambda, m_w_proj_a, m_w_proj_b, m_w_proj_c, m_w_out, m_post_norm_g, v_pre_norm_g, v_w_in, v_gm_ln_g, v_gm_ln_b, v_gm_ws, v_gm_bs, v_mla_q_norm_g, v_mla_w_uq, v_mla_kv_norm_g, v_mla_w_ukv, v_lru_conv_w, v_lru_conv_b, v_lru_w_a, v_lru_b_a, v_lru_w_x, v_lru_b_x, v_lru_lambda, v_w_proj_a, v_w_proj_b, v_w_proj_c, v_w_out, v_post_norm_g):
    wts = dict(zip(WEIGHTS, (pre_norm_g, w_in, gm_ln_g, gm_ln_b, gm_ws, gm_bs, mla_q_norm_g, mla_w_uq, mla_kv_norm_g,
                             mla_w_ukv, lru_conv_w, lru_conv_b, lru_w_a, lru_b_a, lru_w_x, lru_b_x, lru_lambda,
                             w_proj_a, w_proj_b, w_proj_c, w_out, post_norm_g)))
    ms = dict(zip(WEIGHTS, (m_pre_norm_g, m_w_in, m_gm_ln_g, m_gm_ln_b, m_gm_ws, m_gm_bs, m_mla_q_norm_g, m_mla_w_uq,
                            m_mla_kv_norm_g, m_mla_w_ukv, m_lru_conv_w, m_lru_conv_b, m_lru_w_a, m_lru_b_a, m_lru_w_x,
                            m_lru_b_x, m_lru_lambda, m_w_proj_a, m_w_proj_b, m_w_proj_c, m_w_out, m_post_norm_g)))
    vs = dict(zip(WEIGHTS, (v_pre_norm_g, v_w_in, v_gm_ln_g, v_gm_ln_b, v_gm_ws, v_gm_bs, v_mla_q_norm_g, v_mla_w_uq,
                            v_mla_kv_norm_g, v_mla_w_ukv, v_lru_conv_w, v_lru_conv_b, v_lru_w_a, v_lru_b_a, v_lru_w_x,
                            v_lru_b_x, v_lru_lambda, v_w_proj_a, v_w_proj_b, v_w_proj_c, v_w_out, v_post_norm_g)))
    return _step(x[0], loss_target[0], wts, ms, vs)
```

```python
import functools
import math

import jax
import jax.numpy as jnp
from jax import lax
from jax.experimental import pallas as pl
from jax.experimental.pallas import tpu as pltpu

F32 = jnp.float32
BF16 = jnp.bfloat16

T = 2048
D = 1024
L = 2
NDEV = 8
EPS = 1e-6
CHUNK_SHIFT = 6
HEADS = 8
QK = 192
LRU_W = 1280
LRU_TILE = 640
N_IN = 10432
SHARD = N_IN // NDEV
OFF_U, OFF_V, OFF_ZA, OFF_CQ, OFF_CKV, OFF_ZB = 0, 1024, 2048, 3072, 3456, 3840
OFF_XC, OFF_ZC, OFF_GA, OFF_GB, OFF_GC = 5120, 6400, 7680, 8704, 9728
NPAD = 10752
PAD1_AT, PAD1 = 3776, 64
PAD2_AT, PAD2 = 4800, 256
WIN_PIECES = ((0, 888), (888, 280), (1168, 136))
VMEM_LIMIT = 60 * 1024 * 1024

ADAM_LR, ADAM_B1, ADAM_B2, ADAM_EPS, ADAM_WD, ADAM_STEP = 0.001, 0.9, 0.999, 1e-08, 0.01, 10

_NN = (((1,), (0,)), ((), ()))
_NT = (((1,), (1,)), ((), ()))
_TN = (((0,), (0,)), ((), ()))


def _dg(a, b, dims):
    return lax.dot_general(a.astype(BF16), b.astype(BF16), dims, preferred_element_type=F32)


@jax.custom_vjp
def dot_nn(a, b):
    return _dg(a, b, _NN)


def _nn_fwd(a, b):
    return _dg(a, b, _NN), (a, b)


def _nn_bwd(res, g):
    a, b = res
    return _dg(g, b, _NT).astype(a.dtype), _dg(a, g, _TN).astype(b.dtype)


dot_nn.defvjp(_nn_fwd, _nn_bwd)


@jax.custom_vjp
def dot_nt(a, b):
    return _dg(a, b, _NT)


def _nt_fwd(a, b):
    return _dg(a, b, _NT), (a, b)


def _nt_bwd(res, g):
    a, b = res
    return _dg(g, b, _NN).astype(a.dtype), _dg(g, a, _TN).astype(b.dtype)


dot_nt.defvjp(_nt_fwd, _nt_bwd)


def _params(sem=None):
    return pltpu.CompilerParams(dimension_semantics=sem, vmem_limit_bytes=VMEM_LIMIT)


def _sigmoid(x):
    return 1.0 / (1.0 + jnp.exp(-x))


def _silu(x):
    return x * _sigmoid(x)


def _rms(x, g):
    ms = jnp.mean(x * x, axis=-1, keepdims=True)
    return x * lax.rsqrt(ms + EPS) * g


def _acc(ref, val, first):
    @pl.when(first)
    def _():
        ref[...] = val

    @pl.when(jnp.logical_not(first))
    def _():
        ref[...] += val


ANY = pl.BlockSpec(memory_space=pl.ANY)


INPROJ_TN = 768


def inproj_fwd(x, g, wt, l, dep=None):
    tn = INPROJ_TN

    def body(x_ref, g_ref, w_ref, *rest):
        proj_ref, h_ref = rest[-2:]

        @pl.when(pl.program_id(0) == 0)
        def _():
            h_ref[...] = _rms(x_ref[...], g_ref[...]).astype(BF16)

        proj_ref[...] = lax.dot_general(h_ref[...], _unpack(w_ref[...]).astype(BF16), _NT, preferred_element_type=F32)

    deps = [] if dep is None else [dep]
    return pl.pallas_call(
        body, grid=(NPAD // tn,),
        in_specs=[pl.BlockSpec((T, D), lambda j: (0, 0)), pl.BlockSpec((1, D), lambda j: (0, 0)),
                  pl.BlockSpec((None, tn, D // 2), lambda j: (0, j, 0))] + [ANY] * len(deps),
        out_specs=[pl.BlockSpec((T, tn), lambda j: (0, j)), pl.BlockSpec((T, D), lambda j: (0, 0))],
        out_shape=[jax.ShapeDtypeStruct((T, NPAD), F32), jax.ShapeDtypeStruct((T, D), BF16)],
        name=f"inproj_fwd_l{l}", compiler_params=_params(("arbitrary",)))(x, g, wt, *deps)


def inproj_bwd(dproj, h, wt, l, dep=None):
    tn = INPROJ_TN
    deps = [] if dep is None else [dep]

    def body(dp_ref, h_ref, w_ref, *rest):
        dwt_ref, dh_ref = rest[-2:]
        dp = dp_ref[...]
        dwt_ref[...] = lax.dot_general(dp, h_ref[...], _TN, preferred_element_type=F32)
        contrib = lax.dot_general(dp, _unpack(w_ref[...]).astype(BF16), _NN, preferred_element_type=F32)
        _acc(dh_ref, contrib, pl.program_id(0) == 0)

    return pl.pallas_call(
        body, grid=(NPAD // tn,),
        in_specs=[pl.BlockSpec((T, tn), lambda j: (0, j)), pl.BlockSpec((T, D), lambda j: (0, 0)),
                  pl.BlockSpec((None, tn, D // 2), lambda j: (0, j, 0))] + [ANY] * len(deps),
        out_specs=[pl.BlockSpec((None, tn, D), lambda j: (0, j, 0)), pl.BlockSpec((T, D), lambda j: (0, 0))],
        out_shape=[jax.ShapeDtypeStruct((1, NPAD, D), F32), jax.ShapeDtypeStruct((T, D), F32)],
        name=f"inproj_bwd_l{l}", compiler_params=_params(("arbitrary",)))(dproj, h, wt, *deps)


def prenorm_bwd(x, g, dh, dxn, l):
    tm = 256

    def body(x_ref, g_ref, dh_ref, dxn_ref, dx_ref, dg_ref):
        _, vjp = jax.vjp(_rms, x_ref[...], g_ref[...])
        dx, dg = vjp(dh_ref[...])
        dx_ref[...] = dx + dxn_ref[...]
        _acc(dg_ref, dg, pl.program_id(0) == 0)

    tok = pl.BlockSpec((tm, D), lambda i: (i, 0))
    vec = pl.BlockSpec((1, D), lambda i: (0, 0))
    return pl.pallas_call(
        body, grid=(T // tm,), in_specs=[tok, vec, tok, tok], out_specs=[tok, vec],
        out_shape=[jax.ShapeDtypeStruct((T, D), F32), jax.ShapeDtypeStruct((1, D), F32)],
        name=f"prenorm_bwd_l{l}", compiler_params=_params(("arbitrary",)))(x, g, dh, dxn)


def _gmlp_tile(u, v, z, ln_g, ln_b, ws, bs):
    mu = jnp.mean(v, axis=-1, keepdims=True)
    vc = v - mu
    var = jnp.mean(vc * vc, axis=-1, keepdims=True)
    vn = vc * lax.rsqrt(var + EPS) * ln_g + ln_b
    qi = lax.broadcasted_iota(jnp.int32, (128, 128), 0) >> CHUNK_SHIFT
    kj = lax.broadcasted_iota(jnp.int32, (128, 128), 1) >> CHUNK_SHIFT
    mask = kj <= qi
    outs = []
    for g in range(4):
        wm = jnp.where(mask, ws[g], 0.0)
        outs.append(dot_nn(wm, vn[:, 256 * g:256 * (g + 1)]) + bs[g])
    sv = jnp.concatenate(outs, axis=1)
    return u * sv * _silu(z)


def _gmlp_specs():
    blk = lambda c: pl.BlockSpec((128, 1024), lambda n, c=c: (n, c))
    vec = pl.BlockSpec((1, 1024), lambda n: (0, 0))
    return [blk(0), blk(1), blk(2), vec, vec,
            pl.BlockSpec((4, 128, 128), lambda n: (0, 0, 0)), pl.BlockSpec((4, 128, 1), lambda n: (0, 0, 0))]


def gmlp_fwd(proj, ln_g, ln_b, ws, bs, l):
    def body(u_ref, v_ref, z_ref, g_ref, b_ref, ws_ref, bs_ref, y_ref):
        y_ref[...] = _gmlp_tile(u_ref[...], v_ref[...], z_ref[...], g_ref[...], b_ref[...],
                                [ws_ref[g] for g in range(4)], [bs_ref[g] for g in range(4)])

    return pl.pallas_call(
        body, grid=(T // 128,), in_specs=_gmlp_specs(),
        out_specs=pl.BlockSpec((128, 1024), lambda n: (n, 0)),
        out_shape=jax.ShapeDtypeStruct((T, 1024), F32),
        name=f"gmlp_fwd_l{l}", compiler_params=_params(("arbitrary",)))(proj, proj, proj, ln_g, ln_b, ws, bs)


def gmlp_bwd(proj, ln_g, ln_b, ws, bs, dy, dproj, l):
    def body(u_ref, v_ref, z_ref, g_ref, b_ref, ws_ref, bs_ref, dy_ref, _, dseg_ref, dg_ref, db_ref, dws_ref, dbs_ref):
        first = pl.program_id(0) == 0
        _, vjp = jax.vjp(_gmlp_tile, u_ref[...], v_ref[...], z_ref[...], g_ref[...], b_ref[...],
                         [ws_ref[g] for g in range(4)], [bs_ref[g] for g in range(4)])
        du, dv, dz, dg, db, dws, dbs = vjp(dy_ref[...])
        dseg_ref[:, 0:1024] = du.astype(BF16)
        dseg_ref[:, 1024:2048] = dv.astype(BF16)
        dseg_ref[:, 2048:3072] = dz.astype(BF16)
        _acc(dg_ref, dg, first)
        _acc(db_ref, db, first)
        for g in range(4):
            _acc(dws_ref.at[g], dws[g], first)
            _acc(dbs_ref.at[g], dbs[g], first)

    vec = pl.BlockSpec((1, 1024), lambda n: (0, 0))
    return pl.pallas_call(
        body, grid=(T // 128,), in_specs=_gmlp_specs() + [pl.BlockSpec((128, 1024), lambda n: (n, 0)), ANY],
        out_specs=[pl.BlockSpec((128, 3072), lambda n: (n, OFF_U // 3072)), vec, vec,
                   pl.BlockSpec((4, 128, 128), lambda n: (0, 0, 0)), pl.BlockSpec((4, 128, 1), lambda n: (0, 0, 0))],
        out_shape=[jax.ShapeDtypeStruct((T, NPAD), BF16), jax.ShapeDtypeStruct((1, 1024), F32),
                   jax.ShapeDtypeStruct((1, 1024), F32), jax.ShapeDtypeStruct((4, 128, 128), F32),
                   jax.ShapeDtypeStruct((4, 128, 1), F32)],
        input_output_aliases={8: 0},
        name=f"gmlp_bwd_l{l}", compiler_params=_params(("arbitrary",)))(proj, proj, proj, ln_g, ln_b, ws, bs, dy, dproj)


QKV_TM = 256


def _qkv_tile(cq, ckvr, qg, kvg, wq, wkv, ctab, stab):
    tm = cq.shape[0]
    cqn = _rms(cq, qg)
    lane = lax.broadcasted_iota(jnp.int32, ckvr.shape, 1)
    iskv = lane < 256
    ms = jnp.sum(jnp.where(iskv, ckvr * ckvr, 0.0), axis=-1, keepdims=True) * (1.0 / 256)
    lm = jnp.where(iskv, ckvr * lax.rsqrt(ms + EPS) * kvg, ckvr)
    r = lax.broadcasted_iota(jnp.int32, (64, 128), 0)
    c = lax.broadcasted_iota(jnp.int32, (64, 128), 1)
    eye = jnp.where(c == r, 1.0, 0.0)
    eye_sw = jnp.where(c == ((r + 32) & 63), 1.0, 0.0)
    z64 = jnp.zeros((64, 256), F32)
    z128 = jnp.zeros((128, 128), F32)
    rk_rope = jnp.concatenate([z64, eye], axis=1)
    rk_sw = jnp.concatenate([jnp.zeros((128, 384), F32), jnp.concatenate([z64, eye_sw], axis=1)], axis=0)
    k_sw = dot_nt(lm, rk_sw) * stab
    qs, ks, vs = [], [], []
    for h in range(HEADS):
        wn, w1, w2 = wq[h]
        wk, wv = wkv[h]
        wq_h = jnp.concatenate([wn, w1, w2], axis=0)
        wq_sw = jnp.concatenate([jnp.zeros((128, 384), F32), w2, w1], axis=0)
        qs.append(dot_nt(cqn, wq_h) * ctab + dot_nt(cqn, wq_sw) * stab)
        rk_h = jnp.concatenate([jnp.concatenate([wk, z128], axis=1), rk_rope], axis=0)
        ks.append(dot_nt(lm, rk_h) * ctab + k_sw)
        vs.append(dot_nt(lm, jnp.concatenate([wv, z128], axis=1)))
    return qs, ks, vs


def _qkv_in_specs():
    tm = QKV_TM
    return [pl.BlockSpec((tm, 384), lambda i: (i, OFF_CQ // 384)), pl.BlockSpec((tm, 384), lambda i: (i, OFF_CKV // 384)),
            pl.BlockSpec((1, 384), lambda i: (0, 0)), pl.BlockSpec((1, 384), lambda i: (0, 0)),
            pl.BlockSpec((HEADS, 192, 384), lambda i: (0, 0, 0)), pl.BlockSpec((HEADS, 256, 128), lambda i: (0, 0, 0)),
            pl.BlockSpec((tm, 192), lambda i: (i, 0)), pl.BlockSpec((tm, 192), lambda i: (i, 0))]


def _qkv_weights(wq_ref, wkv_ref):
    wq = [(wq_ref[h, 0:128, :], wq_ref[h, 128:160, :], wq_ref[h, 160:192, :]) for h in range(HEADS)]
    wkv = [(_unpack(wkv_ref[h, 0:128, :]), _unpack(wkv_ref[h, 128:256, :])) for h in range(HEADS)]
    return wq, wkv


def qkv_fwd(proj, qg, kvg, wq, wkv, ctab, stab, l, dep=None):
    tm = QKV_TM
    deps = [] if dep is None else [dep]

    def body(cq_ref, ckvr_ref, qg_ref, kvg_ref, wq_ref, wkv_ref, c_ref, s_ref, *rest):
        q_ref, k_ref, v_ref = rest[-3:]
        wq_l, wkv_l = _qkv_weights(wq_ref, wkv_ref)
        qs, ks, vs = _qkv_tile(cq_ref[...], ckvr_ref[...], qg_ref[...], kvg_ref[...], wq_l, wkv_l, c_ref[...], s_ref[...])
        for h in range(HEADS):
            q_ref[h] = qs[h]
            k_ref[h] = ks[h]
            v_ref[h] = vs[h]

    return pl.pallas_call(
        body, grid=(T // tm,), in_specs=_qkv_in_specs() + [ANY] * len(deps),
        out_specs=[pl.BlockSpec((HEADS, tm, QK), lambda i: (0, i, 0)), pl.BlockSpec((HEADS, tm, QK), lambda i: (0, i, 0)),
                   pl.BlockSpec((HEADS, tm, 128), lambda i: (0, i, 0))],
        out_shape=[jax.ShapeDtypeStruct((HEADS, T, QK), F32), jax.ShapeDtypeStruct((HEADS, T, QK), F32),
                   jax.ShapeDtypeStruct((HEADS, T, 128), F32)],
        name=f"qkv_fwd_l{l}", compiler_params=_params(("arbitrary",)))(proj, proj, qg, kvg, wq, wkv, ctab, stab, *deps)


def qkv_bwd(proj, qg, kvg, wq, wkv, ctab, stab, dq, dk, dv, dproj, l):
    tm = QKV_TM

    def body(cq_ref, ckvr_ref, qg_ref, kvg_ref, wq_ref, wkv_ref, c_ref, s_ref, dq_ref, dk_ref, dv_ref, _,
             dseg_ref, dqg_ref, dkvg_ref, dwq_ref, dwkv_ref):
        first = pl.program_id(0) == 0
        wq_l, wkv_l = _qkv_weights(wq_ref, wkv_ref)
        c_tab, s_tab = c_ref[...], s_ref[...]
        fn = lambda cq, ckvr, qg_, kvg_, wq_, wkv_: _qkv_tile(cq, ckvr, qg_, kvg_, wq_, wkv_, c_tab, s_tab)
        _, vjp = jax.vjp(fn, cq_ref[...], ckvr_ref[...], qg_ref[...], kvg_ref[...], wq_l, wkv_l)
        cts = ([dq_ref[h] for h in range(HEADS)], [dk_ref[h] for h in range(HEADS)], [dv_ref[h] for h in range(HEADS)])
        dcq, dckvr, dqg, dkvg, dwq, dwkv = vjp(cts)
        dseg_ref[:, 0:384] = dcq.astype(BF16)
        dseg_ref[:, 384:768] = dckvr.astype(BF16)
        _acc(dqg_ref, dqg, first)
        _acc(dkvg_ref, dkvg, first)
        for h in range(HEADS):
            _acc(dwq_ref.at[h, 0:128, :], dwq[h][0], first)
            _acc(dwq_ref.at[h, 128:160, :], dwq[h][1], first)
            _acc(dwq_ref.at[h, 160:192, :], dwq[h][2], first)
            _acc(dwkv_ref.at[h, 0:128, :], dwkv[h][0], first)
            _acc(dwkv_ref.at[h, 128:256, :], dwkv[h][1], first)

    hq = pl.BlockSpec((HEADS, tm, QK), lambda i: (0, i, 0))
    return pl.pallas_call(
        body, grid=(T // tm,),
        in_specs=_qkv_in_specs() + [hq, hq, pl.BlockSpec((HEADS, tm, 128), lambda i: (0, i, 0)), ANY],
        out_specs=[pl.BlockSpec((tm, 768), lambda i: (i, OFF_CQ // 768)), pl.BlockSpec((1, 384), lambda i: (0, 0)),
                   pl.BlockSpec((1, 384), lambda i: (0, 0)), pl.BlockSpec((HEADS, 192, 384), lambda i: (0, 0, 0)),
                   pl.BlockSpec((HEADS, 256, 256), lambda i: (0, 0, 0))],
        out_shape=[jax.ShapeDtypeStruct((T, NPAD), BF16), jax.ShapeDtypeStruct((1, 384), F32),
                   jax.ShapeDtypeStruct((1, 384), F32), jax.ShapeDtypeStruct((HEADS, 192, 384), F32),
                   jax.ShapeDtypeStruct((HEADS, 256, 256), F32)],
        input_output_aliases={11: 0},
        name=f"qkv_bwd_l{l}", compiler_params=_params(("arbitrary",)))(
            proj, proj, qg, kvg, wq, wkv, ctab, stab, dq, dk, dv, dproj)


ATT_TQ_FWD = 256
ATT_TQ_BWD = 512


def _attn_tile(q, kv_past, k, v, zb):
    q = q * (1.0 / math.sqrt(QK))
    s = dot_nt(q, k)
    qc = lax.broadcasted_iota(jnp.int32, s.shape, 0) >> CHUNK_SHIFT
    kc = lax.broadcasted_iota(jnp.int32, s.shape, 1) >> CHUNK_SHIFT
    s = jnp.where(kc <= qc, s, -1e30)
    m = jnp.max(s, axis=-1, keepdims=True)
    if kv_past is not None:
        sp = dot_nt(q, kv_past[0])
        m = jnp.maximum(m, jnp.max(sp, axis=-1, keepdims=True))
    m = lax.stop_gradient(m)
    p = jnp.exp(s - m)
    denom = jnp.sum(p, axis=-1, keepdims=True)
    o = dot_nn(p, v)
    if kv_past is not None:
        pp = jnp.exp(sp - m)
        denom = denom + jnp.sum(pp, axis=-1, keepdims=True)
        o = o + dot_nn(pp, kv_past[1])
    return o * (1.0 / denom) * _silu(zb)


def _attn_operands(k_ref, v_ref, g, tq):
    n = tq * g
    past = (k_ref[0:n, :], v_ref[0:n, :]) if g else None
    return past, k_ref[n:n + tq, :], v_ref[n:n + tq, :]


def _attn_in_specs(tq):
    return [pl.BlockSpec((None, tq, QK), lambda h, i: (h, i, 0)), pl.BlockSpec((None, T, QK), lambda h, i: (h, 0, 0)),
            pl.BlockSpec((None, T, 128), lambda h, i: (h, 0, 0)),
            pl.BlockSpec((tq, 128), lambda h, i: (i, OFF_ZB // 128 + h))]


def attn_fwd(q, k, v, proj, l):
    tq = ATT_TQ_FWD

    def body(q_ref, k_ref, v_ref, z_ref, y_ref):
        for g in range(T // tq):
            @pl.when(pl.program_id(1) == g)
            def _(g=g):
                past, k, v = _attn_operands(k_ref, v_ref, g, tq)
                y_ref[...] = _attn_tile(q_ref[...], past, k, v, z_ref[...])

    return pl.pallas_call(
        body, grid=(HEADS, T // tq), in_specs=_attn_in_specs(tq),
        out_specs=pl.BlockSpec((tq, 128), lambda h, i: (i, h)),
        out_shape=jax.ShapeDtypeStruct((T, 1024), F32),
        name=f"attn_fwd_l{l}", compiler_params=_params(("arbitrary", "arbitrary")))(q, k, v, proj)


def attn_bwd(q, k, v, proj, dy, dproj, l):
    tq = ATT_TQ_BWD

    def body(q_ref, k_ref, v_ref, z_ref, dy_ref, _, dq_ref, dk_ref, dv_ref, dz_ref):
        @pl.when(pl.program_id(1) == 0)
        def _():
            dk_ref[...] = jnp.zeros_like(dk_ref)
            dv_ref[...] = jnp.zeros_like(dv_ref)

        for g in range(T // tq):
            @pl.when(pl.program_id(1) == g)
            def _(g=g):
                n = tq * g
                past, k, v = _attn_operands(k_ref, v_ref, g, tq)
                _, vjp = jax.vjp(_attn_tile, q_ref[...], past, k, v, z_ref[...])
                dq, dpast, dk, dv, dz = vjp(dy_ref[...])
                dq_ref[...] = dq
                dz_ref[...] = dz.astype(BF16)
                dk_ref[n:n + tq, :] += dk
                dv_ref[n:n + tq, :] += dv
                if g:
                    dk_ref[0:n, :] += dpast[0]
                    dv_ref[0:n, :] += dpast[1]

    return pl.pallas_call(
        body, grid=(HEADS, T // tq),
        in_specs=_attn_in_specs(tq) + [pl.BlockSpec((tq, 128), lambda h, i: (i, h)), ANY],
        out_specs=[pl.BlockSpec((None, tq, QK), lambda h, i: (h, i, 0)), pl.BlockSpec((None, T, QK), lambda h, i: (h, 0, 0)),
                   pl.BlockSpec((None, T, 128), lambda h, i: (h, 0, 0)),
                   pl.BlockSpec((tq, 128), lambda h, i: (i, OFF_ZB // 128 + h))],
        out_shape=[jax.ShapeDtypeStruct((HEADS, T, QK), F32), jax.ShapeDtypeStruct((HEADS, T, QK), F32),
                   jax.ShapeDtypeStruct((HEADS, T, 128), F32), jax.ShapeDtypeStruct((T, NPAD), BF16)],
        input_output_aliases={5: 3},
        name=f"attn_bwd_l{l}", compiler_params=_params(("arbitrary", "arbitrary")))(q, k, v, proj, dy, dproj)


LRU_TT = 256


def _lru_gates(xc, wa, wx, ba, bx, lam):
    r = _sigmoid(dot_nn(xc, wa) + ba)
    i = _sigmoid(dot_nn(xc, wx) + bx)
    sp = jnp.maximum(-lam, 0.0) + jnp.log1p(jnp.exp(-jnp.abs(lam)))
    log_a = -8.0 * r * sp
    a = jnp.exp(log_a)
    mult = jnp.sqrt(jnp.maximum(1.0 - jnp.exp(2.0 * log_a), 0.0))
    return a, mult * (i * xc)


def _shift_down(x, s, halo):
    n, c = x.shape
    r = pltpu.roll(x.reshape(n // 8, 8, c), s, 1)
    before = jnp.concatenate([pltpu.roll(halo, s, 0)[None], r[:-1]], axis=0)
    sub = lax.broadcasted_iota(jnp.int32, r.shape, 1)
    return jnp.where(sub >= s, r, before).reshape(n, c)


def _shift_up(x, s, halo):
    n, c = x.shape
    r = pltpu.roll(x.reshape(n // 8, 8, c), 8 - s, 1)
    after = jnp.concatenate([r[1:], pltpu.roll(halo, 8 - s, 0)[None]], axis=0)
    sub = lax.broadcasted_iota(jnp.int32, r.shape, 1)
    return jnp.where(sub < 8 - s, r, after).reshape(n, c)


def _conv(x, halo, w_ref, b):
    return (w_ref[3:4, :] * x + w_ref[2:3, :] * _shift_down(x, 1, halo) + w_ref[1:2, :] * _shift_down(x, 2, halo)
            + w_ref[0:1, :] * _shift_down(x, 3, halo) + b)


def _scan(a, b, reverse, carry):
    n, c = a.shape
    a, b = a.reshape(n // 8, 8, c), b.reshape(n // 8, 8, c)
    sub = lax.broadcasted_iota(jnp.int32, a.shape, 1)
    for d in (1, 2, 4):
        keep = sub < 8 - d if reverse else sub >= d
        shift = 8 - d if reverse else d
        a_sh = jnp.where(keep, pltpu.roll(a, shift, 1), 1.0)
        b_sh = jnp.where(keep, pltpu.roll(b, shift, 1), 0.0)
        b = a * b_sh + b
        a = a * a_sh
    a, b = a.reshape(n, c), b.reshape(n, c)
    groups = [None] * (n // 8)
    for g in (reversed(range(n // 8)) if reverse else range(n // 8)):
        h = a[8 * g:8 * g + 8] * carry + b[8 * g:8 * g + 8]
        groups[g] = h
        carry = h[0:1] if reverse else h[7:8]
    return jnp.concatenate(groups, axis=0), carry


def _lru_param_specs(l):
    ct = LRU_TILE
    vec = pl.BlockSpec((1, ct), lambda n, i: (0, n))
    mat = pl.BlockSpec((None, 8, 80, 80), lambda n, i: (l, n, 0, 0))
    return [pl.BlockSpec((4, ct), lambda n, i: (0, n)), vec, mat, mat, vec, vec, vec]


def _blocks_to_dense(w_ref, dense):
    dense[...] = jnp.zeros_like(dense)
    for b in range(8):
        dense[80 * b:80 * b + 80, 80 * b:80 * b + 80] = w_ref[b]


def _dense_to_blocks(dense, w_ref):
    for b in range(8):
        w_ref[b] = dense[80 * b:80 * b + 80, 80 * b:80 * b + 80]


def lru_fwd(proj, conv_w, conv_b, wa, wx, ba, bx, lam, l):
    tt, ct = LRU_TT, LRU_TILE

    def body(x_ref, z_ref, cw_ref, cb_ref, wa_ref, wx_ref, ba_ref, bx_ref, lam_ref, h_ref, y_ref, halo, hcar, wa, wx):
        @pl.when(pl.program_id(1) == 0)
        def _():
            halo[...] = jnp.zeros_like(halo)
            hcar[...] = jnp.zeros_like(hcar)
            _blocks_to_dense(wa_ref, wa)
            _blocks_to_dense(wx_ref, wx)

        x = x_ref[...]
        xc = _conv(x, halo[...], cw_ref, cb_ref[...])
        halo[...] = x[tt - 8:tt]
        a, b = _lru_gates(xc, wa[...], wx[...], ba_ref[...], bx_ref[...], lam_ref[...])
        h, hcar[...] = _scan(a, b, False, hcar[...])
        h_ref[...] = h
        y_ref[...] = h * _silu(z_ref[...])

    seq = pl.BlockSpec((tt, ct), lambda n, i: (i, n))
    return pl.pallas_call(
        body, grid=(LRU_W // ct, T // tt),
        in_specs=[pl.BlockSpec((tt, ct), lambda n, i: (i, OFF_XC // ct + n)),
                  pl.BlockSpec((tt, ct), lambda n, i: (i, OFF_ZC // ct + n))] + _lru_param_specs(l),
        out_specs=[seq, seq],
        out_shape=[jax.ShapeDtypeStruct((T, LRU_W), F32), jax.ShapeDtypeStruct((T, LRU_W), F32)],
        scratch_shapes=[pltpu.VMEM((8, ct), F32), pltpu.VMEM((1, ct), F32), pltpu.VMEM((ct, ct), F32),
                        pltpu.VMEM((ct, ct), F32)],
        name=f"lru_fwd_l{l}", compiler_params=_params(("arbitrary", "arbitrary")))(
            proj, proj, conv_w, conv_b, wa, wx, ba, bx, lam)


def lru_bwd(proj, hseq, dy, conv_w, conv_b, wa, wx, ba, bx, lam, dproj, l):
    tt, ct = LRU_TT, LRU_TILE
    nt = T // tt
    rev = lambda i: nt - 1 - i
    prev8 = lambda i: jnp.maximum(rev(i) * (tt // 8) - 1, 0)

    def body(x_ref, xh_ref, z_ref, h_ref, hh_ref, dy_ref, cw_ref, cb_ref, wa_ref, wx_ref, ba_ref, bx_ref, lam_ref, _,
             dx_ref, dcw_ref, dcb_ref, dwa_ref, dwx_ref, dba_ref, dbx_ref, dlam_ref, gcar, dhalo,
             wa, wx, dwa_acc, dwx_acc):
        i = pl.program_id(1)
        first = i == 0

        @pl.when(first)
        def _():
            gcar[...] = jnp.zeros_like(gcar)
            dhalo[...] = jnp.zeros_like(dhalo)
            _blocks_to_dense(wa_ref, wa)
            _blocks_to_dense(wx_ref, wx)

        at_start = rev(i) == 0
        x = x_ref[...]
        xhalo = jnp.where(at_start, 0.0, xh_ref[...])
        sh = [x, _shift_down(x, 1, xhalo), _shift_down(x, 2, xhalo), _shift_down(x, 3, xhalo)]
        xc = (cw_ref[3:4, :] * sh[0] + cw_ref[2:3, :] * sh[1] + cw_ref[1:2, :] * sh[2] + cw_ref[0:1, :] * sh[3]
              + cb_ref[...])
        (a, b), vjp = jax.vjp(_lru_gates, xc, wa[...], wx[...], ba_ref[...], bx_ref[...], lam_ref[...])
        hs = h_ref[...]
        hprev = _shift_down(hs, 1, jnp.where(at_start, 0.0, hh_ref[...]))
        dh = dy_ref[...] * _silu(z_ref[...])
        a_next = _shift_up(a, 1, jnp.ones((8, ct), F32))
        g, _ = _scan(a_next, dh, True, gcar[...])
        dxc, dwa, dwx, dba, dbx, dlam = vjp((g * hprev, g))
        dx = (cw_ref[3:4, :] * dxc + cw_ref[2:3, :] * _shift_up(dxc, 1, dhalo[...])
              + cw_ref[1:2, :] * _shift_up(dxc, 2, dhalo[...]) + cw_ref[0:1, :] * _shift_up(dxc, 3, dhalo[...]))
        dx_ref[...] = dx.astype(BF16)
        dhalo[...] = dxc[0:8]
        ag = a * g
        gcar[...] = ag[0:1]
        dcw = jnp.concatenate([jnp.sum(dxc * sh[3 - j], axis=0, keepdims=True) for j in range(4)], axis=0)
        _acc(dcw_ref, dcw, first)
        _acc(dcb_ref, jnp.sum(dxc, axis=0, keepdims=True), first)
        _acc(dwa_acc, dwa, first)
        _acc(dwx_acc, dwx, first)

        @pl.when(i == nt - 1)
        def _():
            _dense_to_blocks(dwa_acc, dwa_ref)
            _dense_to_blocks(dwx_acc, dwx_ref)

        _acc(dba_ref, dba, first)
        _acc(dbx_ref, dbx, first)
        _acc(dlam_ref, dlam, first)

    xcol = OFF_XC // ct
    zcol = OFF_ZC // ct
    vec = pl.BlockSpec((1, ct), lambda n, i: (0, n))
    mat = pl.BlockSpec((8, 80, 80), lambda n, i: (n, 0, 0))
    seq = pl.BlockSpec((tt, ct), lambda n, i: (rev(i), n))
    return pl.pallas_call(
        body, grid=(LRU_W // ct, nt),
        in_specs=[pl.BlockSpec((tt, ct), lambda n, i: (rev(i), xcol + n)),
                  pl.BlockSpec((8, ct), lambda n, i: (prev8(i), xcol + n)),
                  pl.BlockSpec((tt, ct), lambda n, i: (rev(i), zcol + n)),
                  seq, pl.BlockSpec((8, ct), lambda n, i: (prev8(i), n)), seq] + _lru_param_specs(l) + [ANY],
        out_specs=[pl.BlockSpec((tt, ct), lambda n, i: (rev(i), xcol + n)),
                   pl.BlockSpec((4, ct), lambda n, i: (0, n)), vec, mat, mat, vec, vec, vec],
        out_shape=[jax.ShapeDtypeStruct((T, NPAD), BF16),
                   jax.ShapeDtypeStruct((4, LRU_W), F32), jax.ShapeDtypeStruct((1, LRU_W), F32),
                   jax.ShapeDtypeStruct((16, 80, 80), F32), jax.ShapeDtypeStruct((16, 80, 80), F32),
                   jax.ShapeDtypeStruct((1, LRU_W), F32), jax.ShapeDtypeStruct((1, LRU_W), F32),
                   jax.ShapeDtypeStruct((1, LRU_W), F32)],
        scratch_shapes=[pltpu.VMEM((1, ct), F32), pltpu.VMEM((8, ct), F32)] + [pltpu.VMEM((ct, ct), F32)] * 4,
        input_output_aliases={13: 0},
        name=f"lru_bwd_l{l}", compiler_params=_params(("arbitrary", "arbitrary")))(
            proj, proj, proj, hseq, hseq, dy, conv_w, conv_b, wa, wx, ba, bx, lam, dproj)


def proj_bwd(y, dp, w, l, tag, dep=None, dproj=None, gate=None):
    tm = 512
    k = y.shape[1]
    extra = [] if dep is None else [dep]
    in_specs = [pl.BlockSpec((tm, k), lambda i: (i, 0)), pl.BlockSpec((tm, D), lambda i: (i, 0)),
                pl.BlockSpec((None, k, D // 2), lambda i: (0, 0, 0))]
    out_specs = [pl.BlockSpec((tm, k), lambda i: (i, 0)), pl.BlockSpec((None, k, D), lambda i: (0, 0, 0))]
    out_shape = [jax.ShapeDtypeStruct((T, k), F32), jax.ShapeDtypeStruct((1, k, D), F32)]
    aliases = {}
    if gate is not None:
        in_specs += [pl.BlockSpec((tm, k), lambda i: (i, 0)), pl.BlockSpec((tm, k), lambda i: (i, OFF_ZC // k))]
        extra = list(gate) + extra
    if dproj is not None:
        width = k if gate is not None else PAD2
        at = OFF_ZC if gate is not None else OFF_XC - PAD2
        aliases = {3 + len(extra): 2}
        extra = extra + [dproj]
        out_specs.append(pl.BlockSpec((tm, width), lambda i: (i, at // width)))
        out_shape.append(jax.ShapeDtypeStruct((T, NPAD), BF16))
    in_specs += [ANY] * (3 + len(extra) - len(in_specs))

    def body(y_ref, dp_ref, w_ref, *rest):
        dy_ref, dw_ref = rest[len(extra):len(extra) + 2]
        dp = dp_ref[...]
        dy = _dg(dp, _unpack(w_ref[...]), _NT)
        dy_ref[...] = dy
        _acc(dw_ref, _dg(y_ref[...], dp, _TN), pl.program_id(0) == 0)
        if gate is not None:
            z = rest[1][...]
            sg = _sigmoid(z)
            rest[len(extra) + 2][...] = (dy * rest[0][...] * (sg * (1.0 + z * (1.0 - sg)))).astype(BF16)
        elif dproj is not None:
            rest[len(extra) + 2][...] = jnp.zeros((tm, PAD2), BF16)

    return pl.pallas_call(
        body, grid=(T // tm,), in_specs=in_specs, out_specs=out_specs, out_shape=out_shape,
        input_output_aliases=aliases,
        name=f"proj_{tag}_bwd_l{l}", compiler_params=_params(("arbitrary",)))(y, dp, w, *extra)


OUT_TM = 256


def _out_tile(pa, pb, pc, ga, gb, gc, wout, post_g):
    merged = _sigmoid(ga) * pa + _sigmoid(gb) * pb + _sigmoid(gc) * pc
    return _rms(dot_nn(merged, wout), post_g)


def _out_in_specs():
    tm = OUT_TM
    tok = pl.BlockSpec((tm, D), lambda i: (i, 0))
    gate = lambda off: pl.BlockSpec((tm, 512), lambda i, off=off: (i, off // 512))
    return [tok, tok, tok, gate(OFF_GA), gate(OFF_GA + 512), gate(OFF_GB), gate(OFF_GB + 512), gate(OFF_GC),
            gate(OFF_GC + 512), pl.BlockSpec((None, D, D // 2), lambda i: (0, 0, 0)), pl.BlockSpec((1, D), lambda i: (0, 0))]


def _gates(refs):
    return [jnp.concatenate([refs[2 * j][...], refs[2 * j + 1][...]], axis=1) for j in range(3)]


def out_fwd(x, ya, yb, yc, proj, wpa, wpb, wpc, wout, post_g, l):
    tm = OUT_TM

    def body(ya_ref, yb_ref, yc_ref, g0, g1, g2, g3, g4, g5, wo_ref, pg_ref, x_ref, wa_ref, wb_ref, wc_ref,
             o_ref, pa_ref, pb_ref, pc_ref, wa, wb, wc, wo):
        @pl.when(pl.program_id(0) == 0)
        def _():
            for dst, src in ((wa, wa_ref), (wb, wb_ref), (wc, wc_ref), (wo, wo_ref)):
                dst[...] = _unpack(src[...]).astype(BF16)

        pa = _dg(ya_ref[...], wa[...], _NN)
        pb = _dg(yb_ref[...], wb[...], _NN)
        pc = _dg(yc_ref[...], wc[...], _NN)
        ga, gb, gc = _gates([g0, g1, g2, g3, g4, g5])
        o_ref[...] = x_ref[...] + _out_tile(pa, pb, pc, ga, gb, gc, wo[...], pg_ref[...])
        pa_ref[...] = pa.astype(BF16)
        pb_ref[...] = pb.astype(BF16)
        pc_ref[...] = pc.astype(BF16)

    tok = pl.BlockSpec((tm, D), lambda i: (i, 0))
    words = lambda k: pl.BlockSpec((None, k, D // 2), lambda i: (0, 0, 0))
    specs = _out_in_specs()
    specs[2] = pl.BlockSpec((tm, LRU_W), lambda i: (i, 0))
    return pl.pallas_call(
        body, grid=(T // tm,), in_specs=specs + [tok, words(D), words(D), words(LRU_W)], out_specs=[tok] * 4,
        out_shape=[jax.ShapeDtypeStruct((T, D), F32)] + [jax.ShapeDtypeStruct((T, D), BF16)] * 3,
        scratch_shapes=[pltpu.VMEM((D, D), BF16), pltpu.VMEM((D, D), BF16), pltpu.VMEM((LRU_W, D), BF16),
                        pltpu.VMEM((D, D), BF16)],
        name=f"out_fwd_l{l}", compiler_params=_params(("arbitrary",)))(
            ya, yb, yc, proj, proj, proj, proj, proj, proj, wout, post_g, x, wpa, wpb, wpc)


def out_bwd(pa, pb, pc, proj, wout, post_g, dxn, l, dep=None):
    tm = OUT_TM
    nsteps = T // tm

    def body(pa_ref, pb_ref, pc_ref, g0, g1, g2, g3, g4, g5, w_ref, pg_ref, dxn_ref, *rest):
        dpa_ref, dpb_ref, dpc_ref, dproj_ref, dw_ref, dpg_ref, gbuf, sem = rest[-8:]
        i = pl.program_id(0)
        first = i == 0
        slot = i % 2
        ga, gb, gc = _gates([g0, g1, g2, g3, g4, g5])
        _, vjp = jax.vjp(_out_tile, pa_ref[...], pb_ref[...], pc_ref[...], ga, gb, gc, _unpack(w_ref[...]), pg_ref[...])
        dpa, dpb, dpc, dga, dgb, dgc, dw, dpg = vjp(dxn_ref[...])
        dpa_ref[...] = dpa.astype(BF16)
        dpb_ref[...] = dpb.astype(BF16)
        dpc_ref[...] = dpc.astype(BF16)
        _acc(dw_ref, dw, first)
        _acc(dpg_ref, dpg, first)

        def writeback(step, s):
            rows = pl.ds(pl.multiple_of(step * tm, tm), tm)
            return pltpu.make_async_copy(gbuf.at[s], dproj_ref.at[rows, pl.ds(OFF_GA, 3072)], sem.at[s])

        gbuf[slot, :, 0:1024] = dga.astype(BF16)
        gbuf[slot, :, 1024:2048] = dgb.astype(BF16)
        gbuf[slot, :, 2048:3072] = dgc.astype(BF16)
        writeback(i, slot).start()

        @pl.when(i > 0)
        def _():
            writeback(i - 1, 1 - slot).wait()

        @pl.when(i == nsteps - 1)
        def _():
            writeback(i, slot).wait()

    tok = pl.BlockSpec((tm, D), lambda i: (i, 0))
    deps = [] if dep is None else [dep]
    return pl.pallas_call(
        body, grid=(nsteps,), in_specs=_out_in_specs() + [tok] + [ANY] * len(deps),
        out_specs=[tok, tok, tok, ANY, pl.BlockSpec((None, D, D), lambda i: (0, 0, 0)), pl.BlockSpec((1, D), lambda i: (0, 0))],
        out_shape=[jax.ShapeDtypeStruct((T, D), BF16)] * 3 + [jax.ShapeDtypeStruct((T, NPAD), BF16),
                                                            jax.ShapeDtypeStruct((1, D, D), F32), jax.ShapeDtypeStruct((1, D), F32)],
        scratch_shapes=[pltpu.VMEM((2, tm, 3072), BF16), pltpu.SemaphoreType.DMA((2,))],
        name=f"out_bwd_l{l}", compiler_params=_params(("arbitrary",)))(
            pa, pb, pc, proj, proj, proj, proj, proj, proj, wout, post_g, dxn, *deps)


def loss_head(y, target):
    tm = 256

    def body(y_ref, t_ref, loss_ref, dy_ref):
        e = y_ref[...] - t_ref[...]
        dy_ref[...] = e * (1.0 / D)
        val = 0.5 * jnp.sum(jnp.mean(e * e, axis=-1, keepdims=True), axis=0, keepdims=True)
        _acc(loss_ref, jnp.broadcast_to(val, (8, 128)), pl.program_id(0) == 0)

    tok = pl.BlockSpec((tm, D), lambda i: (i, 0))
    total, dy = pl.pallas_call(
        body, grid=(T // tm,), in_specs=[tok, tok],
        out_specs=[pl.BlockSpec((8, 128), lambda i: (0, 0)), tok],
        out_shape=[jax.ShapeDtypeStruct((8, 128), F32), jax.ShapeDtypeStruct((T, D), F32)],
        name="loss_head", compiler_params=_params(("arbitrary",)))(y, target)
    return total[0, 0], dy


def _rope_tables():
    pos = jnp.arange(T, dtype=F32)
    inv_freq = 10000.0 ** (-jnp.arange(0, 64, 2, dtype=F32) / 64)
    ang = pos[:, None] * inv_freq[None, :]
    cos, sin = jnp.cos(ang), jnp.sin(ang)
    ctab = jnp.concatenate([jnp.ones((T, 128), F32), cos, cos], axis=1)
    stab = jnp.concatenate([jnp.zeros((T, 128), F32), -sin, sin], axis=1)
    return ctab, stab


def _layer_fwd(x, l, w, gw, tabs, dep=None, mid=None):
    row = lambda a: a[l][None]
    proj, h = inproj_fwd(x, row(w["pre_norm_g"]), gw["w_in_t"], l, dep)
    ya = gmlp_fwd(proj, row(w["gm_ln_g"]), row(w["gm_ln_b"]), w["gm_ws"][l], w["gm_bs"][l][..., None], l)
    dep2 = None
    if mid is not None:
        gw, dep2 = mid(ya)
    q, k, v = qkv_fwd(proj, row(w["mla_q_norm_g"]), row(w["kv_g384"]), gw["wq"], gw["wkv"], tabs[0], tabs[1], l, dep2)
    yb = attn_fwd(q, k, v, proj, l)
    hseq, yc = lru_fwd(proj, gw["conv"], row(w["lru_conv_b"]), w["lru_w_a"], w["lru_w_x"],
                       row(w["lru_b_a"]), row(w["lru_b_x"]), row(w["lru_lambda"]), l)
    xn, pa, pb, pc = out_fwd(x, ya, yb, yc, proj, gw["w_proj_a"], gw["w_proj_b"], gw["w_proj_c"], gw["w_out"],
                             row(w["post_norm_g"]), l)
    return xn, (x, proj, h, ya, q, k, v, yb, hseq, yc, pa, pb, pc)


def _layer_bwd(dxn, l, w, gw, tabs, saved, dep=None, early=None, mid=None):
    x, proj, h, ya, q, k, v, yb, hseq, yc, pa, pb, pc = saved
    row = lambda a: a[l][None]
    g, gg = {}, {}
    dpa, dpb, dpc, dproj, gg["w_out"], dpost = out_bwd(pa, pb, pc, proj, gw["w_out"], row(w["post_norm_g"]), dxn, l, dep)
    g["post_norm_g"] = dpost[0]
    dep1 = early(dpa) if early is not None else None
    dya, gg["w_proj_a"], dproj = proj_bwd(ya, dpa, gw["w_proj_a"], l, "a", dep1, dproj)
    dyb, gg["w_proj_b"] = proj_bwd(yb, dpb, gw["w_proj_b"], l, "b")
    dyc, gg["w_proj_c"], dproj = proj_bwd(yc, dpc, gw["w_proj_c"], l, "c", None, dproj, (hseq, proj))
    dproj, dln_g, dln_b, g["gm_ws"], dbs = gmlp_bwd(proj, row(w["gm_ln_g"]), row(w["gm_ln_b"]), w["gm_ws"][l],
                                                   w["gm_bs"][l][..., None], dya, dproj, l)
    g["gm_ln_g"], g["gm_ln_b"], g["gm_bs"] = dln_g[0], dln_b[0], dbs[..., 0]
    dq, dk, dv, dproj = attn_bwd(q, k, v, proj, dyb, dproj, l)
    dproj, dqg, dkvg, dwq, dwkv = qkv_bwd(proj, row(w["mla_q_norm_g"]), row(w["kv_g384"]), gw["wq"], gw["wkv"],
                                          tabs[0], tabs[1], dq, dk, dv, dproj, l)
    gg["wq"], gg["wkv"] = dwq.reshape(1, 1536, 384), dwkv.reshape(1, 2048, 256)
    g["mla_q_norm_g"], g["mla_kv_norm_g"] = dqg[0], dkvg[0, :256]
    dproj, dcw, dcb, dwa, dwx, dba, dbx, dlam = lru_bwd(
        proj, hseq, dyc, gw["conv"], row(w["lru_conv_b"]), w["lru_w_a"], w["lru_w_x"],
        row(w["lru_b_a"]), row(w["lru_b_x"]), row(w["lru_lambda"]), dproj, l)
    gg["conv"] = jnp.pad(dcw.T, ((0, 0), (0, 124)))[None]
    g["lru_conv_b"], g["lru_b_a"], g["lru_b_x"], g["lru_lambda"] = dcb[0], dba[0], dbx[0], dlam[0]
    g["lru_w_a"], g["lru_w_x"] = dwa, dwx
    dep2 = mid(gg, dproj) if mid is not None else None
    gg["w_in_t"], dh = inproj_bwd(dproj, h, gw["w_in_t"], l, dep2)
    dx, dpre = prenorm_bwd(x, row(w["pre_norm_g"]), dh, dxn, l)
    g["pre_norm_g"] = dpre[0]
    return dx, gg, g


MESH = pl.DeviceIdType.MESH
HBM = pl.BlockSpec(memory_space=pltpu.HBM)
SEM = pl.BlockSpec(memory_space=pltpu.SEMAPHORE)
EFFECT = pltpu.SideEffectType.DATAFLOW_SIDE_EFFECTING
FLIPS = ((1, 0), (0, 1), (1, 1))


def _win_off(k, s):
    g = SHARD * k + s
    return g + jnp.where(g >= PAD1_AT, PAD1, 0) + jnp.where(g >= PAD2_AT, PAD2, 0)


def _plain_off(rows):
    return lambda k, s: rows * k + s


class Spec:
    def __init__(self, rows, cols, full_rows, pieces=None, off=None, layers=1, packed=None):
        self.rows, self.cols, self.full_rows, self.layers = rows, cols, full_rows, layers
        self.pieces = pieces or ((0, rows),)
        self.off = off or _plain_off(rows)
        self.packed = cols % 256 == 0 if packed is None else packed
        self.wcols = cols // 2 if self.packed else cols

    def to_words(self, a):
        return _pack(a) if self.packed else a

    def from_words(self, p):
        return _unpack(p) if self.packed else p


def _pack(a):
    bits = lambda v: lax.bitcast_convert_type(v.astype(jnp.bfloat16).astype(F32), jnp.uint32)
    words = [(bits(a[:, g:g + 128]) >> 16) | (bits(a[:, g + 128:g + 256]) & jnp.uint32(0xFFFF0000))
             for g in range(0, a.shape[-1], 256)]
    return lax.bitcast_convert_type(jnp.concatenate(words, axis=-1) if len(words) > 1 else words[0], F32)


def _unpack(p):
    w = lax.bitcast_convert_type(p, jnp.uint32)
    lo = lax.bitcast_convert_type(w << 16, F32)
    hi = lax.bitcast_convert_type(w & jnp.uint32(0xFFFF0000), F32)
    return jnp.concatenate([h[:, g:g + 128] for g in range(0, p.shape[-1], 128) for h in (lo, hi)], axis=-1)


WEIGHT_SPECS = {
    "w_in_t": Spec(SHARD, D, NPAD, WIN_PIECES, _win_off),
    "wq": Spec(192, 384, 1536),
    "wkv": Spec(256, 256, 2048),
    "conv": Spec(160, 128, 1280),
    "w_proj_a": Spec(128, D, 1024),
    "w_proj_b": Spec(128, D, 1024),
    "w_proj_c": Spec(160, D, 1280),
    "w_out": Spec(128, D, 1024),
}
REP_ROWS = 72
REP_SPEC = Spec(REP_ROWS, D, REP_ROWS * NDEV, packed=False)


def _coords():
    return lax.axis_index("x"), lax.axis_index("y"), lax.axis_index("c")


def _rows(ref, start, n):
    if not isinstance(start, int):
        start = pl.multiple_of(start, 8)
    return ref.at[:, pl.ds(start, n), :]


def _col_tile(cols):
    return 256 if cols % 256 == 0 else cols


def _n_pieces(specs):
    return sum(len(sp.pieces) for sp in specs)


def pack_place(shard, sp, layer, tag, dep=None):
    gaps = ((PAD1_AT, PAD1), (PAD2_AT + PAD1, PAD2)) if sp.off is _win_off else ()
    npc = len(sp.pieces)
    deps = [] if dep is None else [dep]

    def body(s_ref, *rest):
        words_ref, full_ref, buf, zbuf, sem = rest[-5:]
        l = 0
        x, y, c = _coords()
        me = 4 * x + 2 * y + c
        words = sp.to_words(s_ref[...])
        words_ref[...] = words
        buf[...] = words
        copies = [pltpu.make_async_copy(buf.at[pl.ds(s, n), :],
                                        full_ref.at[l, pl.ds(pl.multiple_of(sp.off(me, s), 8), n), :], sem.at[i])
                  for i, (s, n) in enumerate(sp.pieces)]
        if gaps:
            zbuf[...] = jnp.zeros_like(zbuf)
            copies += [pltpu.make_async_copy(zbuf.at[pl.ds(0, n), :], full_ref.at[l, pl.ds(at, n), :], sem.at[npc + i])
                       for i, (at, n) in enumerate(gaps)]
        for cp in copies:
            cp.start()
        for cp in copies:
            cp.wait()

    return pl.pallas_call(
        body, grid=(1,), in_specs=[pl.BlockSpec((None, sp.rows, sp.cols), lambda i: (layer, 0, 0))] + [ANY] * len(deps),
        out_specs=[pl.BlockSpec((None, sp.rows, sp.wcols), lambda i: (0, 0, 0)), ANY],
        out_shape=[jax.ShapeDtypeStruct((sp.layers, sp.rows, sp.wcols), F32),
                   jax.ShapeDtypeStruct((sp.layers, sp.full_rows, sp.wcols), F32)],
        scratch_shapes=[pltpu.VMEM((sp.rows, sp.wcols), F32), pltpu.VMEM((PAD2 if gaps else 8, sp.wcols), F32),
                        pltpu.SemaphoreType.DMA((npc + len(gaps),))],
        name=f"pack_place_{tag}", compiler_params=_params(("arbitrary",)))(shard, *deps)


def _gather_copies(srcs, bufs, specs, ssem, rsem, landing):
    x, y, c = _coords()
    me = 4 * x + 2 * y + c
    targets = [(x, y, 1 - c)] + [(x ^ fx, y ^ fy, c) for fx, fy in FLIPS]
    copies = []
    p = 0
    for src, buf, sp in zip(srcs, bufs, specs):
        for s, n in sp.pieces:
            for t, (tx, ty, tc) in enumerate(targets):
                owner = 4 * tx + 2 * ty + tc if landing else me
                copies.append(pltpu.make_async_remote_copy(_rows(src, s, n), _rows(buf, sp.off(owner, s), n),
                                                           ssem.at[4 * p + t], rsem.at[4 * p + t],
                                                           device_id=(tx, ty, tc), device_id_type=MESH))
            p += 1
    return copies


def gather_send(words, fulls, specs, tag):
    ns, npc = len(specs), _n_pieces(specs)

    def body(*refs):
        srcs, bufs, sems = refs[:ns], refs[2 * ns:3 * ns], refs[3 * ns:]
        for cp in _gather_copies(srcs, bufs, specs, *sems, False):
            cp.start()
        for cp in _gather_copies(srcs, bufs, specs, *sems, False):
            cp.wait_send()
        for cp in _gather_copies(srcs, bufs, specs, *sems, True):
            cp.wait_recv()

    return pl.pallas_call(
        body, in_specs=[ANY] * (2 * ns), out_specs=[ANY] * ns,
        out_shape=[jax.ShapeDtypeStruct(f.shape, f.dtype) for f in fulls],
        input_output_aliases={ns + i: i for i in range(ns)},
        scratch_shapes=[pltpu.SemaphoreType.DMA((4 * npc,)), pltpu.SemaphoreType.DMA((4 * npc,))],
        name=f"gather_send_{tag}", compiler_params=pltpu.CompilerParams(has_side_effects=True))(*words, *fulls)


def _in_hbm(arrays):
    return [pltpu.with_memory_space_constraint(a, pltpu.HBM) for a in arrays]


def gather_start(words, fulls, specs, dep, tag):
    ns, npc = len(specs), _n_pieces(specs)
    deps = [] if dep is None else [dep]

    def body(*refs):
        ssem, rsem = refs[2 * ns + len(deps):2 * ns + len(deps) + 2]
        for cp in _gather_copies(refs[:ns], refs[ns:2 * ns], specs, ssem, rsem, False):
            cp.start()
        refs[-1][...] = jnp.zeros_like(refs[-1])

    outs = pl.pallas_call(
        body, in_specs=[HBM] * (2 * ns) + [ANY] * len(deps),
        out_specs=[SEM, SEM] + [HBM] * (2 * ns) + [pl.BlockSpec(memory_space=pltpu.VMEM)],
        out_shape=[pltpu.SemaphoreType.DMA((4 * npc,)), pltpu.SemaphoreType.DMA((4 * npc,))]
        + [pltpu.HBM(a.shape, a.dtype) for a in list(words) + list(fulls)] + [jax.ShapeDtypeStruct((8, 128), F32)],
        input_output_aliases={i: 2 + i for i in range(2 * ns)},
        name=f"gather_start_{tag}", compiler_params=pltpu.CompilerParams(has_side_effects=EFFECT))(
            *_in_hbm(list(words) + list(fulls)), *deps)
    return outs[0], outs[1], outs[2:2 + ns], outs[2 + ns:2 + 2 * ns], outs[-1]


def gather_wait(ssem, rsem, words, fulls, specs, after, tag):
    ns = len(specs)

    def body(*refs):
        srcs, bufs, ssem, rsem = refs[:ns], refs[ns:2 * ns], refs[2 * ns], refs[2 * ns + 1]
        for cp in _gather_copies(srcs, bufs, specs, ssem, rsem, False):
            cp.wait_send()
        for cp in _gather_copies(srcs, bufs, specs, ssem, rsem, True):
            cp.wait_recv()

    outs = pl.pallas_call(
        body, in_specs=[HBM] * (2 * ns) + [SEM, SEM, ANY], out_specs=[HBM] * (2 * ns),
        out_shape=[pltpu.HBM(a.shape, a.dtype) for a in list(words) + list(fulls)],
        input_output_aliases={i: i for i in range(2 * ns)},
        name=f"gather_wait_{tag}", compiler_params=pltpu.CompilerParams(has_side_effects=EFFECT))(
            *words, *fulls, ssem, rsem, after)
    return outs[ns:]


def gather_forward(fulls, specs, tag):
    ns, npc = len(specs), _n_pieces(specs)

    def body(*refs):
        bufs = refs[ns:2 * ns]
        ssem, rsem = refs[2 * ns:]
        x, y, c = _coords()
        sibling = (x, y, 1 - c)
        waits = []
        p = 0
        for buf, sp in zip(bufs, specs):
            for s, n in sp.pieces:
                for t, (fx, fy) in enumerate(FLIPS):
                    chip = 4 * (x ^ fx) + 2 * (y ^ fy)
                    here = _rows(buf, sp.off(chip + c, s), n)
                    send = pltpu.make_async_remote_copy(here, here, ssem.at[t, p], rsem.at[t, p],
                                                        device_id=sibling, device_id_type=MESH)
                    send.start()
                    waits.append(send.wait_send)
                    there = _rows(buf, sp.off(chip + 1 - c, s), n)
                    waits.append(pltpu.make_async_remote_copy(here, there, ssem.at[t, p], rsem.at[t, p],
                                                              device_id=sibling, device_id_type=MESH).wait_recv)
                p += 1
        for w in waits:
            w()

    return pl.pallas_call(
        body, in_specs=[ANY] * ns, out_specs=[ANY] * ns,
        out_shape=[jax.ShapeDtypeStruct(f.shape, f.dtype) for f in fulls],
        input_output_aliases={i: i for i in range(ns)},
        scratch_shapes=[pltpu.SemaphoreType.DMA((3, npc)), pltpu.SemaphoreType.DMA((3, npc))],
        name=f"gather_forward_{tag}", compiler_params=pltpu.CompilerParams(has_side_effects=True))(*fulls)


def all_gather(shards, layer, specs, names, tag):
    placed = [pack_place(s, sp, layer, f"{tag}_{n}") for s, sp, n in zip(shards, specs, names)]
    fulls = gather_send([p[0] for p in placed], [p[1] for p in placed], specs, tag)
    return gather_forward(fulls, specs, tag)


def _pair_copies(srcs, theirs, specs, ssem, rsem):
    x, y, c = _coords()
    copies = []
    p = 0
    for src, their, sp in zip(srcs, theirs, specs):
        for s, n in sp.pieces:
            for j in range(4):
                copies.append(pltpu.make_async_remote_copy(_rows(src, sp.off(2 * j + 1 - c, s), n), _rows(their.at[j], s, n),
                                                           ssem.at[4 * p + j], rsem.at[4 * p + j],
                                                           device_id=(x, y, 1 - c), device_id_type=MESH))
            p += 1
    return copies


def _pair_shapes(specs):
    return [(4, sp.layers, sp.rows, sp.cols) for sp in specs]


def reduce_pair(grads, specs, tag, dep=None):
    ns, npc = len(specs), _n_pieces(specs)
    deps = [] if dep is None else [dep]

    def body(*refs):
        copies = _pair_copies(refs[:ns], refs[ns + len(deps):2 * ns + len(deps)], specs, *refs[2 * ns + len(deps):])
        for cp in copies:
            cp.start()
        for cp in copies:
            cp.wait()

    return pl.pallas_call(
        body, in_specs=[ANY] * (ns + len(deps)), out_specs=[ANY] * ns,
        out_shape=[jax.ShapeDtypeStruct(s, F32) for s in _pair_shapes(specs)],
        scratch_shapes=[pltpu.SemaphoreType.DMA((4 * npc,)), pltpu.SemaphoreType.DMA((4 * npc,))],
        name=f"reduce_pair_{tag}", compiler_params=pltpu.CompilerParams(has_side_effects=True))(*grads, *deps)


def pair_start(grads, specs, dep, tag):
    ns, npc = len(specs), _n_pieces(specs)
    slots = [lax.empty(s, F32) for s in _pair_shapes(specs)]
    deps = [] if dep is None else [dep]

    def body(*refs):
        ssem, rsem = refs[2 * ns + len(deps):2 * ns + len(deps) + 2]
        for cp in _pair_copies(refs[:ns], refs[ns:2 * ns], specs, ssem, rsem):
            cp.start()
        refs[-1][...] = jnp.zeros_like(refs[-1])

    outs = pl.pallas_call(
        body, in_specs=[HBM] * (2 * ns) + [ANY] * len(deps),
        out_specs=[SEM, SEM] + [HBM] * (2 * ns) + [pl.BlockSpec(memory_space=pltpu.VMEM)],
        out_shape=[pltpu.SemaphoreType.DMA((4 * npc,)), pltpu.SemaphoreType.DMA((4 * npc,))]
        + [pltpu.HBM(a.shape, a.dtype) for a in list(grads) + slots] + [jax.ShapeDtypeStruct((8, 128), F32)],
        input_output_aliases={i: 2 + i for i in range(2 * ns)},
        name=f"pair_start_{tag}", compiler_params=pltpu.CompilerParams(has_side_effects=EFFECT))(
            *_in_hbm(list(grads) + slots), *deps)
    return outs[0], outs[1], outs[2:2 + ns], outs[2 + ns:2 + 2 * ns], outs[-1]


def pair_wait(ssem, rsem, grads, slots, specs, after, tag):
    ns = len(specs)

    def body(*refs):
        for cp in _pair_copies(refs[:ns], refs[ns:2 * ns], specs, refs[2 * ns], refs[2 * ns + 1]):
            cp.wait_send()
            cp.wait_recv()

    outs = pl.pallas_call(
        body, in_specs=[HBM] * (2 * ns) + [SEM, SEM, ANY], out_specs=[HBM] * (2 * ns),
        out_shape=[pltpu.HBM(a.shape, a.dtype) for a in list(grads) + list(slots)],
        input_output_aliases={i: i for i in range(2 * ns)},
        name=f"pair_wait_{tag}", compiler_params=pltpu.CompilerParams(has_side_effects=EFFECT))(
            *grads, *slots, ssem, rsem, after)
    return outs[:ns], outs[ns:]


def pair_sum(g, r1, sp, tag):
    npc = len(sp.pieces)
    fetch_all = 4 * sp.rows * sp.cols * 4 <= (8 << 20)

    def body(g_ref, r_ref, own_ref, words_ref, gbuf, sem):
        l, j = pl.program_id(0), pl.program_id(1)
        x, y, c = _coords()

        def fetch(chip, slot):
            copies = [pltpu.make_async_copy(g_ref.at[l, pl.ds(pl.multiple_of(sp.off(2 * chip + c, s), 8), n), :],
                                            gbuf.at[slot, pl.ds(s, n), :], sem.at[slot, i])
                      for i, (s, n) in enumerate(sp.pieces)]
            for cp in copies:
                cp.start()
            return copies

        if fetch_all:
            @pl.when(j == 0)
            def _():
                for cp in [cp for chip in range(4) for cp in fetch(chip, chip)]:
                    cp.wait()

            mine = gbuf[j]
        else:
            for cp in fetch(j, 0):
                cp.wait()
            mine = gbuf[0]
        p = mine + r_ref[...]
        words_ref[...] = sp.to_words(p)

        @pl.when(j == 2 * x + y)
        def _():
            own_ref[...] = p

    return pl.pallas_call(
        body, grid=(sp.layers, 4),
        in_specs=[ANY, pl.BlockSpec((None, None, sp.rows, sp.cols), lambda l, j: (j, l, 0, 0))],
        out_specs=[pl.BlockSpec((None, sp.rows, sp.cols), lambda l, j: (l, 0, 0)),
                   pl.BlockSpec((None, None, sp.rows, sp.wcols), lambda l, j: (j, l, 0, 0))],
        out_shape=[jax.ShapeDtypeStruct((sp.layers, sp.rows, sp.cols), F32),
                   jax.ShapeDtypeStruct((4, sp.layers, sp.rows, sp.wcols), F32)],
        scratch_shapes=[pltpu.VMEM((4 if fetch_all else 1, sp.rows, sp.cols), F32), pltpu.SemaphoreType.DMA((4, npc))],
        name=f"pair_sum_{tag}", compiler_params=_params(("arbitrary", "arbitrary")))(g, r1)


def _chip_copies(srcs, dsts, ssem, rsem):
    x, y, c = _coords()
    copies = []
    for i, (src, dst) in enumerate(zip(srcs, dsts)):
        for t, (fx, fy) in enumerate(FLIPS):
            tx, ty = x ^ fx, y ^ fy
            copies.append(pltpu.make_async_remote_copy(src.at[2 * tx + ty], dst.at[t], ssem.at[3 * i + t], rsem.at[3 * i + t],
                                                       device_id=(tx, ty, c), device_id_type=MESH))
    return copies


def _slot_shapes(words):
    return [(3,) + w.shape[1:] for w in words]


def reduce_chips(words, specs, tag):
    ns = len(specs)

    def body(*refs):
        copies = _chip_copies(refs[:ns], refs[ns:2 * ns], *refs[2 * ns:])
        for cp in copies:
            cp.start()
        for cp in copies:
            cp.wait()

    return pl.pallas_call(
        body, in_specs=[ANY] * ns, out_specs=[ANY] * ns,
        out_shape=[jax.ShapeDtypeStruct(s, F32) for s in _slot_shapes(words)],
        scratch_shapes=[pltpu.SemaphoreType.DMA((3 * ns,)), pltpu.SemaphoreType.DMA((3 * ns,))],
        name=f"reduce_chips_{tag}", compiler_params=pltpu.CompilerParams(has_side_effects=True))(*words)


def chips_start(words, specs, tag):
    ns = len(specs)
    slots = [lax.empty(s, F32) for s in _slot_shapes(words)]

    def body(*refs):
        ssem, rsem = refs[2 * ns:2 * ns + 2]
        for cp in _chip_copies(refs[:ns], refs[ns:2 * ns], ssem, rsem):
            cp.start()
        refs[-1][...] = jnp.zeros_like(refs[-1])

    outs = pl.pallas_call(
        body, in_specs=[HBM] * (2 * ns),
        out_specs=[SEM, SEM] + [HBM] * (2 * ns) + [pl.BlockSpec(memory_space=pltpu.VMEM)],
        out_shape=[pltpu.SemaphoreType.DMA((3 * ns,)), pltpu.SemaphoreType.DMA((3 * ns,))]
        + [pltpu.HBM(a.shape, a.dtype) for a in list(words) + slots] + [jax.ShapeDtypeStruct((8, 128), F32)],
        input_output_aliases={i: 2 + i for i in range(2 * ns)},
        name=f"chips_start_{tag}", compiler_params=pltpu.CompilerParams(has_side_effects=EFFECT))(
            *_in_hbm(list(words) + slots))
    return outs[0], outs[1], outs[2:2 + ns], outs[2 + ns:2 + 2 * ns], outs[-1]


def chips_wait(ssem, rsem, words, slots, specs, after, tag):
    ns = len(specs)

    def body(*refs):
        for cp in _chip_copies(refs[:ns], refs[ns:2 * ns], refs[2 * ns], refs[2 * ns + 1]):
            cp.wait_send()
            cp.wait_recv()

    outs = pl.pallas_call(
        body, in_specs=[HBM] * (2 * ns) + [SEM, SEM, ANY], out_specs=[HBM] * (2 * ns),
        out_shape=[pltpu.HBM(a.shape, a.dtype) for a in list(words) + list(slots)],
        input_output_aliases={i: i for i in range(2 * ns)},
        name=f"chips_wait_{tag}", compiler_params=pltpu.CompilerParams(has_side_effects=EFFECT))(
            *words, *slots, ssem, rsem, after)
    return outs[ns:]


def sum_chips(own, r2, sp, tag):
    def body(own_ref, r_ref, o_ref):
        o_ref[...] = ((own_ref[...] + sp.from_words(r_ref[0])) + sp.from_words(r_ref[1])) + sp.from_words(r_ref[2])

    blk = pl.BlockSpec((None, sp.rows, sp.cols), lambda l: (l, 0, 0))
    return pl.pallas_call(
        body, grid=(sp.layers,), in_specs=[blk, pl.BlockSpec((3, None, sp.rows, sp.wcols), lambda l: (0, l, 0, 0))],
        out_specs=blk, out_shape=jax.ShapeDtypeStruct((sp.layers, sp.rows, sp.cols), F32),
        name=f"sum_chips_{tag}", compiler_params=_params(("arbitrary",)))(own, r2)


def reduce_scatter_start(grads, specs, names, dep, tag):
    theirs = reduce_pair(grads, specs, tag, dep)
    sums = [pair_sum(g, r1, sp, f"{tag}_{n}") for g, r1, sp, n in zip(grads, theirs, specs, names)]
    ssem, rsem, words, slots, token = chips_start([s[1] for s in sums], specs, tag)
    return (ssem, rsem, words, slots, [s[0] for s in sums]), token


def reduce_scatter_finish(state, after, specs, tag):
    ssem, rsem, words, slots, own = state
    return list(zip(own, chips_wait(ssem, rsem, words, slots, specs, after, tag)))


def reduce_scatter(grads, specs, names, tag):
    theirs = reduce_pair(grads, specs, tag)
    sums = [pair_sum(g, r1, sp, f"{tag}_{n}") for g, r1, sp, n in zip(grads, theirs, specs, names)]
    return list(zip([s[0] for s in sums], reduce_chips([s[1] for s in sums], specs, tag)))


def _adamw_math(w, g, m, v):
    c1 = 1.0 - ADAM_B1 ** ADAM_STEP
    c2 = 1.0 - ADAM_B2 ** ADAM_STEP
    m2 = ADAM_B1 * m + (1.0 - ADAM_B1) * g
    v2 = ADAM_B2 * v + (1.0 - ADAM_B2) * (g * g)
    return -ADAM_LR * ((m2 / c1) / (jnp.sqrt(v2 / c2) + ADAM_EPS) + ADAM_WD * w), m2, v2


def adamw(w, g, m, v, name):
    shape = w.shape
    cols = shape[-1]
    rows = math.prod(shape[:-1])
    tr = rows
    while tr * cols * 4 > (1 << 20) and tr % 16 == 0:
        tr //= 2

    def body(w_ref, g_ref, m_ref, v_ref, d_ref, nm_ref, nv_ref):
        d_ref[...], nm_ref[...], nv_ref[...] = _adamw_math(w_ref[...], g_ref[...], m_ref[...], v_ref[...])

    blk = pl.BlockSpec((tr, cols), lambda i: (i, 0))
    outs = pl.pallas_call(
        body, grid=(rows // tr,), in_specs=[blk] * 4, out_specs=[blk] * 3,
        out_shape=[jax.ShapeDtypeStruct((rows, cols), F32)] * 3,
        name=f"adamw_{name}", compiler_params=_params(("arbitrary",)))(
            *[a.reshape(rows, cols) for a in (w, g, m, v)])
    return [o.reshape(shape) for o in outs]


def adamw_layer(w, sums, m, v, sp, l, prev, dep, name):
    _, rows, cols = w.shape
    tc = _col_tile(cols)
    twc = tc // 2 if sp.packed else tc
    extra = ([] if prev is None else list(prev)) + ([] if dep is None else [dep])

    def body(w_ref, own_ref, r_ref, m_ref, v_ref, *rest):
        g_ref, d_ref, nm_ref, nv_ref = rest[-4:]
        g = ((own_ref[...] + sp.from_words(r_ref[0])) + sp.from_words(r_ref[1])) + sp.from_words(r_ref[2])
        g_ref[...] = g
        d_ref[...], nm_ref[...], nv_ref[...] = _adamw_math(w_ref[...], g, m_ref[...], v_ref[...])

    blk = pl.BlockSpec((None, rows, tc), lambda n: (l, 0, n))
    return pl.pallas_call(
        body, grid=(cols // tc,),
        in_specs=[blk, pl.BlockSpec((None, rows, tc), lambda n: (0, 0, n)),
                  pl.BlockSpec((3, None, rows, twc), lambda n: (0, 0, 0, n)), blk, blk] + [ANY] * len(extra),
        out_specs=[blk] * 4, out_shape=[jax.ShapeDtypeStruct(w.shape, F32)] * 4,
        input_output_aliases={} if prev is None else {5 + i: i for i in range(4)},
        name=f"adamw_{name}_l{l}", compiler_params=_params(("arbitrary",)))(w, sums[0], sums[1], m, v, *extra)


WEIGHTS = ("pre_norm_g", "w_in", "gm_ln_g", "gm_ln_b", "gm_ws", "gm_bs", "mla_q_norm_g", "mla_w_uq", "mla_kv_norm_g",
           "mla_w_ukv", "lru_conv_w", "lru_conv_b", "lru_w_a", "lru_b_a", "lru_w_x", "lru_b_x", "lru_lambda",
           "w_proj_a", "w_proj_b", "w_proj_c", "w_out", "post_norm_g")
SHARDED = ("w_in", "mla_w_uq", "mla_w_ukv", "lru_conv_w", "w_proj_a", "w_proj_b", "w_proj_c", "w_out")
REPLICATED = tuple(n for n in WEIGHTS if n not in SHARDED)


def _step(x, target, wts, ms, vs):
    t12 = lambda a: jnp.swapaxes(a, 1, 2)
    names = list(WEIGHT_SPECS)
    specs = [WEIGHT_SPECS[n] for n in names]
    tabs = _rope_tables()
    own = {"w_in_t": t12(wts["w_in"]), "wq": t12(wts["mla_w_uq"]), "wkv": t12(wts["mla_w_ukv"]),
           "conv": jnp.pad(t12(wts["lru_conv_w"]), ((0, 0), (0, 0), (0, 124))),
           "w_proj_a": wts["w_proj_a"], "w_proj_b": wts["w_proj_b"], "w_proj_c": wts["w_proj_c"], "w_out": wts["w_out"]}
    first, rest = ["w_in_t"], [n for n in names if n != "w_in_t"]
    sfirst, srest = [WEIGHT_SPECS[n] for n in first], [WEIGHT_SPECS[n] for n in rest]

    w = {n: wts[n] for n in REPLICATED}
    w["kv_g384"] = jnp.concatenate([wts["mla_kv_norm_g"], jnp.ones((L, 128), F32)], axis=1)

    def layer_weights(ns, words):
        gw = dict(zip(ns, words))
        gw["wq"] = gw["wq"].reshape(HEADS, 192, 384)
        gw["wkv"] = gw["wkv"].reshape(HEADS, 256, 128)
        gw["conv"] = gw["conv"][0, :, :4].T
        return gw

    place = lambda l, dep: {n: pack_place(own[n], WEIGHT_SPECS[n], l, f"w{l}_{n}", dep) for n in names}
    placed = [place(0, None)]
    words_of = lambda l, ns: [placed[l][n][0] for n in ns]
    bufs_of = lambda l, ns: [placed[l][n][1] for n in ns]
    later = {}

    ssem_a, rsem_a, wthru_a, fthru_a, token_a = gather_start(words_of(0, first), bufs_of(0, first), sfirst, None, "w0a")
    placed.append(place(1, token_a))
    win0 = gather_forward(gather_wait(ssem_a, rsem_a, wthru_a, fthru_a, sfirst, placed[1]["w_in_t"][0], "w0a"), sfirst, "w0a")
    ssem_b, rsem_b, wthru_b, fthru_b, token_b = gather_start(words_of(0, rest), bufs_of(0, rest), srest, win0[0], "w0b")

    def fwd0_mid(ya):
        rest0 = gather_forward(gather_wait(ssem_b, rsem_b, wthru_b, fthru_b, srest, ya, "w0b"), srest, "w0b")
        later["w1"] = gather_start(words_of(1, names), bufs_of(1, names), specs, rest0[0], "w1")
        later["gw0"] = layer_weights(first + rest, list(win0) + list(rest0))
        return later["gw0"], later["w1"][4]

    x1, saved0 = _layer_fwd(x, 0, w, {"w_in_t": win0[0]}, tabs, dep=token_b, mid=fwd0_mid)
    ssem1, rsem1, wthru1, fthru1, _ = later["w1"]
    words1 = gather_forward(gather_wait(ssem1, rsem1, wthru1, fthru1, specs, x1, "w1"), specs, "w1")
    gw0, gw1 = later["gw0"], layer_weights(names, words1)
    x2, saved1 = _layer_fwd(x1, 1, w, gw1, tabs)
    loss, dx2 = loss_head(x2, target)

    def bwd1_mid(gg, last):
        later["p1b"] = pair_start([gg[n] for n in rest], srest, last, "g1b")
        return later["p1b"][4]

    dx1, gg1, g1 = _layer_bwd(dx2, 1, w, gw1, tabs, saved1, mid=bwd1_mid)
    grads1b, theirs1b = pair_wait(*later["p1b"][:4], srest, dx1, "g1b")
    p1a = pair_start([gg1["w_in_t"]], sfirst, theirs1b[0], "g1a")

    def bwd0_early(last):
        grads1a, theirs1a = pair_wait(*p1a[:4], sfirst, last, "g1a")
        mine = dict(zip(first + rest, list(grads1a) + list(grads1b)))
        theirs = dict(zip(first + rest, list(theirs1a) + list(theirs1b)))
        sums = [pair_sum(mine[n], theirs[n], WEIGHT_SPECS[n], f"g1_{n}") for n in names]
        ssem, rsem, words, slots, token = chips_start([s[1] for s in sums], specs, "g1")
        later["g1"] = (ssem, rsem, words, slots, [s[0] for s in sums])
        return token

    def bwd0_mid(gg, last):
        later["g0b"], token = reduce_scatter_start([gg[n] for n in rest], srest, rest, last, "g0b")
        return token

    dx0, gg0, g0 = _layer_bwd(dx1, 0, w, gw0, tabs, saved0, dep=p1a[4], early=bwd0_early, mid=bwd0_mid)
    s1 = dict(zip(names, reduce_scatter_finish(later["g1"], dx0, specs, "g1")))
    s0 = dict(zip(rest, reduce_scatter_finish(later["g0b"], dx0, srest, "g0b")))
    rep_flat = jnp.concatenate([jnp.stack([g0[n], g1[n]]).reshape(-1) for n in REPLICATED])
    rep_flat = jnp.pad(rep_flat, (0, REP_ROWS * NDEV * D - rep_flat.shape[0])).reshape(1, REP_ROWS * NDEV, D)
    state_a, token_g = reduce_scatter_start([gg0["w_in_t"], rep_flat], sfirst + [REP_SPEC], first + ["rep"], None, "g0a")

    keys = {"w_in": "w_in_t", "mla_w_uq": "wq", "mla_w_ukv": "wkv",
            "w_proj_a": "w_proj_a", "w_proj_b": "w_proj_b", "w_proj_c": "w_proj_c", "w_out": "w_out"}
    transposed = ("w_in", "mla_w_uq", "mla_w_ukv")
    state_of = lambda n: [own[keys[n]], t12(ms[n]), t12(vs[n])] if n in transposed else [wts[n], ms[n], vs[n]]

    def update(n, l, sums, prev, dep):
        wl, ml, vl = state_of(n)
        return adamw_layer(wl, sums[keys[n]], ml, vl, WEIGHT_SPECS[keys[n]], l, prev, dep, n)

    upd = {n: update(n, 1, s1, None, token_g) for n in keys}
    for n in keys:
        if n != "w_in":
            upd[n] = update(n, 0, s0, upd[n], None)
    s0["w_in_t"], rep_parts = reduce_scatter_finish(state_a, upd["w_out"][0], sfirst + [REP_SPEC], "g0a")
    upd["w_in"] = update("w_in", 0, s0, upd["w_in"], None)
    rep_sum = sum_chips(*rep_parts, REP_SPEC, "rep")
    rep_full = all_gather([rep_sum], 0, [REP_SPEC], ["rep"], "rep")[0].reshape(-1)

    out = {n: [t12(r) for r in upd[n]] if n in transposed else upd[n] for n in keys}
    conv_sp = WEIGHT_SPECS["conv"]
    g_conv = t12(jnp.concatenate([sum_chips(*s0["conv"], conv_sp, "conv0"), sum_chips(*s1["conv"], conv_sp, "conv1")])[:, :, :4])
    out["lru_conv_w"] = [g_conv] + adamw(wts["lru_conv_w"], g_conv, ms["lru_conv_w"], vs["lru_conv_w"], "lru_conv_w")
    at = 0
    for n in REPLICATED:
        size = math.prod(wts[n].shape)
        g = rep_full[at:at + size].reshape(wts[n].shape)
        out[n] = [g] + adamw(wts[n], g, ms[n], vs[n], n)
        at += size

    loss = lax.psum(loss, ("x", "y", "c"))
    return (loss, dx0[None], *[out[n][k] for k in range(4) for n in WEIGHTS])


def kernel(x, pre_norm_g, w_in, gm_ln_g, gm_ln_b, gm_ws, gm_bs, mla_q_norm_g, mla_w_uq, mla_kv_norm_g, mla_w_ukv, lru_conv_w, lru_conv_b, lru_w_a, lru_b_a, lru_w_x, lru_b_x, lru_lambda, w_proj_a, w_proj_b, w_proj_c, w_out, post_norm_g, loss_target, m_pre_norm_g, m_w_in, m_gm_ln_g, m_gm_ln_b, m_gm_ws, m_gm_bs, m_mla_q_norm_g, m_mla_w_uq, m_mla_kv_norm_g, m_mla_w_ukv, m_lru_conv_w, m_lru_conv_b, m_lru_w_a, m_lru_b_a, m_lru_w_x, m_lru_b_x, m_lru_lambda, m_w_proj_a, m_w_proj_b, m_w_proj_c, m_w_out, m_post_norm_g, v_pre_norm_g, v_w_in, v_gm_ln_g, v_gm_ln_b, v_gm_ws, v_gm_bs, v_mla_q_norm_g, v_mla_w_uq, v_mla_kv_norm_g, v_mla_w_ukv, v_lru_conv_w, v_lru_conv_b, v_lru_w_a, v_lru_b_a, v_lru_w_x, v_lru_b_x, v_lru_lambda, v_w_proj_a, v_w_proj_b, v_w_proj_c, v_w_out, v_post_norm_g):
    wts = dict(zip(WEIGHTS, (pre_norm_g, w_in, gm_ln_g, gm_ln_b, gm_ws, gm_bs, mla_q_norm_g, mla_w_uq, mla_kv_norm_g,
                             mla_w_ukv, lru_conv_w, lru_conv_b, lru_w_a, lru_b_a, lru_w_x, lru_b_x, lru_lambda,
                             w_proj_a, w_proj_b, w_proj_c, w_out, post_norm_g)))
    ms = dict(zip(WEIGHTS, (m_pre_norm_g, m_w_in, m_gm_ln_g, m_gm_ln_b, m_gm_ws, m_gm_bs, m_mla_q_norm_g, m_mla_w_uq,
                            m_mla_kv_norm_g, m_mla_w_ukv, m_lru_conv_w, m_lru_conv_b, m_lru_w_a, m_lru_b_a, m_lru_w_x,
                            m_lru_b_x, m_lru_lambda, m_w_proj_a, m_w_proj_b, m_w_proj_c, m_w_out, m_post_norm_g)))
    vs = dict(zip(WEIGHTS, (v_pre_norm_g, v_w_in, v_gm_ln_g, v_gm_ln_b, v_gm_ws, v_gm_bs, v_mla_q_norm_g, v_mla_w_uq,
                            v_mla_kv_norm_g, v_mla_w_ukv, v_lru_conv_w, v_lru_conv_b, v_lru_w_a, v_lru_b_a, v_lru_w_x,
                            v_lru_b_x, v_lru_lambda, v_w_proj_a, v_w_proj_b, v_w_proj_c, v_w_out, v_post_norm_g)))
    return _step(x[0], loss_target[0], wts, ms, vs)
```

```python
import functools
import math

import jax
import jax.numpy as jnp
from jax import lax
from jax.experimental import pallas as pl
from jax.experimental.pallas import tpu as pltpu

F32 = jnp.float32
BF16 = jnp.bfloat16

T = 2048
D = 1024
L = 2
NDEV = 8
EPS = 1e-6
CHUNK_SHIFT = 6
HEADS = 8
QK = 192
LRU_W = 1280
LRU_TILE = 640
N_IN = 10432
SHARD = N_IN // NDEV
OFF_U, OFF_V, OFF_ZA, OFF_CQ, OFF_CKV, OFF_ZB = 0, 1024, 2048, 3072, 3456, 3840
OFF_XC, OFF_ZC, OFF_GA, OFF_GB, OFF_GC = 5120, 6400, 7680, 8704, 9728
NPAD = 10752
PAD1_AT, PAD1 = 3776, 64
PAD2_AT, PAD2 = 4800, 256
WIN_PIECES = ((0, 888), (888, 280), (1168, 136))
VMEM_LIMIT = 60 * 1024 * 1024

ADAM_LR, ADAM_B1, ADAM_B2, ADAM_EPS, ADAM_WD, ADAM_STEP = 0.001, 0.9, 0.999, 1e-08, 0.01, 10

_NN = (((1,), (0,)), ((), ()))
_NT = (((1,), (1,)), ((), ()))
_TN = (((0,), (0,)), ((), ()))


def _dg(a, b, dims):
    return lax.dot_general(a.astype(BF16), b.astype(BF16), dims, preferred_element_type=F32)


@jax.custom_vjp
def dot_nn(a, b):
    return _dg(a, b, _NN)


def _nn_fwd(a, b):
    return _dg(a, b, _NN), (a, b)


def _nn_bwd(res, g):
    a, b = res
    return _dg(g, b, _NT).astype(a.dtype), _dg(a, g, _TN).astype(b.dtype)


dot_nn.defvjp(_nn_fwd, _nn_bwd)


@jax.custom_vjp
def dot_nt(a, b):
    return _dg(a, b, _NT)


def _nt_fwd(a, b):
    return _dg(a, b, _NT), (a, b)


def _nt_bwd(res, g):
    a, b = res
    return _dg(g, b, _NN).astype(a.dtype), _dg(g, a, _TN).astype(b.dtype)


dot_nt.defvjp(_nt_fwd, _nt_bwd)


def _params(sem=None):
    return pltpu.CompilerParams(dimension_semantics=sem, vmem_limit_bytes=VMEM_LIMIT)


def _sigmoid(x):
    return 1.0 / (1.0 + jnp.exp(-x))


def _silu(x):
    return x * _sigmoid(x)


def _rms(x, g):
    ms = jnp.mean(x * x, axis=-1, keepdims=True)
    return x * lax.rsqrt(ms + EPS) * g


def _acc(ref, val, first):
    @pl.when(first)
    def _():
        ref[...] = val

    @pl.when(jnp.logical_not(first))
    def _():
        ref[...] += val


ANY = pl.BlockSpec(memory_space=pl.ANY)


INPROJ_TN = 768


def inproj_fwd(x, g, wt, l, dep=None):
    tn = INPROJ_TN

    def body(x_ref, g_ref, w_ref, *rest):
        proj_ref, h_ref = rest[-2:]

        @pl.when(pl.program_id(0) == 0)
        def _():
            h_ref[...] = _rms(x_ref[...], g_ref[...]).astype(BF16)

        proj_ref[...] = lax.dot_general(h_ref[...], _unpack(w_ref[...]).astype(BF16), _NT, preferred_element_type=F32)

    deps = [] if dep is None else [dep]
    return pl.pallas_call(
        body, grid=(NPAD // tn,),
        in_specs=[pl.BlockSpec((T, D), lambda j: (0, 0)), pl.BlockSpec((1, D), lambda j: (0, 0)),
                  pl.BlockSpec((None, tn, D // 2), lambda j: (0, j, 0))] + [ANY] * len(deps),
        out_specs=[pl.BlockSpec((T, tn), lambda j: (0, j)), pl.BlockSpec((T, D), lambda j: (0, 0))],
        out_shape=[jax.ShapeDtypeStruct((T, NPAD), F32), jax.ShapeDtypeStruct((T, D), BF16)],
        name=f"inproj_fwd_l{l}", compiler_params=_params(("arbitrary",)))(x, g, wt, *deps)


def inproj_bwd(dproj, h, wt, l, dep=None):
    tn = INPROJ_TN
    deps = [] if dep is None else [dep]

    def body(dp_ref, h_ref, w_ref, *rest):
        dwt_ref, dh_ref = rest[-2:]
        dp = dp_ref[...]
        dwt_ref[...] = lax.dot_general(dp, h_ref[...], _TN, preferred_element_type=F32)
        contrib = lax.dot_general(dp, _unpack(w_ref[...]).astype(BF16), _NN, preferred_element_type=F32)
        _acc(dh_ref, contrib, pl.program_id(0) == 0)

    return pl.pallas_call(
        body, grid=(NPAD // tn,),
        in_specs=[pl.BlockSpec((T, tn), lambda j: (0, j)), pl.BlockSpec((T, D), lambda j: (0, 0)),
                  pl.BlockSpec((None, tn, D // 2), lambda j: (0, j, 0))] + [ANY] * len(deps),
        out_specs=[pl.BlockSpec((None, tn, D), lambda j: (0, j, 0)), pl.BlockSpec((T, D), lambda j: (0, 0))],
        out_shape=[jax.ShapeDtypeStruct((1, NPAD, D), F32), jax.ShapeDtypeStruct((T, D), F32)],
        name=f"inproj_bwd_l{l}", compiler_params=_params(("arbitrary",)))(dproj, h, wt, *deps)


def prenorm_bwd(x, g, dh, dxn, l):
    tm = 256

    def body(x_ref, g_ref, dh_ref, dxn_ref, dx_ref, dg_ref):
        _, vjp = jax.vjp(_rms, x_ref[...], g_ref[...])
        dx, dg = vjp(dh_ref[...])
        dx_ref[...] = dx + dxn_ref[...]
        _acc(dg_ref, dg, pl.program_id(0) == 0)

    tok = pl.BlockSpec((tm, D), lambda i: (i, 0))
    vec = pl.BlockSpec((1, D), lambda i: (0, 0))
    return pl.pallas_call(
        body, grid=(T // tm,), in_specs=[tok, vec, tok, tok], out_specs=[tok, vec],
        out_shape=[jax.ShapeDtypeStruct((T, D), F32), jax.ShapeDtypeStruct((1, D), F32)],
        name=f"prenorm_bwd_l{l}", compiler_params=_params(("arbitrary",)))(x, g, dh, dxn)


def _gmlp_tile(u, v, z, ln_g, ln_b, ws, bs):
    mu = jnp.mean(v, axis=-1, keepdims=True)
    vc = v - mu
    var = jnp.mean(vc * vc, axis=-1, keepdims=True)
    vn = vc * lax.rsqrt(var + EPS) * ln_g + ln_b
    qi = lax.broadcasted_iota(jnp.int32, (128, 128), 0) >> CHUNK_SHIFT
    kj = lax.broadcasted_iota(jnp.int32, (128, 128), 1) >> CHUNK_SHIFT
    mask = kj <= qi
    outs = []
    for g in range(4):
        wm = jnp.where(mask, ws[g], 0.0)
        outs.append(dot_nn(wm, vn[:, 256 * g:256 * (g + 1)]) + bs[g])
    sv = jnp.concatenate(outs, axis=1)
    return u * sv * _silu(z)


def _gmlp_specs():
    blk = lambda c: pl.BlockSpec((128, 1024), lambda n, c=c: (n, c))
    vec = pl.BlockSpec((1, 1024), lambda n: (0, 0))
    return [blk(0), blk(1), blk(2), vec, vec,
            pl.BlockSpec((4, 128, 128), lambda n: (0, 0, 0)), pl.BlockSpec((4, 128, 1), lambda n: (0, 0, 0))]


def gmlp_fwd(proj, ln_g, ln_b, ws, bs, l):
    def body(u_ref, v_ref, z_ref, g_ref, b_ref, ws_ref, bs_ref, y_ref):
        y_ref[...] = _gmlp_tile(u_ref[...], v_ref[...], z_ref[...], g_ref[...], b_ref[...],
                                [ws_ref[g] for g in range(4)], [bs_ref[g] for g in range(4)])

    return pl.pallas_call(
        body, grid=(T // 128,), in_specs=_gmlp_specs(),
        out_specs=pl.BlockSpec((128, 1024), lambda n: (n, 0)),
        out_shape=jax.ShapeDtypeStruct((T, 1024), F32),
        name=f"gmlp_fwd_l{l}", compiler_params=_params(("arbitrary",)))(proj, proj, proj, ln_g, ln_b, ws, bs)


def gmlp_bwd(proj, ln_g, ln_b, ws, bs, dy, dproj, l):
    def body(u_ref, v_ref, z_ref, g_ref, b_ref, ws_ref, bs_ref, dy_ref, _, dseg_ref, dg_ref, db_ref, dws_ref, dbs_ref):
        first = pl.program_id(0) == 0
        _, vjp = jax.vjp(_gmlp_tile, u_ref[...], v_ref[...], z_ref[...], g_ref[...], b_ref[...],
                         [ws_ref[g] for g in range(4)], [bs_ref[g] for g in range(4)])
        du, dv, dz, dg, db, dws, dbs = vjp(dy_ref[...])
        dseg_ref[:, 0:1024] = du.astype(BF16)
        dseg_ref[:, 1024:2048] = dv.astype(BF16)
        dseg_ref[:, 2048:3072] = dz.astype(BF16)
        _acc(dg_ref, dg, first)
        _acc(db_ref, db, first)
        for g in range(4):
            _acc(dws_ref.at[g], dws[g], first)
            _acc(dbs_ref.at[g], dbs[g], first)

    vec = pl.BlockSpec((1, 1024), lambda n: (0, 0))
    return pl.pallas_call(
        body, grid=(T // 128,), in_specs=_gmlp_specs() + [pl.BlockSpec((128, 1024), lambda n: (n, 0)), ANY],
        out_specs=[pl.BlockSpec((128, 3072), lambda n: (n, OFF_U // 3072)), vec, vec,
                   pl.BlockSpec((4, 128, 128), lambda n: (0, 0, 0)), pl.BlockSpec((4, 128, 1), lambda n: (0, 0, 0))],
        out_shape=[jax.ShapeDtypeStruct((T, NPAD), BF16), jax.ShapeDtypeStruct((1, 1024), F32),
                   jax.ShapeDtypeStruct((1, 1024), F32), jax.ShapeDtypeStruct((4, 128, 128), F32),
                   jax.ShapeDtypeStruct((4, 128, 1), F32)],
        input_output_aliases={8: 0},
        name=f"gmlp_bwd_l{l}", compiler_params=_params(("arbitrary",)))(proj, proj, proj, ln_g, ln_b, ws, bs, dy, dproj)


QKV_TM = 256


def _qkv_tile(cq, ckvr, qg, kvg, wq, wkv, ctab, stab):
    tm = cq.shape[0]
    cqn = _rms(cq, qg)
    lane = lax.broadcasted_iota(jnp.int32, ckvr.shape, 1)
    iskv = lane < 256
    ms = jnp.sum(jnp.where(iskv, ckvr * ckvr, 0.0), axis=-1, keepdims=True) * (1.0 / 256)
    lm = jnp.where(iskv, ckvr * lax.rsqrt(ms + EPS) * kvg, ckvr)
    r = lax.broadcasted_iota(jnp.int32, (64, 128), 0)
    c = lax.broadcasted_iota(jnp.int32, (64, 128), 1)
    eye = jnp.where(c == r, 1.0, 0.0)
    eye_sw = jnp.where(c == ((r + 32) & 63), 1.0, 0.0)
    z64 = jnp.zeros((64, 256), F32)
    z128 = jnp.zeros((128, 128), F32)
    rk_rope = jnp.concatenate([z64, eye], axis=1)
    rk_sw = jnp.concatenate([jnp.zeros((128, 384), F32), jnp.concatenate([z64, eye_sw], axis=1)], axis=0)
    k_sw = dot_nt(lm, rk_sw) * stab
    qs, ks, vs = [], [], []
    for h in range(HEADS):
        wn, w1, w2 = wq[h]
        wk, wv = wkv[h]
        wq_h = jnp.concatenate([wn, w1, w2], axis=0)
        wq_sw = jnp.concatenate([jnp.zeros((128, 384), F32), w2, w1], axis=0)
        qs.append(dot_nt(cqn, wq_h) * ctab + dot_nt(cqn, wq_sw) * stab)
        rk_h = jnp.concatenate([jnp.concatenate([wk, z128], axis=1), rk_rope], axis=0)
        ks.append(dot_nt(lm, rk_h) * ctab + k_sw)
        vs.append(dot_nt(lm, jnp.concatenate([wv, z128], axis=1)))
    return qs, ks, vs


def _qkv_in_specs():
    tm = QKV_TM
    return [pl.BlockSpec((tm, 384), lambda i: (i, OFF_CQ // 384)), pl.BlockSpec((tm, 384), lambda i: (i, OFF_CKV // 384)),
            pl.BlockSpec((1, 384), lambda i: (0, 0)), pl.BlockSpec((1, 384), lambda i: (0, 0)),
            pl.BlockSpec((HEADS, 192, 384), lambda i: (0, 0, 0)), pl.BlockSpec((HEADS, 256, 128), lambda i: (0, 0, 0)),
            pl.BlockSpec((tm, 192), lambda i: (i, 0)), pl.BlockSpec((tm, 192), lambda i: (i, 0))]


def _qkv_weights(wq_ref, wkv_ref):
    wq = [(wq_ref[h, 0:128, :], wq_ref[h, 128:160, :], wq_ref[h, 160:192, :]) for h in range(HEADS)]
    wkv = [(_unpack(wkv_ref[h, 0:128, :]), _unpack(wkv_ref[h, 128:256, :])) for h in range(HEADS)]
    return wq, wkv


def qkv_fwd(proj, qg, kvg, wq, wkv, ctab, stab, l, dep=None):
    tm = QKV_TM
    deps = [] if dep is None else [dep]

    def body(cq_ref, ckvr_ref, qg_ref, kvg_ref, wq_ref, wkv_ref, c_ref, s_ref, *rest):
        q_ref, k_ref, v_ref = rest[-3:]
        wq_l, wkv_l = _qkv_weights(wq_ref, wkv_ref)
        qs, ks, vs = _qkv_tile(cq_ref[...], ckvr_ref[...], qg_ref[...], kvg_ref[...], wq_l, wkv_l, c_ref[...], s_ref[...])
        for h in range(HEADS):
            q_ref[h] = qs[h]
            k_ref[h] = ks[h]
            v_ref[h] = vs[h]

    return pl.pallas_call(
        body, grid=(T // tm,), in_specs=_qkv_in_specs() + [ANY] * len(deps),
        out_specs=[pl.BlockSpec((HEADS, tm, QK), lambda i: (0, i, 0)), pl.BlockSpec((HEADS, tm, QK), lambda i: (0, i, 0)),
                   pl.BlockSpec((HEADS, tm, 128), lambda i: (0, i, 0))],
        out_shape=[jax.ShapeDtypeStruct((HEADS, T, QK), F32), jax.ShapeDtypeStruct((HEADS, T, QK), F32),
                   jax.ShapeDtypeStruct((HEADS, T, 128), F32)],
        name=f"qkv_fwd_l{l}", compiler_params=_params(("arbitrary",)))(proj, proj, qg, kvg, wq, wkv, ctab, stab, *deps)


def qkv_bwd(proj, qg, kvg, wq, wkv, ctab, stab, dq, dk, dv, dproj, l):
    tm = QKV_TM

    def body(cq_ref, ckvr_ref, qg_ref, kvg_ref, wq_ref, wkv_ref, c_ref, s_ref, dq_ref, dk_ref, dv_ref, _,
             dseg_ref, dqg_ref, dkvg_ref, dwq_ref, dwkv_ref):
        first = pl.program_id(0) == 0
        wq_l, wkv_l = _qkv_weights(wq_ref, wkv_ref)
        c_tab, s_tab = c_ref[...], s_ref[...]
        fn = lambda cq, ckvr, qg_, kvg_, wq_, wkv_: _qkv_tile(cq, ckvr, qg_, kvg_, wq_, wkv_, c_tab, s_tab)
        _, vjp = jax.vjp(fn, cq_ref[...], ckvr_ref[...], qg_ref[...], kvg_ref[...], wq_l, wkv_l)
        cts = ([dq_ref[h] for h in range(HEADS)], [dk_ref[h] for h in range(HEADS)], [dv_ref[h] for h in range(HEADS)])
        dcq, dckvr, dqg, dkvg, dwq, dwkv = vjp(cts)
        dseg_ref[:, 0:384] = dcq.astype(BF16)
        dseg_ref[:, 384:768] = dckvr.astype(BF16)
        _acc(dqg_ref, dqg, first)
        _acc(dkvg_ref, dkvg, first)
        for h in range(HEADS):
            _acc(dwq_ref.at[h, 0:128, :], dwq[h][0], first)
            _acc(dwq_ref.at[h, 128:160, :], dwq[h][1], first)
            _acc(dwq_ref.at[h, 160:192, :], dwq[h][2], first)
            _acc(dwkv_ref.at[h, 0:128, :], dwkv[h][0], first)
            _acc(dwkv_ref.at[h, 128:256, :], dwkv[h][1], first)

    hq = pl.BlockSpec((HEADS, tm, QK), lambda i: (0, i, 0))
    return pl.pallas_call(
        body, grid=(T // tm,),
        in_specs=_qkv_in_specs() + [hq, hq, pl.BlockSpec((HEADS, tm, 128), lambda i: (0, i, 0)), ANY],
        out_specs=[pl.BlockSpec((tm, 768), lambda i: (i, OFF_CQ // 768)), pl.BlockSpec((1, 384), lambda i: (0, 0)),
                   pl.BlockSpec((1, 384), lambda i: (0, 0)), pl.BlockSpec((HEADS, 192, 384), lambda i: (0, 0, 0)),
                   pl.BlockSpec((HEADS, 256, 256), lambda i: (0, 0, 0))],
        out_shape=[jax.ShapeDtypeStruct((T, NPAD), BF16), jax.ShapeDtypeStruct((1, 384), F32),
                   jax.ShapeDtypeStruct((1, 384), F32), jax.ShapeDtypeStruct((HEADS, 192, 384), F32),
                   jax.ShapeDtypeStruct((HEADS, 256, 256), F32)],
        input_output_aliases={11: 0},
        name=f"qkv_bwd_l{l}", compiler_params=_params(("arbitrary",)))(
            proj, proj, qg, kvg, wq, wkv, ctab, stab, dq, dk, dv, dproj)


ATT_TQ_FWD = 256
ATT_TQ_BWD = 512


def _attn_tile(q, kv_past, k, v, zb):
    q = q * (1.0 / math.sqrt(QK))
    s = dot_nt(q, k)
    qc = lax.broadcasted_iota(jnp.int32, s.shape, 0) >> CHUNK_SHIFT
    kc = lax.broadcasted_iota(jnp.int32, s.shape, 1) >> CHUNK_SHIFT
    s = jnp.where(kc <= qc, s, -1e30)
    m = jnp.max(s, axis=-1, keepdims=True)
    if kv_past is not None:
        sp = dot_nt(q, kv_past[0])
        m = jnp.maximum(m, jnp.max(sp, axis=-1, keepdims=True))
    m = lax.stop_gradient(m)
    p = jnp.exp(s - m)
    denom = jnp.sum(p, axis=-1, keepdims=True)
    o = dot_nn(p, v)
    if kv_past is not None:
        pp = jnp.exp(sp - m)
        denom = denom + jnp.sum(pp, axis=-1, keepdims=True)
        o = o + dot_nn(pp, kv_past[1])
    return o * (1.0 / denom) * _silu(zb)


def _attn_operands(k_ref, v_ref, g, tq):
    n = tq * g
    past = (k_ref[0:n, :], v_ref[0:n, :]) if g else None
    return past, k_ref[n:n + tq, :], v_ref[n:n + tq, :]


def _attn_in_specs(tq):
    return [pl.BlockSpec((None, tq, QK), lambda h, i: (h, i, 0)), pl.BlockSpec((None, T, QK), lambda h, i: (h, 0, 0)),
            pl.BlockSpec((None, T, 128), lambda h, i: (h, 0, 0)),
            pl.BlockSpec((tq, 128), lambda h, i: (i, OFF_ZB // 128 + h))]


def attn_fwd(q, k, v, proj, l):
    tq = ATT_TQ_FWD

    def body(q_ref, k_ref, v_ref, z_ref, y_ref):
        for g in range(T // tq):
            @pl.when(pl.program_id(1) == g)
            def _(g=g):
                past, k, v = _attn_operands(k_ref, v_ref, g, tq)
                y_ref[...] = _attn_tile(q_ref[...], past, k, v, z_ref[...])

    return pl.pallas_call(
        body, grid=(HEADS, T // tq), in_specs=_attn_in_specs(tq),
        out_specs=pl.BlockSpec((tq, 128), lambda h, i: (i, h)),
        out_shape=jax.ShapeDtypeStruct((T, 1024), F32),
        name=f"attn_fwd_l{l}", compiler_params=_params(("arbitrary", "arbitrary")))(q, k, v, proj)


def attn_bwd(q, k, v, proj, dy, dproj, l):
    tq = ATT_TQ_BWD

    def body(q_ref, k_ref, v_ref, z_ref, dy_ref, _, dq_ref, dk_ref, dv_ref, dz_ref):
        @pl.when(pl.program_id(1) == 0)
        def _():
            dk_ref[...] = jnp.zeros_like(dk_ref)
            dv_ref[...] = jnp.zeros_like(dv_ref)

        for g in range(T // tq):
            @pl.when(pl.program_id(1) == g)
            def _(g=g):
                n = tq * g
                past, k, v = _attn_operands(k_ref, v_ref, g, tq)
                _, vjp = jax.vjp(_attn_tile, q_ref[...], past, k, v, z_ref[...])
                dq, dpast, dk, dv, dz = vjp(dy_ref[...])
                dq_ref[...] = dq
                dz_ref[...] = dz.astype(BF16)
                dk_ref[n:n + tq, :] += dk
                dv_ref[n:n + tq, :] += dv
                if g:
                    dk_ref[0:n, :] += dpast[0]
                    dv_ref[0:n, :] += dpast[1]

    return pl.pallas_call(
        body, grid=(HEADS, T // tq),
        in_specs=_attn_in_specs(tq) + [pl.BlockSpec((tq, 128), lambda h, i: (i, h)), ANY],
        out_specs=[pl.BlockSpec((None, tq, QK), lambda h, i: (h, i, 0)), pl.BlockSpec((None, T, QK), lambda h, i: (h, 0, 0)),
                   pl.BlockSpec((None, T, 128), lambda h, i: (h, 0, 0)),
                   pl.BlockSpec((tq, 128), lambda h, i: (i, OFF_ZB // 128 + h))],
        out_shape=[jax.ShapeDtypeStruct((HEADS, T, QK), F32), jax.ShapeDtypeStruct((HEADS, T, QK), F32),
                   jax.ShapeDtypeStruct((HEADS, T, 128), F32), jax.ShapeDtypeStruct((T, NPAD), BF16)],
        input_output_aliases={5: 3},
        name=f"attn_bwd_l{l}", compiler_params=_params(("arbitrary", "arbitrary")))(q, k, v, proj, dy, dproj)


LRU_TT = 256


def _lru_gates(xc, wa, wx, ba, bx, lam):
    r = _sigmoid(dot_nn(xc, wa) + ba)
    i = _sigmoid(dot_nn(xc, wx) + bx)
    sp = jnp.maximum(-lam, 0.0) + jnp.log1p(jnp.exp(-jnp.abs(lam)))
    log_a = -8.0 * r * sp
    a = jnp.exp(log_a)
    mult = jnp.sqrt(jnp.maximum(1.0 - jnp.exp(2.0 * log_a), 0.0))
    return a, mult * (i * xc)


def _shift_down(x, s, halo):
    n, c = x.shape
    r = pltpu.roll(x.reshape(n // 8, 8, c), s, 1)
    before = jnp.concatenate([pltpu.roll(halo, s, 0)[None], r[:-1]], axis=0)
    sub = lax.broadcasted_iota(jnp.int32, r.shape, 1)
    return jnp.where(sub >= s, r, before).reshape(n, c)


def _shift_up(x, s, halo):
    n, c = x.shape
    r = pltpu.roll(x.reshape(n // 8, 8, c), 8 - s, 1)
    after = jnp.concatenate([r[1:], pltpu.roll(halo, 8 - s, 0)[None]], axis=0)
    sub = lax.broadcasted_iota(jnp.int32, r.shape, 1)
    return jnp.where(sub < 8 - s, r, after).reshape(n, c)


def _conv(x, halo, w_ref, b):
    return (w_ref[3:4, :] * x + w_ref[2:3, :] * _shift_down(x, 1, halo) + w_ref[1:2, :] * _shift_down(x, 2, halo)
            + w_ref[0:1, :] * _shift_down(x, 3, halo) + b)


def _scan(a, b, reverse, carry):
    n, c = a.shape
    a, b = a.reshape(n // 8, 8, c), b.reshape(n // 8, 8, c)
    sub = lax.broadcasted_iota(jnp.int32, a.shape, 1)
    for d in (1, 2, 4):
        keep = sub < 8 - d if reverse else sub >= d
        shift = 8 - d if reverse else d
        a_sh = jnp.where(keep, pltpu.roll(a, shift, 1), 1.0)
        b_sh = jnp.where(keep, pltpu.roll(b, shift, 1), 0.0)
        b = a * b_sh + b
        a = a * a_sh
    a, b = a.reshape(n, c), b.reshape(n, c)
    groups = [None] * (n // 8)
    for g in (reversed(range(n // 8)) if reverse else range(n // 8)):
        h = a[8 * g:8 * g + 8] * carry + b[8 * g:8 * g + 8]
        groups[g] = h
        carry = h[0:1] if reverse else h[7:8]
    return jnp.concatenate(groups, axis=0), carry


def _lru_param_specs(l):
    ct = LRU_TILE
    vec = pl.BlockSpec((1, ct), lambda n, i: (0, n))
    mat = pl.BlockSpec((None, 8, 80, 80), lambda n, i: (l, n, 0, 0))
    return [pl.BlockSpec((4, ct), lambda n, i: (0, n)), vec, mat, mat, vec, vec, vec]


def _blocks_to_dense(w_ref, dense):
    dense[...] = jnp.zeros_like(dense)
    for b in range(8):
        dense[80 * b:80 * b + 80, 80 * b:80 * b + 80] = w_ref[b]


def _dense_to_blocks(dense, w_ref):
    for b in range(8):
        w_ref[b] = dense[80 * b:80 * b + 80, 80 * b:80 * b + 80]


def lru_fwd(proj, conv_w, conv_b, wa, wx, ba, bx, lam, l):
    tt, ct = LRU_TT, LRU_TILE

    def body(x_ref, z_ref, cw_ref, cb_ref, wa_ref, wx_ref, ba_ref, bx_ref, lam_ref, h_ref, y_ref, halo, hcar, wa, wx):
        @pl.when(pl.program_id(1) == 0)
        def _():
            halo[...] = jnp.zeros_like(halo)
            hcar[...] = jnp.zeros_like(hcar)
            _blocks_to_dense(wa_ref, wa)
            _blocks_to_dense(wx_ref, wx)

        x = x_ref[...]
        xc = _conv(x, halo[...], cw_ref, cb_ref[...])
        halo[...] = x[tt - 8:tt]
        a, b = _lru_gates(xc, wa[...], wx[...], ba_ref[...], bx_ref[...], lam_ref[...])
        h, hcar[...] = _scan(a, b, False, hcar[...])
        h_ref[...] = h
        y_ref[...] = h * _silu(z_ref[...])

    seq = pl.BlockSpec((tt, ct), lambda n, i: (i, n))
    return pl.pallas_call(
        body, grid=(LRU_W // ct, T // tt),
        in_specs=[pl.BlockSpec((tt, ct), lambda n, i: (i, OFF_XC // ct + n)),
                  pl.BlockSpec((tt, ct), lambda n, i: (i, OFF_ZC // ct + n))] + _lru_param_specs(l),
        out_specs=[seq, seq],
        out_shape=[jax.ShapeDtypeStruct((T, LRU_W), F32), jax.ShapeDtypeStruct((T, LRU_W), F32)],
        scratch_shapes=[pltpu.VMEM((8, ct), F32), pltpu.VMEM((1, ct), F32), pltpu.VMEM((ct, ct), F32),
                        pltpu.VMEM((ct, ct), F32)],
        name=f"lru_fwd_l{l}", compiler_params=_params(("arbitrary", "arbitrary")))(
            proj, proj, conv_w, conv_b, wa, wx, ba, bx, lam)


def lru_bwd(proj, hseq, dy, conv_w, conv_b, wa, wx, ba, bx, lam, dproj, l):
    tt, ct = LRU_TT, LRU_TILE
    nt = T // tt
    rev = lambda i: nt - 1 - i
    prev8 = lambda i: jnp.maximum(rev(i) * (tt // 8) - 1, 0)

    def body(x_ref, xh_ref, z_ref, h_ref, hh_ref, dy_ref, cw_ref, cb_ref, wa_ref, wx_ref, ba_ref, bx_ref, lam_ref, _,
             dx_ref, dcw_ref, dcb_ref, dwa_ref, dwx_ref, dba_ref, dbx_ref, dlam_ref, gcar, dhalo,
             wa, wx, dwa_acc, dwx_acc):
        i = pl.program_id(1)
        first = i == 0

        @pl.when(first)
        def _():
            gcar[...] = jnp.zeros_like(gcar)
            dhalo[...] = jnp.zeros_like(dhalo)
            _blocks_to_dense(wa_ref, wa)
            _blocks_to_dense(wx_ref, wx)

        at_start = rev(i) == 0
        x = x_ref[...]
        xhalo = jnp.where(at_start, 0.0, xh_ref[...])
        sh = [x, _shift_down(x, 1, xhalo), _shift_down(x, 2, xhalo), _shift_down(x, 3, xhalo)]
        xc = (cw_ref[3:4, :] * sh[0] + cw_ref[2:3, :] * sh[1] + cw_ref[1:2, :] * sh[2] + cw_ref[0:1, :] * sh[3]
              + cb_ref[...])
        (a, b), vjp = jax.vjp(_lru_gates, xc, wa[...], wx[...], ba_ref[...], bx_ref[...], lam_ref[...])
        hs = h_ref[...]
        hprev = _shift_down(hs, 1, jnp.where(at_start, 0.0, hh_ref[...]))
        dh = dy_ref[...] * _silu(z_ref[...])
        a_next = _shift_up(a, 1, jnp.ones((8, ct), F32))
        g, _ = _scan(a_next, dh, True, gcar[...])
        dxc, dwa, dwx, dba, dbx, dlam = vjp((g * hprev, g))
        dx = (cw_ref[3:4, :] * dxc + cw_ref[2:3, :] * _shift_up(dxc, 1, dhalo[...])
              + cw_ref[1:2, :] * _shift_up(dxc, 2, dhalo[...]) + cw_ref[0:1, :] * _shift_up(dxc, 3, dhalo[...]))
        dx_ref[...] = dx.astype(BF16)
        dhalo[...] = dxc[0:8]
        ag = a * g
        gcar[...] = ag[0:1]
        dcw = jnp.concatenate([jnp.sum(dxc * sh[3 - j], axis=0, keepdims=True) for j in range(4)], axis=0)
        _acc(dcw_ref, dcw, first)
        _acc(dcb_ref, jnp.sum(dxc, axis=0, keepdims=True), first)
        _acc(dwa_acc, dwa, first)
        _acc(dwx_acc, dwx, first)

        @pl.when(i == nt - 1)
        def _():
            _dense_to_blocks(dwa_acc, dwa_ref)
            _dense_to_blocks(dwx_acc, dwx_ref)

        _acc(dba_ref, dba, first)
        _acc(dbx_ref, dbx, first)
        _acc(dlam_ref, dlam, first)

    xcol = OFF_XC // ct
    zcol = OFF_ZC // ct
    vec = pl.BlockSpec((1, ct), lambda n, i: (0, n))
    mat = pl.BlockSpec((8, 80, 80), lambda n, i: (n, 0, 0))
    seq = pl.BlockSpec((tt, ct), lambda n, i: (rev(i), n))
    return pl.pallas_call(
        body, grid=(LRU_W // ct, nt),
        in_specs=[pl.BlockSpec((tt, ct), lambda n, i: (rev(i), xcol + n)),
                  pl.BlockSpec((8, ct), lambda n, i: (prev8(i), xcol + n)),
                  pl.BlockSpec((tt, ct), lambda n, i: (rev(i), zcol + n)),
                  seq, pl.BlockSpec((8, ct), lambda n, i: (prev8(i), n)), seq] + _lru_param_specs(l) + [ANY],
        out_specs=[pl.BlockSpec((tt, ct), lambda n, i: (rev(i), xcol + n)),
                   pl.BlockSpec((4, ct), lambda n, i: (0, n)), vec, mat, mat, vec, vec, vec],
        out_shape=[jax.ShapeDtypeStruct((T, NPAD), BF16),
                   jax.ShapeDtypeStruct((4, LRU_W), F32), jax.ShapeDtypeStruct((1, LRU_W), F32),
                   jax.ShapeDtypeStruct((16, 80, 80), F32), jax.ShapeDtypeStruct((16, 80, 80), F32),
                   jax.ShapeDtypeStruct((1, LRU_W), F32), jax.ShapeDtypeStruct((1, LRU_W), F32),
                   jax.ShapeDtypeStruct((1, LRU_W), F32)],
        scratch_shapes=[pltpu.VMEM((1, ct), F32), pltpu.VMEM((8, ct), F32)] + [pltpu.VMEM((ct, ct), F32)] * 4,
        input_output_aliases={13: 0},
        name=f"lru_bwd_l{l}", compiler_params=_params(("arbitrary", "arbitrary")))(
            proj, proj, proj, hseq, hseq, dy, conv_w, conv_b, wa, wx, ba, bx, lam, dproj)


def proj_bwd(y, dp, w, l, tag, dep=None, dproj=None, gate=None):
    tm = 512
    k = y.shape[1]
    extra = [] if dep is None else [dep]
    in_specs = [pl.BlockSpec((tm, k), lambda i: (i, 0)), pl.BlockSpec((tm, D), lambda i: (i, 0)),
                pl.BlockSpec((None, k, D // 2), lambda i: (0, 0, 0))]
    out_specs = [pl.BlockSpec((tm, k), lambda i: (i, 0)), pl.BlockSpec((None, k, D), lambda i: (0, 0, 0))]
    out_shape = [jax.ShapeDtypeStruct((T, k), F32), jax.ShapeDtypeStruct((1, k, D), F32)]
    aliases = {}
    if gate is not None:
        in_specs += [pl.BlockSpec((tm, k), lambda i: (i, 0)), pl.BlockSpec((tm, k), lambda i: (i, OFF_ZC // k))]
        extra = list(gate) + extra
    if dproj is not None:
        width = k if gate is not None else PAD2
        at = OFF_ZC if gate is not None else OFF_XC - PAD2
        aliases = {3 + len(extra): 2}
        extra = extra + [dproj]
        out_specs.append(pl.BlockSpec((tm, width), lambda i: (i, at // width)))
        out_shape.append(jax.ShapeDtypeStruct((T, NPAD), BF16))
    in_specs += [ANY] * (3 + len(extra) - len(in_specs))

    def body(y_ref, dp_ref, w_ref, *rest):
        dy_ref, dw_ref = rest[len(extra):len(extra) + 2]
        dp = dp_ref[...]
        dy = _dg(dp, _unpack(w_ref[...]), _NT)
        dy_ref[...] = dy
        _acc(dw_ref, _dg(y_ref[...], dp, _TN), pl.program_id(0) == 0)
        if gate is not None:
            z = rest[1][...]
            sg = _sigmoid(z)
            rest[len(extra) + 2][...] = (dy * rest[0][...] * (sg * (1.0 + z * (1.0 - sg)))).astype(BF16)
        elif dproj is not None:
            rest[len(extra) + 2][...] = jnp.zeros((tm, PAD2), BF16)

    return pl.pallas_call(
        body, grid=(T // tm,), in_specs=in_specs, out_specs=out_specs, out_shape=out_shape,
        input_output_aliases=aliases,
        name=f"proj_{tag}_bwd_l{l}", compiler_params=_params(("arbitrary",)))(y, dp, w, *extra)


OUT_TM = 256


def _out_tile(pa, pb, pc, ga, gb, gc, wout, post_g):
    merged = _sigmoid(ga) * pa + _sigmoid(gb) * pb + _sigmoid(gc) * pc
    return _rms(dot_nn(merged, wout), post_g)


def _out_in_specs():
    tm = OUT_TM
    tok = pl.BlockSpec((tm, D), lambda i: (i, 0))
    gate = lambda off: pl.BlockSpec((tm, 512), lambda i, off=off: (i, off // 512))
    return [tok, tok, tok, gate(OFF_GA), gate(OFF_GA + 512), gate(OFF_GB), gate(OFF_GB + 512), gate(OFF_GC),
            gate(OFF_GC + 512), pl.BlockSpec((None, D, D // 2), lambda i: (0, 0, 0)), pl.BlockSpec((1, D), lambda i: (0, 0))]


def _gates(refs):
    return [jnp.concatenate([refs[2 * j][...], refs[2 * j + 1][...]], axis=1) for j in range(3)]


def out_fwd(x, ya, yb, yc, proj, wpa, wpb, wpc, wout, post_g, l):
    tm = OUT_TM

    def body(ya_ref, yb_ref, yc_ref, g0, g1, g2, g3, g4, g5, wo_ref, pg_ref, x_ref, wa_ref, wb_ref, wc_ref,
             o_ref, pa_ref, pb_ref, pc_ref, wa, wb, wc, wo):
        @pl.when(pl.program_id(0) == 0)
        def _():
            for dst, src in ((wa, wa_ref), (wb, wb_ref), (wc, wc_ref), (wo, wo_ref)):
                dst[...] = _unpack(src[...]).astype(BF16)

        pa = _dg(ya_ref[...], wa[...], _NN)
        pb = _dg(yb_ref[...], wb[...], _NN)
        pc = _dg(yc_ref[...], wc[...], _NN)
        ga, gb, gc = _gates([g0, g1, g2, g3, g4, g5])
        o_ref[...] = x_ref[...] + _out_tile(pa, pb, pc, ga, gb, gc, wo[...], pg_ref[...])
        pa_ref[...] = pa.astype(BF16)
        pb_ref[...] = pb.astype(BF16)
        pc_ref[...] = pc.astype(BF16)

    tok = pl.BlockSpec((tm, D), lambda i: (i, 0))
    words = lambda k: pl.BlockSpec((None, k, D // 2), lambda i: (0, 0, 0))
    specs = _out_in_specs()
    specs[2] = pl.BlockSpec((tm, LRU_W), lambda i: (i, 0))
    return pl.pallas_call(
        body, grid=(T // tm,), in_specs=specs + [tok, words(D), words(D), words(LRU_W)], out_specs=[tok] * 4,
        out_shape=[jax.ShapeDtypeStruct((T, D), F32)] + [jax.ShapeDtypeStruct((T, D), BF16)] * 3,
        scratch_shapes=[pltpu.VMEM((D, D), BF16), pltpu.VMEM((D, D), BF16), pltpu.VMEM((LRU_W, D), BF16),
                        pltpu.VMEM((D, D), BF16)],
        name=f"out_fwd_l{l}", compiler_params=_params(("arbitrary",)))(
            ya, yb, yc, proj, proj, proj, proj, proj, proj, wout, post_g, x, wpa, wpb, wpc)


def out_bwd(pa, pb, pc, proj, wout, post_g, dxn, l, dep=None):
    tm = OUT_TM
    nsteps = T // tm

    def body(pa_ref, pb_ref, pc_ref, g0, g1, g2, g3, g4, g5, w_ref, pg_ref, dxn_ref, *rest):
        dpa_ref, dpb_ref, dpc_ref, dproj_ref, dw_ref, dpg_ref, gbuf, sem = rest[-8:]
        i = pl.program_id(0)
        first = i == 0
        slot = i % 2
        ga, gb, gc = _gates([g0, g1, g2, g3, g4, g5])
        _, vjp = jax.vjp(_out_tile, pa_ref[...], pb_ref[...], pc_ref[...], ga, gb, gc, _unpack(w_ref[...]), pg_ref[...])
        dpa, dpb, dpc, dga, dgb, dgc, dw, dpg = vjp(dxn_ref[...])
        dpa_ref[...] = dpa.astype(BF16)
        dpb_ref[...] = dpb.astype(BF16)
        dpc_ref[...] = dpc.astype(BF16)
        _acc(dw_ref, dw, first)
        _acc(dpg_ref, dpg, first)

        def writeback(step, s):
            rows = pl.ds(pl.multiple_of(step * tm, tm), tm)
            return pltpu.make_async_copy(gbuf.at[s], dproj_ref.at[rows, pl.ds(OFF_GA, 3072)], sem.at[s])

        gbuf[slot, :, 0:1024] = dga.astype(BF16)
        gbuf[slot, :, 1024:2048] = dgb.astype(BF16)
        gbuf[slot, :, 2048:3072] = dgc.astype(BF16)
        writeback(i, slot).start()

        @pl.when(i > 0)
        def _():
            writeback(i - 1, 1 - slot).wait()

        @pl.when(i == nsteps - 1)
        def _():
            writeback(i, slot).wait()

    tok = pl.BlockSpec((tm, D), lambda i: (i, 0))
    deps = [] if dep is None else [dep]
    return pl.pallas_call(
        body, grid=(nsteps,), in_specs=_out_in_specs() + [tok] + [ANY] * len(deps),
        out_specs=[tok, tok, tok, ANY, pl.BlockSpec((None, D, D), lambda i: (0, 0, 0)), pl.BlockSpec((1, D), lambda i: (0, 0))],
        out_shape=[jax.ShapeDtypeStruct((T, D), BF16)] * 3 + [jax.ShapeDtypeStruct((T, NPAD), BF16),
                                                            jax.ShapeDtypeStruct((1, D, D), F32), jax.ShapeDtypeStruct((1, D), F32)],
        scratch_shapes=[pltpu.VMEM((2, tm, 3072), BF16), pltpu.SemaphoreType.DMA((2,))],
        name=f"out_bwd_l{l}", compiler_params=_params(("arbitrary",)))(
            pa, pb, pc, proj, proj, proj, proj, proj, proj, wout, post_g, dxn, *deps)


def loss_head(y, target):
    tm = 256

    def body(y_ref, t_ref, loss_ref, dy_ref):
        e = y_ref[...] - t_ref[...]
        dy_ref[...] = e * (1.0 / D)
        val = 0.5 * jnp.sum(jnp.mean(e * e, axis=-1, keepdims=True), axis=0, keepdims=True)
        _acc(loss_ref, jnp.broadcast_to(val, (8, 128)), pl.program_id(0) == 0)

    tok = pl.BlockSpec((tm, D), lambda i: (i, 0))
    total, dy = pl.pallas_call(
        body, grid=(T // tm,), in_specs=[tok, tok],
        out_specs=[pl.BlockSpec((8, 128), lambda i: (0, 0)), tok],
        out_shape=[jax.ShapeDtypeStruct((8, 128), F32), jax.ShapeDtypeStruct((T, D), F32)],
        name="loss_head", compiler_params=_params(("arbitrary",)))(y, target)
    return total[0, 0], dy


def _rope_tables():
    pos = jnp.arange(T, dtype=F32)
    inv_freq = 10000.0 ** (-jnp.arange(0, 64, 2, dtype=F32) / 64)
    ang = pos[:, None] * inv_freq[None, :]
    cos, sin = jnp.cos(ang), jnp.sin(ang)
    ctab = jnp.concatenate([jnp.ones((T, 128), F32), cos, cos], axis=1)
    stab = jnp.concatenate([jnp.zeros((T, 128), F32), -sin, sin], axis=1)
    return ctab, stab


def _layer_fwd(x, l, w, gw, tabs, dep=None, mid=None):
    row = lambda a: a[l][None]
    proj, h = inproj_fwd(x, row(w["pre_norm_g"]), gw["w_in_t"], l, dep)
    ya = gmlp_fwd(proj, row(w["gm_ln_g"]), row(w["gm_ln_b"]), w["gm_ws"][l], w["gm_bs"][l][..., None], l)
    dep2 = None
    if mid is not None:
        gw, dep2 = mid(ya)
    q, k, v = qkv_fwd(proj, row(w["mla_q_norm_g"]), row(w["kv_g384"]), gw["wq"], gw["wkv"], tabs[0], tabs[1], l, dep2)
    yb = attn_fwd(q, k, v, proj, l)
    hseq, yc = lru_fwd(proj, gw["conv"], row(w["lru_conv_b"]), w["lru_w_a"], w["lru_w_x"],
                       row(w["lru_b_a"]), row(w["lru_b_x"]), row(w["lru_lambda"]), l)
    xn, pa, pb, pc = out_fwd(x, ya, yb, yc, proj, gw["w_proj_a"], gw["w_proj_b"], gw["w_proj_c"], gw["w_out"],
                             row(w["post_norm_g"]), l)
    return xn, (x, proj, h, ya, q, k, v, yb, hseq, yc, pa, pb, pc)


def _layer_bwd(dxn, l, w, gw, tabs, saved, dep=None, early=None, mid=None):
    x, proj, h, ya, q, k, v, yb, hseq, yc, pa, pb, pc = saved
    row = lambda a: a[l][None]
    g, gg = {}, {}
    dpa, dpb, dpc, dproj, gg["w_out"], dpost = out_bwd(pa, pb, pc, proj, gw["w_out"], row(w["post_norm_g"]), dxn, l, dep)
    g["post_norm_g"] = dpost[0]
    dep1 = early(dpa) if early is not None else None
    dya, gg["w_proj_a"], dproj = proj_bwd(ya, dpa, gw["w_proj_a"], l, "a", dep1, dproj)
    dyb, gg["w_proj_b"] = proj_bwd(yb, dpb, gw["w_proj_b"], l, "b")
    dyc, gg["w_proj_c"], dproj = proj_bwd(yc, dpc, gw["w_proj_c"], l, "c", None, dproj, (hseq, proj))
    dproj, dln_g, dln_b, g["gm_ws"], dbs = gmlp_bwd(proj, row(w["gm_ln_g"]), row(w["gm_ln_b"]), w["gm_ws"][l],
                                                   w["gm_bs"][l][..., None], dya, dproj, l)
    g["gm_ln_g"], g["gm_ln_b"], g["gm_bs"] = dln_g[0], dln_b[0], dbs[..., 0]
    dq, dk, dv, dproj = attn_bwd(q, k, v, proj, dyb, dproj, l)
    dproj, dqg, dkvg, dwq, dwkv = qkv_bwd(proj, row(w["mla_q_norm_g"]), row(w["kv_g384"]), gw["wq"], gw["wkv"],
                                          tabs[0], tabs[1], dq, dk, dv, dproj, l)
    gg["wq"], gg["wkv"] = dwq.reshape(1, 1536, 384), dwkv.reshape(1, 2048, 256)
    g["mla_q_norm_g"], g["mla_kv_norm_g"] = dqg[0], dkvg[0, :256]
    dproj, dcw, dcb, dwa, dwx, dba, dbx, dlam = lru_bwd(
        proj, hseq, dyc, gw["conv"], row(w["lru_conv_b"]), w["lru_w_a"], w["lru_w_x"],
        row(w["lru_b_a"]), row(w["lru_b_x"]), row(w["lru_lambda"]), dproj, l)
    gg["conv"] = jnp.pad(dcw.T, ((0, 0), (0, 124)))[None]
    g["lru_conv_b"], g["lru_b_a"], g["lru_b_x"], g["lru_lambda"] = dcb[0], dba[0], dbx[0], dlam[0]
    g["lru_w_a"], g["lru_w_x"] = dwa, dwx
    dep2 = mid(gg, dproj) if mid is not None else None
    gg["w_in_t"], dh = inproj_bwd(dproj, h, gw["w_in_t"], l, dep2)
    dx, dpre = prenorm_bwd(x, row(w["pre_norm_g"]), dh, dxn, l)
    g["pre_norm_g"] = dpre[0]
    return dx, gg, g


MESH = pl.DeviceIdType.MESH
HBM = pl.BlockSpec(memory_space=pltpu.HBM)
SEM = pl.BlockSpec(memory_space=pltpu.SEMAPHORE)
EFFECT = pltpu.SideEffectType.DATAFLOW_SIDE_EFFECTING
FLIPS = ((1, 0), (0, 1), (1, 1))


def _win_off(k, s):
    g = SHARD * k + s
    return g + jnp.where(g >= PAD1_AT, PAD1, 0) + jnp.where(g >= PAD2_AT, PAD2, 0)


def _plain_off(rows):
    return lambda k, s: rows * k + s


class Spec:
    def __init__(self, rows, cols, full_rows, pieces=None, off=None, layers=1, packed=None):
        self.rows, self.cols, self.full_rows, self.layers = rows, cols, full_rows, layers
        self.pieces = pieces or ((0, rows),)
        self.off = off or _plain_off(rows)
        self.packed = cols % 256 == 0 if packed is None else packed
        self.wcols = cols // 2 if self.packed else cols

    def to_words(self, a):
        return _pack(a) if self.packed else a

    def from_words(self, p):
        return _unpack(p) if self.packed else p


def _pack(a):
    bits = lambda v: lax.bitcast_convert_type(v.astype(jnp.bfloat16).astype(F32), jnp.uint32)
    words = [(bits(a[:, g:g + 128]) >> 16) | (bits(a[:, g + 128:g + 256]) & jnp.uint32(0xFFFF0000))
             for g in range(0, a.shape[-1], 256)]
    return lax.bitcast_convert_type(jnp.concatenate(words, axis=-1) if len(words) > 1 else words[0], F32)


def _unpack(p):
    w = lax.bitcast_convert_type(p, jnp.uint32)
    lo = lax.bitcast_convert_type(w << 16, F32)
    hi = lax.bitcast_convert_type(w & jnp.uint32(0xFFFF0000), F32)
    return jnp.concatenate([h[:, g:g + 128] for g in range(0, p.shape[-1], 128) for h in (lo, hi)], axis=-1)


WEIGHT_SPECS = {
    "w_in_t": Spec(SHARD, D, NPAD, WIN_PIECES, _win_off),
    "wq": Spec(192, 384, 1536),
    "wkv": Spec(256, 256, 2048),
    "conv": Spec(160, 128, 1280),
    "w_proj_a": Spec(128, D, 1024),
    "w_proj_b": Spec(128, D, 1024),
    "w_proj_c": Spec(160, D, 1280),
    "w_out": Spec(128, D, 1024),
}
REP_ROWS = 72
REP_SPEC = Spec(REP_ROWS, D, REP_ROWS * NDEV, packed=False)


def _coords():
    return lax.axis_index("x"), lax.axis_index("y"), lax.axis_index("c")


def _rows(ref, start, n):
    if not isinstance(start, int):
        start = pl.multiple_of(start, 8)
    return ref.at[:, pl.ds(start, n), :]


def _col_tile(cols):
    return 256 if cols % 256 == 0 else cols


def _n_pieces(specs):
    return sum(len(sp.pieces) for sp in specs)


def pack_place(shard, sp, layer, tag, dep=None):
    gaps = ((PAD1_AT, PAD1), (PAD2_AT + PAD1, PAD2)) if sp.off is _win_off else ()
    npc = len(sp.pieces)
    deps = [] if dep is None else [dep]

    def body(s_ref, *rest):
        words_ref, full_ref, buf, zbuf, sem = rest[-5:]
        l = 0
        x, y, c = _coords()
        me = 4 * x + 2 * y + c
        words = sp.to_words(s_ref[...])
        words_ref[...] = words
        buf[...] = words
        copies = [pltpu.make_async_copy(buf.at[pl.ds(s, n), :],
                                        full_ref.at[l, pl.ds(pl.multiple_of(sp.off(me, s), 8), n), :], sem.at[i])
                  for i, (s, n) in enumerate(sp.pieces)]
        if gaps:
            zbuf[...] = jnp.zeros_like(zbuf)
            copies += [pltpu.make_async_copy(zbuf.at[pl.ds(0, n), :], full_ref.at[l, pl.ds(at, n), :], sem.at[npc + i])
                       for i, (at, n) in enumerate(gaps)]
        for cp in copies:
            cp.start()
        for cp in copies:
            cp.wait()

    return pl.pallas_call(
        body, grid=(1,), in_specs=[pl.BlockSpec((None, sp.rows, sp.cols), lambda i: (layer, 0, 0))] + [ANY] * len(deps),
        out_specs=[pl.BlockSpec((None, sp.rows, sp.wcols), lambda i: (0, 0, 0)), ANY],
        out_shape=[jax.ShapeDtypeStruct((sp.layers, sp.rows, sp.wcols), F32),
                   jax.ShapeDtypeStruct((sp.layers, sp.full_rows, sp.wcols), F32)],
        scratch_shapes=[pltpu.VMEM((sp.rows, sp.wcols), F32), pltpu.VMEM((PAD2 if gaps else 8, sp.wcols), F32),
                        pltpu.SemaphoreType.DMA((npc + len(gaps),))],
        name=f"pack_place_{tag}", compiler_params=_params(("arbitrary",)))(shard, *deps)


def _gather_copies(srcs, bufs, specs, ssem, rsem, landing):
    x, y, c = _coords()
    me = 4 * x + 2 * y + c
    targets = [(x, y, 1 - c)] + [(x ^ fx, y ^ fy, c) for fx, fy in FLIPS]
    copies = []
    p = 0
    for src, buf, sp in zip(srcs, bufs, specs):
        for s, n in sp.pieces:
            for t, (tx, ty, tc) in enumerate(targets):
                owner = 4 * tx + 2 * ty + tc if landing else me
                copies.append(pltpu.make_async_remote_copy(_rows(src, s, n), _rows(buf, sp.off(owner, s), n),
                                                           ssem.at[4 * p + t], rsem.at[4 * p + t],
                                                           device_id=(tx, ty, tc), device_id_type=MESH))
            p += 1
    return copies


def gather_send(words, fulls, specs, tag):
    ns, npc = len(specs), _n_pieces(specs)

    def body(*refs):
        srcs, bufs, sems = refs[:ns], refs[2 * ns:3 * ns], refs[3 * ns:]
        for cp in _gather_copies(srcs, bufs, specs, *sems, False):
            cp.start()
        for cp in _gather_copies(srcs, bufs, specs, *sems, False):
            cp.wait_send()
        for cp in _gather_copies(srcs, bufs, specs, *sems, True):
            cp.wait_recv()

    return pl.pallas_call(
        body, in_specs=[ANY] * (2 * ns), out_specs=[ANY] * ns,
        out_shape=[jax.ShapeDtypeStruct(f.shape, f.dtype) for f in fulls],
        input_output_aliases={ns + i: i for i in range(ns)},
        scratch_shapes=[pltpu.SemaphoreType.DMA((4 * npc,)), pltpu.SemaphoreType.DMA((4 * npc,))],
        name=f"gather_send_{tag}", compiler_params=pltpu.CompilerParams(has_side_effects=True))(*words, *fulls)


def _in_hbm(arrays):
    return [pltpu.with_memory_space_constraint(a, pltpu.HBM) for a in arrays]


def gather_start(words, fulls, specs, dep, tag):
    ns, npc = len(specs), _n_pieces(specs)
    deps = [] if dep is None else [dep]

    def body(*refs):
        ssem, rsem = refs[2 * ns + len(deps):2 * ns + len(deps) + 2]
        for cp in _gather_copies(refs[:ns], refs[ns:2 * ns], specs, ssem, rsem, False):
            cp.start()
        refs[-1][...] = jnp.zeros_like(refs[-1])

    outs = pl.pallas_call(
        body, in_specs=[HBM] * (2 * ns) + [ANY] * len(deps),
        out_specs=[SEM, SEM] + [HBM] * (2 * ns) + [pl.BlockSpec(memory_space=pltpu.VMEM)],
        out_shape=[pltpu.SemaphoreType.DMA((4 * npc,)), pltpu.SemaphoreType.DMA((4 * npc,))]
        + [pltpu.HBM(a.shape, a.dtype) for a in list(words) + list(fulls)] + [jax.ShapeDtypeStruct((8, 128), F32)],
        input_output_aliases={i: 2 + i for i in range(2 * ns)},
        name=f"gather_start_{tag}", compiler_params=pltpu.CompilerParams(has_side_effects=EFFECT))(
            *_in_hbm(list(words) + list(fulls)), *deps)
    return outs[0], outs[1], outs[2:2 + ns], outs[2 + ns:2 + 2 * ns], outs[-1]


def gather_wait(ssem, rsem, words, fulls, specs, after, tag):
    ns = len(specs)

    def body(*refs):
        srcs, bufs, ssem, rsem = refs[:ns], refs[ns:2 * ns], refs[2 * ns], refs[2 * ns + 1]
        for cp in _gather_copies(srcs, bufs, specs, ssem, rsem, False):
            cp.wait_send()
        for cp in _gather_copies(srcs, bufs, specs, ssem, rsem, True):
            cp.wait_recv()

    outs = pl.pallas_call(
        body, in_specs=[HBM] * (2 * ns) + [SEM, SEM, ANY], out_specs=[HBM] * (2 * ns),
        out_shape=[pltpu.HBM(a.shape, a.dtype) for a in list(words) + list(fulls)],
        input_output_aliases={i: i for i in range(2 * ns)},
        name=f"gather_wait_{tag}", compiler_params=pltpu.CompilerParams(has_side_effects=EFFECT))(
            *words, *fulls, ssem, rsem, after)
    return outs[ns:]


def gather_forward(fulls, specs, tag):
    ns, npc = len(specs), _n_pieces(specs)

    def body(*refs):
        bufs = refs[ns:2 * ns]
        ssem, rsem = refs[2 * ns:]
        x, y, c = _coords()
        sibling = (x, y, 1 - c)
        waits = []
        p = 0
        for buf, sp in zip(bufs, specs):
            for s, n in sp.pieces:
                for t, (fx, fy) in enumerate(FLIPS):
                    chip = 4 * (x ^ fx) + 2 * (y ^ fy)
                    here = _rows(buf, sp.off(chip + c, s), n)
                    send = pltpu.make_async_remote_copy(here, here, ssem.at[t, p], rsem.at[t, p],
                                                        device_id=sibling, device_id_type=MESH)
                    send.start()
                    waits.append(send.wait_send)
                    there = _rows(buf, sp.off(chip + 1 - c, s), n)
                    waits.append(pltpu.make_async_remote_copy(here, there, ssem.at[t, p], rsem.at[t, p],
                                                              device_id=sibling, device_id_type=MESH).wait_recv)
                p += 1
        for w in waits:
            w()

    return pl.pallas_call(
        body, in_specs=[ANY] * ns, out_specs=[ANY] * ns,
        out_shape=[jax.ShapeDtypeStruct(f.shape, f.dtype) for f in fulls],
        input_output_aliases={i: i for i in range(ns)},
        scratch_shapes=[pltpu.SemaphoreType.DMA((3, npc)), pltpu.SemaphoreType.DMA((3, npc))],
        name=f"gather_forward_{tag}", compiler_params=pltpu.CompilerParams(has_side_effects=True))(*fulls)


def all_gather(shards, layer, specs, names, tag):
    placed = [pack_place(s, sp, layer, f"{tag}_{n}") for s, sp, n in zip(shards, specs, names)]
    fulls = gather_send([p[0] for p in placed], [p[1] for p in placed], specs, tag)
    return gather_forward(fulls, specs, tag)


def _pair_copies(srcs, theirs, specs, ssem, rsem):
    x, y, c = _coords()
    copies = []
    p = 0
    for src, their, sp in zip(srcs, theirs, specs):
        for s, n in sp.pieces:
            for j in range(4):
                copies.append(pltpu.make_async_remote_copy(_rows(src, sp.off(2 * j + 1 - c, s), n), _rows(their.at[j], s, n),
                                                           ssem.at[4 * p + j], rsem.at[4 * p + j],
                                                           device_id=(x, y, 1 - c), device_id_type=MESH))
            p += 1
    return copies


def _pair_shapes(specs):
    return [(4, sp.layers, sp.rows, sp.cols) for sp in specs]


def reduce_pair(grads, specs, tag, dep=None):
    ns, npc = len(specs), _n_pieces(specs)
    deps = [] if dep is None else [dep]

    def body(*refs):
        copies = _pair_copies(refs[:ns], refs[ns + len(deps):2 * ns + len(deps)], specs, *refs[2 * ns + len(deps):])
        for cp in copies:
            cp.start()
        for cp in copies:
            cp.wait()

    return pl.pallas_call(
        body, in_specs=[ANY] * (ns + len(deps)), out_specs=[ANY] * ns,
        out_shape=[jax.ShapeDtypeStruct(s, F32) for s in _pair_shapes(specs)],
        scratch_shapes=[pltpu.SemaphoreType.DMA((4 * npc,)), pltpu.SemaphoreType.DMA((4 * npc,))],
        name=f"reduce_pair_{tag}", compiler_params=pltpu.CompilerParams(has_side_effects=True))(*grads, *deps)


def pair_start(grads, specs, dep, tag):
    ns, npc = len(specs), _n_pieces(specs)
    slots = [lax.empty(s, F32) for s in _pair_shapes(specs)]
    deps = [] if dep is None else [dep]

    def body(*refs):
        ssem, rsem = refs[2 * ns + len(deps):2 * ns + len(deps) + 2]
        for cp in _pair_copies(refs[:ns], refs[ns:2 * ns], specs, ssem, rsem):
            cp.start()
        refs[-1][...] = jnp.zeros_like(refs[-1])

    outs = pl.pallas_call(
        body, in_specs=[HBM] * (2 * ns) + [ANY] * len(deps),
        out_specs=[SEM, SEM] + [HBM] * (2 * ns) + [pl.BlockSpec(memory_space=pltpu.VMEM)],
        out_shape=[pltpu.SemaphoreType.DMA((4 * npc,)), pltpu.SemaphoreType.DMA((4 * npc,))]
        + [pltpu.HBM(a.shape, a.dtype) for a in list(grads) + slots] + [jax.ShapeDtypeStruct((8, 128), F32)],
        input_output_aliases={i: 2 + i for i in range(2 * ns)},
        name=f"pair_start_{tag}", compiler_params=pltpu.CompilerParams(has_side_effects=EFFECT))(
            *_in_hbm(list(grads) + slots), *deps)
    return outs[0], outs[1], outs[2:2 + ns], outs[2 + ns:2 + 2 * ns], outs[-1]


def pair_wait(ssem, rsem, grads, slots, specs, after, tag):
    ns = len(specs)

    def body(*refs):
        for cp in _pair_copies(refs[:ns], refs[ns:2 * ns], specs, refs[2 * ns], refs[2 * ns + 1]):
            cp.wait_send()
            cp.wait_recv()

    outs = pl.pallas_call(
        body, in_specs=[HBM] * (2 * ns) + [SEM, SEM, ANY], out_specs=[HBM] * (2 * ns),
        out_shape=[pltpu.HBM(a.shape, a.dtype) for a in list(grads) + list(slots)],
        input_output_aliases={i: i for i in range(2 * ns)},
        name=f"pair_wait_{tag}", compiler_params=pltpu.CompilerParams(has_side_effects=EFFECT))(
            *grads, *slots, ssem, rsem, after)
    return outs[:ns], outs[ns:]


def pair_sum(g, r1, sp, tag):
    npc = len(sp.pieces)
    fetch_all = 4 * sp.rows * sp.cols * 4 <= (8 << 20)

    def body(g_ref, r_ref, own_ref, words_ref, gbuf, sem):
        l, j = pl.program_id(0), pl.program_id(1)
        x, y, c = _coords()

        def fetch(chip, slot):
            copies = [pltpu.make_async_copy(g_ref.at[l, pl.ds(pl.multiple_of(sp.off(2 * chip + c, s), 8), n), :],
                                            gbuf.at[slot, pl.ds(s, n), :], sem.at[slot, i])
                      for i, (s, n) in enumerate(sp.pieces)]
            for cp in copies:
                cp.start()
            return copies

        if fetch_all:
            @pl.when(j == 0)
            def _():
                for cp in [cp for chip in range(4) for cp in fetch(chip, chip)]:
                    cp.wait()

            mine = gbuf[j]
        else:
            for cp in fetch(j, 0):
                cp.wait()
            mine = gbuf[0]
        p = mine + r_ref[...]
        words_ref[...] = sp.to_words(p)

        @pl.when(j == 2 * x + y)
        def _():
            own_ref[...] = p

    return pl.pallas_call(
        body, grid=(sp.layers, 4),
        in_specs=[ANY, pl.BlockSpec((None, None, sp.rows, sp.cols), lambda l, j: (j, l, 0, 0))],
        out_specs=[pl.BlockSpec((None, sp.rows, sp.cols), lambda l, j: (l, 0, 0)),
                   pl.BlockSpec((None, None, sp.rows, sp.wcols), lambda l, j: (j, l, 0, 0))],
        out_shape=[jax.ShapeDtypeStruct((sp.layers, sp.rows, sp.cols), F32),
                   jax.ShapeDtypeStruct((4, sp.layers, sp.rows, sp.wcols), F32)],
        scratch_shapes=[pltpu.VMEM((4 if fetch_all else 1, sp.rows, sp.cols), F32), pltpu.SemaphoreType.DMA((4, npc))],
        name=f"pair_sum_{tag}", compiler_params=_params(("arbitrary", "arbitrary")))(g, r1)


def _chip_copies(srcs, dsts, ssem, rsem):
    x, y, c = _coords()
    copies = []
    for i, (src, dst) in enumerate(zip(srcs, dsts)):
        for t, (fx, fy) in enumerate(FLIPS):
            tx, ty = x ^ fx, y ^ fy
            copies.append(pltpu.make_async_remote_copy(src.at[2 * tx + ty], dst.at[t], ssem.at[3 * i + t], rsem.at[3 * i + t],
                                                       device_id=(tx, ty, c), device_id_type=MESH))
    return copies


def _slot_shapes(words):
    return [(3,) + w.shape[1:] for w in words]


def reduce_chips(words, specs, tag):
    ns = len(specs)

    def body(*refs):
        copies = _chip_copies(refs[:ns], refs[ns:2 * ns], *refs[2 * ns:])
        for cp in copies:
            cp.start()
        for cp in copies:
            cp.wait()

    return pl.pallas_call(
        body, in_specs=[ANY] * ns, out_specs=[ANY] * ns,
        out_shape=[jax.ShapeDtypeStruct(s, F32) for s in _slot_shapes(words)],
        scratch_shapes=[pltpu.SemaphoreType.DMA((3 * ns,)), pltpu.SemaphoreType.DMA((3 * ns,))],
        name=f"reduce_chips_{tag}", compiler_params=pltpu.CompilerParams(has_side_effects=True))(*words)


def chips_start(words, specs, tag):
    ns = len(specs)
    slots = [lax.empty(s, F32) for s in _slot_shapes(words)]

    def body(*refs):
        ssem, rsem = refs[2 * ns:2 * ns + 2]
        for cp in _chip_copies(refs[:ns], refs[ns:2 * ns], ssem, rsem):
            cp.start()
        refs[-1][...] = jnp.zeros_like(refs[-1])

    outs = pl.pallas_call(
        body, in_specs=[HBM] * (2 * ns),
        out_specs=[SEM, SEM] + [HBM] * (2 * ns) + [pl.BlockSpec(memory_space=pltpu.VMEM)],
        out_shape=[pltpu.SemaphoreType.DMA((3 * ns,)), pltpu.SemaphoreType.DMA((3 * ns,))]
        + [pltpu.HBM(a.shape, a.dtype) for a in list(words) + slots] + [jax.ShapeDtypeStruct((8, 128), F32)],
        input_output_aliases={i: 2 + i for i in range(2 * ns)},
        name=f"chips_start_{tag}", compiler_params=pltpu.CompilerParams(has_side_effects=EFFECT))(
            *_in_hbm(list(words) + slots))
    return outs[0], outs[1], outs[2:2 + ns], outs[2 + ns:2 + 2 * ns], outs[-1]


def chips_wait(ssem, rsem, words, slots, specs, after, tag):
    ns = len(specs)

    def body(*refs):
        for cp in _chip_copies(refs[:ns], refs[ns:2 * ns], refs[2 * ns], refs[2 * ns + 1]):
            cp.wait_send()
            cp.wait_recv()

    outs = pl.pallas_call(
        body, in_specs=[HBM] * (2 * ns) + [SEM, SEM, ANY], out_specs=[HBM] * (2 * ns),
        out_shape=[pltpu.HBM(a.shape, a.dtype) for a in list(words) + list(slots)],
        input_output_aliases={i: i for i in range(2 * ns)},
        name=f"chips_wait_{tag}", compiler_params=pltpu.CompilerParams(has_side_effects=EFFECT))(
            *words, *slots, ssem, rsem, after)
    return outs[ns:]


def sum_chips(own, r2, sp, tag):
    def body(own_ref, r_ref, o_ref):
        o_ref[...] = ((own_ref[...] + sp.from_words(r_ref[0])) + sp.from_words(r_ref[1])) + sp.from_words(r_ref[2])

    blk = pl.BlockSpec((None, sp.rows, sp.cols), lambda l: (l, 0, 0))
    return pl.pallas_call(
        body, grid=(sp.layers,), in_specs=[blk, pl.BlockSpec((3, None, sp.rows, sp.wcols), lambda l: (0, l, 0, 0))],
        out_specs=blk, out_shape=jax.ShapeDtypeStruct((sp.layers, sp.rows, sp.cols), F32),
        name=f"sum_chips_{tag}", compiler_params=_params(("arbitrary",)))(own, r2)


def reduce_scatter_start(grads, specs, names, dep, tag):
    theirs = reduce_pair(grads, specs, tag, dep)
    sums = [pair_sum(g, r1, sp, f"{tag}_{n}") for g, r1, sp, n in zip(grads, theirs, specs, names)]
    ssem, rsem, words, slots, token = chips_start([s[1] for s in sums], specs, tag)
    return (ssem, rsem, words, slots, [s[0] for s in sums]), token


def reduce_scatter_finish(state, after, specs, tag):
    ssem, rsem, words, slots, own = state
    return list(zip(own, chips_wait(ssem, rsem, words, slots, specs, after, tag)))


def reduce_scatter(grads, specs, names, tag):
    theirs = reduce_pair(grads, specs, tag)
    sums = [pair_sum(g, r1, sp, f"{tag}_{n}") for g, r1, sp, n in zip(grads, theirs, specs, names)]
    return list(zip([s[0] for s in sums], reduce_chips([s[1] for s in sums], specs, tag)))


def _adamw_math(w, g, m, v):
    c1 = 1.0 - ADAM_B1 ** ADAM_STEP
    c2 = 1.0 - ADAM_B2 ** ADAM_STEP
    m2 = ADAM_B1 * m + (1.0 - ADAM_B1) * g
    v2 = ADAM_B2 * v + (1.0 - ADAM_B2) * (g * g)
    return -ADAM_LR * ((m2 / c1) / (jnp.sqrt(v2 / c2) + ADAM_EPS) + ADAM_WD * w), m2, v2


def adamw(w, g, m, v, name):
    shape = w.shape
    cols = shape[-1]
    rows = math.prod(shape[:-1])
    tr = rows
    while tr * cols * 4 > (1 << 20) and tr % 16 == 0:
        tr //= 2

    def body(w_ref, g_ref, m_ref, v_ref, d_ref, nm_ref, nv_ref):
        d_ref[...], nm_ref[...], nv_ref[...] = _adamw_math(w_ref[...], g_ref[...], m_ref[...], v_ref[...])

    blk = pl.BlockSpec((tr, cols), lambda i: (i, 0))
    outs = pl.pallas_call(
        body, grid=(rows // tr,), in_specs=[blk] * 4, out_specs=[blk] * 3,
        out_shape=[jax.ShapeDtypeStruct((rows, cols), F32)] * 3,
        name=f"adamw_{name}", compiler_params=_params(("arbitrary",)))(
            *[a.reshape(rows, cols) for a in (w, g, m, v)])
    return [o.reshape(shape) for o in outs]


def adamw_layer(w, sums, m, v, sp, l, prev, dep, name):
    _, rows, cols = w.shape
    tc = _col_tile(cols)
    twc = tc // 2 if sp.packed else tc
    extra = ([] if prev is None else list(prev)) + ([] if dep is None else [dep])

    def body(w_ref, own_ref, r_ref, m_ref, v_ref, *rest):
        g_ref, d_ref, nm_ref, nv_ref = rest[-4:]
        g = ((own_ref[...] + sp.from_words(r_ref[0])) + sp.from_words(r_ref[1])) + sp.from_words(r_ref[2])
        g_ref[...] = g
        d_ref[...], nm_ref[...], nv_ref[...] = _adamw_math(w_ref[...], g, m_ref[...], v_ref[...])

    blk = pl.BlockSpec((None, rows, tc), lambda n: (l, 0, n))
    return pl.pallas_call(
        body, grid=(cols // tc,),
        in_specs=[blk, pl.BlockSpec((None, rows, tc), lambda n: (0, 0, n)),
                  pl.BlockSpec((3, None, rows, twc), lambda n: (0, 0, 0, n)), blk, blk] + [ANY] * len(extra),
        out_specs=[blk] * 4, out_shape=[jax.ShapeDtypeStruct(w.shape, F32)] * 4,
        input_output_aliases={} if prev is None else {5 + i: i for i in range(4)},
        name=f"adamw_{name}_l{l}", compiler_params=_params(("arbitrary",)))(w, sums[0], sums[1], m, v, *extra)


WEIGHTS = ("pre_norm_g", "w_in", "gm_ln_g", "gm_ln_b", "gm_ws", "gm_bs", "mla_q_norm_g", "mla_w_uq", "mla_kv_norm_g",
           "mla_w_ukv", "lru_conv_w", "lru_conv_b", "lru_w_a", "lru_b_a", "lru_w_x", "lru_b_x", "lru_lambda",
           "w_proj_a", "w_proj_b", "w_proj_c", "w_out", "post_norm_g")
SHARDED = ("w_in", "mla_w_uq", "mla_w_ukv", "lru_conv_w", "w_proj_a", "w_proj_b", "w_proj_c", "w_out")
REPLICATED = tuple(n for n in WEIGHTS if n not in SHARDED)


def _step(x, target, wts, ms, vs):
    t12 = lambda a: jnp.swapaxes(a, 1, 2)
    names = list(WEIGHT_SPECS)
    specs = [WEIGHT_SPECS[n] for n in names]
    tabs = _rope_tables()
    own = {"w_in_t": t12(wts["w_in"]), "wq": t12(wts["mla_w_uq"]), "wkv": t12(wts["mla_w_ukv"]),
           "conv": jnp.pad(t12(wts["lru_conv_w"]), ((0, 0), (0, 0), (0, 124))),
           "w_proj_a": wts["w_proj_a"], "w_proj_b": wts["w_proj_b"], "w_proj_c": wts["w_proj_c"], "w_out": wts["w_out"]}
    first, rest = ["w_in_t"], [n for n in names if n != "w_in_t"]
    sfirst, srest = [WEIGHT_SPECS[n] for n in first], [WEIGHT_SPECS[n] for n in rest]

    w = {n: wts[n] for n in REPLICATED}
    w["kv_g384"] = jnp.concatenate([wts["mla_kv_norm_g"], jnp.ones((L, 128), F32)], axis=1)

    def layer_weights(ns, words):
        gw = dict(zip(ns, words))
        gw["wq"] = gw["wq"].reshape(HEADS, 192, 384)
        gw["wkv"] = gw["wkv"].reshape(HEADS, 256, 128)
        gw["conv"] = gw["conv"][0, :, :4].T
        return gw

    place = lambda l, dep: {n: pack_place(own[n], WEIGHT_SPECS[n], l, f"w{l}_{n}", dep) for n in names}
    placed = [place(0, None)]
    words_of = lambda l, ns: [placed[l][n][0] for n in ns]
    bufs_of = lambda l, ns: [placed[l][n][1] for n in ns]
    later = {}

    ssem_a, rsem_a, wthru_a, fthru_a, token_a = gather_start(words_of(0, first), bufs_of(0, first), sfirst, None, "w0a")
    placed.append(place(1, token_a))
    win0 = gather_forward(gather_wait(ssem_a, rsem_a, wthru_a, fthru_a, sfirst, placed[1]["w_in_t"][0], "w0a"), sfirst, "w0a")
    ssem_b, rsem_b, wthru_b, fthru_b, token_b = gather_start(words_of(0, rest), bufs_of(0, rest), srest, win0[0], "w0b")

    def fwd0_mid(ya):
        rest0 = gather_forward(gather_wait(ssem_b, rsem_b, wthru_b, fthru_b, srest, ya, "w0b"), srest, "w0b")
        later["w1"] = gather_start(words_of(1, names), bufs_of(1, names), specs, rest0[0], "w1")
        later["gw0"] = layer_weights(first + rest, list(win0) + list(rest0))
        return later["gw0"], later["w1"][4]

    x1, saved0 = _layer_fwd(x, 0, w, {"w_in_t": win0[0]}, tabs, dep=token_b, mid=fwd0_mid)
    ssem1, rsem1, wthru1, fthru1, _ = later["w1"]
    words1 = gather_forward(gather_wait(ssem1, rsem1, wthru1, fthru1, specs, x1, "w1"), specs, "w1")
    gw0, gw1 = later["gw0"], layer_weights(names, words1)
    x2, saved1 = _layer_fwd(x1, 1, w, gw1, tabs)
    loss, dx2 = loss_head(x2, target)

    def bwd1_mid(gg, last):
        later["g1b"], token = reduce_scatter_start([gg[n] for n in rest], srest, rest, last, "g1b")
        return token

    dx1, gg1, g1 = _layer_bwd(dx2, 1, w, gw1, tabs, saved1, mid=bwd1_mid)
    p1a = pair_start([gg1["w_in_t"]], sfirst, dx1, "g1a")

    def bwd0_early(last):
        grads1a, theirs1a = pair_wait(*p1a[:4], sfirst, last, "g1a")
        own_sum, words = pair_sum(grads1a[0], theirs1a[0], sfirst[0], "g1_w_in_t")
        ssem, rsem, words, slots, token = chips_start([words], sfirst, "g1a")
        later["g1a"] = (ssem, rsem, words, slots, [own_sum])
        return token

    def bwd0_mid(gg, last):
        later["g0b"], token = reduce_scatter_start([gg[n] for n in rest], srest, rest, last, "g0b")
        return token

    dx0, gg0, g0 = _layer_bwd(dx1, 0, w, gw0, tabs, saved0, dep=p1a[4], early=bwd0_early, mid=bwd0_mid)
    s1 = dict(zip(rest, reduce_scatter_finish(later["g1b"], dx0, srest, "g1b")))
    s1["w_in_t"] = reduce_scatter_finish(later["g1a"], dx0, sfirst, "g1a")[0]
    s0 = dict(zip(rest, reduce_scatter_finish(later["g0b"], dx0, srest, "g0b")))
    rep_flat = jnp.concatenate([jnp.stack([g0[n], g1[n]]).reshape(-1) for n in REPLICATED])
    rep_flat = jnp.pad(rep_flat, (0, REP_ROWS * NDEV * D - rep_flat.shape[0])).reshape(1, REP_ROWS * NDEV, D)
    state_a, token_g = reduce_scatter_start([gg0["w_in_t"], rep_flat], sfirst + [REP_SPEC], first + ["rep"], None, "g0a")

    keys = {"w_in": "w_in_t", "mla_w_uq": "wq", "mla_w_ukv": "wkv",
            "w_proj_a": "w_proj_a", "w_proj_b": "w_proj_b", "w_proj_c": "w_proj_c", "w_out": "w_out"}
    transposed = ("w_in", "mla_w_uq", "mla_w_ukv")
    state_of = lambda n: [own[keys[n]], t12(ms[n]), t12(vs[n])] if n in transposed else [wts[n], ms[n], vs[n]]

    def update(n, l, sums, prev, dep):
        wl, ml, vl = state_of(n)
        return adamw_layer(wl, sums[keys[n]], ml, vl, WEIGHT_SPECS[keys[n]], l, prev, dep, n)

    upd = {n: update(n, 1, s1, None, token_g) for n in keys}
    for n in keys:
        if n != "w_in":
            upd[n] = update(n, 0, s0, upd[n], None)
    s0["w_in_t"], rep_parts = reduce_scatter_finish(state_a, upd["w_out"][0], sfirst + [REP_SPEC], "g0a")
    upd["w_in"] = update("w_in", 0, s0, upd["w_in"], None)
    rep_sum = sum_chips(*rep_parts, REP_SPEC, "rep")
    rep_full = all_gather([rep_sum], 0, [REP_SPEC], ["rep"], "rep")[0].reshape(-1)

    out = {n: [t12(r) for r in upd[n]] if n in transposed else upd[n] for n in keys}
    conv_sp = WEIGHT_SPECS["conv"]
    g_conv = t12(jnp.concatenate([sum_chips(*s0["conv"], conv_sp, "conv0"), sum_chips(*s1["conv"], conv_sp, "conv1")])[:, :, :4])
    out["lru_conv_w"] = [g_conv] + adamw(wts["lru_conv_w"], g_conv, ms["lru_conv_w"], vs["lru_conv_w"], "lru_conv_w")
    at = 0
    for n in REPLICATED:
        size = math.prod(wts[n].shape)
        g = rep_full[at:at + size].reshape(wts[n].shape)
        out[n] = [g] + adamw(wts[n], g, ms[n], vs[n], n)
        at += size

    loss = lax.psum(loss, ("x", "y", "c"))
    return (loss, dx0[None], *[out[n][k] for k in range(4) for n in WEIGHTS])


def kernel(x, pre_norm_g, w_in, gm_ln_g, gm_ln_b, gm_ws, gm_bs, mla_q_norm_g, mla_w_uq, mla_kv_norm_g, mla_w_ukv, lru_conv_w, lru_conv_b, lru_w_a, lru_b_a, lru_w_x, lru_b_x, lru_lambda, w_proj_a, w_proj_b, w_proj_c, w_out, post_norm_g, loss_target, m_pre_norm_g, m_w_in, m_gm_ln_g, m_gm_ln_b, m_gm_ws, m_gm_bs, m_mla_q_norm_g, m_mla_w_uq, m_mla_kv_norm_g, m_mla_w_ukv, m_lru_conv_w, m_lru_conv_b, m_lru_w_a, m_lru_b_a, m_lru_w_x, m_lru_b_x, m_lru_lambda, m_w_proj_a, m_w_proj_b, m_w_proj_c, m_w_out, m_post_norm_g, v_pre_norm_g, v_w_in, v_gm_ln_g, v_gm_ln_b, v_gm_ws, v_gm_bs, v_mla_q_norm_g, v_mla_w_uq, v_mla_kv_norm_g, v_mla_w_ukv, v_lru_conv_w, v_lru_conv_b, v_lru_w_a, v_lru_b_a, v_lru_w_x, v_lru_b_x, v_lru_lambda, v_w_proj_a, v_w_proj_b, v_w_proj_c, v_w_out, v_post_norm_g):
    wts = dict(zip(WEIGHTS, (pre_norm_g, w_in, gm_ln_g, gm_ln_b, gm_ws, gm_bs, mla_q_norm_g, mla_w_uq, mla_kv_norm_g,
                             mla_w_ukv, lru_conv_w, lru_conv_b, lru_w_a, lru_b_a, lru_w_x, lru_b_x, lru_lambda,
                             w_proj_a, w_proj_b, w_proj_c, w_out, post_norm_g)))
    ms = dict(zip(WEIGHTS, (m_pre_norm_g, m_w_in, m_gm_ln_g, m_gm_ln_b, m_gm_ws, m_gm_bs, m_mla_q_norm_g, m_mla_w_uq,
                            m_mla_kv_norm_g, m_mla_w_ukv, m_lru_conv_w, m_lru_conv_b, m_lru_w_a, m_lru_b_a, m_lru_w_x,
                            m_lru_b_x, m_lru_lambda, m_w_proj_a, m_w_proj_b, m_w_proj_c, m_w_out, m_post_norm_g)))
    vs = dict(zip(WEIGHTS, (v_pre_norm_g, v_w_in, v_gm_ln_g, v_gm_ln_b, v_gm_ws, v_gm_bs, v_mla_q_norm_g, v_mla_w_uq,
                            v_mla_kv_norm_g, v_mla_w_ukv, v_lru_conv_w, v_lru_conv_b, v_lru_w_a, v_lru_b_a, v_lru_w_x,
                            v_lru_b_x, v_lru_lambda, v_w_proj_a, v_w_proj_b, v_w_proj_c, v_w_out, v_post_norm_g)))
    return _step(x[0], loss_target[0], wts, ms, vs)
```

```python
import functools
import math

import jax
import jax.numpy as jnp
from jax import lax
from jax.experimental import pallas as pl
from jax.experimental.pallas import tpu as pltpu

F32 = jnp.float32
BF16 = jnp.bfloat16

T = 2048
D = 1024
L = 2
NDEV = 8
EPS = 1e-6
CHUNK_SHIFT = 6
HEADS = 8
QK = 192
LRU_W = 1280
LRU_TILE = 640
N_IN = 10432
SHARD = N_IN // NDEV
OFF_U, OFF_V, OFF_ZA, OFF_CQ, OFF_CKV, OFF_ZB = 0, 1024, 2048, 3072, 3456, 3840
OFF_XC, OFF_ZC, OFF_GA, OFF_GB, OFF_GC = 5120, 6400, 7680, 8704, 9728
NPAD = 10752
PAD1_AT, PAD1 = 3776, 64
PAD2_AT, PAD2 = 4800, 256
WIN_PIECES = ((0, 888), (888, 280), (1168, 136))
VMEM_LIMIT = 60 * 1024 * 1024

ADAM_LR, ADAM_B1, ADAM_B2, ADAM_EPS, ADAM_WD, ADAM_STEP = 0.001, 0.9, 0.999, 1e-08, 0.01, 10

_NN = (((1,), (0,)), ((), ()))
_NT = (((1,), (1,)), ((), ()))
_TN = (((0,), (0,)), ((), ()))


def _dg(a, b, dims):
    return lax.dot_general(a.astype(BF16), b.astype(BF16), dims, preferred_element_type=F32)


@jax.custom_vjp
def dot_nn(a, b):
    return _dg(a, b, _NN)


def _nn_fwd(a, b):
    return _dg(a, b, _NN), (a, b)


def _nn_bwd(res, g):
    a, b = res
    return _dg(g, b, _NT).astype(a.dtype), _dg(a, g, _TN).astype(b.dtype)


dot_nn.defvjp(_nn_fwd, _nn_bwd)


@jax.custom_vjp
def dot_nt(a, b):
    return _dg(a, b, _NT)


def _nt_fwd(a, b):
    return _dg(a, b, _NT), (a, b)


def _nt_bwd(res, g):
    a, b = res
    return _dg(g, b, _NN).astype(a.dtype), _dg(g, a, _TN).astype(b.dtype)


dot_nt.defvjp(_nt_fwd, _nt_bwd)


def _params(sem=None):
    return pltpu.CompilerParams(dimension_semantics=sem, vmem_limit_bytes=VMEM_LIMIT)


def _sigmoid(x):
    return 1.0 / (1.0 + jnp.exp(-x))


def _silu(x):
    return x * _sigmoid(x)


def _rms(x, g):
    ms = jnp.mean(x * x, axis=-1, keepdims=True)
    return x * lax.rsqrt(ms + EPS) * g


def _acc(ref, val, first):
    @pl.when(first)
    def _():
        ref[...] = val

    @pl.when(jnp.logical_not(first))
    def _():
        ref[...] += val


ANY = pl.BlockSpec(memory_space=pl.ANY)


INPROJ_TN = 768


def inproj_fwd(x, g, wt, l, dep=None):
    tn = INPROJ_TN

    def body(x_ref, g_ref, w_ref, *rest):
        proj_ref, h_ref = rest[-2:]

        @pl.when(pl.program_id(0) == 0)
        def _():
            h_ref[...] = _rms(x_ref[...], g_ref[...]).astype(BF16)

        proj_ref[...] = lax.dot_general(h_ref[...], _unpack(w_ref[...]).astype(BF16), _NT, preferred_element_type=F32)

    deps = [] if dep is None else [dep]
    return pl.pallas_call(
        body, grid=(NPAD // tn,),
        in_specs=[pl.BlockSpec((T, D), lambda j: (0, 0)), pl.BlockSpec((1, D), lambda j: (0, 0)),
                  pl.BlockSpec((None, tn, D // 2), lambda j: (0, j, 0))] + [ANY] * len(deps),
        out_specs=[pl.BlockSpec((T, tn), lambda j: (0, j)), pl.BlockSpec((T, D), lambda j: (0, 0))],
        out_shape=[jax.ShapeDtypeStruct((T, NPAD), F32), jax.ShapeDtypeStruct((T, D), BF16)],
        name=f"inproj_fwd_l{l}", compiler_params=_params(("arbitrary",)))(x, g, wt, *deps)


def inproj_bwd(dproj, h, wt, l, dep=None):
    tn = INPROJ_TN
    deps = [] if dep is None else [dep]

    def body(dp_ref, h_ref, w_ref, *rest):
        dwt_ref, dh_ref = rest[-2:]
        dp = dp_ref[...]
        dwt_ref[...] = lax.dot_general(dp, h_ref[...], _TN, preferred_element_type=F32)
        contrib = lax.dot_general(dp, _unpack(w_ref[...]).astype(BF16), _NN, preferred_element_type=F32)
        _acc(dh_ref, contrib, pl.program_id(0) == 0)

    return pl.pallas_call(
        body, grid=(NPAD // tn,),
        in_specs=[pl.BlockSpec((T, tn), lambda j: (0, j)), pl.BlockSpec((T, D), lambda j: (0, 0)),
                  pl.BlockSpec((None, tn, D // 2), lambda j: (0, j, 0))] + [ANY] * len(deps),
        out_specs=[pl.BlockSpec((None, tn, D), lambda j: (0, j, 0)), pl.BlockSpec((T, D), lambda j: (0, 0))],
        out_shape=[jax.ShapeDtypeStruct((1, NPAD, D), F32), jax.ShapeDtypeStruct((T, D), F32)],
        name=f"inproj_bwd_l{l}", compiler_params=_params(("arbitrary",)))(dproj, h, wt, *deps)


def prenorm_bwd(x, g, dh, dxn, l):
    tm = 512

    def body(x_ref, g_ref, dh_ref, dxn_ref, dx_ref, dg_ref):
        _, vjp = jax.vjp(_rms, x_ref[...], g_ref[...])
        dx, dg = vjp(dh_ref[...])
        dx_ref[...] = dx + dxn_ref[...]
        _acc(dg_ref, dg, pl.program_id(0) == 0)

    tok = pl.BlockSpec((tm, D), lambda i: (i, 0))
    vec = pl.BlockSpec((1, D), lambda i: (0, 0))
    return pl.pallas_call(
        body, grid=(T // tm,), in_specs=[tok, vec, tok, tok], out_specs=[tok, vec],
        out_shape=[jax.ShapeDtypeStruct((T, D), F32), jax.ShapeDtypeStruct((1, D), F32)],
        name=f"prenorm_bwd_l{l}", compiler_params=_params(("arbitrary",)))(x, g, dh, dxn)


def _gmlp_tile(u, v, z, ln_g, ln_b, ws, bs):
    mu = jnp.mean(v, axis=-1, keepdims=True)
    vc = v - mu
    var = jnp.mean(vc * vc, axis=-1, keepdims=True)
    vn = vc * lax.rsqrt(var + EPS) * ln_g + ln_b
    qi = lax.broadcasted_iota(jnp.int32, (128, 128), 0) >> CHUNK_SHIFT
    kj = lax.broadcasted_iota(jnp.int32, (128, 128), 1) >> CHUNK_SHIFT
    mask = kj <= qi
    outs = []
    for g in range(4):
        wm = jnp.where(mask, ws[g], 0.0)
        outs.append(dot_nn(wm, vn[:, 256 * g:256 * (g + 1)]) + bs[g])
    sv = jnp.concatenate(outs, axis=1)
    return u * sv * _silu(z)


GMLP_ROWS = 256


def _gmlp_specs():
    blk = lambda c: pl.BlockSpec((GMLP_ROWS, 1024), lambda n, c=c: (n, c))
    vec = pl.BlockSpec((1, 1024), lambda n: (0, 0))
    return [blk(0), blk(1), blk(2), vec, vec,
            pl.BlockSpec((4, 128, 128), lambda n: (0, 0, 0)), pl.BlockSpec((4, 128, 1), lambda n: (0, 0, 0))]


def gmlp_fwd(proj, ln_g, ln_b, ws, bs, l):
    def body(u_ref, v_ref, z_ref, g_ref, b_ref, ws_ref, bs_ref, y_ref):
        for r in range(0, GMLP_ROWS, 128):
            rows = slice(r, r + 128)
            y_ref[rows, :] = _gmlp_tile(u_ref[rows, :], v_ref[rows, :], z_ref[rows, :], g_ref[...], b_ref[...],
                                        [ws_ref[g] for g in range(4)], [bs_ref[g] for g in range(4)])

    return pl.pallas_call(
        body, grid=(T // GMLP_ROWS,), in_specs=_gmlp_specs(),
        out_specs=pl.BlockSpec((GMLP_ROWS, 1024), lambda n: (n, 0)),
        out_shape=jax.ShapeDtypeStruct((T, 1024), F32),
        name=f"gmlp_fwd_l{l}", compiler_params=_params(("arbitrary",)))(proj, proj, proj, ln_g, ln_b, ws, bs)


def gmlp_bwd(proj, ln_g, ln_b, ws, bs, dy, dproj, l):
    def body(u_ref, v_ref, z_ref, g_ref, b_ref, ws_ref, bs_ref, dy_ref, _, dseg_ref, dg_ref, db_ref, dws_ref, dbs_ref):
        for r in range(0, GMLP_ROWS, 128):
            rows = slice(r, r + 128)
            first = jnp.logical_and(pl.program_id(0) == 0, r == 0)
            _, vjp = jax.vjp(_gmlp_tile, u_ref[rows, :], v_ref[rows, :], z_ref[rows, :], g_ref[...], b_ref[...],
                             [ws_ref[g] for g in range(4)], [bs_ref[g] for g in range(4)])
            du, dv, dz, dg, db, dws, dbs = vjp(dy_ref[rows, :])
            dseg_ref[rows, 0:1024] = du.astype(BF16)
            dseg_ref[rows, 1024:2048] = dv.astype(BF16)
            dseg_ref[rows, 2048:3072] = dz.astype(BF16)
            _acc(dg_ref, dg, first)
            _acc(db_ref, db, first)
            for g in range(4):
                _acc(dws_ref.at[g], dws[g], first)
                _acc(dbs_ref.at[g], dbs[g], first)

    vec = pl.BlockSpec((1, 1024), lambda n: (0, 0))
    return pl.pallas_call(
        body, grid=(T // GMLP_ROWS,),
        in_specs=_gmlp_specs() + [pl.BlockSpec((GMLP_ROWS, 1024), lambda n: (n, 0)), ANY],
        out_specs=[pl.BlockSpec((GMLP_ROWS, 3072), lambda n: (n, OFF_U // 3072)), vec, vec,
                   pl.BlockSpec((4, 128, 128), lambda n: (0, 0, 0)), pl.BlockSpec((4, 128, 1), lambda n: (0, 0, 0))],
        out_shape=[jax.ShapeDtypeStruct((T, NPAD), BF16), jax.ShapeDtypeStruct((1, 1024), F32),
                   jax.ShapeDtypeStruct((1, 1024), F32), jax.ShapeDtypeStruct((4, 128, 128), F32),
                   jax.ShapeDtypeStruct((4, 128, 1), F32)],
        input_output_aliases={8: 0},
        name=f"gmlp_bwd_l{l}", compiler_params=_params(("arbitrary",)))(proj, proj, proj, ln_g, ln_b, ws, bs, dy, dproj)


QKV_TM = 512


def _qkv_tile(cq, ckvr, qg, kvg, wq, wkv, ctab, stab):
    tm = cq.shape[0]
    cqn = _rms(cq, qg)
    lane = lax.broadcasted_iota(jnp.int32, ckvr.shape, 1)
    iskv = lane < 256
    ms = jnp.sum(jnp.where(iskv, ckvr * ckvr, 0.0), axis=-1, keepdims=True) * (1.0 / 256)
    lm = jnp.where(iskv, ckvr * lax.rsqrt(ms + EPS) * kvg, ckvr)
    r = lax.broadcasted_iota(jnp.int32, (64, 128), 0)
    c = lax.broadcasted_iota(jnp.int32, (64, 128), 1)
    eye = jnp.where(c == r, 1.0, 0.0)
    eye_sw = jnp.where(c == ((r + 32) & 63), 1.0, 0.0)
    z64 = jnp.zeros((64, 256), F32)
    z128 = jnp.zeros((128, 128), F32)
    rk_rope = jnp.concatenate([z64, eye], axis=1)
    rk_sw = jnp.concatenate([jnp.zeros((128, 384), F32), jnp.concatenate([z64, eye_sw], axis=1)], axis=0)
    k_sw = dot_nt(lm, rk_sw) * stab
    qs, ks, vs = [], [], []
    for h in range(HEADS):
        wn, w1, w2 = wq[h]
        wk, wv = wkv[h]
        wq_h = jnp.concatenate([wn, w1, w2], axis=0)
        wq_sw = jnp.concatenate([jnp.zeros((128, 384), F32), w2, w1], axis=0)
        qs.append(dot_nt(cqn, wq_h) * ctab + dot_nt(cqn, wq_sw) * stab)
        rk_h = jnp.concatenate([jnp.concatenate([wk, z128], axis=1), rk_rope], axis=0)
        ks.append(dot_nt(lm, rk_h) * ctab + k_sw)
        vs.append(dot_nt(lm, jnp.concatenate([wv, z128], axis=1)))
    return qs, ks, vs


def _qkv_in_specs():
    tm = QKV_TM
    return [pl.BlockSpec((tm, 384), lambda i: (i, OFF_CQ // 384)), pl.BlockSpec((tm, 384), lambda i: (i, OFF_CKV // 384)),
            pl.BlockSpec((1, 384), lambda i: (0, 0)), pl.BlockSpec((1, 384), lambda i: (0, 0)),
            pl.BlockSpec((HEADS, 192, 384), lambda i: (0, 0, 0)), pl.BlockSpec((HEADS, 256, 128), lambda i: (0, 0, 0)),
            pl.BlockSpec((tm, 192), lambda i: (i, 0)), pl.BlockSpec((tm, 192), lambda i: (i, 0))]


def _qkv_weights(wq_ref, wkv_ref):
    wq = [(wq_ref[h, 0:128, :], wq_ref[h, 128:160, :], wq_ref[h, 160:192, :]) for h in range(HEADS)]
    wkv = [(_unpack(wkv_ref[h, 0:128, :]), _unpack(wkv_ref[h, 128:256, :])) for h in range(HEADS)]
    return wq, wkv


def qkv_fwd(proj, qg, kvg, wq, wkv, ctab, stab, l, dep=None):
    tm = QKV_TM
    deps = [] if dep is None else [dep]

    def body(cq_ref, ckvr_ref, qg_ref, kvg_ref, wq_ref, wkv_ref, c_ref, s_ref, *rest):
        q_ref, k_ref, v_ref = rest[-3:]
        wq_l, wkv_l = _qkv_weights(wq_ref, wkv_ref)
        qs, ks, vs = _qkv_tile(cq_ref[...], ckvr_ref[...], qg_ref[...], kvg_ref[...], wq_l, wkv_l, c_ref[...], s_ref[...])
        for h in range(HEADS):
            q_ref[h] = qs[h]
            k_ref[h] = ks[h]
            v_ref[h] = vs[h]

    return pl.pallas_call(
        body, grid=(T // tm,), in_specs=_qkv_in_specs() + [ANY] * len(deps),
        out_specs=[pl.BlockSpec((HEADS, tm, QK), lambda i: (0, i, 0)), pl.BlockSpec((HEADS, tm, QK), lambda i: (0, i, 0)),
                   pl.BlockSpec((HEADS, tm, 128), lambda i: (0, i, 0))],
        out_shape=[jax.ShapeDtypeStruct((HEADS, T, QK), F32), jax.ShapeDtypeStruct((HEADS, T, QK), F32),
                   jax.ShapeDtypeStruct((HEADS, T, 128), F32)],
        name=f"qkv_fwd_l{l}", compiler_params=_params(("arbitrary",)))(proj, proj, qg, kvg, wq, wkv, ctab, stab, *deps)


def qkv_bwd(proj, qg, kvg, wq, wkv, ctab, stab, dq, dk, dv, dproj, l):
    tm = QKV_TM

    def body(cq_ref, ckvr_ref, qg_ref, kvg_ref, wq_ref, wkv_ref, c_ref, s_ref, dq_ref, dk_ref, dv_ref, _,
             dseg_ref, dqg_ref, dkvg_ref, dwq_ref, dwkv_ref):
        first = pl.program_id(0) == 0
        wq_l, wkv_l = _qkv_weights(wq_ref, wkv_ref)
        c_tab, s_tab = c_ref[...], s_ref[...]
        fn = lambda cq, ckvr, qg_, kvg_, wq_, wkv_: _qkv_tile(cq, ckvr, qg_, kvg_, wq_, wkv_, c_tab, s_tab)
        _, vjp = jax.vjp(fn, cq_ref[...], ckvr_ref[...], qg_ref[...], kvg_ref[...], wq_l, wkv_l)
        cts = ([dq_ref[h] for h in range(HEADS)], [dk_ref[h] for h in range(HEADS)], [dv_ref[h] for h in range(HEADS)])
        dcq, dckvr, dqg, dkvg, dwq, dwkv = vjp(cts)
        dseg_ref[:, 0:384] = dcq.astype(BF16)
        dseg_ref[:, 384:768] = dckvr.astype(BF16)
        _acc(dqg_ref, dqg, first)
        _acc(dkvg_ref, dkvg, first)
        for h in range(HEADS):
            _acc(dwq_ref.at[h, 0:128, :], dwq[h][0], first)
            _acc(dwq_ref.at[h, 128:160, :], dwq[h][1], first)
            _acc(dwq_ref.at[h, 160:192, :], dwq[h][2], first)
            _acc(dwkv_ref.at[h, 0:128, :], dwkv[h][0], first)
            _acc(dwkv_ref.at[h, 128:256, :], dwkv[h][1], first)

    hq = pl.BlockSpec((HEADS, tm, QK), lambda i: (0, i, 0))
    return pl.pallas_call(
        body, grid=(T // tm,),
        in_specs=_qkv_in_specs() + [hq, hq, pl.BlockSpec((HEADS, tm, 128), lambda i: (0, i, 0)), ANY],
        out_specs=[pl.BlockSpec((tm, 768), lambda i: (i, OFF_CQ // 768)), pl.BlockSpec((1, 384), lambda i: (0, 0)),
                   pl.BlockSpec((1, 384), lambda i: (0, 0)), pl.BlockSpec((HEADS, 192, 384), lambda i: (0, 0, 0)),
                   pl.BlockSpec((HEADS, 256, 256), lambda i: (0, 0, 0))],
        out_shape=[jax.ShapeDtypeStruct((T, NPAD), BF16), jax.ShapeDtypeStruct((1, 384), F32),
                   jax.ShapeDtypeStruct((1, 384), F32), jax.ShapeDtypeStruct((HEADS, 192, 384), F32),
                   jax.ShapeDtypeStruct((HEADS, 256, 256), F32)],
        input_output_aliases={11: 0},
        name=f"qkv_bwd_l{l}", compiler_params=_params(("arbitrary",)))(
            proj, proj, qg, kvg, wq, wkv, ctab, stab, dq, dk, dv, dproj)


ATT_TQ_FWD = 256
ATT_TQ_BWD = 512


def _attn_tile(q, kv_past, k, v, zb):
    q = q * (1.0 / math.sqrt(QK))
    s = dot_nt(q, k)
    qc = lax.broadcasted_iota(jnp.int32, s.shape, 0) >> CHUNK_SHIFT
    kc = lax.broadcasted_iota(jnp.int32, s.shape, 1) >> CHUNK_SHIFT
    s = jnp.where(kc <= qc, s, -1e30)
    m = jnp.max(s, axis=-1, keepdims=True)
    if kv_past is not None:
        sp = dot_nt(q, kv_past[0])
        m = jnp.maximum(m, jnp.max(sp, axis=-1, keepdims=True))
    m = lax.stop_gradient(m)
    p = jnp.exp(s - m)
    denom = jnp.sum(p, axis=-1, keepdims=True)
    o = dot_nn(p, v)
    if kv_past is not None:
        pp = jnp.exp(sp - m)
        denom = denom + jnp.sum(pp, axis=-1, keepdims=True)
        o = o + dot_nn(pp, kv_past[1])
    return o * (1.0 / denom) * _silu(zb)


def _attn_operands(k_ref, v_ref, g, tq):
    n = tq * g
    past = (k_ref[0:n, :], v_ref[0:n, :]) if g else None
    return past, k_ref[n:n + tq, :], v_ref[n:n + tq, :]


def _attn_in_specs(tq):
    return [pl.BlockSpec((None, tq, QK), lambda h, i: (h, i, 0)), pl.BlockSpec((None, T, QK), lambda h, i: (h, 0, 0)),
            pl.BlockSpec((None, T, 128), lambda h, i: (h, 0, 0)),
            pl.BlockSpec((tq, 128), lambda h, i: (i, OFF_ZB // 128 + h))]


def attn_fwd(q, k, v, proj, l):
    tq = ATT_TQ_FWD

    def body(q_ref, k_ref, v_ref, z_ref, y_ref):
        for g in range(T // tq):
            @pl.when(pl.program_id(1) == g)
            def _(g=g):
                past, k, v = _attn_operands(k_ref, v_ref, g, tq)
                y_ref[...] = _attn_tile(q_ref[...], past, k, v, z_ref[...])

    return pl.pallas_call(
        body, grid=(HEADS, T // tq), in_specs=_attn_in_specs(tq),
        out_specs=pl.BlockSpec((tq, 128), lambda h, i: (i, h)),
        out_shape=jax.ShapeDtypeStruct((T, 1024), F32),
        name=f"attn_fwd_l{l}", compiler_params=_params(("arbitrary", "arbitrary")))(q, k, v, proj)


def attn_bwd(q, k, v, proj, dy, dproj, l):
    tq = ATT_TQ_BWD

    def body(q_ref, k_ref, v_ref, z_ref, dy_ref, _, dq_ref, dk_ref, dv_ref, dz_ref):
        @pl.when(pl.program_id(1) == 0)
        def _():
            dk_ref[...] = jnp.zeros_like(dk_ref)
            dv_ref[...] = jnp.zeros_like(dv_ref)

        for g in range(T // tq):
            @pl.when(pl.program_id(1) == g)
            def _(g=g):
                n = tq * g
                past, k, v = _attn_operands(k_ref, v_ref, g, tq)
                _, vjp = jax.vjp(_attn_tile, q_ref[...], past, k, v, z_ref[...])
                dq, dpast, dk, dv, dz = vjp(dy_ref[...])
                dq_ref[...] = dq
                dz_ref[...] = dz.astype(BF16)
                dk_ref[n:n + tq, :] += dk
                dv_ref[n:n + tq, :] += dv
                if g:
                    dk_ref[0:n, :] += dpast[0]
                    dv_ref[0:n, :] += dpast[1]

    return pl.pallas_call(
        body, grid=(HEADS, T // tq),
        in_specs=_attn_in_specs(tq) + [pl.BlockSpec((tq, 128), lambda h, i: (i, h)), ANY],
        out_specs=[pl.BlockSpec((None, tq, QK), lambda h, i: (h, i, 0)), pl.BlockSpec((None, T, QK), lambda h, i: (h, 0, 0)),
                   pl.BlockSpec((None, T, 128), lambda h, i: (h, 0, 0)),
                   pl.BlockSpec((tq, 128), lambda h, i: (i, OFF_ZB // 128 + h))],
        out_shape=[jax.ShapeDtypeStruct((HEADS, T, QK), F32), jax.ShapeDtypeStruct((HEADS, T, QK), F32),
                   jax.ShapeDtypeStruct((HEADS, T, 128), F32), jax.ShapeDtypeStruct((T, NPAD), BF16)],
        input_output_aliases={5: 3},
        name=f"attn_bwd_l{l}", compiler_params=_params(("arbitrary", "arbitrary")))(q, k, v, proj, dy, dproj)


LRU_TT = 256


def _lru_gates(xc, wa, wx, ba, bx, lam):
    r = _sigmoid(dot_nn(xc, wa) + ba)
    i = _sigmoid(dot_nn(xc, wx) + bx)
    sp = jnp.maximum(-lam, 0.0) + jnp.log1p(jnp.exp(-jnp.abs(lam)))
    log_a = -8.0 * r * sp
    a = jnp.exp(log_a)
    mult = jnp.sqrt(jnp.maximum(1.0 - jnp.exp(2.0 * log_a), 0.0))
    return a, mult * (i * xc)


def _shift_down(x, s, halo):
    n, c = x.shape
    r = pltpu.roll(x.reshape(n // 8, 8, c), s, 1)
    before = jnp.concatenate([pltpu.roll(halo, s, 0)[None], r[:-1]], axis=0)
    sub = lax.broadcasted_iota(jnp.int32, r.shape, 1)
    return jnp.where(sub >= s, r, before).reshape(n, c)


def _shift_up(x, s, halo):
    n, c = x.shape
    r = pltpu.roll(x.reshape(n // 8, 8, c), 8 - s, 1)
    after = jnp.concatenate([r[1:], pltpu.roll(halo, 8 - s, 0)[None]], axis=0)
    sub = lax.broadcasted_iota(jnp.int32, r.shape, 1)
    return jnp.where(sub < 8 - s, r, after).reshape(n, c)


def _conv(x, halo, w_ref, b):
    return (w_ref[3:4, :] * x + w_ref[2:3, :] * _shift_down(x, 1, halo) + w_ref[1:2, :] * _shift_down(x, 2, halo)
            + w_ref[0:1, :] * _shift_down(x, 3, halo) + b)


def _scan(a, b, reverse, carry):
    n, c = a.shape
    a, b = a.reshape(n // 8, 8, c), b.reshape(n // 8, 8, c)
    sub = lax.broadcasted_iota(jnp.int32, a.shape, 1)
    for d in (1, 2, 4):
        keep = sub < 8 - d if reverse else sub >= d
        shift = 8 - d if reverse else d
        a_sh = jnp.where(keep, pltpu.roll(a, shift, 1), 1.0)
        b_sh = jnp.where(keep, pltpu.roll(b, shift, 1), 0.0)
        b = a * b_sh + b
        a = a * a_sh
    a, b = a.reshape(n, c), b.reshape(n, c)
    groups = [None] * (n // 8)
    for g in (reversed(range(n // 8)) if reverse else range(n // 8)):
        h = a[8 * g:8 * g + 8] * carry + b[8 * g:8 * g + 8]
        groups[g] = h
        carry = h[0:1] if reverse else h[7:8]
    return jnp.concatenate(groups, axis=0), carry


def _lru_param_specs(l):
    ct = LRU_TILE
    vec = pl.BlockSpec((1, ct), lambda n, i: (0, n))
    mat = pl.BlockSpec((None, 8, 80, 80), lambda n, i: (l, n, 0, 0))
    return [pl.BlockSpec((4, ct), lambda n, i: (0, n)), vec, mat, mat, vec, vec, vec]


def _blocks_to_dense(w_ref, dense):
    dense[...] = jnp.zeros_like(dense)
    for b in range(8):
        dense[80 * b:80 * b + 80, 80 * b:80 * b + 80] = w_ref[b]


def _dense_to_blocks(dense, w_ref):
    for b in range(8):
        w_ref[b] = dense[80 * b:80 * b + 80, 80 * b:80 * b + 80]


def lru_fwd(proj, conv_w, conv_b, wa, wx, ba, bx, lam, l):
    tt, ct = LRU_TT, LRU_TILE

    def body(x_ref, z_ref, cw_ref, cb_ref, wa_ref, wx_ref, ba_ref, bx_ref, lam_ref, h_ref, y_ref, halo, hcar, wa, wx):
        @pl.when(pl.program_id(1) == 0)
        def _():
            halo[...] = jnp.zeros_like(halo)
            hcar[...] = jnp.zeros_like(hcar)
            _blocks_to_dense(wa_ref, wa)
            _blocks_to_dense(wx_ref, wx)

        x = x_ref[...]
        xc = _conv(x, halo[...], cw_ref, cb_ref[...])
        halo[...] = x[tt - 8:tt]
        a, b = _lru_gates(xc, wa[...], wx[...], ba_ref[...], bx_ref[...], lam_ref[...])
        h, hcar[...] = _scan(a, b, False, hcar[...])
        h_ref[...] = h
        y_ref[...] = h * _silu(z_ref[...])

    seq = pl.BlockSpec((tt, ct), lambda n, i: (i, n))
    return pl.pallas_call(
        body, grid=(LRU_W // ct, T // tt),
        in_specs=[pl.BlockSpec((tt, ct), lambda n, i: (i, OFF_XC // ct + n)),
                  pl.BlockSpec((tt, ct), lambda n, i: (i, OFF_ZC // ct + n))] + _lru_param_specs(l),
        out_specs=[seq, seq],
        out_shape=[jax.ShapeDtypeStruct((T, LRU_W), F32), jax.ShapeDtypeStruct((T, LRU_W), F32)],
        scratch_shapes=[pltpu.VMEM((8, ct), F32), pltpu.VMEM((1, ct), F32), pltpu.VMEM((ct, ct), F32),
                        pltpu.VMEM((ct, ct), F32)],
        name=f"lru_fwd_l{l}", compiler_params=_params(("arbitrary", "arbitrary")))(
            proj, proj, conv_w, conv_b, wa, wx, ba, bx, lam)


def lru_bwd(proj, hseq, dy, conv_w, conv_b, wa, wx, ba, bx, lam, dproj, l):
    tt, ct = LRU_TT, LRU_TILE
    nt = T // tt
    rev = lambda i: nt - 1 - i
    prev8 = lambda i: jnp.maximum(rev(i) * (tt // 8) - 1, 0)

    def body(x_ref, xh_ref, z_ref, h_ref, hh_ref, dy_ref, cw_ref, cb_ref, wa_ref, wx_ref, ba_ref, bx_ref, lam_ref, _,
             dx_ref, dcw_ref, dcb_ref, dwa_ref, dwx_ref, dba_ref, dbx_ref, dlam_ref, gcar, dhalo,
             wa, wx, dwa_acc, dwx_acc):
        i = pl.program_id(1)
        first = i == 0

        @pl.when(first)
        def _():
            gcar[...] = jnp.zeros_like(gcar)
            dhalo[...] = jnp.zeros_like(dhalo)
            _blocks_to_dense(wa_ref, wa)
            _blocks_to_dense(wx_ref, wx)

        at_start = rev(i) == 0
        x = x_ref[...]
        xhalo = jnp.where(at_start, 0.0, xh_ref[...])
        sh = [x, _shift_down(x, 1, xhalo), _shift_down(x, 2, xhalo), _shift_down(x, 3, xhalo)]
        xc = (cw_ref[3:4, :] * sh[0] + cw_ref[2:3, :] * sh[1] + cw_ref[1:2, :] * sh[2] + cw_ref[0:1, :] * sh[3]
              + cb_ref[...])
        (a, b), vjp = jax.vjp(_lru_gates, xc, wa[...], wx[...], ba_ref[...], bx_ref[...], lam_ref[...])
        hs = h_ref[...]
        hprev = _shift_down(hs, 1, jnp.where(at_start, 0.0, hh_ref[...]))
        dh = dy_ref[...] * _silu(z_ref[...])
        a_next = _shift_up(a, 1, jnp.ones((8, ct), F32))
        g, _ = _scan(a_next, dh, True, gcar[...])
        dxc, dwa, dwx, dba, dbx, dlam = vjp((g * hprev, g))
        dx = (cw_ref[3:4, :] * dxc + cw_ref[2:3, :] * _shift_up(dxc, 1, dhalo[...])
              + cw_ref[1:2, :] * _shift_up(dxc, 2, dhalo[...]) + cw_ref[0:1, :] * _shift_up(dxc, 3, dhalo[...]))
        dx_ref[...] = dx.astype(BF16)
        dhalo[...] = dxc[0:8]
        ag = a * g
        gcar[...] = ag[0:1]
        dcw = jnp.concatenate([jnp.sum(dxc * sh[3 - j], axis=0, keepdims=True) for j in range(4)], axis=0)
        _acc(dcw_ref, dcw, first)
        _acc(dcb_ref, jnp.sum(dxc, axis=0, keepdims=True), first)
        _acc(dwa_acc, dwa, first)
        _acc(dwx_acc, dwx, first)

        @pl.when(i == nt - 1)
        def _():
            _dense_to_blocks(dwa_acc, dwa_ref)
            _dense_to_blocks(dwx_acc, dwx_ref)

        _acc(dba_ref, dba, first)
        _acc(dbx_ref, dbx, first)
        _acc(dlam_ref, dlam, first)

    xcol = OFF_XC // ct
    zcol = OFF_ZC // ct
    vec = pl.BlockSpec((1, ct), lambda n, i: (0, n))
    mat = pl.BlockSpec((8, 80, 80), lambda n, i: (n, 0, 0))
    seq = pl.BlockSpec((tt, ct), lambda n, i: (rev(i), n))
    return pl.pallas_call(
        body, grid=(LRU_W // ct, nt),
        in_specs=[pl.BlockSpec((tt, ct), lambda n, i: (rev(i), xcol + n)),
                  pl.BlockSpec((8, ct), lambda n, i: (prev8(i), xcol + n)),
                  pl.BlockSpec((tt, ct), lambda n, i: (rev(i), zcol + n)),
                  seq, pl.BlockSpec((8, ct), lambda n, i: (prev8(i), n)), seq] + _lru_param_specs(l) + [ANY],
        out_specs=[pl.BlockSpec((tt, ct), lambda n, i: (rev(i), xcol + n)),
                   pl.BlockSpec((4, ct), lambda n, i: (0, n)), vec, mat, mat, vec, vec, vec],
        out_shape=[jax.ShapeDtypeStruct((T, NPAD), BF16),
                   jax.ShapeDtypeStruct((4, LRU_W), F32), jax.ShapeDtypeStruct((1, LRU_W), F32),
                   jax.ShapeDtypeStruct((16, 80, 80), F32), jax.ShapeDtypeStruct((16, 80, 80), F32),
                   jax.ShapeDtypeStruct((1, LRU_W), F32), jax.ShapeDtypeStruct((1, LRU_W), F32),
                   jax.ShapeDtypeStruct((1, LRU_W), F32)],
        scratch_shapes=[pltpu.VMEM((1, ct), F32), pltpu.VMEM((8, ct), F32)] + [pltpu.VMEM((ct, ct), F32)] * 4,
        input_output_aliases={13: 0},
        name=f"lru_bwd_l{l}", compiler_params=_params(("arbitrary", "arbitrary")))(
            proj, proj, proj, hseq, hseq, dy, conv_w, conv_b, wa, wx, ba, bx, lam, dproj)


def proj_bwd(y, dp, w, l, tag, dep=None, dproj=None, gate=None):
    tm = 512
    k = y.shape[1]
    extra = [] if dep is None else [dep]
    in_specs = [pl.BlockSpec((tm, k), lambda i: (i, 0)), pl.BlockSpec((tm, D), lambda i: (i, 0)),
                pl.BlockSpec((None, k, D // 2), lambda i: (0, 0, 0))]
    out_specs = [pl.BlockSpec((tm, k), lambda i: (i, 0)), pl.BlockSpec((None, k, D), lambda i: (0, 0, 0))]
    out_shape = [jax.ShapeDtypeStruct((T, k), F32), jax.ShapeDtypeStruct((1, k, D), F32)]
    aliases = {}
    if gate is not None:
        in_specs += [pl.BlockSpec((tm, k), lambda i: (i, 0)), pl.BlockSpec((tm, k), lambda i: (i, OFF_ZC // k))]
        extra = list(gate) + extra
    if dproj is not None:
        width = k if gate is not None else PAD2
        at = OFF_ZC if gate is not None else OFF_XC - PAD2
        aliases = {3 + len(extra): 2}
        extra = extra + [dproj]
        out_specs.append(pl.BlockSpec((tm, width), lambda i: (i, at // width)))
        out_shape.append(jax.ShapeDtypeStruct((T, NPAD), BF16))
    in_specs += [ANY] * (3 + len(extra) - len(in_specs))

    def body(y_ref, dp_ref, w_ref, *rest):
        dy_ref, dw_ref = rest[len(extra):len(extra) + 2]
        dp = dp_ref[...]
        dy = _dg(dp, _unpack(w_ref[...]), _NT)
        dy_ref[...] = dy
        _acc(dw_ref, _dg(y_ref[...], dp, _TN), pl.program_id(0) == 0)
        if gate is not None:
            z = rest[1][...]
            sg = _sigmoid(z)
            rest[len(extra) + 2][...] = (dy * rest[0][...] * (sg * (1.0 + z * (1.0 - sg)))).astype(BF16)
        elif dproj is not None:
            rest[len(extra) + 2][...] = jnp.zeros((tm, PAD2), BF16)

    return pl.pallas_call(
        body, grid=(T // tm,), in_specs=in_specs, out_specs=out_specs, out_shape=out_shape,
        input_output_aliases=aliases,
        name=f"proj_{tag}_bwd_l{l}", compiler_params=_params(("arbitrary",)))(y, dp, w, *extra)


OUT_TM = 256


def _out_tile(pa, pb, pc, ga, gb, gc, wout, post_g):
    merged = _sigmoid(ga) * pa + _sigmoid(gb) * pb + _sigmoid(gc) * pc
    return _rms(dot_nn(merged, wout), post_g)


def _out_in_specs():
    tm = OUT_TM
    tok = pl.BlockSpec((tm, D), lambda i: (i, 0))
    gate = lambda off: pl.BlockSpec((tm, 512), lambda i, off=off: (i, off // 512))
    return [tok, tok, tok, gate(OFF_GA), gate(OFF_GA + 512), gate(OFF_GB), gate(OFF_GB + 512), gate(OFF_GC),
            gate(OFF_GC + 512), pl.BlockSpec((None, D, D // 2), lambda i: (0, 0, 0)), pl.BlockSpec((1, D), lambda i: (0, 0))]


def _gates(refs):
    return [jnp.concatenate([refs[2 * j][...], refs[2 * j + 1][...]], axis=1) for j in range(3)]


def out_fwd(x, ya, yb, yc, proj, wpa, wpb, wpc, wout, post_g, l):
    tm = OUT_TM

    def body(ya_ref, yb_ref, yc_ref, g0, g1, g2, g3, g4, g5, wo_ref, pg_ref, x_ref, wa_ref, wb_ref, wc_ref,
             o_ref, pa_ref, pb_ref, pc_ref, wa, wb, wc, wo):
        @pl.when(pl.program_id(0) == 0)
        def _():
            for dst, src in ((wa, wa_ref), (wb, wb_ref), (wc, wc_ref), (wo, wo_ref)):
                dst[...] = _unpack(src[...]).astype(BF16)

        pa = _dg(ya_ref[...], wa[...], _NN)
        pb = _dg(yb_ref[...], wb[...], _NN)
        pc = _dg(yc_ref[...], wc[...], _NN)
        ga, gb, gc = _gates([g0, g1, g2, g3, g4, g5])
        o_ref[...] = x_ref[...] + _out_tile(pa, pb, pc, ga, gb, gc, wo[...], pg_ref[...])
        pa_ref[...] = pa.astype(BF16)
        pb_ref[...] = pb.astype(BF16)
        pc_ref[...] = pc.astype(BF16)

    tok = pl.BlockSpec((tm, D), lambda i: (i, 0))
    words = lambda k: pl.BlockSpec((None, k, D // 2), lambda i: (0, 0, 0))
    specs = _out_in_specs()
    specs[2] = pl.BlockSpec((tm, LRU_W), lambda i: (i, 0))
    return pl.pallas_call(
        body, grid=(T // tm,), in_specs=specs + [tok, words(D), words(D), words(LRU_W)], out_specs=[tok] * 4,
        out_shape=[jax.ShapeDtypeStruct((T, D), F32)] + [jax.ShapeDtypeStruct((T, D), BF16)] * 3,
        scratch_shapes=[pltpu.VMEM((D, D), BF16), pltpu.VMEM((D, D), BF16), pltpu.VMEM((LRU_W, D), BF16),
                        pltpu.VMEM((D, D), BF16)],
        name=f"out_fwd_l{l}", compiler_params=_params(("arbitrary",)))(
            ya, yb, yc, proj, proj, proj, proj, proj, proj, wout, post_g, x, wpa, wpb, wpc)


def out_bwd(pa, pb, pc, proj, wout, post_g, dxn, l, dep=None):
    tm = OUT_TM
    nsteps = T // tm

    def body(pa_ref, pb_ref, pc_ref, g0, g1, g2, g3, g4, g5, w_ref, pg_ref, dxn_ref, *rest):
        dpa_ref, dpb_ref, dpc_ref, dproj_ref, dw_ref, dpg_ref, gbuf, sem = rest[-8:]
        i = pl.program_id(0)
        first = i == 0
        slot = i % 2
        ga, gb, gc = _gates([g0, g1, g2, g3, g4, g5])
        _, vjp = jax.vjp(_out_tile, pa_ref[...], pb_ref[...], pc_ref[...], ga, gb, gc, _unpack(w_ref[...]), pg_ref[...])
        dpa, dpb, dpc, dga, dgb, dgc, dw, dpg = vjp(dxn_ref[...])
        dpa_ref[...] = dpa.astype(BF16)
        dpb_ref[...] = dpb.astype(BF16)
        dpc_ref[...] = dpc.astype(BF16)
        _acc(dw_ref, dw, first)
        _acc(dpg_ref, dpg, first)

        def writeback(step, s):
            rows = pl.ds(pl.multiple_of(step * tm, tm), tm)
            return pltpu.make_async_copy(gbuf.at[s], dproj_ref.at[rows, pl.ds(OFF_GA, 3072)], sem.at[s])

        gbuf[slot, :, 0:1024] = dga.astype(BF16)
        gbuf[slot, :, 1024:2048] = dgb.astype(BF16)
        gbuf[slot, :, 2048:3072] = dgc.astype(BF16)
        writeback(i, slot).start()

        @pl.when(i > 0)
        def _():
            writeback(i - 1, 1 - slot).wait()

        @pl.when(i == nsteps - 1)
        def _():
            writeback(i, slot).wait()

    tok = pl.BlockSpec((tm, D), lambda i: (i, 0))
    deps = [] if dep is None else [dep]
    return pl.pallas_call(
        body, grid=(nsteps,), in_specs=_out_in_specs() + [tok] + [ANY] * len(deps),
        out_specs=[tok, tok, tok, ANY, pl.BlockSpec((None, D, D), lambda i: (0, 0, 0)), pl.BlockSpec((1, D), lambda i: (0, 0))],
        out_shape=[jax.ShapeDtypeStruct((T, D), BF16)] * 3 + [jax.ShapeDtypeStruct((T, NPAD), BF16),
                                                            jax.ShapeDtypeStruct((1, D, D), F32), jax.ShapeDtypeStruct((1, D), F32)],
        scratch_shapes=[pltpu.VMEM((2, tm, 3072), BF16), pltpu.SemaphoreType.DMA((2,))],
        name=f"out_bwd_l{l}", compiler_params=_params(("arbitrary",)))(
            pa, pb, pc, proj, proj, proj, proj, proj, proj, wout, post_g, dxn, *deps)


def loss_head(y, target):
    tm = 256

    def body(y_ref, t_ref, loss_ref, dy_ref):
        e = y_ref[...] - t_ref[...]
        dy_ref[...] = e * (1.0 / D)
        val = 0.5 * jnp.sum(jnp.mean(e * e, axis=-1, keepdims=True), axis=0, keepdims=True)
        _acc(loss_ref, jnp.broadcast_to(val, (8, 128)), pl.program_id(0) == 0)

    tok = pl.BlockSpec((tm, D), lambda i: (i, 0))
    total, dy = pl.pallas_call(
        body, grid=(T // tm,), in_specs=[tok, tok],
        out_specs=[pl.BlockSpec((8, 128), lambda i: (0, 0)), tok],
        out_shape=[jax.ShapeDtypeStruct((8, 128), F32), jax.ShapeDtypeStruct((T, D), F32)],
        name="loss_head", compiler_params=_params(("arbitrary",)))(y, target)
    return total[0, 0], dy


def _rope_tables():
    pos = jnp.arange(T, dtype=F32)
    inv_freq = 10000.0 ** (-jnp.arange(0, 64, 2, dtype=F32) / 64)
    ang = pos[:, None] * inv_freq[None, :]
    cos, sin = jnp.cos(ang), jnp.sin(ang)
    ctab = jnp.concatenate([jnp.ones((T, 128), F32), cos, cos], axis=1)
    stab = jnp.concatenate([jnp.zeros((T, 128), F32), -sin, sin], axis=1)
    return ctab, stab


def _layer_fwd(x, l, w, gw, tabs, dep=None, mid=None):
    row = lambda a: a[l][None]
    proj, h = inproj_fwd(x, row(w["pre_norm_g"]), gw["w_in_t"], l, dep)
    ya = gmlp_fwd(proj, row(w["gm_ln_g"]), row(w["gm_ln_b"]), w["gm_ws"][l], w["gm_bs"][l][..., None], l)
    dep2 = None
    if mid is not None:
        gw, dep2 = mid(ya)
    q, k, v = qkv_fwd(proj, row(w["mla_q_norm_g"]), row(w["kv_g384"]), gw["wq"], gw["wkv"], tabs[0], tabs[1], l, dep2)
    yb = attn_fwd(q, k, v, proj, l)
    hseq, yc = lru_fwd(proj, gw["conv"], row(w["lru_conv_b"]), w["lru_w_a"], w["lru_w_x"],
                       row(w["lru_b_a"]), row(w["lru_b_x"]), row(w["lru_lambda"]), l)
    xn, pa, pb, pc = out_fwd(x, ya, yb, yc, proj, gw["w_proj_a"], gw["w_proj_b"], gw["w_proj_c"], gw["w_out"],
                             row(w["post_norm_g"]), l)
    return xn, (x, proj, h, ya, q, k, v, yb, hseq, yc, pa, pb, pc)


def _layer_bwd(dxn, l, w, gw, tabs, saved, dep=None, early=None, mid=None):
    x, proj, h, ya, q, k, v, yb, hseq, yc, pa, pb, pc = saved
    row = lambda a: a[l][None]
    g, gg = {}, {}
    dpa, dpb, dpc, dproj, gg["w_out"], dpost = out_bwd(pa, pb, pc, proj, gw["w_out"], row(w["post_norm_g"]), dxn, l, dep)
    g["post_norm_g"] = dpost[0]
    dep1 = early(dpa) if early is not None else None
    dya, gg["w_proj_a"], dproj = proj_bwd(ya, dpa, gw["w_proj_a"], l, "a", dep1, dproj)
    dyb, gg["w_proj_b"] = proj_bwd(yb, dpb, gw["w_proj_b"], l, "b")
    dyc, gg["w_proj_c"], dproj = proj_bwd(yc, dpc, gw["w_proj_c"], l, "c", None, dproj, (hseq, proj))
    dproj, dln_g, dln_b, g["gm_ws"], dbs = gmlp_bwd(proj, row(w["gm_ln_g"]), row(w["gm_ln_b"]), w["gm_ws"][l],
                                                   w["gm_bs"][l][..., None], dya, dproj, l)
    g["gm_ln_g"], g["gm_ln_b"], g["gm_bs"] = dln_g[0], dln_b[0], dbs[..., 0]
    dq, dk, dv, dproj = attn_bwd(q, k, v, proj, dyb, dproj, l)
    dproj, dqg, dkvg, dwq, dwkv = qkv_bwd(proj, row(w["mla_q_norm_g"]), row(w["kv_g384"]), gw["wq"], gw["wkv"],
                                          tabs[0], tabs[1], dq, dk, dv, dproj, l)
    gg["wq"], gg["wkv"] = dwq.reshape(1, 1536, 384), dwkv.reshape(1, 2048, 256)
    g["mla_q_norm_g"], g["mla_kv_norm_g"] = dqg[0], dkvg[0, :256]
    dproj, dcw, dcb, dwa, dwx, dba, dbx, dlam = lru_bwd(
        proj, hseq, dyc, gw["conv"], row(w["lru_conv_b"]), w["lru_w_a"], w["lru_w_x"],
        row(w["lru_b_a"]), row(w["lru_b_x"]), row(w["lru_lambda"]), dproj, l)
    gg["conv"] = jnp.pad(dcw.T, ((0, 0), (0, 124)))[None]
    g["lru_conv_b"], g["lru_b_a"], g["lru_b_x"], g["lru_lambda"] = dcb[0], dba[0], dbx[0], dlam[0]
    g["lru_w_a"], g["lru_w_x"] = dwa, dwx
    dep2 = mid(gg, dproj) if mid is not None else None
    gg["w_in_t"], dh = inproj_bwd(dproj, h, gw["w_in_t"], l, dep2)
    dx, dpre = prenorm_bwd(x, row(w["pre_norm_g"]), dh, dxn, l)
    g["pre_norm_g"] = dpre[0]
    return dx, gg, g


MESH = pl.DeviceIdType.MESH
HBM = pl.BlockSpec(memory_space=pltpu.HBM)
SEM = pl.BlockSpec(memory_space=pltpu.SEMAPHORE)
EFFECT = pltpu.SideEffectType.DATAFLOW_SIDE_EFFECTING
FLIPS = ((1, 0), (0, 1), (1, 1))


def _win_off(k, s):
    g = SHARD * k + s
    return g + jnp.where(g >= PAD1_AT, PAD1, 0) + jnp.where(g >= PAD2_AT, PAD2, 0)


def _plain_off(rows):
    return lambda k, s: rows * k + s


class Spec:
    def __init__(self, rows, cols, full_rows, pieces=None, off=None, layers=1, packed=None):
        self.rows, self.cols, self.full_rows, self.layers = rows, cols, full_rows, layers
        self.pieces = pieces or ((0, rows),)
        self.off = off or _plain_off(rows)
        self.packed = cols % 256 == 0 if packed is None else packed
        self.wcols = cols // 2 if self.packed else cols

    def to_words(self, a):
        return _pack(a) if self.packed else a

    def from_words(self, p):
        return _unpack(p) if self.packed else p


def _pack(a):
    def bits(v):
        u = lax.bitcast_convert_type(v, jnp.uint32)
        return u + jnp.uint32(0x7FFF) + ((u >> 16) & jnp.uint32(1))

    words = [(bits(a[:, g:g + 128]) >> 16) | (bits(a[:, g + 128:g + 256]) & jnp.uint32(0xFFFF0000))
             for g in range(0, a.shape[-1], 256)]
    return lax.bitcast_convert_type(jnp.concatenate(words, axis=-1) if len(words) > 1 else words[0], F32)


def _unpack(p):
    w = lax.bitcast_convert_type(p, jnp.uint32)
    lo = lax.bitcast_convert_type(w << 16, F32)
    hi = lax.bitcast_convert_type(w & jnp.uint32(0xFFFF0000), F32)
    return jnp.concatenate([h[:, g:g + 128] for g in range(0, p.shape[-1], 128) for h in (lo, hi)], axis=-1)


WEIGHT_SPECS = {
    "w_in_t": Spec(SHARD, D, NPAD, WIN_PIECES, _win_off),
    "wq": Spec(192, 384, 1536),
    "wkv": Spec(256, 256, 2048),
    "conv": Spec(160, 128, 1280),
    "w_proj_a": Spec(128, D, 1024),
    "w_proj_b": Spec(128, D, 1024),
    "w_proj_c": Spec(160, D, 1280),
    "w_out": Spec(128, D, 1024),
}
REP_ROWS = 72
REP_SPEC = Spec(REP_ROWS, D, REP_ROWS * NDEV, packed=False)


def _coords():
    return lax.axis_index("x"), lax.axis_index("y"), lax.axis_index("c")


def _rows(ref, start, n):
    if not isinstance(start, int):
        start = pl.multiple_of(start, 8)
    return ref.at[:, pl.ds(start, n), :]


def _col_tile(cols):
    return 256 if cols % 256 == 0 else cols


def _n_pieces(specs):
    return sum(len(sp.pieces) for sp in specs)


def pack_place(shard, sp, layer, tag, dep=None):
    gaps = ((PAD1_AT, PAD1), (PAD2_AT + PAD1, PAD2)) if sp.off is _win_off else ()
    npc = len(sp.pieces)
    deps = [] if dep is None else [dep]

    def body(s_ref, *rest):
        words_ref, full_ref, buf, zbuf, sem = rest[-5:]
        l = 0
        x, y, c = _coords()
        me = 4 * x + 2 * y + c
        words = sp.to_words(s_ref[...])
        words_ref[...] = words
        buf[...] = words
        copies = [pltpu.make_async_copy(buf.at[pl.ds(s, n), :],
                                        full_ref.at[l, pl.ds(pl.multiple_of(sp.off(me, s), 8), n), :], sem.at[i])
                  for i, (s, n) in enumerate(sp.pieces)]
        if gaps:
            zbuf[...] = jnp.zeros_like(zbuf)
            copies += [pltpu.make_async_copy(zbuf.at[pl.ds(0, n), :], full_ref.at[l, pl.ds(at, n), :], sem.at[npc + i])
                       for i, (at, n) in enumerate(gaps)]
        for cp in copies:
            cp.start()
        for cp in copies:
            cp.wait()

    return pl.pallas_call(
        body, grid=(1,), in_specs=[pl.BlockSpec((None, sp.rows, sp.cols), lambda i: (layer, 0, 0))] + [ANY] * len(deps),
        out_specs=[pl.BlockSpec((None, sp.rows, sp.wcols), lambda i: (0, 0, 0)), ANY],
        out_shape=[jax.ShapeDtypeStruct((sp.layers, sp.rows, sp.wcols), F32),
                   jax.ShapeDtypeStruct((sp.layers, sp.full_rows, sp.wcols), F32)],
        scratch_shapes=[pltpu.VMEM((sp.rows, sp.wcols), F32), pltpu.VMEM((PAD2 if gaps else 8, sp.wcols), F32),
                        pltpu.SemaphoreType.DMA((npc + len(gaps),))],
        name=f"pack_place_{tag}", compiler_params=_params(("arbitrary",)))(shard, *deps)


def _gather_copies(srcs, bufs, specs, ssem, rsem, landing):
    x, y, c = _coords()
    me = 4 * x + 2 * y + c
    targets = [(x, y, 1 - c)] + [(x ^ fx, y ^ fy, c) for fx, fy in FLIPS]
    copies = []
    p = 0
    for src, buf, sp in zip(srcs, bufs, specs):
        for s, n in sp.pieces:
            for t, (tx, ty, tc) in enumerate(targets):
                owner = 4 * tx + 2 * ty + tc if landing else me
                copies.append(pltpu.make_async_remote_copy(_rows(src, s, n), _rows(buf, sp.off(owner, s), n),
                                                           ssem.at[4 * p + t], rsem.at[4 * p + t],
                                                           device_id=(tx, ty, tc), device_id_type=MESH))
            p += 1
    return copies


def gather_send(words, fulls, specs, tag):
    ns, npc = len(specs), _n_pieces(specs)

    def body(*refs):
        srcs, bufs, sems = refs[:ns], refs[2 * ns:3 * ns], refs[3 * ns:]
        for cp in _gather_copies(srcs, bufs, specs, *sems, False):
            cp.start()
        for cp in _gather_copies(srcs, bufs, specs, *sems, False):
            cp.wait_send()
        for cp in _gather_copies(srcs, bufs, specs, *sems, True):
            cp.wait_recv()

    return pl.pallas_call(
        body, in_specs=[ANY] * (2 * ns), out_specs=[ANY] * ns,
        out_shape=[jax.ShapeDtypeStruct(f.shape, f.dtype) for f in fulls],
        input_output_aliases={ns + i: i for i in range(ns)},
        scratch_shapes=[pltpu.SemaphoreType.DMA((4 * npc,)), pltpu.SemaphoreType.DMA((4 * npc,))],
        name=f"gather_send_{tag}", compiler_params=pltpu.CompilerParams(has_side_effects=True))(*words, *fulls)


def _in_hbm(arrays):
    return [pltpu.with_memory_space_constraint(a, pltpu.HBM) for a in arrays]


def gather_start(words, fulls, specs, dep, tag):
    ns, npc = len(specs), _n_pieces(specs)
    deps = [] if dep is None else [dep]

    def body(*refs):
        ssem, rsem = refs[2 * ns + len(deps):2 * ns + len(deps) + 2]
        for cp in _gather_copies(refs[:ns], refs[ns:2 * ns], specs, ssem, rsem, False):
            cp.start()
        refs[-1][...] = jnp.zeros_like(refs[-1])

    outs = pl.pallas_call(
        body, in_specs=[HBM] * (2 * ns) + [ANY] * len(deps),
        out_specs=[SEM, SEM] + [HBM] * (2 * ns) + [pl.BlockSpec(memory_space=pltpu.VMEM)],
        out_shape=[pltpu.SemaphoreType.DMA((4 * npc,)), pltpu.SemaphoreType.DMA((4 * npc,))]
        + [pltpu.HBM(a.shape, a.dtype) for a in list(words) + list(fulls)] + [jax.ShapeDtypeStruct((8, 128), F32)],
        input_output_aliases={i: 2 + i for i in range(2 * ns)},
        name=f"gather_start_{tag}", compiler_params=pltpu.CompilerParams(has_side_effects=EFFECT))(
            *_in_hbm(list(words) + list(fulls)), *deps)
    return outs[0], outs[1], outs[2:2 + ns], outs[2 + ns:2 + 2 * ns], outs[-1]


def gather_wait(ssem, rsem, words, fulls, specs, after, tag):
    ns = len(specs)

    def body(*refs):
        srcs, bufs, ssem, rsem = refs[:ns], refs[ns:2 * ns], refs[2 * ns], refs[2 * ns + 1]
        for cp in _gather_copies(srcs, bufs, specs, ssem, rsem, False):
            cp.wait_send()
        for cp in _gather_copies(srcs, bufs, specs, ssem, rsem, True):
            cp.wait_recv()

    outs = pl.pallas_call(
        body, in_specs=[HBM] * (2 * ns) + [SEM, SEM, ANY], out_specs=[HBM] * (2 * ns),
        out_shape=[pltpu.HBM(a.shape, a.dtype) for a in list(words) + list(fulls)],
        input_output_aliases={i: i for i in range(2 * ns)},
        name=f"gather_wait_{tag}", compiler_params=pltpu.CompilerParams(has_side_effects=EFFECT))(
            *words, *fulls, ssem, rsem, after)
    return outs[ns:]


def gather_forward(fulls, specs, tag):
    ns, npc = len(specs), _n_pieces(specs)

    def body(*refs):
        bufs = refs[ns:2 * ns]
        ssem, rsem = refs[2 * ns:]
        x, y, c = _coords()
        sibling = (x, y, 1 - c)
        waits = []
        p = 0
        for buf, sp in zip(bufs, specs):
            for s, n in sp.pieces:
                for t, (fx, fy) in enumerate(FLIPS):
                    chip = 4 * (x ^ fx) + 2 * (y ^ fy)
                    here = _rows(buf, sp.off(chip + c, s), n)
                    send = pltpu.make_async_remote_copy(here, here, ssem.at[t, p], rsem.at[t, p],
                                                        device_id=sibling, device_id_type=MESH)
                    send.start()
                    waits.append(send.wait_send)
                    there = _rows(buf, sp.off(chip + 1 - c, s), n)
                    waits.append(pltpu.make_async_remote_copy(here, there, ssem.at[t, p], rsem.at[t, p],
                                                              device_id=sibling, device_id_type=MESH).wait_recv)
                p += 1
        for w in waits:
            w()

    return pl.pallas_call(
        body, in_specs=[ANY] * ns, out_specs=[ANY] * ns,
        out_shape=[jax.ShapeDtypeStruct(f.shape, f.dtype) for f in fulls],
        input_output_aliases={i: i for i in range(ns)},
        scratch_shapes=[pltpu.SemaphoreType.DMA((3, npc)), pltpu.SemaphoreType.DMA((3, npc))],
        name=f"gather_forward_{tag}", compiler_params=pltpu.CompilerParams(has_side_effects=True))(*fulls)


def all_gather(shards, layer, specs, names, tag):
    placed = [pack_place(s, sp, layer, f"{tag}_{n}") for s, sp, n in zip(shards, specs, names)]
    fulls = gather_send([p[0] for p in placed], [p[1] for p in placed], specs, tag)
    return gather_forward(fulls, specs, tag)


def _pair_copies(srcs, theirs, specs, ssem, rsem):
    x, y, c = _coords()
    copies = []
    p = 0
    for src, their, sp in zip(srcs, theirs, specs):
        for s, n in sp.pieces:
            for j in range(4):
                copies.append(pltpu.make_async_remote_copy(_rows(src, sp.off(2 * j + 1 - c, s), n), _rows(their.at[j], s, n),
                                                           ssem.at[4 * p + j], rsem.at[4 * p + j],
                                                           device_id=(x, y, 1 - c), device_id_type=MESH))
            p += 1
    return copies


def _pair_shapes(specs):
    return [(4, sp.layers, sp.rows, sp.cols) for sp in specs]


def reduce_pair(grads, specs, tag, dep=None):
    ns, npc = len(specs), _n_pieces(specs)
    deps = [] if dep is None else [dep]

    def body(*refs):
        copies = _pair_copies(refs[:ns], refs[ns + len(deps):2 * ns + len(deps)], specs, *refs[2 * ns + len(deps):])
        for cp in copies:
            cp.start()
        for cp in copies:
            cp.wait()

    return pl.pallas_call(
        body, in_specs=[ANY] * (ns + len(deps)), out_specs=[ANY] * ns,
        out_shape=[jax.ShapeDtypeStruct(s, F32) for s in _pair_shapes(specs)],
        scratch_shapes=[pltpu.SemaphoreType.DMA((4 * npc,)), pltpu.SemaphoreType.DMA((4 * npc,))],
        name=f"reduce_pair_{tag}", compiler_params=pltpu.CompilerParams(has_side_effects=True))(*grads, *deps)


def pair_start(grads, specs, dep, tag):
    ns, npc = len(specs), _n_pieces(specs)
    slots = [lax.empty(s, F32) for s in _pair_shapes(specs)]
    deps = [] if dep is None else [dep]

    def body(*refs):
        ssem, rsem = refs[2 * ns + len(deps):2 * ns + len(deps) + 2]
        for cp in _pair_copies(refs[:ns], refs[ns:2 * ns], specs, ssem, rsem):
            cp.start()
        refs[-1][...] = jnp.zeros_like(refs[-1])

    outs = pl.pallas_call(
        body, in_specs=[HBM] * (2 * ns) + [ANY] * len(deps),
        out_specs=[SEM, SEM] + [HBM] * (2 * ns) + [pl.BlockSpec(memory_space=pltpu.VMEM)],
        out_shape=[pltpu.SemaphoreType.DMA((4 * npc,)), pltpu.SemaphoreType.DMA((4 * npc,))]
        + [pltpu.HBM(a.shape, a.dtype) for a in list(grads) + slots] + [jax.ShapeDtypeStruct((8, 128), F32)],
        input_output_aliases={i: 2 + i for i in range(2 * ns)},
        name=f"pair_start_{tag}", compiler_params=pltpu.CompilerParams(has_side_effects=EFFECT))(
            *_in_hbm(list(grads) + slots), *deps)
    return outs[0], outs[1], outs[2:2 + ns], outs[2 + ns:2 + 2 * ns], outs[-1]


def pair_wait(ssem, rsem, grads, slots, specs, after, tag):
    ns = len(specs)

    def body(*refs):
        for cp in _pair_copies(refs[:ns], refs[ns:2 * ns], specs, refs[2 * ns], refs[2 * ns + 1]):
            cp.wait_send()
            cp.wait_recv()

    outs = pl.pallas_call(
        body, in_specs=[HBM] * (2 * ns) + [SEM, SEM, ANY], out_specs=[HBM] * (2 * ns),
        out_shape=[pltpu.HBM(a.shape, a.dtype) for a in list(grads) + list(slots)],
        input_output_aliases={i: i for i in range(2 * ns)},
        name=f"pair_wait_{tag}", compiler_params=pltpu.CompilerParams(has_side_effects=EFFECT))(
            *grads, *slots, ssem, rsem, after)
    return outs[:ns], outs[ns:]


def pair_sum(g, r1, sp, tag):
    npc = len(sp.pieces)
    fetch_all = 4 * sp.rows * sp.cols * 4 <= (8 << 20)

    def body(g_ref, r_ref, own_ref, words_ref, gbuf, sem):
        l, j = pl.program_id(0), pl.program_id(1)
        x, y, c = _coords()

        def copies(chip, slot):
            return [pltpu.make_async_copy(g_ref.at[l, pl.ds(pl.multiple_of(sp.off(2 * chip + c, s), 8), n), :],
                                          gbuf.at[slot, pl.ds(s, n), :], sem.at[slot, i])
                    for i, (s, n) in enumerate(sp.pieces)]

        def fetch(chip, slot):
            for cp in copies(chip, slot):
                cp.start()

        def arrived(chip, slot):
            for cp in copies(chip, slot):
                cp.wait()

        if fetch_all:
            @pl.when(j == 0)
            def _():
                for chip in range(4):
                    fetch(chip, chip)
                for chip in range(4):
                    arrived(chip, chip)

            mine = gbuf[j]
        else:
            @pl.when(j == 0)
            def _():
                fetch(0, 0)

            @pl.when(j < 3)
            def _():
                fetch(j + 1, (j + 1) % 2)

            arrived(j, j % 2)
            mine = gbuf[j % 2]
        p = mine + r_ref[...]
        words_ref[...] = sp.to_words(p)

        @pl.when(j == 2 * x + y)
        def _():
            own_ref[...] = p

    return pl.pallas_call(
        body, grid=(sp.layers, 4),
        in_specs=[ANY, pl.BlockSpec((None, None, sp.rows, sp.cols), lambda l, j: (j, l, 0, 0))],
        out_specs=[pl.BlockSpec((None, sp.rows, sp.cols), lambda l, j: (l, 0, 0)),
                   pl.BlockSpec((None, None, sp.rows, sp.wcols), lambda l, j: (j, l, 0, 0))],
        out_shape=[jax.ShapeDtypeStruct((sp.layers, sp.rows, sp.cols), F32),
                   jax.ShapeDtypeStruct((4, sp.layers, sp.rows, sp.wcols), F32)],
        scratch_shapes=[pltpu.VMEM((4 if fetch_all else 2, sp.rows, sp.cols), F32), pltpu.SemaphoreType.DMA((4, npc))],
        name=f"pair_sum_{tag}", compiler_params=_params(("arbitrary", "arbitrary")))(g, r1)


def _chip_copies(srcs, dsts, ssem, rsem):
    x, y, c = _coords()
    copies = []
    for i, (src, dst) in enumerate(zip(srcs, dsts)):
        for t, (fx, fy) in enumerate(FLIPS):
            tx, ty = x ^ fx, y ^ fy
            copies.append(pltpu.make_async_remote_copy(src.at[2 * tx + ty], dst.at[t], ssem.at[3 * i + t], rsem.at[3 * i + t],
                                                       device_id=(tx, ty, c), device_id_type=MESH))
    return copies


def _slot_shapes(words):
    return [(3,) + w.shape[1:] for w in words]


def reduce_chips(words, specs, tag):
    ns = len(specs)

    def body(*refs):
        copies = _chip_copies(refs[:ns], refs[ns:2 * ns], *refs[2 * ns:])
        for cp in copies:
            cp.start()
        for cp in copies:
            cp.wait()

    return pl.pallas_call(
        body, in_specs=[ANY] * ns, out_specs=[ANY] * ns,
        out_shape=[jax.ShapeDtypeStruct(s, F32) for s in _slot_shapes(words)],
        scratch_shapes=[pltpu.SemaphoreType.DMA((3 * ns,)), pltpu.SemaphoreType.DMA((3 * ns,))],
        name=f"reduce_chips_{tag}", compiler_params=pltpu.CompilerParams(has_side_effects=True))(*words)


def chips_start(words, specs, tag):
    ns = len(specs)
    slots = [lax.empty(s, F32) for s in _slot_shapes(words)]

    def body(*refs):
        ssem, rsem = refs[2 * ns:2 * ns + 2]
        for cp in _chip_copies(refs[:ns], refs[ns:2 * ns], ssem, rsem):
            cp.start()
        refs[-1][...] = jnp.zeros_like(refs[-1])

    outs = pl.pallas_call(
        body, in_specs=[HBM] * (2 * ns),
        out_specs=[SEM, SEM] + [HBM] * (2 * ns) + [pl.BlockSpec(memory_space=pltpu.VMEM)],
        out_shape=[pltpu.SemaphoreType.DMA((3 * ns,)), pltpu.SemaphoreType.DMA((3 * ns,))]
        + [pltpu.HBM(a.shape, a.dtype) for a in list(words) + slots] + [jax.ShapeDtypeStruct((8, 128), F32)],
        input_output_aliases={i: 2 + i for i in range(2 * ns)},
        name=f"chips_start_{tag}", compiler_params=pltpu.CompilerParams(has_side_effects=EFFECT))(
            *_in_hbm(list(words) + slots))
    return outs[0], outs[1], outs[2:2 + ns], outs[2 + ns:2 + 2 * ns], outs[-1]


def chips_wait(ssem, rsem, words, slots, specs, after, tag):
    ns = len(specs)

    def body(*refs):
        for cp in _chip_copies(refs[:ns], refs[ns:2 * ns], refs[2 * ns], refs[2 * ns + 1]):
            cp.wait_send()
            cp.wait_recv()

    outs = pl.pallas_call(
        body, in_specs=[HBM] * (2 * ns) + [SEM, SEM, ANY], out_specs=[HBM] * (2 * ns),
        out_shape=[pltpu.HBM(a.shape, a.dtype) for a in list(words) + list(slots)],
        input_output_aliases={i: i for i in range(2 * ns)},
        name=f"chips_wait_{tag}", compiler_params=pltpu.CompilerParams(has_side_effects=EFFECT))(
            *words, *slots, ssem, rsem, after)
    return outs[ns:]


def sum_chips(own, r2, sp, tag):
    def body(own_ref, r_ref, o_ref):
        o_ref[...] = ((own_ref[...] + sp.from_words(r_ref[0])) + sp.from_words(r_ref[1])) + sp.from_words(r_ref[2])

    blk = pl.BlockSpec((None, sp.rows, sp.cols), lambda l: (l, 0, 0))
    return pl.pallas_call(
        body, grid=(sp.layers,), in_specs=[blk, pl.BlockSpec((3, None, sp.rows, sp.wcols), lambda l: (0, l, 0, 0))],
        out_specs=blk, out_shape=jax.ShapeDtypeStruct((sp.layers, sp.rows, sp.cols), F32),
        name=f"sum_chips_{tag}", compiler_params=_params(("arbitrary",)))(own, r2)


def reduce_scatter_start(grads, specs, names, dep, tag):
    theirs = reduce_pair(grads, specs, tag, dep)
    sums = [pair_sum(g, r1, sp, f"{tag}_{n}") for g, r1, sp, n in zip(grads, theirs, specs, names)]
    ssem, rsem, words, slots, token = chips_start([s[1] for s in sums], specs, tag)
    return (ssem, rsem, words, slots, [s[0] for s in sums]), token


def reduce_scatter_finish(state, after, specs, tag):
    ssem, rsem, words, slots, own = state
    return list(zip(own, chips_wait(ssem, rsem, words, slots, specs, after, tag)))


def reduce_scatter(grads, specs, names, tag):
    theirs = reduce_pair(grads, specs, tag)
    sums = [pair_sum(g, r1, sp, f"{tag}_{n}") for g, r1, sp, n in zip(grads, theirs, specs, names)]
    return list(zip([s[0] for s in sums], reduce_chips([s[1] for s in sums], specs, tag)))


def _adamw_math(w, g, m, v):
    c1 = 1.0 - ADAM_B1 ** ADAM_STEP
    c2 = 1.0 - ADAM_B2 ** ADAM_STEP
    m2 = ADAM_B1 * m + (1.0 - ADAM_B1) * g
    v2 = ADAM_B2 * v + (1.0 - ADAM_B2) * (g * g)
    return -ADAM_LR * ((m2 / c1) / (jnp.sqrt(v2 / c2) + ADAM_EPS) + ADAM_WD * w), m2, v2


def adamw(w, g, m, v, name):
    shape = w.shape
    cols = shape[-1]
    rows = math.prod(shape[:-1])
    tr = rows
    while tr * cols * 4 > (1 << 20) and tr % 16 == 0:
        tr //= 2

    def body(w_ref, g_ref, m_ref, v_ref, d_ref, nm_ref, nv_ref):
        d_ref[...], nm_ref[...], nv_ref[...] = _adamw_math(w_ref[...], g_ref[...], m_ref[...], v_ref[...])

    blk = pl.BlockSpec((tr, cols), lambda i: (i, 0))
    outs = pl.pallas_call(
        body, grid=(rows // tr,), in_specs=[blk] * 4, out_specs=[blk] * 3,
        out_shape=[jax.ShapeDtypeStruct((rows, cols), F32)] * 3,
        name=f"adamw_{name}", compiler_params=_params(("arbitrary",)))(
            *[a.reshape(rows, cols) for a in (w, g, m, v)])
    return [o.reshape(shape) for o in outs]


def adamw_layer(w, sums, m, v, sp, l, prev, dep, name):
    _, rows, cols = w.shape
    tc = _col_tile(cols)
    twc = tc // 2 if sp.packed else tc
    extra = ([] if prev is None else list(prev)) + ([] if dep is None else [dep])

    def body(w_ref, own_ref, r_ref, m_ref, v_ref, *rest):
        g_ref, d_ref, nm_ref, nv_ref = rest[-4:]
        g = ((own_ref[...] + sp.from_words(r_ref[0])) + sp.from_words(r_ref[1])) + sp.from_words(r_ref[2])
        g_ref[...] = g
        d_ref[...], nm_ref[...], nv_ref[...] = _adamw_math(w_ref[...], g, m_ref[...], v_ref[...])

    blk = pl.BlockSpec((None, rows, tc), lambda n: (l, 0, n))
    return pl.pallas_call(
        body, grid=(cols // tc,),
        in_specs=[blk, pl.BlockSpec((None, rows, tc), lambda n: (0, 0, n)),
                  pl.BlockSpec((3, None, rows, twc), lambda n: (0, 0, 0, n)), blk, blk] + [ANY] * len(extra),
        out_specs=[blk] * 4, out_shape=[jax.ShapeDtypeStruct(w.shape, F32)] * 4,
        input_output_aliases={} if prev is None else {5 + i: i for i in range(4)},
        name=f"adamw_{name}_l{l}", compiler_params=_params(("arbitrary",)))(w, sums[0], sums[1], m, v, *extra)


WEIGHTS = ("pre_norm_g", "w_in", "gm_ln_g", "gm_ln_b", "gm_ws", "gm_bs", "mla_q_norm_g", "mla_w_uq", "mla_kv_norm_g",
           "mla_w_ukv", "lru_conv_w", "lru_conv_b", "lru_w_a", "lru_b_a", "lru_w_x", "lru_b_x", "lru_lambda",
           "w_proj_a", "w_proj_b", "w_proj_c", "w_out", "post_norm_g")
SHARDED = ("w_in", "mla_w_uq", "mla_w_ukv", "lru_conv_w", "w_proj_a", "w_proj_b", "w_proj_c", "w_out")
REPLICATED = tuple(n for n in WEIGHTS if n not in SHARDED)


def _step(x, target, wts, ms, vs):
    t12 = lambda a: jnp.swapaxes(a, 1, 2)
    names = list(WEIGHT_SPECS)
    specs = [WEIGHT_SPECS[n] for n in names]
    tabs = _rope_tables()
    own = {"w_in_t": t12(wts["w_in"]), "wq": t12(wts["mla_w_uq"]), "wkv": t12(wts["mla_w_ukv"]),
           "conv": jnp.pad(t12(wts["lru_conv_w"]), ((0, 0), (0, 0), (0, 124))),
           "w_proj_a": wts["w_proj_a"], "w_proj_b": wts["w_proj_b"], "w_proj_c": wts["w_proj_c"], "w_out": wts["w_out"]}
    first, rest = ["w_in_t"], [n for n in names if n != "w_in_t"]
    sfirst, srest = [WEIGHT_SPECS[n] for n in first], [WEIGHT_SPECS[n] for n in rest]

    w = {n: wts[n] for n in REPLICATED}
    w["kv_g384"] = jnp.concatenate([wts["mla_kv_norm_g"], jnp.ones((L, 128), F32)], axis=1)

    def layer_weights(ns, words):
        gw = dict(zip(ns, words))
        gw["wq"] = gw["wq"].reshape(HEADS, 192, 384)
        gw["wkv"] = gw["wkv"].reshape(HEADS, 256, 128)
        gw["conv"] = gw["conv"][0, :, :4].T
        return gw

    place = lambda l, dep: {n: pack_place(own[n], WEIGHT_SPECS[n], l, f"w{l}_{n}", dep) for n in names}
    placed = [place(0, None)]
    words_of = lambda l, ns: [placed[l][n][0] for n in ns]
    bufs_of = lambda l, ns: [placed[l][n][1] for n in ns]
    later = {}

    ssem_a, rsem_a, wthru_a, fthru_a, token_a = gather_start(words_of(0, first), bufs_of(0, first), sfirst, None, "w0a")
    placed.append(place(1, token_a))
    win0 = gather_forward(gather_wait(ssem_a, rsem_a, wthru_a, fthru_a, sfirst, placed[1]["w_in_t"][0], "w0a"), sfirst, "w0a")
    ssem_b, rsem_b, wthru_b, fthru_b, token_b = gather_start(words_of(0, rest), bufs_of(0, rest), srest, win0[0], "w0b")

    def fwd0_mid(ya):
        rest0 = gather_forward(gather_wait(ssem_b, rsem_b, wthru_b, fthru_b, srest, ya, "w0b"), srest, "w0b")
        later["w1"] = gather_start(words_of(1, names), bufs_of(1, names), specs, rest0[0], "w1")
        later["gw0"] = layer_weights(first + rest, list(win0) + list(rest0))
        return later["gw0"], later["w1"][4]

    x1, saved0 = _layer_fwd(x, 0, w, {"w_in_t": win0[0]}, tabs, dep=token_b, mid=fwd0_mid)
    ssem1, rsem1, wthru1, fthru1, _ = later["w1"]
    words1 = gather_forward(gather_wait(ssem1, rsem1, wthru1, fthru1, specs, x1, "w1"), specs, "w1")
    gw0, gw1 = later["gw0"], layer_weights(names, words1)
    x2, saved1 = _layer_fwd(x1, 1, w, gw1, tabs)
    loss, dx2 = loss_head(x2, target)

    def bwd1_mid(gg, last):
        later["p1b"] = pair_start([gg[n] for n in rest], srest, last, "g1b")
        return later["p1b"][4]

    dx1, gg1, g1 = _layer_bwd(dx2, 1, w, gw1, tabs, saved1, mid=bwd1_mid)
    grads1b, theirs1b = pair_wait(*later["p1b"][:4], srest, dx1, "g1b")
    p1a = pair_start([gg1["w_in_t"]], sfirst, theirs1b[0], "g1a")

    def bwd0_early(last):
        grads1a, theirs1a = pair_wait(*p1a[:4], sfirst, last, "g1a")
        mine = dict(zip(first + rest, list(grads1a) + list(grads1b)))
        theirs = dict(zip(first + rest, list(theirs1a) + list(theirs1b)))
        sums = [pair_sum(mine[n], theirs[n], WEIGHT_SPECS[n], f"g1_{n}") for n in names]
        ssem, rsem, words, slots, token = chips_start([s[1] for s in sums], specs, "g1")
        later["g1"] = (ssem, rsem, words, slots, [s[0] for s in sums])
        return token

    def bwd0_mid(gg, last):
        later["g0b"], token = reduce_scatter_start([gg[n] for n in rest], srest, rest, last, "g0b")
        return token

    dx0, gg0, g0 = _layer_bwd(dx1, 0, w, gw0, tabs, saved0, dep=p1a[4], early=bwd0_early, mid=bwd0_mid)
    s1 = dict(zip(names, reduce_scatter_finish(later["g1"], dx0, specs, "g1")))
    s0 = dict(zip(rest, reduce_scatter_finish(later["g0b"], dx0, srest, "g0b")))
    rep_flat = jnp.concatenate([jnp.stack([g0[n], g1[n]]).reshape(-1) for n in REPLICATED])
    rep_flat = jnp.pad(rep_flat, (0, REP_ROWS * NDEV * D - rep_flat.shape[0])).reshape(1, REP_ROWS * NDEV, D)
    state_a, token_g = reduce_scatter_start([gg0["w_in_t"], rep_flat], sfirst + [REP_SPEC], first + ["rep"], None, "g0a")

    keys = {"w_in": "w_in_t", "mla_w_uq": "wq", "mla_w_ukv": "wkv",
            "w_proj_a": "w_proj_a", "w_proj_b": "w_proj_b", "w_proj_c": "w_proj_c", "w_out": "w_out"}
    transposed = ("w_in", "mla_w_uq", "mla_w_ukv")
    state_of = lambda n: [own[keys[n]], t12(ms[n]), t12(vs[n])] if n in transposed else [wts[n], ms[n], vs[n]]

    def update(n, l, sums, prev, dep):
        wl, ml, vl = state_of(n)
        return adamw_layer(wl, sums[keys[n]], ml, vl, WEIGHT_SPECS[keys[n]], l, prev, dep, n)

    upd = {n: update(n, 1, s1, None, token_g) for n in keys}
    for n in keys:
        if n != "w_in":
            upd[n] = update(n, 0, s0, upd[n], None)
    s0["w_in_t"], rep_parts = reduce_scatter_finish(state_a, upd["w_out"][0], sfirst + [REP_SPEC], "g0a")
    upd["w_in"] = update("w_in", 0, s0, upd["w_in"], None)
    rep_sum = sum_chips(*rep_parts, REP_SPEC, "rep")
    rep_full = all_gather([rep_sum], 0, [REP_SPEC], ["rep"], "rep")[0].reshape(-1)

    out = {n: [t12(r) for r in upd[n]] if n in transposed else upd[n] for n in keys}
    conv_sp = WEIGHT_SPECS["conv"]
    g_conv = t12(jnp.concatenate([sum_chips(*s0["conv"], conv_sp, "conv0"), sum_chips(*s1["conv"], conv_sp, "conv1")])[:, :, :4])
    out["lru_conv_w"] = [g_conv] + adamw(wts["lru_conv_w"], g_conv, ms["lru_conv_w"], vs["lru_conv_w"], "lru_conv_w")
    at = 0
    for n in REPLICATED:
        size = math.prod(wts[n].shape)
        g = rep_full[at:at + size].reshape(wts[n].shape)
        out[n] = [g] + adamw(wts[n], g, ms[n], vs[n], n)
        at += size

    loss = lax.psum(loss, ("x", "y", "c"))
    return (loss, dx0[None], *[out[n][k] for k in range(4) for n in WEIGHTS])


def kernel(x, pre_norm_g, w_in, gm_ln_g, gm_ln_b, gm_ws, gm_bs, mla_q_norm_g, mla_w_uq, mla_kv_norm_g, mla_w_ukv, lru_conv_w, lru_conv_b, lru_w_a, lru_b_a, lru_w_x, lru_b_x, lru_lambda, w_proj_a, w_proj_b, w_proj_c, w_out, post_norm_g, loss_target, m_pre_norm_g, m_w_in, m_gm_ln_g, m_gm_ln_b, m_gm_ws, m_gm_bs, m_mla_q_norm_g, m_mla_w_uq, m_mla_kv_norm_g, m_mla_w_ukv, m_lru_conv_w, m_lru_conv_b, m_lru_w_a, m_lru_b_a, m_lru_w_x, m_lru_b_x, m_lru_lambda, m_w_proj_a, m_w_proj_b, m_w_proj_c, m_w_out, m_post_norm_g, v_pre_norm_g, v_w_in, v_gm_ln_g, v_gm_ln_b, v_gm_ws, v_gm_bs, v_mla_q_norm_g, v_mla_w_uq, v_mla_kv_norm_g, v_mla_w_ukv, v_lru_conv_w, v_lru_conv_b, v_lru_w_a, v_lru_b_a, v_lru_w_x, v_lru_b_x, v_lru_lambda, v_w_proj_a, v_w_proj_b, v_w_proj_c, v_w_out, v_post_norm_g):
    wts = dict(zip(WEIGHTS, (pre_norm_g, w_in, gm_ln_g, gm_ln_b, gm_ws, gm_bs, mla_q_norm_g, mla_w_uq, mla_kv_norm_g,
                             mla_w_ukv, lru_conv_w, lru_conv_b, lru_w_a, lru_b_a, lru_w_x, lru_b_x, lru_lambda,
                             w_proj_a, w_proj_b, w_proj_c, w_out, post_norm_g)))
    ms = dict(zip(WEIGHTS, (m_pre_norm_g, m_w_in, m_gm_ln_g, m_gm_ln_b, m_gm_ws, m_gm_bs, m_mla_q_norm_g, m_mla_w_uq,
                            m_mla_kv_norm_g, m_mla_w_ukv, m_lru_conv_w, m_lru_conv_b, m_lru_w_a, m_lru_b_a, m_lru_w_x,
                            m_lru_b_x, m_lru_lambda, m_w_proj_a, m_w_proj_b, m_w_proj_c, m_w_out, m_post_norm_g)))
    vs = dict(zip(WEIGHTS, (v_pre_norm_g, v_w_in, v_gm_ln_g, v_gm_ln_b, v_gm_ws, v_gm_bs, v_mla_q_norm_g, v_mla_w_uq,
                            v_mla_kv_norm_g, v_mla_w_ukv, v_lru_conv_w, v_lru_conv_b, v_lru_w_a, v_lru_b_a, v_lru_w_x,
                            v_lru_b_x, v_lru_lambda, v_w_proj_a, v_w_proj_b, v_w_proj_c, v_w_out, v_post_norm_g)))
    return _step(x[0], loss_target[0], wts, ms, vs)
```

```python
import functools
import math

import jax
import jax.numpy as jnp
from jax import lax
from jax.experimental import pallas as pl
from jax.experimental.pallas import tpu as pltpu

F32 = jnp.float32
BF16 = jnp.bfloat16

T = 2048
D = 1024
L = 2
NDEV = 8
EPS = 1e-6
CHUNK_SHIFT = 6
HEADS = 8
QK = 192
LRU_W = 1280
LRU_TILE = 640
N_IN = 10432
SHARD = N_IN // NDEV
OFF_U, OFF_V, OFF_ZA, OFF_CQ, OFF_CKV, OFF_ZB = 0, 1024, 2048, 3072, 3456, 3840
OFF_XC, OFF_ZC, OFF_GA, OFF_GB, OFF_GC = 5120, 6400, 7680, 8704, 9728
NPAD = 10752
PAD1_AT, PAD1 = 3776, 64
PAD2_AT, PAD2 = 4800, 256
WIN_PIECES = ((0, 888), (888, 280), (1168, 136))
VMEM_LIMIT = 60 * 1024 * 1024

ADAM_LR, ADAM_B1, ADAM_B2, ADAM_EPS, ADAM_WD, ADAM_STEP = 0.001, 0.9, 0.999, 1e-08, 0.01, 10

_NN = (((1,), (0,)), ((), ()))
_NT = (((1,), (1,)), ((), ()))
_TN = (((0,), (0,)), ((), ()))


def _dg(a, b, dims):
    return lax.dot_general(a.astype(BF16), b.astype(BF16), dims, preferred_element_type=F32)


@jax.custom_vjp
def dot_nn(a, b):
    return _dg(a, b, _NN)


def _nn_fwd(a, b):
    return _dg(a, b, _NN), (a, b)


def _nn_bwd(res, g):
    a, b = res
    return _dg(g, b, _NT).astype(a.dtype), _dg(a, g, _TN).astype(b.dtype)


dot_nn.defvjp(_nn_fwd, _nn_bwd)


@jax.custom_vjp
def dot_nt(a, b):
    return _dg(a, b, _NT)


def _nt_fwd(a, b):
    return _dg(a, b, _NT), (a, b)


def _nt_bwd(res, g):
    a, b = res
    return _dg(g, b, _NN).astype(a.dtype), _dg(g, a, _TN).astype(b.dtype)


dot_nt.defvjp(_nt_fwd, _nt_bwd)


def _params(sem=None):
    return pltpu.CompilerParams(dimension_semantics=sem, vmem_limit_bytes=VMEM_LIMIT)


def _sigmoid(x):
    return 1.0 / (1.0 + jnp.exp(-x))


def _silu(x):
    return x * _sigmoid(x)


def _rms(x, g):
    ms = jnp.mean(x * x, axis=-1, keepdims=True)
    return x * lax.rsqrt(ms + EPS) * g


def _acc(ref, val, first):
    @pl.when(first)
    def _():
        ref[...] = val

    @pl.when(jnp.logical_not(first))
    def _():
        ref[...] += val


ANY = pl.BlockSpec(memory_space=pl.ANY)


INPROJ_TN = 768


def inproj_fwd(x, g, wt, l, dep=None):
    tn = INPROJ_TN

    def body(x_ref, g_ref, w_ref, *rest):
        proj_ref, h_ref = rest[-2:]

        @pl.when(pl.program_id(0) == 0)
        def _():
            h_ref[...] = _rms(x_ref[...], g_ref[...]).astype(BF16)

        proj_ref[...] = lax.dot_general(h_ref[...], _unpack(w_ref[...]).astype(BF16), _NT, preferred_element_type=F32)

    deps = [] if dep is None else [dep]
    return pl.pallas_call(
        body, grid=(NPAD // tn,),
        in_specs=[pl.BlockSpec((T, D), lambda j: (0, 0)), pl.BlockSpec((1, D), lambda j: (0, 0)),
                  pl.BlockSpec((None, tn, D // 2), lambda j: (0, j, 0))] + [ANY] * len(deps),
        out_specs=[pl.BlockSpec((T, tn), lambda j: (0, j)), pl.BlockSpec((T, D), lambda j: (0, 0))],
        out_shape=[jax.ShapeDtypeStruct((T, NPAD), F32), jax.ShapeDtypeStruct((T, D), BF16)],
        name=f"inproj_fwd_l{l}", compiler_params=_params(("arbitrary",)))(x, g, wt, *deps)


def inproj_bwd(dproj, h, wt, l, dep=None):
    tn = INPROJ_TN
    deps = [] if dep is None else [dep]

    def body(dp_ref, h_ref, w_ref, *rest):
        dwt_ref, dh_ref = rest[-2:]
        dp = dp_ref[...]
        dwt_ref[...] = lax.dot_general(dp, h_ref[...], _TN, preferred_element_type=F32)
        contrib = lax.dot_general(dp, _unpack(w_ref[...]).astype(BF16), _NN, preferred_element_type=F32)
        _acc(dh_ref, contrib, pl.program_id(0) == 0)

    return pl.pallas_call(
        body, grid=(NPAD // tn,),
        in_specs=[pl.BlockSpec((T, tn), lambda j: (0, j)), pl.BlockSpec((T, D), lambda j: (0, 0)),
                  pl.BlockSpec((None, tn, D // 2), lambda j: (0, j, 0))] + [ANY] * len(deps),
        out_specs=[pl.BlockSpec((None, tn, D), lambda j: (0, j, 0)), pl.BlockSpec((T, D), lambda j: (0, 0))],
        out_shape=[jax.ShapeDtypeStruct((1, NPAD, D), F32), jax.ShapeDtypeStruct((T, D), F32)],
        name=f"inproj_bwd_l{l}", compiler_params=_params(("arbitrary",)))(dproj, h, wt, *deps)


def prenorm_bwd(x, g, dh, dxn, l, dep=None):
    tm = 512
    deps = [] if dep is None else [dep]

    def body(x_ref, g_ref, dh_ref, dxn_ref, *rest):
        dx_ref, dg_ref = rest[-2:]
        _, vjp = jax.vjp(_rms, x_ref[...], g_ref[...])
        dx, dg = vjp(dh_ref[...])
        dx_ref[...] = dx + dxn_ref[...]
        _acc(dg_ref, dg, pl.program_id(0) == 0)

    tok = pl.BlockSpec((tm, D), lambda i: (i, 0))
    vec = pl.BlockSpec((1, D), lambda i: (0, 0))
    return pl.pallas_call(
        body, grid=(T // tm,), in_specs=[tok, vec, tok, tok] + [ANY] * len(deps), out_specs=[tok, vec],
        out_shape=[jax.ShapeDtypeStruct((T, D), F32), jax.ShapeDtypeStruct((1, D), F32)],
        name=f"prenorm_bwd_l{l}", compiler_params=_params(("arbitrary",)))(x, g, dh, dxn, *deps)


def _gmlp_tile(u, v, z, ln_g, ln_b, ws, bs):
    mu = jnp.mean(v, axis=-1, keepdims=True)
    vc = v - mu
    var = jnp.mean(vc * vc, axis=-1, keepdims=True)
    vn = vc * lax.rsqrt(var + EPS) * ln_g + ln_b
    qi = lax.broadcasted_iota(jnp.int32, (128, 128), 0) >> CHUNK_SHIFT
    kj = lax.broadcasted_iota(jnp.int32, (128, 128), 1) >> CHUNK_SHIFT
    mask = kj <= qi
    outs = []
    for g in range(4):
        wm = jnp.where(mask, ws[g], 0.0)
        outs.append(dot_nn(wm, vn[:, 256 * g:256 * (g + 1)]) + bs[g])
    sv = jnp.concatenate(outs, axis=1)
    return u * sv * _silu(z)


GMLP_ROWS = 256


def _gmlp_specs():
    blk = lambda c: pl.BlockSpec((GMLP_ROWS, 1024), lambda n, c=c: (n, c))
    vec = pl.BlockSpec((1, 1024), lambda n: (0, 0))
    return [blk(0), blk(1), blk(2), vec, vec,
            pl.BlockSpec((4, 128, 128), lambda n: (0, 0, 0)), pl.BlockSpec((4, 128, 1), lambda n: (0, 0, 0))]


def gmlp_fwd(proj, ln_g, ln_b, ws, bs, l):
    def body(u_ref, v_ref, z_ref, g_ref, b_ref, ws_ref, bs_ref, y_ref):
        for r in range(0, GMLP_ROWS, 128):
            rows = slice(r, r + 128)
            y_ref[rows, :] = _gmlp_tile(u_ref[rows, :], v_ref[rows, :], z_ref[rows, :], g_ref[...], b_ref[...],
                                        [ws_ref[g] for g in range(4)], [bs_ref[g] for g in range(4)])

    return pl.pallas_call(
        body, grid=(T // GMLP_ROWS,), in_specs=_gmlp_specs(),
        out_specs=pl.BlockSpec((GMLP_ROWS, 1024), lambda n: (n, 0)),
        out_shape=jax.ShapeDtypeStruct((T, 1024), F32),
        name=f"gmlp_fwd_l{l}", compiler_params=_params(("arbitrary",)))(proj, proj, proj, ln_g, ln_b, ws, bs)


def gmlp_bwd(proj, ln_g, ln_b, ws, bs, dy, dproj, l):
    def body(u_ref, v_ref, z_ref, g_ref, b_ref, ws_ref, bs_ref, dy_ref, _, dseg_ref, dg_ref, db_ref, dws_ref, dbs_ref):
        for r in range(0, GMLP_ROWS, 128):
            rows = slice(r, r + 128)
            first = jnp.logical_and(pl.program_id(0) == 0, r == 0)
            _, vjp = jax.vjp(_gmlp_tile, u_ref[rows, :], v_ref[rows, :], z_ref[rows, :], g_ref[...], b_ref[...],
                             [ws_ref[g] for g in range(4)], [bs_ref[g] for g in range(4)])
            du, dv, dz, dg, db, dws, dbs = vjp(dy_ref[rows, :])
            dseg_ref[rows, 0:1024] = du.astype(BF16)
            dseg_ref[rows, 1024:2048] = dv.astype(BF16)
            dseg_ref[rows, 2048:3072] = dz.astype(BF16)
            _acc(dg_ref, dg, first)
            _acc(db_ref, db, first)
            for g in range(4):
                _acc(dws_ref.at[g], dws[g], first)
                _acc(dbs_ref.at[g], dbs[g], first)

    vec = pl.BlockSpec((1, 1024), lambda n: (0, 0))
    return pl.pallas_call(
        body, grid=(T // GMLP_ROWS,),
        in_specs=_gmlp_specs() + [pl.BlockSpec((GMLP_ROWS, 1024), lambda n: (n, 0)), ANY],
        out_specs=[pl.BlockSpec((GMLP_ROWS, 3072), lambda n: (n, OFF_U // 3072)), vec, vec,
                   pl.BlockSpec((4, 128, 128), lambda n: (0, 0, 0)), pl.BlockSpec((4, 128, 1), lambda n: (0, 0, 0))],
        out_shape=[jax.ShapeDtypeStruct((T, NPAD), BF16), jax.ShapeDtypeStruct((1, 1024), F32),
                   jax.ShapeDtypeStruct((1, 1024), F32), jax.ShapeDtypeStruct((4, 128, 128), F32),
                   jax.ShapeDtypeStruct((4, 128, 1), F32)],
        input_output_aliases={8: 0},
        name=f"gmlp_bwd_l{l}", compiler_params=_params(("arbitrary",)))(proj, proj, proj, ln_g, ln_b, ws, bs, dy, dproj)


QKV_TM = 512


def _qkv_tile(cq, ckvr, qg, kvg, wq, wkv, ctab, stab):
    tm = cq.shape[0]
    cqn = _rms(cq, qg)
    lane = lax.broadcasted_iota(jnp.int32, ckvr.shape, 1)
    iskv = lane < 256
    ms = jnp.sum(jnp.where(iskv, ckvr * ckvr, 0.0), axis=-1, keepdims=True) * (1.0 / 256)
    lm = jnp.where(iskv, ckvr * lax.rsqrt(ms + EPS) * kvg, ckvr)
    r = lax.broadcasted_iota(jnp.int32, (64, 128), 0)
    c = lax.broadcasted_iota(jnp.int32, (64, 128), 1)
    eye = jnp.where(c == r, 1.0, 0.0)
    eye_sw = jnp.where(c == ((r + 32) & 63), 1.0, 0.0)
    z64 = jnp.zeros((64, 256), F32)
    z128 = jnp.zeros((128, 128), F32)
    rk_rope = jnp.concatenate([z64, eye], axis=1)
    rk_sw = jnp.concatenate([jnp.zeros((128, 384), F32), jnp.concatenate([z64, eye_sw], axis=1)], axis=0)
    k_sw = dot_nt(lm, rk_sw) * stab
    qs, ks, vs = [], [], []
    for h in range(HEADS):
        wn, w1, w2 = wq[h]
        wk, wv = wkv[h]
        wq_h = jnp.concatenate([wn, w1, w2], axis=0)
        wq_sw = jnp.concatenate([jnp.zeros((128, 384), F32), w2, w1], axis=0)
        qs.append(dot_nt(cqn, wq_h) * ctab + dot_nt(cqn, wq_sw) * stab)
        rk_h = jnp.concatenate([jnp.concatenate([wk, z128], axis=1), rk_rope], axis=0)
        ks.append(dot_nt(lm, rk_h) * ctab + k_sw)
        vs.append(dot_nt(lm, jnp.concatenate([wv, z128], axis=1)))
    return qs, ks, vs


def _qkv_in_specs():
    tm = QKV_TM
    return [pl.BlockSpec((tm, 384), lambda i: (i, OFF_CQ // 384)), pl.BlockSpec((tm, 384), lambda i: (i, OFF_CKV // 384)),
            pl.BlockSpec((1, 384), lambda i: (0, 0)), pl.BlockSpec((1, 384), lambda i: (0, 0)),
            pl.BlockSpec((HEADS, 192, 384), lambda i: (0, 0, 0)), pl.BlockSpec((HEADS, 256, 128), lambda i: (0, 0, 0)),
            pl.BlockSpec((tm, 192), lambda i: (i, 0)), pl.BlockSpec((tm, 192), lambda i: (i, 0))]


def _qkv_weights(wq_ref, wkv_ref):
    wq = [(wq_ref[h, 0:128, :], wq_ref[h, 128:160, :], wq_ref[h, 160:192, :]) for h in range(HEADS)]
    wkv = [(_unpack(wkv_ref[h, 0:128, :]), _unpack(wkv_ref[h, 128:256, :])) for h in range(HEADS)]
    return wq, wkv


def qkv_fwd(proj, qg, kvg, wq, wkv, ctab, stab, l, dep=None):
    tm = QKV_TM
    deps = [] if dep is None else [dep]

    def body(cq_ref, ckvr_ref, qg_ref, kvg_ref, wq_ref, wkv_ref, c_ref, s_ref, *rest):
        q_ref, k_ref, v_ref = rest[-3:]
        wq_l, wkv_l = _qkv_weights(wq_ref, wkv_ref)
        qs, ks, vs = _qkv_tile(cq_ref[...], ckvr_ref[...], qg_ref[...], kvg_ref[...], wq_l, wkv_l, c_ref[...], s_ref[...])
        for h in range(HEADS):
            q_ref[h] = qs[h]
            k_ref[h] = ks[h]
            v_ref[h] = vs[h]

    return pl.pallas_call(
        body, grid=(T // tm,), in_specs=_qkv_in_specs() + [ANY] * len(deps),
        out_specs=[pl.BlockSpec((HEADS, tm, QK), lambda i: (0, i, 0)), pl.BlockSpec((HEADS, tm, QK), lambda i: (0, i, 0)),
                   pl.BlockSpec((HEADS, tm, 128), lambda i: (0, i, 0))],
        out_shape=[jax.ShapeDtypeStruct((HEADS, T, QK), F32), jax.ShapeDtypeStruct((HEADS, T, QK), F32),
                   jax.ShapeDtypeStruct((HEADS, T, 128), F32)],
        name=f"qkv_fwd_l{l}", compiler_params=_params(("arbitrary",)))(proj, proj, qg, kvg, wq, wkv, ctab, stab, *deps)


def qkv_bwd(proj, qg, kvg, wq, wkv, ctab, stab, dq, dk, dv, dproj, l):
    tm = QKV_TM

    def body(cq_ref, ckvr_ref, qg_ref, kvg_ref, wq_ref, wkv_ref, c_ref, s_ref, dq_ref, dk_ref, dv_ref, _,
             dseg_ref, dqg_ref, dkvg_ref, dwq_ref, dwkv_ref):
        first = pl.program_id(0) == 0
        wq_l, wkv_l = _qkv_weights(wq_ref, wkv_ref)
        c_tab, s_tab = c_ref[...], s_ref[...]
        fn = lambda cq, ckvr, qg_, kvg_, wq_, wkv_: _qkv_tile(cq, ckvr, qg_, kvg_, wq_, wkv_, c_tab, s_tab)
        _, vjp = jax.vjp(fn, cq_ref[...], ckvr_ref[...], qg_ref[...], kvg_ref[...], wq_l, wkv_l)
        cts = ([dq_ref[h] for h in range(HEADS)], [dk_ref[h] for h in range(HEADS)], [dv_ref[h] for h in range(HEADS)])
        dcq, dckvr, dqg, dkvg, dwq, dwkv = vjp(cts)
        dseg_ref[:, 0:384] = dcq.astype(BF16)
        dseg_ref[:, 384:768] = dckvr.astype(BF16)
        _acc(dqg_ref, dqg, first)
        _acc(dkvg_ref, dkvg, first)
        for h in range(HEADS):
            _acc(dwq_ref.at[h, 0:128, :], dwq[h][0], first)
            _acc(dwq_ref.at[h, 128:160, :], dwq[h][1], first)
            _acc(dwq_ref.at[h, 160:192, :], dwq[h][2], first)
            _acc(dwkv_ref.at[h, 0:128, :], dwkv[h][0], first)
            _acc(dwkv_ref.at[h, 128:256, :], dwkv[h][1], first)

    hq = pl.BlockSpec((HEADS, tm, QK), lambda i: (0, i, 0))
    return pl.pallas_call(
        body, grid=(T // tm,),
        in_specs=_qkv_in_specs() + [hq, hq, pl.BlockSpec((HEADS, tm, 128), lambda i: (0, i, 0)), ANY],
        out_specs=[pl.BlockSpec((tm, 768), lambda i: (i, OFF_CQ // 768)), pl.BlockSpec((1, 384), lambda i: (0, 0)),
                   pl.BlockSpec((1, 384), lambda i: (0, 0)), pl.BlockSpec((HEADS, 192, 384), lambda i: (0, 0, 0)),
                   pl.BlockSpec((HEADS, 256, 256), lambda i: (0, 0, 0))],
        out_shape=[jax.ShapeDtypeStruct((T, NPAD), BF16), jax.ShapeDtypeStruct((1, 384), F32),
                   jax.ShapeDtypeStruct((1, 384), F32), jax.ShapeDtypeStruct((HEADS, 192, 384), F32),
                   jax.ShapeDtypeStruct((HEADS, 256, 256), F32)],
        input_output_aliases={11: 0},
        name=f"qkv_bwd_l{l}", compiler_params=_params(("arbitrary",)))(
            proj, proj, qg, kvg, wq, wkv, ctab, stab, dq, dk, dv, dproj)


ATT_TQ_FWD = 256
ATT_TQ_BWD = 512


def _attn_tile(q, kv_past, k, v, zb):
    q = q * (1.0 / math.sqrt(QK))
    s = dot_nt(q, k)
    qc = lax.broadcasted_iota(jnp.int32, s.shape, 0) >> CHUNK_SHIFT
    kc = lax.broadcasted_iota(jnp.int32, s.shape, 1) >> CHUNK_SHIFT
    s = jnp.where(kc <= qc, s, -1e30)
    m = jnp.max(s, axis=-1, keepdims=True)
    if kv_past is not None:
        sp = dot_nt(q, kv_past[0])
        m = jnp.maximum(m, jnp.max(sp, axis=-1, keepdims=True))
    m = lax.stop_gradient(m)
    p = jnp.exp(s - m)
    denom = jnp.sum(p, axis=-1, keepdims=True)
    o = dot_nn(p, v)
    if kv_past is not None:
        pp = jnp.exp(sp - m)
        denom = denom + jnp.sum(pp, axis=-1, keepdims=True)
        o = o + dot_nn(pp, kv_past[1])
    return o * (1.0 / denom) * _silu(zb)


def _attn_operands(k_ref, v_ref, g, tq):
    n = tq * g
    past = (k_ref[0:n, :], v_ref[0:n, :]) if g else None
    return past, k_ref[n:n + tq, :], v_ref[n:n + tq, :]


def _attn_in_specs(tq):
    return [pl.BlockSpec((None, tq, QK), lambda h, i: (h, i, 0)), pl.BlockSpec((None, T, QK), lambda h, i: (h, 0, 0)),
            pl.BlockSpec((None, T, 128), lambda h, i: (h, 0, 0)),
            pl.BlockSpec((tq, 128), lambda h, i: (i, OFF_ZB // 128 + h))]


def attn_fwd(q, k, v, proj, l):
    tq = ATT_TQ_FWD

    def body(q_ref, k_ref, v_ref, z_ref, y_ref):
        for g in range(T // tq):
            @pl.when(pl.program_id(1) == g)
            def _(g=g):
                past, k, v = _attn_operands(k_ref, v_ref, g, tq)
                y_ref[...] = _attn_tile(q_ref[...], past, k, v, z_ref[...])

    return pl.pallas_call(
        body, grid=(HEADS, T // tq), in_specs=_attn_in_specs(tq),
        out_specs=pl.BlockSpec((tq, 128), lambda h, i: (i, h)),
        out_shape=jax.ShapeDtypeStruct((T, 1024), F32),
        name=f"attn_fwd_l{l}", compiler_params=_params(("arbitrary", "arbitrary")))(q, k, v, proj)


def attn_bwd(q, k, v, proj, dy, dproj, l):
    tq = ATT_TQ_BWD

    def body(q_ref, k_ref, v_ref, z_ref, dy_ref, _, dq_ref, dk_ref, dv_ref, dz_ref):
        @pl.when(pl.program_id(1) == 0)
        def _():
            dk_ref[...] = jnp.zeros_like(dk_ref)
            dv_ref[...] = jnp.zeros_like(dv_ref)

        for g in range(T // tq):
            @pl.when(pl.program_id(1) == g)
            def _(g=g):
                n = tq * g
                past, k, v = _attn_operands(k_ref, v_ref, g, tq)
                _, vjp = jax.vjp(_attn_tile, q_ref[...], past, k, v, z_ref[...])
                dq, dpast, dk, dv, dz = vjp(dy_ref[...])
                dq_ref[...] = dq
                dz_ref[...] = dz.astype(BF16)
                dk_ref[n:n + tq, :] += dk
                dv_ref[n:n + tq, :] += dv
                if g:
                    dk_ref[0:n, :] += dpast[0]
                    dv_ref[0:n, :] += dpast[1]

    return pl.pallas_call(
        body, grid=(HEADS, T // tq),
        in_specs=_attn_in_specs(tq) + [pl.BlockSpec((tq, 128), lambda h, i: (i, h)), ANY],
        out_specs=[pl.BlockSpec((None, tq, QK), lambda h, i: (h, i, 0)), pl.BlockSpec((None, T, QK), lambda h, i: (h, 0, 0)),
                   pl.BlockSpec((None, T, 128), lambda h, i: (h, 0, 0)),
                   pl.BlockSpec((tq, 128), lambda h, i: (i, OFF_ZB // 128 + h))],
        out_shape=[jax.ShapeDtypeStruct((HEADS, T, QK), F32), jax.ShapeDtypeStruct((HEADS, T, QK), F32),
                   jax.ShapeDtypeStruct((HEADS, T, 128), F32), jax.ShapeDtypeStruct((T, NPAD), BF16)],
        input_output_aliases={5: 3},
        name=f"attn_bwd_l{l}", compiler_params=_params(("arbitrary", "arbitrary")))(q, k, v, proj, dy, dproj)


LRU_TT = 256


def _lru_gates(xc, wa, wx, ba, bx, lam):
    r = _sigmoid(dot_nn(xc, wa) + ba)
    i = _sigmoid(dot_nn(xc, wx) + bx)
    sp = jnp.maximum(-lam, 0.0) + jnp.log1p(jnp.exp(-jnp.abs(lam)))
    log_a = -8.0 * r * sp
    a = jnp.exp(log_a)
    mult = jnp.sqrt(jnp.maximum(1.0 - jnp.exp(2.0 * log_a), 0.0))
    return a, mult * (i * xc)


def _shift_down(x, s, halo):
    n, c = x.shape
    r = pltpu.roll(x.reshape(n // 8, 8, c), s, 1)
    before = jnp.concatenate([pltpu.roll(halo, s, 0)[None], r[:-1]], axis=0)
    sub = lax.broadcasted_iota(jnp.int32, r.shape, 1)
    return jnp.where(sub >= s, r, before).reshape(n, c)


def _shift_up(x, s, halo):
    n, c = x.shape
    r = pltpu.roll(x.reshape(n // 8, 8, c), 8 - s, 1)
    after = jnp.concatenate([r[1:], pltpu.roll(halo, 8 - s, 0)[None]], axis=0)
    sub = lax.broadcasted_iota(jnp.int32, r.shape, 1)
    return jnp.where(sub < 8 - s, r, after).reshape(n, c)


def _conv(x, halo, w_ref, b):
    return (w_ref[3:4, :] * x + w_ref[2:3, :] * _shift_down(x, 1, halo) + w_ref[1:2, :] * _shift_down(x, 2, halo)
            + w_ref[0:1, :] * _shift_down(x, 3, halo) + b)


def _scan(a, b, reverse, carry):
    n, c = a.shape
    a, b = a.reshape(n // 8, 8, c), b.reshape(n // 8, 8, c)
    sub = lax.broadcasted_iota(jnp.int32, a.shape, 1)
    for d in (1, 2, 4):
        keep = sub < 8 - d if reverse else sub >= d
        shift = 8 - d if reverse else d
        a_sh = jnp.where(keep, pltpu.roll(a, shift, 1), 1.0)
        b_sh = jnp.where(keep, pltpu.roll(b, shift, 1), 0.0)
        b = a * b_sh + b
        a = a * a_sh
    a, b = a.reshape(n, c), b.reshape(n, c)
    groups = [None] * (n // 8)
    for g in (reversed(range(n // 8)) if reverse else range(n // 8)):
        h = a[8 * g:8 * g + 8] * carry + b[8 * g:8 * g + 8]
        groups[g] = h
        carry = h[0:1] if reverse else h[7:8]
    return jnp.concatenate(groups, axis=0), carry


def _lru_param_specs(l):
    ct = LRU_TILE
    vec = pl.BlockSpec((1, ct), lambda n, i: (0, n))
    mat = pl.BlockSpec((None, 8, 80, 80), lambda n, i: (l, n, 0, 0))
    return [pl.BlockSpec((4, ct), lambda n, i: (0, n)), vec, mat, mat, vec, vec, vec]


def _blocks_to_dense(w_ref, dense):
    dense[...] = jnp.zeros_like(dense)
    for b in range(8):
        dense[80 * b:80 * b + 80, 80 * b:80 * b + 80] = w_ref[b]


def _dense_to_blocks(dense, w_ref):
    for b in range(8):
        w_ref[b] = dense[80 * b:80 * b + 80, 80 * b:80 * b + 80]


def lru_fwd(proj, conv_w, conv_b, wa, wx, ba, bx, lam, l):
    tt, ct = LRU_TT, LRU_TILE

    def body(x_ref, z_ref, cw_ref, cb_ref, wa_ref, wx_ref, ba_ref, bx_ref, lam_ref, h_ref, y_ref, halo, hcar, wa, wx):
        @pl.when(pl.program_id(1) == 0)
        def _():
            halo[...] = jnp.zeros_like(halo)
            hcar[...] = jnp.zeros_like(hcar)
            _blocks_to_dense(wa_ref, wa)
            _blocks_to_dense(wx_ref, wx)

        x = x_ref[...]
        xc = _conv(x, halo[...], cw_ref, cb_ref[...])
        halo[...] = x[tt - 8:tt]
        a, b = _lru_gates(xc, wa[...], wx[...], ba_ref[...], bx_ref[...], lam_ref[...])
        h, hcar[...] = _scan(a, b, False, hcar[...])
        h_ref[...] = h
        y_ref[...] = h * _silu(z_ref[...])

    seq = pl.BlockSpec((tt, ct), lambda n, i: (i, n))
    return pl.pallas_call(
        body, grid=(LRU_W // ct, T // tt),
        in_specs=[pl.BlockSpec((tt, ct), lambda n, i: (i, OFF_XC // ct + n)),
                  pl.BlockSpec((tt, ct), lambda n, i: (i, OFF_ZC // ct + n))] + _lru_param_specs(l),
        out_specs=[seq, seq],
        out_shape=[jax.ShapeDtypeStruct((T, LRU_W), F32), jax.ShapeDtypeStruct((T, LRU_W), F32)],
        scratch_shapes=[pltpu.VMEM((8, ct), F32), pltpu.VMEM((1, ct), F32), pltpu.VMEM((ct, ct), F32),
                        pltpu.VMEM((ct, ct), F32)],
        name=f"lru_fwd_l{l}", compiler_params=_params(("arbitrary", "arbitrary")))(
            proj, proj, conv_w, conv_b, wa, wx, ba, bx, lam)


def lru_bwd(proj, hseq, dy, conv_w, conv_b, wa, wx, ba, bx, lam, dproj, l):
    tt, ct = LRU_TT, LRU_TILE
    nt = T // tt
    rev = lambda i: nt - 1 - i
    prev8 = lambda i: jnp.maximum(rev(i) * (tt // 8) - 1, 0)

    def body(x_ref, xh_ref, z_ref, h_ref, hh_ref, dy_ref, cw_ref, cb_ref, wa_ref, wx_ref, ba_ref, bx_ref, lam_ref, _,
             dx_ref, dcw_ref, dcb_ref, dwa_ref, dwx_ref, dba_ref, dbx_ref, dlam_ref, gcar, dhalo,
             wa, wx, dwa_acc, dwx_acc):
        i = pl.program_id(1)
        first = i == 0

        @pl.when(first)
        def _():
            gcar[...] = jnp.zeros_like(gcar)
            dhalo[...] = jnp.zeros_like(dhalo)
            _blocks_to_dense(wa_ref, wa)
            _blocks_to_dense(wx_ref, wx)

        at_start = rev(i) == 0
        x = x_ref[...]
        xhalo = jnp.where(at_start, 0.0, xh_ref[...])
        sh = [x, _shift_down(x, 1, xhalo), _shift_down(x, 2, xhalo), _shift_down(x, 3, xhalo)]
        xc = (cw_ref[3:4, :] * sh[0] + cw_ref[2:3, :] * sh[1] + cw_ref[1:2, :] * sh[2] + cw_ref[0:1, :] * sh[3]
              + cb_ref[...])
        (a, b), vjp = jax.vjp(_lru_gates, xc, wa[...], wx[...], ba_ref[...], bx_ref[...], lam_ref[...])
        hs = h_ref[...]
        hprev = _shift_down(hs, 1, jnp.where(at_start, 0.0, hh_ref[...]))
        dh = dy_ref[...] * _silu(z_ref[...])
        a_next = _shift_up(a, 1, jnp.ones((8, ct), F32))
        g, _ = _scan(a_next, dh, True, gcar[...])
        dxc, dwa, dwx, dba, dbx, dlam = vjp((g * hprev, g))
        dx = (cw_ref[3:4, :] * dxc + cw_ref[2:3, :] * _shift_up(dxc, 1, dhalo[...])
              + cw_ref[1:2, :] * _shift_up(dxc, 2, dhalo[...]) + cw_ref[0:1, :] * _shift_up(dxc, 3, dhalo[...]))
        dx_ref[...] = dx.astype(BF16)
        dhalo[...] = dxc[0:8]
        ag = a * g
        gcar[...] = ag[0:1]
        dcw = jnp.concatenate([jnp.sum(dxc * sh[3 - j], axis=0, keepdims=True) for j in range(4)], axis=0)
        _acc(dcw_ref, dcw, first)
        _acc(dcb_ref, jnp.sum(dxc, axis=0, keepdims=True), first)
        _acc(dwa_acc, dwa, first)
        _acc(dwx_acc, dwx, first)

        @pl.when(i == nt - 1)
        def _():
            _dense_to_blocks(dwa_acc, dwa_ref)
            _dense_to_blocks(dwx_acc, dwx_ref)

        _acc(dba_ref, dba, first)
        _acc(dbx_ref, dbx, first)
        _acc(dlam_ref, dlam, first)

    xcol = OFF_XC // ct
    zcol = OFF_ZC // ct
    vec = pl.BlockSpec((1, ct), lambda n, i: (0, n))
    mat = pl.BlockSpec((8, 80, 80), lambda n, i: (n, 0, 0))
    seq = pl.BlockSpec((tt, ct), lambda n, i: (rev(i), n))
    return pl.pallas_call(
        body, grid=(LRU_W // ct, nt),
        in_specs=[pl.BlockSpec((tt, ct), lambda n, i: (rev(i), xcol + n)),
                  pl.BlockSpec((8, ct), lambda n, i: (prev8(i), xcol + n)),
                  pl.BlockSpec((tt, ct), lambda n, i: (rev(i), zcol + n)),
                  seq, pl.BlockSpec((8, ct), lambda n, i: (prev8(i), n)), seq] + _lru_param_specs(l) + [ANY],
        out_specs=[pl.BlockSpec((tt, ct), lambda n, i: (rev(i), xcol + n)),
                   pl.BlockSpec((4, ct), lambda n, i: (0, n)), vec, mat, mat, vec, vec, vec],
        out_shape=[jax.ShapeDtypeStruct((T, NPAD), BF16),
                   jax.ShapeDtypeStruct((4, LRU_W), F32), jax.ShapeDtypeStruct((1, LRU_W), F32),
                   jax.ShapeDtypeStruct((16, 80, 80), F32), jax.ShapeDtypeStruct((16, 80, 80), F32),
                   jax.ShapeDtypeStruct((1, LRU_W), F32), jax.ShapeDtypeStruct((1, LRU_W), F32),
                   jax.ShapeDtypeStruct((1, LRU_W), F32)],
        scratch_shapes=[pltpu.VMEM((1, ct), F32), pltpu.VMEM((8, ct), F32)] + [pltpu.VMEM((ct, ct), F32)] * 4,
        input_output_aliases={13: 0},
        name=f"lru_bwd_l{l}", compiler_params=_params(("arbitrary", "arbitrary")))(
            proj, proj, proj, hseq, hseq, dy, conv_w, conv_b, wa, wx, ba, bx, lam, dproj)


def proj_bwd(y, dp, w, l, tag, dep=None, dproj=None, gate=None):
    tm = 512
    k = y.shape[1]
    extra = [] if dep is None else [dep]
    in_specs = [pl.BlockSpec((tm, k), lambda i: (i, 0)), pl.BlockSpec((tm, D), lambda i: (i, 0)),
                pl.BlockSpec((None, k, D // 2), lambda i: (0, 0, 0))]
    out_specs = [pl.BlockSpec((tm, k), lambda i: (i, 0)), pl.BlockSpec((None, k, D), lambda i: (0, 0, 0))]
    out_shape = [jax.ShapeDtypeStruct((T, k), F32), jax.ShapeDtypeStruct((1, k, D), F32)]
    aliases = {}
    if gate is not None:
        in_specs += [pl.BlockSpec((tm, k), lambda i: (i, 0)), pl.BlockSpec((tm, k), lambda i: (i, OFF_ZC // k))]
        extra = list(gate) + extra
    if dproj is not None:
        width = k if gate is not None else PAD2
        at = OFF_ZC if gate is not None else OFF_XC - PAD2
        aliases = {3 + len(extra): 2}
        extra = extra + [dproj]
        out_specs.append(pl.BlockSpec((tm, width), lambda i: (i, at // width)))
        out_shape.append(jax.ShapeDtypeStruct((T, NPAD), BF16))
    in_specs += [ANY] * (3 + len(extra) - len(in_specs))

    def body(y_ref, dp_ref, w_ref, *rest):
        dy_ref, dw_ref = rest[len(extra):len(extra) + 2]
        dp = dp_ref[...]
        dy = _dg(dp, _unpack(w_ref[...]), _NT)
        dy_ref[...] = dy
        _acc(dw_ref, _dg(y_ref[...], dp, _TN), pl.program_id(0) == 0)
        if gate is not None:
            z = rest[1][...]
            sg = _sigmoid(z)
            rest[len(extra) + 2][...] = (dy * rest[0][...] * (sg * (1.0 + z * (1.0 - sg)))).astype(BF16)
        elif dproj is not None:
            rest[len(extra) + 2][...] = jnp.zeros((tm, PAD2), BF16)

    return pl.pallas_call(
        body, grid=(T // tm,), in_specs=in_specs, out_specs=out_specs, out_shape=out_shape,
        input_output_aliases=aliases,
        name=f"proj_{tag}_bwd_l{l}", compiler_params=_params(("arbitrary",)))(y, dp, w, *extra)


OUT_TM = 256


def _out_tile(pa, pb, pc, ga, gb, gc, wout, post_g):
    merged = _sigmoid(ga) * pa + _sigmoid(gb) * pb + _sigmoid(gc) * pc
    return _rms(dot_nn(merged, wout), post_g)


def _out_in_specs():
    tm = OUT_TM
    tok = pl.BlockSpec((tm, D), lambda i: (i, 0))
    gate = lambda off: pl.BlockSpec((tm, 512), lambda i, off=off: (i, off // 512))
    return [tok, tok, tok, gate(OFF_GA), gate(OFF_GA + 512), gate(OFF_GB), gate(OFF_GB + 512), gate(OFF_GC),
            gate(OFF_GC + 512), pl.BlockSpec((None, D, D // 2), lambda i: (0, 0, 0)), pl.BlockSpec((1, D), lambda i: (0, 0))]


def _gates(refs):
    return [jnp.concatenate([refs[2 * j][...], refs[2 * j + 1][...]], axis=1) for j in range(3)]


def out_fwd(x, ya, yb, yc, proj, wpa, wpb, wpc, wout, post_g, l):
    tm = OUT_TM

    def body(ya_ref, yb_ref, yc_ref, g0, g1, g2, g3, g4, g5, wo_ref, pg_ref, x_ref, wa_ref, wb_ref, wc_ref,
             o_ref, pa_ref, pb_ref, pc_ref, wa, wb, wc, wo):
        @pl.when(pl.program_id(0) == 0)
        def _():
            for dst, src in ((wa, wa_ref), (wb, wb_ref), (wc, wc_ref), (wo, wo_ref)):
                dst[...] = _unpack(src[...]).astype(BF16)

        pa = _dg(ya_ref[...], wa[...], _NN)
        pb = _dg(yb_ref[...], wb[...], _NN)
        pc = _dg(yc_ref[...], wc[...], _NN)
        ga, gb, gc = _gates([g0, g1, g2, g3, g4, g5])
        o_ref[...] = x_ref[...] + _out_tile(pa, pb, pc, ga, gb, gc, wo[...], pg_ref[...])
        pa_ref[...] = pa.astype(BF16)
        pb_ref[...] = pb.astype(BF16)
        pc_ref[...] = pc.astype(BF16)

    tok = pl.BlockSpec((tm, D), lambda i: (i, 0))
    words = lambda k: pl.BlockSpec((None, k, D // 2), lambda i: (0, 0, 0))
    specs = _out_in_specs()
    specs[2] = pl.BlockSpec((tm, LRU_W), lambda i: (i, 0))
    return pl.pallas_call(
        body, grid=(T // tm,), in_specs=specs + [tok, words(D), words(D), words(LRU_W)], out_specs=[tok] * 4,
        out_shape=[jax.ShapeDtypeStruct((T, D), F32)] + [jax.ShapeDtypeStruct((T, D), BF16)] * 3,
        scratch_shapes=[pltpu.VMEM((D, D), BF16), pltpu.VMEM((D, D), BF16), pltpu.VMEM((LRU_W, D), BF16),
                        pltpu.VMEM((D, D), BF16)],
        name=f"out_fwd_l{l}", compiler_params=_params(("arbitrary",)))(
            ya, yb, yc, proj, proj, proj, proj, proj, proj, wout, post_g, x, wpa, wpb, wpc)


def out_bwd(pa, pb, pc, proj, wout, post_g, dxn, l, dep=None):
    tm = OUT_TM
    nsteps = T // tm

    def body(pa_ref, pb_ref, pc_ref, g0, g1, g2, g3, g4, g5, w_ref, pg_ref, dxn_ref, *rest):
        dpa_ref, dpb_ref, dpc_ref, dproj_ref, dw_ref, dpg_ref, gbuf, sem = rest[-8:]
        i = pl.program_id(0)
        first = i == 0
        slot = i % 2
        ga, gb, gc = _gates([g0, g1, g2, g3, g4, g5])
        _, vjp = jax.vjp(_out_tile, pa_ref[...], pb_ref[...], pc_ref[...], ga, gb, gc, _unpack(w_ref[...]), pg_ref[...])
        dpa, dpb, dpc, dga, dgb, dgc, dw, dpg = vjp(dxn_ref[...])
        dpa_ref[...] = dpa.astype(BF16)
        dpb_ref[...] = dpb.astype(BF16)
        dpc_ref[...] = dpc.astype(BF16)
        _acc(dw_ref, dw, first)
        _acc(dpg_ref, dpg, first)

        def writeback(step, s):
            rows = pl.ds(pl.multiple_of(step * tm, tm), tm)
            return pltpu.make_async_copy(gbuf.at[s], dproj_ref.at[rows, pl.ds(OFF_GA, 3072)], sem.at[s])

        gbuf[slot, :, 0:1024] = dga.astype(BF16)
        gbuf[slot, :, 1024:2048] = dgb.astype(BF16)
        gbuf[slot, :, 2048:3072] = dgc.astype(BF16)
        writeback(i, slot).start()

        @pl.when(i > 0)
        def _():
            writeback(i - 1, 1 - slot).wait()

        @pl.when(i == nsteps - 1)
        def _():
            writeback(i, slot).wait()

    tok = pl.BlockSpec((tm, D), lambda i: (i, 0))
    deps = [] if dep is None else [dep]
    return pl.pallas_call(
        body, grid=(nsteps,), in_specs=_out_in_specs() + [tok] + [ANY] * len(deps),
        out_specs=[tok, tok, tok, ANY, pl.BlockSpec((None, D, D), lambda i: (0, 0, 0)), pl.BlockSpec((1, D), lambda i: (0, 0))],
        out_shape=[jax.ShapeDtypeStruct((T, D), BF16)] * 3 + [jax.ShapeDtypeStruct((T, NPAD), BF16),
                                                            jax.ShapeDtypeStruct((1, D, D), F32), jax.ShapeDtypeStruct((1, D), F32)],
        scratch_shapes=[pltpu.VMEM((2, tm, 3072), BF16), pltpu.SemaphoreType.DMA((2,))],
        name=f"out_bwd_l{l}", compiler_params=_params(("arbitrary",)))(
            pa, pb, pc, proj, proj, proj, proj, proj, proj, wout, post_g, dxn, *deps)


def loss_head(y, target):
    tm = 256

    def body(y_ref, t_ref, loss_ref, dy_ref):
        e = y_ref[...] - t_ref[...]
        dy_ref[...] = e * (1.0 / D)
        val = 0.5 * jnp.sum(jnp.mean(e * e, axis=-1, keepdims=True), axis=0, keepdims=True)
        _acc(loss_ref, jnp.broadcast_to(val, (8, 128)), pl.program_id(0) == 0)

    tok = pl.BlockSpec((tm, D), lambda i: (i, 0))
    total, dy = pl.pallas_call(
        body, grid=(T // tm,), in_specs=[tok, tok],
        out_specs=[pl.BlockSpec((8, 128), lambda i: (0, 0)), tok],
        out_shape=[jax.ShapeDtypeStruct((8, 128), F32), jax.ShapeDtypeStruct((T, D), F32)],
        name="loss_head", compiler_params=_params(("arbitrary",)))(y, target)
    return total[0, 0], dy


def _rope_tables():
    pos = jnp.arange(T, dtype=F32)
    inv_freq = 10000.0 ** (-jnp.arange(0, 64, 2, dtype=F32) / 64)
    ang = pos[:, None] * inv_freq[None, :]
    cos, sin = jnp.cos(ang), jnp.sin(ang)
    ctab = jnp.concatenate([jnp.ones((T, 128), F32), cos, cos], axis=1)
    stab = jnp.concatenate([jnp.zeros((T, 128), F32), -sin, sin], axis=1)
    return ctab, stab


def _layer_fwd(x, l, w, gw, tabs, dep=None, mid=None):
    row = lambda a: a[l][None]
    proj, h = inproj_fwd(x, row(w["pre_norm_g"]), gw["w_in_t"], l, dep)
    ya = gmlp_fwd(proj, row(w["gm_ln_g"]), row(w["gm_ln_b"]), w["gm_ws"][l], w["gm_bs"][l][..., None], l)
    dep2 = None
    if mid is not None:
        gw, dep2 = mid(ya)
    q, k, v = qkv_fwd(proj, row(w["mla_q_norm_g"]), row(w["kv_g384"]), gw["wq"], gw["wkv"], tabs[0], tabs[1], l, dep2)
    yb = attn_fwd(q, k, v, proj, l)
    hseq, yc = lru_fwd(proj, gw["conv"], row(w["lru_conv_b"]), w["lru_w_a"], w["lru_w_x"],
                       row(w["lru_b_a"]), row(w["lru_b_x"]), row(w["lru_lambda"]), l)
    xn, pa, pb, pc = out_fwd(x, ya, yb, yc, proj, gw["w_proj_a"], gw["w_proj_b"], gw["w_proj_c"], gw["w_out"],
                             row(w["post_norm_g"]), l)
    return xn, (x, proj, h, ya, q, k, v, yb, hseq, yc, pa, pb, pc)


def _layer_bwd(dxn, l, w, gw, tabs, saved, dep=None, early=None, mid=None, late=None):
    x, proj, h, ya, q, k, v, yb, hseq, yc, pa, pb, pc = saved
    row = lambda a: a[l][None]
    g, gg = {}, {}
    dpa, dpb, dpc, dproj, gg["w_out"], dpost = out_bwd(pa, pb, pc, proj, gw["w_out"], row(w["post_norm_g"]), dxn, l, dep)
    g["post_norm_g"] = dpost[0]
    dep1 = early(dpa) if early is not None else None
    dya, gg["w_proj_a"], dproj = proj_bwd(ya, dpa, gw["w_proj_a"], l, "a", dep1, dproj)
    dyb, gg["w_proj_b"] = proj_bwd(yb, dpb, gw["w_proj_b"], l, "b")
    dyc, gg["w_proj_c"], dproj = proj_bwd(yc, dpc, gw["w_proj_c"], l, "c", None, dproj, (hseq, proj))
    dproj, dln_g, dln_b, g["gm_ws"], dbs = gmlp_bwd(proj, row(w["gm_ln_g"]), row(w["gm_ln_b"]), w["gm_ws"][l],
                                                   w["gm_bs"][l][..., None], dya, dproj, l)
    g["gm_ln_g"], g["gm_ln_b"], g["gm_bs"] = dln_g[0], dln_b[0], dbs[..., 0]
    dq, dk, dv, dproj = attn_bwd(q, k, v, proj, dyb, dproj, l)
    dproj, dqg, dkvg, dwq, dwkv = qkv_bwd(proj, row(w["mla_q_norm_g"]), row(w["kv_g384"]), gw["wq"], gw["wkv"],
                                          tabs[0], tabs[1], dq, dk, dv, dproj, l)
    gg["wq"], gg["wkv"] = dwq.reshape(1, 1536, 384), dwkv.reshape(1, 2048, 256)
    g["mla_q_norm_g"], g["mla_kv_norm_g"] = dqg[0], dkvg[0, :256]
    dproj, dcw, dcb, dwa, dwx, dba, dbx, dlam = lru_bwd(
        proj, hseq, dyc, gw["conv"], row(w["lru_conv_b"]), w["lru_w_a"], w["lru_w_x"],
        row(w["lru_b_a"]), row(w["lru_b_x"]), row(w["lru_lambda"]), dproj, l)
    gg["conv"] = jnp.pad(dcw.T, ((0, 0), (0, 124)))[None]
    g["lru_conv_b"], g["lru_b_a"], g["lru_b_x"], g["lru_lambda"] = dcb[0], dba[0], dbx[0], dlam[0]
    g["lru_w_a"], g["lru_w_x"] = dwa, dwx
    dep2 = mid(gg, dproj) if mid is not None else None
    gg["w_in_t"], dh = inproj_bwd(dproj, h, gw["w_in_t"], l, dep2)
    dep3 = late(gg["w_in_t"]) if late is not None else None
    dx, dpre = prenorm_bwd(x, row(w["pre_norm_g"]), dh, dxn, l, dep3)
    g["pre_norm_g"] = dpre[0]
    return dx, gg, g


MESH = pl.DeviceIdType.MESH
HBM = pl.BlockSpec(memory_space=pltpu.HBM)
SEM = pl.BlockSpec(memory_space=pltpu.SEMAPHORE)
EFFECT = pltpu.SideEffectType.DATAFLOW_SIDE_EFFECTING
FLIPS = ((1, 0), (0, 1), (1, 1))


def _win_off(k, s):
    g = SHARD * k + s
    return g + jnp.where(g >= PAD1_AT, PAD1, 0) + jnp.where(g >= PAD2_AT, PAD2, 0)


def _plain_off(rows):
    return lambda k, s: rows * k + s


class Spec:
    def __init__(self, rows, cols, full_rows, pieces=None, off=None, layers=1, packed=None):
        self.rows, self.cols, self.full_rows, self.layers = rows, cols, full_rows, layers
        self.pieces = pieces or ((0, rows),)
        self.off = off or _plain_off(rows)
        self.packed = cols % 256 == 0 if packed is None else packed
        self.wcols = cols // 2 if self.packed else cols

    def to_words(self, a):
        return _pack(a) if self.packed else a

    def from_words(self, p):
        return _unpack(p) if self.packed else p


def _pack(a):
    def bits(v):
        u = lax.bitcast_convert_type(v, jnp.uint32)
        return u + jnp.uint32(0x7FFF) + ((u >> 16) & jnp.uint32(1))

    words = [(bits(a[:, g:g + 128]) >> 16) | (bits(a[:, g + 128:g + 256]) & jnp.uint32(0xFFFF0000))
             for g in range(0, a.shape[-1], 256)]
    return lax.bitcast_convert_type(jnp.concatenate(words, axis=-1) if len(words) > 1 else words[0], F32)


def _unpack(p):
    w = lax.bitcast_convert_type(p, jnp.uint32)
    lo = lax.bitcast_convert_type(w << 16, F32)
    hi = lax.bitcast_convert_type(w & jnp.uint32(0xFFFF0000), F32)
    return jnp.concatenate([h[:, g:g + 128] for g in range(0, p.shape[-1], 128) for h in (lo, hi)], axis=-1)


WEIGHT_SPECS = {
    "w_in_t": Spec(SHARD, D, NPAD, WIN_PIECES, _win_off),
    "wq": Spec(192, 384, 1536),
    "wkv": Spec(256, 256, 2048),
    "conv": Spec(160, 128, 1280),
    "w_proj_a": Spec(128, D, 1024),
    "w_proj_b": Spec(128, D, 1024),
    "w_proj_c": Spec(160, D, 1280),
    "w_out": Spec(128, D, 1024),
}
REP_ROWS = 72
REP_SPEC = Spec(REP_ROWS, D, REP_ROWS * NDEV, packed=False)


def _coords():
    return lax.axis_index("x"), lax.axis_index("y"), lax.axis_index("c")


def _rows(ref, start, n):
    if not isinstance(start, int):
        start = pl.multiple_of(start, 8)
    return ref.at[:, pl.ds(start, n), :]


def _col_tile(cols):
    return 256 if cols % 256 == 0 else cols


def _n_pieces(specs):
    return sum(len(sp.pieces) for sp in specs)


def pack_place(shard, sp, layer, tag, dep=None):
    gaps = ((PAD1_AT, PAD1), (PAD2_AT + PAD1, PAD2)) if sp.off is _win_off else ()
    npc = len(sp.pieces)
    deps = [] if dep is None else [dep]

    def body(s_ref, *rest):
        words_ref, full_ref, buf, zbuf, sem = rest[-5:]
        l = 0
        x, y, c = _coords()
        me = 4 * x + 2 * y + c
        words = sp.to_words(s_ref[...])
        words_ref[...] = words
        buf[...] = words
        copies = [pltpu.make_async_copy(buf.at[pl.ds(s, n), :],
                                        full_ref.at[l, pl.ds(pl.multiple_of(sp.off(me, s), 8), n), :], sem.at[i])
                  for i, (s, n) in enumerate(sp.pieces)]
        if gaps:
            zbuf[...] = jnp.zeros_like(zbuf)
            copies += [pltpu.make_async_copy(zbuf.at[pl.ds(0, n), :], full_ref.at[l, pl.ds(at, n), :], sem.at[npc + i])
                       for i, (at, n) in enumerate(gaps)]
        for cp in copies:
            cp.start()
        for cp in copies:
            cp.wait()

    return pl.pallas_call(
        body, grid=(1,), in_specs=[pl.BlockSpec((None, sp.rows, sp.cols), lambda i: (layer, 0, 0))] + [ANY] * len(deps),
        out_specs=[pl.BlockSpec((None, sp.rows, sp.wcols), lambda i: (0, 0, 0)), ANY],
        out_shape=[jax.ShapeDtypeStruct((sp.layers, sp.rows, sp.wcols), F32),
                   jax.ShapeDtypeStruct((sp.layers, sp.full_rows, sp.wcols), F32)],
        scratch_shapes=[pltpu.VMEM((sp.rows, sp.wcols), F32), pltpu.VMEM((PAD2 if gaps else 8, sp.wcols), F32),
                        pltpu.SemaphoreType.DMA((npc + len(gaps),))],
        name=f"pack_place_{tag}", compiler_params=_params(("arbitrary",)))(shard, *deps)


def _gather_copies(srcs, bufs, specs, ssem, rsem, landing):
    x, y, c = _coords()
    me = 4 * x + 2 * y + c
    targets = [(x, y, 1 - c)] + [(x ^ fx, y ^ fy, c) for fx, fy in FLIPS]
    copies = []
    p = 0
    for src, buf, sp in zip(srcs, bufs, specs):
        for s, n in sp.pieces:
            for t, (tx, ty, tc) in enumerate(targets):
                owner = 4 * tx + 2 * ty + tc if landing else me
                copies.append(pltpu.make_async_remote_copy(_rows(src, s, n), _rows(buf, sp.off(owner, s), n),
                                                           ssem.at[4 * p + t], rsem.at[4 * p + t],
                                                           device_id=(tx, ty, tc), device_id_type=MESH))
            p += 1
    return copies


def gather_send(words, fulls, specs, tag):
    ns, npc = len(specs), _n_pieces(specs)

    def body(*refs):
        srcs, bufs, sems = refs[:ns], refs[2 * ns:3 * ns], refs[3 * ns:]
        for cp in _gather_copies(srcs, bufs, specs, *sems, False):
            cp.start()
        for cp in _gather_copies(srcs, bufs, specs, *sems, False):
            cp.wait_send()
        for cp in _gather_copies(srcs, bufs, specs, *sems, True):
            cp.wait_recv()

    return pl.pallas_call(
        body, in_specs=[ANY] * (2 * ns), out_specs=[ANY] * ns,
        out_shape=[jax.ShapeDtypeStruct(f.shape, f.dtype) for f in fulls],
        input_output_aliases={ns + i: i for i in range(ns)},
        scratch_shapes=[pltpu.SemaphoreType.DMA((4 * npc,)), pltpu.SemaphoreType.DMA((4 * npc,))],
        name=f"gather_send_{tag}", compiler_params=pltpu.CompilerParams(has_side_effects=True))(*words, *fulls)


def _in_hbm(arrays):
    return [pltpu.with_memory_space_constraint(a, pltpu.HBM) for a in arrays]


def gather_start(words, fulls, specs, dep, tag):
    ns, npc = len(specs), _n_pieces(specs)
    deps = [] if dep is None else [dep]

    def body(*refs):
        ssem, rsem = refs[2 * ns + len(deps):2 * ns + len(deps) + 2]
        for cp in _gather_copies(refs[:ns], refs[ns:2 * ns], specs, ssem, rsem, False):
            cp.start()
        refs[-1][...] = jnp.zeros_like(refs[-1])

    outs = pl.pallas_call(
        body, in_specs=[HBM] * (2 * ns) + [ANY] * len(deps),
        out_specs=[SEM, SEM] + [HBM] * (2 * ns) + [pl.BlockSpec(memory_space=pltpu.VMEM)],
        out_shape=[pltpu.SemaphoreType.DMA((4 * npc,)), pltpu.SemaphoreType.DMA((4 * npc,))]
        + [pltpu.HBM(a.shape, a.dtype) for a in list(words) + list(fulls)] + [jax.ShapeDtypeStruct((8, 128), F32)],
        input_output_aliases={i: 2 + i for i in range(2 * ns)},
        name=f"gather_start_{tag}", compiler_params=pltpu.CompilerParams(has_side_effects=EFFECT))(
            *_in_hbm(list(words) + list(fulls)), *deps)
    return outs[0], outs[1], outs[2:2 + ns], outs[2 + ns:2 + 2 * ns], outs[-1]


def gather_wait(ssem, rsem, words, fulls, specs, after, tag):
    ns = len(specs)

    def body(*refs):
        srcs, bufs, ssem, rsem = refs[:ns], refs[ns:2 * ns], refs[2 * ns], refs[2 * ns + 1]
        for cp in _gather_copies(srcs, bufs, specs, ssem, rsem, False):
            cp.wait_send()
        for cp in _gather_copies(srcs, bufs, specs, ssem, rsem, True):
            cp.wait_recv()

    outs = pl.pallas_call(
        body, in_specs=[HBM] * (2 * ns) + [SEM, SEM, ANY], out_specs=[HBM] * (2 * ns),
        out_shape=[pltpu.HBM(a.shape, a.dtype) for a in list(words) + list(fulls)],
        input_output_aliases={i: i for i in range(2 * ns)},
        name=f"gather_wait_{tag}", compiler_params=pltpu.CompilerParams(has_side_effects=EFFECT))(
            *words, *fulls, ssem, rsem, after)
    return outs[ns:]


def gather_forward(fulls, specs, tag):
    ns, npc = len(specs), _n_pieces(specs)

    def body(*refs):
        bufs = refs[ns:2 * ns]
        ssem, rsem = refs[2 * ns:]
        x, y, c = _coords()
        sibling = (x, y, 1 - c)
        waits = []
        p = 0
        for buf, sp in zip(bufs, specs):
            for s, n in sp.pieces:
                for t, (fx, fy) in enumerate(FLIPS):
                    chip = 4 * (x ^ fx) + 2 * (y ^ fy)
                    here = _rows(buf, sp.off(chip + c, s), n)
                    send = pltpu.make_async_remote_copy(here, here, ssem.at[t, p], rsem.at[t, p],
                                                        device_id=sibling, device_id_type=MESH)
                    send.start()
                    waits.append(send.wait_send)
                    there = _rows(buf, sp.off(chip + 1 - c, s), n)
                    waits.append(pltpu.make_async_remote_copy(here, there, ssem.at[t, p], rsem.at[t, p],
                                                              device_id=sibling, device_id_type=MESH).wait_recv)
                p += 1
        for w in waits:
            w()

    return pl.pallas_call(
        body, in_specs=[ANY] * ns, out_specs=[ANY] * ns,
        out_shape=[jax.ShapeDtypeStruct(f.shape, f.dtype) for f in fulls],
        input_output_aliases={i: i for i in range(ns)},
        scratch_shapes=[pltpu.SemaphoreType.DMA((3, npc)), pltpu.SemaphoreType.DMA((3, npc))],
        name=f"gather_forward_{tag}", compiler_params=pltpu.CompilerParams(has_side_effects=True))(*fulls)


def all_gather(shards, layer, specs, names, tag):
    placed = [pack_place(s, sp, layer, f"{tag}_{n}") for s, sp, n in zip(shards, specs, names)]
    fulls = gather_send([p[0] for p in placed], [p[1] for p in placed], specs, tag)
    return gather_forward(fulls, specs, tag)


def _pair_copies(srcs, theirs, specs, ssem, rsem):
    x, y, c = _coords()
    copies = []
    p = 0
    for src, their, sp in zip(srcs, theirs, specs):
        for s, n in sp.pieces:
            for j in range(4):
                copies.append(pltpu.make_async_remote_copy(_rows(src, sp.off(2 * j + 1 - c, s), n), _rows(their.at[j], s, n),
                                                           ssem.at[4 * p + j], rsem.at[4 * p + j],
                                                           device_id=(x, y, 1 - c), device_id_type=MESH))
            p += 1
    return copies


def _pair_shapes(specs):
    return [(4, sp.layers, sp.rows, sp.cols) for sp in specs]


def reduce_pair(grads, specs, tag, dep=None):
    ns, npc = len(specs), _n_pieces(specs)
    deps = [] if dep is None else [dep]

    def body(*refs):
        copies = _pair_copies(refs[:ns], refs[ns + len(deps):2 * ns + len(deps)], specs, *refs[2 * ns + len(deps):])
        for cp in copies:
            cp.start()
        for cp in copies:
            cp.wait()

    return pl.pallas_call(
        body, in_specs=[ANY] * (ns + len(deps)), out_specs=[ANY] * ns,
        out_shape=[jax.ShapeDtypeStruct(s, F32) for s in _pair_shapes(specs)],
        scratch_shapes=[pltpu.SemaphoreType.DMA((4 * npc,)), pltpu.SemaphoreType.DMA((4 * npc,))],
        name=f"reduce_pair_{tag}", compiler_params=pltpu.CompilerParams(has_side_effects=True))(*grads, *deps)


def pair_start(grads, specs, dep, tag):
    ns, npc = len(specs), _n_pieces(specs)
    slots = [lax.empty(s, F32) for s in _pair_shapes(specs)]
    deps = [] if dep is None else [dep]

    def body(*refs):
        ssem, rsem = refs[2 * ns + len(deps):2 * ns + len(deps) + 2]
        for cp in _pair_copies(refs[:ns], refs[ns:2 * ns], specs, ssem, rsem):
            cp.start()
        refs[-1][...] = jnp.zeros_like(refs[-1])

    outs = pl.pallas_call(
        body, in_specs=[HBM] * (2 * ns) + [ANY] * len(deps),
        out_specs=[SEM, SEM] + [HBM] * (2 * ns) + [pl.BlockSpec(memory_space=pltpu.VMEM)],
        out_shape=[pltpu.SemaphoreType.DMA((4 * npc,)), pltpu.SemaphoreType.DMA((4 * npc,))]
        + [pltpu.HBM(a.shape, a.dtype) for a in list(grads) + slots] + [jax.ShapeDtypeStruct((8, 128), F32)],
        input_output_aliases={i: 2 + i for i in range(2 * ns)},
        name=f"pair_start_{tag}", compiler_params=pltpu.CompilerParams(has_side_effects=EFFECT))(
            *_in_hbm(list(grads) + slots), *deps)
    return outs[0], outs[1], outs[2:2 + ns], outs[2 + ns:2 + 2 * ns], outs[-1]


def pair_wait(ssem, rsem, grads, slots, specs, after, tag):
    ns = len(specs)

    def body(*refs):
        for cp in _pair_copies(refs[:ns], refs[ns:2 * ns], specs, refs[2 * ns], refs[2 * ns + 1]):
            cp.wait_send()
            cp.wait_recv()

    outs = pl.pallas_call(
        body, in_specs=[HBM] * (2 * ns) + [SEM, SEM, ANY], out_specs=[HBM] * (2 * ns),
        out_shape=[pltpu.HBM(a.shape, a.dtype) for a in list(grads) + list(slots)],
        input_output_aliases={i: i for i in range(2 * ns)},
        name=f"pair_wait_{tag}", compiler_params=pltpu.CompilerParams(has_side_effects=EFFECT))(
            *grads, *slots, ssem, rsem, after)
    return outs[:ns], outs[ns:]


def pair_sum(g, r1, sp, tag):
    npc = len(sp.pieces)
    fetch_all = 4 * sp.rows * sp.cols * 4 <= (8 << 20)

    def body(g_ref, r_ref, own_ref, words_ref, gbuf, sem):
        l, j = pl.program_id(0), pl.program_id(1)
        x, y, c = _coords()

        def copies(chip, slot):
            return [pltpu.make_async_copy(g_ref.at[l, pl.ds(pl.multiple_of(sp.off(2 * chip + c, s), 8), n), :],
                                          gbuf.at[slot, pl.ds(s, n), :], sem.at[slot, i])
                    for i, (s, n) in enumerate(sp.pieces)]

        def fetch(chip, slot):
            for cp in copies(chip, slot):
                cp.start()

        def arrived(chip, slot):
            for cp in copies(chip, slot):
                cp.wait()

        if fetch_all:
            @pl.when(j == 0)
            def _():
                for chip in range(4):
                    fetch(chip, chip)
                for chip in range(4):
                    arrived(chip, chip)

            mine = gbuf[j]
        else:
            @pl.when(j == 0)
            def _():
                fetch(0, 0)

            @pl.when(j < 3)
            def _():
                fetch(j + 1, (j + 1) % 2)

            arrived(j, j % 2)
            mine = gbuf[j % 2]
        p = mine + r_ref[...]
        words_ref[...] = sp.to_words(p)

        @pl.when(j == 2 * x + y)
        def _():
            own_ref[...] = p

    return pl.pallas_call(
        body, grid=(sp.layers, 4),
        in_specs=[ANY, pl.BlockSpec((None, None, sp.rows, sp.cols), lambda l, j: (j, l, 0, 0))],
        out_specs=[pl.BlockSpec((None, sp.rows, sp.cols), lambda l, j: (l, 0, 0)),
                   pl.BlockSpec((None, None, sp.rows, sp.wcols), lambda l, j: (j, l, 0, 0))],
        out_shape=[jax.ShapeDtypeStruct((sp.layers, sp.rows, sp.cols), F32),
                   jax.ShapeDtypeStruct((4, sp.layers, sp.rows, sp.wcols), F32)],
        scratch_shapes=[pltpu.VMEM((4 if fetch_all else 2, sp.rows, sp.cols), F32), pltpu.SemaphoreType.DMA((4, npc))],
        name=f"pair_sum_{tag}", compiler_params=_params(("arbitrary", "arbitrary")))(g, r1)


def _chip_copies(srcs, dsts, ssem, rsem):
    x, y, c = _coords()
    copies = []
    for i, (src, dst) in enumerate(zip(srcs, dsts)):
        for t, (fx, fy) in enumerate(FLIPS):
            tx, ty = x ^ fx, y ^ fy
            copies.append(pltpu.make_async_remote_copy(src.at[2 * tx + ty], dst.at[t], ssem.at[3 * i + t], rsem.at[3 * i + t],
                                                       device_id=(tx, ty, c), device_id_type=MESH))
    return copies


def _slot_shapes(words):
    return [(3,) + w.shape[1:] for w in words]


def reduce_chips(words, specs, tag):
    ns = len(specs)

    def body(*refs):
        copies = _chip_copies(refs[:ns], refs[ns:2 * ns], *refs[2 * ns:])
        for cp in copies:
            cp.start()
        for cp in copies:
            cp.wait()

    return pl.pallas_call(
        body, in_specs=[ANY] * ns, out_specs=[ANY] * ns,
        out_shape=[jax.ShapeDtypeStruct(s, F32) for s in _slot_shapes(words)],
        scratch_shapes=[pltpu.SemaphoreType.DMA((3 * ns,)), pltpu.SemaphoreType.DMA((3 * ns,))],
        name=f"reduce_chips_{tag}", compiler_params=pltpu.CompilerParams(has_side_effects=True))(*words)


def chips_start(words, specs, tag):
    ns = len(specs)
    slots = [lax.empty(s, F32) for s in _slot_shapes(words)]

    def body(*refs):
        ssem, rsem = refs[2 * ns:2 * ns + 2]
        for cp in _chip_copies(refs[:ns], refs[ns:2 * ns], ssem, rsem):
            cp.start()
        refs[-1][...] = jnp.zeros_like(refs[-1])

    outs = pl.pallas_call(
        body, in_specs=[HBM] * (2 * ns),
        out_specs=[SEM, SEM] + [HBM] * (2 * ns) + [pl.BlockSpec(memory_space=pltpu.VMEM)],
        out_shape=[pltpu.SemaphoreType.DMA((3 * ns,)), pltpu.SemaphoreType.DMA((3 * ns,))]
        + [pltpu.HBM(a.shape, a.dtype) for a in list(words) + slots] + [jax.ShapeDtypeStruct((8, 128), F32)],
        input_output_aliases={i: 2 + i for i in range(2 * ns)},
        name=f"chips_start_{tag}", compiler_params=pltpu.CompilerParams(has_side_effects=EFFECT))(
            *_in_hbm(list(words) + slots))
    return outs[0], outs[1], outs[2:2 + ns], outs[2 + ns:2 + 2 * ns], outs[-1]


def chips_wait(ssem, rsem, words, slots, specs, after, tag):
    ns = len(specs)

    def body(*refs):
        for cp in _chip_copies(refs[:ns], refs[ns:2 * ns], refs[2 * ns], refs[2 * ns + 1]):
            cp.wait_send()
            cp.wait_recv()

    outs = pl.pallas_call(
        body, in_specs=[HBM] * (2 * ns) + [SEM, SEM, ANY], out_specs=[HBM] * (2 * ns),
        out_shape=[pltpu.HBM(a.shape, a.dtype) for a in list(words) + list(slots)],
        input_output_aliases={i: i for i in range(2 * ns)},
        name=f"chips_wait_{tag}", compiler_params=pltpu.CompilerParams(has_side_effects=EFFECT))(
            *words, *slots, ssem, rsem, after)
    return outs[ns:]


def sum_chips(own, r2, sp, tag):
    def body(own_ref, r_ref, o_ref):
        o_ref[...] = ((own_ref[...] + sp.from_words(r_ref[0])) + sp.from_words(r_ref[1])) + sp.from_words(r_ref[2])

    blk = pl.BlockSpec((None, sp.rows, sp.cols), lambda l: (l, 0, 0))
    return pl.pallas_call(
        body, grid=(sp.layers,), in_specs=[blk, pl.BlockSpec((3, None, sp.rows, sp.wcols), lambda l: (0, l, 0, 0))],
        out_specs=blk, out_shape=jax.ShapeDtypeStruct((sp.layers, sp.rows, sp.cols), F32),
        name=f"sum_chips_{tag}", compiler_params=_params(("arbitrary",)))(own, r2)


def reduce_scatter_start(grads, specs, names, dep, tag):
    theirs = reduce_pair(grads, specs, tag, dep)
    sums = [pair_sum(g, r1, sp, f"{tag}_{n}") for g, r1, sp, n in zip(grads, theirs, specs, names)]
    ssem, rsem, words, slots, token = chips_start([s[1] for s in sums], specs, tag)
    return (ssem, rsem, words, slots, [s[0] for s in sums]), token


def reduce_scatter_finish(state, after, specs, tag):
    ssem, rsem, words, slots, own = state
    return list(zip(own, chips_wait(ssem, rsem, words, slots, specs, after, tag)))


def reduce_scatter(grads, specs, names, tag, dep=None):
    theirs = reduce_pair(grads, specs, tag, dep)
    sums = [pair_sum(g, r1, sp, f"{tag}_{n}") for g, r1, sp, n in zip(grads, theirs, specs, names)]
    return list(zip([s[0] for s in sums], reduce_chips([s[1] for s in sums], specs, tag)))


def _adamw_math(w, g, m, v):
    c1 = 1.0 - ADAM_B1 ** ADAM_STEP
    c2 = 1.0 - ADAM_B2 ** ADAM_STEP
    m2 = ADAM_B1 * m + (1.0 - ADAM_B1) * g
    v2 = ADAM_B2 * v + (1.0 - ADAM_B2) * (g * g)
    return -ADAM_LR * ((m2 / c1) / (jnp.sqrt(v2 / c2) + ADAM_EPS) + ADAM_WD * w), m2, v2


def adamw(w, g, m, v, name):
    shape = w.shape
    cols = shape[-1]
    rows = math.prod(shape[:-1])
    tr = rows
    while tr * cols * 4 > (1 << 20) and tr % 16 == 0:
        tr //= 2

    def body(w_ref, g_ref, m_ref, v_ref, d_ref, nm_ref, nv_ref):
        d_ref[...], nm_ref[...], nv_ref[...] = _adamw_math(w_ref[...], g_ref[...], m_ref[...], v_ref[...])

    blk = pl.BlockSpec((tr, cols), lambda i: (i, 0))
    outs = pl.pallas_call(
        body, grid=(rows // tr,), in_specs=[blk] * 4, out_specs=[blk] * 3,
        out_shape=[jax.ShapeDtypeStruct((rows, cols), F32)] * 3,
        name=f"adamw_{name}", compiler_params=_params(("arbitrary",)))(
            *[a.reshape(rows, cols) for a in (w, g, m, v)])
    return [o.reshape(shape) for o in outs]


def adamw_layer(w, sums, m, v, sp, l, prev, dep, name):
    _, rows, cols = w.shape
    tc = _col_tile(cols)
    twc = tc // 2 if sp.packed else tc
    extra = ([] if prev is None else list(prev)) + ([] if dep is None else [dep])

    def body(w_ref, own_ref, r_ref, m_ref, v_ref, *rest):
        g_ref, d_ref, nm_ref, nv_ref = rest[-4:]
        g = ((own_ref[...] + sp.from_words(r_ref[0])) + sp.from_words(r_ref[1])) + sp.from_words(r_ref[2])
        g_ref[...] = g
        d_ref[...], nm_ref[...], nv_ref[...] = _adamw_math(w_ref[...], g, m_ref[...], v_ref[...])

    blk = pl.BlockSpec((None, rows, tc), lambda n: (l, 0, n))
    return pl.pallas_call(
        body, grid=(cols // tc,),
        in_specs=[blk, pl.BlockSpec((None, rows, tc), lambda n: (0, 0, n)),
                  pl.BlockSpec((3, None, rows, twc), lambda n: (0, 0, 0, n)), blk, blk] + [ANY] * len(extra),
        out_specs=[blk] * 4, out_shape=[jax.ShapeDtypeStruct(w.shape, F32)] * 4,
        input_output_aliases={} if prev is None else {5 + i: i for i in range(4)},
        name=f"adamw_{name}_l{l}", compiler_params=_params(("arbitrary",)))(w, sums[0], sums[1], m, v, *extra)


WEIGHTS = ("pre_norm_g", "w_in", "gm_ln_g", "gm_ln_b", "gm_ws", "gm_bs", "mla_q_norm_g", "mla_w_uq", "mla_kv_norm_g",
           "mla_w_ukv", "lru_conv_w", "lru_conv_b", "lru_w_a", "lru_b_a", "lru_w_x", "lru_b_x", "lru_lambda",
           "w_proj_a", "w_proj_b", "w_proj_c", "w_out", "post_norm_g")
SHARDED = ("w_in", "mla_w_uq", "mla_w_ukv", "lru_conv_w", "w_proj_a", "w_proj_b", "w_proj_c", "w_out")
REPLICATED = tuple(n for n in WEIGHTS if n not in SHARDED)


def _step(x, target, wts, ms, vs):
    t12 = lambda a: jnp.swapaxes(a, 1, 2)
    names = list(WEIGHT_SPECS)
    specs = [WEIGHT_SPECS[n] for n in names]
    tabs = _rope_tables()
    own = {"w_in_t": t12(wts["w_in"]), "wq": t12(wts["mla_w_uq"]), "wkv": t12(wts["mla_w_ukv"]),
           "conv": jnp.pad(t12(wts["lru_conv_w"]), ((0, 0), (0, 0), (0, 124))),
           "w_proj_a": wts["w_proj_a"], "w_proj_b": wts["w_proj_b"], "w_proj_c": wts["w_proj_c"], "w_out": wts["w_out"]}
    first, rest = ["w_in_t"], [n for n in names if n != "w_in_t"]
    sfirst, srest = [WEIGHT_SPECS[n] for n in first], [WEIGHT_SPECS[n] for n in rest]

    w = {n: wts[n] for n in REPLICATED}
    w["kv_g384"] = jnp.concatenate([wts["mla_kv_norm_g"], jnp.ones((L, 128), F32)], axis=1)

    def layer_weights(ns, words):
        gw = dict(zip(ns, words))
        gw["wq"] = gw["wq"].reshape(HEADS, 192, 384)
        gw["wkv"] = gw["wkv"].reshape(HEADS, 256, 128)
        gw["conv"] = gw["conv"][0, :, :4].T
        return gw

    place = lambda l, dep: {n: pack_place(own[n], WEIGHT_SPECS[n], l, f"w{l}_{n}", dep) for n in names}
    placed = [place(0, None)]
    words_of = lambda l, ns: [placed[l][n][0] for n in ns]
    bufs_of = lambda l, ns: [placed[l][n][1] for n in ns]
    later = {}

    ssem_a, rsem_a, wthru_a, fthru_a, token_a = gather_start(words_of(0, first), bufs_of(0, first), sfirst, None, "w0a")
    placed.append(place(1, token_a))
    win0 = gather_forward(gather_wait(ssem_a, rsem_a, wthru_a, fthru_a, sfirst, placed[1]["w_in_t"][0], "w0a"), sfirst, "w0a")
    ssem_b, rsem_b, wthru_b, fthru_b, token_b = gather_start(words_of(0, rest), bufs_of(0, rest), srest, win0[0], "w0b")

    def fwd0_mid(ya):
        rest0 = gather_forward(gather_wait(ssem_b, rsem_b, wthru_b, fthru_b, srest, ya, "w0b"), srest, "w0b")
        later["w1"] = gather_start(words_of(1, names), bufs_of(1, names), specs, rest0[0], "w1")
        later["gw0"] = layer_weights(first + rest, list(win0) + list(rest0))
        return later["gw0"], later["w1"][4]

    x1, saved0 = _layer_fwd(x, 0, w, {"w_in_t": win0[0]}, tabs, dep=token_b, mid=fwd0_mid)
    ssem1, rsem1, wthru1, fthru1, _ = later["w1"]
    words1 = gather_forward(gather_wait(ssem1, rsem1, wthru1, fthru1, specs, x1, "w1"), specs, "w1")
    gw0, gw1 = later["gw0"], layer_weights(names, words1)
    x2, saved1 = _layer_fwd(x1, 1, w, gw1, tabs)
    loss, dx2 = loss_head(x2, target)

    def bwd1_mid(gg, last):
        later["p1b"] = pair_start([gg[n] for n in rest], srest, last, "g1b")
        return later["p1b"][4]

    dx1, gg1, g1 = _layer_bwd(dx2, 1, w, gw1, tabs, saved1, mid=bwd1_mid)
    grads1b, theirs1b = pair_wait(*later["p1b"][:4], srest, dx1, "g1b")
    p1a = pair_start([gg1["w_in_t"]], sfirst, theirs1b[0], "g1a")

    def bwd0_early(last):
        grads1a, theirs1a = pair_wait(*p1a[:4], sfirst, last, "g1a")
        mine = dict(zip(first + rest, list(grads1a) + list(grads1b)))
        theirs = dict(zip(first + rest, list(theirs1a) + list(theirs1b)))
        sums = [pair_sum(mine[n], theirs[n], WEIGHT_SPECS[n], f"g1_{n}") for n in names]
        ssem, rsem, words, slots, token = chips_start([s[1] for s in sums], specs, "g1")
        later["g1"] = (ssem, rsem, words, slots, [s[0] for s in sums])
        return token

    def bwd0_mid(gg, last):
        later["g0b"], token = reduce_scatter_start([gg[n] for n in rest], srest, rest, last, "g0b")
        return token

    def bwd0_late(g_win):
        later["p0a"] = pair_start([g_win], sfirst, None, "g0a")
        return later["p0a"][4]

    dx0, gg0, g0 = _layer_bwd(dx1, 0, w, gw0, tabs, saved0, dep=p1a[4], early=bwd0_early, mid=bwd0_mid, late=bwd0_late)
    s1 = dict(zip(names, reduce_scatter_finish(later["g1"], dx0, specs, "g1")))
    s0 = dict(zip(rest, reduce_scatter_finish(later["g0b"], dx0, srest, "g0b")))

    grads0a, theirs0a = pair_wait(*later["p0a"][:4], sfirst, dx0, "g0a")
    own0a, words0a = pair_sum(grads0a[0], theirs0a[0], sfirst[0], "g0a_w_in_t")
    ssem_g, rsem_g, wthru_g, slots_g, token_g = chips_start([words0a], sfirst, "g0a")

    keys = {"w_in": "w_in_t", "mla_w_uq": "wq", "mla_w_ukv": "wkv",
            "w_proj_a": "w_proj_a", "w_proj_b": "w_proj_b", "w_proj_c": "w_proj_c", "w_out": "w_out"}
    transposed = ("w_in", "mla_w_uq", "mla_w_ukv")
    state_of = lambda n: [own[keys[n]], t12(ms[n]), t12(vs[n])] if n in transposed else [wts[n], ms[n], vs[n]]

    def update(n, l, sums, prev, dep):
        wl, ml, vl = state_of(n)
        return adamw_layer(wl, sums[keys[n]], ml, vl, WEIGHT_SPECS[keys[n]], l, prev, dep, n)

    upd = {n: update(n, 1, s1, None, token_g) for n in keys}
    for n in keys:
        if n != "w_in":
            upd[n] = update(n, 0, s0, upd[n], None)
    rep_flat = jnp.concatenate([jnp.stack([g0[n], g1[n]]).reshape(-1) for n in REPLICATED])
    rep_flat = jnp.pad(rep_flat, (0, REP_ROWS * NDEV * D - rep_flat.shape[0])).reshape(1, REP_ROWS * NDEV, D)
    rep_parts = reduce_scatter([rep_flat], [REP_SPEC], ["rep"], "rep", upd["w_out"][0])[0]
    rep_sum = sum_chips(*rep_parts, REP_SPEC, "rep")
    rep_full = all_gather([rep_sum], 0, [REP_SPEC], ["rep"], "rep")[0].reshape(-1)

    out = {}
    conv_sp = WEIGHT_SPECS["conv"]
    g_conv = t12(jnp.concatenate([sum_chips(*s0["conv"], conv_sp, "conv0"), sum_chips(*s1["conv"], conv_sp, "conv1")])[:, :, :4])
    out["lru_conv_w"] = [g_conv] + adamw(wts["lru_conv_w"], g_conv, ms["lru_conv_w"], vs["lru_conv_w"], "lru_conv_w")
    at = 0
    for n in REPLICATED:
        size = math.prod(wts[n].shape)
        g = rep_full[at:at + size].reshape(wts[n].shape)
        out[n] = [g] + adamw(wts[n], g, ms[n], vs[n], n)
        at += size

    landed = chips_wait(ssem_g, rsem_g, wthru_g, slots_g, sfirst, out[REPLICATED[-1]][1], "g0a")
    s0["w_in_t"] = (own0a, landed[0])
    upd["w_in"] = update("w_in", 0, s0, upd["w_in"], None)
    out.update({n: [t12(r) for r in upd[n]] if n in transposed else upd[n] for n in keys})

    loss = lax.psum(loss, ("x", "y", "c"))
    return (loss, dx0[None], *[out[n][k] for k in range(4) for n in WEIGHTS])


def kernel(x, pre_norm_g, w_in, gm_ln_g, gm_ln_b, gm_ws, gm_bs, mla_q_norm_g, mla_w_uq, mla_kv_norm_g, mla_w_ukv, lru_conv_w, lru_conv_b, lru_w_a, lru_b_a, lru_w_x, lru_b_x, lru_lambda, w_proj_a, w_proj_b, w_proj_c, w_out, post_norm_g, loss_target, m_pre_norm_g, m_w_in, m_gm_ln_g, m_gm_ln_b, m_gm_ws, m_gm_bs, m_mla_q_norm_g, m_mla_w_uq, m_mla_kv_norm_g, m_mla_w_ukv, m_lru_conv_w, m_lru_conv_b, m_lru_w_a, m_lru_b_a, m_lru_w_x, m_lru_b_x, m_lru_lambda, m_w_proj_a, m_w_proj_b, m_w_proj_c, m_w_out, m_post_norm_g, v_pre_norm_g, v_w_in, v_gm_ln_g, v_gm_ln_b, v_gm_ws, v_gm_bs, v_mla_q_norm_g, v_mla_w_uq, v_mla_kv_norm_g, v_mla_w_ukv, v_lru_conv_w, v_lru_conv_b, v_lru_w_a, v_lru_b_a, v_lru_w_x, v_lru_b_x, v_lru_lambda, v_w_proj_a, v_w_proj_b, v_w_proj_c, v_w_out, v_post_norm_g):
    wts = dict(zip(WEIGHTS, (pre_norm_g, w_in, gm_ln_g, gm_ln_b, gm_ws, gm_bs, mla_q_norm_g, mla_w_uq, mla_kv_norm_g,
                             mla_w_ukv, lru_conv_w, lru_conv_b, lru_w_a, lru_b_a, lru_w_x, lru_b_x, lru_lambda,
                             w_proj_a, w_proj_b, w_proj_c, w_out, post_norm_g)))
    ms = dict(zip(WEIGHTS, (m_pre_norm_g, m_w_in, m_gm_ln_g, m_gm_ln_b, m_gm_ws, m_gm_bs, m_mla_q_norm_g, m_mla_w_uq,
                            m_mla_kv_norm_g, m_mla_w_ukv, m_lru_conv_w, m_lru_conv_b, m_lru_w_a, m_lru_b_a, m_lru_w_x,
                            m_lru_b_x, m_lru_lambda, m_w_proj_a, m_w_proj_b, m_w_proj_c, m_w_out, m_post_norm_g)))
    vs = dict(zip(WEIGHTS, (v_pre_norm_g, v_w_in, v_gm_ln_g, v_gm_ln_b, v_gm_ws, v_gm_bs, v_mla_q_norm_g, v_mla_w_uq,
                            v_mla_kv_norm_g, v_mla_w_ukv, v_lru_conv_w, v_lru_conv_b, v_lru_w_a, v_lru_b_a, v_lru_w_x,
                            v_lru_b_x, v_lru_lambda, v_w_proj_a, v_w_proj_b, v_w_proj_c, v_w_out, v_post_norm_g)))
    return _step(x[0], loss_target[0], wts, ms, vs)
```

```python
import functools
import math

import jax
import jax.numpy as jnp
from jax import lax
from jax.experimental import pallas as pl
from jax.experimental.pallas import tpu as pltpu

F32 = jnp.float32
BF16 = jnp.bfloat16

T = 2048
D = 1024
L = 2
NDEV = 8
EPS = 1e-6
CHUNK_SHIFT = 6
HEADS = 8
QK = 192
LRU_W = 1280
LRU_TILE = 640
N_IN = 10432
SHARD = N_IN // NDEV
OFF_U, OFF_V, OFF_ZA, OFF_CQ, OFF_CKV, OFF_ZB = 0, 1024, 2048, 3072, 3456, 3840
OFF_XC, OFF_ZC, OFF_GA, OFF_GB, OFF_GC = 5120, 6400, 7680, 8704, 9728
NPAD = 10752
PAD1_AT, PAD1 = 3776, 64
PAD2_AT, PAD2 = 4800, 256
WIN_PIECES = ((0, 888), (888, 280), (1168, 136))
VMEM_LIMIT = 60 * 1024 * 1024

ADAM_LR, ADAM_B1, ADAM_B2, ADAM_EPS, ADAM_WD, ADAM_STEP = 0.001, 0.9, 0.999, 1e-08, 0.01, 10

_NN = (((1,), (0,)), ((), ()))
_NT = (((1,), (1,)), ((), ()))
_TN = (((0,), (0,)), ((), ()))


def _dg(a, b, dims):
    return lax.dot_general(a.astype(BF16), b.astype(BF16), dims, preferred_element_type=F32)


@jax.custom_vjp
def dot_nn(a, b):
    return _dg(a, b, _NN)


def _nn_fwd(a, b):
    return _dg(a, b, _NN), (a, b)


def _nn_bwd(res, g):
    a, b = res
    return _dg(g, b, _NT).astype(a.dtype), _dg(a, g, _TN).astype(b.dtype)


dot_nn.defvjp(_nn_fwd, _nn_bwd)


@jax.custom_vjp
def dot_nt(a, b):
    return _dg(a, b, _NT)


def _nt_fwd(a, b):
    return _dg(a, b, _NT), (a, b)


def _nt_bwd(res, g):
    a, b = res
    return _dg(g, b, _NN).astype(a.dtype), _dg(g, a, _TN).astype(b.dtype)


dot_nt.defvjp(_nt_fwd, _nt_bwd)


def _params(sem=None):
    return pltpu.CompilerParams(dimension_semantics=sem, vmem_limit_bytes=VMEM_LIMIT)


def _sigmoid(x):
    return 1.0 / (1.0 + jnp.exp(-x))


def _silu(x):
    return x * _sigmoid(x)


def _rms(x, g):
    ms = jnp.mean(x * x, axis=-1, keepdims=True)
    return x * lax.rsqrt(ms + EPS) * g


def _acc(ref, val, first):
    @pl.when(first)
    def _():
        ref[...] = val

    @pl.when(jnp.logical_not(first))
    def _():
        ref[...] += val


ANY = pl.BlockSpec(memory_space=pl.ANY)


INPROJ_TN = 768


def inproj_fwd(x, g, wt, l, dep=None):
    tn = INPROJ_TN

    def body(x_ref, g_ref, w_ref, *rest):
        proj_ref, h_ref = rest[-2:]

        @pl.when(pl.program_id(0) == 0)
        def _():
            h_ref[...] = _rms(x_ref[...], g_ref[...]).astype(BF16)

        proj_ref[...] = lax.dot_general(h_ref[...], _unpack(w_ref[...]).astype(BF16), _NT, preferred_element_type=F32)

    deps = [] if dep is None else [dep]
    return pl.pallas_call(
        body, grid=(NPAD // tn,),
        in_specs=[pl.BlockSpec((T, D), lambda j: (0, 0)), pl.BlockSpec((1, D), lambda j: (0, 0)),
                  pl.BlockSpec((None, tn, D // 2), lambda j: (0, j, 0))] + [ANY] * len(deps),
        out_specs=[pl.BlockSpec((T, tn), lambda j: (0, j)), pl.BlockSpec((T, D), lambda j: (0, 0))],
        out_shape=[jax.ShapeDtypeStruct((T, NPAD), F32), jax.ShapeDtypeStruct((T, D), BF16)],
        name=f"inproj_fwd_l{l}", compiler_params=_params(("arbitrary",)))(x, g, wt, *deps)


def inproj_bwd(dproj, h, wt, l, dep=None):
    tn = INPROJ_TN
    deps = [] if dep is None else [dep]

    def body(dp_ref, h_ref, w_ref, *rest):
        dwt_ref, dh_ref = rest[-2:]
        dp = dp_ref[...]
        dwt_ref[...] = lax.dot_general(dp, h_ref[...], _TN, preferred_element_type=F32)
        contrib = lax.dot_general(dp, _unpack(w_ref[...]).astype(BF16), _NN, preferred_element_type=F32)
        _acc(dh_ref, contrib, pl.program_id(0) == 0)

    return pl.pallas_call(
        body, grid=(NPAD // tn,),
        in_specs=[pl.BlockSpec((T, tn), lambda j: (0, j)), pl.BlockSpec((T, D), lambda j: (0, 0)),
                  pl.BlockSpec((None, tn, D // 2), lambda j: (0, j, 0))] + [ANY] * len(deps),
        out_specs=[pl.BlockSpec((None, tn, D), lambda j: (0, j, 0)), pl.BlockSpec((T, D), lambda j: (0, 0))],
        out_shape=[jax.ShapeDtypeStruct((1, NPAD, D), F32), jax.ShapeDtypeStruct((T, D), F32)],
        name=f"inproj_bwd_l{l}", compiler_params=_params(("arbitrary",)))(dproj, h, wt, *deps)


def prenorm_bwd(x, g, dh, dxn, l, dep=None):
    tm = 512
    deps = [] if dep is None else [dep]

    def body(x_ref, g_ref, dh_ref, dxn_ref, *rest):
        dx_ref, dg_ref = rest[-2:]
        _, vjp = jax.vjp(_rms, x_ref[...], g_ref[...])
        dx, dg = vjp(dh_ref[...])
        dx_ref[...] = dx + dxn_ref[...]
        _acc(dg_ref, dg, pl.program_id(0) == 0)

    tok = pl.BlockSpec((tm, D), lambda i: (i, 0))
    vec = pl.BlockSpec((1, D), lambda i: (0, 0))
    return pl.pallas_call(
        body, grid=(T // tm,), in_specs=[tok, vec, tok, tok] + [ANY] * len(deps), out_specs=[tok, vec],
        out_shape=[jax.ShapeDtypeStruct((T, D), F32), jax.ShapeDtypeStruct((1, D), F32)],
        name=f"prenorm_bwd_l{l}", compiler_params=_params(("arbitrary",)))(x, g, dh, dxn, *deps)


def _gmlp_tile(u, v, z, ln_g, ln_b, ws, bs):
    mu = jnp.mean(v, axis=-1, keepdims=True)
    vc = v - mu
    var = jnp.mean(vc * vc, axis=-1, keepdims=True)
    vn = vc * lax.rsqrt(var + EPS) * ln_g + ln_b
    qi = lax.broadcasted_iota(jnp.int32, (128, 128), 0) >> CHUNK_SHIFT
    kj = lax.broadcasted_iota(jnp.int32, (128, 128), 1) >> CHUNK_SHIFT
    mask = kj <= qi
    outs = []
    for g in range(4):
        wm = jnp.where(mask, ws[g], 0.0)
        outs.append(dot_nn(wm, vn[:, 256 * g:256 * (g + 1)]) + bs[g])
    sv = jnp.concatenate(outs, axis=1)
    return u * sv * _silu(z)


GMLP_ROWS = 256


def _gmlp_specs():
    blk = lambda c: pl.BlockSpec((GMLP_ROWS, 1024), lambda n, c=c: (n, c))
    vec = pl.BlockSpec((1, 1024), lambda n: (0, 0))
    return [blk(0), blk(1), blk(2), vec, vec,
            pl.BlockSpec((4, 128, 128), lambda n: (0, 0, 0)), pl.BlockSpec((4, 128, 1), lambda n: (0, 0, 0))]


def gmlp_fwd(proj, ln_g, ln_b, ws, bs, l):
    def body(u_ref, v_ref, z_ref, g_ref, b_ref, ws_ref, bs_ref, y_ref):
        for r in range(0, GMLP_ROWS, 128):
            rows = slice(r, r + 128)
            y_ref[rows, :] = _gmlp_tile(u_ref[rows, :], v_ref[rows, :], z_ref[rows, :], g_ref[...], b_ref[...],
                                        [ws_ref[g] for g in range(4)], [bs_ref[g] for g in range(4)])

    return pl.pallas_call(
        body, grid=(T // GMLP_ROWS,), in_specs=_gmlp_specs(),
        out_specs=pl.BlockSpec((GMLP_ROWS, 1024), lambda n: (n, 0)),
        out_shape=jax.ShapeDtypeStruct((T, 1024), F32),
        name=f"gmlp_fwd_l{l}", compiler_params=_params(("arbitrary",)))(proj, proj, proj, ln_g, ln_b, ws, bs)


def gmlp_bwd(proj, ln_g, ln_b, ws, bs, dy, dproj, l):
    def body(u_ref, v_ref, z_ref, g_ref, b_ref, ws_ref, bs_ref, dy_ref, _, dseg_ref, dg_ref, db_ref, dws_ref, dbs_ref):
        for r in range(0, GMLP_ROWS, 128):
            rows = slice(r, r + 128)
            first = jnp.logical_and(pl.program_id(0) == 0, r == 0)
            _, vjp = jax.vjp(_gmlp_tile, u_ref[rows, :], v_ref[rows, :], z_ref[rows, :], g_ref[...], b_ref[...],
                             [ws_ref[g] for g in range(4)], [bs_ref[g] for g in range(4)])
            du, dv, dz, dg, db, dws, dbs = vjp(dy_ref[rows, :])
            dseg_ref[rows, 0:1024] = du.astype(BF16)
            dseg_ref[rows, 1024:2048] = dv.astype(BF16)
            dseg_ref[rows, 2048:3072] = dz.astype(BF16)
            _acc(dg_ref, dg, first)
            _acc(db_ref, db, first)
            for g in range(4):
                _acc(dws_ref.at[g], dws[g], first)
                _acc(dbs_ref.at[g], dbs[g], first)

    vec = pl.BlockSpec((1, 1024), lambda n: (0, 0))
    return pl.pallas_call(
        body, grid=(T // GMLP_ROWS,),
        in_specs=_gmlp_specs() + [pl.BlockSpec((GMLP_ROWS, 1024), lambda n: (n, 0)), ANY],
        out_specs=[pl.BlockSpec((GMLP_ROWS, 3072), lambda n: (n, OFF_U // 3072)), vec, vec,
                   pl.BlockSpec((4, 128, 128), lambda n: (0, 0, 0)), pl.BlockSpec((4, 128, 1), lambda n: (0, 0, 0))],
        out_shape=[jax.ShapeDtypeStruct((T, NPAD), BF16), jax.ShapeDtypeStruct((1, 1024), F32),
                   jax.ShapeDtypeStruct((1, 1024), F32), jax.ShapeDtypeStruct((4, 128, 128), F32),
                   jax.ShapeDtypeStruct((4, 128, 1), F32)],
        input_output_aliases={8: 0},
        name=f"gmlp_bwd_l{l}", compiler_params=_params(("arbitrary",)))(proj, proj, proj, ln_g, ln_b, ws, bs, dy, dproj)


QKV_TM = 512


def _qkv_tile(cq, ckvr, qg, kvg, wq, wkv, ctab, stab):
    tm = cq.shape[0]
    cqn = _rms(cq, qg)
    lane = lax.broadcasted_iota(jnp.int32, ckvr.shape, 1)
    iskv = lane < 256
    ms = jnp.sum(jnp.where(iskv, ckvr * ckvr, 0.0), axis=-1, keepdims=True) * (1.0 / 256)
    lm = jnp.where(iskv, ckvr * lax.rsqrt(ms + EPS) * kvg, ckvr)
    r = lax.broadcasted_iota(jnp.int32, (64, 128), 0)
    c = lax.broadcasted_iota(jnp.int32, (64, 128), 1)
    eye = jnp.where(c == r, 1.0, 0.0)
    eye_sw = jnp.where(c == ((r + 32) & 63), 1.0, 0.0)
    z64 = jnp.zeros((64, 256), F32)
    z128 = jnp.zeros((128, 128), F32)
    rk_rope = jnp.concatenate([z64, eye], axis=1)
    rk_sw = jnp.concatenate([jnp.zeros((128, 384), F32), jnp.concatenate([z64, eye_sw], axis=1)], axis=0)
    k_sw = dot_nt(lm, rk_sw) * stab
    qs, ks, vs = [], [], []
    for h in range(HEADS):
        wn, w1, w2 = wq[h]
        wk, wv = wkv[h]
        wq_h = jnp.concatenate([wn, w1, w2], axis=0)
        wq_sw = jnp.concatenate([jnp.zeros((128, 384), F32), w2, w1], axis=0)
        qs.append(dot_nt(cqn, wq_h) * ctab + dot_nt(cqn, wq_sw) * stab)
        rk_h = jnp.concatenate([jnp.concatenate([wk, z128], axis=1), rk_rope], axis=0)
        ks.append(dot_nt(lm, rk_h) * ctab + k_sw)
        vs.append(dot_nt(lm, jnp.concatenate([wv, z128], axis=1)))
    return qs, ks, vs


def _qkv_in_specs():
    tm = QKV_TM
    return [pl.BlockSpec((tm, 384), lambda i: (i, OFF_CQ // 384)), pl.BlockSpec((tm, 384), lambda i: (i, OFF_CKV // 384)),
            pl.BlockSpec((1, 384), lambda i: (0, 0)), pl.BlockSpec((1, 384), lambda i: (0, 0)),
            pl.BlockSpec((HEADS, 192, 384), lambda i: (0, 0, 0)), pl.BlockSpec((HEADS, 256, 128), lambda i: (0, 0, 0)),
            pl.BlockSpec((tm, 192), lambda i: (i, 0)), pl.BlockSpec((tm, 192), lambda i: (i, 0))]


def _qkv_weights(wq_ref, wkv_ref):
    wq = [(wq_ref[h, 0:128, :], wq_ref[h, 128:160, :], wq_ref[h, 160:192, :]) for h in range(HEADS)]
    wkv = [(_unpack(wkv_ref[h, 0:128, :]), _unpack(wkv_ref[h, 128:256, :])) for h in range(HEADS)]
    return wq, wkv


def qkv_fwd(proj, qg, kvg, wq, wkv, ctab, stab, l, dep=None):
    tm = QKV_TM
    deps = [] if dep is None else [dep]

    def body(cq_ref, ckvr_ref, qg_ref, kvg_ref, wq_ref, wkv_ref, c_ref, s_ref, *rest):
        q_ref, k_ref, v_ref = rest[-3:]
        wq_l, wkv_l = _qkv_weights(wq_ref, wkv_ref)
        qs, ks, vs = _qkv_tile(cq_ref[...], ckvr_ref[...], qg_ref[...], kvg_ref[...], wq_l, wkv_l, c_ref[...], s_ref[...])
        for h in range(HEADS):
            q_ref[h] = qs[h]
            k_ref[h] = ks[h]
            v_ref[h] = vs[h]

    return pl.pallas_call(
        body, grid=(T // tm,), in_specs=_qkv_in_specs() + [ANY] * len(deps),
        out_specs=[pl.BlockSpec((HEADS, tm, QK), lambda i: (0, i, 0)), pl.BlockSpec((HEADS, tm, QK), lambda i: (0, i, 0)),
                   pl.BlockSpec((HEADS, tm, 128), lambda i: (0, i, 0))],
        out_shape=[jax.ShapeDtypeStruct((HEADS, T, QK), F32), jax.ShapeDtypeStruct((HEADS, T, QK), F32),
                   jax.ShapeDtypeStruct((HEADS, T, 128), F32)],
        name=f"qkv_fwd_l{l}", compiler_params=_params(("arbitrary",)))(proj, proj, qg, kvg, wq, wkv, ctab, stab, *deps)


def qkv_bwd(proj, qg, kvg, wq, wkv, ctab, stab, dq, dk, dv, dproj, l):
    tm = QKV_TM

    def body(cq_ref, ckvr_ref, qg_ref, kvg_ref, wq_ref, wkv_ref, c_ref, s_ref, dq_ref, dk_ref, dv_ref, _,
             dseg_ref, dqg_ref, dkvg_ref, dwq_ref, dwkv_ref):
        first = pl.program_id(0) == 0
        wq_l, wkv_l = _qkv_weights(wq_ref, wkv_ref)
        c_tab, s_tab = c_ref[...], s_ref[...]
        fn = lambda cq, ckvr, qg_, kvg_, wq_, wkv_: _qkv_tile(cq, ckvr, qg_, kvg_, wq_, wkv_, c_tab, s_tab)
        _, vjp = jax.vjp(fn, cq_ref[...], ckvr_ref[...], qg_ref[...], kvg_ref[...], wq_l, wkv_l)
        cts = ([dq_ref[h] for h in range(HEADS)], [dk_ref[h] for h in range(HEADS)], [dv_ref[h] for h in range(HEADS)])
        dcq, dckvr, dqg, dkvg, dwq, dwkv = vjp(cts)
        dseg_ref[:, 0:384] = dcq.astype(BF16)
        dseg_ref[:, 384:768] = dckvr.astype(BF16)
        _acc(dqg_ref, dqg, first)
        _acc(dkvg_ref, dkvg, first)
        for h in range(HEADS):
            _acc(dwq_ref.at[h, 0:128, :], dwq[h][0], first)
            _acc(dwq_ref.at[h, 128:160, :], dwq[h][1], first)
            _acc(dwq_ref.at[h, 160:192, :], dwq[h][2], first)
            _acc(dwkv_ref.at[h, 0:128, :], dwkv[h][0], first)
            _acc(dwkv_ref.at[h, 128:256, :], dwkv[h][1], first)

    hq = pl.BlockSpec((HEADS, tm, QK), lambda i: (0, i, 0))
    return pl.pallas_call(
        body, grid=(T // tm,),
        in_specs=_qkv_in_specs() + [hq, hq, pl.BlockSpec((HEADS, tm, 128), lambda i: (0, i, 0)), ANY],
        out_specs=[pl.BlockSpec((tm, 768), lambda i: (i, OFF_CQ // 768)), pl.BlockSpec((1, 384), lambda i: (0, 0)),
                   pl.BlockSpec((1, 384), lambda i: (0, 0)), pl.BlockSpec((HEADS, 192, 384), lambda i: (0, 0, 0)),
                   pl.BlockSpec((HEADS, 256, 256), lambda i: (0, 0, 0))],
        out_shape=[jax.ShapeDtypeStruct((T, NPAD), BF16), jax.ShapeDtypeStruct((1, 384), F32),
                   jax.ShapeDtypeStruct((1, 384), F32), jax.ShapeDtypeStruct((HEADS, 192, 384), F32),
                   jax.ShapeDtypeStruct((HEADS, 256, 256), F32)],
        input_output_aliases={11: 0},
        name=f"qkv_bwd_l{l}", compiler_params=_params(("arbitrary",)))(
            proj, proj, qg, kvg, wq, wkv, ctab, stab, dq, dk, dv, dproj)


ATT_TQ_FWD = 256
ATT_TQ_BWD = 512


def _attn_tile(q, kv_past, k, v, zb):
    q = q * (1.0 / math.sqrt(QK))
    s = dot_nt(q, k)
    qc = lax.broadcasted_iota(jnp.int32, s.shape, 0) >> CHUNK_SHIFT
    kc = lax.broadcasted_iota(jnp.int32, s.shape, 1) >> CHUNK_SHIFT
    s = jnp.where(kc <= qc, s, -1e30)
    m = jnp.max(s, axis=-1, keepdims=True)
    if kv_past is not None:
        sp = dot_nt(q, kv_past[0])
        m = jnp.maximum(m, jnp.max(sp, axis=-1, keepdims=True))
    m = lax.stop_gradient(m)
    p = jnp.exp(s - m)
    denom = jnp.sum(p, axis=-1, keepdims=True)
    o = dot_nn(p, v)
    if kv_past is not None:
        pp = jnp.exp(sp - m)
        denom = denom + jnp.sum(pp, axis=-1, keepdims=True)
        o = o + dot_nn(pp, kv_past[1])
    return o * (1.0 / denom) * _silu(zb)


def _attn_operands(k_ref, v_ref, g, tq):
    n = tq * g
    past = (k_ref[0:n, :], v_ref[0:n, :]) if g else None
    return past, k_ref[n:n + tq, :], v_ref[n:n + tq, :]


def _attn_in_specs(tq):
    return [pl.BlockSpec((None, tq, QK), lambda h, i: (h, i, 0)), pl.BlockSpec((None, T, QK), lambda h, i: (h, 0, 0)),
            pl.BlockSpec((None, T, 128), lambda h, i: (h, 0, 0)),
            pl.BlockSpec((tq, 128), lambda h, i: (i, OFF_ZB // 128 + h))]


def attn_fwd(q, k, v, proj, l):
    tq = ATT_TQ_FWD

    def body(q_ref, k_ref, v_ref, z_ref, y_ref):
        for g in range(T // tq):
            @pl.when(pl.program_id(1) == g)
            def _(g=g):
                past, k, v = _attn_operands(k_ref, v_ref, g, tq)
                y_ref[...] = _attn_tile(q_ref[...], past, k, v, z_ref[...])

    return pl.pallas_call(
        body, grid=(HEADS, T // tq), in_specs=_attn_in_specs(tq),
        out_specs=pl.BlockSpec((tq, 128), lambda h, i: (i, h)),
        out_shape=jax.ShapeDtypeStruct((T, 1024), F32),
        name=f"attn_fwd_l{l}", compiler_params=_params(("arbitrary", "arbitrary")))(q, k, v, proj)


def attn_bwd(q, k, v, proj, dy, dproj, l):
    tq = ATT_TQ_BWD

    def body(q_ref, k_ref, v_ref, z_ref, dy_ref, _, dq_ref, dk_ref, dv_ref, dz_ref):
        @pl.when(pl.program_id(1) == 0)
        def _():
            dk_ref[...] = jnp.zeros_like(dk_ref)
            dv_ref[...] = jnp.zeros_like(dv_ref)

        for g in range(T // tq):
            @pl.when(pl.program_id(1) == g)
            def _(g=g):
                n = tq * g
                past, k, v = _attn_operands(k_ref, v_ref, g, tq)
                _, vjp = jax.vjp(_attn_tile, q_ref[...], past, k, v, z_ref[...])
                dq, dpast, dk, dv, dz = vjp(dy_ref[...])
                dq_ref[...] = dq
                dz_ref[...] = dz.astype(BF16)
                dk_ref[n:n + tq, :] += dk
                dv_ref[n:n + tq, :] += dv
                if g:
                    dk_ref[0:n, :] += dpast[0]
                    dv_ref[0:n, :] += dpast[1]

    return pl.pallas_call(
        body, grid=(HEADS, T // tq),
        in_specs=_attn_in_specs(tq) + [pl.BlockSpec((tq, 128), lambda h, i: (i, h)), ANY],
        out_specs=[pl.BlockSpec((None, tq, QK), lambda h, i: (h, i, 0)), pl.BlockSpec((None, T, QK), lambda h, i: (h, 0, 0)),
                   pl.BlockSpec((None, T, 128), lambda h, i: (h, 0, 0)),
                   pl.BlockSpec((tq, 128), lambda h, i: (i, OFF_ZB // 128 + h))],
        out_shape=[jax.ShapeDtypeStruct((HEADS, T, QK), F32), jax.ShapeDtypeStruct((HEADS, T, QK), F32),
                   jax.ShapeDtypeStruct((HEADS, T, 128), F32), jax.ShapeDtypeStruct((T, NPAD), BF16)],
        input_output_aliases={5: 3},
        name=f"attn_bwd_l{l}", compiler_params=_params(("arbitrary", "arbitrary")))(q, k, v, proj, dy, dproj)


LRU_TT = 256


def _lru_gates(xc, wa, wx, ba, bx, lam):
    r = _sigmoid(dot_nn(xc, wa) + ba)
    i = _sigmoid(dot_nn(xc, wx) + bx)
    sp = jnp.maximum(-lam, 0.0) + jnp.log1p(jnp.exp(-jnp.abs(lam)))
    log_a = -8.0 * r * sp
    a = jnp.exp(log_a)
    mult = jnp.sqrt(jnp.maximum(1.0 - jnp.exp(2.0 * log_a), 0.0))
    return a, mult * (i * xc)


def _shift_down(x, s, halo):
    n, c = x.shape
    r = pltpu.roll(x.reshape(n // 8, 8, c), s, 1)
    before = jnp.concatenate([pltpu.roll(halo, s, 0)[None], r[:-1]], axis=0)
    sub = lax.broadcasted_iota(jnp.int32, r.shape, 1)
    return jnp.where(sub >= s, r, before).reshape(n, c)


def _shift_up(x, s, halo):
    n, c = x.shape
    r = pltpu.roll(x.reshape(n // 8, 8, c), 8 - s, 1)
    after = jnp.concatenate([r[1:], pltpu.roll(halo, 8 - s, 0)[None]], axis=0)
    sub = lax.broadcasted_iota(jnp.int32, r.shape, 1)
    return jnp.where(sub < 8 - s, r, after).reshape(n, c)


def _conv(x, halo, w_ref, b):
    return (w_ref[3:4, :] * x + w_ref[2:3, :] * _shift_down(x, 1, halo) + w_ref[1:2, :] * _shift_down(x, 2, halo)
            + w_ref[0:1, :] * _shift_down(x, 3, halo) + b)


def _scan(a, b, reverse, carry):
    n, c = a.shape
    a, b = a.reshape(n // 8, 8, c), b.reshape(n // 8, 8, c)
    sub = lax.broadcasted_iota(jnp.int32, a.shape, 1)
    for d in (1, 2, 4):
        keep = sub < 8 - d if reverse else sub >= d
        shift = 8 - d if reverse else d
        a_sh = jnp.where(keep, pltpu.roll(a, shift, 1), 1.0)
        b_sh = jnp.where(keep, pltpu.roll(b, shift, 1), 0.0)
        b = a * b_sh + b
        a = a * a_sh
    a, b = a.reshape(n, c), b.reshape(n, c)
    groups = [None] * (n // 8)
    for g in (reversed(range(n // 8)) if reverse else range(n // 8)):
        h = a[8 * g:8 * g + 8] * carry + b[8 * g:8 * g + 8]
        groups[g] = h
        carry = h[0:1] if reverse else h[7:8]
    return jnp.concatenate(groups, axis=0), carry


def _lru_param_specs(l):
    ct = LRU_TILE
    vec = pl.BlockSpec((1, ct), lambda n, i: (0, n))
    mat = pl.BlockSpec((None, 8, 80, 80), lambda n, i: (l, n, 0, 0))
    return [pl.BlockSpec((4, ct), lambda n, i: (0, n)), vec, mat, mat, vec, vec, vec]


def _blocks_to_dense(w_ref, dense):
    dense[...] = jnp.zeros_like(dense)
    for b in range(8):
        dense[80 * b:80 * b + 80, 80 * b:80 * b + 80] = w_ref[b]


def _dense_to_blocks(dense, w_ref):
    for b in range(8):
        w_ref[b] = dense[80 * b:80 * b + 80, 80 * b:80 * b + 80]


def lru_fwd(proj, conv_w, conv_b, wa, wx, ba, bx, lam, l):
    tt, ct = LRU_TT, LRU_TILE

    def body(x_ref, z_ref, cw_ref, cb_ref, wa_ref, wx_ref, ba_ref, bx_ref, lam_ref, h_ref, y_ref, halo, hcar, wa, wx):
        @pl.when(pl.program_id(1) == 0)
        def _():
            halo[...] = jnp.zeros_like(halo)
            hcar[...] = jnp.zeros_like(hcar)
            _blocks_to_dense(wa_ref, wa)
            _blocks_to_dense(wx_ref, wx)

        x = x_ref[...]
        xc = _conv(x, halo[...], cw_ref, cb_ref[...])
        halo[...] = x[tt - 8:tt]
        a, b = _lru_gates(xc, wa[...], wx[...], ba_ref[...], bx_ref[...], lam_ref[...])
        h, hcar[...] = _scan(a, b, False, hcar[...])
        h_ref[...] = h
        y_ref[...] = h * _silu(z_ref[...])

    seq = pl.BlockSpec((tt, ct), lambda n, i: (i, n))
    return pl.pallas_call(
        body, grid=(LRU_W // ct, T // tt),
        in_specs=[pl.BlockSpec((tt, ct), lambda n, i: (i, OFF_XC // ct + n)),
                  pl.BlockSpec((tt, ct), lambda n, i: (i, OFF_ZC // ct + n))] + _lru_param_specs(l),
        out_specs=[seq, seq],
        out_shape=[jax.ShapeDtypeStruct((T, LRU_W), F32), jax.ShapeDtypeStruct((T, LRU_W), F32)],
        scratch_shapes=[pltpu.VMEM((8, ct), F32), pltpu.VMEM((1, ct), F32), pltpu.VMEM((ct, ct), F32),
                        pltpu.VMEM((ct, ct), F32)],
        name=f"lru_fwd_l{l}", compiler_params=_params(("arbitrary", "arbitrary")))(
            proj, proj, conv_w, conv_b, wa, wx, ba, bx, lam)


def lru_bwd(proj, hseq, dy, conv_w, conv_b, wa, wx, ba, bx, lam, dproj, l):
    tt, ct = LRU_TT, LRU_TILE
    nt = T // tt
    rev = lambda i: nt - 1 - i
    prev8 = lambda i: jnp.maximum(rev(i) * (tt // 8) - 1, 0)

    def body(x_ref, xh_ref, z_ref, h_ref, hh_ref, dy_ref, cw_ref, cb_ref, wa_ref, wx_ref, ba_ref, bx_ref, lam_ref, _,
             dx_ref, dcw_ref, dcb_ref, dwa_ref, dwx_ref, dba_ref, dbx_ref, dlam_ref, gcar, dhalo,
             wa, wx, dwa_acc, dwx_acc):
        i = pl.program_id(1)
        first = i == 0

        @pl.when(first)
        def _():
            gcar[...] = jnp.zeros_like(gcar)
            dhalo[...] = jnp.zeros_like(dhalo)
            _blocks_to_dense(wa_ref, wa)
            _blocks_to_dense(wx_ref, wx)

        at_start = rev(i) == 0
        x = x_ref[...]
        xhalo = jnp.where(at_start, 0.0, xh_ref[...])
        sh = [x, _shift_down(x, 1, xhalo), _shift_down(x, 2, xhalo), _shift_down(x, 3, xhalo)]
        xc = (cw_ref[3:4, :] * sh[0] + cw_ref[2:3, :] * sh[1] + cw_ref[1:2, :] * sh[2] + cw_ref[0:1, :] * sh[3]
              + cb_ref[...])
        (a, b), vjp = jax.vjp(_lru_gates, xc, wa[...], wx[...], ba_ref[...], bx_ref[...], lam_ref[...])
        hs = h_ref[...]
        hprev = _shift_down(hs, 1, jnp.where(at_start, 0.0, hh_ref[...]))
        dh = dy_ref[...] * _silu(z_ref[...])
        a_next = _shift_up(a, 1, jnp.ones((8, ct), F32))
        g, _ = _scan(a_next, dh, True, gcar[...])
        dxc, dwa, dwx, dba, dbx, dlam = vjp((g * hprev, g))
        dx = (cw_ref[3:4, :] * dxc + cw_ref[2:3, :] * _shift_up(dxc, 1, dhalo[...])
              + cw_ref[1:2, :] * _shift_up(dxc, 2, dhalo[...]) + cw_ref[0:1, :] * _shift_up(dxc, 3, dhalo[...]))
        dx_ref[...] = dx.astype(BF16)
        dhalo[...] = dxc[0:8]
        ag = a * g
        gcar[...] = ag[0:1]
        dcw = jnp.concatenate([jnp.sum(dxc * sh[3 - j], axis=0, keepdims=True) for j in range(4)], axis=0)
        _acc(dcw_ref, dcw, first)
        _acc(dcb_ref, jnp.sum(dxc, axis=0, keepdims=True), first)
        _acc(dwa_acc, dwa, first)
        _acc(dwx_acc, dwx, first)

        @pl.when(i == nt - 1)
        def _():
            _dense_to_blocks(dwa_acc, dwa_ref)
            _dense_to_blocks(dwx_acc, dwx_ref)

        _acc(dba_ref, dba, first)
        _acc(dbx_ref, dbx, first)
        _acc(dlam_ref, dlam, first)

    xcol = OFF_XC // ct
    zcol = OFF_ZC // ct
    vec = pl.BlockSpec((1, ct), lambda n, i: (0, n))
    mat = pl.BlockSpec((8, 80, 80), lambda n, i: (n, 0, 0))
    seq = pl.BlockSpec((tt, ct), lambda n, i: (rev(i), n))
    return pl.pallas_call(
        body, grid=(LRU_W // ct, nt),
        in_specs=[pl.BlockSpec((tt, ct), lambda n, i: (rev(i), xcol + n)),
                  pl.BlockSpec((8, ct), lambda n, i: (prev8(i), xcol + n)),
                  pl.BlockSpec((tt, ct), lambda n, i: (rev(i), zcol + n)),
                  seq, pl.BlockSpec((8, ct), lambda n, i: (prev8(i), n)), seq] + _lru_param_specs(l) + [ANY],
        out_specs=[pl.BlockSpec((tt, ct), lambda n, i: (rev(i), xcol + n)),
                   pl.BlockSpec((4, ct), lambda n, i: (0, n)), vec, mat, mat, vec, vec, vec],
        out_shape=[jax.ShapeDtypeStruct((T, NPAD), BF16),
                   jax.ShapeDtypeStruct((4, LRU_W), F32), jax.ShapeDtypeStruct((1, LRU_W), F32),
                   jax.ShapeDtypeStruct((16, 80, 80), F32), jax.ShapeDtypeStruct((16, 80, 80), F32),
                   jax.ShapeDtypeStruct((1, LRU_W), F32), jax.ShapeDtypeStruct((1, LRU_W), F32),
                   jax.ShapeDtypeStruct((1, LRU_W), F32)],
        scratch_shapes=[pltpu.VMEM((1, ct), F32), pltpu.VMEM((8, ct), F32)] + [pltpu.VMEM((ct, ct), F32)] * 4,
        input_output_aliases={13: 0},
        name=f"lru_bwd_l{l}", compiler_params=_params(("arbitrary", "arbitrary")))(
            proj, proj, proj, hseq, hseq, dy, conv_w, conv_b, wa, wx, ba, bx, lam, dproj)


def proj_bwd(y, dp, w, l, tag, dep=None, dproj=None, gate=None):
    tm = 512
    k = y.shape[1]
    extra = [] if dep is None else [dep]
    in_specs = [pl.BlockSpec((tm, k), lambda i: (i, 0)), pl.BlockSpec((tm, D), lambda i: (i, 0)),
                pl.BlockSpec((None, k, D // 2), lambda i: (0, 0, 0))]
    out_specs = [pl.BlockSpec((tm, k), lambda i: (i, 0)), pl.BlockSpec((None, k, D), lambda i: (0, 0, 0))]
    out_shape = [jax.ShapeDtypeStruct((T, k), F32), jax.ShapeDtypeStruct((1, k, D), F32)]
    aliases = {}
    if gate is not None:
        in_specs += [pl.BlockSpec((tm, k), lambda i: (i, 0)), pl.BlockSpec((tm, k), lambda i: (i, OFF_ZC // k))]
        extra = list(gate) + extra
    if dproj is not None:
        width = k if gate is not None else PAD2
        at = OFF_ZC if gate is not None else OFF_XC - PAD2
        aliases = {3 + len(extra): 2}
        extra = extra + [dproj]
        out_specs.append(pl.BlockSpec((tm, width), lambda i: (i, at // width)))
        out_shape.append(jax.ShapeDtypeStruct((T, NPAD), BF16))
    in_specs += [ANY] * (3 + len(extra) - len(in_specs))

    def body(y_ref, dp_ref, w_ref, *rest):
        dy_ref, dw_ref = rest[len(extra):len(extra) + 2]
        dp = dp_ref[...]
        dy = _dg(dp, _unpack(w_ref[...]), _NT)
        dy_ref[...] = dy
        _acc(dw_ref, _dg(y_ref[...], dp, _TN), pl.program_id(0) == 0)
        if gate is not None:
            z = rest[1][...]
            sg = _sigmoid(z)
            rest[len(extra) + 2][...] = (dy * rest[0][...] * (sg * (1.0 + z * (1.0 - sg)))).astype(BF16)
        elif dproj is not None:
            rest[len(extra) + 2][...] = jnp.zeros((tm, PAD2), BF16)

    return pl.pallas_call(
        body, grid=(T // tm,), in_specs=in_specs, out_specs=out_specs, out_shape=out_shape,
        input_output_aliases=aliases,
        name=f"proj_{tag}_bwd_l{l}", compiler_params=_params(("arbitrary",)))(y, dp, w, *extra)


OUT_TM = 256


def _out_tile(pa, pb, pc, ga, gb, gc, wout, post_g):
    merged = _sigmoid(ga) * pa + _sigmoid(gb) * pb + _sigmoid(gc) * pc
    return _rms(dot_nn(merged, wout), post_g)


def _out_in_specs():
    tm = OUT_TM
    tok = pl.BlockSpec((tm, D), lambda i: (i, 0))
    gate = lambda off: pl.BlockSpec((tm, 512), lambda i, off=off: (i, off // 512))
    return [tok, tok, tok, gate(OFF_GA), gate(OFF_GA + 512), gate(OFF_GB), gate(OFF_GB + 512), gate(OFF_GC),
            gate(OFF_GC + 512), pl.BlockSpec((None, D, D // 2), lambda i: (0, 0, 0)), pl.BlockSpec((1, D), lambda i: (0, 0))]


def _gates(refs):
    return [jnp.concatenate([refs[2 * j][...], refs[2 * j + 1][...]], axis=1) for j in range(3)]


def out_fwd(x, ya, yb, yc, proj, wpa, wpb, wpc, wout, post_g, l):
    tm = OUT_TM

    def body(ya_ref, yb_ref, yc_ref, g0, g1, g2, g3, g4, g5, wo_ref, pg_ref, x_ref, wa_ref, wb_ref, wc_ref,
             o_ref, pa_ref, pb_ref, pc_ref, wa, wb, wc, wo):
        @pl.when(pl.program_id(0) == 0)
        def _():
            for dst, src in ((wa, wa_ref), (wb, wb_ref), (wc, wc_ref), (wo, wo_ref)):
                dst[...] = _unpack(src[...]).astype(BF16)

        pa = _dg(ya_ref[...], wa[...], _NN)
        pb = _dg(yb_ref[...], wb[...], _NN)
        pc = _dg(yc_ref[...], wc[...], _NN)
        ga, gb, gc = _gates([g0, g1, g2, g3, g4, g5])
        o_ref[...] = x_ref[...] + _out_tile(pa, pb, pc, ga, gb, gc, wo[...], pg_ref[...])
        pa_ref[...] = pa.astype(BF16)
        pb_ref[...] = pb.astype(BF16)
        pc_ref[...] = pc.astype(BF16)

    tok = pl.BlockSpec((tm, D), lambda i: (i, 0))
    words = lambda k: pl.BlockSpec((None, k, D // 2), lambda i: (0, 0, 0))
    specs = _out_in_specs()
    specs[2] = pl.BlockSpec((tm, LRU_W), lambda i: (i, 0))
    return pl.pallas_call(
        body, grid=(T // tm,), in_specs=specs + [tok, words(D), words(D), words(LRU_W)], out_specs=[tok] * 4,
        out_shape=[jax.ShapeDtypeStruct((T, D), F32)] + [jax.ShapeDtypeStruct((T, D), BF16)] * 3,
        scratch_shapes=[pltpu.VMEM((D, D), BF16), pltpu.VMEM((D, D), BF16), pltpu.VMEM((LRU_W, D), BF16),
                        pltpu.VMEM((D, D), BF16)],
        name=f"out_fwd_l{l}", compiler_params=_params(("arbitrary",)))(
            ya, yb, yc, proj, proj, proj, proj, proj, proj, wout, post_g, x, wpa, wpb, wpc)


def out_bwd(pa, pb, pc, proj, wout, post_g, dxn, l, dep=None):
    tm = OUT_TM
    nsteps = T // tm

    def body(pa_ref, pb_ref, pc_ref, g0, g1, g2, g3, g4, g5, w_ref, pg_ref, dxn_ref, *rest):
        dpa_ref, dpb_ref, dpc_ref, dproj_ref, dw_ref, dpg_ref, gbuf, sem = rest[-8:]
        i = pl.program_id(0)
        first = i == 0
        slot = i % 2
        ga, gb, gc = _gates([g0, g1, g2, g3, g4, g5])
        _, vjp = jax.vjp(_out_tile, pa_ref[...], pb_ref[...], pc_ref[...], ga, gb, gc, _unpack(w_ref[...]), pg_ref[...])
        dpa, dpb, dpc, dga, dgb, dgc, dw, dpg = vjp(dxn_ref[...])
        dpa_ref[...] = dpa.astype(BF16)
        dpb_ref[...] = dpb.astype(BF16)
        dpc_ref[...] = dpc.astype(BF16)
        _acc(dw_ref, dw, first)
        _acc(dpg_ref, dpg, first)

        def writeback(step, s):
            rows = pl.ds(pl.multiple_of(step * tm, tm), tm)
            return pltpu.make_async_copy(gbuf.at[s], dproj_ref.at[rows, pl.ds(OFF_GA, 3072)], sem.at[s])

        gbuf[slot, :, 0:1024] = dga.astype(BF16)
        gbuf[slot, :, 1024:2048] = dgb.astype(BF16)
        gbuf[slot, :, 2048:3072] = dgc.astype(BF16)
        writeback(i, slot).start()

        @pl.when(i > 0)
        def _():
            writeback(i - 1, 1 - slot).wait()

        @pl.when(i == nsteps - 1)
        def _():
            writeback(i, slot).wait()

    tok = pl.BlockSpec((tm, D), lambda i: (i, 0))
    deps = [] if dep is None else [dep]
    return pl.pallas_call(
        body, grid=(nsteps,), in_specs=_out_in_specs() + [tok] + [ANY] * len(deps),
        out_specs=[tok, tok, tok, ANY, pl.BlockSpec((None, D, D), lambda i: (0, 0, 0)), pl.BlockSpec((1, D), lambda i: (0, 0))],
        out_shape=[jax.ShapeDtypeStruct((T, D), BF16)] * 3 + [jax.ShapeDtypeStruct((T, NPAD), BF16),
                                                            jax.ShapeDtypeStruct((1, D, D), F32), jax.ShapeDtypeStruct((1, D), F32)],
        scratch_shapes=[pltpu.VMEM((2, tm, 3072), BF16), pltpu.SemaphoreType.DMA((2,))],
        name=f"out_bwd_l{l}", compiler_params=_params(("arbitrary",)))(
            pa, pb, pc, proj, proj, proj, proj, proj, proj, wout, post_g, dxn, *deps)


def loss_head(y, target):
    tm = 256

    def body(y_ref, t_ref, loss_ref, dy_ref):
        e = y_ref[...] - t_ref[...]
        dy_ref[...] = e * (1.0 / D)
        val = 0.5 * jnp.sum(jnp.mean(e * e, axis=-1, keepdims=True), axis=0, keepdims=True)
        _acc(loss_ref, jnp.broadcast_to(val, (8, 128)), pl.program_id(0) == 0)

    tok = pl.BlockSpec((tm, D), lambda i: (i, 0))
    total, dy = pl.pallas_call(
        body, grid=(T // tm,), in_specs=[tok, tok],
        out_specs=[pl.BlockSpec((8, 128), lambda i: (0, 0)), tok],
        out_shape=[jax.ShapeDtypeStruct((8, 128), F32), jax.ShapeDtypeStruct((T, D), F32)],
        name="loss_head", compiler_params=_params(("arbitrary",)))(y, target)
    return total[0, 0], dy


def _rope_tables():
    pos = jnp.arange(T, dtype=F32)
    inv_freq = 10000.0 ** (-jnp.arange(0, 64, 2, dtype=F32) / 64)
    ang = pos[:, None] * inv_freq[None, :]
    cos, sin = jnp.cos(ang), jnp.sin(ang)
    ctab = jnp.concatenate([jnp.ones((T, 128), F32), cos, cos], axis=1)
    stab = jnp.concatenate([jnp.zeros((T, 128), F32), -sin, sin], axis=1)
    return ctab, stab


def _layer_fwd(x, l, w, gw, tabs, dep=None, mid=None):
    row = lambda a: a[l][None]
    proj, h = inproj_fwd(x, row(w["pre_norm_g"]), gw["w_in_t"], l, dep)
    ya = gmlp_fwd(proj, row(w["gm_ln_g"]), row(w["gm_ln_b"]), w["gm_ws"][l], w["gm_bs"][l][..., None], l)
    dep2 = None
    if mid is not None:
        gw, dep2 = mid(ya)
    q, k, v = qkv_fwd(proj, row(w["mla_q_norm_g"]), row(w["kv_g384"]), gw["wq"], gw["wkv"], tabs[0], tabs[1], l, dep2)
    yb = attn_fwd(q, k, v, proj, l)
    hseq, yc = lru_fwd(proj, gw["conv"], row(w["lru_conv_b"]), w["lru_w_a"], w["lru_w_x"],
                       row(w["lru_b_a"]), row(w["lru_b_x"]), row(w["lru_lambda"]), l)
    xn, pa, pb, pc = out_fwd(x, ya, yb, yc, proj, gw["w_proj_a"], gw["w_proj_b"], gw["w_proj_c"], gw["w_out"],
                             row(w["post_norm_g"]), l)
    return xn, (x, proj, h, ya, q, k, v, yb, hseq, yc, pa, pb, pc)


def _layer_bwd(dxn, l, w, gw, tabs, saved, dep=None, early=None, mid=None, late=None):
    x, proj, h, ya, q, k, v, yb, hseq, yc, pa, pb, pc = saved
    row = lambda a: a[l][None]
    g, gg = {}, {}
    dpa, dpb, dpc, dproj, gg["w_out"], dpost = out_bwd(pa, pb, pc, proj, gw["w_out"], row(w["post_norm_g"]), dxn, l, dep)
    g["post_norm_g"] = dpost[0]
    dep1 = early(dpa) if early is not None else None
    dya, gg["w_proj_a"], dproj = proj_bwd(ya, dpa, gw["w_proj_a"], l, "a", dep1, dproj)
    dyb, gg["w_proj_b"] = proj_bwd(yb, dpb, gw["w_proj_b"], l, "b")
    dyc, gg["w_proj_c"], dproj = proj_bwd(yc, dpc, gw["w_proj_c"], l, "c", None, dproj, (hseq, proj))
    dproj, dln_g, dln_b, g["gm_ws"], dbs = gmlp_bwd(proj, row(w["gm_ln_g"]), row(w["gm_ln_b"]), w["gm_ws"][l],
                                                   w["gm_bs"][l][..., None], dya, dproj, l)
    g["gm_ln_g"], g["gm_ln_b"], g["gm_bs"] = dln_g[0], dln_b[0], dbs[..., 0]
    dq, dk, dv, dproj = attn_bwd(q, k, v, proj, dyb, dproj, l)
    dproj, dqg, dkvg, dwq, dwkv = qkv_bwd(proj, row(w["mla_q_norm_g"]), row(w["kv_g384"]), gw["wq"], gw["wkv"],
                                          tabs[0], tabs[1], dq, dk, dv, dproj, l)
    gg["wq"], gg["wkv"] = dwq.reshape(1, 1536, 384), dwkv.reshape(1, 2048, 256)
    g["mla_q_norm_g"], g["mla_kv_norm_g"] = dqg[0], dkvg[0, :256]
    dproj, dcw, dcb, dwa, dwx, dba, dbx, dlam = lru_bwd(
        proj, hseq, dyc, gw["conv"], row(w["lru_conv_b"]), w["lru_w_a"], w["lru_w_x"],
        row(w["lru_b_a"]), row(w["lru_b_x"]), row(w["lru_lambda"]), dproj, l)
    gg["conv"] = jnp.pad(dcw.T, ((0, 0), (0, 124)))[None]
    g["lru_conv_b"], g["lru_b_a"], g["lru_b_x"], g["lru_lambda"] = dcb[0], dba[0], dbx[0], dlam[0]
    g["lru_w_a"], g["lru_w_x"] = dwa, dwx
    dep2 = mid(gg, dproj) if mid is not None else None
    gg["w_in_t"], dh = inproj_bwd(dproj, h, gw["w_in_t"], l, dep2)
    dep3 = late(gg["w_in_t"]) if late is not None else None
    dx, dpre = prenorm_bwd(x, row(w["pre_norm_g"]), dh, dxn, l, dep3)
    g["pre_norm_g"] = dpre[0]
    return dx, gg, g


MESH = pl.DeviceIdType.MESH
HBM = pl.BlockSpec(memory_space=pltpu.HBM)
SEM = pl.BlockSpec(memory_space=pltpu.SEMAPHORE)
EFFECT = pltpu.SideEffectType.DATAFLOW_SIDE_EFFECTING
FLIPS = ((1, 0), (0, 1), (1, 1))


def _win_off(k, s):
    g = SHARD * k + s
    return g + jnp.where(g >= PAD1_AT, PAD1, 0) + jnp.where(g >= PAD2_AT, PAD2, 0)


def _plain_off(rows):
    return lambda k, s: rows * k + s


class Spec:
    def __init__(self, rows, cols, full_rows, pieces=None, off=None, layers=1, packed=None):
        self.rows, self.cols, self.full_rows, self.layers = rows, cols, full_rows, layers
        self.pieces = pieces or ((0, rows),)
        self.off = off or _plain_off(rows)
        self.packed = cols % 256 == 0 if packed is None else packed
        self.wcols = cols // 2 if self.packed else cols

    def to_words(self, a):
        return _pack(a) if self.packed else a

    def from_words(self, p):
        return _unpack(p) if self.packed else p


def _pack(a):
    def bits(v):
        u = lax.bitcast_convert_type(v, jnp.uint32)
        return u + jnp.uint32(0x7FFF) + ((u >> 16) & jnp.uint32(1))

    words = [(bits(a[:, g:g + 128]) >> 16) | (bits(a[:, g + 128:g + 256]) & jnp.uint32(0xFFFF0000))
             for g in range(0, a.shape[-1], 256)]
    return lax.bitcast_convert_type(jnp.concatenate(words, axis=-1) if len(words) > 1 else words[0], F32)


def _unpack(p):
    w = lax.bitcast_convert_type(p, jnp.uint32)
    lo = lax.bitcast_convert_type(w << 16, F32)
    hi = lax.bitcast_convert_type(w & jnp.uint32(0xFFFF0000), F32)
    return jnp.concatenate([h[:, g:g + 128] for g in range(0, p.shape[-1], 128) for h in (lo, hi)], axis=-1)


WEIGHT_SPECS = {
    "w_in_t": Spec(SHARD, D, NPAD, WIN_PIECES, _win_off),
    "wq": Spec(192, 384, 1536),
    "wkv": Spec(256, 256, 2048),
    "conv": Spec(160, 128, 1280),
    "w_proj_a": Spec(128, D, 1024),
    "w_proj_b": Spec(128, D, 1024),
    "w_proj_c": Spec(160, D, 1280),
    "w_out": Spec(128, D, 1024),
}
REP_ROWS = 72
REP_SPEC = Spec(REP_ROWS, D, REP_ROWS * NDEV, packed=False)


def _coords():
    return lax.axis_index("x"), lax.axis_index("y"), lax.axis_index("c")


def _rows(ref, start, n):
    if not isinstance(start, int):
        start = pl.multiple_of(start, 8)
    return ref.at[:, pl.ds(start, n), :]


def _col_tile(cols):
    return 256 if cols % 256 == 0 else cols


def _n_pieces(specs):
    return sum(len(sp.pieces) for sp in specs)


def pack_place(shard, sp, layer, tag, dep=None):
    gaps = ((PAD1_AT, PAD1), (PAD2_AT + PAD1, PAD2)) if sp.off is _win_off else ()
    npc = len(sp.pieces)
    deps = [] if dep is None else [dep]

    def body(s_ref, *rest):
        words_ref, full_ref, buf, zbuf, sem = rest[-5:]
        l = 0
        x, y, c = _coords()
        me = 4 * x + 2 * y + c
        words = sp.to_words(s_ref[...])
        words_ref[...] = words
        buf[...] = words
        copies = [pltpu.make_async_copy(buf.at[pl.ds(s, n), :],
                                        full_ref.at[l, pl.ds(pl.multiple_of(sp.off(me, s), 8), n), :], sem.at[i])
                  for i, (s, n) in enumerate(sp.pieces)]
        if gaps:
            zbuf[...] = jnp.zeros_like(zbuf)
            copies += [pltpu.make_async_copy(zbuf.at[pl.ds(0, n), :], full_ref.at[l, pl.ds(at, n), :], sem.at[npc + i])
                       for i, (at, n) in enumerate(gaps)]
        for cp in copies:
            cp.start()
        for cp in copies:
            cp.wait()

    return pl.pallas_call(
        body, grid=(1,), in_specs=[pl.BlockSpec((None, sp.rows, sp.cols), lambda i: (layer, 0, 0))] + [ANY] * len(deps),
        out_specs=[pl.BlockSpec((None, sp.rows, sp.wcols), lambda i: (0, 0, 0)), ANY],
        out_shape=[jax.ShapeDtypeStruct((sp.layers, sp.rows, sp.wcols), F32),
                   jax.ShapeDtypeStruct((sp.layers, sp.full_rows, sp.wcols), F32)],
        scratch_shapes=[pltpu.VMEM((sp.rows, sp.wcols), F32), pltpu.VMEM((PAD2 if gaps else 8, sp.wcols), F32),
                        pltpu.SemaphoreType.DMA((npc + len(gaps),))],
        name=f"pack_place_{tag}", compiler_params=_params(("arbitrary",)))(shard, *deps)


def _gather_copies(srcs, bufs, specs, ssem, rsem, landing):
    x, y, c = _coords()
    me = 4 * x + 2 * y + c
    targets = [(x, y, 1 - c)] + [(x ^ fx, y ^ fy, c) for fx, fy in FLIPS]
    copies = []
    p = 0
    for src, buf, sp in zip(srcs, bufs, specs):
        for s, n in sp.pieces:
            for t, (tx, ty, tc) in enumerate(targets):
                owner = 4 * tx + 2 * ty + tc if landing else me
                copies.append(pltpu.make_async_remote_copy(_rows(src, s, n), _rows(buf, sp.off(owner, s), n),
                                                           ssem.at[4 * p + t], rsem.at[4 * p + t],
                                                           device_id=(tx, ty, tc), device_id_type=MESH))
            p += 1
    return copies


def gather_send(words, fulls, specs, tag):
    ns, npc = len(specs), _n_pieces(specs)

    def body(*refs):
        srcs, bufs, sems = refs[:ns], refs[2 * ns:3 * ns], refs[3 * ns:]
        for cp in _gather_copies(srcs, bufs, specs, *sems, False):
            cp.start()
        for cp in _gather_copies(srcs, bufs, specs, *sems, False):
            cp.wait_send()
        for cp in _gather_copies(srcs, bufs, specs, *sems, True):
            cp.wait_recv()

    return pl.pallas_call(
        body, in_specs=[ANY] * (2 * ns), out_specs=[ANY] * ns,
        out_shape=[jax.ShapeDtypeStruct(f.shape, f.dtype) for f in fulls],
        input_output_aliases={ns + i: i for i in range(ns)},
        scratch_shapes=[pltpu.SemaphoreType.DMA((4 * npc,)), pltpu.SemaphoreType.DMA((4 * npc,))],
        name=f"gather_send_{tag}", compiler_params=pltpu.CompilerParams(has_side_effects=True))(*words, *fulls)


def _in_hbm(arrays):
    return [pltpu.with_memory_space_constraint(a, pltpu.HBM) for a in arrays]


def gather_start(words, fulls, specs, dep, tag):
    ns, npc = len(specs), _n_pieces(specs)
    deps = [] if dep is None else [dep]

    def body(*refs):
        ssem, rsem = refs[2 * ns + len(deps):2 * ns + len(deps) + 2]
        for cp in _gather_copies(refs[:ns], refs[ns:2 * ns], specs, ssem, rsem, False):
            cp.start()
        refs[-1][...] = jnp.zeros_like(refs[-1])

    outs = pl.pallas_call(
        body, in_specs=[HBM] * (2 * ns) + [ANY] * len(deps),
        out_specs=[SEM, SEM] + [HBM] * (2 * ns) + [pl.BlockSpec(memory_space=pltpu.VMEM)],
        out_shape=[pltpu.SemaphoreType.DMA((4 * npc,)), pltpu.SemaphoreType.DMA((4 * npc,))]
        + [pltpu.HBM(a.shape, a.dtype) for a in list(words) + list(fulls)] + [jax.ShapeDtypeStruct((8, 128), F32)],
        input_output_aliases={i: 2 + i for i in range(2 * ns)},
        name=f"gather_start_{tag}", compiler_params=pltpu.CompilerParams(has_side_effects=EFFECT))(
            *_in_hbm(list(words) + list(fulls)), *deps)
    return outs[0], outs[1], outs[2:2 + ns], outs[2 + ns:2 + 2 * ns], outs[-1]


def gather_wait(ssem, rsem, words, fulls, specs, after, tag):
    ns = len(specs)

    def body(*refs):
        srcs, bufs, ssem, rsem = refs[:ns], refs[ns:2 * ns], refs[2 * ns], refs[2 * ns + 1]
        for cp in _gather_copies(srcs, bufs, specs, ssem, rsem, False):
            cp.wait_send()
        for cp in _gather_copies(srcs, bufs, specs, ssem, rsem, True):
            cp.wait_recv()

    outs = pl.pallas_call(
        body, in_specs=[HBM] * (2 * ns) + [SEM, SEM, ANY], out_specs=[HBM] * (2 * ns),
        out_shape=[pltpu.HBM(a.shape, a.dtype) for a in list(words) + list(fulls)],
        input_output_aliases={i: i for i in range(2 * ns)},
        name=f"gather_wait_{tag}", compiler_params=pltpu.CompilerParams(has_side_effects=EFFECT))(
            *words, *fulls, ssem, rsem, after)
    return outs[ns:]


def gather_forward(fulls, specs, tag):
    ns, npc = len(specs), _n_pieces(specs)

    def body(*refs):
        bufs = refs[ns:2 * ns]
        ssem, rsem = refs[2 * ns:]
        x, y, c = _coords()
        sibling = (x, y, 1 - c)
        waits = []
        p = 0
        for buf, sp in zip(bufs, specs):
            for s, n in sp.pieces:
                for t, (fx, fy) in enumerate(FLIPS):
                    chip = 4 * (x ^ fx) + 2 * (y ^ fy)
                    here = _rows(buf, sp.off(chip + c, s), n)
                    send = pltpu.make_async_remote_copy(here, here, ssem.at[t, p], rsem.at[t, p],
                                                        device_id=sibling, device_id_type=MESH)
                    send.start()
                    waits.append(send.wait_send)
                    there = _rows(buf, sp.off(chip + 1 - c, s), n)
                    waits.append(pltpu.make_async_remote_copy(here, there, ssem.at[t, p], rsem.at[t, p],
                                                              device_id=sibling, device_id_type=MESH).wait_recv)
                p += 1
        for w in waits:
            w()

    return pl.pallas_call(
        body, in_specs=[ANY] * ns, out_specs=[ANY] * ns,
        out_shape=[jax.ShapeDtypeStruct(f.shape, f.dtype) for f in fulls],
        input_output_aliases={i: i for i in range(ns)},
        scratch_shapes=[pltpu.SemaphoreType.DMA((3, npc)), pltpu.SemaphoreType.DMA((3, npc))],
        name=f"gather_forward_{tag}", compiler_params=pltpu.CompilerParams(has_side_effects=True))(*fulls)


def all_gather(shards, layer, specs, names, tag):
    placed = [pack_place(s, sp, layer, f"{tag}_{n}") for s, sp, n in zip(shards, specs, names)]
    fulls = gather_send([p[0] for p in placed], [p[1] for p in placed], specs, tag)
    return gather_forward(fulls, specs, tag)


def _pair_copies(srcs, theirs, specs, ssem, rsem):
    x, y, c = _coords()
    copies = []
    p = 0
    for src, their, sp in zip(srcs, theirs, specs):
        for s, n in sp.pieces:
            for j in range(4):
                copies.append(pltpu.make_async_remote_copy(_rows(src, sp.off(2 * j + 1 - c, s), n), _rows(their.at[j], s, n),
                                                           ssem.at[4 * p + j], rsem.at[4 * p + j],
                                                           device_id=(x, y, 1 - c), device_id_type=MESH))
            p += 1
    return copies


def _pair_shapes(specs):
    return [(4, sp.layers, sp.rows, sp.cols) for sp in specs]


def reduce_pair(grads, specs, tag, dep=None):
    ns, npc = len(specs), _n_pieces(specs)
    deps = [] if dep is None else [dep]

    def body(*refs):
        copies = _pair_copies(refs[:ns], refs[ns + len(deps):2 * ns + len(deps)], specs, *refs[2 * ns + len(deps):])
        for cp in copies:
            cp.start()
        for cp in copies:
            cp.wait()

    return pl.pallas_call(
        body, in_specs=[ANY] * (ns + len(deps)), out_specs=[ANY] * ns,
        out_shape=[jax.ShapeDtypeStruct(s, F32) for s in _pair_shapes(specs)],
        scratch_shapes=[pltpu.SemaphoreType.DMA((4 * npc,)), pltpu.SemaphoreType.DMA((4 * npc,))],
        name=f"reduce_pair_{tag}", compiler_params=pltpu.CompilerParams(has_side_effects=True))(*grads, *deps)


def pair_start(grads, specs, dep, tag):
    ns, npc = len(specs), _n_pieces(specs)
    slots = [lax.empty(s, F32) for s in _pair_shapes(specs)]
    deps = [] if dep is None else [dep]

    def body(*refs):
        ssem, rsem = refs[2 * ns + len(deps):2 * ns + len(deps) + 2]
        for cp in _pair_copies(refs[:ns], refs[ns:2 * ns], specs, ssem, rsem):
            cp.start()
        refs[-1][...] = jnp.zeros_like(refs[-1])

    outs = pl.pallas_call(
        body, in_specs=[HBM] * (2 * ns) + [ANY] * len(deps),
        out_specs=[SEM, SEM] + [HBM] * (2 * ns) + [pl.BlockSpec(memory_space=pltpu.VMEM)],
        out_shape=[pltpu.SemaphoreType.DMA((4 * npc,)), pltpu.SemaphoreType.DMA((4 * npc,))]
        + [pltpu.HBM(a.shape, a.dtype) for a in list(grads) + slots] + [jax.ShapeDtypeStruct((8, 128), F32)],
        input_output_aliases={i: 2 + i for i in range(2 * ns)},
        name=f"pair_start_{tag}", compiler_params=pltpu.CompilerParams(has_side_effects=EFFECT))(
            *_in_hbm(list(grads) + slots), *deps)
    return outs[0], outs[1], outs[2:2 + ns], outs[2 + ns:2 + 2 * ns], outs[-1]


def pair_wait(ssem, rsem, grads, slots, specs, after, tag):
    ns = len(specs)

    def body(*refs):
        for cp in _pair_copies(refs[:ns], refs[ns:2 * ns], specs, refs[2 * ns], refs[2 * ns + 1]):
            cp.wait_send()
            cp.wait_recv()

    outs = pl.pallas_call(
        body, in_specs=[HBM] * (2 * ns) + [SEM, SEM, ANY], out_specs=[HBM] * (2 * ns),
        out_shape=[pltpu.HBM(a.shape, a.dtype) for a in list(grads) + list(slots)],
        input_output_aliases={i: i for i in range(2 * ns)},
        name=f"pair_wait_{tag}", compiler_params=pltpu.CompilerParams(has_side_effects=EFFECT))(
            *grads, *slots, ssem, rsem, after)
    return outs[:ns], outs[ns:]


def pair_sum(g, r1, sp, tag):
    npc = len(sp.pieces)
    fetch_all = 4 * sp.rows * sp.cols * 4 <= (8 << 20)

    def body(g_ref, r_ref, own_ref, words_ref, gbuf, sem):
        l, j = pl.program_id(0), pl.program_id(1)
        x, y, c = _coords()

        def copies(chip, slot):
            return [pltpu.make_async_copy(g_ref.at[l, pl.ds(pl.multiple_of(sp.off(2 * chip + c, s), 8), n), :],
                                          gbuf.at[slot, pl.ds(s, n), :], sem.at[slot, i])
                    for i, (s, n) in enumerate(sp.pieces)]

        def fetch(chip, slot):
            for cp in copies(chip, slot):
                cp.start()

        def arrived(chip, slot):
            for cp in copies(chip, slot):
                cp.wait()

        if fetch_all:
            @pl.when(j == 0)
            def _():
                for chip in range(4):
                    fetch(chip, chip)
                for chip in range(4):
                    arrived(chip, chip)

            mine = gbuf[j]
        else:
            @pl.when(j == 0)
            def _():
                fetch(0, 0)

            @pl.when(j < 3)
            def _():
                fetch(j + 1, (j + 1) % 2)

            arrived(j, j % 2)
            mine = gbuf[j % 2]
        p = mine + r_ref[...]
        words_ref[...] = sp.to_words(p)

        @pl.when(j == 2 * x + y)
        def _():
            own_ref[...] = p

    return pl.pallas_call(
        body, grid=(sp.layers, 4),
        in_specs=[ANY, pl.BlockSpec((None, None, sp.rows, sp.cols), lambda l, j: (j, l, 0, 0))],
        out_specs=[pl.BlockSpec((None, sp.rows, sp.cols), lambda l, j: (l, 0, 0)),
                   pl.BlockSpec((None, None, sp.rows, sp.wcols), lambda l, j: (j, l, 0, 0))],
        out_shape=[jax.ShapeDtypeStruct((sp.layers, sp.rows, sp.cols), F32),
                   jax.ShapeDtypeStruct((4, sp.layers, sp.rows, sp.wcols), F32)],
        scratch_shapes=[pltpu.VMEM((4 if fetch_all else 2, sp.rows, sp.cols), F32), pltpu.SemaphoreType.DMA((4, npc))],
        name=f"pair_sum_{tag}", compiler_params=_params(("arbitrary", "arbitrary")))(g, r1)


def _chip_copies(srcs, dsts, ssem, rsem):
    x, y, c = _coords()
    copies = []
    for i, (src, dst) in enumerate(zip(srcs, dsts)):
        for t, (fx, fy) in enumerate(FLIPS):
            tx, ty = x ^ fx, y ^ fy
            copies.append(pltpu.make_async_remote_copy(src.at[2 * tx + ty], dst.at[t], ssem.at[3 * i + t], rsem.at[3 * i + t],
                                                       device_id=(tx, ty, c), device_id_type=MESH))
    return copies


def _slot_shapes(words):
    return [(3,) + w.shape[1:] for w in words]


def reduce_chips(words, specs, tag):
    ns = len(specs)

    def body(*refs):
        copies = _chip_copies(refs[:ns], refs[ns:2 * ns], *refs[2 * ns:])
        for cp in copies:
            cp.start()
        for cp in copies:
            cp.wait()

    return pl.pallas_call(
        body, in_specs=[ANY] * ns, out_specs=[ANY] * ns,
        out_shape=[jax.ShapeDtypeStruct(s, F32) for s in _slot_shapes(words)],
        scratch_shapes=[pltpu.SemaphoreType.DMA((3 * ns,)), pltpu.SemaphoreType.DMA((3 * ns,))],
        name=f"reduce_chips_{tag}", compiler_params=pltpu.CompilerParams(has_side_effects=True))(*words)


def chips_start(words, specs, tag):
    ns = len(specs)
    slots = [lax.empty(s, F32) for s in _slot_shapes(words)]

    def body(*refs):
        ssem, rsem = refs[2 * ns:2 * ns + 2]
        for cp in _chip_copies(refs[:ns], refs[ns:2 * ns], ssem, rsem):
            cp.start()
        refs[-1][...] = jnp.zeros_like(refs[-1])

    outs = pl.pallas_call(
        body, in_specs=[HBM] * (2 * ns),
        out_specs=[SEM, SEM] + [HBM] * (2 * ns) + [pl.BlockSpec(memory_space=pltpu.VMEM)],
        out_shape=[pltpu.SemaphoreType.DMA((3 * ns,)), pltpu.SemaphoreType.DMA((3 * ns,))]
        + [pltpu.HBM(a.shape, a.dtype) for a in list(words) + slots] + [jax.ShapeDtypeStruct((8, 128), F32)],
        input_output_aliases={i: 2 + i for i in range(2 * ns)},
        name=f"chips_start_{tag}", compiler_params=pltpu.CompilerParams(has_side_effects=EFFECT))(
            *_in_hbm(list(words) + slots))
    return outs[0], outs[1], outs[2:2 + ns], outs[2 + ns:2 + 2 * ns], outs[-1]


def chips_wait(ssem, rsem, words, slots, specs, after, tag):
    ns = len(specs)

    def body(*refs):
        for cp in _chip_copies(refs[:ns], refs[ns:2 * ns], refs[2 * ns], refs[2 * ns + 1]):
            cp.wait_send()
            cp.wait_recv()

    outs = pl.pallas_call(
        body, in_specs=[HBM] * (2 * ns) + [SEM, SEM, ANY], out_specs=[HBM] * (2 * ns),
        out_shape=[pltpu.HBM(a.shape, a.dtype) for a in list(words) + list(slots)],
        input_output_aliases={i: i for i in range(2 * ns)},
        name=f"chips_wait_{tag}", compiler_params=pltpu.CompilerParams(has_side_effects=EFFECT))(
            *words, *slots, ssem, rsem, after)
    return outs[ns:]


def sum_chips(own, r2, sp, tag):
    def body(own_ref, r_ref, o_ref):
        o_ref[...] = ((own_ref[...] + sp.from_words(r_ref[0])) + sp.from_words(r_ref[1])) + sp.from_words(r_ref[2])

    blk = pl.BlockSpec((None, sp.rows, sp.cols), lambda l: (l, 0, 0))
    return pl.pallas_call(
        body, grid=(sp.layers,), in_specs=[blk, pl.BlockSpec((3, None, sp.rows, sp.wcols), lambda l: (0, l, 0, 0))],
        out_specs=blk, out_shape=jax.ShapeDtypeStruct((sp.layers, sp.rows, sp.cols), F32),
        name=f"sum_chips_{tag}", compiler_params=_params(("arbitrary",)))(own, r2)


def reduce_scatter_start(grads, specs, names, dep, tag):
    theirs = reduce_pair(grads, specs, tag, dep)
    sums = [pair_sum(g, r1, sp, f"{tag}_{n}") for g, r1, sp, n in zip(grads, theirs, specs, names)]
    ssem, rsem, words, slots, token = chips_start([s[1] for s in sums], specs, tag)
    return (ssem, rsem, words, slots, [s[0] for s in sums]), token


def reduce_scatter_finish(state, after, specs, tag):
    ssem, rsem, words, slots, own = state
    return list(zip(own, chips_wait(ssem, rsem, words, slots, specs, after, tag)))


def reduce_scatter(grads, specs, names, tag, dep=None):
    theirs = reduce_pair(grads, specs, tag, dep)
    sums = [pair_sum(g, r1, sp, f"{tag}_{n}") for g, r1, sp, n in zip(grads, theirs, specs, names)]
    return list(zip([s[0] for s in sums], reduce_chips([s[1] for s in sums], specs, tag)))


def _adamw_math(w, g, m, v):
    c1 = 1.0 - ADAM_B1 ** ADAM_STEP
    c2 = 1.0 - ADAM_B2 ** ADAM_STEP
    m2 = ADAM_B1 * m + (1.0 - ADAM_B1) * g
    v2 = ADAM_B2 * v + (1.0 - ADAM_B2) * (g * g)
    return -ADAM_LR * ((m2 / c1) / (jnp.sqrt(v2 / c2) + ADAM_EPS) + ADAM_WD * w), m2, v2


def adamw(w, g, m, v, name, dep=None):
    shape = w.shape
    cols = shape[-1]
    rows = math.prod(shape[:-1])
    tr = rows
    while tr * cols * 4 > (1 << 20) and tr % 16 == 0:
        tr //= 2
    deps = [] if dep is None else [dep]

    def body(w_ref, g_ref, m_ref, v_ref, *rest):
        d_ref, nm_ref, nv_ref = rest[-3:]
        d_ref[...], nm_ref[...], nv_ref[...] = _adamw_math(w_ref[...], g_ref[...], m_ref[...], v_ref[...])

    blk = pl.BlockSpec((tr, cols), lambda i: (i, 0))
    outs = pl.pallas_call(
        body, grid=(rows // tr,), in_specs=[blk] * 4 + [ANY] * len(deps), out_specs=[blk] * 3,
        out_shape=[jax.ShapeDtypeStruct((rows, cols), F32)] * 3,
        name=f"adamw_{name}", compiler_params=_params(("arbitrary",)))(
            *[a.reshape(rows, cols) for a in (w, g, m, v)], *deps)
    return [o.reshape(shape) for o in outs]


def adamw_layer(w, sums, m, v, sp, l, prev, dep, name):
    _, rows, cols = w.shape
    tc = _col_tile(cols)
    twc = tc // 2 if sp.packed else tc
    extra = ([] if prev is None else list(prev)) + ([] if dep is None else [dep])

    def body(w_ref, own_ref, r_ref, m_ref, v_ref, *rest):
        g_ref, d_ref, nm_ref, nv_ref = rest[-4:]
        g = ((own_ref[...] + sp.from_words(r_ref[0])) + sp.from_words(r_ref[1])) + sp.from_words(r_ref[2])
        g_ref[...] = g
        d_ref[...], nm_ref[...], nv_ref[...] = _adamw_math(w_ref[...], g, m_ref[...], v_ref[...])

    blk = pl.BlockSpec((None, rows, tc), lambda n: (l, 0, n))
    return pl.pallas_call(
        body, grid=(cols // tc,),
        in_specs=[blk, pl.BlockSpec((None, rows, tc), lambda n: (0, 0, n)),
                  pl.BlockSpec((3, None, rows, twc), lambda n: (0, 0, 0, n)), blk, blk] + [ANY] * len(extra),
        out_specs=[blk] * 4, out_shape=[jax.ShapeDtypeStruct(w.shape, F32)] * 4,
        input_output_aliases={} if prev is None else {5 + i: i for i in range(4)},
        name=f"adamw_{name}_l{l}", compiler_params=_params(("arbitrary",)))(w, sums[0], sums[1], m, v, *extra)


WEIGHTS = ("pre_norm_g", "w_in", "gm_ln_g", "gm_ln_b", "gm_ws", "gm_bs", "mla_q_norm_g", "mla_w_uq", "mla_kv_norm_g",
           "mla_w_ukv", "lru_conv_w", "lru_conv_b", "lru_w_a", "lru_b_a", "lru_w_x", "lru_b_x", "lru_lambda",
           "w_proj_a", "w_proj_b", "w_proj_c", "w_out", "post_norm_g")
SHARDED = ("w_in", "mla_w_uq", "mla_w_ukv", "lru_conv_w", "w_proj_a", "w_proj_b", "w_proj_c", "w_out")
REPLICATED = tuple(n for n in WEIGHTS if n not in SHARDED)


def _step(x, target, wts, ms, vs):
    t12 = lambda a: jnp.swapaxes(a, 1, 2)
    names = list(WEIGHT_SPECS)
    specs = [WEIGHT_SPECS[n] for n in names]
    tabs = _rope_tables()
    own = {"w_in_t": t12(wts["w_in"]), "wq": t12(wts["mla_w_uq"]), "wkv": t12(wts["mla_w_ukv"]),
           "conv": jnp.pad(t12(wts["lru_conv_w"]), ((0, 0), (0, 0), (0, 124))),
           "w_proj_a": wts["w_proj_a"], "w_proj_b": wts["w_proj_b"], "w_proj_c": wts["w_proj_c"], "w_out": wts["w_out"]}
    first, rest = ["w_in_t"], [n for n in names if n != "w_in_t"]
    sfirst, srest = [WEIGHT_SPECS[n] for n in first], [WEIGHT_SPECS[n] for n in rest]

    w = {n: wts[n] for n in REPLICATED}
    w["kv_g384"] = jnp.concatenate([wts["mla_kv_norm_g"], jnp.ones((L, 128), F32)], axis=1)

    def layer_weights(ns, words):
        gw = dict(zip(ns, words))
        gw["wq"] = gw["wq"].reshape(HEADS, 192, 384)
        gw["wkv"] = gw["wkv"].reshape(HEADS, 256, 128)
        gw["conv"] = gw["conv"][0, :, :4].T
        return gw

    place = lambda l, dep: {n: pack_place(own[n], WEIGHT_SPECS[n], l, f"w{l}_{n}", dep) for n in names}
    placed = [place(0, None)]
    words_of = lambda l, ns: [placed[l][n][0] for n in ns]
    bufs_of = lambda l, ns: [placed[l][n][1] for n in ns]
    later = {}

    ssem_a, rsem_a, wthru_a, fthru_a, token_a = gather_start(words_of(0, first), bufs_of(0, first), sfirst, None, "w0a")
    placed.append(place(1, token_a))
    win0 = gather_forward(gather_wait(ssem_a, rsem_a, wthru_a, fthru_a, sfirst, placed[1]["w_in_t"][0], "w0a"), sfirst, "w0a")
    ssem_b, rsem_b, wthru_b, fthru_b, token_b = gather_start(words_of(0, rest), bufs_of(0, rest), srest, win0[0], "w0b")

    def fwd0_mid(ya):
        rest0 = gather_forward(gather_wait(ssem_b, rsem_b, wthru_b, fthru_b, srest, ya, "w0b"), srest, "w0b")
        later["w1"] = gather_start(words_of(1, names), bufs_of(1, names), specs, rest0[0], "w1")
        later["gw0"] = layer_weights(first + rest, list(win0) + list(rest0))
        return later["gw0"], later["w1"][4]

    x1, saved0 = _layer_fwd(x, 0, w, {"w_in_t": win0[0]}, tabs, dep=token_b, mid=fwd0_mid)
    ssem1, rsem1, wthru1, fthru1, _ = later["w1"]
    words1 = gather_forward(gather_wait(ssem1, rsem1, wthru1, fthru1, specs, x1, "w1"), specs, "w1")
    gw0, gw1 = later["gw0"], layer_weights(names, words1)
    x2, saved1 = _layer_fwd(x1, 1, w, gw1, tabs)
    loss, dx2 = loss_head(x2, target)

    def bwd1_mid(gg, last):
        later["p1b"] = pair_start([gg[n] for n in rest], srest, last, "g1b")
        return later["p1b"][4]

    dx1, gg1, g1 = _layer_bwd(dx2, 1, w, gw1, tabs, saved1, mid=bwd1_mid)
    grads1b, theirs1b = pair_wait(*later["p1b"][:4], srest, dx1, "g1b")
    p1a = pair_start([gg1["w_in_t"]], sfirst, theirs1b[0], "g1a")

    def bwd0_early(last):
        grads1a, theirs1a = pair_wait(*p1a[:4], sfirst, last, "g1a")
        mine = dict(zip(first + rest, list(grads1a) + list(grads1b)))
        theirs = dict(zip(first + rest, list(theirs1a) + list(theirs1b)))
        sums = [pair_sum(mine[n], theirs[n], WEIGHT_SPECS[n], f"g1_{n}") for n in names]
        ssem, rsem, words, slots, token = chips_start([s[1] for s in sums], specs, "g1")
        later["g1"] = (ssem, rsem, words, slots, [s[0] for s in sums])
        return token

    def bwd0_mid(gg, last):
        later["g0b"], token = reduce_scatter_start([gg[n] for n in rest], srest, rest, last, "g0b")
        return token

    def bwd0_late(g_win):
        later["p0a"] = pair_start([g_win], sfirst, None, "g0a")
        return later["p0a"][4]

    dx0, gg0, g0 = _layer_bwd(dx1, 0, w, gw0, tabs, saved0, dep=p1a[4], early=bwd0_early, mid=bwd0_mid, late=bwd0_late)
    s1 = dict(zip(names, reduce_scatter_finish(later["g1"], dx0, specs, "g1")))
    s0 = dict(zip(rest, reduce_scatter_finish(later["g0b"], dx0, srest, "g0b")))

    rep_flat = jnp.concatenate([jnp.stack([g0[n], g1[n]]).reshape(-1) for n in REPLICATED])
    rep_flat = jnp.pad(rep_flat, (0, REP_ROWS * NDEV * D - rep_flat.shape[0])).reshape(1, REP_ROWS * NDEV, D)
    rep_sum = sum_chips(*reduce_scatter([rep_flat], [REP_SPEC], ["rep"], "rep")[0], REP_SPEC, "rep")
    rep_full = all_gather([rep_sum], 0, [REP_SPEC], ["rep"], "rep")[0]
    grads0a, theirs0a = pair_wait(*later["p0a"][:4], sfirst, rep_full, "g0a")
    own0a, words0a = pair_sum(grads0a[0], theirs0a[0], sfirst[0], "g0a_w_in_t")
    ssem_g, rsem_g, wthru_g, slots_g, token_g = chips_start([words0a], sfirst, "g0a")
    rep_full = rep_full.reshape(-1)

    keys = {"w_in": "w_in_t", "mla_w_uq": "wq", "mla_w_ukv": "wkv",
            "w_proj_a": "w_proj_a", "w_proj_b": "w_proj_b", "w_proj_c": "w_proj_c", "w_out": "w_out"}
    transposed = ("w_in", "mla_w_uq", "mla_w_ukv")
    state_of = lambda n: [own[keys[n]], t12(ms[n]), t12(vs[n])] if n in transposed else [wts[n], ms[n], vs[n]]

    def update(n, l, sums, prev, dep):
        wl, ml, vl = state_of(n)
        return adamw_layer(wl, sums[keys[n]], ml, vl, WEIGHT_SPECS[keys[n]], l, prev, dep, n)

    upd = {n: update(n, 1, s1, None, token_g) for n in keys}
    for n in keys:
        if n != "w_in":
            upd[n] = update(n, 0, s0, upd[n], None)
    out = {}
    conv_sp = WEIGHT_SPECS["conv"]
    g_conv = t12(jnp.concatenate([sum_chips(*s0["conv"], conv_sp, "conv0"), sum_chips(*s1["conv"], conv_sp, "conv1")])[:, :, :4])
    out["lru_conv_w"] = [g_conv] + adamw(wts["lru_conv_w"], g_conv, ms["lru_conv_w"], vs["lru_conv_w"], "lru_conv_w")
    at = 0
    for n in REPLICATED:
        size = math.prod(wts[n].shape)
        g = rep_full[at:at + size].reshape(wts[n].shape)
        out[n] = [g] + adamw(wts[n], g, ms[n], vs[n], n, upd["w_out"][0] if at == 0 else None)
        at += size

    landed = chips_wait(ssem_g, rsem_g, wthru_g, slots_g, sfirst, out[REPLICATED[-1]][1], "g0a")
    s0["w_in_t"] = (own0a, landed[0])
    upd["w_in"] = update("w_in", 0, s0, upd["w_in"], None)
    out.update({n: [t12(r) for r in upd[n]] if n in transposed else upd[n] for n in keys})

    loss = lax.psum(loss, ("x", "y", "c"))
    return (loss, dx0[None], *[out[n][k] for k in range(4) for n in WEIGHTS])


def kernel(x, pre_norm_g, w_in, gm_ln_g, gm_ln_b, gm_ws, gm_bs, mla_q_norm_g, mla_w_uq, mla_kv_norm_g, mla_w_ukv, lru_conv_w, lru_conv_b, lru_w_a, lru_b_a, lru_w_x, lru_b_x, lru_lambda, w_proj_a, w_proj_b, w_proj_c, w_out, post_norm_g, loss_target, m_pre_norm_g, m_w_in, m_gm_ln_g, m_gm_ln_b, m_gm_ws, m_gm_bs, m_mla_q_norm_g, m_mla_w_uq, m_mla_kv_norm_g, m_mla_w_ukv, m_lru_conv_w, m_lru_conv_b, m_lru_w_a, m_lru_b_a, m_lru_w_x, m_lru_b_x, m_lru_lambda, m_w_proj_a, m_w_proj_b, m_w_proj_c, m_w_out, m_post_norm_g, v_pre_norm_g, v_w_in, v_gm_ln_g, v_gm_ln_b, v_gm_ws, v_gm_bs, v_mla_q_norm_g, v_mla_w_uq, v_mla_kv_norm_g, v_mla_w_ukv, v_lru_conv_w, v_lru_conv_b, v_lru_w_a, v_lru_b_a, v_lru_w_x, v_lru_b_x, v_lru_lambda, v_w_proj_a, v_w_proj_b, v_w_proj_c, v_w_out, v_post_norm_g):
    wts = dict(zip(WEIGHTS, (pre_norm_g, w_in, gm_ln_g, gm_ln_b, gm_ws, gm_bs, mla_q_norm_g, mla_w_uq, mla_kv_norm_g,
                             mla_w_ukv, lru_conv_w, lru_conv_b, lru_w_a, lru_b_a, lru_w_x, lru_b_x, lru_lambda,
                             w_proj_a, w_proj_b, w_proj_c, w_out, post_norm_g)))
    ms = dict(zip(WEIGHTS, (m_pre_norm_g, m_w_in, m_gm_ln_g, m_gm_ln_b, m_gm_ws, m_gm_bs, m_mla_q_norm_g, m_mla_w_uq,
                            m_mla_kv_norm_g, m_mla_w_ukv, m_lru_conv_w, m_lru_conv_b, m_lru_w_a, m_lru_b_a, m_lru_w_x,
                            m_lru_b_x, m_lru_lambda, m_w_proj_a, m_w_proj_b, m_w_proj_c, m_w_out, m_post_norm_g)))
    vs = dict(zip(WEIGHTS, (v_pre_norm_g, v_w_in, v_gm_ln_g, v_gm_ln_b, v_gm_ws, v_gm_bs, v_mla_q_norm_g, v_mla_w_uq,
                            v_mla_kv_norm_g, v_mla_w_ukv, v_lru_conv_w, v_lru_conv_b, v_lru_w_a, v_lru_b_a, v_lru_w_x,
                            v_lru_b_x, v_lru_lambda, v_w_proj_a, v_w_proj_b, v_w_proj_c, v_w_out, v_post_norm_g)))
    return _step(x[0], loss_target[0], wts, ms, vs)
```

```python
import functools
import math

import jax
import jax.numpy as jnp
from jax import lax
from jax.experimental import pallas as pl
from jax.experimental.pallas import tpu as pltpu

F32 = jnp.float32
BF16 = jnp.bfloat16

T = 2048
D = 1024
L = 2
NDEV = 8
EPS = 1e-6
CHUNK_SHIFT = 6
HEADS = 8
QK = 192
LRU_W = 1280
LRU_TILE = 640
N_IN = 10432
SHARD = N_IN // NDEV
OFF_U, OFF_V, OFF_ZA, OFF_CQ, OFF_CKV, OFF_ZB = 0, 1024, 2048, 3072, 3456, 3840
OFF_XC, OFF_ZC, OFF_GA, OFF_GB, OFF_GC = 5120, 6400, 7680, 8704, 9728
NPAD = 10752
PAD1_AT, PAD1 = 3776, 64
PAD2_AT, PAD2 = 4800, 256
WIN_PIECES = ((0, 888), (888, 280), (1168, 136))
VMEM_LIMIT = 60 * 1024 * 1024

ADAM_LR, ADAM_B1, ADAM_B2, ADAM_EPS, ADAM_WD, ADAM_STEP = 0.001, 0.9, 0.999, 1e-08, 0.01, 10

_NN = (((1,), (0,)), ((), ()))
_NT = (((1,), (1,)), ((), ()))
_TN = (((0,), (0,)), ((), ()))


def _dg(a, b, dims):
    return lax.dot_general(a.astype(BF16), b.astype(BF16), dims, preferred_element_type=F32)


@jax.custom_vjp
def dot_nn(a, b):
    return _dg(a, b, _NN)


def _nn_fwd(a, b):
    return _dg(a, b, _NN), (a, b)


def _nn_bwd(res, g):
    a, b = res
    return _dg(g, b, _NT).astype(a.dtype), _dg(a, g, _TN).astype(b.dtype)


dot_nn.defvjp(_nn_fwd, _nn_bwd)


@jax.custom_vjp
def dot_nt(a, b):
    return _dg(a, b, _NT)


def _nt_fwd(a, b):
    return _dg(a, b, _NT), (a, b)


def _nt_bwd(res, g):
    a, b = res
    return _dg(g, b, _NN).astype(a.dtype), _dg(g, a, _TN).astype(b.dtype)


dot_nt.defvjp(_nt_fwd, _nt_bwd)


def _params(sem=None):
    return pltpu.CompilerParams(dimension_semantics=sem, vmem_limit_bytes=VMEM_LIMIT)


def _sigmoid(x):
    return 1.0 / (1.0 + jnp.exp(-x))


def _silu(x):
    return x * _sigmoid(x)


def _rms(x, g):
    ms = jnp.mean(x * x, axis=-1, keepdims=True)
    return x * lax.rsqrt(ms + EPS) * g


def _acc(ref, val, first):
    @pl.when(first)
    def _():
        ref[...] = val

    @pl.when(jnp.logical_not(first))
    def _():
        ref[...] += val


ANY = pl.BlockSpec(memory_space=pl.ANY)


INPROJ_TN = 768


def inproj_fwd(x, g, wt, l, dep=None):
    tn = INPROJ_TN

    def body(x_ref, g_ref, w_ref, *rest):
        proj_ref, h_ref = rest[-2:]

        @pl.when(pl.program_id(0) == 0)
        def _():
            h_ref[...] = _rms(x_ref[...], g_ref[...]).astype(BF16)

        proj_ref[...] = lax.dot_general(h_ref[...], _unpack(w_ref[...]).astype(BF16), _NT, preferred_element_type=F32)

    deps = [] if dep is None else [dep]
    return pl.pallas_call(
        body, grid=(NPAD // tn,),
        in_specs=[pl.BlockSpec((T, D), lambda j: (0, 0)), pl.BlockSpec((1, D), lambda j: (0, 0)),
                  pl.BlockSpec((None, tn, D // 2), lambda j: (0, j, 0))] + [ANY] * len(deps),
        out_specs=[pl.BlockSpec((T, tn), lambda j: (0, j)), pl.BlockSpec((T, D), lambda j: (0, 0))],
        out_shape=[jax.ShapeDtypeStruct((T, NPAD), F32), jax.ShapeDtypeStruct((T, D), BF16)],
        name=f"inproj_fwd_l{l}", compiler_params=_params(("arbitrary",)))(x, g, wt, *deps)


def inproj_bwd(dproj, h, wt, l, dep=None):
    tn = INPROJ_TN
    deps = [] if dep is None else [dep]

    def body(dp_ref, h_ref, w_ref, *rest):
        dwt_ref, dh_ref = rest[-2:]
        dp = dp_ref[...]
        dwt_ref[...] = lax.dot_general(dp, h_ref[...], _TN, preferred_element_type=F32)
        contrib = lax.dot_general(dp, _unpack(w_ref[...]).astype(BF16), _NN, preferred_element_type=F32)
        _acc(dh_ref, contrib, pl.program_id(0) == 0)

    return pl.pallas_call(
        body, grid=(NPAD // tn,),
        in_specs=[pl.BlockSpec((T, tn), lambda j: (0, j)), pl.BlockSpec((T, D), lambda j: (0, 0)),
                  pl.BlockSpec((None, tn, D // 2), lambda j: (0, j, 0))] + [ANY] * len(deps),
        out_specs=[pl.BlockSpec((None, tn, D), lambda j: (0, j, 0)), pl.BlockSpec((T, D), lambda j: (0, 0))],
        out_shape=[jax.ShapeDtypeStruct((1, NPAD, D), F32), jax.ShapeDtypeStruct((T, D), F32)],
        name=f"inproj_bwd_l{l}", compiler_params=_params(("arbitrary",)))(dproj, h, wt, *deps)


def prenorm_bwd(x, g, dh, dxn, l, dep=None):
    tm = 512
    deps = [] if dep is None else [dep]

    def body(x_ref, g_ref, dh_ref, dxn_ref, *rest):
        dx_ref, dg_ref = rest[-2:]
        _, vjp = jax.vjp(_rms, x_ref[...], g_ref[...])
        dx, dg = vjp(dh_ref[...])
        dx_ref[...] = dx + dxn_ref[...]
        _acc(dg_ref, dg, pl.program_id(0) == 0)

    tok = pl.BlockSpec((tm, D), lambda i: (i, 0))
    vec = pl.BlockSpec((1, D), lambda i: (0, 0))
    return pl.pallas_call(
        body, grid=(T // tm,), in_specs=[tok, vec, tok, tok] + [ANY] * len(deps), out_specs=[tok, vec],
        out_shape=[jax.ShapeDtypeStruct((T, D), F32), jax.ShapeDtypeStruct((1, D), F32)],
        name=f"prenorm_bwd_l{l}", compiler_params=_params(("arbitrary",)))(x, g, dh, dxn, *deps)


def _gmlp_tile(u, v, z, ln_g, ln_b, ws, bs):
    mu = jnp.mean(v, axis=-1, keepdims=True)
    vc = v - mu
    var = jnp.mean(vc * vc, axis=-1, keepdims=True)
    vn = vc * lax.rsqrt(var + EPS) * ln_g + ln_b
    qi = lax.broadcasted_iota(jnp.int32, (128, 128), 0) >> CHUNK_SHIFT
    kj = lax.broadcasted_iota(jnp.int32, (128, 128), 1) >> CHUNK_SHIFT
    mask = kj <= qi
    outs = []
    for g in range(4):
        wm = jnp.where(mask, ws[g], 0.0)
        outs.append(dot_nn(wm, vn[:, 256 * g:256 * (g + 1)]) + bs[g])
    sv = jnp.concatenate(outs, axis=1)
    return u * sv * _silu(z)


GMLP_ROWS = 256


def _gmlp_specs():
    blk = lambda c: pl.BlockSpec((GMLP_ROWS, 1024), lambda n, c=c: (n, c))
    vec = pl.BlockSpec((1, 1024), lambda n: (0, 0))
    return [blk(0), blk(1), blk(2), vec, vec,
            pl.BlockSpec((4, 128, 128), lambda n: (0, 0, 0)), pl.BlockSpec((4, 128, 1), lambda n: (0, 0, 0))]


def gmlp_fwd(proj, ln_g, ln_b, ws, bs, l):
    def body(u_ref, v_ref, z_ref, g_ref, b_ref, ws_ref, bs_ref, y_ref):
        for r in range(0, GMLP_ROWS, 128):
            rows = slice(r, r + 128)
            y_ref[rows, :] = _gmlp_tile(u_ref[rows, :], v_ref[rows, :], z_ref[rows, :], g_ref[...], b_ref[...],
                                        [ws_ref[g] for g in range(4)], [bs_ref[g] for g in range(4)])

    return pl.pallas_call(
        body, grid=(T // GMLP_ROWS,), in_specs=_gmlp_specs(),
        out_specs=pl.BlockSpec((GMLP_ROWS, 1024), lambda n: (n, 0)),
        out_shape=jax.ShapeDtypeStruct((T, 1024), F32),
        name=f"gmlp_fwd_l{l}", compiler_params=_params(("arbitrary",)))(proj, proj, proj, ln_g, ln_b, ws, bs)


def gmlp_bwd(proj, ln_g, ln_b, ws, bs, dy, dproj, l):
    def body(u_ref, v_ref, z_ref, g_ref, b_ref, ws_ref, bs_ref, dy_ref, _, dseg_ref, dg_ref, db_ref, dws_ref, dbs_ref):
        for r in range(0, GMLP_ROWS, 128):
            rows = slice(r, r + 128)
            first = jnp.logical_and(pl.program_id(0) == 0, r == 0)
            _, vjp = jax.vjp(_gmlp_tile, u_ref[rows, :], v_ref[rows, :], z_ref[rows, :], g_ref[...], b_ref[...],
                             [ws_ref[g] for g in range(4)], [bs_ref[g] for g in range(4)])
            du, dv, dz, dg, db, dws, dbs = vjp(dy_ref[rows, :])
            dseg_ref[rows, 0:1024] = du.astype(BF16)
            dseg_ref[rows, 1024:2048] = dv.astype(BF16)
            dseg_ref[rows, 2048:3072] = dz.astype(BF16)
            _acc(dg_ref, dg, first)
            _acc(db_ref, db, first)
            for g in range(4):
                _acc(dws_ref.at[g], dws[g], first)
                _acc(dbs_ref.at[g], dbs[g], first)

    vec = pl.BlockSpec((1, 1024), lambda n: (0, 0))
    return pl.pallas_call(
        body, grid=(T // GMLP_ROWS,),
        in_specs=_gmlp_specs() + [pl.BlockSpec((GMLP_ROWS, 1024), lambda n: (n, 0)), ANY],
        out_specs=[pl.BlockSpec((GMLP_ROWS, 3072), lambda n: (n, OFF_U // 3072)), vec, vec,
                   pl.BlockSpec((4, 128, 128), lambda n: (0, 0, 0)), pl.BlockSpec((4, 128, 1), lambda n: (0, 0, 0))],
        out_shape=[jax.ShapeDtypeStruct((T, NPAD), BF16), jax.ShapeDtypeStruct((1, 1024), F32),
                   jax.ShapeDtypeStruct((1, 1024), F32), jax.ShapeDtypeStruct((4, 128, 128), F32),
                   jax.ShapeDtypeStruct((4, 128, 1), F32)],
        input_output_aliases={8: 0},
        name=f"gmlp_bwd_l{l}", compiler_params=_params(("arbitrary",)))(proj, proj, proj, ln_g, ln_b, ws, bs, dy, dproj)


QKV_TM = 512


def _qkv_tile(cq, ckvr, qg, kvg, wq, wkv, ctab, stab):
    tm = cq.shape[0]
    cqn = _rms(cq, qg)
    lane = lax.broadcasted_iota(jnp.int32, ckvr.shape, 1)
    iskv = lane < 256
    ms = jnp.sum(jnp.where(iskv, ckvr * ckvr, 0.0), axis=-1, keepdims=True) * (1.0 / 256)
    lm = jnp.where(iskv, ckvr * lax.rsqrt(ms + EPS) * kvg, ckvr)
    r = lax.broadcasted_iota(jnp.int32, (64, 128), 0)
    c = lax.broadcasted_iota(jnp.int32, (64, 128), 1)
    eye = jnp.where(c == r, 1.0, 0.0)
    eye_sw = jnp.where(c == ((r + 32) & 63), 1.0, 0.0)
    z64 = jnp.zeros((64, 256), F32)
    z128 = jnp.zeros((128, 128), F32)
    rk_rope = jnp.concatenate([z64, eye], axis=1)
    rk_sw = jnp.concatenate([jnp.zeros((128, 384), F32), jnp.concatenate([z64, eye_sw], axis=1)], axis=0)
    k_sw = dot_nt(lm, rk_sw) * stab
    qs, ks, vs = [], [], []
    for h in range(HEADS):
        wn, w1, w2 = wq[h]
        wk, wv = wkv[h]
        wq_h = jnp.concatenate([wn, w1, w2], axis=0)
        wq_sw = jnp.concatenate([jnp.zeros((128, 384), F32), w2, w1], axis=0)
        qs.append(dot_nt(cqn, wq_h) * ctab + dot_nt(cqn, wq_sw) * stab)
        rk_h = jnp.concatenate([jnp.concatenate([wk, z128], axis=1), rk_rope], axis=0)
        ks.append(dot_nt(lm, rk_h) * ctab + k_sw)
        vs.append(dot_nt(lm, jnp.concatenate([wv, z128], axis=1)))
    return qs, ks, vs


def _qkv_in_specs():
    tm = QKV_TM
    return [pl.BlockSpec((tm, 384), lambda i: (i, OFF_CQ // 384)), pl.BlockSpec((tm, 384), lambda i: (i, OFF_CKV // 384)),
            pl.BlockSpec((1, 384), lambda i: (0, 0)), pl.BlockSpec((1, 384), lambda i: (0, 0)),
            pl.BlockSpec((HEADS, 192, 384), lambda i: (0, 0, 0)), pl.BlockSpec((HEADS, 256, 128), lambda i: (0, 0, 0)),
            pl.BlockSpec((tm, 192), lambda i: (i, 0)), pl.BlockSpec((tm, 192), lambda i: (i, 0))]


def _qkv_weights(wq_ref, wkv_ref):
    wq = [(wq_ref[h, 0:128, :], wq_ref[h, 128:160, :], wq_ref[h, 160:192, :]) for h in range(HEADS)]
    wkv = [(_unpack(wkv_ref[h, 0:128, :]), _unpack(wkv_ref[h, 128:256, :])) for h in range(HEADS)]
    return wq, wkv


def qkv_fwd(proj, qg, kvg, wq, wkv, ctab, stab, l, dep=None):
    tm = QKV_TM
    deps = [] if dep is None else [dep]

    def body(cq_ref, ckvr_ref, qg_ref, kvg_ref, wq_ref, wkv_ref, c_ref, s_ref, *rest):
        q_ref, k_ref, v_ref = rest[-3:]
        wq_l, wkv_l = _qkv_weights(wq_ref, wkv_ref)
        qs, ks, vs = _qkv_tile(cq_ref[...], ckvr_ref[...], qg_ref[...], kvg_ref[...], wq_l, wkv_l, c_ref[...], s_ref[...])
        for h in range(HEADS):
            q_ref[h] = qs[h]
            k_ref[h] = ks[h]
            v_ref[h] = vs[h]

    return pl.pallas_call(
        body, grid=(T // tm,), in_specs=_qkv_in_specs() + [ANY] * len(deps),
        out_specs=[pl.BlockSpec((HEADS, tm, QK), lambda i: (0, i, 0)), pl.BlockSpec((HEADS, tm, QK), lambda i: (0, i, 0)),
                   pl.BlockSpec((HEADS, tm, 128), lambda i: (0, i, 0))],
        out_shape=[jax.ShapeDtypeStruct((HEADS, T, QK), F32), jax.ShapeDtypeStruct((HEADS, T, QK), F32),
                   jax.ShapeDtypeStruct((HEADS, T, 128), F32)],
        name=f"qkv_fwd_l{l}", compiler_params=_params(("arbitrary",)))(proj, proj, qg, kvg, wq, wkv, ctab, stab, *deps)


def qkv_bwd(proj, qg, kvg, wq, wkv, ctab, stab, dq, dk, dv, dproj, l):
    tm = QKV_TM

    def body(cq_ref, ckvr_ref, qg_ref, kvg_ref, wq_ref, wkv_ref, c_ref, s_ref, dq_ref, dk_ref, dv_ref, _,
             dseg_ref, dqg_ref, dkvg_ref, dwq_ref, dwkv_ref):
        first = pl.program_id(0) == 0
        wq_l, wkv_l = _qkv_weights(wq_ref, wkv_ref)
        c_tab, s_tab = c_ref[...], s_ref[...]
        fn = lambda cq, ckvr, qg_, kvg_, wq_, wkv_: _qkv_tile(cq, ckvr, qg_, kvg_, wq_, wkv_, c_tab, s_tab)
        _, vjp = jax.vjp(fn, cq_ref[...], ckvr_ref[...], qg_ref[...], kvg_ref[...], wq_l, wkv_l)
        cts = ([dq_ref[h] for h in range(HEADS)], [dk_ref[h] for h in range(HEADS)], [dv_ref[h] for h in range(HEADS)])
        dcq, dckvr, dqg, dkvg, dwq, dwkv = vjp(cts)
        dseg_ref[:, 0:384] = dcq.astype(BF16)
        dseg_ref[:, 384:768] = dckvr.astype(BF16)
        _acc(dqg_ref, dqg, first)
        _acc(dkvg_ref, dkvg, first)
        for h in range(HEADS):
            _acc(dwq_ref.at[h, 0:128, :], dwq[h][0], first)
            _acc(dwq_ref.at[h, 128:160, :], dwq[h][1], first)
            _acc(dwq_ref.at[h, 160:192, :], dwq[h][2], first)
            _acc(dwkv_ref.at[h, 0:128, :], dwkv[h][0], first)
            _acc(dwkv_ref.at[h, 128:256, :], dwkv[h][1], first)

    hq = pl.BlockSpec((HEADS, tm, QK), lambda i: (0, i, 0))
    return pl.pallas_call(
        body, grid=(T // tm,),
        in_specs=_qkv_in_specs() + [hq, hq, pl.BlockSpec((HEADS, tm, 128), lambda i: (0, i, 0)), ANY],
        out_specs=[pl.BlockSpec((tm, 768), lambda i: (i, OFF_CQ // 768)), pl.BlockSpec((1, 384), lambda i: (0, 0)),
                   pl.BlockSpec((1, 384), lambda i: (0, 0)), pl.BlockSpec((HEADS, 192, 384), lambda i: (0, 0, 0)),
                   pl.BlockSpec((HEADS, 256, 256), lambda i: (0, 0, 0))],
        out_shape=[jax.ShapeDtypeStruct((T, NPAD), BF16), jax.ShapeDtypeStruct((1, 384), F32),
                   jax.ShapeDtypeStruct((1, 384), F32), jax.ShapeDtypeStruct((HEADS, 192, 384), F32),
                   jax.ShapeDtypeStruct((HEADS, 256, 256), F32)],
        input_output_aliases={11: 0},
        name=f"qkv_bwd_l{l}", compiler_params=_params(("arbitrary",)))(
            proj, proj, qg, kvg, wq, wkv, ctab, stab, dq, dk, dv, dproj)


ATT_TQ_FWD = 256
ATT_TQ_BWD = 512


def _attn_tile(q, kv_past, k, v, zb):
    q = q * (1.0 / math.sqrt(QK))
    s = dot_nt(q, k)
    qc = lax.broadcasted_iota(jnp.int32, s.shape, 0) >> CHUNK_SHIFT
    kc = lax.broadcasted_iota(jnp.int32, s.shape, 1) >> CHUNK_SHIFT
    s = jnp.where(kc <= qc, s, -1e30)
    m = jnp.max(s, axis=-1, keepdims=True)
    if kv_past is not None:
        sp = dot_nt(q, kv_past[0])
        m = jnp.maximum(m, jnp.max(sp, axis=-1, keepdims=True))
    m = lax.stop_gradient(m)
    p = jnp.exp(s - m)
    denom = jnp.sum(p, axis=-1, keepdims=True)
    o = dot_nn(p, v)
    if kv_past is not None:
        pp = jnp.exp(sp - m)
        denom = denom + jnp.sum(pp, axis=-1, keepdims=True)
        o = o + dot_nn(pp, kv_past[1])
    return o * (1.0 / denom) * _silu(zb)


def _attn_operands(k_ref, v_ref, g, tq):
    n = tq * g
    past = (k_ref[0:n, :], v_ref[0:n, :]) if g else None
    return past, k_ref[n:n + tq, :], v_ref[n:n + tq, :]


def _attn_in_specs(tq):
    return [pl.BlockSpec((None, tq, QK), lambda h, i: (h, i, 0)), pl.BlockSpec((None, T, QK), lambda h, i: (h, 0, 0)),
            pl.BlockSpec((None, T, 128), lambda h, i: (h, 0, 0)),
            pl.BlockSpec((tq, 128), lambda h, i: (i, OFF_ZB // 128 + h))]


def attn_fwd(q, k, v, proj, l):
    tq = ATT_TQ_FWD

    def body(q_ref, k_ref, v_ref, z_ref, y_ref):
        for g in range(T // tq):
            @pl.when(pl.program_id(1) == g)
            def _(g=g):
                past, k, v = _attn_operands(k_ref, v_ref, g, tq)
                y_ref[...] = _attn_tile(q_ref[...], past, k, v, z_ref[...])

    return pl.pallas_call(
        body, grid=(HEADS, T // tq), in_specs=_attn_in_specs(tq),
        out_specs=pl.BlockSpec((tq, 128), lambda h, i: (i, h)),
        out_shape=jax.ShapeDtypeStruct((T, 1024), F32),
        name=f"attn_fwd_l{l}", compiler_params=_params(("arbitrary", "arbitrary")))(q, k, v, proj)


def attn_bwd(q, k, v, proj, dy, dproj, l):
    tq = ATT_TQ_BWD

    def body(q_ref, k_ref, v_ref, z_ref, dy_ref, _, dq_ref, dk_ref, dv_ref, dz_ref):
        @pl.when(pl.program_id(1) == 0)
        def _():
            dk_ref[...] = jnp.zeros_like(dk_ref)
            dv_ref[...] = jnp.zeros_like(dv_ref)

        for g in range(T // tq):
            @pl.when(pl.program_id(1) == g)
            def _(g=g):
                n = tq * g
                past, k, v = _attn_operands(k_ref, v_ref, g, tq)
                _, vjp = jax.vjp(_attn_tile, q_ref[...], past, k, v, z_ref[...])
                dq, dpast, dk, dv, dz = vjp(dy_ref[...])
                dq_ref[...] = dq
                dz_ref[...] = dz.astype(BF16)
                dk_ref[n:n + tq, :] += dk
                dv_ref[n:n + tq, :] += dv
                if g:
                    dk_ref[0:n, :] += dpast[0]
                    dv_ref[0:n, :] += dpast[1]

    return pl.pallas_call(
        body, grid=(HEADS, T // tq),
        in_specs=_attn_in_specs(tq) + [pl.BlockSpec((tq, 128), lambda h, i: (i, h)), ANY],
        out_specs=[pl.BlockSpec((None, tq, QK), lambda h, i: (h, i, 0)), pl.BlockSpec((None, T, QK), lambda h, i: (h, 0, 0)),
                   pl.BlockSpec((None, T, 128), lambda h, i: (h, 0, 0)),
                   pl.BlockSpec((tq, 128), lambda h, i: (i, OFF_ZB // 128 + h))],
        out_shape=[jax.ShapeDtypeStruct((HEADS, T, QK), F32), jax.ShapeDtypeStruct((HEADS, T, QK), F32),
                   jax.ShapeDtypeStruct((HEADS, T, 128), F32), jax.ShapeDtypeStruct((T, NPAD), BF16)],
        input_output_aliases={5: 3},
        name=f"attn_bwd_l{l}", compiler_params=_params(("arbitrary", "arbitrary")))(q, k, v, proj, dy, dproj)


LRU_TT = 256


def _lru_gates(xc, wa, wx, ba, bx, lam):
    r = _sigmoid(dot_nn(xc, wa) + ba)
    i = _sigmoid(dot_nn(xc, wx) + bx)
    sp = jnp.maximum(-lam, 0.0) + jnp.log1p(jnp.exp(-jnp.abs(lam)))
    log_a = -8.0 * r * sp
    a = jnp.exp(log_a)
    mult = jnp.sqrt(jnp.maximum(1.0 - jnp.exp(2.0 * log_a), 0.0))
    return a, mult * (i * xc)


def _shift_down(x, s, halo):
    n, c = x.shape
    r = pltpu.roll(x.reshape(n // 8, 8, c), s, 1)
    before = jnp.concatenate([pltpu.roll(halo, s, 0)[None], r[:-1]], axis=0)
    sub = lax.broadcasted_iota(jnp.int32, r.shape, 1)
    return jnp.where(sub >= s, r, before).reshape(n, c)


def _shift_up(x, s, halo):
    n, c = x.shape
    r = pltpu.roll(x.reshape(n // 8, 8, c), 8 - s, 1)
    after = jnp.concatenate([r[1:], pltpu.roll(halo, 8 - s, 0)[None]], axis=0)
    sub = lax.broadcasted_iota(jnp.int32, r.shape, 1)
    return jnp.where(sub < 8 - s, r, after).reshape(n, c)


def _conv(x, halo, w_ref, b):
    return (w_ref[3:4, :] * x + w_ref[2:3, :] * _shift_down(x, 1, halo) + w_ref[1:2, :] * _shift_down(x, 2, halo)
            + w_ref[0:1, :] * _shift_down(x, 3, halo) + b)


def _scan(a, b, reverse, carry):
    n, c = a.shape
    a, b = a.reshape(n // 8, 8, c), b.reshape(n // 8, 8, c)
    sub = lax.broadcasted_iota(jnp.int32, a.shape, 1)
    for d in (1, 2, 4):
        keep = sub < 8 - d if reverse else sub >= d
        shift = 8 - d if reverse else d
        a_sh = jnp.where(keep, pltpu.roll(a, shift, 1), 1.0)
        b_sh = jnp.where(keep, pltpu.roll(b, shift, 1), 0.0)
        b = a * b_sh + b
        a = a * a_sh
    a, b = a.reshape(n, c), b.reshape(n, c)
    groups = [None] * (n // 8)
    for g in (reversed(range(n // 8)) if reverse else range(n // 8)):
        h = a[8 * g:8 * g + 8] * carry + b[8 * g:8 * g + 8]
        groups[g] = h
        carry = h[0:1] if reverse else h[7:8]
    return jnp.concatenate(groups, axis=0), carry


def _lru_param_specs(l):
    ct = LRU_TILE
    vec = pl.BlockSpec((1, ct), lambda n, i: (0, n))
    mat = pl.BlockSpec((None, 8, 80, 80), lambda n, i: (l, n, 0, 0))
    return [pl.BlockSpec((4, ct), lambda n, i: (0, n)), vec, mat, mat, vec, vec, vec]


def _blocks_to_dense(w_ref, dense):
    dense[...] = jnp.zeros_like(dense)
    for b in range(8):
        dense[80 * b:80 * b + 80, 80 * b:80 * b + 80] = w_ref[b]


def _dense_to_blocks(dense, w_ref):
    for b in range(8):
        w_ref[b] = dense[80 * b:80 * b + 80, 80 * b:80 * b + 80]


def lru_fwd(proj, conv_w, conv_b, wa, wx, ba, bx, lam, l):
    tt, ct = LRU_TT, LRU_TILE

    def body(x_ref, z_ref, cw_ref, cb_ref, wa_ref, wx_ref, ba_ref, bx_ref, lam_ref, h_ref, y_ref, halo, hcar, wa, wx):
        @pl.when(pl.program_id(1) == 0)
        def _():
            halo[...] = jnp.zeros_like(halo)
            hcar[...] = jnp.zeros_like(hcar)
            _blocks_to_dense(wa_ref, wa)
            _blocks_to_dense(wx_ref, wx)

        x = x_ref[...]
        xc = _conv(x, halo[...], cw_ref, cb_ref[...])
        halo[...] = x[tt - 8:tt]
        a, b = _lru_gates(xc, wa[...], wx[...], ba_ref[...], bx_ref[...], lam_ref[...])
        h, hcar[...] = _scan(a, b, False, hcar[...])
        h_ref[...] = h
        y_ref[...] = h * _silu(z_ref[...])

    seq = pl.BlockSpec((tt, ct), lambda n, i: (i, n))
    return pl.pallas_call(
        body, grid=(LRU_W // ct, T // tt),
        in_specs=[pl.BlockSpec((tt, ct), lambda n, i: (i, OFF_XC // ct + n)),
                  pl.BlockSpec((tt, ct), lambda n, i: (i, OFF_ZC // ct + n))] + _lru_param_specs(l),
        out_specs=[seq, seq],
        out_shape=[jax.ShapeDtypeStruct((T, LRU_W), F32), jax.ShapeDtypeStruct((T, LRU_W), F32)],
        scratch_shapes=[pltpu.VMEM((8, ct), F32), pltpu.VMEM((1, ct), F32), pltpu.VMEM((ct, ct), F32),
                        pltpu.VMEM((ct, ct), F32)],
        name=f"lru_fwd_l{l}", compiler_params=_params(("arbitrary", "arbitrary")))(
            proj, proj, conv_w, conv_b, wa, wx, ba, bx, lam)


def lru_bwd(proj, hseq, dy, conv_w, conv_b, wa, wx, ba, bx, lam, dproj, l):
    tt, ct = LRU_TT, LRU_TILE
    nt = T // tt
    rev = lambda i: nt - 1 - i
    prev8 = lambda i: jnp.maximum(rev(i) * (tt // 8) - 1, 0)

    def body(x_ref, xh_ref, z_ref, h_ref, hh_ref, dy_ref, cw_ref, cb_ref, wa_ref, wx_ref, ba_ref, bx_ref, lam_ref, _,
             dx_ref, dcw_ref, dcb_ref, dwa_ref, dwx_ref, dba_ref, dbx_ref, dlam_ref, gcar, dhalo,
             wa, wx, dwa_acc, dwx_acc):
        i = pl.program_id(1)
        first = i == 0

        @pl.when(first)
        def _():
            gcar[...] = jnp.zeros_like(gcar)
            dhalo[...] = jnp.zeros_like(dhalo)
            _blocks_to_dense(wa_ref, wa)
            _blocks_to_dense(wx_ref, wx)

        at_start = rev(i) == 0
        x = x_ref[...]
        xhalo = jnp.where(at_start, 0.0, xh_ref[...])
        sh = [x, _shift_down(x, 1, xhalo), _shift_down(x, 2, xhalo), _shift_down(x, 3, xhalo)]
        xc = (cw_ref[3:4, :] * sh[0] + cw_ref[2:3, :] * sh[1] + cw_ref[1:2, :] * sh[2] + cw_ref[0:1, :] * sh[3]
              + cb_ref[...])
        (a, b), vjp = jax.vjp(_lru_gates, xc, wa[...], wx[...], ba_ref[...], bx_ref[...], lam_ref[...])
        hs = h_ref[...]
        hprev = _shift_down(hs, 1, jnp.where(at_start, 0.0, hh_ref[...]))
        dh = dy_ref[...] * _silu(z_ref[...])
        a_next = _shift_up(a, 1, jnp.ones((8, ct), F32))
        g, _ = _scan(a_next, dh, True, gcar[...])
        dxc, dwa, dwx, dba, dbx, dlam = vjp((g * hprev, g))
        dx = (cw_ref[3:4, :] * dxc + cw_ref[2:3, :] * _shift_up(dxc, 1, dhalo[...])
              + cw_ref[1:2, :] * _shift_up(dxc, 2, dhalo[...]) + cw_ref[0:1, :] * _shift_up(dxc, 3, dhalo[...]))
        dx_ref[...] = dx.astype(BF16)
        dhalo[...] = dxc[0:8]
        ag = a * g
        gcar[...] = ag[0:1]
        dcw = jnp.concatenate([jnp.sum(dxc * sh[3 - j], axis=0, keepdims=True) for j in range(4)], axis=0)
        _acc(dcw_ref, dcw, first)
        _acc(dcb_ref, jnp.sum(dxc, axis=0, keepdims=True), first)
        _acc(dwa_acc, dwa, first)
        _acc(dwx_acc, dwx, first)

        @pl.when(i == nt - 1)
        def _():
            _dense_to_blocks(dwa_acc, dwa_ref)
            _dense_to_blocks(dwx_acc, dwx_ref)

        _acc(dba_ref, dba, first)
        _acc(dbx_ref, dbx, first)
        _acc(dlam_ref, dlam, first)

    xcol = OFF_XC // ct
    zcol = OFF_ZC // ct
    vec = pl.BlockSpec((1, ct), lambda n, i: (0, n))
    mat = pl.BlockSpec((8, 80, 80), lambda n, i: (n, 0, 0))
    seq = pl.BlockSpec((tt, ct), lambda n, i: (rev(i), n))
    return pl.pallas_call(
        body, grid=(LRU_W // ct, nt),
        in_specs=[pl.BlockSpec((tt, ct), lambda n, i: (rev(i), xcol + n)),
                  pl.BlockSpec((8, ct), lambda n, i: (prev8(i), xcol + n)),
                  pl.BlockSpec((tt, ct), lambda n, i: (rev(i), zcol + n)),
                  seq, pl.BlockSpec((8, ct), lambda n, i: (prev8(i), n)), seq] + _lru_param_specs(l) + [ANY],
        out_specs=[pl.BlockSpec((tt, ct), lambda n, i: (rev(i), xcol + n)),
                   pl.BlockSpec((4, ct), lambda n, i: (0, n)), vec, mat, mat, vec, vec, vec],
        out_shape=[jax.ShapeDtypeStruct((T, NPAD), BF16),
                   jax.ShapeDtypeStruct((4, LRU_W), F32), jax.ShapeDtypeStruct((1, LRU_W), F32),
                   jax.ShapeDtypeStruct((16, 80, 80), F32), jax.ShapeDtypeStruct((16, 80, 80), F32),
                   jax.ShapeDtypeStruct((1, LRU_W), F32), jax.ShapeDtypeStruct((1, LRU_W), F32),
                   jax.ShapeDtypeStruct((1, LRU_W), F32)],
        scratch_shapes=[pltpu.VMEM((1, ct), F32), pltpu.VMEM((8, ct), F32)] + [pltpu.VMEM((ct, ct), F32)] * 4,
        input_output_aliases={13: 0},
        name=f"lru_bwd_l{l}", compiler_params=_params(("arbitrary", "arbitrary")))(
            proj, proj, proj, hseq, hseq, dy, conv_w, conv_b, wa, wx, ba, bx, lam, dproj)


def proj_bwd(y, dp, w, l, tag, dep=None, dproj=None, gate=None):
    tm = 512
    k = y.shape[1]
    extra = [] if dep is None else [dep]
    in_specs = [pl.BlockSpec((tm, k), lambda i: (i, 0)), pl.BlockSpec((tm, D), lambda i: (i, 0)),
                pl.BlockSpec((None, k, D // 2), lambda i: (0, 0, 0))]
    out_specs = [pl.BlockSpec((tm, k), lambda i: (i, 0)), pl.BlockSpec((None, k, D), lambda i: (0, 0, 0))]
    out_shape = [jax.ShapeDtypeStruct((T, k), F32), jax.ShapeDtypeStruct((1, k, D), F32)]
    aliases = {}
    if gate is not None:
        in_specs += [pl.BlockSpec((tm, k), lambda i: (i, 0)), pl.BlockSpec((tm, k), lambda i: (i, OFF_ZC // k))]
        extra = list(gate) + extra
    if dproj is not None:
        width = k if gate is not None else PAD2
        at = OFF_ZC if gate is not None else OFF_XC - PAD2
        aliases = {3 + len(extra): 2}
        extra = extra + [dproj]
        out_specs.append(pl.BlockSpec((tm, width), lambda i: (i, at // width)))
        out_shape.append(jax.ShapeDtypeStruct((T, NPAD), BF16))
    in_specs += [ANY] * (3 + len(extra) - len(in_specs))

    def body(y_ref, dp_ref, w_ref, *rest):
        dy_ref, dw_ref = rest[len(extra):len(extra) + 2]
        dp = dp_ref[...]
        dy = _dg(dp, _unpack(w_ref[...]), _NT)
        dy_ref[...] = dy
        _acc(dw_ref, _dg(y_ref[...], dp, _TN), pl.program_id(0) == 0)
        if gate is not None:
            z = rest[1][...]
            sg = _sigmoid(z)
            rest[len(extra) + 2][...] = (dy * rest[0][...] * (sg * (1.0 + z * (1.0 - sg)))).astype(BF16)
        elif dproj is not None:
            rest[len(extra) + 2][...] = jnp.zeros((tm, PAD2), BF16)

    return pl.pallas_call(
        body, grid=(T // tm,), in_specs=in_specs, out_specs=out_specs, out_shape=out_shape,
        input_output_aliases=aliases,
        name=f"proj_{tag}_bwd_l{l}", compiler_params=_params(("arbitrary",)))(y, dp, w, *extra)


OUT_TM = 256


def _out_tile(pa, pb, pc, ga, gb, gc, wout, post_g):
    merged = _sigmoid(ga) * pa + _sigmoid(gb) * pb + _sigmoid(gc) * pc
    return _rms(dot_nn(merged, wout), post_g)


def _out_in_specs():
    tm = OUT_TM
    tok = pl.BlockSpec((tm, D), lambda i: (i, 0))
    gate = lambda off: pl.BlockSpec((tm, 512), lambda i, off=off: (i, off // 512))
    return [tok, tok, tok, gate(OFF_GA), gate(OFF_GA + 512), gate(OFF_GB), gate(OFF_GB + 512), gate(OFF_GC),
            gate(OFF_GC + 512), pl.BlockSpec((None, D, D // 2), lambda i: (0, 0, 0)), pl.BlockSpec((1, D), lambda i: (0, 0))]


def _gates(refs):
    return [jnp.concatenate([refs[2 * j][...], refs[2 * j + 1][...]], axis=1) for j in range(3)]


def out_fwd(x, ya, yb, yc, proj, wpa, wpb, wpc, wout, post_g, l):
    tm = OUT_TM

    def body(ya_ref, yb_ref, yc_ref, g0, g1, g2, g3, g4, g5, wo_ref, pg_ref, x_ref, wa_ref, wb_ref, wc_ref,
             o_ref, pa_ref, pb_ref, pc_ref, wa, wb, wc, wo):
        @pl.when(pl.program_id(0) == 0)
        def _():
            for dst, src in ((wa, wa_ref), (wb, wb_ref), (wc, wc_ref), (wo, wo_ref)):
                dst[...] = _unpack(src[...]).astype(BF16)

        pa = _dg(ya_ref[...], wa[...], _NN)
        pb = _dg(yb_ref[...], wb[...], _NN)
        pc = _dg(yc_ref[...], wc[...], _NN)
        ga, gb, gc = _gates([g0, g1, g2, g3, g4, g5])
        o_ref[...] = x_ref[...] + _out_tile(pa, pb, pc, ga, gb, gc, wo[...], pg_ref[...])
        pa_ref[...] = pa.astype(BF16)
        pb_ref[...] = pb.astype(BF16)
        pc_ref[...] = pc.astype(BF16)

    tok = pl.BlockSpec((tm, D), lambda i: (i, 0))
    words = lambda k: pl.BlockSpec((None, k, D // 2), lambda i: (0, 0, 0))
    specs = _out_in_specs()
    specs[2] = pl.BlockSpec((tm, LRU_W), lambda i: (i, 0))
    return pl.pallas_call(
        body, grid=(T // tm,), in_specs=specs + [tok, words(D), words(D), words(LRU_W)], out_specs=[tok] * 4,
        out_shape=[jax.ShapeDtypeStruct((T, D), F32)] + [jax.ShapeDtypeStruct((T, D), BF16)] * 3,
        scratch_shapes=[pltpu.VMEM((D, D), BF16), pltpu.VMEM((D, D), BF16), pltpu.VMEM((LRU_W, D), BF16),
                        pltpu.VMEM((D, D), BF16)],
        name=f"out_fwd_l{l}", compiler_params=_params(("arbitrary",)))(
            ya, yb, yc, proj, proj, proj, proj, proj, proj, wout, post_g, x, wpa, wpb, wpc)


def out_bwd(pa, pb, pc, proj, wout, post_g, dxn, l, dep=None):
    tm = OUT_TM
    nsteps = T // tm

    def body(pa_ref, pb_ref, pc_ref, g0, g1, g2, g3, g4, g5, w_ref, pg_ref, dxn_ref, *rest):
        dpa_ref, dpb_ref, dpc_ref, dproj_ref, dw_ref, dpg_ref, gbuf, sem = rest[-8:]
        i = pl.program_id(0)
        first = i == 0
        slot = i % 2
        ga, gb, gc = _gates([g0, g1, g2, g3, g4, g5])
        _, vjp = jax.vjp(_out_tile, pa_ref[...], pb_ref[...], pc_ref[...], ga, gb, gc, _unpack(w_ref[...]), pg_ref[...])
        dpa, dpb, dpc, dga, dgb, dgc, dw, dpg = vjp(dxn_ref[...])
        dpa_ref[...] = dpa.astype(BF16)
        dpb_ref[...] = dpb.astype(BF16)
        dpc_ref[...] = dpc.astype(BF16)
        _acc(dw_ref, dw, first)
        _acc(dpg_ref, dpg, first)

        def writeback(step, s):
            rows = pl.ds(pl.multiple_of(step * tm, tm), tm)
            return pltpu.make_async_copy(gbuf.at[s], dproj_ref.at[rows, pl.ds(OFF_GA, 3072)], sem.at[s])

        gbuf[slot, :, 0:1024] = dga.astype(BF16)
        gbuf[slot, :, 1024:2048] = dgb.astype(BF16)
        gbuf[slot, :, 2048:3072] = dgc.astype(BF16)
        writeback(i, slot).start()

        @pl.when(i > 0)
        def _():
            writeback(i - 1, 1 - slot).wait()

        @pl.when(i == nsteps - 1)
        def _():
            writeback(i, slot).wait()

    tok = pl.BlockSpec((tm, D), lambda i: (i, 0))
    deps = [] if dep is None else [dep]
    return pl.pallas_call(
        body, grid=(nsteps,), in_specs=_out_in_specs() + [tok] + [ANY] * len(deps),
        out_specs=[tok, tok, tok, ANY, pl.BlockSpec((None, D, D), lambda i: (0, 0, 0)), pl.BlockSpec((1, D), lambda i: (0, 0))],
        out_shape=[jax.ShapeDtypeStruct((T, D), BF16)] * 3 + [jax.ShapeDtypeStruct((T, NPAD), BF16),
                                                            jax.ShapeDtypeStruct((1, D, D), F32), jax.ShapeDtypeStruct((1, D), F32)],
        scratch_shapes=[pltpu.VMEM((2, tm, 3072), BF16), pltpu.SemaphoreType.DMA((2,))],
        name=f"out_bwd_l{l}", compiler_params=_params(("arbitrary",)))(
            pa, pb, pc, proj, proj, proj, proj, proj, proj, wout, post_g, dxn, *deps)


def loss_head(y, target):
    tm = 256

    def body(y_ref, t_ref, loss_ref, dy_ref):
        e = y_ref[...] - t_ref[...]
        dy_ref[...] = e * (1.0 / D)
        val = 0.5 * jnp.sum(jnp.mean(e * e, axis=-1, keepdims=True), axis=0, keepdims=True)
        _acc(loss_ref, jnp.broadcast_to(val, (8, 128)), pl.program_id(0) == 0)

    tok = pl.BlockSpec((tm, D), lambda i: (i, 0))
    total, dy = pl.pallas_call(
        body, grid=(T // tm,), in_specs=[tok, tok],
        out_specs=[pl.BlockSpec((8, 128), lambda i: (0, 0)), tok],
        out_shape=[jax.ShapeDtypeStruct((8, 128), F32), jax.ShapeDtypeStruct((T, D), F32)],
        name="loss_head", compiler_params=_params(("arbitrary",)))(y, target)
    return total[0, 0], dy


def _rope_tables():
    pos = jnp.arange(T, dtype=F32)
    inv_freq = 10000.0 ** (-jnp.arange(0, 64, 2, dtype=F32) / 64)
    ang = pos[:, None] * inv_freq[None, :]
    cos, sin = jnp.cos(ang), jnp.sin(ang)
    ctab = jnp.concatenate([jnp.ones((T, 128), F32), cos, cos], axis=1)
    stab = jnp.concatenate([jnp.zeros((T, 128), F32), -sin, sin], axis=1)
    return ctab, stab


def _layer_fwd(x, l, w, gw, tabs, dep=None, mid=None):
    row = lambda a: a[l][None]
    proj, h = inproj_fwd(x, row(w["pre_norm_g"]), gw["w_in_t"], l, dep)
    ya = gmlp_fwd(proj, row(w["gm_ln_g"]), row(w["gm_ln_b"]), w["gm_ws"][l], w["gm_bs"][l][..., None], l)
    dep2 = None
    if mid is not None:
        gw, dep2 = mid(ya)
    q, k, v = qkv_fwd(proj, row(w["mla_q_norm_g"]), row(w["kv_g384"]), gw["wq"], gw["wkv"], tabs[0], tabs[1], l, dep2)
    yb = attn_fwd(q, k, v, proj, l)
    hseq, yc = lru_fwd(proj, gw["conv"], row(w["lru_conv_b"]), w["lru_w_a"], w["lru_w_x"],
                       row(w["lru_b_a"]), row(w["lru_b_x"]), row(w["lru_lambda"]), l)
    xn, pa, pb, pc = out_fwd(x, ya, yb, yc, proj, gw["w_proj_a"], gw["w_proj_b"], gw["w_proj_c"], gw["w_out"],
                             row(w["post_norm_g"]), l)
    return xn, (x, proj, h, ya, q, k, v, yb, hseq, yc, pa, pb, pc)


def _layer_bwd(dxn, l, w, gw, tabs, saved, dep=None, early=None, mid=None, late=None):
    x, proj, h, ya, q, k, v, yb, hseq, yc, pa, pb, pc = saved
    row = lambda a: a[l][None]
    g, gg = {}, {}
    dpa, dpb, dpc, dproj, gg["w_out"], dpost = out_bwd(pa, pb, pc, proj, gw["w_out"], row(w["post_norm_g"]), dxn, l, dep)
    g["post_norm_g"] = dpost[0]
    dep1 = early(dpa) if early is not None else None
    dya, gg["w_proj_a"], dproj = proj_bwd(ya, dpa, gw["w_proj_a"], l, "a", dep1, dproj)
    dyb, gg["w_proj_b"] = proj_bwd(yb, dpb, gw["w_proj_b"], l, "b")
    dyc, gg["w_proj_c"], dproj = proj_bwd(yc, dpc, gw["w_proj_c"], l, "c", None, dproj, (hseq, proj))
    dproj, dln_g, dln_b, g["gm_ws"], dbs = gmlp_bwd(proj, row(w["gm_ln_g"]), row(w["gm_ln_b"]), w["gm_ws"][l],
                                                   w["gm_bs"][l][..., None], dya, dproj, l)
    g["gm_ln_g"], g["gm_ln_b"], g["gm_bs"] = dln_g[0], dln_b[0], dbs[..., 0]
    dq, dk, dv, dproj = attn_bwd(q, k, v, proj, dyb, dproj, l)
    dproj, dqg, dkvg, dwq, dwkv = qkv_bwd(proj, row(w["mla_q_norm_g"]), row(w["kv_g384"]), gw["wq"], gw["wkv"],
                                          tabs[0], tabs[1], dq, dk, dv, dproj, l)
    gg["wq"], gg["wkv"] = dwq.reshape(1, 1536, 384), dwkv.reshape(1, 2048, 256)
    g["mla_q_norm_g"], g["mla_kv_norm_g"] = dqg[0], dkvg[0, :256]
    dproj, dcw, dcb, dwa, dwx, dba, dbx, dlam = lru_bwd(
        proj, hseq, dyc, gw["conv"], row(w["lru_conv_b"]), w["lru_w_a"], w["lru_w_x"],
        row(w["lru_b_a"]), row(w["lru_b_x"]), row(w["lru_lambda"]), dproj, l)
    gg["conv"] = jnp.pad(dcw.T, ((0, 0), (0, 124)))[None]
    g["lru_conv_b"], g["lru_b_a"], g["lru_b_x"], g["lru_lambda"] = dcb[0], dba[0], dbx[0], dlam[0]
    g["lru_w_a"], g["lru_w_x"] = dwa, dwx
    dep2 = mid(gg, dproj) if mid is not None else None
    gg["w_in_t"], dh = inproj_bwd(dproj, h, gw["w_in_t"], l, dep2)
    dep3 = late(gg["w_in_t"]) if late is not None else None
    dx, dpre = prenorm_bwd(x, row(w["pre_norm_g"]), dh, dxn, l, dep3)
    g["pre_norm_g"] = dpre[0]
    return dx, gg, g


MESH = pl.DeviceIdType.MESH
HBM = pl.BlockSpec(memory_space=pltpu.HBM)
SEM = pl.BlockSpec(memory_space=pltpu.SEMAPHORE)
EFFECT = pltpu.SideEffectType.DATAFLOW_SIDE_EFFECTING
FLIPS = ((1, 0), (0, 1), (1, 1))


def _win_off(k, s):
    g = SHARD * k + s
    return g + jnp.where(g >= PAD1_AT, PAD1, 0) + jnp.where(g >= PAD2_AT, PAD2, 0)


def _plain_off(rows):
    return lambda k, s: rows * k + s


class Spec:
    def __init__(self, rows, cols, full_rows, pieces=None, off=None, layers=1, packed=None):
        self.rows, self.cols, self.full_rows, self.layers = rows, cols, full_rows, layers
        self.pieces = pieces or ((0, rows),)
        self.off = off or _plain_off(rows)
        self.packed = cols % 256 == 0 if packed is None else packed
        self.wcols = cols // 2 if self.packed else cols

    def to_words(self, a):
        return _pack(a) if self.packed else a

    def from_words(self, p):
        return _unpack(p) if self.packed else p


def _pack(a):
    def bits(v):
        u = lax.bitcast_convert_type(v, jnp.uint32)
        return u + jnp.uint32(0x7FFF) + ((u >> 16) & jnp.uint32(1))

    words = [(bits(a[:, g:g + 128]) >> 16) | (bits(a[:, g + 128:g + 256]) & jnp.uint32(0xFFFF0000))
             for g in range(0, a.shape[-1], 256)]
    return lax.bitcast_convert_type(jnp.concatenate(words, axis=-1) if len(words) > 1 else words[0], F32)


def _unpack(p):
    w = lax.bitcast_convert_type(p, jnp.uint32)
    lo = lax.bitcast_convert_type(w << 16, F32)
    hi = lax.bitcast_convert_type(w & jnp.uint32(0xFFFF0000), F32)
    return jnp.concatenate([h[:, g:g + 128] for g in range(0, p.shape[-1], 128) for h in (lo, hi)], axis=-1)


WEIGHT_SPECS = {
    "w_in_t": Spec(SHARD, D, NPAD, WIN_PIECES, _win_off),
    "wq": Spec(192, 384, 1536),
    "wkv": Spec(256, 256, 2048),
    "conv": Spec(160, 128, 1280),
    "w_proj_a": Spec(128, D, 1024),
    "w_proj_b": Spec(128, D, 1024),
    "w_proj_c": Spec(160, D, 1280),
    "w_out": Spec(128, D, 1024),
}
REP_ROWS = 72
REP_SPEC = Spec(REP_ROWS, D, REP_ROWS * NDEV, packed=False)


def _coords():
    return lax.axis_index("x"), lax.axis_index("y"), lax.axis_index("c")


def _rows(ref, start, n):
    if not isinstance(start, int):
        start = pl.multiple_of(start, 8)
    return ref.at[:, pl.ds(start, n), :]


def _col_tile(cols):
    return 256 if cols % 256 == 0 else cols


def _n_pieces(specs):
    return sum(len(sp.pieces) for sp in specs)


def pack_place(shard, sp, layer, tag, dep=None):
    gaps = ((PAD1_AT, PAD1), (PAD2_AT + PAD1, PAD2)) if sp.off is _win_off else ()
    npc = len(sp.pieces)
    deps = [] if dep is None else [dep]

    def body(s_ref, *rest):
        words_ref, full_ref, buf, zbuf, sem = rest[-5:]
        l = 0
        x, y, c = _coords()
        me = 4 * x + 2 * y + c
        words = sp.to_words(s_ref[...])
        words_ref[...] = words
        buf[...] = words
        copies = [pltpu.make_async_copy(buf.at[pl.ds(s, n), :],
                                        full_ref.at[l, pl.ds(pl.multiple_of(sp.off(me, s), 8), n), :], sem.at[i])
                  for i, (s, n) in enumerate(sp.pieces)]
        if gaps:
            zbuf[...] = jnp.zeros_like(zbuf)
            copies += [pltpu.make_async_copy(zbuf.at[pl.ds(0, n), :], full_ref.at[l, pl.ds(at, n), :], sem.at[npc + i])
                       for i, (at, n) in enumerate(gaps)]
        for cp in copies:
            cp.start()
        for cp in copies:
            cp.wait()

    return pl.pallas_call(
        body, grid=(1,), in_specs=[pl.BlockSpec((None, sp.rows, sp.cols), lambda i: (layer, 0, 0))] + [ANY] * len(deps),
        out_specs=[pl.BlockSpec((None, sp.rows, sp.wcols), lambda i: (0, 0, 0)), ANY],
        out_shape=[jax.ShapeDtypeStruct((sp.layers, sp.rows, sp.wcols), F32),
                   jax.ShapeDtypeStruct((sp.layers, sp.full_rows, sp.wcols), F32)],
        scratch_shapes=[pltpu.VMEM((sp.rows, sp.wcols), F32), pltpu.VMEM((PAD2 if gaps else 8, sp.wcols), F32),
                        pltpu.SemaphoreType.DMA((npc + len(gaps),))],
        name=f"pack_place_{tag}", compiler_params=_params(("arbitrary",)))(shard, *deps)


def _gather_copies(srcs, bufs, specs, ssem, rsem, landing):
    x, y, c = _coords()
    me = 4 * x + 2 * y + c
    targets = [(x, y, 1 - c)] + [(x ^ fx, y ^ fy, c) for fx, fy in FLIPS]
    copies = []
    p = 0
    for src, buf, sp in zip(srcs, bufs, specs):
        for s, n in sp.pieces:
            for t, (tx, ty, tc) in enumerate(targets):
                owner = 4 * tx + 2 * ty + tc if landing else me
                copies.append(pltpu.make_async_remote_copy(_rows(src, s, n), _rows(buf, sp.off(owner, s), n),
                                                           ssem.at[4 * p + t], rsem.at[4 * p + t],
                                                           device_id=(tx, ty, tc), device_id_type=MESH))
            p += 1
    return copies


def gather_send(words, fulls, specs, tag):
    ns, npc = len(specs), _n_pieces(specs)

    def body(*refs):
        srcs, bufs, sems = refs[:ns], refs[2 * ns:3 * ns], refs[3 * ns:]
        for cp in _gather_copies(srcs, bufs, specs, *sems, False):
            cp.start()
        for cp in _gather_copies(srcs, bufs, specs, *sems, False):
            cp.wait_send()
        for cp in _gather_copies(srcs, bufs, specs, *sems, True):
            cp.wait_recv()

    return pl.pallas_call(
        body, in_specs=[ANY] * (2 * ns), out_specs=[ANY] * ns,
        out_shape=[jax.ShapeDtypeStruct(f.shape, f.dtype) for f in fulls],
        input_output_aliases={ns + i: i for i in range(ns)},
        scratch_shapes=[pltpu.SemaphoreType.DMA((4 * npc,)), pltpu.SemaphoreType.DMA((4 * npc,))],
        name=f"gather_send_{tag}", compiler_params=pltpu.CompilerParams(has_side_effects=True))(*words, *fulls)


def _in_hbm(arrays):
    return [pltpu.with_memory_space_constraint(a, pltpu.HBM) for a in arrays]


def gather_start(words, fulls, specs, dep, tag):
    ns, npc = len(specs), _n_pieces(specs)
    deps = [] if dep is None else [dep]

    def body(*refs):
        ssem, rsem = refs[2 * ns + len(deps):2 * ns + len(deps) + 2]
        for cp in _gather_copies(refs[:ns], refs[ns:2 * ns], specs, ssem, rsem, False):
            cp.start()
        refs[-1][...] = jnp.zeros_like(refs[-1])

    outs = pl.pallas_call(
        body, in_specs=[HBM] * (2 * ns) + [ANY] * len(deps),
        out_specs=[SEM, SEM] + [HBM] * (2 * ns) + [pl.BlockSpec(memory_space=pltpu.VMEM)],
        out_shape=[pltpu.SemaphoreType.DMA((4 * npc,)), pltpu.SemaphoreType.DMA((4 * npc,))]
        + [pltpu.HBM(a.shape, a.dtype) for a in list(words) + list(fulls)] + [jax.ShapeDtypeStruct((8, 128), F32)],
        input_output_aliases={i: 2 + i for i in range(2 * ns)},
        name=f"gather_start_{tag}", compiler_params=pltpu.CompilerParams(has_side_effects=EFFECT))(
            *_in_hbm(list(words) + list(fulls)), *deps)
    return outs[0], outs[1], outs[2:2 + ns], outs[2 + ns:2 + 2 * ns], outs[-1]


def gather_wait(ssem, rsem, words, fulls, specs, after, tag):
    ns = len(specs)

    def body(*refs):
        srcs, bufs, ssem, rsem = refs[:ns], refs[ns:2 * ns], refs[2 * ns], refs[2 * ns + 1]
        for cp in _gather_copies(srcs, bufs, specs, ssem, rsem, False):
            cp.wait_send()
        for cp in _gather_copies(srcs, bufs, specs, ssem, rsem, True):
            cp.wait_recv()

    outs = pl.pallas_call(
        body, in_specs=[HBM] * (2 * ns) + [SEM, SEM, ANY], out_specs=[HBM] * (2 * ns),
        out_shape=[pltpu.HBM(a.shape, a.dtype) for a in list(words) + list(fulls)],
        input_output_aliases={i: i for i in range(2 * ns)},
        name=f"gather_wait_{tag}", compiler_params=pltpu.CompilerParams(has_side_effects=EFFECT))(
            *words, *fulls, ssem, rsem, after)
    return outs[ns:]


def gather_forward(fulls, specs, tag):
    ns, npc = len(specs), _n_pieces(specs)

    def body(*refs):
        bufs = refs[ns:2 * ns]
        ssem, rsem = refs[2 * ns:]
        x, y, c = _coords()
        sibling = (x, y, 1 - c)
        waits = []
        p = 0
        for buf, sp in zip(bufs, specs):
            for s, n in sp.pieces:
                for t, (fx, fy) in enumerate(FLIPS):
                    chip = 4 * (x ^ fx) + 2 * (y ^ fy)
                    here = _rows(buf, sp.off(chip + c, s), n)
                    send = pltpu.make_async_remote_copy(here, here, ssem.at[t, p], rsem.at[t, p],
                                                        device_id=sibling, device_id_type=MESH)
                    send.start()
                    waits.append(send.wait_send)
                    there = _rows(buf, sp.off(chip + 1 - c, s), n)
                    waits.append(pltpu.make_async_remote_copy(here, there, ssem.at[t, p], rsem.at[t, p],
                                                              device_id=sibling, device_id_type=MESH).wait_recv)
                p += 1
        for w in waits:
            w()

    return pl.pallas_call(
        body, in_specs=[ANY] * ns, out_specs=[ANY] * ns,
        out_shape=[jax.ShapeDtypeStruct(f.shape, f.dtype) for f in fulls],
        input_output_aliases={i: i for i in range(ns)},
        scratch_shapes=[pltpu.SemaphoreType.DMA((3, npc)), pltpu.SemaphoreType.DMA((3, npc))],
        name=f"gather_forward_{tag}", compiler_params=pltpu.CompilerParams(has_side_effects=True))(*fulls)


def all_gather(shards, layer, specs, names, tag):
    placed = [pack_place(s, sp, layer, f"{tag}_{n}") for s, sp, n in zip(shards, specs, names)]
    fulls = gather_send([p[0] for p in placed], [p[1] for p in placed], specs, tag)
    return gather_forward(fulls, specs, tag)


def _pair_copies(srcs, theirs, specs, ssem, rsem):
    x, y, c = _coords()
    copies = []
    p = 0
    for src, their, sp in zip(srcs, theirs, specs):
        for s, n in sp.pieces:
            for j in range(4):
                copies.append(pltpu.make_async_remote_copy(_rows(src, sp.off(2 * j + 1 - c, s), n), _rows(their.at[j], s, n),
                                                           ssem.at[4 * p + j], rsem.at[4 * p + j],
                                                           device_id=(x, y, 1 - c), device_id_type=MESH))
            p += 1
    return copies


def _pair_shapes(specs):
    return [(4, sp.layers, sp.rows, sp.cols) for sp in specs]


def reduce_pair(grads, specs, tag, dep=None):
    ns, npc = len(specs), _n_pieces(specs)
    deps = [] if dep is None else [dep]

    def body(*refs):
        copies = _pair_copies(refs[:ns], refs[ns + len(deps):2 * ns + len(deps)], specs, *refs[2 * ns + len(deps):])
        for cp in copies:
            cp.start()
        for cp in copies:
            cp.wait()

    return pl.pallas_call(
        body, in_specs=[ANY] * (ns + len(deps)), out_specs=[ANY] * ns,
        out_shape=[jax.ShapeDtypeStruct(s, F32) for s in _pair_shapes(specs)],
        scratch_shapes=[pltpu.SemaphoreType.DMA((4 * npc,)), pltpu.SemaphoreType.DMA((4 * npc,))],
        name=f"reduce_pair_{tag}", compiler_params=pltpu.CompilerParams(has_side_effects=True))(*grads, *deps)


def pair_start(grads, specs, dep, tag):
    ns, npc = len(specs), _n_pieces(specs)
    slots = [lax.empty(s, F32) for s in _pair_shapes(specs)]
    deps = [] if dep is None else [dep]

    def body(*refs):
        ssem, rsem = refs[2 * ns + len(deps):2 * ns + len(deps) + 2]
        for cp in _pair_copies(refs[:ns], refs[ns:2 * ns], specs, ssem, rsem):
            cp.start()
        refs[-1][...] = jnp.zeros_like(refs[-1])

    outs = pl.pallas_call(
        body, in_specs=[HBM] * (2 * ns) + [ANY] * len(deps),
        out_specs=[SEM, SEM] + [HBM] * (2 * ns) + [pl.BlockSpec(memory_space=pltpu.VMEM)],
        out_shape=[pltpu.SemaphoreType.DMA((4 * npc,)), pltpu.SemaphoreType.DMA((4 * npc,))]
        + [pltpu.HBM(a.shape, a.dtype) for a in list(grads) + slots] + [jax.ShapeDtypeStruct((8, 128), F32)],
        input_output_aliases={i: 2 + i for i in range(2 * ns)},
        name=f"pair_start_{tag}", compiler_params=pltpu.CompilerParams(has_side_effects=EFFECT))(
            *_in_hbm(list(grads) + slots), *deps)
    return outs[0], outs[1], outs[2:2 + ns], outs[2 + ns:2 + 2 * ns], outs[-1]


def pair_wait(ssem, rsem, grads, slots, specs, after, tag):
    ns = len(specs)

    def body(*refs):
        for cp in _pair_copies(refs[:ns], refs[ns:2 * ns], specs, refs[2 * ns], refs[2 * ns + 1]):
            cp.wait_send()
            cp.wait_recv()

    outs = pl.pallas_call(
        body, in_specs=[HBM] * (2 * ns) + [SEM, SEM, ANY], out_specs=[HBM] * (2 * ns),
        out_shape=[pltpu.HBM(a.shape, a.dtype) for a in list(grads) + list(slots)],
        input_output_aliases={i: i for i in range(2 * ns)},
        name=f"pair_wait_{tag}", compiler_params=pltpu.CompilerParams(has_side_effects=EFFECT))(
            *grads, *slots, ssem, rsem, after)
    return outs[:ns], outs[ns:]


def pair_sum(g, r1, sp, tag):
    npc = len(sp.pieces)
    fetch_all = 4 * sp.rows * sp.cols * 4 <= (8 << 20)

    def body(g_ref, r_ref, own_ref, words_ref, gbuf, sem):
        l, j = pl.program_id(0), pl.program_id(1)
        x, y, c = _coords()

        def copies(chip, slot):
            return [pltpu.make_async_copy(g_ref.at[l, pl.ds(pl.multiple_of(sp.off(2 * chip + c, s), 8), n), :],
                                          gbuf.at[slot, pl.ds(s, n), :], sem.at[slot, i])
                    for i, (s, n) in enumerate(sp.pieces)]

        def fetch(chip, slot):
            for cp in copies(chip, slot):
                cp.start()

        def arrived(chip, slot):
            for cp in copies(chip, slot):
                cp.wait()

        if fetch_all:
            @pl.when(j == 0)
            def _():
                for chip in range(4):
                    fetch(chip, chip)
                for chip in range(4):
                    arrived(chip, chip)

            mine = gbuf[j]
        else:
            @pl.when(j == 0)
            def _():
                fetch(0, 0)

            @pl.when(j < 3)
            def _():
                fetch(j + 1, (j + 1) % 2)

            arrived(j, j % 2)
            mine = gbuf[j % 2]
        p = mine + r_ref[...]
        words_ref[...] = sp.to_words(p)

        @pl.when(j == 2 * x + y)
        def _():
            own_ref[...] = p

    return pl.pallas_call(
        body, grid=(sp.layers, 4),
        in_specs=[ANY, pl.BlockSpec((None, None, sp.rows, sp.cols), lambda l, j: (j, l, 0, 0))],
        out_specs=[pl.BlockSpec((None, sp.rows, sp.cols), lambda l, j: (l, 0, 0)),
                   pl.BlockSpec((None, None, sp.rows, sp.wcols), lambda l, j: (j, l, 0, 0))],
        out_shape=[jax.ShapeDtypeStruct((sp.layers, sp.rows, sp.cols), F32),
                   jax.ShapeDtypeStruct((4, sp.layers, sp.rows, sp.wcols), F32)],
        scratch_shapes=[pltpu.VMEM((4 if fetch_all else 2, sp.rows, sp.cols), F32), pltpu.SemaphoreType.DMA((4, npc))],
        name=f"pair_sum_{tag}", compiler_params=_params(("arbitrary", "arbitrary")))(g, r1)


def _chip_copies(srcs, dsts, ssem, rsem):
    x, y, c = _coords()
    copies = []
    for i, (src, dst) in enumerate(zip(srcs, dsts)):
        for t, (fx, fy) in enumerate(FLIPS):
            tx, ty = x ^ fx, y ^ fy
            copies.append(pltpu.make_async_remote_copy(src.at[2 * tx + ty], dst.at[t], ssem.at[3 * i + t], rsem.at[3 * i + t],
                                                       device_id=(tx, ty, c), device_id_type=MESH))
    return copies


def _slot_shapes(words):
    return [(3,) + w.shape[1:] for w in words]


def reduce_chips(words, specs, tag):
    ns = len(specs)

    def body(*refs):
        copies = _chip_copies(refs[:ns], refs[ns:2 * ns], *refs[2 * ns:])
        for cp in copies:
            cp.start()
        for cp in copies:
            cp.wait()

    return pl.pallas_call(
        body, in_specs=[ANY] * ns, out_specs=[ANY] * ns,
        out_shape=[jax.ShapeDtypeStruct(s, F32) for s in _slot_shapes(words)],
        scratch_shapes=[pltpu.SemaphoreType.DMA((3 * ns,)), pltpu.SemaphoreType.DMA((3 * ns,))],
        name=f"reduce_chips_{tag}", compiler_params=pltpu.CompilerParams(has_side_effects=True))(*words)


def chips_start(words, specs, tag):
    ns = len(specs)
    slots = [lax.empty(s, F32) for s in _slot_shapes(words)]

    def body(*refs):
        ssem, rsem = refs[2 * ns:2 * ns + 2]
        for cp in _chip_copies(refs[:ns], refs[ns:2 * ns], ssem, rsem):
            cp.start()
        refs[-1][...] = jnp.zeros_like(refs[-1])

    outs = pl.pallas_call(
        body, in_specs=[HBM] * (2 * ns),
        out_specs=[SEM, SEM] + [HBM] * (2 * ns) + [pl.BlockSpec(memory_space=pltpu.VMEM)],
        out_shape=[pltpu.SemaphoreType.DMA((3 * ns,)), pltpu.SemaphoreType.DMA((3 * ns,))]
        + [pltpu.HBM(a.shape, a.dtype) for a in list(words) + slots] + [jax.ShapeDtypeStruct((8, 128), F32)],
        input_output_aliases={i: 2 + i for i in range(2 * ns)},
        name=f"chips_start_{tag}", compiler_params=pltpu.CompilerParams(has_side_effects=EFFECT))(
            *_in_hbm(list(words) + slots))
    return outs[0], outs[1], outs[2:2 + ns], outs[2 + ns:2 + 2 * ns], outs[-1]


def chips_wait(ssem, rsem, words, slots, specs, after, tag):
    ns = len(specs)

    def body(*refs):
        for cp in _chip_copies(refs[:ns], refs[ns:2 * ns], refs[2 * ns], refs[2 * ns + 1]):
            cp.wait_send()
            cp.wait_recv()

    outs = pl.pallas_call(
        body, in_specs=[HBM] * (2 * ns) + [SEM, SEM, ANY], out_specs=[HBM] * (2 * ns),
        out_shape=[pltpu.HBM(a.shape, a.dtype) for a in list(words) + list(slots)],
        input_output_aliases={i: i for i in range(2 * ns)},
        name=f"chips_wait_{tag}", compiler_params=pltpu.CompilerParams(has_side_effects=EFFECT))(
            *words, *slots, ssem, rsem, after)
    return outs[ns:]


def sum_chips(own, r2, sp, tag):
    def body(own_ref, r_ref, o_ref):
        o_ref[...] = ((own_ref[...] + sp.from_words(r_ref[0])) + sp.from_words(r_ref[1])) + sp.from_words(r_ref[2])

    blk = pl.BlockSpec((None, sp.rows, sp.cols), lambda l: (l, 0, 0))
    return pl.pallas_call(
        body, grid=(sp.layers,), in_specs=[blk, pl.BlockSpec((3, None, sp.rows, sp.wcols), lambda l: (0, l, 0, 0))],
        out_specs=blk, out_shape=jax.ShapeDtypeStruct((sp.layers, sp.rows, sp.cols), F32),
        name=f"sum_chips_{tag}", compiler_params=_params(("arbitrary",)))(own, r2)


def reduce_scatter_start(grads, specs, names, dep, tag):
    theirs = reduce_pair(grads, specs, tag, dep)
    sums = [pair_sum(g, r1, sp, f"{tag}_{n}") for g, r1, sp, n in zip(grads, theirs, specs, names)]
    ssem, rsem, words, slots, token = chips_start([s[1] for s in sums], specs, tag)
    return (ssem, rsem, words, slots, [s[0] for s in sums]), token


def reduce_scatter_finish(state, after, specs, tag):
    ssem, rsem, words, slots, own = state
    return list(zip(own, chips_wait(ssem, rsem, words, slots, specs, after, tag)))


def reduce_scatter(grads, specs, names, tag, dep=None):
    theirs = reduce_pair(grads, specs, tag, dep)
    sums = [pair_sum(g, r1, sp, f"{tag}_{n}") for g, r1, sp, n in zip(grads, theirs, specs, names)]
    return list(zip([s[0] for s in sums], reduce_chips([s[1] for s in sums], specs, tag)))


def _adamw_math(w, g, m, v):
    c1 = 1.0 - ADAM_B1 ** ADAM_STEP
    c2 = 1.0 - ADAM_B2 ** ADAM_STEP
    m2 = ADAM_B1 * m + (1.0 - ADAM_B1) * g
    v2 = ADAM_B2 * v + (1.0 - ADAM_B2) * (g * g)
    return -ADAM_LR * ((m2 / c1) / (jnp.sqrt(v2 / c2) + ADAM_EPS) + ADAM_WD * w), m2, v2


def adamw(w, g, m, v, name):
    shape = w.shape
    cols = shape[-1]
    rows = math.prod(shape[:-1])
    tr = rows
    while tr * cols * 4 > (1 << 20) and tr % 16 == 0:
        tr //= 2

    def body(w_ref, g_ref, m_ref, v_ref, d_ref, nm_ref, nv_ref):
        d_ref[...], nm_ref[...], nv_ref[...] = _adamw_math(w_ref[...], g_ref[...], m_ref[...], v_ref[...])

    blk = pl.BlockSpec((tr, cols), lambda i: (i, 0))
    outs = pl.pallas_call(
        body, grid=(rows // tr,), in_specs=[blk] * 4, out_specs=[blk] * 3,
        out_shape=[jax.ShapeDtypeStruct((rows, cols), F32)] * 3,
        name=f"adamw_{name}", compiler_params=_params(("arbitrary",)))(
            *[a.reshape(rows, cols) for a in (w, g, m, v)])
    return [o.reshape(shape) for o in outs]


def adamw_layer(w, sums, m, v, sp, l, prev, dep, name):
    _, rows, cols = w.shape
    tc = _col_tile(cols)
    twc = tc // 2 if sp.packed else tc
    extra = ([] if prev is None else list(prev)) + ([] if dep is None else [dep])

    def body(w_ref, own_ref, r_ref, m_ref, v_ref, *rest):
        g_ref, d_ref, nm_ref, nv_ref = rest[-4:]
        g = ((own_ref[...] + sp.from_words(r_ref[0])) + sp.from_words(r_ref[1])) + sp.from_words(r_ref[2])
        g_ref[...] = g
        d_ref[...], nm_ref[...], nv_ref[...] = _adamw_math(w_ref[...], g, m_ref[...], v_ref[...])

    blk = pl.BlockSpec((None, rows, tc), lambda n: (l, 0, n))
    return pl.pallas_call(
        body, grid=(cols // tc,),
        in_specs=[blk, pl.BlockSpec((None, rows, tc), lambda n: (0, 0, n)),
                  pl.BlockSpec((3, None, rows, twc), lambda n: (0, 0, 0, n)), blk, blk] + [ANY] * len(extra),
        out_specs=[blk] * 4, out_shape=[jax.ShapeDtypeStruct(w.shape, F32)] * 4,
        input_output_aliases={} if prev is None else {5 + i: i for i in range(4)},
        name=f"adamw_{name}_l{l}", compiler_params=_params(("arbitrary",)))(w, sums[0], sums[1], m, v, *extra)


WEIGHTS = ("pre_norm_g", "w_in", "gm_ln_g", "gm_ln_b", "gm_ws", "gm_bs", "mla_q_norm_g", "mla_w_uq", "mla_kv_norm_g",
           "mla_w_ukv", "lru_conv_w", "lru_conv_b", "lru_w_a", "lru_b_a", "lru_w_x", "lru_b_x", "lru_lambda",
           "w_proj_a", "w_proj_b", "w_proj_c", "w_out", "post_norm_g")
SHARDED = ("w_in", "mla_w_uq", "mla_w_ukv", "lru_conv_w", "w_proj_a", "w_proj_b", "w_proj_c", "w_out")
REPLICATED = tuple(n for n in WEIGHTS if n not in SHARDED)


def _step(x, target, wts, ms, vs):
    t12 = lambda a: jnp.swapaxes(a, 1, 2)
    names = list(WEIGHT_SPECS)
    specs = [WEIGHT_SPECS[n] for n in names]
    tabs = _rope_tables()
    own = {"w_in_t": t12(wts["w_in"]), "wq": t12(wts["mla_w_uq"]), "wkv": t12(wts["mla_w_ukv"]),
           "conv": jnp.pad(t12(wts["lru_conv_w"]), ((0, 0), (0, 0), (0, 124))),
           "w_proj_a": wts["w_proj_a"], "w_proj_b": wts["w_proj_b"], "w_proj_c": wts["w_proj_c"], "w_out": wts["w_out"]}
    first, rest = ["w_in_t"], [n for n in names if n != "w_in_t"]
    sfirst, srest = [WEIGHT_SPECS[n] for n in first], [WEIGHT_SPECS[n] for n in rest]

    w = {n: wts[n] for n in REPLICATED}
    w["kv_g384"] = jnp.concatenate([wts["mla_kv_norm_g"], jnp.ones((L, 128), F32)], axis=1)

    def layer_weights(ns, words):
        gw = dict(zip(ns, words))
        gw["wq"] = gw["wq"].reshape(HEADS, 192, 384)
        gw["wkv"] = gw["wkv"].reshape(HEADS, 256, 128)
        gw["conv"] = gw["conv"][0, :, :4].T
        return gw

    place = lambda l, dep: {n: pack_place(own[n], WEIGHT_SPECS[n], l, f"w{l}_{n}", dep) for n in names}
    placed = [place(0, None)]
    words_of = lambda l, ns: [placed[l][n][0] for n in ns]
    bufs_of = lambda l, ns: [placed[l][n][1] for n in ns]
    later = {}

    ssem_a, rsem_a, wthru_a, fthru_a, token_a = gather_start(words_of(0, first), bufs_of(0, first), sfirst, None, "w0a")
    placed.append(place(1, token_a))
    win0 = gather_forward(gather_wait(ssem_a, rsem_a, wthru_a, fthru_a, sfirst, placed[1]["w_in_t"][0], "w0a"), sfirst, "w0a")
    ssem_b, rsem_b, wthru_b, fthru_b, token_b = gather_start(words_of(0, rest), bufs_of(0, rest), srest, win0[0], "w0b")

    def fwd0_mid(ya):
        rest0 = gather_forward(gather_wait(ssem_b, rsem_b, wthru_b, fthru_b, srest, ya, "w0b"), srest, "w0b")
        later["w1"] = gather_start(words_of(1, names), bufs_of(1, names), specs, rest0[0], "w1")
        later["gw0"] = layer_weights(first + rest, list(win0) + list(rest0))
        return later["gw0"], later["w1"][4]

    x1, saved0 = _layer_fwd(x, 0, w, {"w_in_t": win0[0]}, tabs, dep=token_b, mid=fwd0_mid)
    ssem1, rsem1, wthru1, fthru1, _ = later["w1"]
    words1 = gather_forward(gather_wait(ssem1, rsem1, wthru1, fthru1, specs, x1, "w1"), specs, "w1")
    gw0, gw1 = later["gw0"], layer_weights(names, words1)
    x2, saved1 = _layer_fwd(x1, 1, w, gw1, tabs)
    loss, dx2 = loss_head(x2, target)

    def bwd1_mid(gg, last):
        later["p1b"] = pair_start([gg[n] for n in rest], srest, last, "g1b")
        return later["p1b"][4]

    dx1, gg1, g1 = _layer_bwd(dx2, 1, w, gw1, tabs, saved1, mid=bwd1_mid)
    grads1b, theirs1b = pair_wait(*later["p1b"][:4], srest, dx1, "g1b")
    p1a = pair_start([gg1["w_in_t"]], sfirst, theirs1b[0], "g1a")

    def bwd0_early(last):
        grads1a, theirs1a = pair_wait(*p1a[:4], sfirst, last, "g1a")
        mine = dict(zip(first + rest, list(grads1a) + list(grads1b)))
        theirs = dict(zip(first + rest, list(theirs1a) + list(theirs1b)))
        sums = [pair_sum(mine[n], theirs[n], WEIGHT_SPECS[n], f"g1_{n}") for n in names]
        ssem, rsem, words, slots, token = chips_start([s[1] for s in sums], specs, "g1")
        later["g1"] = (ssem, rsem, words, slots, [s[0] for s in sums])
        return token

    def bwd0_mid(gg, last):
        later["g0b"], token = reduce_scatter_start([gg[n] for n in rest], srest, rest, last, "g0b")
        return token

    def bwd0_late(g_win):
        later["p0a"] = pair_start([g_win], sfirst, None, "g0a")
        return later["p0a"][4]

    dx0, gg0, g0 = _layer_bwd(dx1, 0, w, gw0, tabs, saved0, dep=p1a[4], early=bwd0_early, mid=bwd0_mid, late=bwd0_late)
    s1 = dict(zip(names, reduce_scatter_finish(later["g1"], dx0, specs, "g1")))
    s0 = dict(zip(rest, reduce_scatter_finish(later["g0b"], dx0, srest, "g0b")))

    grads0a, theirs0a = pair_wait(*later["p0a"][:4], sfirst, dx0, "g0a")
    own0a, words0a = pair_sum(grads0a[0], theirs0a[0], sfirst[0], "g0a_w_in_t")
    ssem_g, rsem_g, wthru_g, slots_g, token_g = chips_start([words0a], sfirst, "g0a")

    keys = {"w_in": "w_in_t", "mla_w_uq": "wq", "mla_w_ukv": "wkv",
            "w_proj_a": "w_proj_a", "w_proj_b": "w_proj_b", "w_proj_c": "w_proj_c", "w_out": "w_out"}
    transposed = ("w_in", "mla_w_uq", "mla_w_ukv")
    state_of = lambda n: [own[keys[n]], t12(ms[n]), t12(vs[n])] if n in transposed else [wts[n], ms[n], vs[n]]

    def update(n, l, sums, prev, dep):
        wl, ml, vl = state_of(n)
        return adamw_layer(wl, sums[keys[n]], ml, vl, WEIGHT_SPECS[keys[n]], l, prev, dep, n)

    upd = {n: update(n, 1, s1, None, token_g) for n in keys}
    for n in keys:
        if n != "w_in":
            upd[n] = update(n, 0, s0, upd[n], None)
    rep_flat = jnp.concatenate([jnp.stack([g0[n], g1[n]]).reshape(-1) for n in REPLICATED] + [loss[None]])
    rep_flat = jnp.pad(rep_flat, (0, REP_ROWS * NDEV * D - rep_flat.shape[0])).reshape(1, REP_ROWS * NDEV, D)
    rep_parts = reduce_scatter([rep_flat], [REP_SPEC], ["rep"], "rep", upd["w_out"][0])[0]
    rep_sum = sum_chips(*rep_parts, REP_SPEC, "rep")
    rep_full = all_gather([rep_sum], 0, [REP_SPEC], ["rep"], "rep")[0].reshape(-1)

    out = {}
    conv_sp = WEIGHT_SPECS["conv"]
    g_conv = t12(jnp.concatenate([sum_chips(*s0["conv"], conv_sp, "conv0"), sum_chips(*s1["conv"], conv_sp, "conv1")])[:, :, :4])
    out["lru_conv_w"] = [g_conv] + adamw(wts["lru_conv_w"], g_conv, ms["lru_conv_w"], vs["lru_conv_w"], "lru_conv_w")
    at = 0
    for n in REPLICATED:
        size = math.prod(wts[n].shape)
        g = rep_full[at:at + size].reshape(wts[n].shape)
        out[n] = [g] + adamw(wts[n], g, ms[n], vs[n], n)
        at += size

    landed = chips_wait(ssem_g, rsem_g, wthru_g, slots_g, sfirst, out[REPLICATED[-1]][1], "g0a")
    s0["w_in_t"] = (own0a, landed[0])
    upd["w_in"] = update("w_in", 0, s0, upd["w_in"], None)
    out.update({n: [t12(r) for r in upd[n]] if n in transposed else upd[n] for n in keys})

    return (rep_full[at], dx0[None], *[out[n][k] for k in range(4) for n in WEIGHTS])


def kernel(x, pre_norm_g, w_in, gm_ln_g, gm_ln_b, gm_ws, gm_bs, mla_q_norm_g, mla_w_uq, mla_kv_norm_g, mla_w_ukv, lru_conv_w, lru_conv_b, lru_w_a, lru_b_a, lru_w_x, lru_b_x, lru_lambda, w_proj_a, w_proj_b, w_proj_c, w_out, post_norm_g, loss_target, m_pre_norm_g, m_w_in, m_gm_ln_g, m_gm_ln_b, m_gm_ws, m_gm_bs, m_mla_q_norm_g, m_mla_w_uq, m_mla_kv_norm_g, m_mla_w_ukv, m_lru_conv_w, m_lru_conv_b, m_lru_w_a, m_lru_b_a, m_lru_w_x, m_lru_b_x, m_lru_lambda, m_w_proj_a, m_w_proj_b, m_w_proj_c, m_w_out, m_post_norm_g, v_pre_norm_g, v_w_in, v_gm_ln_g, v_gm_ln_b, v_gm_ws, v_gm_bs, v_mla_q_norm_g, v_mla_w_uq, v_mla_kv_norm_g, v_mla_w_ukv, v_lru_conv_w, v_lru_conv_b, v_lru_w_a, v_lru_b_a, v_lru_w_x, v_lru_b_x, v_lru_lambda, v_w_proj_a, v_w_proj_b, v_w_proj_c, v_w_out, v_post_norm_g):
    wts = dict(zip(WEIGHTS, (pre_norm_g, w_in, gm_ln_g, gm_ln_b, gm_ws, gm_bs, mla_q_norm_g, mla_w_uq, mla_kv_norm_g,
                             mla_w_ukv, lru_conv_w, lru_conv_b, lru_w_a, lru_b_a, lru_w_x, lru_b_x, lru_lambda,
                             w_proj_a, w_proj_b, w_proj_c, w_out, post_norm_g)))
    ms = dict(zip(WEIGHTS, (m_pre_norm_g, m_w_in, m_gm_ln_g, m_gm_ln_b, m_gm_ws, m_gm_bs, m_mla_q_norm_g, m_mla_w_uq,
                            m_mla_kv_norm_g, m_mla_w_ukv, m_lru_conv_w, m_lru_conv_b, m_lru_w_a, m_lru_b_a, m_lru_w_x,
                            m_lru_b_x, m_lru_lambda, m_w_proj_a, m_w_proj_b, m_w_proj_c, m_w_out, m_post_norm_g)))
    vs = dict(zip(WEIGHTS, (v_pre_norm_g, v_w_in, v_gm_ln_g, v_gm_ln_b, v_gm_ws, v_gm_bs, v_mla_q_norm_g, v_mla_w_uq,
                            v_mla_kv_norm_g, v_mla_w_ukv, v_lru_conv_w, v_lru_conv_b, v_lru_w_a, v_lru_b_a, v_lru_w_x,
                            v_lru_b_x, v_lru_lambda, v_w_proj_a, v_w_proj_b, v_w_proj_c, v_w_out, v_post_norm_g)))
    return _step(x[0], loss_target[0], wts, ms, vs)
```

```python
import functools
import math

import jax
import jax.numpy as jnp
from jax import lax
from jax.experimental import pallas as pl
from jax.experimental.pallas import tpu as pltpu

F32 = jnp.float32
BF16 = jnp.bfloat16

T = 2048
D = 1024
L = 2
NDEV = 8
EPS = 1e-6
CHUNK_SHIFT = 6
HEADS = 8
QK = 192
LRU_W = 1280
LRU_TILE = 640
N_IN = 10432
SHARD = N_IN // NDEV
OFF_U, OFF_V, OFF_ZA, OFF_CQ, OFF_CKV, OFF_ZB = 0, 1024, 2048, 3072, 3456, 3840
OFF_XC, OFF_ZC, OFF_GA, OFF_GB, OFF_GC = 5120, 6400, 7680, 8704, 9728
NPAD = 10752
PAD1_AT, PAD1 = 3776, 64
PAD2_AT, PAD2 = 4800, 256
WIN_PIECES = ((0, 888), (888, 280), (1168, 136))
VMEM_LIMIT = 60 * 1024 * 1024

ADAM_LR, ADAM_B1, ADAM_B2, ADAM_EPS, ADAM_WD, ADAM_STEP = 0.001, 0.9, 0.999, 1e-08, 0.01, 10

_NN = (((1,), (0,)), ((), ()))
_NT = (((1,), (1,)), ((), ()))
_TN = (((0,), (0,)), ((), ()))


def _dg(a, b, dims):
    return lax.dot_general(a.astype(BF16), b.astype(BF16), dims, preferred_element_type=F32)


@jax.custom_vjp
def dot_nn(a, b):
    return _dg(a, b, _NN)


def _nn_fwd(a, b):
    return _dg(a, b, _NN), (a, b)


def _nn_bwd(res, g):
    a, b = res
    return _dg(g, b, _NT).astype(a.dtype), _dg(a, g, _TN).astype(b.dtype)


dot_nn.defvjp(_nn_fwd, _nn_bwd)


@jax.custom_vjp
def dot_nt(a, b):
    return _dg(a, b, _NT)


def _nt_fwd(a, b):
    return _dg(a, b, _NT), (a, b)


def _nt_bwd(res, g):
    a, b = res
    return _dg(g, b, _NN).astype(a.dtype), _dg(g, a, _TN).astype(b.dtype)


dot_nt.defvjp(_nt_fwd, _nt_bwd)


def _params(sem=None):
    return pltpu.CompilerParams(dimension_semantics=sem, vmem_limit_bytes=VMEM_LIMIT)


def _sigmoid(x):
    return 1.0 / (1.0 + jnp.exp(-x))


def _silu(x):
    return x * _sigmoid(x)


def _rms(x, g):
    ms = jnp.mean(x * x, axis=-1, keepdims=True)
    return x * lax.rsqrt(ms + EPS) * g


def _acc(ref, val, first):
    @pl.when(first)
    def _():
        ref[...] = val

    @pl.when(jnp.logical_not(first))
    def _():
        ref[...] += val


ANY = pl.BlockSpec(memory_space=pl.ANY)


INPROJ_TN = 768


def inproj_fwd(x, g, wt, l, dep=None):
    tn = INPROJ_TN

    def body(x_ref, g_ref, w_ref, *rest):
        proj_ref, h_ref = rest[-2:]

        @pl.when(pl.program_id(0) == 0)
        def _():
            h_ref[...] = _rms(x_ref[...], g_ref[...]).astype(BF16)

        proj_ref[...] = lax.dot_general(h_ref[...], _unpack(w_ref[...]).astype(BF16), _NT, preferred_element_type=F32)

    deps = [] if dep is None else [dep]
    return pl.pallas_call(
        body, grid=(NPAD // tn,),
        in_specs=[pl.BlockSpec((T, D), lambda j: (0, 0)), pl.BlockSpec((1, D), lambda j: (0, 0)),
                  pl.BlockSpec((None, tn, D // 2), lambda j: (0, j, 0))] + [ANY] * len(deps),
        out_specs=[pl.BlockSpec((T, tn), lambda j: (0, j)), pl.BlockSpec((T, D), lambda j: (0, 0))],
        out_shape=[jax.ShapeDtypeStruct((T, NPAD), F32), jax.ShapeDtypeStruct((T, D), BF16)],
        name=f"inproj_fwd_l{l}", compiler_params=_params(("arbitrary",)))(x, g, wt, *deps)


def inproj_bwd(dproj, h, wt, l, dep=None):
    tn = INPROJ_TN
    deps = [] if dep is None else [dep]

    def body(dp_ref, h_ref, w_ref, *rest):
        dwt_ref, dh_ref = rest[-2:]
        dp = dp_ref[...]
        dwt_ref[...] = lax.dot_general(dp, h_ref[...], _TN, preferred_element_type=F32)
        contrib = lax.dot_general(dp, _unpack(w_ref[...]).astype(BF16), _NN, preferred_element_type=F32)
        _acc(dh_ref, contrib, pl.program_id(0) == 0)

    return pl.pallas_call(
        body, grid=(NPAD // tn,),
        in_specs=[pl.BlockSpec((T, tn), lambda j: (0, j)), pl.BlockSpec((T, D), lambda j: (0, 0)),
                  pl.BlockSpec((None, tn, D // 2), lambda j: (0, j, 0))] + [ANY] * len(deps),
        out_specs=[pl.BlockSpec((None, tn, D), lambda j: (0, j, 0)), pl.BlockSpec((T, D), lambda j: (0, 0))],
        out_shape=[jax.ShapeDtypeStruct((1, NPAD, D), F32), jax.ShapeDtypeStruct((T, D), F32)],
        name=f"inproj_bwd_l{l}", compiler_params=_params(("arbitrary",)))(dproj, h, wt, *deps)


def prenorm_bwd(x, g, dh, dxn, l, dep=None):
    tm = 512
    deps = [] if dep is None else [dep]

    def body(x_ref, g_ref, dh_ref, dxn_ref, *rest):
        dx_ref, dg_ref = rest[-2:]
        _, vjp = jax.vjp(_rms, x_ref[...], g_ref[...])
        dx, dg = vjp(dh_ref[...])
        dx_ref[...] = dx + dxn_ref[...]
        _acc(dg_ref, dg, pl.program_id(0) == 0)

    tok = pl.BlockSpec((tm, D), lambda i: (i, 0))
    vec = pl.BlockSpec((1, D), lambda i: (0, 0))
    return pl.pallas_call(
        body, grid=(T // tm,), in_specs=[tok, vec, tok, tok] + [ANY] * len(deps), out_specs=[tok, vec],
        out_shape=[jax.ShapeDtypeStruct((T, D), F32), jax.ShapeDtypeStruct((1, D), F32)],
        name=f"prenorm_bwd_l{l}", compiler_params=_params(("arbitrary",)))(x, g, dh, dxn, *deps)


def _gmlp_tile(u, v, z, ln_g, ln_b, ws, bs):
    mu = jnp.mean(v, axis=-1, keepdims=True)
    vc = v - mu
    var = jnp.mean(vc * vc, axis=-1, keepdims=True)
    vn = vc * lax.rsqrt(var + EPS) * ln_g + ln_b
    qi = lax.broadcasted_iota(jnp.int32, (128, 128), 0) >> CHUNK_SHIFT
    kj = lax.broadcasted_iota(jnp.int32, (128, 128), 1) >> CHUNK_SHIFT
    mask = kj <= qi
    outs = []
    for g in range(4):
        wm = jnp.where(mask, ws[g], 0.0)
        outs.append(dot_nn(wm, vn[:, 256 * g:256 * (g + 1)]) + bs[g])
    sv = jnp.concatenate(outs, axis=1)
    return u * sv * _silu(z)


GMLP_ROWS = 256


def _gmlp_specs():
    blk = lambda c: pl.BlockSpec((GMLP_ROWS, 1024), lambda n, c=c: (n, c))
    vec = pl.BlockSpec((1, 1024), lambda n: (0, 0))
    return [blk(0), blk(1), blk(2), vec, vec,
            pl.BlockSpec((4, 128, 128), lambda n: (0, 0, 0)), pl.BlockSpec((4, 128, 1), lambda n: (0, 0, 0))]


def gmlp_fwd(proj, ln_g, ln_b, ws, bs, l):
    def body(u_ref, v_ref, z_ref, g_ref, b_ref, ws_ref, bs_ref, y_ref):
        for r in range(0, GMLP_ROWS, 128):
            rows = slice(r, r + 128)
            y_ref[rows, :] = _gmlp_tile(u_ref[rows, :], v_ref[rows, :], z_ref[rows, :], g_ref[...], b_ref[...],
                                        [ws_ref[g] for g in range(4)], [bs_ref[g] for g in range(4)])

    return pl.pallas_call(
        body, grid=(T // GMLP_ROWS,), in_specs=_gmlp_specs(),
        out_specs=pl.BlockSpec((GMLP_ROWS, 1024), lambda n: (n, 0)),
        out_shape=jax.ShapeDtypeStruct((T, 1024), F32),
        name=f"gmlp_fwd_l{l}", compiler_params=_params(("arbitrary",)))(proj, proj, proj, ln_g, ln_b, ws, bs)


def gmlp_bwd(proj, ln_g, ln_b, ws, bs, dy, dproj, l):
    def body(u_ref, v_ref, z_ref, g_ref, b_ref, ws_ref, bs_ref, dy_ref, _, dseg_ref, dg_ref, db_ref, dws_ref, dbs_ref):
        for r in range(0, GMLP_ROWS, 128):
            rows = slice(r, r + 128)
            first = jnp.logical_and(pl.program_id(0) == 0, r == 0)
            _, vjp = jax.vjp(_gmlp_tile, u_ref[rows, :], v_ref[rows, :], z_ref[rows, :], g_ref[...], b_ref[...],
                             [ws_ref[g] for g in range(4)], [bs_ref[g] for g in range(4)])
            du, dv, dz, dg, db, dws, dbs = vjp(dy_ref[rows, :])
            dseg_ref[rows, 0:1024] = du.astype(BF16)
            dseg_ref[rows, 1024:2048] = dv.astype(BF16)
            dseg_ref[rows, 2048:3072] = dz.astype(BF16)
            _acc(dg_ref, dg, first)
            _acc(db_ref, db, first)
            for g in range(4):
                _acc(dws_ref.at[g], dws[g], first)
                _acc(dbs_ref.at[g], dbs[g], first)

    vec = pl.BlockSpec((1, 1024), lambda n: (0, 0))
    return pl.pallas_call(
        body, grid=(T // GMLP_ROWS,),
        in_specs=_gmlp_specs() + [pl.BlockSpec((GMLP_ROWS, 1024), lambda n: (n, 0)), ANY],
        out_specs=[pl.BlockSpec((GMLP_ROWS, 3072), lambda n: (n, OFF_U // 3072)), vec, vec,
                   pl.BlockSpec((4, 128, 128), lambda n: (0, 0, 0)), pl.BlockSpec((4, 128, 1), lambda n: (0, 0, 0))],
        out_shape=[jax.ShapeDtypeStruct((T, NPAD), BF16), jax.ShapeDtypeStruct((1, 1024), F32),
                   jax.ShapeDtypeStruct((1, 1024), F32), jax.ShapeDtypeStruct((4, 128, 128), F32),
                   jax.ShapeDtypeStruct((4, 128, 1), F32)],
        input_output_aliases={8: 0},
        name=f"gmlp_bwd_l{l}", compiler_params=_params(("arbitrary",)))(proj, proj, proj, ln_g, ln_b, ws, bs, dy, dproj)


QKV_TM = 512


def _qkv_tile(cq, ckvr, qg, kvg, wq, wkv, ctab, stab):
    tm = cq.shape[0]
    cqn = _rms(cq, qg)
    lane = lax.broadcasted_iota(jnp.int32, ckvr.shape, 1)
    iskv = lane < 256
    ms = jnp.sum(jnp.where(iskv, ckvr * ckvr, 0.0), axis=-1, keepdims=True) * (1.0 / 256)
    lm = jnp.where(iskv, ckvr * lax.rsqrt(ms + EPS) * kvg, ckvr)
    r = lax.broadcasted_iota(jnp.int32, (64, 128), 0)
    c = lax.broadcasted_iota(jnp.int32, (64, 128), 1)
    eye = jnp.where(c == r, 1.0, 0.0)
    eye_sw = jnp.where(c == ((r + 32) & 63), 1.0, 0.0)
    z64 = jnp.zeros((64, 256), F32)
    z128 = jnp.zeros((128, 128), F32)
    rk_rope = jnp.concatenate([z64, eye], axis=1)
    rk_sw = jnp.concatenate([jnp.zeros((128, 384), F32), jnp.concatenate([z64, eye_sw], axis=1)], axis=0)
    k_sw = dot_nt(lm, rk_sw) * stab
    qs, ks, vs = [], [], []
    for h in range(HEADS):
        wn, w1, w2 = wq[h]
        wk, wv = wkv[h]
        wq_h = jnp.concatenate([wn, w1, w2], axis=0)
        wq_sw = jnp.concatenate([jnp.zeros((128, 384), F32), w2, w1], axis=0)
        qs.append(dot_nt(cqn, wq_h) * ctab + dot_nt(cqn, wq_sw) * stab)
        rk_h = jnp.concatenate([jnp.concatenate([wk, z128], axis=1), rk_rope], axis=0)
        ks.append(dot_nt(lm, rk_h) * ctab + k_sw)
        vs.append(dot_nt(lm, jnp.concatenate([wv, z128], axis=1)))
    return qs, ks, vs


def _qkv_in_specs():
    tm = QKV_TM
    return [pl.BlockSpec((tm, 384), lambda i: (i, OFF_CQ // 384)), pl.BlockSpec((tm, 384), lambda i: (i, OFF_CKV // 384)),
            pl.BlockSpec((1, 384), lambda i: (0, 0)), pl.BlockSpec((1, 384), lambda i: (0, 0)),
            pl.BlockSpec((HEADS, 192, 384), lambda i: (0, 0, 0)), pl.BlockSpec((HEADS, 256, 128), lambda i: (0, 0, 0)),
            pl.BlockSpec((tm, 192), lambda i: (i, 0)), pl.BlockSpec((tm, 192), lambda i: (i, 0))]


def _qkv_weights(wq_ref, wkv_ref):
    wq = [(wq_ref[h, 0:128, :], wq_ref[h, 128:160, :], wq_ref[h, 160:192, :]) for h in range(HEADS)]
    wkv = [(_unpack(wkv_ref[h, 0:128, :]), _unpack(wkv_ref[h, 128:256, :])) for h in range(HEADS)]
    return wq, wkv


def qkv_fwd(proj, qg, kvg, wq, wkv, ctab, stab, l, dep=None):
    tm = QKV_TM
    deps = [] if dep is None else [dep]

    def body(cq_ref, ckvr_ref, qg_ref, kvg_ref, wq_ref, wkv_ref, c_ref, s_ref, *rest):
        q_ref, k_ref, v_ref = rest[-3:]
        wq_l, wkv_l = _qkv_weights(wq_ref, wkv_ref)
        qs, ks, vs = _qkv_tile(cq_ref[...], ckvr_ref[...], qg_ref[...], kvg_ref[...], wq_l, wkv_l, c_ref[...], s_ref[...])
        for h in range(HEADS):
            q_ref[h] = qs[h]
            k_ref[h] = ks[h]
            v_ref[h] = vs[h]

    return pl.pallas_call(
        body, grid=(T // tm,), in_specs=_qkv_in_specs() + [ANY] * len(deps),
        out_specs=[pl.BlockSpec((HEADS, tm, QK), lambda i: (0, i, 0)), pl.BlockSpec((HEADS, tm, QK), lambda i: (0, i, 0)),
                   pl.BlockSpec((HEADS, tm, 128), lambda i: (0, i, 0))],
        out_shape=[jax.ShapeDtypeStruct((HEADS, T, QK), F32), jax.ShapeDtypeStruct((HEADS, T, QK), F32),
                   jax.ShapeDtypeStruct((HEADS, T, 128), F32)],
        name=f"qkv_fwd_l{l}", compiler_params=_params(("arbitrary",)))(proj, proj, qg, kvg, wq, wkv, ctab, stab, *deps)


def qkv_bwd(proj, qg, kvg, wq, wkv, ctab, stab, dq, dk, dv, dproj, l):
    tm = QKV_TM

    def body(cq_ref, ckvr_ref, qg_ref, kvg_ref, wq_ref, wkv_ref, c_ref, s_ref, dq_ref, dk_ref, dv_ref, _,
             dseg_ref, dqg_ref, dkvg_ref, dwq_ref, dwkv_ref):
        first = pl.program_id(0) == 0
        wq_l, wkv_l = _qkv_weights(wq_ref, wkv_ref)
        c_tab, s_tab = c_ref[...], s_ref[...]
        fn = lambda cq, ckvr, qg_, kvg_, wq_, wkv_: _qkv_tile(cq, ckvr, qg_, kvg_, wq_, wkv_, c_tab, s_tab)
        _, vjp = jax.vjp(fn, cq_ref[...], ckvr_ref[...], qg_ref[...], kvg_ref[...], wq_l, wkv_l)
        cts = ([dq_ref[h] for h in range(HEADS)], [dk_ref[h] for h in range(HEADS)], [dv_ref[h] for h in range(HEADS)])
        dcq, dckvr, dqg, dkvg, dwq, dwkv = vjp(cts)
        dseg_ref[:, 0:384] = dcq.astype(BF16)
        dseg_ref[:, 384:768] = dckvr.astype(BF16)
        _acc(dqg_ref, dqg, first)
        _acc(dkvg_ref, dkvg, first)
        for h in range(HEADS):
            _acc(dwq_ref.at[h, 0:128, :], dwq[h][0], first)
            _acc(dwq_ref.at[h, 128:160, :], dwq[h][1], first)
            _acc(dwq_ref.at[h, 160:192, :], dwq[h][2], first)
            _acc(dwkv_ref.at[h, 0:128, :], dwkv[h][0], first)
            _acc(dwkv_ref.at[h, 128:256, :], dwkv[h][1], first)

    hq = pl.BlockSpec((HEADS, tm, QK), lambda i: (0, i, 0))
    return pl.pallas_call(
        body, grid=(T // tm,),
        in_specs=_qkv_in_specs() + [hq, hq, pl.BlockSpec((HEADS, tm, 128), lambda i: (0, i, 0)), ANY],
        out_specs=[pl.BlockSpec((tm, 768), lambda i: (i, OFF_CQ // 768)), pl.BlockSpec((1, 384), lambda i: (0, 0)),
                   pl.BlockSpec((1, 384), lambda i: (0, 0)), pl.BlockSpec((HEADS, 192, 384), lambda i: (0, 0, 0)),
                   pl.BlockSpec((HEADS, 256, 256), lambda i: (0, 0, 0))],
        out_shape=[jax.ShapeDtypeStruct((T, NPAD), BF16), jax.ShapeDtypeStruct((1, 384), F32),
                   jax.ShapeDtypeStruct((1, 384), F32), jax.ShapeDtypeStruct((HEADS, 192, 384), F32),
                   jax.ShapeDtypeStruct((HEADS, 256, 256), F32)],
        input_output_aliases={11: 0},
        name=f"qkv_bwd_l{l}", compiler_params=_params(("arbitrary",)))(
            proj, proj, qg, kvg, wq, wkv, ctab, stab, dq, dk, dv, dproj)


ATT_TQ_FWD = 256
ATT_TQ_BWD = 512


def _attn_tile(q, kv_past, k, v, zb):
    q = q * (1.0 / math.sqrt(QK))
    s = dot_nt(q, k)
    qc = lax.broadcasted_iota(jnp.int32, s.shape, 0) >> CHUNK_SHIFT
    kc = lax.broadcasted_iota(jnp.int32, s.shape, 1) >> CHUNK_SHIFT
    s = jnp.where(kc <= qc, s, -1e30)
    m = jnp.max(s, axis=-1, keepdims=True)
    if kv_past is not None:
        sp = dot_nt(q, kv_past[0])
        m = jnp.maximum(m, jnp.max(sp, axis=-1, keepdims=True))
    m = lax.stop_gradient(m)
    p = jnp.exp(s - m)
    denom = jnp.sum(p, axis=-1, keepdims=True)
    o = dot_nn(p, v)
    if kv_past is not None:
        pp = jnp.exp(sp - m)
        denom = denom + jnp.sum(pp, axis=-1, keepdims=True)
        o = o + dot_nn(pp, kv_past[1])
    return o * (1.0 / denom) * _silu(zb)


def _attn_operands(k_ref, v_ref, g, tq):
    n = tq * g
    past = (k_ref[0:n, :], v_ref[0:n, :]) if g else None
    return past, k_ref[n:n + tq, :], v_ref[n:n + tq, :]


def _attn_in_specs(tq):
    return [pl.BlockSpec((None, tq, QK), lambda h, i: (h, i, 0)), pl.BlockSpec((None, T, QK), lambda h, i: (h, 0, 0)),
            pl.BlockSpec((None, T, 128), lambda h, i: (h, 0, 0)),
            pl.BlockSpec((tq, 128), lambda h, i: (i, OFF_ZB // 128 + h))]


def attn_fwd(q, k, v, proj, l):
    tq = ATT_TQ_FWD

    def body(q_ref, k_ref, v_ref, z_ref, y_ref):
        for g in range(T // tq):
            @pl.when(pl.program_id(1) == g)
            def _(g=g):
                past, k, v = _attn_operands(k_ref, v_ref, g, tq)
                y_ref[...] = _attn_tile(q_ref[...], past, k, v, z_ref[...])

    return pl.pallas_call(
        body, grid=(HEADS, T // tq), in_specs=_attn_in_specs(tq),
        out_specs=pl.BlockSpec((tq, 128), lambda h, i: (i, h)),
        out_shape=jax.ShapeDtypeStruct((T, 1024), F32),
        name=f"attn_fwd_l{l}", compiler_params=_params(("arbitrary", "arbitrary")))(q, k, v, proj)


def attn_bwd(q, k, v, proj, dy, dproj, l):
    tq = ATT_TQ_BWD

    def body(q_ref, k_ref, v_ref, z_ref, dy_ref, _, dq_ref, dk_ref, dv_ref, dz_ref):
        @pl.when(pl.program_id(1) == 0)
        def _():
            dk_ref[...] = jnp.zeros_like(dk_ref)
            dv_ref[...] = jnp.zeros_like(dv_ref)

        for g in range(T // tq):
            @pl.when(pl.program_id(1) == g)
            def _(g=g):
                n = tq * g
                past, k, v = _attn_operands(k_ref, v_ref, g, tq)
                _, vjp = jax.vjp(_attn_tile, q_ref[...], past, k, v, z_ref[...])
                dq, dpast, dk, dv, dz = vjp(dy_ref[...])
                dq_ref[...] = dq
                dz_ref[...] = dz.astype(BF16)
                dk_ref[n:n + tq, :] += dk
                dv_ref[n:n + tq, :] += dv
                if g:
                    dk_ref[0:n, :] += dpast[0]
                    dv_ref[0:n, :] += dpast[1]

    return pl.pallas_call(
        body, grid=(HEADS, T // tq),
        in_specs=_attn_in_specs(tq) + [pl.BlockSpec((tq, 128), lambda h, i: (i, h)), ANY],
        out_specs=[pl.BlockSpec((None, tq, QK), lambda h, i: (h, i, 0)), pl.BlockSpec((None, T, QK), lambda h, i: (h, 0, 0)),
                   pl.BlockSpec((None, T, 128), lambda h, i: (h, 0, 0)),
                   pl.BlockSpec((tq, 128), lambda h, i: (i, OFF_ZB // 128 + h))],
        out_shape=[jax.ShapeDtypeStruct((HEADS, T, QK), F32), jax.ShapeDtypeStruct((HEADS, T, QK), F32),
                   jax.ShapeDtypeStruct((HEADS, T, 128), F32), jax.ShapeDtypeStruct((T, NPAD), BF16)],
        input_output_aliases={5: 3},
        name=f"attn_bwd_l{l}", compiler_params=_params(("arbitrary", "arbitrary")))(q, k, v, proj, dy, dproj)


LRU_TT = 256


def _lru_gates(xc, wa, wx, ba, bx, lam):
    r = _sigmoid(dot_nn(xc, wa) + ba)
    i = _sigmoid(dot_nn(xc, wx) + bx)
    sp = jnp.maximum(-lam, 0.0) + jnp.log1p(jnp.exp(-jnp.abs(lam)))
    log_a = -8.0 * r * sp
    a = jnp.exp(log_a)
    mult = jnp.sqrt(jnp.maximum(1.0 - jnp.exp(2.0 * log_a), 0.0))
    return a, mult * (i * xc)


def _shift_down(x, s, halo):
    n, c = x.shape
    r = pltpu.roll(x.reshape(n // 8, 8, c), s, 1)
    before = jnp.concatenate([pltpu.roll(halo, s, 0)[None], r[:-1]], axis=0)
    sub = lax.broadcasted_iota(jnp.int32, r.shape, 1)
    return jnp.where(sub >= s, r, before).reshape(n, c)


def _shift_up(x, s, halo):
    n, c = x.shape
    r = pltpu.roll(x.reshape(n // 8, 8, c), 8 - s, 1)
    after = jnp.concatenate([r[1:], pltpu.roll(halo, 8 - s, 0)[None]], axis=0)
    sub = lax.broadcasted_iota(jnp.int32, r.shape, 1)
    return jnp.where(sub < 8 - s, r, after).reshape(n, c)


def _conv(x, halo, w_ref, b):
    return (w_ref[3:4, :] * x + w_ref[2:3, :] * _shift_down(x, 1, halo) + w_ref[1:2, :] * _shift_down(x, 2, halo)
            + w_ref[0:1, :] * _shift_down(x, 3, halo) + b)


def _scan(a, b, reverse, carry):
    n, c = a.shape
    a, b = a.reshape(n // 8, 8, c), b.reshape(n // 8, 8, c)
    sub = lax.broadcasted_iota(jnp.int32, a.shape, 1)
    for d in (1, 2, 4):
        keep = sub < 8 - d if reverse else sub >= d
        shift = 8 - d if reverse else d
        a_sh = jnp.where(keep, pltpu.roll(a, shift, 1), 1.0)
        b_sh = jnp.where(keep, pltpu.roll(b, shift, 1), 0.0)
        b = a * b_sh + b
        a = a * a_sh
    a, b = a.reshape(n, c), b.reshape(n, c)
    groups = [None] * (n // 8)
    for g in (reversed(range(n // 8)) if reverse else range(n // 8)):
        h = a[8 * g:8 * g + 8] * carry + b[8 * g:8 * g + 8]
        groups[g] = h
        carry = h[0:1] if reverse else h[7:8]
    return jnp.concatenate(groups, axis=0), carry


def _lru_param_specs(l):
    ct = LRU_TILE
    vec = pl.BlockSpec((1, ct), lambda n, i: (0, n))
    mat = pl.BlockSpec((None, 8, 80, 80), lambda n, i: (l, n, 0, 0))
    return [pl.BlockSpec((4, ct), lambda n, i: (0, n)), vec, mat, mat, vec, vec, vec]


def _blocks_to_dense(w_ref, dense):
    dense[...] = jnp.zeros_like(dense)
    for b in range(8):
        dense[80 * b:80 * b + 80, 80 * b:80 * b + 80] = w_ref[b]


def _dense_to_blocks(dense, w_ref):
    for b in range(8):
        w_ref[b] = dense[80 * b:80 * b + 80, 80 * b:80 * b + 80]


def lru_fwd(proj, conv_w, conv_b, wa, wx, ba, bx, lam, l):
    tt, ct = LRU_TT, LRU_TILE

    def body(x_ref, z_ref, cw_ref, cb_ref, wa_ref, wx_ref, ba_ref, bx_ref, lam_ref, h_ref, y_ref, halo, hcar, wa, wx):
        @pl.when(pl.program_id(1) == 0)
        def _():
            halo[...] = jnp.zeros_like(halo)
            hcar[...] = jnp.zeros_like(hcar)
            _blocks_to_dense(wa_ref, wa)
            _blocks_to_dense(wx_ref, wx)

        x = x_ref[...]
        xc = _conv(x, halo[...], cw_ref, cb_ref[...])
        halo[...] = x[tt - 8:tt]
        a, b = _lru_gates(xc, wa[...], wx[...], ba_ref[...], bx_ref[...], lam_ref[...])
        h, hcar[...] = _scan(a, b, False, hcar[...])
        h_ref[...] = h
        y_ref[...] = h * _silu(z_ref[...])

    seq = pl.BlockSpec((tt, ct), lambda n, i: (i, n))
    return pl.pallas_call(
        body, grid=(LRU_W // ct, T // tt),
        in_specs=[pl.BlockSpec((tt, ct), lambda n, i: (i, OFF_XC // ct + n)),
                  pl.BlockSpec((tt, ct), lambda n, i: (i, OFF_ZC // ct + n))] + _lru_param_specs(l),
        out_specs=[seq, seq],
        out_shape=[jax.ShapeDtypeStruct((T, LRU_W), F32), jax.ShapeDtypeStruct((T, LRU_W), F32)],
        scratch_shapes=[pltpu.VMEM((8, ct), F32), pltpu.VMEM((1, ct), F32), pltpu.VMEM((ct, ct), F32),
                        pltpu.VMEM((ct, ct), F32)],
        name=f"lru_fwd_l{l}", compiler_params=_params(("arbitrary", "arbitrary")))(
            proj, proj, conv_w, conv_b, wa, wx, ba, bx, lam)


def lru_bwd(proj, hseq, dy, conv_w, conv_b, wa, wx, ba, bx, lam, dproj, l):
    tt, ct = LRU_TT, LRU_TILE
    nt = T // tt
    rev = lambda i: nt - 1 - i
    prev8 = lambda i: jnp.maximum(rev(i) * (tt // 8) - 1, 0)

    def body(x_ref, xh_ref, z_ref, h_ref, hh_ref, dy_ref, cw_ref, cb_ref, wa_ref, wx_ref, ba_ref, bx_ref, lam_ref, _,
             dx_ref, dcw_ref, dcb_ref, dwa_ref, dwx_ref, dba_ref, dbx_ref, dlam_ref, gcar, dhalo,
             wa, wx, dwa_acc, dwx_acc):
        i = pl.program_id(1)
        first = i == 0

        @pl.when(first)
        def _():
            gcar[...] = jnp.zeros_like(gcar)
            dhalo[...] = jnp.zeros_like(dhalo)
            _blocks_to_dense(wa_ref, wa)
            _blocks_to_dense(wx_ref, wx)

        at_start = rev(i) == 0
        x = x_ref[...]
        xhalo = jnp.where(at_start, 0.0, xh_ref[...])
        sh = [x, _shift_down(x, 1, xhalo), _shift_down(x, 2, xhalo), _shift_down(x, 3, xhalo)]
        xc = (cw_ref[3:4, :] * sh[0] + cw_ref[2:3, :] * sh[1] + cw_ref[1:2, :] * sh[2] + cw_ref[0:1, :] * sh[3]
              + cb_ref[...])
        (a, b), vjp = jax.vjp(_lru_gates, xc, wa[...], wx[...], ba_ref[...], bx_ref[...], lam_ref[...])
        hs = h_ref[...]
        hprev = _shift_down(hs, 1, jnp.where(at_start, 0.0, hh_ref[...]))
        dh = dy_ref[...] * _silu(z_ref[...])
        a_next = _shift_up(a, 1, jnp.ones((8, ct), F32))
        g, _ = _scan(a_next, dh, True, gcar[...])
        dxc, dwa, dwx, dba, dbx, dlam = vjp((g * hprev, g))
        dx = (cw_ref[3:4, :] * dxc + cw_ref[2:3, :] * _shift_up(dxc, 1, dhalo[...])
              + cw_ref[1:2, :] * _shift_up(dxc, 2, dhalo[...]) + cw_ref[0:1, :] * _shift_up(dxc, 3, dhalo[...]))
        dx_ref[...] = dx.astype(BF16)
        dhalo[...] = dxc[0:8]
        ag = a * g
        gcar[...] = ag[0:1]
        dcw = jnp.concatenate([jnp.sum(dxc * sh[3 - j], axis=0, keepdims=True) for j in range(4)], axis=0)
        _acc(dcw_ref, dcw, first)
        _acc(dcb_ref, jnp.sum(dxc, axis=0, keepdims=True), first)
        _acc(dwa_acc, dwa, first)
        _acc(dwx_acc, dwx, first)

        @pl.when(i == nt - 1)
        def _():
            _dense_to_blocks(dwa_acc, dwa_ref)
            _dense_to_blocks(dwx_acc, dwx_ref)

        _acc(dba_ref, dba, first)
        _acc(dbx_ref, dbx, first)
        _acc(dlam_ref, dlam, first)

    xcol = OFF_XC // ct
    zcol = OFF_ZC // ct
    vec = pl.BlockSpec((1, ct), lambda n, i: (0, n))
    mat = pl.BlockSpec((8, 80, 80), lambda n, i: (n, 0, 0))
    seq = pl.BlockSpec((tt, ct), lambda n, i: (rev(i), n))
    return pl.pallas_call(
        body, grid=(LRU_W // ct, nt),
        in_specs=[pl.BlockSpec((tt, ct), lambda n, i: (rev(i), xcol + n)),
                  pl.BlockSpec((8, ct), lambda n, i: (prev8(i), xcol + n)),
                  pl.BlockSpec((tt, ct), lambda n, i: (rev(i), zcol + n)),
                  seq, pl.BlockSpec((8, ct), lambda n, i: (prev8(i), n)), seq] + _lru_param_specs(l) + [ANY],
        out_specs=[pl.BlockSpec((tt, ct), lambda n, i: (rev(i), xcol + n)),
                   pl.BlockSpec((4, ct), lambda n, i: (0, n)), vec, mat, mat, vec, vec, vec],
        out_shape=[jax.ShapeDtypeStruct((T, NPAD), BF16),
                   jax.ShapeDtypeStruct((4, LRU_W), F32), jax.ShapeDtypeStruct((1, LRU_W), F32),
                   jax.ShapeDtypeStruct((16, 80, 80), F32), jax.ShapeDtypeStruct((16, 80, 80), F32),
                   jax.ShapeDtypeStruct((1, LRU_W), F32), jax.ShapeDtypeStruct((1, LRU_W), F32),
                   jax.ShapeDtypeStruct((1, LRU_W), F32)],
        scratch_shapes=[pltpu.VMEM((1, ct), F32), pltpu.VMEM((8, ct), F32)] + [pltpu.VMEM((ct, ct), F32)] * 4,
        input_output_aliases={13: 0},
        name=f"lru_bwd_l{l}", compiler_params=_params(("arbitrary", "arbitrary")))(
            proj, proj, proj, hseq, hseq, dy, conv_w, conv_b, wa, wx, ba, bx, lam, dproj)


def proj_bwd(y, dp, w, l, tag, dep=None, dproj=None, gate=None):
    tm = 512
    k = y.shape[1]
    extra = [] if dep is None else [dep]
    in_specs = [pl.BlockSpec((tm, k), lambda i: (i, 0)), pl.BlockSpec((tm, D), lambda i: (i, 0)),
                pl.BlockSpec((None, k, D // 2), lambda i: (0, 0, 0))]
    out_specs = [pl.BlockSpec((tm, k), lambda i: (i, 0)), pl.BlockSpec((None, k, D), lambda i: (0, 0, 0))]
    out_shape = [jax.ShapeDtypeStruct((T, k), F32), jax.ShapeDtypeStruct((1, k, D), F32)]
    aliases = {}
    if gate is not None:
        in_specs += [pl.BlockSpec((tm, k), lambda i: (i, 0)), pl.BlockSpec((tm, k), lambda i: (i, OFF_ZC // k))]
        extra = list(gate) + extra
    if dproj is not None:
        width = k if gate is not None else PAD2
        at = OFF_ZC if gate is not None else OFF_XC - PAD2
        aliases = {3 + len(extra): 2}
        extra = extra + [dproj]
        out_specs.append(pl.BlockSpec((tm, width), lambda i: (i, at // width)))
        out_shape.append(jax.ShapeDtypeStruct((T, NPAD), BF16))
    in_specs += [ANY] * (3 + len(extra) - len(in_specs))

    def body(y_ref, dp_ref, w_ref, *rest):
        dy_ref, dw_ref = rest[len(extra):len(extra) + 2]
        dp = dp_ref[...]
        dy = _dg(dp, _unpack(w_ref[...]), _NT)
        dy_ref[...] = dy
        _acc(dw_ref, _dg(y_ref[...], dp, _TN), pl.program_id(0) == 0)
        if gate is not None:
            z = rest[1][...]
            sg = _sigmoid(z)
            rest[len(extra) + 2][...] = (dy * rest[0][...] * (sg * (1.0 + z * (1.0 - sg)))).astype(BF16)
        elif dproj is not None:
            rest[len(extra) + 2][...] = jnp.zeros((tm, PAD2), BF16)

    return pl.pallas_call(
        body, grid=(T // tm,), in_specs=in_specs, out_specs=out_specs, out_shape=out_shape,
        input_output_aliases=aliases,
        name=f"proj_{tag}_bwd_l{l}", compiler_params=_params(("arbitrary",)))(y, dp, w, *extra)


OUT_TM = 256


def _out_tile(pa, pb, pc, ga, gb, gc, wout, post_g):
    merged = _sigmoid(ga) * pa + _sigmoid(gb) * pb + _sigmoid(gc) * pc
    return _rms(dot_nn(merged, wout), post_g)


def _out_in_specs():
    tm = OUT_TM
    tok = pl.BlockSpec((tm, D), lambda i: (i, 0))
    gate = lambda off: pl.BlockSpec((tm, 512), lambda i, off=off: (i, off // 512))
    return [tok, tok, tok, gate(OFF_GA), gate(OFF_GA + 512), gate(OFF_GB), gate(OFF_GB + 512), gate(OFF_GC),
            gate(OFF_GC + 512), pl.BlockSpec((None, D, D // 2), lambda i: (0, 0, 0)), pl.BlockSpec((1, D), lambda i: (0, 0))]


def _gates(refs):
    return [jnp.concatenate([refs[2 * j][...], refs[2 * j + 1][...]], axis=1) for j in range(3)]


def out_fwd(x, ya, yb, yc, proj, wpa, wpb, wpc, wout, post_g, l):
    tm = OUT_TM

    def body(ya_ref, yb_ref, yc_ref, g0, g1, g2, g3, g4, g5, wo_ref, pg_ref, x_ref, wa_ref, wb_ref, wc_ref,
             o_ref, pa_ref, pb_ref, pc_ref, wa, wb, wc, wo):
        @pl.when(pl.program_id(0) == 0)
        def _():
            for dst, src in ((wa, wa_ref), (wb, wb_ref), (wc, wc_ref), (wo, wo_ref)):
                dst[...] = _unpack(src[...]).astype(BF16)

        pa = _dg(ya_ref[...], wa[...], _NN)
        pb = _dg(yb_ref[...], wb[...], _NN)
        pc = _dg(yc_ref[...], wc[...], _NN)
        ga, gb, gc = _gates([g0, g1, g2, g3, g4, g5])
        o_ref[...] = x_ref[...] + _out_tile(pa, pb, pc, ga, gb, gc, wo[...], pg_ref[...])
        pa_ref[...] = pa.astype(BF16)
        pb_ref[...] = pb.astype(BF16)
        pc_ref[...] = pc.astype(BF16)

    tok = pl.BlockSpec((tm, D), lambda i: (i, 0))
    words = lambda k: pl.BlockSpec((None, k, D // 2), lambda i: (0, 0, 0))
    specs = _out_in_specs()
    specs[2] = pl.BlockSpec((tm, LRU_W), lambda i: (i, 0))
    return pl.pallas_call(
        body, grid=(T // tm,), in_specs=specs + [tok, words(D), words(D), words(LRU_W)], out_specs=[tok] * 4,
        out_shape=[jax.ShapeDtypeStruct((T, D), F32)] + [jax.ShapeDtypeStruct((T, D), BF16)] * 3,
        scratch_shapes=[pltpu.VMEM((D, D), BF16), pltpu.VMEM((D, D), BF16), pltpu.VMEM((LRU_W, D), BF16),
                        pltpu.VMEM((D, D), BF16)],
        name=f"out_fwd_l{l}", compiler_params=_params(("arbitrary",)))(
            ya, yb, yc, proj, proj, proj, proj, proj, proj, wout, post_g, x, wpa, wpb, wpc)


def out_bwd(pa, pb, pc, proj, wout, post_g, dxn, l, dep=None):
    tm = OUT_TM
    nsteps = T // tm

    def body(pa_ref, pb_ref, pc_ref, g0, g1, g2, g3, g4, g5, w_ref, pg_ref, dxn_ref, *rest):
        dpa_ref, dpb_ref, dpc_ref, dproj_ref, dw_ref, dpg_ref, gbuf, sem = rest[-8:]
        i = pl.program_id(0)
        first = i == 0
        slot = i % 2
        ga, gb, gc = _gates([g0, g1, g2, g3, g4, g5])
        _, vjp = jax.vjp(_out_tile, pa_ref[...], pb_ref[...], pc_ref[...], ga, gb, gc, _unpack(w_ref[...]), pg_ref[...])
        dpa, dpb, dpc, dga, dgb, dgc, dw, dpg = vjp(dxn_ref[...])
        dpa_ref[...] = dpa.astype(BF16)
        dpb_ref[...] = dpb.astype(BF16)
        dpc_ref[...] = dpc.astype(BF16)
        _acc(dw_ref, dw, first)
        _acc(dpg_ref, dpg, first)

        def writeback(step, s):
            rows = pl.ds(pl.multiple_of(step * tm, tm), tm)
            return pltpu.make_async_copy(gbuf.at[s], dproj_ref.at[rows, pl.ds(OFF_GA, 3072)], sem.at[s])

        gbuf[slot, :, 0:1024] = dga.astype(BF16)
        gbuf[slot, :, 1024:2048] = dgb.astype(BF16)
        gbuf[slot, :, 2048:3072] = dgc.astype(BF16)
        writeback(i, slot).start()

        @pl.when(i > 0)
        def _():
            writeback(i - 1, 1 - slot).wait()

        @pl.when(i == nsteps - 1)
        def _():
            writeback(i, slot).wait()

    tok = pl.BlockSpec((tm, D), lambda i: (i, 0))
    deps = [] if dep is None else [dep]
    return pl.pallas_call(
        body, grid=(nsteps,), in_specs=_out_in_specs() + [tok] + [ANY] * len(deps),
        out_specs=[tok, tok, tok, ANY, pl.BlockSpec((None, D, D), lambda i: (0, 0, 0)), pl.BlockSpec((1, D), lambda i: (0, 0))],
        out_shape=[jax.ShapeDtypeStruct((T, D), BF16)] * 3 + [jax.ShapeDtypeStruct((T, NPAD), BF16),
                                                            jax.ShapeDtypeStruct((1, D, D), F32), jax.ShapeDtypeStruct((1, D), F32)],
        scratch_shapes=[pltpu.VMEM((2, tm, 3072), BF16), pltpu.SemaphoreType.DMA((2,))],
        name=f"out_bwd_l{l}", compiler_params=_params(("arbitrary",)))(
            pa, pb, pc, proj, proj, proj, proj, proj, proj, wout, post_g, dxn, *deps)


def loss_head(y, target):
    tm = 256

    def body(y_ref, t_ref, loss_ref, dy_ref):
        e = y_ref[...] - t_ref[...]
        dy_ref[...] = e * (1.0 / D)
        val = 0.5 * jnp.sum(jnp.mean(e * e, axis=-1, keepdims=True), axis=0, keepdims=True)
        _acc(loss_ref, jnp.broadcast_to(val, (8, 128)), pl.program_id(0) == 0)

    tok = pl.BlockSpec((tm, D), lambda i: (i, 0))
    total, dy = pl.pallas_call(
        body, grid=(T // tm,), in_specs=[tok, tok],
        out_specs=[pl.BlockSpec((8, 128), lambda i: (0, 0)), tok],
        out_shape=[jax.ShapeDtypeStruct((8, 128), F32), jax.ShapeDtypeStruct((T, D), F32)],
        name="loss_head", compiler_params=_params(("arbitrary",)))(y, target)
    return total[0, 0], dy


def _rope_tables():
    pos = jnp.arange(T, dtype=F32)
    inv_freq = 10000.0 ** (-jnp.arange(0, 64, 2, dtype=F32) / 64)
    ang = pos[:, None] * inv_freq[None, :]
    cos, sin = jnp.cos(ang), jnp.sin(ang)
    ctab = jnp.concatenate([jnp.ones((T, 128), F32), cos, cos], axis=1)
    stab = jnp.concatenate([jnp.zeros((T, 128), F32), -sin, sin], axis=1)
    return ctab, stab


def _layer_fwd(x, l, w, gw, tabs, dep=None, mid=None):
    row = lambda a: a[l][None]
    proj, h = inproj_fwd(x, row(w["pre_norm_g"]), gw["w_in_t"], l, dep)
    ya = gmlp_fwd(proj, row(w["gm_ln_g"]), row(w["gm_ln_b"]), w["gm_ws"][l], w["gm_bs"][l][..., None], l)
    dep2 = None
    if mid is not None:
        gw, dep2 = mid(ya)
    q, k, v = qkv_fwd(proj, row(w["mla_q_norm_g"]), row(w["kv_g384"]), gw["wq"], gw["wkv"], tabs[0], tabs[1], l, dep2)
    yb = attn_fwd(q, k, v, proj, l)
    hseq, yc = lru_fwd(proj, gw["conv"], row(w["lru_conv_b"]), w["lru_w_a"], w["lru_w_x"],
                       row(w["lru_b_a"]), row(w["lru_b_x"]), row(w["lru_lambda"]), l)
    xn, pa, pb, pc = out_fwd(x, ya, yb, yc, proj, gw["w_proj_a"], gw["w_proj_b"], gw["w_proj_c"], gw["w_out"],
                             row(w["post_norm_g"]), l)
    return xn, (x, proj, h, ya, q, k, v, yb, hseq, yc, pa, pb, pc)


def _layer_bwd(dxn, l, w, gw, tabs, saved, dep=None, early=None, mid=None, late=None):
    x, proj, h, ya, q, k, v, yb, hseq, yc, pa, pb, pc = saved
    row = lambda a: a[l][None]
    g, gg = {}, {}
    dpa, dpb, dpc, dproj, gg["w_out"], dpost = out_bwd(pa, pb, pc, proj, gw["w_out"], row(w["post_norm_g"]), dxn, l, dep)
    g["post_norm_g"] = dpost[0]
    dep1 = early(dpa) if early is not None else None
    dya, gg["w_proj_a"], dproj = proj_bwd(ya, dpa, gw["w_proj_a"], l, "a", dep1, dproj)
    dyb, gg["w_proj_b"] = proj_bwd(yb, dpb, gw["w_proj_b"], l, "b")
    dyc, gg["w_proj_c"], dproj = proj_bwd(yc, dpc, gw["w_proj_c"], l, "c", None, dproj, (hseq, proj))
    dproj, dln_g, dln_b, g["gm_ws"], dbs = gmlp_bwd(proj, row(w["gm_ln_g"]), row(w["gm_ln_b"]), w["gm_ws"][l],
                                                   w["gm_bs"][l][..., None], dya, dproj, l)
    g["gm_ln_g"], g["gm_ln_b"], g["gm_bs"] = dln_g[0], dln_b[0], dbs[..., 0]
    dq, dk, dv, dproj = attn_bwd(q, k, v, proj, dyb, dproj, l)
    dproj, dqg, dkvg, dwq, dwkv = qkv_bwd(proj, row(w["mla_q_norm_g"]), row(w["kv_g384"]), gw["wq"], gw["wkv"],
                                          tabs[0], tabs[1], dq, dk, dv, dproj, l)
    gg["wq"], gg["wkv"] = dwq.reshape(1, 1536, 384), dwkv.reshape(1, 2048, 256)
    g["mla_q_norm_g"], g["mla_kv_norm_g"] = dqg[0], dkvg[0, :256]
    dproj, dcw, dcb, dwa, dwx, dba, dbx, dlam = lru_bwd(
        proj, hseq, dyc, gw["conv"], row(w["lru_conv_b"]), w["lru_w_a"], w["lru_w_x"],
        row(w["lru_b_a"]), row(w["lru_b_x"]), row(w["lru_lambda"]), dproj, l)
    gg["conv"] = jnp.pad(dcw.T, ((0, 0), (0, 124)))[None]
    g["lru_conv_b"], g["lru_b_a"], g["lru_b_x"], g["lru_lambda"] = dcb[0], dba[0], dbx[0], dlam[0]
    g["lru_w_a"], g["lru_w_x"] = dwa, dwx
    dep2 = mid(gg, dproj) if mid is not None else None
    gg["w_in_t"], dh = inproj_bwd(dproj, h, gw["w_in_t"], l, dep2)
    dep3 = late(gg["w_in_t"]) if late is not None else None
    dx, dpre = prenorm_bwd(x, row(w["pre_norm_g"]), dh, dxn, l, dep3)
    g["pre_norm_g"] = dpre[0]
    return dx, gg, g


MESH = pl.DeviceIdType.MESH
HBM = pl.BlockSpec(memory_space=pltpu.HBM)
SEM = pl.BlockSpec(memory_space=pltpu.SEMAPHORE)
EFFECT = pltpu.SideEffectType.DATAFLOW_SIDE_EFFECTING
FLIPS = ((1, 0), (0, 1), (1, 1))


def _win_off(k, s):
    g = SHARD * k + s
    return g + jnp.where(g >= PAD1_AT, PAD1, 0) + jnp.where(g >= PAD2_AT, PAD2, 0)


def _plain_off(rows):
    return lambda k, s: rows * k + s


class Spec:
    def __init__(self, rows, cols, full_rows, pieces=None, off=None, layers=1, packed=None):
        self.rows, self.cols, self.full_rows, self.layers = rows, cols, full_rows, layers
        self.pieces = pieces or ((0, rows),)
        self.off = off or _plain_off(rows)
        self.packed = cols % 256 == 0 if packed is None else packed
        self.wcols = cols // 2 if self.packed else cols

    def to_words(self, a):
        return _pack(a) if self.packed else a

    def from_words(self, p):
        return _unpack(p) if self.packed else p


def _pack(a):
    def bits(v):
        u = lax.bitcast_convert_type(v, jnp.uint32)
        return u + jnp.uint32(0x7FFF) + ((u >> 16) & jnp.uint32(1))

    words = [(bits(a[:, g:g + 128]) >> 16) | (bits(a[:, g + 128:g + 256]) & jnp.uint32(0xFFFF0000))
             for g in range(0, a.shape[-1], 256)]
    return lax.bitcast_convert_type(jnp.concatenate(words, axis=-1) if len(words) > 1 else words[0], F32)


def _unpack(p):
    w = lax.bitcast_convert_type(p, jnp.uint32)
    lo = lax.bitcast_convert_type(w << 16, F32)
    hi = lax.bitcast_convert_type(w & jnp.uint32(0xFFFF0000), F32)
    return jnp.concatenate([h[:, g:g + 128] for g in range(0, p.shape[-1], 128) for h in (lo, hi)], axis=-1)


WEIGHT_SPECS = {
    "w_in_t": Spec(SHARD, D, NPAD, WIN_PIECES, _win_off),
    "wq": Spec(192, 384, 1536),
    "wkv": Spec(256, 256, 2048),
    "conv": Spec(160, 128, 1280),
    "w_proj_a": Spec(128, D, 1024),
    "w_proj_b": Spec(128, D, 1024),
    "w_proj_c": Spec(160, D, 1280),
    "w_out": Spec(128, D, 1024),
}
REP_ROWS = 72
REP_SPEC = Spec(REP_ROWS, D, REP_ROWS * NDEV, packed=False)


def _coords():
    return lax.axis_index("x"), lax.axis_index("y"), lax.axis_index("c")


def _rows(ref, start, n):
    if not isinstance(start, int):
        start = pl.multiple_of(start, 8)
    return ref.at[:, pl.ds(start, n), :]


def _col_tile(cols):
    return 256 if cols % 256 == 0 else cols


def _n_pieces(specs):
    return sum(len(sp.pieces) for sp in specs)


def pack_place(shard, sp, layer, tag, dep=None):
    gaps = ((PAD1_AT, PAD1), (PAD2_AT + PAD1, PAD2)) if sp.off is _win_off else ()
    npc = len(sp.pieces)
    deps = [] if dep is None else [dep]

    def body(s_ref, *rest):
        words_ref, full_ref, buf, zbuf, sem = rest[-5:]
        l = 0
        x, y, c = _coords()
        me = 4 * x + 2 * y + c
        words = sp.to_words(s_ref[...])
        words_ref[...] = words
        buf[...] = words
        copies = [pltpu.make_async_copy(buf.at[pl.ds(s, n), :],
                                        full_ref.at[l, pl.ds(pl.multiple_of(sp.off(me, s), 8), n), :], sem.at[i])
                  for i, (s, n) in enumerate(sp.pieces)]
        if gaps:
            zbuf[...] = jnp.zeros_like(zbuf)
            copies += [pltpu.make_async_copy(zbuf.at[pl.ds(0, n), :], full_ref.at[l, pl.ds(at, n), :], sem.at[npc + i])
                       for i, (at, n) in enumerate(gaps)]
        for cp in copies:
            cp.start()
        for cp in copies:
            cp.wait()

    return pl.pallas_call(
        body, grid=(1,), in_specs=[pl.BlockSpec((None, sp.rows, sp.cols), lambda i: (layer, 0, 0))] + [ANY] * len(deps),
        out_specs=[pl.BlockSpec((None, sp.rows, sp.wcols), lambda i: (0, 0, 0)), ANY],
        out_shape=[jax.ShapeDtypeStruct((sp.layers, sp.rows, sp.wcols), F32),
                   jax.ShapeDtypeStruct((sp.layers, sp.full_rows, sp.wcols), F32)],
        scratch_shapes=[pltpu.VMEM((sp.rows, sp.wcols), F32), pltpu.VMEM((PAD2 if gaps else 8, sp.wcols), F32),
                        pltpu.SemaphoreType.DMA((npc + len(gaps),))],
        name=f"pack_place_{tag}", compiler_params=_params(("arbitrary",)))(shard, *deps)


def _gather_copies(srcs, bufs, specs, ssem, rsem, landing):
    x, y, c = _coords()
    me = 4 * x + 2 * y + c
    targets = [(x, y, 1 - c)] + [(x ^ fx, y ^ fy, c) for fx, fy in FLIPS]
    copies = []
    p = 0
    for src, buf, sp in zip(srcs, bufs, specs):
        for s, n in sp.pieces:
            for t, (tx, ty, tc) in enumerate(targets):
                owner = 4 * tx + 2 * ty + tc if landing else me
                copies.append(pltpu.make_async_remote_copy(_rows(src, s, n), _rows(buf, sp.off(owner, s), n),
                                                           ssem.at[4 * p + t], rsem.at[4 * p + t],
                                                           device_id=(tx, ty, tc), device_id_type=MESH))
            p += 1
    return copies


def gather_send(words, fulls, specs, tag):
    ns, npc = len(specs), _n_pieces(specs)

    def body(*refs):
        srcs, bufs, sems = refs[:ns], refs[2 * ns:3 * ns], refs[3 * ns:]
        for cp in _gather_copies(srcs, bufs, specs, *sems, False):
            cp.start()
        for cp in _gather_copies(srcs, bufs, specs, *sems, False):
            cp.wait_send()
        for cp in _gather_copies(srcs, bufs, specs, *sems, True):
            cp.wait_recv()

    return pl.pallas_call(
        body, in_specs=[ANY] * (2 * ns), out_specs=[ANY] * ns,
        out_shape=[jax.ShapeDtypeStruct(f.shape, f.dtype) for f in fulls],
        input_output_aliases={ns + i: i for i in range(ns)},
        scratch_shapes=[pltpu.SemaphoreType.DMA((4 * npc,)), pltpu.SemaphoreType.DMA((4 * npc,))],
        name=f"gather_send_{tag}", compiler_params=pltpu.CompilerParams(has_side_effects=True))(*words, *fulls)


def _in_hbm(arrays):
    return [pltpu.with_memory_space_constraint(a, pltpu.HBM) for a in arrays]


def gather_start(words, fulls, specs, dep, tag):
    ns, npc = len(specs), _n_pieces(specs)
    deps = [] if dep is None else [dep]

    def body(*refs):
        ssem, rsem = refs[2 * ns + len(deps):2 * ns + len(deps) + 2]
        for cp in _gather_copies(refs[:ns], refs[ns:2 * ns], specs, ssem, rsem, False):
            cp.start()
        refs[-1][...] = jnp.zeros_like(refs[-1])

    outs = pl.pallas_call(
        body, in_specs=[HBM] * (2 * ns) + [ANY] * len(deps),
        out_specs=[SEM, SEM] + [HBM] * (2 * ns) + [pl.BlockSpec(memory_space=pltpu.VMEM)],
        out_shape=[pltpu.SemaphoreType.DMA((4 * npc,)), pltpu.SemaphoreType.DMA((4 * npc,))]
        + [pltpu.HBM(a.shape, a.dtype) for a in list(words) + list(fulls)] + [jax.ShapeDtypeStruct((8, 128), F32)],
        input_output_aliases={i: 2 + i for i in range(2 * ns)},
        name=f"gather_start_{tag}", compiler_params=pltpu.CompilerParams(has_side_effects=EFFECT))(
            *_in_hbm(list(words) + list(fulls)), *deps)
    return outs[0], outs[1], outs[2:2 + ns], outs[2 + ns:2 + 2 * ns], outs[-1]


def gather_wait(ssem, rsem, words, fulls, specs, after, tag):
    ns = len(specs)

    def body(*refs):
        srcs, bufs, ssem, rsem = refs[:ns], refs[ns:2 * ns], refs[2 * ns], refs[2 * ns + 1]
        for cp in _gather_copies(srcs, bufs, specs, ssem, rsem, False):
            cp.wait_send()
        for cp in _gather_copies(srcs, bufs, specs, ssem, rsem, True):
            cp.wait_recv()

    outs = pl.pallas_call(
        body, in_specs=[HBM] * (2 * ns) + [SEM, SEM, ANY], out_specs=[HBM] * (2 * ns),
        out_shape=[pltpu.HBM(a.shape, a.dtype) for a in list(words) + list(fulls)],
        input_output_aliases={i: i for i in range(2 * ns)},
        name=f"gather_wait_{tag}", compiler_params=pltpu.CompilerParams(has_side_effects=EFFECT))(
            *words, *fulls, ssem, rsem, after)
    return outs[ns:]


def gather_forward(fulls, specs, tag):
    ns, npc = len(specs), _n_pieces(specs)

    def body(*refs):
        bufs = refs[ns:2 * ns]
        ssem, rsem = refs[2 * ns:]
        x, y, c = _coords()
        sibling = (x, y, 1 - c)
        waits = []
        p = 0
        for buf, sp in zip(bufs, specs):
            for s, n in sp.pieces:
                for t, (fx, fy) in enumerate(FLIPS):
                    chip = 4 * (x ^ fx) + 2 * (y ^ fy)
                    here = _rows(buf, sp.off(chip + c, s), n)
                    send = pltpu.make_async_remote_copy(here, here, ssem.at[t, p], rsem.at[t, p],
                                                        device_id=sibling, device_id_type=MESH)
                    send.start()
                    waits.append(send.wait_send)
                    there = _rows(buf, sp.off(chip + 1 - c, s), n)
                    waits.append(pltpu.make_async_remote_copy(here, there, ssem.at[t, p], rsem.at[t, p],
                                                              device_id=sibling, device_id_type=MESH).wait_recv)
                p += 1
        for w in waits:
            w()

    return pl.pallas_call(
        body, in_specs=[ANY] * ns, out_specs=[ANY] * ns,
        out_shape=[jax.ShapeDtypeStruct(f.shape, f.dtype) for f in fulls],
        input_output_aliases={i: i for i in range(ns)},
        scratch_shapes=[pltpu.SemaphoreType.DMA((3, npc)), pltpu.SemaphoreType.DMA((3, npc))],
        name=f"gather_forward_{tag}", compiler_params=pltpu.CompilerParams(has_side_effects=True))(*fulls)


def all_gather(shards, layer, specs, names, tag):
    placed = [pack_place(s, sp, layer, f"{tag}_{n}") for s, sp, n in zip(shards, specs, names)]
    fulls = gather_send([p[0] for p in placed], [p[1] for p in placed], specs, tag)
    return gather_forward(fulls, specs, tag)


def _pair_copies(srcs, theirs, specs, ssem, rsem):
    x, y, c = _coords()
    copies = []
    p = 0
    for src, their, sp in zip(srcs, theirs, specs):
        for s, n in sp.pieces:
            for j in range(4):
                copies.append(pltpu.make_async_remote_copy(_rows(src, sp.off(2 * j + 1 - c, s), n), _rows(their.at[j], s, n),
                                                           ssem.at[4 * p + j], rsem.at[4 * p + j],
                                                           device_id=(x, y, 1 - c), device_id_type=MESH))
            p += 1
    return copies


def _pair_shapes(specs):
    return [(4, sp.layers, sp.rows, sp.cols) for sp in specs]


def reduce_pair(grads, specs, tag, dep=None):
    ns, npc = len(specs), _n_pieces(specs)
    deps = [] if dep is None else [dep]

    def body(*refs):
        copies = _pair_copies(refs[:ns], refs[ns + len(deps):2 * ns + len(deps)], specs, *refs[2 * ns + len(deps):])
        for cp in copies:
            cp.start()
        for cp in copies:
            cp.wait()

    return pl.pallas_call(
        body, in_specs=[ANY] * (ns + len(deps)), out_specs=[ANY] * ns,
        out_shape=[jax.ShapeDtypeStruct(s, F32) for s in _pair_shapes(specs)],
        scratch_shapes=[pltpu.SemaphoreType.DMA((4 * npc,)), pltpu.SemaphoreType.DMA((4 * npc,))],
        name=f"reduce_pair_{tag}", compiler_params=pltpu.CompilerParams(has_side_effects=True))(*grads, *deps)


def pair_start(grads, specs, dep, tag):
    ns, npc = len(specs), _n_pieces(specs)
    slots = [lax.empty(s, F32) for s in _pair_shapes(specs)]
    deps = [] if dep is None else [dep]

    def body(*refs):
        ssem, rsem = refs[2 * ns + len(deps):2 * ns + len(deps) + 2]
        for cp in _pair_copies(refs[:ns], refs[ns:2 * ns], specs, ssem, rsem):
            cp.start()
        refs[-1][...] = jnp.zeros_like(refs[-1])

    outs = pl.pallas_call(
        body, in_specs=[HBM] * (2 * ns) + [ANY] * len(deps),
        out_specs=[SEM, SEM] + [HBM] * (2 * ns) + [pl.BlockSpec(memory_space=pltpu.VMEM)],
        out_shape=[pltpu.SemaphoreType.DMA((4 * npc,)), pltpu.SemaphoreType.DMA((4 * npc,))]
        + [pltpu.HBM(a.shape, a.dtype) for a in list(grads) + slots] + [jax.ShapeDtypeStruct((8, 128), F32)],
        input_output_aliases={i: 2 + i for i in range(2 * ns)},
        name=f"pair_start_{tag}", compiler_params=pltpu.CompilerParams(has_side_effects=EFFECT))(
            *_in_hbm(list(grads) + slots), *deps)
    return outs[0], outs[1], outs[2:2 + ns], outs[2 + ns:2 + 2 * ns], outs[-1]


def pair_wait(ssem, rsem, grads, slots, specs, after, tag):
    ns = len(specs)

    def body(*refs):
        for cp in _pair_copies(refs[:ns], refs[ns:2 * ns], specs, refs[2 * ns], refs[2 * ns + 1]):
            cp.wait_send()
            cp.wait_recv()

    outs = pl.pallas_call(
        body, in_specs=[HBM] * (2 * ns) + [SEM, SEM, ANY], out_specs=[HBM] * (2 * ns),
        out_shape=[pltpu.HBM(a.shape, a.dtype) for a in list(grads) + list(slots)],
        input_output_aliases={i: i for i in range(2 * ns)},
        name=f"pair_wait_{tag}", compiler_params=pltpu.CompilerParams(has_side_effects=EFFECT))(
            *grads, *slots, ssem, rsem, after)
    return outs[:ns], outs[ns:]


def pair_sum(g, r1, sp, tag):
    npc = len(sp.pieces)
    fetch_all = 4 * sp.rows * sp.cols * 4 <= (8 << 20)

    def body(g_ref, r_ref, own_ref, words_ref, gbuf, sem):
        l, j = pl.program_id(0), pl.program_id(1)
        x, y, c = _coords()

        def copies(chip, slot):
            return [pltpu.make_async_copy(g_ref.at[l, pl.ds(pl.multiple_of(sp.off(2 * chip + c, s), 8), n), :],
                                          gbuf.at[slot, pl.ds(s, n), :], sem.at[slot, i])
                    for i, (s, n) in enumerate(sp.pieces)]

        def fetch(chip, slot):
            for cp in copies(chip, slot):
                cp.start()

        def arrived(chip, slot):
            for cp in copies(chip, slot):
                cp.wait()

        if fetch_all:
            @pl.when(j == 0)
            def _():
                for chip in range(4):
                    fetch(chip, chip)
                for chip in range(4):
                    arrived(chip, chip)

            mine = gbuf[j]
        else:
            @pl.when(j == 0)
            def _():
                fetch(0, 0)

            @pl.when(j < 3)
            def _():
                fetch(j + 1, (j + 1) % 2)

            arrived(j, j % 2)
            mine = gbuf[j % 2]
        p = mine + r_ref[...]
        words_ref[...] = sp.to_words(p)

        @pl.when(j == 2 * x + y)
        def _():
            own_ref[...] = p

    return pl.pallas_call(
        body, grid=(sp.layers, 4),
        in_specs=[ANY, pl.BlockSpec((None, None, sp.rows, sp.cols), lambda l, j: (j, l, 0, 0))],
        out_specs=[pl.BlockSpec((None, sp.rows, sp.cols), lambda l, j: (l, 0, 0)),
                   pl.BlockSpec((None, None, sp.rows, sp.wcols), lambda l, j: (j, l, 0, 0))],
        out_shape=[jax.ShapeDtypeStruct((sp.layers, sp.rows, sp.cols), F32),
                   jax.ShapeDtypeStruct((4, sp.layers, sp.rows, sp.wcols), F32)],
        scratch_shapes=[pltpu.VMEM((4 if fetch_all else 2, sp.rows, sp.cols), F32), pltpu.SemaphoreType.DMA((4, npc))],
        name=f"pair_sum_{tag}", compiler_params=_params(("arbitrary", "arbitrary")))(g, r1)


def _chip_copies(srcs, dsts, ssem, rsem):
    x, y, c = _coords()
    copies = []
    for i, (src, dst) in enumerate(zip(srcs, dsts)):
        for t, (fx, fy) in enumerate(FLIPS):
            tx, ty = x ^ fx, y ^ fy
            copies.append(pltpu.make_async_remote_copy(src.at[2 * tx + ty], dst.at[t], ssem.at[3 * i + t], rsem.at[3 * i + t],
                                                       device_id=(tx, ty, c), device_id_type=MESH))
    return copies


def _slot_shapes(words):
    return [(3,) + w.shape[1:] for w in words]


def reduce_chips(words, specs, tag):
    ns = len(specs)

    def body(*refs):
        copies = _chip_copies(refs[:ns], refs[ns:2 * ns], *refs[2 * ns:])
        for cp in copies:
            cp.start()
        for cp in copies:
            cp.wait()

    return pl.pallas_call(
        body, in_specs=[ANY] * ns, out_specs=[ANY] * ns,
        out_shape=[jax.ShapeDtypeStruct(s, F32) for s in _slot_shapes(words)],
        scratch_shapes=[pltpu.SemaphoreType.DMA((3 * ns,)), pltpu.SemaphoreType.DMA((3 * ns,))],
        name=f"reduce_chips_{tag}", compiler_params=pltpu.CompilerParams(has_side_effects=True))(*words)


def chips_start(words, specs, tag):
    ns = len(specs)
    slots = [lax.empty(s, F32) for s in _slot_shapes(words)]

    def body(*refs):
        ssem, rsem = refs[2 * ns:2 * ns + 2]
        for cp in _chip_copies(refs[:ns], refs[ns:2 * ns], ssem, rsem):
            cp.start()
        refs[-1][...] = jnp.zeros_like(refs[-1])

    outs = pl.pallas_call(
        body, in_specs=[HBM] * (2 * ns),
        out_specs=[SEM, SEM] + [HBM] * (2 * ns) + [pl.BlockSpec(memory_space=pltpu.VMEM)],
        out_shape=[pltpu.SemaphoreType.DMA((3 * ns,)), pltpu.SemaphoreType.DMA((3 * ns,))]
        + [pltpu.HBM(a.shape, a.dtype) for a in list(words) + slots] + [jax.ShapeDtypeStruct((8, 128), F32)],
        input_output_aliases={i: 2 + i for i in range(2 * ns)},
        name=f"chips_start_{tag}", compiler_params=pltpu.CompilerParams(has_side_effects=EFFECT))(
            *_in_hbm(list(words) + slots))
    return outs[0], outs[1], outs[2:2 + ns], outs[2 + ns:2 + 2 * ns], outs[-1]


def chips_wait(ssem, rsem, words, slots, specs, after, tag):
    ns = len(specs)

    def body(*refs):
        for cp in _chip_copies(refs[:ns], refs[ns:2 * ns], refs[2 * ns], refs[2 * ns + 1]):
            cp.wait_send()
            cp.wait_recv()

    outs = pl.pallas_call(
        body, in_specs=[HBM] * (2 * ns) + [SEM, SEM, ANY], out_specs=[HBM] * (2 * ns),
        out_shape=[pltpu.HBM(a.shape, a.dtype) for a in list(words) + list(slots)],
        input_output_aliases={i: i for i in range(2 * ns)},
        name=f"chips_wait_{tag}", compiler_params=pltpu.CompilerParams(has_side_effects=EFFECT))(
            *words, *slots, ssem, rsem, after)
    return outs[ns:]


def sum_chips(own, r2, sp, tag):
    def body(own_ref, r_ref, o_ref):
        o_ref[...] = ((own_ref[...] + sp.from_words(r_ref[0])) + sp.from_words(r_ref[1])) + sp.from_words(r_ref[2])

    blk = pl.BlockSpec((None, sp.rows, sp.cols), lambda l: (l, 0, 0))
    return pl.pallas_call(
        body, grid=(sp.layers,), in_specs=[blk, pl.BlockSpec((3, None, sp.rows, sp.wcols), lambda l: (0, l, 0, 0))],
        out_specs=blk, out_shape=jax.ShapeDtypeStruct((sp.layers, sp.rows, sp.cols), F32),
        name=f"sum_chips_{tag}", compiler_params=_params(("arbitrary",)))(own, r2)


def reduce_scatter_start(grads, specs, names, dep, tag):
    theirs = reduce_pair(grads, specs, tag, dep)
    sums = [pair_sum(g, r1, sp, f"{tag}_{n}") for g, r1, sp, n in zip(grads, theirs, specs, names)]
    ssem, rsem, words, slots, token = chips_start([s[1] for s in sums], specs, tag)
    return (ssem, rsem, words, slots, [s[0] for s in sums]), token


def reduce_scatter_finish(state, after, specs, tag):
    ssem, rsem, words, slots, own = state
    return list(zip(own, chips_wait(ssem, rsem, words, slots, specs, after, tag)))


def reduce_scatter(grads, specs, names, tag, dep=None):
    theirs = reduce_pair(grads, specs, tag, dep)
    sums = [pair_sum(g, r1, sp, f"{tag}_{n}") for g, r1, sp, n in zip(grads, theirs, specs, names)]
    return list(zip([s[0] for s in sums], reduce_chips([s[1] for s in sums], specs, tag)))


def _adamw_math(w, g, m, v):
    c1 = 1.0 - ADAM_B1 ** ADAM_STEP
    c2 = 1.0 - ADAM_B2 ** ADAM_STEP
    m2 = ADAM_B1 * m + (1.0 - ADAM_B1) * g
    v2 = ADAM_B2 * v + (1.0 - ADAM_B2) * (g * g)
    return -ADAM_LR * ((m2 / c1) / (jnp.sqrt(v2 / c2) + ADAM_EPS) + ADAM_WD * w), m2, v2


def adamw(w, g, m, v, name):
    shape = w.shape
    cols = shape[-1]
    rows = math.prod(shape[:-1])
    tr = rows
    while tr * cols * 4 > (1 << 20) and tr % 16 == 0:
        tr //= 2

    def body(w_ref, g_ref, m_ref, v_ref, d_ref, nm_ref, nv_ref):
        d_ref[...], nm_ref[...], nv_ref[...] = _adamw_math(w_ref[...], g_ref[...], m_ref[...], v_ref[...])

    blk = pl.BlockSpec((tr, cols), lambda i: (i, 0))
    outs = pl.pallas_call(
        body, grid=(rows // tr,), in_specs=[blk] * 4, out_specs=[blk] * 3,
        out_shape=[jax.ShapeDtypeStruct((rows, cols), F32)] * 3,
        name=f"adamw_{name}", compiler_params=_params(("arbitrary",)))(
            *[a.reshape(rows, cols) for a in (w, g, m, v)])
    return [o.reshape(shape) for o in outs]


def adamw_layer(w, sums, m, v, sp, l, prev, dep, name):
    _, rows, cols = w.shape
    tc = _col_tile(cols)
    twc = tc // 2 if sp.packed else tc
    extra = ([] if prev is None else list(prev)) + ([] if dep is None else [dep])

    def body(w_ref, own_ref, r_ref, m_ref, v_ref, *rest):
        g_ref, d_ref, nm_ref, nv_ref = rest[-4:]
        g = ((own_ref[...] + sp.from_words(r_ref[0])) + sp.from_words(r_ref[1])) + sp.from_words(r_ref[2])
        g_ref[...] = g
        d_ref[...], nm_ref[...], nv_ref[...] = _adamw_math(w_ref[...], g, m_ref[...], v_ref[...])

    blk = pl.BlockSpec((None, rows, tc), lambda n: (l, 0, n))
    return pl.pallas_call(
        body, grid=(cols // tc,),
        in_specs=[blk, pl.BlockSpec((None, rows, tc), lambda n: (0, 0, n)),
                  pl.BlockSpec((3, None, rows, twc), lambda n: (0, 0, 0, n)), blk, blk] + [ANY] * len(extra),
        out_specs=[blk] * 4, out_shape=[jax.ShapeDtypeStruct(w.shape, F32)] * 4,
        input_output_aliases={} if prev is None else {5 + i: i for i in range(4)},
        name=f"adamw_{name}_l{l}", compiler_params=_params(("arbitrary",)))(w, sums[0], sums[1], m, v, *extra)


WEIGHTS = ("pre_norm_g", "w_in", "gm_ln_g", "gm_ln_b", "gm_ws", "gm_bs", "mla_q_norm_g", "mla_w_uq", "mla_kv_norm_g",
           "mla_w_ukv", "lru_conv_w", "lru_conv_b", "lru_w_a", "lru_b_a", "lru_w_x", "lru_b_x", "lru_lambda",
           "w_proj_a", "w_proj_b", "w_proj_c", "w_out", "post_norm_g")
SHARDED = ("w_in", "mla_w_uq", "mla_w_ukv", "lru_conv_w", "w_proj_a", "w_proj_b", "w_proj_c", "w_out")
REPLICATED = tuple(n for n in WEIGHTS if n not in SHARDED)


def _step(x, target, wts, ms, vs):
    t12 = lambda a: jnp.swapaxes(a, 1, 2)
    names = list(WEIGHT_SPECS)
    specs = [WEIGHT_SPECS[n] for n in names]
    tabs = _rope_tables()
    own = {"w_in_t": t12(wts["w_in"]), "wq": t12(wts["mla_w_uq"]), "wkv": t12(wts["mla_w_ukv"]),
           "conv": jnp.pad(t12(wts["lru_conv_w"]), ((0, 0), (0, 0), (0, 124))),
           "w_proj_a": wts["w_proj_a"], "w_proj_b": wts["w_proj_b"], "w_proj_c": wts["w_proj_c"], "w_out": wts["w_out"]}
    first, rest = ["w_in_t"], [n for n in names if n != "w_in_t"]
    sfirst, srest = [WEIGHT_SPECS[n] for n in first], [WEIGHT_SPECS[n] for n in rest]

    w = {n: wts[n] for n in REPLICATED}
    w["kv_g384"] = jnp.concatenate([wts["mla_kv_norm_g"], jnp.ones((L, 128), F32)], axis=1)

    def layer_weights(ns, words):
        gw = dict(zip(ns, words))
        gw["wq"] = gw["wq"].reshape(HEADS, 192, 384)
        gw["wkv"] = gw["wkv"].reshape(HEADS, 256, 128)
        gw["conv"] = gw["conv"][0, :, :4].T
        return gw

    place = lambda l, dep: {n: pack_place(own[n], WEIGHT_SPECS[n], l, f"w{l}_{n}", dep) for n in names}
    placed = [place(0, None)]
    words_of = lambda l, ns: [placed[l][n][0] for n in ns]
    bufs_of = lambda l, ns: [placed[l][n][1] for n in ns]
    later = {}

    ssem_a, rsem_a, wthru_a, fthru_a, token_a = gather_start(words_of(0, first), bufs_of(0, first), sfirst, None, "w0a")
    placed.append(place(1, token_a))
    win0 = gather_forward(gather_wait(ssem_a, rsem_a, wthru_a, fthru_a, sfirst, placed[1]["w_in_t"][0], "w0a"), sfirst, "w0a")
    ssem_b, rsem_b, wthru_b, fthru_b, token_b = gather_start(words_of(0, rest), bufs_of(0, rest), srest, win0[0], "w0b")
    ssem1, rsem1, wthru1, fthru1, token1 = gather_start(words_of(1, names), bufs_of(1, names), specs, token_b, "w1")

    def fwd0_mid(ya):
        rest0 = gather_forward(gather_wait(ssem_b, rsem_b, wthru_b, fthru_b, srest, ya, "w0b"), srest, "w0b")
        later["gw0"] = layer_weights(first + rest, list(win0) + list(rest0))
        return later["gw0"], None

    x1, saved0 = _layer_fwd(x, 0, w, {"w_in_t": win0[0]}, tabs, dep=token1, mid=fwd0_mid)
    words1 = gather_forward(gather_wait(ssem1, rsem1, wthru1, fthru1, specs, x1, "w1"), specs, "w1")
    gw0, gw1 = later["gw0"], layer_weights(names, words1)
    x2, saved1 = _layer_fwd(x1, 1, w, gw1, tabs)
    loss, dx2 = loss_head(x2, target)

    def bwd1_mid(gg, last):
        later["p1b"] = pair_start([gg[n] for n in rest], srest, last, "g1b")
        return later["p1b"][4]

    dx1, gg1, g1 = _layer_bwd(dx2, 1, w, gw1, tabs, saved1, mid=bwd1_mid)
    grads1b, theirs1b = pair_wait(*later["p1b"][:4], srest, dx1, "g1b")
    p1a = pair_start([gg1["w_in_t"]], sfirst, theirs1b[0], "g1a")

    def bwd0_early(last):
        grads1a, theirs1a = pair_wait(*p1a[:4], sfirst, last, "g1a")
        mine = dict(zip(first + rest, list(grads1a) + list(grads1b)))
        theirs = dict(zip(first + rest, list(theirs1a) + list(theirs1b)))
        sums = [pair_sum(mine[n], theirs[n], WEIGHT_SPECS[n], f"g1_{n}") for n in names]
        ssem, rsem, words, slots, token = chips_start([s[1] for s in sums], specs, "g1")
        later["g1"] = (ssem, rsem, words, slots, [s[0] for s in sums])
        return token

    def bwd0_mid(gg, last):
        later["g0b"], token = reduce_scatter_start([gg[n] for n in rest], srest, rest, last, "g0b")
        return token

    def bwd0_late(g_win):
        later["p0a"] = pair_start([g_win], sfirst, None, "g0a")
        return later["p0a"][4]

    dx0, gg0, g0 = _layer_bwd(dx1, 0, w, gw0, tabs, saved0, dep=p1a[4], early=bwd0_early, mid=bwd0_mid, late=bwd0_late)
    s1 = dict(zip(names, reduce_scatter_finish(later["g1"], dx0, specs, "g1")))
    s0 = dict(zip(rest, reduce_scatter_finish(later["g0b"], dx0, srest, "g0b")))

    grads0a, theirs0a = pair_wait(*later["p0a"][:4], sfirst, dx0, "g0a")
    own0a, words0a = pair_sum(grads0a[0], theirs0a[0], sfirst[0], "g0a_w_in_t")
    ssem_g, rsem_g, wthru_g, slots_g, token_g = chips_start([words0a], sfirst, "g0a")

    keys = {"w_in": "w_in_t", "mla_w_uq": "wq", "mla_w_ukv": "wkv",
            "w_proj_a": "w_proj_a", "w_proj_b": "w_proj_b", "w_proj_c": "w_proj_c", "w_out": "w_out"}
    transposed = ("w_in", "mla_w_uq", "mla_w_ukv")
    state_of = lambda n: [own[keys[n]], t12(ms[n]), t12(vs[n])] if n in transposed else [wts[n], ms[n], vs[n]]

    def update(n, l, sums, prev, dep):
        wl, ml, vl = state_of(n)
        return adamw_layer(wl, sums[keys[n]], ml, vl, WEIGHT_SPECS[keys[n]], l, prev, dep, n)

    upd = {n: update(n, 1, s1, None, token_g) for n in keys}
    for n in keys:
        if n != "w_in":
            upd[n] = update(n, 0, s0, upd[n], None)
    rep_flat = jnp.concatenate([jnp.stack([g0[n], g1[n]]).reshape(-1) for n in REPLICATED] + [loss[None]])
    rep_flat = jnp.pad(rep_flat, (0, REP_ROWS * NDEV * D - rep_flat.shape[0])).reshape(1, REP_ROWS * NDEV, D)
    rep_parts = reduce_scatter([rep_flat], [REP_SPEC], ["rep"], "rep", upd["w_out"][0])[0]
    rep_sum = sum_chips(*rep_parts, REP_SPEC, "rep")
    rep_full = all_gather([rep_sum], 0, [REP_SPEC], ["rep"], "rep")[0].reshape(-1)

    out = {}
    conv_sp = WEIGHT_SPECS["conv"]
    g_conv = t12(jnp.concatenate([sum_chips(*s0["conv"], conv_sp, "conv0"), sum_chips(*s1["conv"], conv_sp, "conv1")])[:, :, :4])
    out["lru_conv_w"] = [g_conv] + adamw(wts["lru_conv_w"], g_conv, ms["lru_conv_w"], vs["lru_conv_w"], "lru_conv_w")
    at = 0
    for n in REPLICATED:
        size = math.prod(wts[n].shape)
        g = rep_full[at:at + size].reshape(wts[n].shape)
        out[n] = [g] + adamw(wts[n], g, ms[n], vs[n], n)
        at += size

    landed = chips_wait(ssem_g, rsem_g, wthru_g, slots_g, sfirst, out[REPLICATED[-1]][1], "g0a")
    s0["w_in_t"] = (own0a, landed[0])
    upd["w_in"] = update("w_in", 0, s0, upd["w_in"], None)
    out.update({n: [t12(r) for r in upd[n]] if n in transposed else upd[n] for n in keys})

    return (rep_full[at], dx0[None], *[out[n][k] for k in range(4) for n in WEIGHTS])


def kernel(x, pre_norm_g, w_in, gm_ln_g, gm_ln_b, gm_ws, gm_bs, mla_q_norm_g, mla_w_uq, mla_kv_norm_g, mla_w_ukv, lru_conv_w, lru_conv_b, lru_w_a, lru_b_a, lru_w_x, lru_b_x, lru_lambda, w_proj_a, w_proj_b, w_proj_c, w_out, post_norm_g, loss_target, m_pre_norm_g, m_w_in, m_gm_ln_g, m_gm_ln_b, m_gm_ws, m_gm_bs, m_mla_q_norm_g, m_mla_w_uq, m_mla_kv_norm_g, m_mla_w_ukv, m_lru_conv_w, m_lru_conv_b, m_lru_w_a, m_lru_b_a, m_lru_w_x, m_lru_b_x, m_lru_lambda, m_w_proj_a, m_w_proj_b, m_w_proj_c, m_w_out, m_post_norm_g, v_pre_norm_g, v_w_in, v_gm_ln_g, v_gm_ln_b, v_gm_ws, v_gm_bs, v_mla_q_norm_g, v_mla_w_uq, v_mla_kv_norm_g, v_mla_w_ukv, v_lru_conv_w, v_lru_conv_b, v_lru_w_a, v_lru_b_a, v_lru_w_x, v_lru_b_x, v_lru_lambda, v_w_proj_a, v_w_proj_b, v_w_proj_c, v_w_out, v_post_norm_g):
    wts = dict(zip(WEIGHTS, (pre_norm_g, w_in, gm_ln_g, gm_ln_b, gm_ws, gm_bs, mla_q_norm_g, mla_w_uq, mla_kv_norm_g,
                             mla_w_ukv, lru_conv_w, lru_conv_b, lru_w_a, lru_b_a, lru_w_x, lru_b_x, lru_lambda,
                             w_proj_a, w_proj_b, w_proj_c, w_out, post_norm_g)))
    ms = dict(zip(WEIGHTS, (m_pre_norm_g, m_w_in, m_gm_ln_g, m_gm_ln_b, m_gm_ws, m_gm_bs, m_mla_q_norm_g, m_mla_w_uq,
                            m_mla_kv_norm_g, m_mla_w_ukv, m_lru_conv_w, m_lru_conv_b, m_lru_w_a, m_lru_b_a, m_lru_w_x,
                            m_lru_b_x, m_lru_lambda, m_w_proj_a, m_w_proj_b, m_w_proj_c, m_w_out, m_post_norm_g)))
    vs = dict(zip(WEIGHTS, (v_pre_norm_g, v_w_in, v_gm_ln_g, v_gm_ln_b, v_gm_ws, v_gm_bs, v_mla_q_norm_g, v_mla_w_uq,
                            v_mla_kv_norm_g, v_mla_w_ukv, v_lru_conv_w, v_lru_conv_b, v_lru_w_a, v_lru_b_a, v_lru_w_x,
                            v_lru_b_x, v_lru_lambda, v_w_proj_a, v_w_proj_b, v_w_proj_c, v_w_out, v_post_norm_g)))
    return _step(x[0], loss_target[0], wts, ms, vs)
```

```python
import functools
import math

import jax
import jax.numpy as jnp
from jax import lax
from jax.experimental import pallas as pl
from jax.experimental.pallas import tpu as pltpu

F32 = jnp.float32
BF16 = jnp.bfloat16

T = 2048
D = 1024
L = 2
NDEV = 8
EPS = 1e-6
CHUNK_SHIFT = 6
HEADS = 8
QK = 192
LRU_W = 1280
LRU_TILE = 640
N_IN = 10432
SHARD = N_IN // NDEV
OFF_U, OFF_V, OFF_ZA, OFF_CQ, OFF_CKV, OFF_ZB = 0, 1024, 2048, 3072, 3456, 3840
OFF_XC, OFF_ZC, OFF_GA, OFF_GB, OFF_GC = 5120, 6400, 7680, 8704, 9728
NPAD = 10752
PAD1_AT, PAD1 = 3776, 64
PAD2_AT, PAD2 = 4800, 256
WIN_PIECES = ((0, 888), (888, 280), (1168, 136))
VMEM_LIMIT = 60 * 1024 * 1024

ADAM_LR, ADAM_B1, ADAM_B2, ADAM_EPS, ADAM_WD, ADAM_STEP = 0.001, 0.9, 0.999, 1e-08, 0.01, 10

_NN = (((1,), (0,)), ((), ()))
_NT = (((1,), (1,)), ((), ()))
_TN = (((0,), (0,)), ((), ()))


def _dg(a, b, dims):
    return lax.dot_general(a.astype(BF16), b.astype(BF16), dims, preferred_element_type=F32)


@jax.custom_vjp
def dot_nn(a, b):
    return _dg(a, b, _NN)


def _nn_fwd(a, b):
    return _dg(a, b, _NN), (a, b)


def _nn_bwd(res, g):
    a, b = res
    return _dg(g, b, _NT).astype(a.dtype), _dg(a, g, _TN).astype(b.dtype)


dot_nn.defvjp(_nn_fwd, _nn_bwd)


@jax.custom_vjp
def dot_nt(a, b):
    return _dg(a, b, _NT)


def _nt_fwd(a, b):
    return _dg(a, b, _NT), (a, b)


def _nt_bwd(res, g):
    a, b = res
    return _dg(g, b, _NN).astype(a.dtype), _dg(g, a, _TN).astype(b.dtype)


dot_nt.defvjp(_nt_fwd, _nt_bwd)


def _params(sem=None):
    return pltpu.CompilerParams(dimension_semantics=sem, vmem_limit_bytes=VMEM_LIMIT)


def _sigmoid(x):
    return 1.0 / (1.0 + jnp.exp(-x))


def _silu(x):
    return x * _sigmoid(x)


def _rms(x, g):
    ms = jnp.mean(x * x, axis=-1, keepdims=True)
    return x * lax.rsqrt(ms + EPS) * g


def _acc(ref, val, first):
    @pl.when(first)
    def _():
        ref[...] = val

    @pl.when(jnp.logical_not(first))
    def _():
        ref[...] += val


ANY = pl.BlockSpec(memory_space=pl.ANY)


INPROJ_TN = 768


def inproj_fwd(x, g, wt, l, dep=None):
    tn = INPROJ_TN

    def body(x_ref, g_ref, w_ref, *rest):
        proj_ref, h_ref = rest[-2:]

        @pl.when(pl.program_id(0) == 0)
        def _():
            h_ref[...] = _rms(x_ref[...], g_ref[...]).astype(BF16)

        proj_ref[...] = lax.dot_general(h_ref[...], _unpack(w_ref[...]).astype(BF16), _NT, preferred_element_type=F32)

    deps = [] if dep is None else [dep]
    return pl.pallas_call(
        body, grid=(NPAD // tn,),
        in_specs=[pl.BlockSpec((T, D), lambda j: (0, 0)), pl.BlockSpec((1, D), lambda j: (0, 0)),
                  pl.BlockSpec((None, tn, D // 2), lambda j: (0, j, 0))] + [ANY] * len(deps),
        out_specs=[pl.BlockSpec((T, tn), lambda j: (0, j)), pl.BlockSpec((T, D), lambda j: (0, 0))],
        out_shape=[jax.ShapeDtypeStruct((T, NPAD), F32), jax.ShapeDtypeStruct((T, D), BF16)],
        name=f"inproj_fwd_l{l}", compiler_params=_params(("arbitrary",)))(x, g, wt, *deps)


def inproj_bwd(dproj, h, wt, l, dep=None):
    tn = INPROJ_TN
    deps = [] if dep is None else [dep]

    def body(dp_ref, h_ref, w_ref, *rest):
        dwt_ref, dh_ref = rest[-2:]
        dp = dp_ref[...]
        dwt_ref[...] = lax.dot_general(dp, h_ref[...], _TN, preferred_element_type=F32)
        contrib = lax.dot_general(dp, _unpack(w_ref[...]).astype(BF16), _NN, preferred_element_type=F32)
        _acc(dh_ref, contrib, pl.program_id(0) == 0)

    return pl.pallas_call(
        body, grid=(NPAD // tn,),
        in_specs=[pl.BlockSpec((T, tn), lambda j: (0, j)), pl.BlockSpec((T, D), lambda j: (0, 0)),
                  pl.BlockSpec((None, tn, D // 2), lambda j: (0, j, 0))] + [ANY] * len(deps),
        out_specs=[pl.BlockSpec((None, tn, D), lambda j: (0, j, 0)), pl.BlockSpec((T, D), lambda j: (0, 0))],
        out_shape=[jax.ShapeDtypeStruct((1, NPAD, D), F32), jax.ShapeDtypeStruct((T, D), F32)],
        name=f"inproj_bwd_l{l}", compiler_params=_params(("arbitrary",)))(dproj, h, wt, *deps)


def prenorm_bwd(x, g, dh, dxn, l, dep=None):
    tm = 512
    deps = [] if dep is None else [dep]

    def body(x_ref, g_ref, dh_ref, dxn_ref, *rest):
        dx_ref, dg_ref = rest[-2:]
        _, vjp = jax.vjp(_rms, x_ref[...], g_ref[...])
        dx, dg = vjp(dh_ref[...])
        dx_ref[...] = dx + dxn_ref[...]
        _acc(dg_ref, dg, pl.program_id(0) == 0)

    tok = pl.BlockSpec((tm, D), lambda i: (i, 0))
    vec = pl.BlockSpec((1, D), lambda i: (0, 0))
    return pl.pallas_call(
        body, grid=(T // tm,), in_specs=[tok, vec, tok, tok] + [ANY] * len(deps), out_specs=[tok, vec],
        out_shape=[jax.ShapeDtypeStruct((T, D), F32), jax.ShapeDtypeStruct((1, D), F32)],
        name=f"prenorm_bwd_l{l}", compiler_params=_params(("arbitrary",)))(x, g, dh, dxn, *deps)


def _gmlp_tile(u, v, z, ln_g, ln_b, ws, bs):
    mu = jnp.mean(v, axis=-1, keepdims=True)
    vc = v - mu
    var = jnp.mean(vc * vc, axis=-1, keepdims=True)
    vn = vc * lax.rsqrt(var + EPS) * ln_g + ln_b
    qi = lax.broadcasted_iota(jnp.int32, (128, 128), 0) >> CHUNK_SHIFT
    kj = lax.broadcasted_iota(jnp.int32, (128, 128), 1) >> CHUNK_SHIFT
    mask = kj <= qi
    outs = []
    for g in range(4):
        wm = jnp.where(mask, ws[g], 0.0)
        outs.append(dot_nn(wm, vn[:, 256 * g:256 * (g + 1)]) + bs[g])
    sv = jnp.concatenate(outs, axis=1)
    return u * sv * _silu(z)


GMLP_ROWS = 256


def _gmlp_specs():
    blk = lambda c: pl.BlockSpec((GMLP_ROWS, 1024), lambda n, c=c: (n, c))
    vec = pl.BlockSpec((1, 1024), lambda n: (0, 0))
    return [blk(0), blk(1), blk(2), vec, vec,
            pl.BlockSpec((4, 128, 128), lambda n: (0, 0, 0)), pl.BlockSpec((4, 128, 1), lambda n: (0, 0, 0))]


def gmlp_fwd(proj, ln_g, ln_b, ws, bs, l):
    def body(u_ref, v_ref, z_ref, g_ref, b_ref, ws_ref, bs_ref, y_ref):
        for r in range(0, GMLP_ROWS, 128):
            rows = slice(r, r + 128)
            y_ref[rows, :] = _gmlp_tile(u_ref[rows, :], v_ref[rows, :], z_ref[rows, :], g_ref[...], b_ref[...],
                                        [ws_ref[g] for g in range(4)], [bs_ref[g] for g in range(4)])

    return pl.pallas_call(
        body, grid=(T // GMLP_ROWS,), in_specs=_gmlp_specs(),
        out_specs=pl.BlockSpec((GMLP_ROWS, 1024), lambda n: (n, 0)),
        out_shape=jax.ShapeDtypeStruct((T, 1024), F32),
        name=f"gmlp_fwd_l{l}", compiler_params=_params(("arbitrary",)))(proj, proj, proj, ln_g, ln_b, ws, bs)


def gmlp_bwd(proj, ln_g, ln_b, ws, bs, dy, dproj, l):
    def body(u_ref, v_ref, z_ref, g_ref, b_ref, ws_ref, bs_ref, dy_ref, _, dseg_ref, dg_ref, db_ref, dws_ref, dbs_ref):
        for r in range(0, GMLP_ROWS, 128):
            rows = slice(r, r + 128)
            first = jnp.logical_and(pl.program_id(0) == 0, r == 0)
            _, vjp = jax.vjp(_gmlp_tile, u_ref[rows, :], v_ref[rows, :], z_ref[rows, :], g_ref[...], b_ref[...],
                             [ws_ref[g] for g in range(4)], [bs_ref[g] for g in range(4)])
            du, dv, dz, dg, db, dws, dbs = vjp(dy_ref[rows, :])
            dseg_ref[rows, 0:1024] = du.astype(BF16)
            dseg_ref[rows, 1024:2048] = dv.astype(BF16)
            dseg_ref[rows, 2048:3072] = dz.astype(BF16)
            _acc(dg_ref, dg, first)
            _acc(db_ref, db, first)
            for g in range(4):
                _acc(dws_ref.at[g], dws[g], first)
                _acc(dbs_ref.at[g], dbs[g], first)

    vec = pl.BlockSpec((1, 1024), lambda n: (0, 0))
    return pl.pallas_call(
        body, grid=(T // GMLP_ROWS,),
        in_specs=_gmlp_specs() + [pl.BlockSpec((GMLP_ROWS, 1024), lambda n: (n, 0)), ANY],
        out_specs=[pl.BlockSpec((GMLP_ROWS, 3072), lambda n: (n, OFF_U // 3072)), vec, vec,
                   pl.BlockSpec((4, 128, 128), lambda n: (0, 0, 0)), pl.BlockSpec((4, 128, 1), lambda n: (0, 0, 0))],
        out_shape=[jax.ShapeDtypeStruct((T, NPAD), BF16), jax.ShapeDtypeStruct((1, 1024), F32),
                   jax.ShapeDtypeStruct((1, 1024), F32), jax.ShapeDtypeStruct((4, 128, 128), F32),
                   jax.ShapeDtypeStruct((4, 128, 1), F32)],
        input_output_aliases={8: 0},
        name=f"gmlp_bwd_l{l}", compiler_params=_params(("arbitrary",)))(proj, proj, proj, ln_g, ln_b, ws, bs, dy, dproj)


QKV_TM = 512


def _qkv_tile(cq, ckvr, qg, kvg, wq, wkv, ctab, stab):
    tm = cq.shape[0]
    cqn = _rms(cq, qg)
    lane = lax.broadcasted_iota(jnp.int32, ckvr.shape, 1)
    iskv = lane < 256
    ms = jnp.sum(jnp.where(iskv, ckvr * ckvr, 0.0), axis=-1, keepdims=True) * (1.0 / 256)
    lm = jnp.where(iskv, ckvr * lax.rsqrt(ms + EPS) * kvg, ckvr)
    r = lax.broadcasted_iota(jnp.int32, (64, 128), 0)
    c = lax.broadcasted_iota(jnp.int32, (64, 128), 1)
    eye = jnp.where(c == r, 1.0, 0.0)
    eye_sw = jnp.where(c == ((r + 32) & 63), 1.0, 0.0)
    z64 = jnp.zeros((64, 256), F32)
    z128 = jnp.zeros((128, 128), F32)
    rk_rope = jnp.concatenate([z64, eye], axis=1)
    rk_sw = jnp.concatenate([jnp.zeros((128, 384), F32), jnp.concatenate([z64, eye_sw], axis=1)], axis=0)
    k_sw = dot_nt(lm, rk_sw) * stab
    qs, ks, vs = [], [], []
    for h in range(HEADS):
        wn, w1, w2 = wq[h]
        wk, wv = wkv[h]
        wq_h = jnp.concatenate([wn, w1, w2], axis=0)
        wq_sw = jnp.concatenate([jnp.zeros((128, 384), F32), w2, w1], axis=0)
        qs.append(dot_nt(cqn, wq_h) * ctab + dot_nt(cqn, wq_sw) * stab)
        rk_h = jnp.concatenate([jnp.concatenate([wk, z128], axis=1), rk_rope], axis=0)
        ks.append(dot_nt(lm, rk_h) * ctab + k_sw)
        vs.append(dot_nt(lm, jnp.concatenate([wv, z128], axis=1)))
    return qs, ks, vs


def _qkv_in_specs():
    tm = QKV_TM
    return [pl.BlockSpec((tm, 384), lambda i: (i, OFF_CQ // 384)), pl.BlockSpec((tm, 384), lambda i: (i, OFF_CKV // 384)),
            pl.BlockSpec((1, 384), lambda i: (0, 0)), pl.BlockSpec((1, 384), lambda i: (0, 0)),
            pl.BlockSpec((HEADS, 192, 384), lambda i: (0, 0, 0)), pl.BlockSpec((HEADS, 256, 128), lambda i: (0, 0, 0)),
            pl.BlockSpec((tm, 192), lambda i: (i, 0)), pl.BlockSpec((tm, 192), lambda i: (i, 0))]


def _qkv_weights(wq_ref, wkv_ref):
    wq = [(wq_ref[h, 0:128, :], wq_ref[h, 128:160, :], wq_ref[h, 160:192, :]) for h in range(HEADS)]
    wkv = [(_unpack(wkv_ref[h, 0:128, :]), _unpack(wkv_ref[h, 128:256, :])) for h in range(HEADS)]
    return wq, wkv


def qkv_fwd(proj, qg, kvg, wq, wkv, ctab, stab, l, dep=None):
    tm = QKV_TM
    deps = [] if dep is None else [dep]

    def body(cq_ref, ckvr_ref, qg_ref, kvg_ref, wq_ref, wkv_ref, c_ref, s_ref, *rest):
        q_ref, k_ref, v_ref = rest[-3:]
        wq_l, wkv_l = _qkv_weights(wq_ref, wkv_ref)
        qs, ks, vs = _qkv_tile(cq_ref[...], ckvr_ref[...], qg_ref[...], kvg_ref[...], wq_l, wkv_l, c_ref[...], s_ref[...])
        for h in range(HEADS):
            q_ref[h] = qs[h]
            k_ref[h] = ks[h]
            v_ref[h] = vs[h]

    return pl.pallas_call(
        body, grid=(T // tm,), in_specs=_qkv_in_specs() + [ANY] * len(deps),
        out_specs=[pl.BlockSpec((HEADS, tm, QK), lambda i: (0, i, 0)), pl.BlockSpec((HEADS, tm, QK), lambda i: (0, i, 0)),
                   pl.BlockSpec((HEADS, tm, 128), lambda i: (0, i, 0))],
        out_shape=[jax.ShapeDtypeStruct((HEADS, T, QK), F32), jax.ShapeDtypeStruct((HEADS, T, QK), F32),
                   jax.ShapeDtypeStruct((HEADS, T, 128), F32)],
        name=f"qkv_fwd_l{l}", compiler_params=_params(("arbitrary",)))(proj, proj, qg, kvg, wq, wkv, ctab, stab, *deps)


def qkv_bwd(proj, qg, kvg, wq, wkv, ctab, stab, dq, dk, dv, dproj, l):
    tm = QKV_TM

    def body(cq_ref, ckvr_ref, qg_ref, kvg_ref, wq_ref, wkv_ref, c_ref, s_ref, dq_ref, dk_ref, dv_ref, _,
             dseg_ref, dqg_ref, dkvg_ref, dwq_ref, dwkv_ref):
        first = pl.program_id(0) == 0
        wq_l, wkv_l = _qkv_weights(wq_ref, wkv_ref)
        c_tab, s_tab = c_ref[...], s_ref[...]
        fn = lambda cq, ckvr, qg_, kvg_, wq_, wkv_: _qkv_tile(cq, ckvr, qg_, kvg_, wq_, wkv_, c_tab, s_tab)
        _, vjp = jax.vjp(fn, cq_ref[...], ckvr_ref[...], qg_ref[...], kvg_ref[...], wq_l, wkv_l)
        cts = ([dq_ref[h] for h in range(HEADS)], [dk_ref[h] for h in range(HEADS)], [dv_ref[h] for h in range(HEADS)])
        dcq, dckvr, dqg, dkvg, dwq, dwkv = vjp(cts)
        dseg_ref[:, 0:384] = dcq.astype(BF16)
        dseg_ref[:, 384:768] = dckvr.astype(BF16)
        _acc(dqg_ref, dqg, first)
        _acc(dkvg_ref, dkvg, first)
        for h in range(HEADS):
            _acc(dwq_ref.at[h, 0:128, :], dwq[h][0], first)
            _acc(dwq_ref.at[h, 128:160, :], dwq[h][1], first)
            _acc(dwq_ref.at[h, 160:192, :], dwq[h][2], first)
            _acc(dwkv_ref.at[h, 0:128, :], dwkv[h][0], first)
            _acc(dwkv_ref.at[h, 128:256, :], dwkv[h][1], first)

    hq = pl.BlockSpec((HEADS, tm, QK), lambda i: (0, i, 0))
    return pl.pallas_call(
        body, grid=(T // tm,),
        in_specs=_qkv_in_specs() + [hq, hq, pl.BlockSpec((HEADS, tm, 128), lambda i: (0, i, 0)), ANY],
        out_specs=[pl.BlockSpec((tm, 768), lambda i: (i, OFF_CQ // 768)), pl.BlockSpec((1, 384), lambda i: (0, 0)),
                   pl.BlockSpec((1, 384), lambda i: (0, 0)), pl.BlockSpec((HEADS, 192, 384), lambda i: (0, 0, 0)),
                   pl.BlockSpec((HEADS, 256, 256), lambda i: (0, 0, 0))],
        out_shape=[jax.ShapeDtypeStruct((T, NPAD), BF16), jax.ShapeDtypeStruct((1, 384), F32),
                   jax.ShapeDtypeStruct((1, 384), F32), jax.ShapeDtypeStruct((HEADS, 192, 384), F32),
                   jax.ShapeDtypeStruct((HEADS, 256, 256), F32)],
        input_output_aliases={11: 0},
        name=f"qkv_bwd_l{l}", compiler_params=_params(("arbitrary",)))(
            proj, proj, qg, kvg, wq, wkv, ctab, stab, dq, dk, dv, dproj)


ATT_TQ_FWD = 256
ATT_TQ_BWD = 512


def _attn_tile(q, kv_past, k, v, zb):
    q = q * (1.0 / math.sqrt(QK))
    s = dot_nt(q, k)
    qc = lax.broadcasted_iota(jnp.int32, s.shape, 0) >> CHUNK_SHIFT
    kc = lax.broadcasted_iota(jnp.int32, s.shape, 1) >> CHUNK_SHIFT
    s = jnp.where(kc <= qc, s, -1e30)
    m = jnp.max(s, axis=-1, keepdims=True)
    if kv_past is not None:
        sp = dot_nt(q, kv_past[0])
        m = jnp.maximum(m, jnp.max(sp, axis=-1, keepdims=True))
    m = lax.stop_gradient(m)
    p = jnp.exp(s - m)
    denom = jnp.sum(p, axis=-1, keepdims=True)
    o = dot_nn(p, v)
    if kv_past is not None:
        pp = jnp.exp(sp - m)
        denom = denom + jnp.sum(pp, axis=-1, keepdims=True)
        o = o + dot_nn(pp, kv_past[1])
    return o * (1.0 / denom) * _silu(zb)


def _attn_operands(k_ref, v_ref, g, tq):
    n = tq * g
    past = (k_ref[0:n, :], v_ref[0:n, :]) if g else None
    return past, k_ref[n:n + tq, :], v_ref[n:n + tq, :]


def _attn_in_specs(tq):
    return [pl.BlockSpec((None, tq, QK), lambda h, i: (h, i, 0)), pl.BlockSpec((None, T, QK), lambda h, i: (h, 0, 0)),
            pl.BlockSpec((None, T, 128), lambda h, i: (h, 0, 0)),
            pl.BlockSpec((tq, 128), lambda h, i: (i, OFF_ZB // 128 + h))]


def attn_fwd(q, k, v, proj, l):
    tq = ATT_TQ_FWD

    def body(q_ref, k_ref, v_ref, z_ref, y_ref):
        for g in range(T // tq):
            @pl.when(pl.program_id(1) == g)
            def _(g=g):
                past, k, v = _attn_operands(k_ref, v_ref, g, tq)
                y_ref[...] = _attn_tile(q_ref[...], past, k, v, z_ref[...])

    return pl.pallas_call(
        body, grid=(HEADS, T // tq), in_specs=_attn_in_specs(tq),
        out_specs=pl.BlockSpec((tq, 128), lambda h, i: (i, h)),
        out_shape=jax.ShapeDtypeStruct((T, 1024), F32),
        name=f"attn_fwd_l{l}", compiler_params=_params(("arbitrary", "arbitrary")))(q, k, v, proj)


def attn_bwd(q, k, v, proj, dy, dproj, l):
    tq = ATT_TQ_BWD

    def body(q_ref, k_ref, v_ref, z_ref, dy_ref, _, dq_ref, dk_ref, dv_ref, dz_ref):
        @pl.when(pl.program_id(1) == 0)
        def _():
            dk_ref[...] = jnp.zeros_like(dk_ref)
            dv_ref[...] = jnp.zeros_like(dv_ref)

        for g in range(T // tq):
            @pl.when(pl.program_id(1) == g)
            def _(g=g):
                n = tq * g
                past, k, v = _attn_operands(k_ref, v_ref, g, tq)
                _, vjp = jax.vjp(_attn_tile, q_ref[...], past, k, v, z_ref[...])
                dq, dpast, dk, dv, dz = vjp(dy_ref[...])
                dq_ref[...] = dq
                dz_ref[...] = dz.astype(BF16)
                dk_ref[n:n + tq, :] += dk
                dv_ref[n:n + tq, :] += dv
                if g:
                    dk_ref[0:n, :] += dpast[0]
                    dv_ref[0:n, :] += dpast[1]

    return pl.pallas_call(
        body, grid=(HEADS, T // tq),
        in_specs=_attn_in_specs(tq) + [pl.BlockSpec((tq, 128), lambda h, i: (i, h)), ANY],
        out_specs=[pl.BlockSpec((None, tq, QK), lambda h, i: (h, i, 0)), pl.BlockSpec((None, T, QK), lambda h, i: (h, 0, 0)),
                   pl.BlockSpec((None, T, 128), lambda h, i: (h, 0, 0)),
                   pl.BlockSpec((tq, 128), lambda h, i: (i, OFF_ZB // 128 + h))],
        out_shape=[jax.ShapeDtypeStruct((HEADS, T, QK), F32), jax.ShapeDtypeStruct((HEADS, T, QK), F32),
                   jax.ShapeDtypeStruct((HEADS, T, 128), F32), jax.ShapeDtypeStruct((T, NPAD), BF16)],
        input_output_aliases={5: 3},
        name=f"attn_bwd_l{l}", compiler_params=_params(("arbitrary", "arbitrary")))(q, k, v, proj, dy, dproj)


LRU_TT = 256


def _lru_gates(xc, wa, wx, ba, bx, lam):
    r = _sigmoid(dot_nn(xc, wa) + ba)
    i = _sigmoid(dot_nn(xc, wx) + bx)
    sp = jnp.maximum(-lam, 0.0) + jnp.log1p(jnp.exp(-jnp.abs(lam)))
    log_a = -8.0 * r * sp
    a = jnp.exp(log_a)
    mult = jnp.sqrt(jnp.maximum(1.0 - jnp.exp(2.0 * log_a), 0.0))
    return a, mult * (i * xc)


def _shift_down(x, s, halo):
    n, c = x.shape
    r = pltpu.roll(x.reshape(n // 8, 8, c), s, 1)
    before = jnp.concatenate([pltpu.roll(halo, s, 0)[None], r[:-1]], axis=0)
    sub = lax.broadcasted_iota(jnp.int32, r.shape, 1)
    return jnp.where(sub >= s, r, before).reshape(n, c)


def _shift_up(x, s, halo):
    n, c = x.shape
    r = pltpu.roll(x.reshape(n // 8, 8, c), 8 - s, 1)
    after = jnp.concatenate([r[1:], pltpu.roll(halo, 8 - s, 0)[None]], axis=0)
    sub = lax.broadcasted_iota(jnp.int32, r.shape, 1)
    return jnp.where(sub < 8 - s, r, after).reshape(n, c)


def _conv(x, halo, w_ref, b):
    return (w_ref[3:4, :] * x + w_ref[2:3, :] * _shift_down(x, 1, halo) + w_ref[1:2, :] * _shift_down(x, 2, halo)
            + w_ref[0:1, :] * _shift_down(x, 3, halo) + b)


def _scan(a, b, reverse, carry):
    n, c = a.shape
    a, b = a.reshape(n // 8, 8, c), b.reshape(n // 8, 8, c)
    sub = lax.broadcasted_iota(jnp.int32, a.shape, 1)
    for d in (1, 2, 4):
        keep = sub < 8 - d if reverse else sub >= d
        shift = 8 - d if reverse else d
        a_sh = jnp.where(keep, pltpu.roll(a, shift, 1), 1.0)
        b_sh = jnp.where(keep, pltpu.roll(b, shift, 1), 0.0)
        b = a * b_sh + b
        a = a * a_sh
    a, b = a.reshape(n, c), b.reshape(n, c)
    groups = [None] * (n // 8)
    for g in (reversed(range(n // 8)) if reverse else range(n // 8)):
        h = a[8 * g:8 * g + 8] * carry + b[8 * g:8 * g + 8]
        groups[g] = h
        carry = h[0:1] if reverse else h[7:8]
    return jnp.concatenate(groups, axis=0), carry


def _lru_param_specs(l):
    ct = LRU_TILE
    vec = pl.BlockSpec((1, ct), lambda n, i: (0, n))
    mat = pl.BlockSpec((None, 8, 80, 80), lambda n, i: (l, n, 0, 0))
    return [pl.BlockSpec((4, ct), lambda n, i: (0, n)), vec, mat, mat, vec, vec, vec]


def _blocks_to_dense(w_ref, dense):
    dense[...] = jnp.zeros_like(dense)
    for b in range(8):
        dense[80 * b:80 * b + 80, 80 * b:80 * b + 80] = w_ref[b]


def _dense_to_blocks(dense, w_ref):
    for b in range(8):
        w_ref[b] = dense[80 * b:80 * b + 80, 80 * b:80 * b + 80]


def lru_fwd(proj, conv_w, conv_b, wa, wx, ba, bx, lam, l):
    tt, ct = LRU_TT, LRU_TILE

    def body(x_ref, z_ref, cw_ref, cb_ref, wa_ref, wx_ref, ba_ref, bx_ref, lam_ref, h_ref, y_ref, halo, hcar, wa, wx):
        @pl.when(pl.program_id(1) == 0)
        def _():
            halo[...] = jnp.zeros_like(halo)
            hcar[...] = jnp.zeros_like(hcar)
            _blocks_to_dense(wa_ref, wa)
            _blocks_to_dense(wx_ref, wx)

        x = x_ref[...]
        xc = _conv(x, halo[...], cw_ref, cb_ref[...])
        halo[...] = x[tt - 8:tt]
        a, b = _lru_gates(xc, wa[...], wx[...], ba_ref[...], bx_ref[...], lam_ref[...])
        h, hcar[...] = _scan(a, b, False, hcar[...])
        h_ref[...] = h
        y_ref[...] = h * _silu(z_ref[...])

    seq = pl.BlockSpec((tt, ct), lambda n, i: (i, n))
    return pl.pallas_call(
        body, grid=(LRU_W // ct, T // tt),
        in_specs=[pl.BlockSpec((tt, ct), lambda n, i: (i, OFF_XC // ct + n)),
                  pl.BlockSpec((tt, ct), lambda n, i: (i, OFF_ZC // ct + n))] + _lru_param_specs(l),
        out_specs=[seq, seq],
        out_shape=[jax.ShapeDtypeStruct((T, LRU_W), F32), jax.ShapeDtypeStruct((T, LRU_W), F32)],
        scratch_shapes=[pltpu.VMEM((8, ct), F32), pltpu.VMEM((1, ct), F32), pltpu.VMEM((ct, ct), F32),
                        pltpu.VMEM((ct, ct), F32)],
        name=f"lru_fwd_l{l}", compiler_params=_params(("arbitrary", "arbitrary")))(
            proj, proj, conv_w, conv_b, wa, wx, ba, bx, lam)


def lru_bwd(proj, hseq, dy, conv_w, conv_b, wa, wx, ba, bx, lam, dproj, l):
    tt, ct = LRU_TT, LRU_TILE
    nt = T // tt
    rev = lambda i: nt - 1 - i
    prev8 = lambda i: jnp.maximum(rev(i) * (tt // 8) - 1, 0)

    def body(x_ref, xh_ref, z_ref, h_ref, hh_ref, dy_ref, cw_ref, cb_ref, wa_ref, wx_ref, ba_ref, bx_ref, lam_ref, _,
             dx_ref, dcw_ref, dcb_ref, dwa_ref, dwx_ref, dba_ref, dbx_ref, dlam_ref, gcar, dhalo,
             wa, wx, dwa_acc, dwx_acc):
        i = pl.program_id(1)
        first = i == 0

        @pl.when(first)
        def _():
            gcar[...] = jnp.zeros_like(gcar)
            dhalo[...] = jnp.zeros_like(dhalo)
            _blocks_to_dense(wa_ref, wa)
            _blocks_to_dense(wx_ref, wx)

        at_start = rev(i) == 0
        x = x_ref[...]
        xhalo = jnp.where(at_start, 0.0, xh_ref[...])
        sh = [x, _shift_down(x, 1, xhalo), _shift_down(x, 2, xhalo), _shift_down(x, 3, xhalo)]
        xc = (cw_ref[3:4, :] * sh[0] + cw_ref[2:3, :] * sh[1] + cw_ref[1:2, :] * sh[2] + cw_ref[0:1, :] * sh[3]
              + cb_ref[...])
        (a, b), vjp = jax.vjp(_lru_gates, xc, wa[...], wx[...], ba_ref[...], bx_ref[...], lam_ref[...])
        hs = h_ref[...]
        hprev = _shift_down(hs, 1, jnp.where(at_start, 0.0, hh_ref[...]))
        dh = dy_ref[...] * _silu(z_ref[...])
        a_next = _shift_up(a, 1, jnp.ones((8, ct), F32))
        g, _ = _scan(a_next, dh, True, gcar[...])
        dxc, dwa, dwx, dba, dbx, dlam = vjp((g * hprev, g))
        dx = (cw_ref[3:4, :] * dxc + cw_ref[2:3, :] * _shift_up(dxc, 1, dhalo[...])
              + cw_ref[1:2, :] * _shift_up(dxc, 2, dhalo[...]) + cw_ref[0:1, :] * _shift_up(dxc, 3, dhalo[...]))
        dx_ref[...] = dx.astype(BF16)
        dhalo[...] = dxc[0:8]
        ag = a * g
        gcar[...] = ag[0:1]
        dcw = jnp.concatenate([jnp.sum(dxc * sh[3 - j], axis=0, keepdims=True) for j in range(4)], axis=0)
        _acc(dcw_ref, dcw, first)
        _acc(dcb_ref, jnp.sum(dxc, axis=0, keepdims=True), first)
        _acc(dwa_acc, dwa, first)
        _acc(dwx_acc, dwx, first)

        @pl.when(i == nt - 1)
        def _():
            _dense_to_blocks(dwa_acc, dwa_ref)
            _dense_to_blocks(dwx_acc, dwx_ref)

        _acc(dba_ref, dba, first)
        _acc(dbx_ref, dbx, first)
        _acc(dlam_ref, dlam, first)

    xcol = OFF_XC // ct
    zcol = OFF_ZC // ct
    vec = pl.BlockSpec((1, ct), lambda n, i: (0, n))
    mat = pl.BlockSpec((8, 80, 80), lambda n, i: (n, 0, 0))
    seq = pl.BlockSpec((tt, ct), lambda n, i: (rev(i), n))
    return pl.pallas_call(
        body, grid=(LRU_W // ct, nt),
        in_specs=[pl.BlockSpec((tt, ct), lambda n, i: (rev(i), xcol + n)),
                  pl.BlockSpec((8, ct), lambda n, i: (prev8(i), xcol + n)),
                  pl.BlockSpec((tt, ct), lambda n, i: (rev(i), zcol + n)),
                  seq, pl.BlockSpec((8, ct), lambda n, i: (prev8(i), n)), seq] + _lru_param_specs(l) + [ANY],
        out_specs=[pl.BlockSpec((tt, ct), lambda n, i: (rev(i), xcol + n)),
                   pl.BlockSpec((4, ct), lambda n, i: (0, n)), vec, mat, mat, vec, vec, vec],
        out_shape=[jax.ShapeDtypeStruct((T, NPAD), BF16),
                   jax.ShapeDtypeStruct((4, LRU_W), F32), jax.ShapeDtypeStruct((1, LRU_W), F32),
                   jax.ShapeDtypeStruct((16, 80, 80), F32), jax.ShapeDtypeStruct((16, 80, 80), F32),
                   jax.ShapeDtypeStruct((1, LRU_W), F32), jax.ShapeDtypeStruct((1, LRU_W), F32),
                   jax.ShapeDtypeStruct((1, LRU_W), F32)],
        scratch_shapes=[pltpu.VMEM((1, ct), F32), pltpu.VMEM((8, ct), F32)] + [pltpu.VMEM((ct, ct), F32)] * 4,
        input_output_aliases={13: 0},
        name=f"lru_bwd_l{l}", compiler_params=_params(("arbitrary", "arbitrary")))(
            proj, proj, proj, hseq, hseq, dy, conv_w, conv_b, wa, wx, ba, bx, lam, dproj)


def proj_bwd(y, dp, w, l, tag, dep=None, dproj=None, gate=None):
    tm = 512
    k = y.shape[1]
    extra = [] if dep is None else [dep]
    in_specs = [pl.BlockSpec((tm, k), lambda i: (i, 0)), pl.BlockSpec((tm, D), lambda i: (i, 0)),
                pl.BlockSpec((None, k, D // 2), lambda i: (0, 0, 0))]
    out_specs = [pl.BlockSpec((tm, k), lambda i: (i, 0)), pl.BlockSpec((None, k, D), lambda i: (0, 0, 0))]
    out_shape = [jax.ShapeDtypeStruct((T, k), F32), jax.ShapeDtypeStruct((1, k, D), F32)]
    aliases = {}
    if gate is not None:
        in_specs += [pl.BlockSpec((tm, k), lambda i: (i, 0)), pl.BlockSpec((tm, k), lambda i: (i, OFF_ZC // k))]
        extra = list(gate) + extra
    if dproj is not None:
        width = k if gate is not None else PAD2
        at = OFF_ZC if gate is not None else OFF_XC - PAD2
        aliases = {3 + len(extra): 2}
        extra = extra + [dproj]
        out_specs.append(pl.BlockSpec((tm, width), lambda i: (i, at // width)))
        out_shape.append(jax.ShapeDtypeStruct((T, NPAD), BF16))
    in_specs += [ANY] * (3 + len(extra) - len(in_specs))

    def body(y_ref, dp_ref, w_ref, *rest):
        dy_ref, dw_ref = rest[len(extra):len(extra) + 2]
        dp = dp_ref[...]
        dy = _dg(dp, _unpack(w_ref[...]), _NT)
        dy_ref[...] = dy
        _acc(dw_ref, _dg(y_ref[...], dp, _TN), pl.program_id(0) == 0)
        if gate is not None:
            z = rest[1][...]
            sg = _sigmoid(z)
            rest[len(extra) + 2][...] = (dy * rest[0][...] * (sg * (1.0 + z * (1.0 - sg)))).astype(BF16)
        elif dproj is not None:
            rest[len(extra) + 2][...] = jnp.zeros((tm, PAD2), BF16)

    return pl.pallas_call(
        body, grid=(T // tm,), in_specs=in_specs, out_specs=out_specs, out_shape=out_shape,
        input_output_aliases=aliases,
        name=f"proj_{tag}_bwd_l{l}", compiler_params=_params(("arbitrary",)))(y, dp, w, *extra)


OUT_TM = 256


def _out_tile(pa, pb, pc, ga, gb, gc, wout, post_g):
    merged = _sigmoid(ga) * pa + _sigmoid(gb) * pb + _sigmoid(gc) * pc
    return _rms(dot_nn(merged, wout), post_g)


def _out_in_specs():
    tm = OUT_TM
    tok = pl.BlockSpec((tm, D), lambda i: (i, 0))
    gate = lambda off: pl.BlockSpec((tm, 512), lambda i, off=off: (i, off // 512))
    return [tok, tok, tok, gate(OFF_GA), gate(OFF_GA + 512), gate(OFF_GB), gate(OFF_GB + 512), gate(OFF_GC),
            gate(OFF_GC + 512), pl.BlockSpec((None, D, D // 2), lambda i: (0, 0, 0)), pl.BlockSpec((1, D), lambda i: (0, 0))]


def _gates(refs):
    return [jnp.concatenate([refs[2 * j][...], refs[2 * j + 1][...]], axis=1) for j in range(3)]


def out_fwd(x, ya, yb, yc, proj, wpa, wpb, wpc, wout, post_g, l):
    tm = OUT_TM

    def body(ya_ref, yb_ref, yc_ref, g0, g1, g2, g3, g4, g5, wo_ref, pg_ref, x_ref, wa_ref, wb_ref, wc_ref,
             o_ref, pa_ref, pb_ref, pc_ref, wa, wb, wc, wo):
        @pl.when(pl.program_id(0) == 0)
        def _():
            for dst, src in ((wa, wa_ref), (wb, wb_ref), (wc, wc_ref), (wo, wo_ref)):
                dst[...] = _unpack(src[...]).astype(BF16)

        pa = _dg(ya_ref[...], wa[...], _NN)
        pb = _dg(yb_ref[...], wb[...], _NN)
        pc = _dg(yc_ref[...], wc[...], _NN)
        ga, gb, gc = _gates([g0, g1, g2, g3, g4, g5])
        o_ref[...] = x_ref[...] + _out_tile(pa, pb, pc, ga, gb, gc, wo[...], pg_ref[...])
        pa_ref[...] = pa.astype(BF16)
        pb_ref[...] = pb.astype(BF16)
        pc_ref[...] = pc.astype(BF16)

    tok = pl.BlockSpec((tm, D), lambda i: (i, 0))
    words = lambda k: pl.BlockSpec((None, k, D // 2), lambda i: (0, 0, 0))
    specs = _out_in_specs()
    specs[2] = pl.BlockSpec((tm, LRU_W), lambda i: (i, 0))
    return pl.pallas_call(
        body, grid=(T // tm,), in_specs=specs + [tok, words(D), words(D), words(LRU_W)], out_specs=[tok] * 4,
        out_shape=[jax.ShapeDtypeStruct((T, D), F32)] + [jax.ShapeDtypeStruct((T, D), BF16)] * 3,
        scratch_shapes=[pltpu.VMEM((D, D), BF16), pltpu.VMEM((D, D), BF16), pltpu.VMEM((LRU_W, D), BF16),
                        pltpu.VMEM((D, D), BF16)],
        name=f"out_fwd_l{l}", compiler_params=_params(("arbitrary",)))(
            ya, yb, yc, proj, proj, proj, proj, proj, proj, wout, post_g, x, wpa, wpb, wpc)


def out_bwd(pa, pb, pc, proj, wout, post_g, dxn, l, dep=None):
    tm = OUT_TM
    nsteps = T // tm

    def body(pa_ref, pb_ref, pc_ref, g0, g1, g2, g3, g4, g5, w_ref, pg_ref, dxn_ref, *rest):
        dpa_ref, dpb_ref, dpc_ref, dproj_ref, dw_ref, dpg_ref, gbuf, sem = rest[-8:]
        i = pl.program_id(0)
        first = i == 0
        slot = i % 2
        ga, gb, gc = _gates([g0, g1, g2, g3, g4, g5])
        _, vjp = jax.vjp(_out_tile, pa_ref[...], pb_ref[...], pc_ref[...], ga, gb, gc, _unpack(w_ref[...]), pg_ref[...])
        dpa, dpb, dpc, dga, dgb, dgc, dw, dpg = vjp(dxn_ref[...])
        dpa_ref[...] = dpa.astype(BF16)
        dpb_ref[...] = dpb.astype(BF16)
        dpc_ref[...] = dpc.astype(BF16)
        _acc(dw_ref, dw, first)
        _acc(dpg_ref, dpg, first)

        def writeback(step, s):
            rows = pl.ds(pl.multiple_of(step * tm, tm), tm)
            return pltpu.make_async_copy(gbuf.at[s], dproj_ref.at[rows, pl.ds(OFF_GA, 3072)], sem.at[s])

        gbuf[slot, :, 0:1024] = dga.astype(BF16)
        gbuf[slot, :, 1024:2048] = dgb.astype(BF16)
        gbuf[slot, :, 2048:3072] = dgc.astype(BF16)
        writeback(i, slot).start()

        @pl.when(i > 0)
        def _():
            writeback(i - 1, 1 - slot).wait()

        @pl.when(i == nsteps - 1)
        def _():
            writeback(i, slot).wait()

    tok = pl.BlockSpec((tm, D), lambda i: (i, 0))
    deps = [] if dep is None else [dep]
    return pl.pallas_call(
        body, grid=(nsteps,), in_specs=_out_in_specs() + [tok] + [ANY] * len(deps),
        out_specs=[tok, tok, tok, ANY, pl.BlockSpec((None, D, D), lambda i: (0, 0, 0)), pl.BlockSpec((1, D), lambda i: (0, 0))],
        out_shape=[jax.ShapeDtypeStruct((T, D), BF16)] * 3 + [jax.ShapeDtypeStruct((T, NPAD), BF16),
                                                            jax.ShapeDtypeStruct((1, D, D), F32), jax.ShapeDtypeStruct((1, D), F32)],
        scratch_shapes=[pltpu.VMEM((2, tm, 3072), BF16), pltpu.SemaphoreType.DMA((2,))],
        name=f"out_bwd_l{l}", compiler_params=_params(("arbitrary",)))(
            pa, pb, pc, proj, proj, proj, proj, proj, proj, wout, post_g, dxn, *deps)


def loss_head(y, target):
    tm = 256

    def body(y_ref, t_ref, loss_ref, dy_ref):
        e = y_ref[...] - t_ref[...]
        dy_ref[...] = e * (1.0 / D)
        val = 0.5 * jnp.sum(jnp.mean(e * e, axis=-1, keepdims=True), axis=0, keepdims=True)
        _acc(loss_ref, jnp.broadcast_to(val, (8, 128)), pl.program_id(0) == 0)

    tok = pl.BlockSpec((tm, D), lambda i: (i, 0))
    total, dy = pl.pallas_call(
        body, grid=(T // tm,), in_specs=[tok, tok],
        out_specs=[pl.BlockSpec((8, 128), lambda i: (0, 0)), tok],
        out_shape=[jax.ShapeDtypeStruct((8, 128), F32), jax.ShapeDtypeStruct((T, D), F32)],
        name="loss_head", compiler_params=_params(("arbitrary",)))(y, target)
    return total[0, 0], dy


def _rope_tables():
    pos = jnp.arange(T, dtype=F32)
    inv_freq = 10000.0 ** (-jnp.arange(0, 64, 2, dtype=F32) / 64)
    ang = pos[:, None] * inv_freq[None, :]
    cos, sin = jnp.cos(ang), jnp.sin(ang)
    ctab = jnp.concatenate([jnp.ones((T, 128), F32), cos, cos], axis=1)
    stab = jnp.concatenate([jnp.zeros((T, 128), F32), -sin, sin], axis=1)
    return ctab, stab


def _layer_fwd(x, l, w, gw, tabs, dep=None, mid=None):
    row = lambda a: a[l][None]
    proj, h = inproj_fwd(x, row(w["pre_norm_g"]), gw["w_in_t"], l, dep)
    ya = gmlp_fwd(proj, row(w["gm_ln_g"]), row(w["gm_ln_b"]), w["gm_ws"][l], w["gm_bs"][l][..., None], l)
    dep2 = None
    if mid is not None:
        gw, dep2 = mid(ya)
    q, k, v = qkv_fwd(proj, row(w["mla_q_norm_g"]), row(w["kv_g384"]), gw["wq"], gw["wkv"], tabs[0], tabs[1], l, dep2)
    yb = attn_fwd(q, k, v, proj, l)
    hseq, yc = lru_fwd(proj, gw["conv"], row(w["lru_conv_b"]), w["lru_w_a"], w["lru_w_x"],
                       row(w["lru_b_a"]), row(w["lru_b_x"]), row(w["lru_lambda"]), l)
    xn, pa, pb, pc = out_fwd(x, ya, yb, yc, proj, gw["w_proj_a"], gw["w_proj_b"], gw["w_proj_c"], gw["w_out"],
                             row(w["post_norm_g"]), l)
    return xn, (x, proj, h, ya, q, k, v, yb, hseq, yc, pa, pb, pc)


def _layer_bwd(dxn, l, w, gw, tabs, saved, dep=None, early=None, mid=None, late=None):
    x, proj, h, ya, q, k, v, yb, hseq, yc, pa, pb, pc = saved
    row = lambda a: a[l][None]
    g, gg = {}, {}
    dpa, dpb, dpc, dproj, gg["w_out"], dpost = out_bwd(pa, pb, pc, proj, gw["w_out"], row(w["post_norm_g"]), dxn, l, dep)
    g["post_norm_g"] = dpost[0]
    dep1 = early(dpa) if early is not None else None
    dya, gg["w_proj_a"], dproj = proj_bwd(ya, dpa, gw["w_proj_a"], l, "a", dep1, dproj)
    dyb, gg["w_proj_b"] = proj_bwd(yb, dpb, gw["w_proj_b"], l, "b")
    dyc, gg["w_proj_c"], dproj = proj_bwd(yc, dpc, gw["w_proj_c"], l, "c", None, dproj, (hseq, proj))
    dproj, dln_g, dln_b, g["gm_ws"], dbs = gmlp_bwd(proj, row(w["gm_ln_g"]), row(w["gm_ln_b"]), w["gm_ws"][l],
                                                   w["gm_bs"][l][..., None], dya, dproj, l)
    g["gm_ln_g"], g["gm_ln_b"], g["gm_bs"] = dln_g[0], dln_b[0], dbs[..., 0]
    dq, dk, dv, dproj = attn_bwd(q, k, v, proj, dyb, dproj, l)
    dproj, dqg, dkvg, dwq, dwkv = qkv_bwd(proj, row(w["mla_q_norm_g"]), row(w["kv_g384"]), gw["wq"], gw["wkv"],
                                          tabs[0], tabs[1], dq, dk, dv, dproj, l)
    gg["wq"], gg["wkv"] = dwq.reshape(1, 1536, 384), dwkv.reshape(1, 2048, 256)
    g["mla_q_norm_g"], g["mla_kv_norm_g"] = dqg[0], dkvg[0, :256]
    dproj, dcw, dcb, dwa, dwx, dba, dbx, dlam = lru_bwd(
        proj, hseq, dyc, gw["conv"], row(w["lru_conv_b"]), w["lru_w_a"], w["lru_w_x"],
        row(w["lru_b_a"]), row(w["lru_b_x"]), row(w["lru_lambda"]), dproj, l)
    gg["conv"] = jnp.pad(dcw.T, ((0, 0), (0, 124)))[None]
    g["lru_conv_b"], g["lru_b_a"], g["lru_b_x"], g["lru_lambda"] = dcb[0], dba[0], dbx[0], dlam[0]
    g["lru_w_a"], g["lru_w_x"] = dwa, dwx
    dep2 = mid(gg, dproj) if mid is not None else None
    gg["w_in_t"], dh = inproj_bwd(dproj, h, gw["w_in_t"], l, dep2)
    dep3 = late(gg["w_in_t"]) if late is not None else None
    dx, dpre = prenorm_bwd(x, row(w["pre_norm_g"]), dh, dxn, l, dep3)
    g["pre_norm_g"] = dpre[0]
    return dx, gg, g


MESH = pl.DeviceIdType.MESH
HBM = pl.BlockSpec(memory_space=pltpu.HBM)
SEM = pl.BlockSpec(memory_space=pltpu.SEMAPHORE)
EFFECT = pltpu.SideEffectType.DATAFLOW_SIDE_EFFECTING
FLIPS = ((1, 0), (0, 1), (1, 1))


def _win_off(k, s):
    g = SHARD * k + s
    return g + jnp.where(g >= PAD1_AT, PAD1, 0) + jnp.where(g >= PAD2_AT, PAD2, 0)


def _plain_off(rows):
    return lambda k, s: rows * k + s


class Spec:
    def __init__(self, rows, cols, full_rows, pieces=None, off=None, layers=1, packed=None):
        self.rows, self.cols, self.full_rows, self.layers = rows, cols, full_rows, layers
        self.pieces = pieces or ((0, rows),)
        self.off = off or _plain_off(rows)
        self.packed = cols % 256 == 0 if packed is None else packed
        self.wcols = cols // 2 if self.packed else cols

    def to_words(self, a):
        return _pack(a) if self.packed else a

    def from_words(self, p):
        return _unpack(p) if self.packed else p


def _pack(a):
    def bits(v):
        u = lax.bitcast_convert_type(v, jnp.uint32)
        return u + jnp.uint32(0x7FFF) + ((u >> 16) & jnp.uint32(1))

    words = [(bits(a[:, g:g + 128]) >> 16) | (bits(a[:, g + 128:g + 256]) & jnp.uint32(0xFFFF0000))
             for g in range(0, a.shape[-1], 256)]
    return lax.bitcast_convert_type(jnp.concatenate(words, axis=-1) if len(words) > 1 else words[0], F32)


def _unpack(p):
    w = lax.bitcast_convert_type(p, jnp.uint32)
    lo = lax.bitcast_convert_type(w << 16, F32)
    hi = lax.bitcast_convert_type(w & jnp.uint32(0xFFFF0000), F32)
    return jnp.concatenate([h[:, g:g + 128] for g in range(0, p.shape[-1], 128) for h in (lo, hi)], axis=-1)


WEIGHT_SPECS = {
    "w_in_t": Spec(SHARD, D, NPAD, WIN_PIECES, _win_off),
    "wq": Spec(192, 384, 1536),
    "wkv": Spec(256, 256, 2048),
    "conv": Spec(160, 128, 1280),
    "w_proj_a": Spec(128, D, 1024),
    "w_proj_b": Spec(128, D, 1024),
    "w_proj_c": Spec(160, D, 1280),
    "w_out": Spec(128, D, 1024),
}
REP_ROWS = 72
REP_SPEC = Spec(REP_ROWS, D, REP_ROWS * NDEV, packed=False)


def _coords():
    return lax.axis_index("x"), lax.axis_index("y"), lax.axis_index("c")


def _rows(ref, start, n):
    if not isinstance(start, int):
        start = pl.multiple_of(start, 8)
    return ref.at[:, pl.ds(start, n), :]


def _col_tile(cols):
    return 256 if cols % 256 == 0 else cols


def _n_pieces(specs):
    return sum(len(sp.pieces) for sp in specs)


def pack_place(shard, sp, layer, tag, dep=None):
    gaps = ((PAD1_AT, PAD1), (PAD2_AT + PAD1, PAD2)) if sp.off is _win_off else ()
    npc = len(sp.pieces)
    deps = [] if dep is None else [dep]

    def body(s_ref, *rest):
        words_ref, full_ref, buf, zbuf, sem = rest[-5:]
        l = 0
        x, y, c = _coords()
        me = 4 * x + 2 * y + c
        words = sp.to_words(s_ref[...])
        words_ref[...] = words
        buf[...] = words
        copies = [pltpu.make_async_copy(buf.at[pl.ds(s, n), :],
                                        full_ref.at[l, pl.ds(pl.multiple_of(sp.off(me, s), 8), n), :], sem.at[i])
                  for i, (s, n) in enumerate(sp.pieces)]
        if gaps:
            zbuf[...] = jnp.zeros_like(zbuf)
            copies += [pltpu.make_async_copy(zbuf.at[pl.ds(0, n), :], full_ref.at[l, pl.ds(at, n), :], sem.at[npc + i])
                       for i, (at, n) in enumerate(gaps)]
        for cp in copies:
            cp.start()
        for cp in copies:
            cp.wait()

    return pl.pallas_call(
        body, grid=(1,), in_specs=[pl.BlockSpec((None, sp.rows, sp.cols), lambda i: (layer, 0, 0))] + [ANY] * len(deps),
        out_specs=[pl.BlockSpec((None, sp.rows, sp.wcols), lambda i: (0, 0, 0)), ANY],
        out_shape=[jax.ShapeDtypeStruct((sp.layers, sp.rows, sp.wcols), F32),
                   jax.ShapeDtypeStruct((sp.layers, sp.full_rows, sp.wcols), F32)],
        scratch_shapes=[pltpu.VMEM((sp.rows, sp.wcols), F32), pltpu.VMEM((PAD2 if gaps else 8, sp.wcols), F32),
                        pltpu.SemaphoreType.DMA((npc + len(gaps),))],
        name=f"pack_place_{tag}", compiler_params=_params(("arbitrary",)))(shard, *deps)


def _gather_copies(srcs, bufs, specs, ssem, rsem, landing):
    x, y, c = _coords()
    me = 4 * x + 2 * y + c
    targets = [(x, y, 1 - c)] + [(x ^ fx, y ^ fy, c) for fx, fy in FLIPS]
    copies = []
    p = 0
    for src, buf, sp in zip(srcs, bufs, specs):
        for s, n in sp.pieces:
            for t, (tx, ty, tc) in enumerate(targets):
                owner = 4 * tx + 2 * ty + tc if landing else me
                copies.append(pltpu.make_async_remote_copy(_rows(src, s, n), _rows(buf, sp.off(owner, s), n),
                                                           ssem.at[4 * p + t], rsem.at[4 * p + t],
                                                           device_id=(tx, ty, tc), device_id_type=MESH))
            p += 1
    return copies


def gather_send(words, fulls, specs, tag):
    ns, npc = len(specs), _n_pieces(specs)

    def body(*refs):
        srcs, bufs, sems = refs[:ns], refs[2 * ns:3 * ns], refs[3 * ns:]
        for cp in _gather_copies(srcs, bufs, specs, *sems, False):
            cp.start()
        for cp in _gather_copies(srcs, bufs, specs, *sems, False):
            cp.wait_send()
        for cp in _gather_copies(srcs, bufs, specs, *sems, True):
            cp.wait_recv()

    return pl.pallas_call(
        body, in_specs=[ANY] * (2 * ns), out_specs=[ANY] * ns,
        out_shape=[jax.ShapeDtypeStruct(f.shape, f.dtype) for f in fulls],
        input_output_aliases={ns + i: i for i in range(ns)},
        scratch_shapes=[pltpu.SemaphoreType.DMA((4 * npc,)), pltpu.SemaphoreType.DMA((4 * npc,))],
        name=f"gather_send_{tag}", compiler_params=pltpu.CompilerParams(has_side_effects=True))(*words, *fulls)


def _in_hbm(arrays):
    return [pltpu.with_memory_space_constraint(a, pltpu.HBM) for a in arrays]


def gather_start(words, fulls, specs, dep, tag):
    ns, npc = len(specs), _n_pieces(specs)
    deps = [] if dep is None else [dep]

    def body(*refs):
        ssem, rsem = refs[2 * ns + len(deps):2 * ns + len(deps) + 2]
        for cp in _gather_copies(refs[:ns], refs[ns:2 * ns], specs, ssem, rsem, False):
            cp.start()
        refs[-1][...] = jnp.zeros_like(refs[-1])

    outs = pl.pallas_call(
        body, in_specs=[HBM] * (2 * ns) + [ANY] * len(deps),
        out_specs=[SEM, SEM] + [HBM] * (2 * ns) + [pl.BlockSpec(memory_space=pltpu.VMEM)],
        out_shape=[pltpu.SemaphoreType.DMA((4 * npc,)), pltpu.SemaphoreType.DMA((4 * npc,))]
        + [pltpu.HBM(a.shape, a.dtype) for a in list(words) + list(fulls)] + [jax.ShapeDtypeStruct((8, 128), F32)],
        input_output_aliases={i: 2 + i for i in range(2 * ns)},
        name=f"gather_start_{tag}", compiler_params=pltpu.CompilerParams(has_side_effects=EFFECT))(
            *_in_hbm(list(words) + list(fulls)), *deps)
    return outs[0], outs[1], outs[2:2 + ns], outs[2 + ns:2 + 2 * ns], outs[-1]


def gather_wait(ssem, rsem, words, fulls, specs, after, tag):
    ns = len(specs)

    def body(*refs):
        srcs, bufs, ssem, rsem = refs[:ns], refs[ns:2 * ns], refs[2 * ns], refs[2 * ns + 1]
        for cp in _gather_copies(srcs, bufs, specs, ssem, rsem, False):
            cp.wait_send()
        for cp in _gather_copies(srcs, bufs, specs, ssem, rsem, True):
            cp.wait_recv()

    outs = pl.pallas_call(
        body, in_specs=[HBM] * (2 * ns) + [SEM, SEM, ANY], out_specs=[HBM] * (2 * ns),
        out_shape=[pltpu.HBM(a.shape, a.dtype) for a in list(words) + list(fulls)],
        input_output_aliases={i: i for i in range(2 * ns)},
        name=f"gather_wait_{tag}", compiler_params=pltpu.CompilerParams(has_side_effects=EFFECT))(
            *words, *fulls, ssem, rsem, after)
    return outs[ns:]


def gather_forward(fulls, specs, tag):
    ns, npc = len(specs), _n_pieces(specs)

    def body(*refs):
        bufs = refs[ns:2 * ns]
        ssem, rsem = refs[2 * ns:]
        x, y, c = _coords()
        sibling = (x, y, 1 - c)
        waits = []
        p = 0
        for buf, sp in zip(bufs, specs):
            for s, n in sp.pieces:
                for t, (fx, fy) in enumerate(FLIPS):
                    chip = 4 * (x ^ fx) + 2 * (y ^ fy)
                    here = _rows(buf, sp.off(chip + c, s), n)
                    send = pltpu.make_async_remote_copy(here, here, ssem.at[t, p], rsem.at[t, p],
                                                        device_id=sibling, device_id_type=MESH)
                    send.start()
                    waits.append(send.wait_send)
                    there = _rows(buf, sp.off(chip + 1 - c, s), n)
                    waits.append(pltpu.make_async_remote_copy(here, there, ssem.at[t, p], rsem.at[t, p],
                                                              device_id=sibling, device_id_type=MESH).wait_recv)
                p += 1
        for w in waits:
            w()

    return pl.pallas_call(
        body, in_specs=[ANY] * ns, out_specs=[ANY] * ns,
        out_shape=[jax.ShapeDtypeStruct(f.shape, f.dtype) for f in fulls],
        input_output_aliases={i: i for i in range(ns)},
        scratch_shapes=[pltpu.SemaphoreType.DMA((3, npc)), pltpu.SemaphoreType.DMA((3, npc))],
        name=f"gather_forward_{tag}", compiler_params=pltpu.CompilerParams(has_side_effects=True))(*fulls)


def all_gather(shards, layer, specs, names, tag):
    placed = [pack_place(s, sp, layer, f"{tag}_{n}") for s, sp, n in zip(shards, specs, names)]
    fulls = gather_send([p[0] for p in placed], [p[1] for p in placed], specs, tag)
    return gather_forward(fulls, specs, tag)


def _pair_copies(srcs, theirs, specs, ssem, rsem):
    x, y, c = _coords()
    copies = []
    p = 0
    for src, their, sp in zip(srcs, theirs, specs):
        for s, n in sp.pieces:
            for j in range(4):
                copies.append(pltpu.make_async_remote_copy(_rows(src, sp.off(2 * j + 1 - c, s), n), _rows(their.at[j], s, n),
                                                           ssem.at[4 * p + j], rsem.at[4 * p + j],
                                                           device_id=(x, y, 1 - c), device_id_type=MESH))
            p += 1
    return copies


def _pair_shapes(specs):
    return [(4, sp.layers, sp.rows, sp.cols) for sp in specs]


def reduce_pair(grads, specs, tag, dep=None):
    ns, npc = len(specs), _n_pieces(specs)
    deps = [] if dep is None else [dep]

    def body(*refs):
        copies = _pair_copies(refs[:ns], refs[ns + len(deps):2 * ns + len(deps)], specs, *refs[2 * ns + len(deps):])
        for cp in copies:
            cp.start()
        for cp in copies:
            cp.wait()

    return pl.pallas_call(
        body, in_specs=[ANY] * (ns + len(deps)), out_specs=[ANY] * ns,
        out_shape=[jax.ShapeDtypeStruct(s, F32) for s in _pair_shapes(specs)],
        scratch_shapes=[pltpu.SemaphoreType.DMA((4 * npc,)), pltpu.SemaphoreType.DMA((4 * npc,))],
        name=f"reduce_pair_{tag}", compiler_params=pltpu.CompilerParams(has_side_effects=True))(*grads, *deps)


def pair_start(grads, specs, dep, tag):
    ns, npc = len(specs), _n_pieces(specs)
    slots = [lax.empty(s, F32) for s in _pair_shapes(specs)]
    deps = [] if dep is None else [dep]

    def body(*refs):
        ssem, rsem = refs[2 * ns + len(deps):2 * ns + len(deps) + 2]
        for cp in _pair_copies(refs[:ns], refs[ns:2 * ns], specs, ssem, rsem):
            cp.start()
        refs[-1][...] = jnp.zeros_like(refs[-1])

    outs = pl.pallas_call(
        body, in_specs=[HBM] * (2 * ns) + [ANY] * len(deps),
        out_specs=[SEM, SEM] + [HBM] * (2 * ns) + [pl.BlockSpec(memory_space=pltpu.VMEM)],
        out_shape=[pltpu.SemaphoreType.DMA((4 * npc,)), pltpu.SemaphoreType.DMA((4 * npc,))]
        + [pltpu.HBM(a.shape, a.dtype) for a in list(grads) + slots] + [jax.ShapeDtypeStruct((8, 128), F32)],
        input_output_aliases={i: 2 + i for i in range(2 * ns)},
        name=f"pair_start_{tag}", compiler_params=pltpu.CompilerParams(has_side_effects=EFFECT))(
            *_in_hbm(list(grads) + slots), *deps)
    return outs[0], outs[1], outs[2:2 + ns], outs[2 + ns:2 + 2 * ns], outs[-1]


def pair_wait(ssem, rsem, grads, slots, specs, after, tag):
    ns = len(specs)

    def body(*refs):
        for cp in _pair_copies(refs[:ns], refs[ns:2 * ns], specs, refs[2 * ns], refs[2 * ns + 1]):
            cp.wait_send()
            cp.wait_recv()

    outs = pl.pallas_call(
        body, in_specs=[HBM] * (2 * ns) + [SEM, SEM, ANY], out_specs=[HBM] * (2 * ns),
        out_shape=[pltpu.HBM(a.shape, a.dtype) for a in list(grads) + list(slots)],
        input_output_aliases={i: i for i in range(2 * ns)},
        name=f"pair_wait_{tag}", compiler_params=pltpu.CompilerParams(has_side_effects=EFFECT))(
            *grads, *slots, ssem, rsem, after)
    return outs[:ns], outs[ns:]


def pair_sum(g, r1, sp, tag):
    npc = len(sp.pieces)
    fetch_all = 4 * sp.rows * sp.cols * 4 <= (8 << 20)

    def body(g_ref, r_ref, own_ref, words_ref, gbuf, sem):
        l, j = pl.program_id(0), pl.program_id(1)
        x, y, c = _coords()

        def copies(chip, slot):
            return [pltpu.make_async_copy(g_ref.at[l, pl.ds(pl.multiple_of(sp.off(2 * chip + c, s), 8), n), :],
                                          gbuf.at[slot, pl.ds(s, n), :], sem.at[slot, i])
                    for i, (s, n) in enumerate(sp.pieces)]

        def fetch(chip, slot):
            for cp in copies(chip, slot):
                cp.start()

        def arrived(chip, slot):
            for cp in copies(chip, slot):
                cp.wait()

        if fetch_all:
            @pl.when(j == 0)
            def _():
                for chip in range(4):
                    fetch(chip, chip)
                for chip in range(4):
                    arrived(chip, chip)

            mine = gbuf[j]
        else:
            @pl.when(j == 0)
            def _():
                fetch(0, 0)

            @pl.when(j < 3)
            def _():
                fetch(j + 1, (j + 1) % 2)

            arrived(j, j % 2)
            mine = gbuf[j % 2]
        p = mine + r_ref[...]
        words_ref[...] = sp.to_words(p)

        @pl.when(j == 2 * x + y)
        def _():
            own_ref[...] = p

    return pl.pallas_call(
        body, grid=(sp.layers, 4),
        in_specs=[ANY, pl.BlockSpec((None, None, sp.rows, sp.cols), lambda l, j: (j, l, 0, 0))],
        out_specs=[pl.BlockSpec((None, sp.rows, sp.cols), lambda l, j: (l, 0, 0)),
                   pl.BlockSpec((None, None, sp.rows, sp.wcols), lambda l, j: (j, l, 0, 0))],
        out_shape=[jax.ShapeDtypeStruct((sp.layers, sp.rows, sp.cols), F32),
                   jax.ShapeDtypeStruct((4, sp.layers, sp.rows, sp.wcols), F32)],
        scratch_shapes=[pltpu.VMEM((4 if fetch_all else 2, sp.rows, sp.cols), F32), pltpu.SemaphoreType.DMA((4, npc))],
        name=f"pair_sum_{tag}", compiler_params=_params(("arbitrary", "arbitrary")))(g, r1)


def _chip_copies(srcs, dsts, ssem, rsem):
    x, y, c = _coords()
    copies = []
    for i, (src, dst) in enumerate(zip(srcs, dsts)):
        for t, (fx, fy) in enumerate(FLIPS):
            tx, ty = x ^ fx, y ^ fy
            copies.append(pltpu.make_async_remote_copy(src.at[2 * tx + ty], dst.at[t], ssem.at[3 * i + t], rsem.at[3 * i + t],
                                                       device_id=(tx, ty, c), device_id_type=MESH))
    return copies


def _slot_shapes(words):
    return [(3,) + w.shape[1:] for w in words]


def reduce_chips(words, specs, tag):
    ns = len(specs)

    def body(*refs):
        copies = _chip_copies(refs[:ns], refs[ns:2 * ns], *refs[2 * ns:])
        for cp in copies:
            cp.start()
        for cp in copies:
            cp.wait()

    return pl.pallas_call(
        body, in_specs=[ANY] * ns, out_specs=[ANY] * ns,
        out_shape=[jax.ShapeDtypeStruct(s, F32) for s in _slot_shapes(words)],
        scratch_shapes=[pltpu.SemaphoreType.DMA((3 * ns,)), pltpu.SemaphoreType.DMA((3 * ns,))],
        name=f"reduce_chips_{tag}", compiler_params=pltpu.CompilerParams(has_side_effects=True))(*words)


def chips_start(words, specs, tag):
    ns = len(specs)
    slots = [lax.empty(s, F32) for s in _slot_shapes(words)]

    def body(*refs):
        ssem, rsem = refs[2 * ns:2 * ns + 2]
        for cp in _chip_copies(refs[:ns], refs[ns:2 * ns], ssem, rsem):
            cp.start()
        refs[-1][...] = jnp.zeros_like(refs[-1])

    outs = pl.pallas_call(
        body, in_specs=[HBM] * (2 * ns),
        out_specs=[SEM, SEM] + [HBM] * (2 * ns) + [pl.BlockSpec(memory_space=pltpu.VMEM)],
        out_shape=[pltpu.SemaphoreType.DMA((3 * ns,)), pltpu.SemaphoreType.DMA((3 * ns,))]
        + [pltpu.HBM(a.shape, a.dtype) for a in list(words) + slots] + [jax.ShapeDtypeStruct((8, 128), F32)],
        input_output_aliases={i: 2 + i for i in range(2 * ns)},
        name=f"chips_start_{tag}", compiler_params=pltpu.CompilerParams(has_side_effects=EFFECT))(
            *_in_hbm(list(words) + slots))
    return outs[0], outs[1], outs[2:2 + ns], outs[2 + ns:2 + 2 * ns], outs[-1]


def chips_wait(ssem, rsem, words, slots, specs, after, tag):
    ns = len(specs)

    def body(*refs):
        for cp in _chip_copies(refs[:ns], refs[ns:2 * ns], refs[2 * ns], refs[2 * ns + 1]):
            cp.wait_send()
            cp.wait_recv()

    outs = pl.pallas_call(
        body, in_specs=[HBM] * (2 * ns) + [SEM, SEM, ANY], out_specs=[HBM] * (2 * ns),
        out_shape=[pltpu.HBM(a.shape, a.dtype) for a in list(words) + list(slots)],
        input_output_aliases={i: i for i in range(2 * ns)},
        name=f"chips_wait_{tag}", compiler_params=pltpu.CompilerParams(has_side_effects=EFFECT))(
            *words, *slots, ssem, rsem, after)
    return outs[ns:]


def sum_chips(own, r2, sp, tag):
    def body(own_ref, r_ref, o_ref):
        o_ref[...] = ((own_ref[...] + sp.from_words(r_ref[0])) + sp.from_words(r_ref[1])) + sp.from_words(r_ref[2])

    blk = pl.BlockSpec((None, sp.rows, sp.cols), lambda l: (l, 0, 0))
    return pl.pallas_call(
        body, grid=(sp.layers,), in_specs=[blk, pl.BlockSpec((3, None, sp.rows, sp.wcols), lambda l: (0, l, 0, 0))],
        out_specs=blk, out_shape=jax.ShapeDtypeStruct((sp.layers, sp.rows, sp.cols), F32),
        name=f"sum_chips_{tag}", compiler_params=_params(("arbitrary",)))(own, r2)


def reduce_scatter_start(grads, specs, names, dep, tag):
    theirs = reduce_pair(grads, specs, tag, dep)
    sums = [pair_sum(g, r1, sp, f"{tag}_{n}") for g, r1, sp, n in zip(grads, theirs, specs, names)]
    ssem, rsem, words, slots, token = chips_start([s[1] for s in sums], specs, tag)
    return (ssem, rsem, words, slots, [s[0] for s in sums]), token


def reduce_scatter_finish(state, after, specs, tag):
    ssem, rsem, words, slots, own = state
    return list(zip(own, chips_wait(ssem, rsem, words, slots, specs, after, tag)))


def reduce_scatter(grads, specs, names, tag, dep=None):
    theirs = reduce_pair(grads, specs, tag, dep)
    sums = [pair_sum(g, r1, sp, f"{tag}_{n}") for g, r1, sp, n in zip(grads, theirs, specs, names)]
    return list(zip([s[0] for s in sums], reduce_chips([s[1] for s in sums], specs, tag)))


def _adamw_math(w, g, m, v):
    c1 = 1.0 - ADAM_B1 ** ADAM_STEP
    c2 = 1.0 - ADAM_B2 ** ADAM_STEP
    m2 = ADAM_B1 * m + (1.0 - ADAM_B1) * g
    v2 = ADAM_B2 * v + (1.0 - ADAM_B2) * (g * g)
    return -ADAM_LR * ((m2 / c1) / (jnp.sqrt(v2 / c2) + ADAM_EPS) + ADAM_WD * w), m2, v2


def adamw_small(ws, gs, ms, vs):
    n = len(ws)
    flat = lambda a: a.reshape(math.prod(a.shape[:-1]), a.shape[-1])

    def body(*refs):
        ins, outs = refs[:4 * n], refs[4 * n:]
        for i in range(n):
            w_ref, g_ref, m_ref, v_ref = ins[4 * i:4 * i + 4]
            outs[3 * i][...], outs[3 * i + 1][...], outs[3 * i + 2][...] = _adamw_math(
                w_ref[...], g_ref[...], m_ref[...], v_ref[...])

    args = [flat(a) for quad in zip(ws, gs, ms, vs) for a in quad]
    res = pl.pallas_call(
        body, out_shape=[jax.ShapeDtypeStruct(flat(w).shape, F32) for w in ws for _ in range(3)],
        name="adamw_small", compiler_params=_params())(*args)
    return [[res[3 * i + k].reshape(ws[i].shape) for k in range(3)] for i in range(n)]


def adamw_layer(w, sums, m, v, sp, l, prev, dep, name):
    _, rows, cols = w.shape
    tc = _col_tile(cols)
    twc = tc // 2 if sp.packed else tc
    extra = ([] if prev is None else list(prev)) + ([] if dep is None else [dep])

    def body(w_ref, own_ref, r_ref, m_ref, v_ref, *rest):
        g_ref, d_ref, nm_ref, nv_ref = rest[-4:]
        g = ((own_ref[...] + sp.from_words(r_ref[0])) + sp.from_words(r_ref[1])) + sp.from_words(r_ref[2])
        g_ref[...] = g
        d_ref[...], nm_ref[...], nv_ref[...] = _adamw_math(w_ref[...], g, m_ref[...], v_ref[...])

    blk = pl.BlockSpec((None, rows, tc), lambda n: (l, 0, n))
    return pl.pallas_call(
        body, grid=(cols // tc,),
        in_specs=[blk, pl.BlockSpec((None, rows, tc), lambda n: (0, 0, n)),
                  pl.BlockSpec((3, None, rows, twc), lambda n: (0, 0, 0, n)), blk, blk] + [ANY] * len(extra),
        out_specs=[blk] * 4, out_shape=[jax.ShapeDtypeStruct(w.shape, F32)] * 4,
        input_output_aliases={} if prev is None else {5 + i: i for i in range(4)},
        name=f"adamw_{name}_l{l}", compiler_params=_params(("arbitrary",)))(w, sums[0], sums[1], m, v, *extra)


WEIGHTS = ("pre_norm_g", "w_in", "gm_ln_g", "gm_ln_b", "gm_ws", "gm_bs", "mla_q_norm_g", "mla_w_uq", "mla_kv_norm_g",
           "mla_w_ukv", "lru_conv_w", "lru_conv_b", "lru_w_a", "lru_b_a", "lru_w_x", "lru_b_x", "lru_lambda",
           "w_proj_a", "w_proj_b", "w_proj_c", "w_out", "post_norm_g")
SHARDED = ("w_in", "mla_w_uq", "mla_w_ukv", "lru_conv_w", "w_proj_a", "w_proj_b", "w_proj_c", "w_out")
REPLICATED = tuple(n for n in WEIGHTS if n not in SHARDED)


def _step(x, target, wts, ms, vs):
    t12 = lambda a: jnp.swapaxes(a, 1, 2)
    names = list(WEIGHT_SPECS)
    specs = [WEIGHT_SPECS[n] for n in names]
    tabs = _rope_tables()
    own = {"w_in_t": t12(wts["w_in"]), "wq": t12(wts["mla_w_uq"]), "wkv": t12(wts["mla_w_ukv"]),
           "conv": jnp.pad(t12(wts["lru_conv_w"]), ((0, 0), (0, 0), (0, 124))),
           "w_proj_a": wts["w_proj_a"], "w_proj_b": wts["w_proj_b"], "w_proj_c": wts["w_proj_c"], "w_out": wts["w_out"]}
    first, rest = ["w_in_t"], [n for n in names if n != "w_in_t"]
    sfirst, srest = [WEIGHT_SPECS[n] for n in first], [WEIGHT_SPECS[n] for n in rest]

    w = {n: wts[n] for n in REPLICATED}
    w["kv_g384"] = jnp.concatenate([wts["mla_kv_norm_g"], jnp.ones((L, 128), F32)], axis=1)

    def layer_weights(ns, words):
        gw = dict(zip(ns, words))
        gw["wq"] = gw["wq"].reshape(HEADS, 192, 384)
        gw["wkv"] = gw["wkv"].reshape(HEADS, 256, 128)
        gw["conv"] = gw["conv"][0, :, :4].T
        return gw

    place = lambda l, dep: {n: pack_place(own[n], WEIGHT_SPECS[n], l, f"w{l}_{n}", dep) for n in names}
    placed = [place(0, None)]
    words_of = lambda l, ns: [placed[l][n][0] for n in ns]
    bufs_of = lambda l, ns: [placed[l][n][1] for n in ns]
    later = {}

    ssem_a, rsem_a, wthru_a, fthru_a, token_a = gather_start(words_of(0, first), bufs_of(0, first), sfirst, None, "w0a")
    placed.append(place(1, token_a))
    win0 = gather_forward(gather_wait(ssem_a, rsem_a, wthru_a, fthru_a, sfirst, placed[1]["w_in_t"][0], "w0a"), sfirst, "w0a")
    ssem_b, rsem_b, wthru_b, fthru_b, token_b = gather_start(words_of(0, rest), bufs_of(0, rest), srest, win0[0], "w0b")
    ssem1, rsem1, wthru1, fthru1, token1 = gather_start(words_of(1, names), bufs_of(1, names), specs, token_b, "w1")

    def fwd0_mid(ya):
        rest0 = gather_forward(gather_wait(ssem_b, rsem_b, wthru_b, fthru_b, srest, ya, "w0b"), srest, "w0b")
        later["gw0"] = layer_weights(first + rest, list(win0) + list(rest0))
        return later["gw0"], None

    x1, saved0 = _layer_fwd(x, 0, w, {"w_in_t": win0[0]}, tabs, dep=token1, mid=fwd0_mid)
    words1 = gather_forward(gather_wait(ssem1, rsem1, wthru1, fthru1, specs, x1, "w1"), specs, "w1")
    gw0, gw1 = later["gw0"], layer_weights(names, words1)
    x2, saved1 = _layer_fwd(x1, 1, w, gw1, tabs)
    loss, dx2 = loss_head(x2, target)

    def bwd1_mid(gg, last):
        later["p1b"] = pair_start([gg[n] for n in rest], srest, last, "g1b")
        return later["p1b"][4]

    dx1, gg1, g1 = _layer_bwd(dx2, 1, w, gw1, tabs, saved1, mid=bwd1_mid)
    grads1b, theirs1b = pair_wait(*later["p1b"][:4], srest, dx1, "g1b")
    p1a = pair_start([gg1["w_in_t"]], sfirst, theirs1b[0], "g1a")

    def bwd0_early(last):
        grads1a, theirs1a = pair_wait(*p1a[:4], sfirst, last, "g1a")
        mine = dict(zip(first + rest, list(grads1a) + list(grads1b)))
        theirs = dict(zip(first + rest, list(theirs1a) + list(theirs1b)))
        sums = [pair_sum(mine[n], theirs[n], WEIGHT_SPECS[n], f"g1_{n}") for n in names]
        ssem, rsem, words, slots, token = chips_start([s[1] for s in sums], specs, "g1")
        later["g1"] = (ssem, rsem, words, slots, [s[0] for s in sums])
        return token

    def bwd0_mid(gg, last):
        later["g0b"], token = reduce_scatter_start([gg[n] for n in rest], srest, rest, last, "g0b")
        return token

    def bwd0_late(g_win):
        later["p0a"] = pair_start([g_win], sfirst, None, "g0a")
        return later["p0a"][4]

    dx0, gg0, g0 = _layer_bwd(dx1, 0, w, gw0, tabs, saved0, dep=p1a[4], early=bwd0_early, mid=bwd0_mid, late=bwd0_late)
    s1 = dict(zip(names, reduce_scatter_finish(later["g1"], dx0, specs, "g1")))
    s0 = dict(zip(rest, reduce_scatter_finish(later["g0b"], dx0, srest, "g0b")))

    grads0a, theirs0a = pair_wait(*later["p0a"][:4], sfirst, dx0, "g0a")
    own0a, words0a = pair_sum(grads0a[0], theirs0a[0], sfirst[0], "g0a_w_in_t")
    ssem_g, rsem_g, wthru_g, slots_g, token_g = chips_start([words0a], sfirst, "g0a")

    keys = {"w_in": "w_in_t", "mla_w_uq": "wq", "mla_w_ukv": "wkv",
            "w_proj_a": "w_proj_a", "w_proj_b": "w_proj_b", "w_proj_c": "w_proj_c", "w_out": "w_out"}
    transposed = ("w_in", "mla_w_uq", "mla_w_ukv")
    state_of = lambda n: [own[keys[n]], t12(ms[n]), t12(vs[n])] if n in transposed else [wts[n], ms[n], vs[n]]

    def update(n, l, sums, prev, dep):
        wl, ml, vl = state_of(n)
        return adamw_layer(wl, sums[keys[n]], ml, vl, WEIGHT_SPECS[keys[n]], l, prev, dep, n)

    upd = {n: update(n, 1, s1, None, token_g) for n in keys}
    for n in keys:
        if n != "w_in":
            upd[n] = update(n, 0, s0, upd[n], None)
    rep_flat = jnp.concatenate([jnp.stack([g0[n], g1[n]]).reshape(-1) for n in REPLICATED] + [loss[None]])
    rep_flat = jnp.pad(rep_flat, (0, REP_ROWS * NDEV * D - rep_flat.shape[0])).reshape(1, REP_ROWS * NDEV, D)
    rep_parts = reduce_scatter([rep_flat], [REP_SPEC], ["rep"], "rep", upd["w_out"][0])[0]
    rep_sum = sum_chips(*rep_parts, REP_SPEC, "rep")
    rep_full = all_gather([rep_sum], 0, [REP_SPEC], ["rep"], "rep")[0].reshape(-1)

    out = {}
    conv_sp = WEIGHT_SPECS["conv"]
    g_conv = t12(jnp.concatenate([sum_chips(*s0["conv"], conv_sp, "conv0"), sum_chips(*s1["conv"], conv_sp, "conv1")])[:, :, :4])
    small = {"lru_conv_w": g_conv}
    at = 0
    for n in REPLICATED:
        size = math.prod(wts[n].shape)
        small[n] = rep_full[at:at + size].reshape(wts[n].shape)
        at += size
    updates = adamw_small([wts[n] for n in small], list(small.values()), [ms[n] for n in small], [vs[n] for n in small])
    for n, u in zip(small, updates):
        out[n] = [small[n]] + u

    landed = chips_wait(ssem_g, rsem_g, wthru_g, slots_g, sfirst, out[REPLICATED[-1]][1], "g0a")
    s0["w_in_t"] = (own0a, landed[0])
    upd["w_in"] = update("w_in", 0, s0, upd["w_in"], None)
    out.update({n: [t12(r) for r in upd[n]] if n in transposed else upd[n] for n in keys})

    return (rep_full[at], dx0[None], *[out[n][k] for k in range(4) for n in WEIGHTS])


def kernel(x, pre_norm_g, w_in, gm_ln_g, gm_ln_b, gm_ws, gm_bs, mla_q_norm_g, mla_w_uq, mla_kv_norm_g, mla_w_ukv, lru_conv_w, lru_conv_b, lru_w_a, lru_b_a, lru_w_x, lru_b_x, lru_lambda, w_proj_a, w_proj_b, w_proj_c, w_out, post_norm_g, loss_target, m_pre_norm_g, m_w_in, m_gm_ln_g, m_gm_ln_b, m_gm_ws, m_gm_bs, m_mla_q_norm_g, m_mla_w_uq, m_mla_kv_norm_g, m_mla_w_ukv, m_lru_conv_w, m_lru_conv_b, m_lru_w_a, m_lru_b_a, m_lru_w_x, m_lru_b_x, m_lru_lambda, m_w_proj_a, m_w_proj_b, m_w_proj_c, m_w_out, m_post_norm_g, v_pre_norm_g, v_w_in, v_gm_ln_g, v_gm_ln_b, v_gm_ws, v_gm_bs, v_mla_q_norm_g, v_mla_w_uq, v_mla_kv_norm_g, v_mla_w_ukv, v_lru_conv_w, v_lru_conv_b, v_lru_w_a, v_lru_b_a, v_lru_w_x, v_lru_b_x, v_lru_lambda, v_w_proj_a, v_w_proj_b, v_w_proj_c, v_w_out, v_post_norm_g):
    wts = dict(zip(WEIGHTS, (pre_norm_g, w_in, gm_ln_g, gm_ln_b, gm_ws, gm_bs, mla_q_norm_g, mla_w_uq, mla_kv_norm_g,
                             mla_w_ukv, lru_conv_w, lru_conv_b, lru_w_a, lru_b_a, lru_w_x, lru_b_x, lru_lambda,
                             w_proj_a, w_proj_b, w_proj_c, w_out, post_norm_g)))
    ms = dict(zip(WEIGHTS, (m_pre_norm_g, m_w_in, m_gm_ln_g, m_gm_ln_b, m_gm_ws, m_gm_bs, m_mla_q_norm_g, m_mla_w_uq,
                            m_mla_kv_norm_g, m_mla_w_ukv, m_lru_conv_w, m_lru_conv_b, m_lru_w_a, m_lru_b_a, m_lru_w_x,
                            m_lru_b_x, m_lru_lambda, m_w_proj_a, m_w_proj_b, m_w_proj_c, m_w_out, m_post_norm_g)))
    vs = dict(zip(WEIGHTS, (v_pre_norm_g, v_w_in, v_gm_ln_g, v_gm_ln_b, v_gm_ws, v_gm_bs, v_mla_q_norm_g, v_mla_w_uq,
                            v_mla_kv_norm_g, v_mla_w_ukv, v_lru_conv_w, v_lru_conv_b, v_lru_w_a, v_lru_b_a, v_lru_w_x,
                            v_lru_b_x, v_lru_lambda, v_w_proj_a, v_w_proj_b, v_w_proj_c, v_w_out, v_post_norm_g)))
    return _step(x[0], loss_target[0], wts, ms, vs)
```

```python
import functools
import math

import jax
import jax.numpy as jnp
from jax import lax
from jax.experimental import pallas as pl
from jax.experimental.pallas import tpu as pltpu

F32 = jnp.float32
BF16 = jnp.bfloat16

T = 2048
D = 1024
L = 2
NDEV = 8
EPS = 1e-6
CHUNK_SHIFT = 6
HEADS = 8
QK = 192
LRU_W = 1280
LRU_TILE = 640
N_IN = 10432
SHARD = N_IN // NDEV
OFF_U, OFF_V, OFF_ZA, OFF_CQ, OFF_CKV, OFF_ZB = 0, 1024, 2048, 3072, 3456, 3840
OFF_XC, OFF_ZC, OFF_GA, OFF_GB, OFF_GC = 5120, 6400, 7680, 8704, 9728
NPAD = 10752
PAD1_AT, PAD1 = 3776, 64
PAD2_AT, PAD2 = 4800, 256
WIN_PIECES = ((0, 888), (888, 280), (1168, 136))
VMEM_LIMIT = 60 * 1024 * 1024

ADAM_LR, ADAM_B1, ADAM_B2, ADAM_EPS, ADAM_WD, ADAM_STEP = 0.001, 0.9, 0.999, 1e-08, 0.01, 10

_NN = (((1,), (0,)), ((), ()))
_NT = (((1,), (1,)), ((), ()))
_TN = (((0,), (0,)), ((), ()))


def _dg(a, b, dims):
    return lax.dot_general(a.astype(BF16), b.astype(BF16), dims, preferred_element_type=F32)


@jax.custom_vjp
def dot_nn(a, b):
    return _dg(a, b, _NN)


def _nn_fwd(a, b):
    return _dg(a, b, _NN), (a, b)


def _nn_bwd(res, g):
    a, b = res
    return _dg(g, b, _NT).astype(a.dtype), _dg(a, g, _TN).astype(b.dtype)


dot_nn.defvjp(_nn_fwd, _nn_bwd)


@jax.custom_vjp
def dot_nt(a, b):
    return _dg(a, b, _NT)


def _nt_fwd(a, b):
    return _dg(a, b, _NT), (a, b)


def _nt_bwd(res, g):
    a, b = res
    return _dg(g, b, _NN).astype(a.dtype), _dg(g, a, _TN).astype(b.dtype)


dot_nt.defvjp(_nt_fwd, _nt_bwd)


def _params(sem=None):
    return pltpu.CompilerParams(dimension_semantics=sem, vmem_limit_bytes=VMEM_LIMIT)


def _sigmoid(x):
    return 1.0 / (1.0 + jnp.exp(-x))


def _silu(x):
    return x * _sigmoid(x)


def _rms(x, g):
    ms = jnp.mean(x * x, axis=-1, keepdims=True)
    return x * lax.rsqrt(ms + EPS) * g


def _acc(ref, val, first):
    @pl.when(first)
    def _():
        ref[...] = val

    @pl.when(jnp.logical_not(first))
    def _():
        ref[...] += val


ANY = pl.BlockSpec(memory_space=pl.ANY)


INPROJ_TN = 768


def inproj_fwd(x, g, wt, l, dep=None):
    tn = INPROJ_TN

    def body(x_ref, g_ref, w_ref, *rest):
        proj_ref, h_ref = rest[-2:]

        @pl.when(pl.program_id(0) == 0)
        def _():
            h_ref[...] = _rms(x_ref[...], g_ref[...]).astype(BF16)

        proj_ref[...] = lax.dot_general(h_ref[...], _unpack(w_ref[...]).astype(BF16), _NT, preferred_element_type=F32)

    deps = [] if dep is None else [dep]
    return pl.pallas_call(
        body, grid=(NPAD // tn,),
        in_specs=[pl.BlockSpec((T, D), lambda j: (0, 0)), pl.BlockSpec((1, D), lambda j: (0, 0)),
                  pl.BlockSpec((None, tn, D // 2), lambda j: (0, j, 0))] + [ANY] * len(deps),
        out_specs=[pl.BlockSpec((T, tn), lambda j: (0, j)), pl.BlockSpec((T, D), lambda j: (0, 0))],
        out_shape=[jax.ShapeDtypeStruct((T, NPAD), F32), jax.ShapeDtypeStruct((T, D), BF16)],
        name=f"inproj_fwd_l{l}", compiler_params=_params(("arbitrary",)))(x, g, wt, *deps)


def inproj_bwd(dproj, h, wt, l, dep=None):
    tn = INPROJ_TN
    deps = [] if dep is None else [dep]

    def body(dp_ref, h_ref, w_ref, *rest):
        dwt_ref, dh_ref = rest[-2:]
        dp = dp_ref[...]
        dwt_ref[...] = lax.dot_general(dp, h_ref[...], _TN, preferred_element_type=F32)
        contrib = lax.dot_general(dp, _unpack(w_ref[...]).astype(BF16), _NN, preferred_element_type=F32)
        _acc(dh_ref, contrib, pl.program_id(0) == 0)

    return pl.pallas_call(
        body, grid=(NPAD // tn,),
        in_specs=[pl.BlockSpec((T, tn), lambda j: (0, j)), pl.BlockSpec((T, D), lambda j: (0, 0)),
                  pl.BlockSpec((None, tn, D // 2), lambda j: (0, j, 0))] + [ANY] * len(deps),
        out_specs=[pl.BlockSpec((None, tn, D), lambda j: (0, j, 0)), pl.BlockSpec((T, D), lambda j: (0, 0))],
        out_shape=[jax.ShapeDtypeStruct((1, NPAD, D), F32), jax.ShapeDtypeStruct((T, D), F32)],
        name=f"inproj_bwd_l{l}", compiler_params=_params(("arbitrary",)))(dproj, h, wt, *deps)


def prenorm_bwd(x, g, dh, dxn, l, dep=None):
    tm = 512
    deps = [] if dep is None else [dep]

    def body(x_ref, g_ref, dh_ref, dxn_ref, *rest):
        dx_ref, dg_ref = rest[-2:]
        _, vjp = jax.vjp(_rms, x_ref[...], g_ref[...])
        dx, dg = vjp(dh_ref[...])
        dx_ref[...] = dx + dxn_ref[...]
        _acc(dg_ref, dg, pl.program_id(0) == 0)

    tok = pl.BlockSpec((tm, D), lambda i: (i, 0))
    vec = pl.BlockSpec((1, D), lambda i: (0, 0))
    return pl.pallas_call(
        body, grid=(T // tm,), in_specs=[tok, vec, tok, tok] + [ANY] * len(deps), out_specs=[tok, vec],
        out_shape=[jax.ShapeDtypeStruct((T, D), F32), jax.ShapeDtypeStruct((1, D), F32)],
        name=f"prenorm_bwd_l{l}", compiler_params=_params(("arbitrary",)))(x, g, dh, dxn, *deps)


def _gmlp_tile(u, v, z, ln_g, ln_b, ws, bs):
    mu = jnp.mean(v, axis=-1, keepdims=True)
    vc = v - mu
    var = jnp.mean(vc * vc, axis=-1, keepdims=True)
    vn = vc * lax.rsqrt(var + EPS) * ln_g + ln_b
    qi = lax.broadcasted_iota(jnp.int32, (128, 128), 0) >> CHUNK_SHIFT
    kj = lax.broadcasted_iota(jnp.int32, (128, 128), 1) >> CHUNK_SHIFT
    mask = kj <= qi
    outs = []
    for g in range(4):
        wm = jnp.where(mask, ws[g], 0.0)
        outs.append(dot_nn(wm, vn[:, 256 * g:256 * (g + 1)]) + bs[g])
    sv = jnp.concatenate(outs, axis=1)
    return u * sv * _silu(z)


GMLP_ROWS = 256


def _gmlp_specs():
    blk = lambda c: pl.BlockSpec((GMLP_ROWS, 1024), lambda n, c=c: (n, c))
    vec = pl.BlockSpec((1, 1024), lambda n: (0, 0))
    return [blk(0), blk(1), blk(2), vec, vec,
            pl.BlockSpec((4, 128, 128), lambda n: (0, 0, 0)), pl.BlockSpec((4, 128, 1), lambda n: (0, 0, 0))]


def gmlp_fwd(proj, ln_g, ln_b, ws, bs, l):
    def body(u_ref, v_ref, z_ref, g_ref, b_ref, ws_ref, bs_ref, y_ref):
        for r in range(0, GMLP_ROWS, 128):
            rows = slice(r, r + 128)
            y_ref[rows, :] = _gmlp_tile(u_ref[rows, :], v_ref[rows, :], z_ref[rows, :], g_ref[...], b_ref[...],
                                        [ws_ref[g] for g in range(4)], [bs_ref[g] for g in range(4)])

    return pl.pallas_call(
        body, grid=(T // GMLP_ROWS,), in_specs=_gmlp_specs(),
        out_specs=pl.BlockSpec((GMLP_ROWS, 1024), lambda n: (n, 0)),
        out_shape=jax.ShapeDtypeStruct((T, 1024), F32),
        name=f"gmlp_fwd_l{l}", compiler_params=_params(("arbitrary",)))(proj, proj, proj, ln_g, ln_b, ws, bs)


def gmlp_bwd(proj, ln_g, ln_b, ws, bs, dy, dproj, l):
    def body(u_ref, v_ref, z_ref, g_ref, b_ref, ws_ref, bs_ref, dy_ref, _, dseg_ref, dg_ref, db_ref, dws_ref, dbs_ref):
        for r in range(0, GMLP_ROWS, 128):
            rows = slice(r, r + 128)
            first = jnp.logical_and(pl.program_id(0) == 0, r == 0)
            _, vjp = jax.vjp(_gmlp_tile, u_ref[rows, :], v_ref[rows, :], z_ref[rows, :], g_ref[...], b_ref[...],
                             [ws_ref[g] for g in range(4)], [bs_ref[g] for g in range(4)])
            du, dv, dz, dg, db, dws, dbs = vjp(dy_ref[rows, :])
            dseg_ref[rows, 0:1024] = du.astype(BF16)
            dseg_ref[rows, 1024:2048] = dv.astype(BF16)
            dseg_ref[rows, 2048:3072] = dz.astype(BF16)
            _acc(dg_ref, dg, first)
            _acc(db_ref, db, first)
            for g in range(4):
                _acc(dws_ref.at[g], dws[g], first)
                _acc(dbs_ref.at[g], dbs[g], first)

    vec = pl.BlockSpec((1, 1024), lambda n: (0, 0))
    return pl.pallas_call(
        body, grid=(T // GMLP_ROWS,),
        in_specs=_gmlp_specs() + [pl.BlockSpec((GMLP_ROWS, 1024), lambda n: (n, 0)), ANY],
        out_specs=[pl.BlockSpec((GMLP_ROWS, 3072), lambda n: (n, OFF_U // 3072)), vec, vec,
                   pl.BlockSpec((4, 128, 128), lambda n: (0, 0, 0)), pl.BlockSpec((4, 128, 1), lambda n: (0, 0, 0))],
        out_shape=[jax.ShapeDtypeStruct((T, NPAD), BF16), jax.ShapeDtypeStruct((1, 1024), F32),
                   jax.ShapeDtypeStruct((1, 1024), F32), jax.ShapeDtypeStruct((4, 128, 128), F32),
                   jax.ShapeDtypeStruct((4, 128, 1), F32)],
        input_output_aliases={8: 0},
        name=f"gmlp_bwd_l{l}", compiler_params=_params(("arbitrary",)))(proj, proj, proj, ln_g, ln_b, ws, bs, dy, dproj)


QKV_TM = 512


def _qkv_tile(cq, ckvr, qg, kvg, wq, wkv, ctab, stab):
    tm = cq.shape[0]
    cqn = _rms(cq, qg)
    lane = lax.broadcasted_iota(jnp.int32, ckvr.shape, 1)
    iskv = lane < 256
    ms = jnp.sum(jnp.where(iskv, ckvr * ckvr, 0.0), axis=-1, keepdims=True) * (1.0 / 256)
    lm = jnp.where(iskv, ckvr * lax.rsqrt(ms + EPS) * kvg, ckvr)
    r = lax.broadcasted_iota(jnp.int32, (64, 128), 0)
    c = lax.broadcasted_iota(jnp.int32, (64, 128), 1)
    eye = jnp.where(c == r, 1.0, 0.0)
    eye_sw = jnp.where(c == ((r + 32) & 63), 1.0, 0.0)
    z64 = jnp.zeros((64, 256), F32)
    z128 = jnp.zeros((128, 128), F32)
    rk_rope = jnp.concatenate([z64, eye], axis=1)
    rk_sw = jnp.concatenate([jnp.zeros((128, 384), F32), jnp.concatenate([z64, eye_sw], axis=1)], axis=0)
    k_sw = dot_nt(lm, rk_sw) * stab
    qs, ks, vs = [], [], []
    for h in range(HEADS):
        wn, w1, w2 = wq[h]
        wk, wv = wkv[h]
        wq_h = jnp.concatenate([wn, w1, w2], axis=0)
        wq_sw = jnp.concatenate([jnp.zeros((128, 384), F32), w2, w1], axis=0)
        qs.append(dot_nt(cqn, wq_h) * ctab + dot_nt(cqn, wq_sw) * stab)
        rk_h = jnp.concatenate([jnp.concatenate([wk, z128], axis=1), rk_rope], axis=0)
        ks.append(dot_nt(lm, rk_h) * ctab + k_sw)
        vs.append(dot_nt(lm, jnp.concatenate([wv, z128], axis=1)))
    return qs, ks, vs


def _qkv_in_specs():
    tm = QKV_TM
    return [pl.BlockSpec((tm, 384), lambda i: (i, OFF_CQ // 384)), pl.BlockSpec((tm, 384), lambda i: (i, OFF_CKV // 384)),
            pl.BlockSpec((1, 384), lambda i: (0, 0)), pl.BlockSpec((1, 384), lambda i: (0, 0)),
            pl.BlockSpec((HEADS, 192, 384), lambda i: (0, 0, 0)), pl.BlockSpec((HEADS, 256, 128), lambda i: (0, 0, 0)),
            pl.BlockSpec((tm, 192), lambda i: (i, 0)), pl.BlockSpec((tm, 192), lambda i: (i, 0))]


def _qkv_weights(wq_ref, wkv_ref):
    wq = [(wq_ref[h, 0:128, :], wq_ref[h, 128:160, :], wq_ref[h, 160:192, :]) for h in range(HEADS)]
    wkv = [(_unpack(wkv_ref[h, 0:128, :]), _unpack(wkv_ref[h, 128:256, :])) for h in range(HEADS)]
    return wq, wkv


def qkv_fwd(proj, qg, kvg, wq, wkv, ctab, stab, l, dep=None):
    tm = QKV_TM
    deps = [] if dep is None else [dep]

    def body(cq_ref, ckvr_ref, qg_ref, kvg_ref, wq_ref, wkv_ref, c_ref, s_ref, *rest):
        q_ref, k_ref, v_ref = rest[-3:]
        wq_l, wkv_l = _qkv_weights(wq_ref, wkv_ref)
        qs, ks, vs = _qkv_tile(cq_ref[...], ckvr_ref[...], qg_ref[...], kvg_ref[...], wq_l, wkv_l, c_ref[...], s_ref[...])
        for h in range(HEADS):
            q_ref[h] = qs[h]
            k_ref[h] = ks[h]
            v_ref[h] = vs[h]

    return pl.pallas_call(
        body, grid=(T // tm,), in_specs=_qkv_in_specs() + [ANY] * len(deps),
        out_specs=[pl.BlockSpec((HEADS, tm, QK), lambda i: (0, i, 0)), pl.BlockSpec((HEADS, tm, QK), lambda i: (0, i, 0)),
                   pl.BlockSpec((HEADS, tm, 128), lambda i: (0, i, 0))],
        out_shape=[jax.ShapeDtypeStruct((HEADS, T, QK), F32), jax.ShapeDtypeStruct((HEADS, T, QK), F32),
                   jax.ShapeDtypeStruct((HEADS, T, 128), F32)],
        name=f"qkv_fwd_l{l}", compiler_params=_params(("arbitrary",)))(proj, proj, qg, kvg, wq, wkv, ctab, stab, *deps)


def qkv_bwd(proj, qg, kvg, wq, wkv, ctab, stab, dq, dk, dv, dproj, l):
    tm = QKV_TM

    def body(cq_ref, ckvr_ref, qg_ref, kvg_ref, wq_ref, wkv_ref, c_ref, s_ref, dq_ref, dk_ref, dv_ref, _,
             dseg_ref, dqg_ref, dkvg_ref, dwq_ref, dwkv_ref):
        first = pl.program_id(0) == 0
        wq_l, wkv_l = _qkv_weights(wq_ref, wkv_ref)
        c_tab, s_tab = c_ref[...], s_ref[...]
        fn = lambda cq, ckvr, qg_, kvg_, wq_, wkv_: _qkv_tile(cq, ckvr, qg_, kvg_, wq_, wkv_, c_tab, s_tab)
        _, vjp = jax.vjp(fn, cq_ref[...], ckvr_ref[...], qg_ref[...], kvg_ref[...], wq_l, wkv_l)
        cts = ([dq_ref[h] for h in range(HEADS)], [dk_ref[h] for h in range(HEADS)], [dv_ref[h] for h in range(HEADS)])
        dcq, dckvr, dqg, dkvg, dwq, dwkv = vjp(cts)
        dseg_ref[:, 0:384] = dcq.astype(BF16)
        dseg_ref[:, 384:768] = dckvr.astype(BF16)
        _acc(dqg_ref, dqg, first)
        _acc(dkvg_ref, dkvg, first)
        for h in range(HEADS):
            _acc(dwq_ref.at[h, 0:128, :], dwq[h][0], first)
            _acc(dwq_ref.at[h, 128:160, :], dwq[h][1], first)
            _acc(dwq_ref.at[h, 160:192, :], dwq[h][2], first)
            _acc(dwkv_ref.at[h, 0:128, :], dwkv[h][0], first)
            _acc(dwkv_ref.at[h, 128:256, :], dwkv[h][1], first)

    hq = pl.BlockSpec((HEADS, tm, QK), lambda i: (0, i, 0))
    return pl.pallas_call(
        body, grid=(T // tm,),
        in_specs=_qkv_in_specs() + [hq, hq, pl.BlockSpec((HEADS, tm, 128), lambda i: (0, i, 0)), ANY],
        out_specs=[pl.BlockSpec((tm, 768), lambda i: (i, OFF_CQ // 768)), pl.BlockSpec((1, 384), lambda i: (0, 0)),
                   pl.BlockSpec((1, 384), lambda i: (0, 0)), pl.BlockSpec((HEADS, 192, 384), lambda i: (0, 0, 0)),
                   pl.BlockSpec((HEADS, 256, 256), lambda i: (0, 0, 0))],
        out_shape=[jax.ShapeDtypeStruct((T, NPAD), BF16), jax.ShapeDtypeStruct((1, 384), F32),
                   jax.ShapeDtypeStruct((1, 384), F32), jax.ShapeDtypeStruct((HEADS, 192, 384), F32),
                   jax.ShapeDtypeStruct((HEADS, 256, 256), F32)],
        input_output_aliases={11: 0},
        name=f"qkv_bwd_l{l}", compiler_params=_params(("arbitrary",)))(
            proj, proj, qg, kvg, wq, wkv, ctab, stab, dq, dk, dv, dproj)


ATT_TQ_FWD = 256
ATT_TQ_BWD = 512


def _attn_tile(q, kv_past, k, v, zb):
    q = q * (1.0 / math.sqrt(QK))
    s = dot_nt(q, k)
    qc = lax.broadcasted_iota(jnp.int32, s.shape, 0) >> CHUNK_SHIFT
    kc = lax.broadcasted_iota(jnp.int32, s.shape, 1) >> CHUNK_SHIFT
    s = jnp.where(kc <= qc, s, -1e30)
    m = jnp.max(s, axis=-1, keepdims=True)
    if kv_past is not None:
        sp = dot_nt(q, kv_past[0])
        m = jnp.maximum(m, jnp.max(sp, axis=-1, keepdims=True))
    m = lax.stop_gradient(m)
    p = jnp.exp(s - m)
    denom = jnp.sum(p, axis=-1, keepdims=True)
    o = dot_nn(p, v)
    if kv_past is not None:
        pp = jnp.exp(sp - m)
        denom = denom + jnp.sum(pp, axis=-1, keepdims=True)
        o = o + dot_nn(pp, kv_past[1])
    return o * (1.0 / denom) * _silu(zb)


def _attn_operands(k_ref, v_ref, g, tq):
    n = tq * g
    past = (k_ref[0:n, :], v_ref[0:n, :]) if g else None
    return past, k_ref[n:n + tq, :], v_ref[n:n + tq, :]


def _attn_in_specs(tq):
    return [pl.BlockSpec((None, tq, QK), lambda h, i: (h, i, 0)), pl.BlockSpec((None, T, QK), lambda h, i: (h, 0, 0)),
            pl.BlockSpec((None, T, 128), lambda h, i: (h, 0, 0)),
            pl.BlockSpec((tq, 128), lambda h, i: (i, OFF_ZB // 128 + h))]


def attn_fwd(q, k, v, proj, l):
    tq = ATT_TQ_FWD

    def body(q_ref, k_ref, v_ref, z_ref, y_ref):
        for g in range(T // tq):
            @pl.when(pl.program_id(1) == g)
            def _(g=g):
                past, k, v = _attn_operands(k_ref, v_ref, g, tq)
                y_ref[...] = _attn_tile(q_ref[...], past, k, v, z_ref[...])

    return pl.pallas_call(
        body, grid=(HEADS, T // tq), in_specs=_attn_in_specs(tq),
        out_specs=pl.BlockSpec((tq, 128), lambda h, i: (i, h)),
        out_shape=jax.ShapeDtypeStruct((T, 1024), F32),
        name=f"attn_fwd_l{l}", compiler_params=_params(("arbitrary", "arbitrary")))(q, k, v, proj)


def attn_bwd(q, k, v, proj, dy, dproj, l):
    tq = ATT_TQ_BWD

    def body(q_ref, k_ref, v_ref, z_ref, dy_ref, _, dq_ref, dk_ref, dv_ref, dz_ref):
        @pl.when(pl.program_id(1) == 0)
        def _():
            dk_ref[...] = jnp.zeros_like(dk_ref)
            dv_ref[...] = jnp.zeros_like(dv_ref)

        for g in range(T // tq):
            @pl.when(pl.program_id(1) == g)
            def _(g=g):
                n = tq * g
                past, k, v = _attn_operands(k_ref, v_ref, g, tq)
                _, vjp = jax.vjp(_attn_tile, q_ref[...], past, k, v, z_ref[...])
                dq, dpast, dk, dv, dz = vjp(dy_ref[...])
                dq_ref[...] = dq
                dz_ref[...] = dz.astype(BF16)
                dk_ref[n:n + tq, :] += dk
                dv_ref[n:n + tq, :] += dv
                if g:
                    dk_ref[0:n, :] += dpast[0]
                    dv_ref[0:n, :] += dpast[1]

    return pl.pallas_call(
        body, grid=(HEADS, T // tq),
        in_specs=_attn_in_specs(tq) + [pl.BlockSpec((tq, 128), lambda h, i: (i, h)), ANY],
        out_specs=[pl.BlockSpec((None, tq, QK), lambda h, i: (h, i, 0)), pl.BlockSpec((None, T, QK), lambda h, i: (h, 0, 0)),
                   pl.BlockSpec((None, T, 128), lambda h, i: (h, 0, 0)),
                   pl.BlockSpec((tq, 128), lambda h, i: (i, OFF_ZB // 128 + h))],
        out_shape=[jax.ShapeDtypeStruct((HEADS, T, QK), F32), jax.ShapeDtypeStruct((HEADS, T, QK), F32),
                   jax.ShapeDtypeStruct((HEADS, T, 128), F32), jax.ShapeDtypeStruct((T, NPAD), BF16)],
        input_output_aliases={5: 3},
        name=f"attn_bwd_l{l}", compiler_params=_params(("arbitrary", "arbitrary")))(q, k, v, proj, dy, dproj)


LRU_TT = 512


def _lru_gates(xc, wa, wx, ba, bx, lam):
    r = _sigmoid(dot_nn(xc, wa) + ba)
    i = _sigmoid(dot_nn(xc, wx) + bx)
    sp = jnp.maximum(-lam, 0.0) + jnp.log1p(jnp.exp(-jnp.abs(lam)))
    log_a = -8.0 * r * sp
    a = jnp.exp(log_a)
    mult = jnp.sqrt(jnp.maximum(1.0 - jnp.exp(2.0 * log_a), 0.0))
    return a, mult * (i * xc)


def _shift_down(x, s, halo):
    n, c = x.shape
    r = pltpu.roll(x.reshape(n // 8, 8, c), s, 1)
    before = jnp.concatenate([pltpu.roll(halo, s, 0)[None], r[:-1]], axis=0)
    sub = lax.broadcasted_iota(jnp.int32, r.shape, 1)
    return jnp.where(sub >= s, r, before).reshape(n, c)


def _shift_up(x, s, halo):
    n, c = x.shape
    r = pltpu.roll(x.reshape(n // 8, 8, c), 8 - s, 1)
    after = jnp.concatenate([r[1:], pltpu.roll(halo, 8 - s, 0)[None]], axis=0)
    sub = lax.broadcasted_iota(jnp.int32, r.shape, 1)
    return jnp.where(sub < 8 - s, r, after).reshape(n, c)


def _conv(x, halo, w_ref, b):
    return (w_ref[3:4, :] * x + w_ref[2:3, :] * _shift_down(x, 1, halo) + w_ref[1:2, :] * _shift_down(x, 2, halo)
            + w_ref[0:1, :] * _shift_down(x, 3, halo) + b)


def _scan(a, b, reverse, carry):
    n, c = a.shape
    a, b = a.reshape(n // 8, 8, c), b.reshape(n // 8, 8, c)
    sub = lax.broadcasted_iota(jnp.int32, a.shape, 1)
    for d in (1, 2, 4):
        keep = sub < 8 - d if reverse else sub >= d
        shift = 8 - d if reverse else d
        a_sh = jnp.where(keep, pltpu.roll(a, shift, 1), 1.0)
        b_sh = jnp.where(keep, pltpu.roll(b, shift, 1), 0.0)
        b = a * b_sh + b
        a = a * a_sh
    a, b = a.reshape(n, c), b.reshape(n, c)
    groups = [None] * (n // 8)
    for g in (reversed(range(n // 8)) if reverse else range(n // 8)):
        h = a[8 * g:8 * g + 8] * carry + b[8 * g:8 * g + 8]
        groups[g] = h
        carry = h[0:1] if reverse else h[7:8]
    return jnp.concatenate(groups, axis=0), carry


def _lru_param_specs(l):
    ct = LRU_TILE
    vec = pl.BlockSpec((1, ct), lambda n, i: (0, n))
    mat = pl.BlockSpec((None, 8, 80, 80), lambda n, i: (l, n, 0, 0))
    return [pl.BlockSpec((4, ct), lambda n, i: (0, n)), vec, mat, mat, vec, vec, vec]


def _blocks_to_dense(w_ref, dense):
    dense[...] = jnp.zeros_like(dense)
    for b in range(8):
        dense[80 * b:80 * b + 80, 80 * b:80 * b + 80] = w_ref[b]


def _dense_to_blocks(dense, w_ref):
    for b in range(8):
        w_ref[b] = dense[80 * b:80 * b + 80, 80 * b:80 * b + 80]


def lru_fwd(proj, conv_w, conv_b, wa, wx, ba, bx, lam, l):
    tt, ct = LRU_TT, LRU_TILE

    def body(x_ref, z_ref, cw_ref, cb_ref, wa_ref, wx_ref, ba_ref, bx_ref, lam_ref, h_ref, y_ref, halo, hcar, wa, wx):
        @pl.when(pl.program_id(1) == 0)
        def _():
            halo[...] = jnp.zeros_like(halo)
            hcar[...] = jnp.zeros_like(hcar)
            _blocks_to_dense(wa_ref, wa)
            _blocks_to_dense(wx_ref, wx)

        x = x_ref[...]
        xc = _conv(x, halo[...], cw_ref, cb_ref[...])
        halo[...] = x[tt - 8:tt]
        a, b = _lru_gates(xc, wa[...], wx[...], ba_ref[...], bx_ref[...], lam_ref[...])
        h, hcar[...] = _scan(a, b, False, hcar[...])
        h_ref[...] = h
        y_ref[...] = h * _silu(z_ref[...])

    seq = pl.BlockSpec((tt, ct), lambda n, i: (i, n))
    return pl.pallas_call(
        body, grid=(LRU_W // ct, T // tt),
        in_specs=[pl.BlockSpec((tt, ct), lambda n, i: (i, OFF_XC // ct + n)),
                  pl.BlockSpec((tt, ct), lambda n, i: (i, OFF_ZC // ct + n))] + _lru_param_specs(l),
        out_specs=[seq, seq],
        out_shape=[jax.ShapeDtypeStruct((T, LRU_W), F32), jax.ShapeDtypeStruct((T, LRU_W), F32)],
        scratch_shapes=[pltpu.VMEM((8, ct), F32), pltpu.VMEM((1, ct), F32), pltpu.VMEM((ct, ct), F32),
                        pltpu.VMEM((ct, ct), F32)],
        name=f"lru_fwd_l{l}", compiler_params=_params(("arbitrary", "arbitrary")))(
            proj, proj, conv_w, conv_b, wa, wx, ba, bx, lam)


def lru_bwd(proj, hseq, dy, conv_w, conv_b, wa, wx, ba, bx, lam, dproj, l):
    tt, ct = LRU_TT, LRU_TILE
    nt = T // tt
    rev = lambda i: nt - 1 - i
    prev8 = lambda i: jnp.maximum(rev(i) * (tt // 8) - 1, 0)

    def body(x_ref, xh_ref, z_ref, h_ref, hh_ref, dy_ref, cw_ref, cb_ref, wa_ref, wx_ref, ba_ref, bx_ref, lam_ref, _,
             dx_ref, dcw_ref, dcb_ref, dwa_ref, dwx_ref, dba_ref, dbx_ref, dlam_ref, gcar, dhalo,
             wa, wx, dwa_acc, dwx_acc):
        i = pl.program_id(1)
        first = i == 0

        @pl.when(first)
        def _():
            gcar[...] = jnp.zeros_like(gcar)
            dhalo[...] = jnp.zeros_like(dhalo)
            _blocks_to_dense(wa_ref, wa)
            _blocks_to_dense(wx_ref, wx)

        at_start = rev(i) == 0
        x = x_ref[...]
        xhalo = jnp.where(at_start, 0.0, xh_ref[...])
        sh = [x, _shift_down(x, 1, xhalo), _shift_down(x, 2, xhalo), _shift_down(x, 3, xhalo)]
        xc = (cw_ref[3:4, :] * sh[0] + cw_ref[2:3, :] * sh[1] + cw_ref[1:2, :] * sh[2] + cw_ref[0:1, :] * sh[3]
              + cb_ref[...])
        (a, b), vjp = jax.vjp(_lru_gates, xc, wa[...], wx[...], ba_ref[...], bx_ref[...], lam_ref[...])
        hs = h_ref[...]
        hprev = _shift_down(hs, 1, jnp.where(at_start, 0.0, hh_ref[...]))
        dh = dy_ref[...] * _silu(z_ref[...])
        a_next = _shift_up(a, 1, jnp.ones((8, ct), F32))
        g, _ = _scan(a_next, dh, True, gcar[...])
        dxc, dwa, dwx, dba, dbx, dlam = vjp((g * hprev, g))
        dx = (cw_ref[3:4, :] * dxc + cw_ref[2:3, :] * _shift_up(dxc, 1, dhalo[...])
              + cw_ref[1:2, :] * _shift_up(dxc, 2, dhalo[...]) + cw_ref[0:1, :] * _shift_up(dxc, 3, dhalo[...]))
        dx_ref[...] = dx.astype(BF16)
        dhalo[...] = dxc[0:8]
        ag = a * g
        gcar[...] = ag[0:1]
        dcw = jnp.concatenate([jnp.sum(dxc * sh[3 - j], axis=0, keepdims=True) for j in range(4)], axis=0)
        _acc(dcw_ref, dcw, first)
        _acc(dcb_ref, jnp.sum(dxc, axis=0, keepdims=True), first)
        _acc(dwa_acc, dwa, first)
        _acc(dwx_acc, dwx, first)

        @pl.when(i == nt - 1)
        def _():
            _dense_to_blocks(dwa_acc, dwa_ref)
            _dense_to_blocks(dwx_acc, dwx_ref)

        _acc(dba_ref, dba, first)
        _acc(dbx_ref, dbx, first)
        _acc(dlam_ref, dlam, first)

    xcol = OFF_XC // ct
    zcol = OFF_ZC // ct
    vec = pl.BlockSpec((1, ct), lambda n, i: (0, n))
    mat = pl.BlockSpec((8, 80, 80), lambda n, i: (n, 0, 0))
    seq = pl.BlockSpec((tt, ct), lambda n, i: (rev(i), n))
    return pl.pallas_call(
        body, grid=(LRU_W // ct, nt),
        in_specs=[pl.BlockSpec((tt, ct), lambda n, i: (rev(i), xcol + n)),
                  pl.BlockSpec((8, ct), lambda n, i: (prev8(i), xcol + n)),
                  pl.BlockSpec((tt, ct), lambda n, i: (rev(i), zcol + n)),
                  seq, pl.BlockSpec((8, ct), lambda n, i: (prev8(i), n)), seq] + _lru_param_specs(l) + [ANY],
        out_specs=[pl.BlockSpec((tt, ct), lambda n, i: (rev(i), xcol + n)),
                   pl.BlockSpec((4, ct), lambda n, i: (0, n)), vec, mat, mat, vec, vec, vec],
        out_shape=[jax.ShapeDtypeStruct((T, NPAD), BF16),
                   jax.ShapeDtypeStruct((4, LRU_W), F32), jax.ShapeDtypeStruct((1, LRU_W), F32),
                   jax.ShapeDtypeStruct((16, 80, 80), F32), jax.ShapeDtypeStruct((16, 80, 80), F32),
                   jax.ShapeDtypeStruct((1, LRU_W), F32), jax.ShapeDtypeStruct((1, LRU_W), F32),
                   jax.ShapeDtypeStruct((1, LRU_W), F32)],
        scratch_shapes=[pltpu.VMEM((1, ct), F32), pltpu.VMEM((8, ct), F32)] + [pltpu.VMEM((ct, ct), F32)] * 4,
        input_output_aliases={13: 0},
        name=f"lru_bwd_l{l}", compiler_params=_params(("arbitrary", "arbitrary")))(
            proj, proj, proj, hseq, hseq, dy, conv_w, conv_b, wa, wx, ba, bx, lam, dproj)


def proj_bwd(y, dp, w, l, tag, dep=None, dproj=None, gate=None):
    tm = 512
    k = y.shape[1]
    extra = [] if dep is None else [dep]
    in_specs = [pl.BlockSpec((tm, k), lambda i: (i, 0)), pl.BlockSpec((tm, D), lambda i: (i, 0)),
                pl.BlockSpec((None, k, D // 2), lambda i: (0, 0, 0))]
    out_specs = [pl.BlockSpec((tm, k), lambda i: (i, 0)), pl.BlockSpec((None, k, D), lambda i: (0, 0, 0))]
    out_shape = [jax.ShapeDtypeStruct((T, k), F32), jax.ShapeDtypeStruct((1, k, D), F32)]
    aliases = {}
    if gate is not None:
        in_specs += [pl.BlockSpec((tm, k), lambda i: (i, 0)), pl.BlockSpec((tm, k), lambda i: (i, OFF_ZC // k))]
        extra = list(gate) + extra
    if dproj is not None:
        width = k if gate is not None else PAD2
        at = OFF_ZC if gate is not None else OFF_XC - PAD2
        aliases = {3 + len(extra): 2}
        extra = extra + [dproj]
        out_specs.append(pl.BlockSpec((tm, width), lambda i: (i, at // width)))
        out_shape.append(jax.ShapeDtypeStruct((T, NPAD), BF16))
    in_specs += [ANY] * (3 + len(extra) - len(in_specs))

    def body(y_ref, dp_ref, w_ref, *rest):
        dy_ref, dw_ref = rest[len(extra):len(extra) + 2]
        dp = dp_ref[...]
        dy = _dg(dp, _unpack(w_ref[...]), _NT)
        dy_ref[...] = dy
        _acc(dw_ref, _dg(y_ref[...], dp, _TN), pl.program_id(0) == 0)
        if gate is not None:
            z = rest[1][...]
            sg = _sigmoid(z)
            rest[len(extra) + 2][...] = (dy * rest[0][...] * (sg * (1.0 + z * (1.0 - sg)))).astype(BF16)
        elif dproj is not None:
            rest[len(extra) + 2][...] = jnp.zeros((tm, PAD2), BF16)

    return pl.pallas_call(
        body, grid=(T // tm,), in_specs=in_specs, out_specs=out_specs, out_shape=out_shape,
        input_output_aliases=aliases,
        name=f"proj_{tag}_bwd_l{l}", compiler_params=_params(("arbitrary",)))(y, dp, w, *extra)


OUT_TM = 256


def _out_tile(pa, pb, pc, ga, gb, gc, wout, post_g):
    merged = _sigmoid(ga) * pa + _sigmoid(gb) * pb + _sigmoid(gc) * pc
    return _rms(dot_nn(merged, wout), post_g)


def _out_in_specs():
    tm = OUT_TM
    tok = pl.BlockSpec((tm, D), lambda i: (i, 0))
    gate = lambda off: pl.BlockSpec((tm, 512), lambda i, off=off: (i, off // 512))
    return [tok, tok, tok, gate(OFF_GA), gate(OFF_GA + 512), gate(OFF_GB), gate(OFF_GB + 512), gate(OFF_GC),
            gate(OFF_GC + 512), pl.BlockSpec((None, D, D // 2), lambda i: (0, 0, 0)), pl.BlockSpec((1, D), lambda i: (0, 0))]


def _gates(refs):
    return [jnp.concatenate([refs[2 * j][...], refs[2 * j + 1][...]], axis=1) for j in range(3)]


def out_fwd(x, ya, yb, yc, proj, wpa, wpb, wpc, wout, post_g, l):
    tm = OUT_TM

    def body(ya_ref, yb_ref, yc_ref, g0, g1, g2, g3, g4, g5, wo_ref, pg_ref, x_ref, wa_ref, wb_ref, wc_ref,
             o_ref, pa_ref, pb_ref, pc_ref, wa, wb, wc, wo):
        @pl.when(pl.program_id(0) == 0)
        def _():
            for dst, src in ((wa, wa_ref), (wb, wb_ref), (wc, wc_ref), (wo, wo_ref)):
                dst[...] = _unpack(src[...]).astype(BF16)

        pa = _dg(ya_ref[...], wa[...], _NN)
        pb = _dg(yb_ref[...], wb[...], _NN)
        pc = _dg(yc_ref[...], wc[...], _NN)
        ga, gb, gc = _gates([g0, g1, g2, g3, g4, g5])
        o_ref[...] = x_ref[...] + _out_tile(pa, pb, pc, ga, gb, gc, wo[...], pg_ref[...])
        pa_ref[...] = pa.astype(BF16)
        pb_ref[...] = pb.astype(BF16)
        pc_ref[...] = pc.astype(BF16)

    tok = pl.BlockSpec((tm, D), lambda i: (i, 0))
    words = lambda k: pl.BlockSpec((None, k, D // 2), lambda i: (0, 0, 0))
    specs = _out_in_specs()
    specs[2] = pl.BlockSpec((tm, LRU_W), lambda i: (i, 0))
    return pl.pallas_call(
        body, grid=(T // tm,), in_specs=specs + [tok, words(D), words(D), words(LRU_W)], out_specs=[tok] * 4,
        out_shape=[jax.ShapeDtypeStruct((T, D), F32)] + [jax.ShapeDtypeStruct((T, D), BF16)] * 3,
        scratch_shapes=[pltpu.VMEM((D, D), BF16), pltpu.VMEM((D, D), BF16), pltpu.VMEM((LRU_W, D), BF16),
                        pltpu.VMEM((D, D), BF16)],
        name=f"out_fwd_l{l}", compiler_params=_params(("arbitrary",)))(
            ya, yb, yc, proj, proj, proj, proj, proj, proj, wout, post_g, x, wpa, wpb, wpc)


def out_bwd(pa, pb, pc, proj, wout, post_g, dxn, l, dep=None):
    tm = OUT_TM
    nsteps = T // tm

    def body(pa_ref, pb_ref, pc_ref, g0, g1, g2, g3, g4, g5, w_ref, pg_ref, dxn_ref, *rest):
        dpa_ref, dpb_ref, dpc_ref, dproj_ref, dw_ref, dpg_ref, gbuf, sem = rest[-8:]
        i = pl.program_id(0)
        first = i == 0
        slot = i % 2
        ga, gb, gc = _gates([g0, g1, g2, g3, g4, g5])
        _, vjp = jax.vjp(_out_tile, pa_ref[...], pb_ref[...], pc_ref[...], ga, gb, gc, _unpack(w_ref[...]), pg_ref[...])
        dpa, dpb, dpc, dga, dgb, dgc, dw, dpg = vjp(dxn_ref[...])
        dpa_ref[...] = dpa.astype(BF16)
        dpb_ref[...] = dpb.astype(BF16)
        dpc_ref[...] = dpc.astype(BF16)
        _acc(dw_ref, dw, first)
        _acc(dpg_ref, dpg, first)

        def writeback(step, s):
            rows = pl.ds(pl.multiple_of(step * tm, tm), tm)
            return pltpu.make_async_copy(gbuf.at[s], dproj_ref.at[rows, pl.ds(OFF_GA, 3072)], sem.at[s])

        gbuf[slot, :, 0:1024] = dga.astype(BF16)
        gbuf[slot, :, 1024:2048] = dgb.astype(BF16)
        gbuf[slot, :, 2048:3072] = dgc.astype(BF16)
        writeback(i, slot).start()

        @pl.when(i > 0)
        def _():
            writeback(i - 1, 1 - slot).wait()

        @pl.when(i == nsteps - 1)
        def _():
            writeback(i, slot).wait()

    tok = pl.BlockSpec((tm, D), lambda i: (i, 0))
    deps = [] if dep is None else [dep]
    return pl.pallas_call(
        body, grid=(nsteps,), in_specs=_out_in_specs() + [tok] + [ANY] * len(deps),
        out_specs=[tok, tok, tok, ANY, pl.BlockSpec((None, D, D), lambda i: (0, 0, 0)), pl.BlockSpec((1, D), lambda i: (0, 0))],
        out_shape=[jax.ShapeDtypeStruct((T, D), BF16)] * 3 + [jax.ShapeDtypeStruct((T, NPAD), BF16),
                                                            jax.ShapeDtypeStruct((1, D, D), F32), jax.ShapeDtypeStruct((1, D), F32)],
        scratch_shapes=[pltpu.VMEM((2, tm, 3072), BF16), pltpu.SemaphoreType.DMA((2,))],
        name=f"out_bwd_l{l}", compiler_params=_params(("arbitrary",)))(
            pa, pb, pc, proj, proj, proj, proj, proj, proj, wout, post_g, dxn, *deps)


def loss_head(y, target):
    tm = 256

    def body(y_ref, t_ref, loss_ref, dy_ref):
        e = y_ref[...] - t_ref[...]
        dy_ref[...] = e * (1.0 / D)
        val = 0.5 * jnp.sum(jnp.mean(e * e, axis=-1, keepdims=True), axis=0, keepdims=True)
        _acc(loss_ref, jnp.broadcast_to(val, (8, 128)), pl.program_id(0) == 0)

    tok = pl.BlockSpec((tm, D), lambda i: (i, 0))
    total, dy = pl.pallas_call(
        body, grid=(T // tm,), in_specs=[tok, tok],
        out_specs=[pl.BlockSpec((8, 128), lambda i: (0, 0)), tok],
        out_shape=[jax.ShapeDtypeStruct((8, 128), F32), jax.ShapeDtypeStruct((T, D), F32)],
        name="loss_head", compiler_params=_params(("arbitrary",)))(y, target)
    return total[0, 0], dy


def _rope_tables():
    pos = jnp.arange(T, dtype=F32)
    inv_freq = 10000.0 ** (-jnp.arange(0, 64, 2, dtype=F32) / 64)
    ang = pos[:, None] * inv_freq[None, :]
    cos, sin = jnp.cos(ang), jnp.sin(ang)
    ctab = jnp.concatenate([jnp.ones((T, 128), F32), cos, cos], axis=1)
    stab = jnp.concatenate([jnp.zeros((T, 128), F32), -sin, sin], axis=1)
    return ctab, stab


def _layer_fwd(x, l, w, gw, tabs, dep=None, mid=None):
    row = lambda a: a[l][None]
    proj, h = inproj_fwd(x, row(w["pre_norm_g"]), gw["w_in_t"], l, dep)
    ya = gmlp_fwd(proj, row(w["gm_ln_g"]), row(w["gm_ln_b"]), w["gm_ws"][l], w["gm_bs"][l][..., None], l)
    dep2 = None
    if mid is not None:
        gw, dep2 = mid(ya)
    q, k, v = qkv_fwd(proj, row(w["mla_q_norm_g"]), row(w["kv_g384"]), gw["wq"], gw["wkv"], tabs[0], tabs[1], l, dep2)
    yb = attn_fwd(q, k, v, proj, l)
    hseq, yc = lru_fwd(proj, gw["conv"], row(w["lru_conv_b"]), w["lru_w_a"], w["lru_w_x"],
                       row(w["lru_b_a"]), row(w["lru_b_x"]), row(w["lru_lambda"]), l)
    xn, pa, pb, pc = out_fwd(x, ya, yb, yc, proj, gw["w_proj_a"], gw["w_proj_b"], gw["w_proj_c"], gw["w_out"],
                             row(w["post_norm_g"]), l)
    return xn, (x, proj, h, ya, q, k, v, yb, hseq, yc, pa, pb, pc)


def _layer_bwd(dxn, l, w, gw, tabs, saved, dep=None, early=None, mid=None, late=None):
    x, proj, h, ya, q, k, v, yb, hseq, yc, pa, pb, pc = saved
    row = lambda a: a[l][None]
    g, gg = {}, {}
    dpa, dpb, dpc, dproj, gg["w_out"], dpost = out_bwd(pa, pb, pc, proj, gw["w_out"], row(w["post_norm_g"]), dxn, l, dep)
    g["post_norm_g"] = dpost[0]
    dep1 = early(dpa) if early is not None else None
    dya, gg["w_proj_a"], dproj = proj_bwd(ya, dpa, gw["w_proj_a"], l, "a", dep1, dproj)
    dyb, gg["w_proj_b"] = proj_bwd(yb, dpb, gw["w_proj_b"], l, "b")
    dyc, gg["w_proj_c"], dproj = proj_bwd(yc, dpc, gw["w_proj_c"], l, "c", None, dproj, (hseq, proj))
    dproj, dln_g, dln_b, g["gm_ws"], dbs = gmlp_bwd(proj, row(w["gm_ln_g"]), row(w["gm_ln_b"]), w["gm_ws"][l],
                                                   w["gm_bs"][l][..., None], dya, dproj, l)
    g["gm_ln_g"], g["gm_ln_b"], g["gm_bs"] = dln_g[0], dln_b[0], dbs[..., 0]
    dq, dk, dv, dproj = attn_bwd(q, k, v, proj, dyb, dproj, l)
    dproj, dqg, dkvg, dwq, dwkv = qkv_bwd(proj, row(w["mla_q_norm_g"]), row(w["kv_g384"]), gw["wq"], gw["wkv"],
                                          tabs[0], tabs[1], dq, dk, dv, dproj, l)
    gg["wq"], gg["wkv"] = dwq.reshape(1, 1536, 384), dwkv.reshape(1, 2048, 256)
    g["mla_q_norm_g"], g["mla_kv_norm_g"] = dqg[0], dkvg[0, :256]
    dproj, dcw, dcb, dwa, dwx, dba, dbx, dlam = lru_bwd(
        proj, hseq, dyc, gw["conv"], row(w["lru_conv_b"]), w["lru_w_a"], w["lru_w_x"],
        row(w["lru_b_a"]), row(w["lru_b_x"]), row(w["lru_lambda"]), dproj, l)
    gg["conv"] = jnp.pad(dcw.T, ((0, 0), (0, 124)))[None]
    g["lru_conv_b"], g["lru_b_a"], g["lru_b_x"], g["lru_lambda"] = dcb[0], dba[0], dbx[0], dlam[0]
    g["lru_w_a"], g["lru_w_x"] = dwa, dwx
    dep2 = mid(gg, dproj) if mid is not None else None
    gg["w_in_t"], dh = inproj_bwd(dproj, h, gw["w_in_t"], l, dep2)
    dep3 = late(gg["w_in_t"]) if late is not None else None
    dx, dpre = prenorm_bwd(x, row(w["pre_norm_g"]), dh, dxn, l, dep3)
    g["pre_norm_g"] = dpre[0]
    return dx, gg, g


MESH = pl.DeviceIdType.MESH
HBM = pl.BlockSpec(memory_space=pltpu.HBM)
SEM = pl.BlockSpec(memory_space=pltpu.SEMAPHORE)
EFFECT = pltpu.SideEffectType.DATAFLOW_SIDE_EFFECTING
FLIPS = ((1, 0), (0, 1), (1, 1))


def _win_off(k, s):
    g = SHARD * k + s
    return g + jnp.where(g >= PAD1_AT, PAD1, 0) + jnp.where(g >= PAD2_AT, PAD2, 0)


def _plain_off(rows):
    return lambda k, s: rows * k + s


class Spec:
    def __init__(self, rows, cols, full_rows, pieces=None, off=None, layers=1, packed=None):
        self.rows, self.cols, self.full_rows, self.layers = rows, cols, full_rows, layers
        self.pieces = pieces or ((0, rows),)
        self.off = off or _plain_off(rows)
        self.packed = cols % 256 == 0 if packed is None else packed
        self.wcols = cols // 2 if self.packed else cols

    def to_words(self, a):
        return _pack(a) if self.packed else a

    def from_words(self, p):
        return _unpack(p) if self.packed else p


def _pack(a):
    def bits(v):
        u = lax.bitcast_convert_type(v, jnp.uint32)
        return u + jnp.uint32(0x7FFF) + ((u >> 16) & jnp.uint32(1))

    words = [(bits(a[:, g:g + 128]) >> 16) | (bits(a[:, g + 128:g + 256]) & jnp.uint32(0xFFFF0000))
             for g in range(0, a.shape[-1], 256)]
    return lax.bitcast_convert_type(jnp.concatenate(words, axis=-1) if len(words) > 1 else words[0], F32)


def _unpack(p):
    w = lax.bitcast_convert_type(p, jnp.uint32)
    lo = lax.bitcast_convert_type(w << 16, F32)
    hi = lax.bitcast_convert_type(w & jnp.uint32(0xFFFF0000), F32)
    return jnp.concatenate([h[:, g:g + 128] for g in range(0, p.shape[-1], 128) for h in (lo, hi)], axis=-1)


WEIGHT_SPECS = {
    "w_in_t": Spec(SHARD, D, NPAD, WIN_PIECES, _win_off),
    "wq": Spec(192, 384, 1536),
    "wkv": Spec(256, 256, 2048),
    "conv": Spec(160, 128, 1280),
    "w_proj_a": Spec(128, D, 1024),
    "w_proj_b": Spec(128, D, 1024),
    "w_proj_c": Spec(160, D, 1280),
    "w_out": Spec(128, D, 1024),
}
REP_ROWS = 72
REP_SPEC = Spec(REP_ROWS, D, REP_ROWS * NDEV, packed=False)


def _coords():
    return lax.axis_index("x"), lax.axis_index("y"), lax.axis_index("c")


def _rows(ref, start, n):
    if not isinstance(start, int):
        start = pl.multiple_of(start, 8)
    return ref.at[:, pl.ds(start, n), :]


def _col_tile(cols):
    return 256 if cols % 256 == 0 else cols


def _n_pieces(specs):
    return sum(len(sp.pieces) for sp in specs)


def pack_place(shard, sp, layer, tag, dep=None):
    gaps = ((PAD1_AT, PAD1), (PAD2_AT + PAD1, PAD2)) if sp.off is _win_off else ()
    npc = len(sp.pieces)
    deps = [] if dep is None else [dep]

    def body(s_ref, *rest):
        words_ref, full_ref, buf, zbuf, sem = rest[-5:]
        l = 0
        x, y, c = _coords()
        me = 4 * x + 2 * y + c
        words = sp.to_words(s_ref[...])
        words_ref[...] = words
        buf[...] = words
        copies = [pltpu.make_async_copy(buf.at[pl.ds(s, n), :],
                                        full_ref.at[l, pl.ds(pl.multiple_of(sp.off(me, s), 8), n), :], sem.at[i])
                  for i, (s, n) in enumerate(sp.pieces)]
        if gaps:
            zbuf[...] = jnp.zeros_like(zbuf)
            copies += [pltpu.make_async_copy(zbuf.at[pl.ds(0, n), :], full_ref.at[l, pl.ds(at, n), :], sem.at[npc + i])
                       for i, (at, n) in enumerate(gaps)]
        for cp in copies:
            cp.start()
        for cp in copies:
            cp.wait()

    return pl.pallas_call(
        body, grid=(1,), in_specs=[pl.BlockSpec((None, sp.rows, sp.cols), lambda i: (layer, 0, 0))] + [ANY] * len(deps),
        out_specs=[pl.BlockSpec((None, sp.rows, sp.wcols), lambda i: (0, 0, 0)), ANY],
        out_shape=[jax.ShapeDtypeStruct((sp.layers, sp.rows, sp.wcols), F32),
                   jax.ShapeDtypeStruct((sp.layers, sp.full_rows, sp.wcols), F32)],
        scratch_shapes=[pltpu.VMEM((sp.rows, sp.wcols), F32), pltpu.VMEM((PAD2 if gaps else 8, sp.wcols), F32),
                        pltpu.SemaphoreType.DMA((npc + len(gaps),))],
        name=f"pack_place_{tag}", compiler_params=_params(("arbitrary",)))(shard, *deps)


def _gather_copies(srcs, bufs, specs, ssem, rsem, landing):
    x, y, c = _coords()
    me = 4 * x + 2 * y + c
    targets = [(x, y, 1 - c)] + [(x ^ fx, y ^ fy, c) for fx, fy in FLIPS]
    copies = []
    p = 0
    for src, buf, sp in zip(srcs, bufs, specs):
        for s, n in sp.pieces:
            for t, (tx, ty, tc) in enumerate(targets):
                owner = 4 * tx + 2 * ty + tc if landing else me
                copies.append(pltpu.make_async_remote_copy(_rows(src, s, n), _rows(buf, sp.off(owner, s), n),
                                                           ssem.at[4 * p + t], rsem.at[4 * p + t],
                                                           device_id=(tx, ty, tc), device_id_type=MESH))
            p += 1
    return copies


def gather_send(words, fulls, specs, tag):
    ns, npc = len(specs), _n_pieces(specs)

    def body(*refs):
        srcs, bufs, sems = refs[:ns], refs[2 * ns:3 * ns], refs[3 * ns:]
        for cp in _gather_copies(srcs, bufs, specs, *sems, False):
            cp.start()
        for cp in _gather_copies(srcs, bufs, specs, *sems, False):
            cp.wait_send()
        for cp in _gather_copies(srcs, bufs, specs, *sems, True):
            cp.wait_recv()

    return pl.pallas_call(
        body, in_specs=[ANY] * (2 * ns), out_specs=[ANY] * ns,
        out_shape=[jax.ShapeDtypeStruct(f.shape, f.dtype) for f in fulls],
        input_output_aliases={ns + i: i for i in range(ns)},
        scratch_shapes=[pltpu.SemaphoreType.DMA((4 * npc,)), pltpu.SemaphoreType.DMA((4 * npc,))],
        name=f"gather_send_{tag}", compiler_params=pltpu.CompilerParams(has_side_effects=True))(*words, *fulls)


def _in_hbm(arrays):
    return [pltpu.with_memory_space_constraint(a, pltpu.HBM) for a in arrays]


def gather_start(words, fulls, specs, dep, tag):
    ns, npc = len(specs), _n_pieces(specs)
    deps = [] if dep is None else [dep]

    def body(*refs):
        ssem, rsem = refs[2 * ns + len(deps):2 * ns + len(deps) + 2]
        for cp in _gather_copies(refs[:ns], refs[ns:2 * ns], specs, ssem, rsem, False):
            cp.start()
        refs[-1][...] = jnp.zeros_like(refs[-1])

    outs = pl.pallas_call(
        body, in_specs=[HBM] * (2 * ns) + [ANY] * len(deps),
        out_specs=[SEM, SEM] + [HBM] * (2 * ns) + [pl.BlockSpec(memory_space=pltpu.VMEM)],
        out_shape=[pltpu.SemaphoreType.DMA((4 * npc,)), pltpu.SemaphoreType.DMA((4 * npc,))]
        + [pltpu.HBM(a.shape, a.dtype) for a in list(words) + list(fulls)] + [jax.ShapeDtypeStruct((8, 128), F32)],
        input_output_aliases={i: 2 + i for i in range(2 * ns)},
        name=f"gather_start_{tag}", compiler_params=pltpu.CompilerParams(has_side_effects=EFFECT))(
            *_in_hbm(list(words) + list(fulls)), *deps)
    return outs[0], outs[1], outs[2:2 + ns], outs[2 + ns:2 + 2 * ns], outs[-1]


def gather_wait(ssem, rsem, words, fulls, specs, after, tag):
    ns = len(specs)

    def body(*refs):
        srcs, bufs, ssem, rsem = refs[:ns], refs[ns:2 * ns], refs[2 * ns], refs[2 * ns + 1]
        for cp in _gather_copies(srcs, bufs, specs, ssem, rsem, False):
            cp.wait_send()
        for cp in _gather_copies(srcs, bufs, specs, ssem, rsem, True):
            cp.wait_recv()

    outs = pl.pallas_call(
        body, in_specs=[HBM] * (2 * ns) + [SEM, SEM, ANY], out_specs=[HBM] * (2 * ns),
        out_shape=[pltpu.HBM(a.shape, a.dtype) for a in list(words) + list(fulls)],
        input_output_aliases={i: i for i in range(2 * ns)},
        name=f"gather_wait_{tag}", compiler_params=pltpu.CompilerParams(has_side_effects=EFFECT))(
            *words, *fulls, ssem, rsem, after)
    return outs[ns:]


def gather_forward(fulls, specs, tag):
    ns, npc = len(specs), _n_pieces(specs)

    def body(*refs):
        bufs = refs[ns:2 * ns]
        ssem, rsem = refs[2 * ns:]
        x, y, c = _coords()
        sibling = (x, y, 1 - c)
        waits = []
        p = 0
        for buf, sp in zip(bufs, specs):
            for s, n in sp.pieces:
                for t, (fx, fy) in enumerate(FLIPS):
                    chip = 4 * (x ^ fx) + 2 * (y ^ fy)
                    here = _rows(buf, sp.off(chip + c, s), n)
                    send = pltpu.make_async_remote_copy(here, here, ssem.at[t, p], rsem.at[t, p],
                                                        device_id=sibling, device_id_type=MESH)
                    send.start()
                    waits.append(send.wait_send)
                    there = _rows(buf, sp.off(chip + 1 - c, s), n)
                    waits.append(pltpu.make_async_remote_copy(here, there, ssem.at[t, p], rsem.at[t, p],
                                                              device_id=sibling, device_id_type=MESH).wait_recv)
                p += 1
        for w in waits:
            w()

    return pl.pallas_call(
        body, in_specs=[ANY] * ns, out_specs=[ANY] * ns,
        out_shape=[jax.ShapeDtypeStruct(f.shape, f.dtype) for f in fulls],
        input_output_aliases={i: i for i in range(ns)},
        scratch_shapes=[pltpu.SemaphoreType.DMA((3, npc)), pltpu.SemaphoreType.DMA((3, npc))],
        name=f"gather_forward_{tag}", compiler_params=pltpu.CompilerParams(has_side_effects=True))(*fulls)


def all_gather(shards, layer, specs, names, tag):
    placed = [pack_place(s, sp, layer, f"{tag}_{n}") for s, sp, n in zip(shards, specs, names)]
    fulls = gather_send([p[0] for p in placed], [p[1] for p in placed], specs, tag)
    return gather_forward(fulls, specs, tag)


def _pair_copies(srcs, theirs, specs, ssem, rsem):
    x, y, c = _coords()
    copies = []
    p = 0
    for src, their, sp in zip(srcs, theirs, specs):
        for s, n in sp.pieces:
            for j in range(4):
                copies.append(pltpu.make_async_remote_copy(_rows(src, sp.off(2 * j + 1 - c, s), n), _rows(their.at[j], s, n),
                                                           ssem.at[4 * p + j], rsem.at[4 * p + j],
                                                           device_id=(x, y, 1 - c), device_id_type=MESH))
            p += 1
    return copies


def _pair_shapes(specs):
    return [(4, sp.layers, sp.rows, sp.cols) for sp in specs]


def reduce_pair(grads, specs, tag, dep=None):
    ns, npc = len(specs), _n_pieces(specs)
    deps = [] if dep is None else [dep]

    def body(*refs):
        copies = _pair_copies(refs[:ns], refs[ns + len(deps):2 * ns + len(deps)], specs, *refs[2 * ns + len(deps):])
        for cp in copies:
            cp.start()
        for cp in copies:
            cp.wait()

    return pl.pallas_call(
        body, in_specs=[ANY] * (ns + len(deps)), out_specs=[ANY] * ns,
        out_shape=[jax.ShapeDtypeStruct(s, F32) for s in _pair_shapes(specs)],
        scratch_shapes=[pltpu.SemaphoreType.DMA((4 * npc,)), pltpu.SemaphoreType.DMA((4 * npc,))],
        name=f"reduce_pair_{tag}", compiler_params=pltpu.CompilerParams(has_side_effects=True))(*grads, *deps)


def pair_start(grads, specs, dep, tag):
    ns, npc = len(specs), _n_pieces(specs)
    slots = [lax.empty(s, F32) for s in _pair_shapes(specs)]
    deps = [] if dep is None else [dep]

    def body(*refs):
        ssem, rsem = refs[2 * ns + len(deps):2 * ns + len(deps) + 2]
        for cp in _pair_copies(refs[:ns], refs[ns:2 * ns], specs, ssem, rsem):
            cp.start()
        refs[-1][...] = jnp.zeros_like(refs[-1])

    outs = pl.pallas_call(
        body, in_specs=[HBM] * (2 * ns) + [ANY] * len(deps),
        out_specs=[SEM, SEM] + [HBM] * (2 * ns) + [pl.BlockSpec(memory_space=pltpu.VMEM)],
        out_shape=[pltpu.SemaphoreType.DMA((4 * npc,)), pltpu.SemaphoreType.DMA((4 * npc,))]
        + [pltpu.HBM(a.shape, a.dtype) for a in list(grads) + slots] + [jax.ShapeDtypeStruct((8, 128), F32)],
        input_output_aliases={i: 2 + i for i in range(2 * ns)},
        name=f"pair_start_{tag}", compiler_params=pltpu.CompilerParams(has_side_effects=EFFECT))(
            *_in_hbm(list(grads) + slots), *deps)
    return outs[0], outs[1], outs[2:2 + ns], outs[2 + ns:2 + 2 * ns], outs[-1]


def pair_wait(ssem, rsem, grads, slots, specs, after, tag):
    ns = len(specs)

    def body(*refs):
        for cp in _pair_copies(refs[:ns], refs[ns:2 * ns], specs, refs[2 * ns], refs[2 * ns + 1]):
            cp.wait_send()
            cp.wait_recv()

    outs = pl.pallas_call(
        body, in_specs=[HBM] * (2 * ns) + [SEM, SEM, ANY], out_specs=[HBM] * (2 * ns),
        out_shape=[pltpu.HBM(a.shape, a.dtype) for a in list(grads) + list(slots)],
        input_output_aliases={i: i for i in range(2 * ns)},
        name=f"pair_wait_{tag}", compiler_params=pltpu.CompilerParams(has_side_effects=EFFECT))(
            *grads, *slots, ssem, rsem, after)
    return outs[:ns], outs[ns:]


def pair_sum(g, r1, sp, tag):
    npc = len(sp.pieces)
    fetch_all = 4 * sp.rows * sp.cols * 4 <= (8 << 20)

    def body(g_ref, r_ref, own_ref, words_ref, gbuf, sem):
        l, j = pl.program_id(0), pl.program_id(1)
        x, y, c = _coords()

        def copies(chip, slot):
            return [pltpu.make_async_copy(g_ref.at[l, pl.ds(pl.multiple_of(sp.off(2 * chip + c, s), 8), n), :],
                                          gbuf.at[slot, pl.ds(s, n), :], sem.at[slot, i])
                    for i, (s, n) in enumerate(sp.pieces)]

        def fetch(chip, slot):
            for cp in copies(chip, slot):
                cp.start()

        def arrived(chip, slot):
            for cp in copies(chip, slot):
                cp.wait()

        if fetch_all:
            @pl.when(j == 0)
            def _():
                for chip in range(4):
                    fetch(chip, chip)
                for chip in range(4):
                    arrived(chip, chip)

            mine = gbuf[j]
        else:
            @pl.when(j == 0)
            def _():
                fetch(0, 0)

            @pl.when(j < 3)
            def _():
                fetch(j + 1, (j + 1) % 2)

            arrived(j, j % 2)
            mine = gbuf[j % 2]
        p = mine + r_ref[...]
        words_ref[...] = sp.to_words(p)

        @pl.when(j == 2 * x + y)
        def _():
            own_ref[...] = p

    return pl.pallas_call(
        body, grid=(sp.layers, 4),
        in_specs=[ANY, pl.BlockSpec((None, None, sp.rows, sp.cols), lambda l, j: (j, l, 0, 0))],
        out_specs=[pl.BlockSpec((None, sp.rows, sp.cols), lambda l, j: (l, 0, 0)),
                   pl.BlockSpec((None, None, sp.rows, sp.wcols), lambda l, j: (j, l, 0, 0))],
        out_shape=[jax.ShapeDtypeStruct((sp.layers, sp.rows, sp.cols), F32),
                   jax.ShapeDtypeStruct((4, sp.layers, sp.rows, sp.wcols), F32)],
        scratch_shapes=[pltpu.VMEM((4 if fetch_all else 2, sp.rows, sp.cols), F32), pltpu.SemaphoreType.DMA((4, npc))],
        name=f"pair_sum_{tag}", compiler_params=_params(("arbitrary", "arbitrary")))(g, r1)


def _chip_copies(srcs, dsts, ssem, rsem):
    x, y, c = _coords()
    copies = []
    for i, (src, dst) in enumerate(zip(srcs, dsts)):
        for t, (fx, fy) in enumerate(FLIPS):
            tx, ty = x ^ fx, y ^ fy
            copies.append(pltpu.make_async_remote_copy(src.at[2 * tx + ty], dst.at[t], ssem.at[3 * i + t], rsem.at[3 * i + t],
                                                       device_id=(tx, ty, c), device_id_type=MESH))
    return copies


def _slot_shapes(words):
    return [(3,) + w.shape[1:] for w in words]


def reduce_chips(words, specs, tag):
    ns = len(specs)

    def body(*refs):
        copies = _chip_copies(refs[:ns], refs[ns:2 * ns], *refs[2 * ns:])
        for cp in copies:
            cp.start()
        for cp in copies:
            cp.wait()

    return pl.pallas_call(
        body, in_specs=[ANY] * ns, out_specs=[ANY] * ns,
        out_shape=[jax.ShapeDtypeStruct(s, F32) for s in _slot_shapes(words)],
        scratch_shapes=[pltpu.SemaphoreType.DMA((3 * ns,)), pltpu.SemaphoreType.DMA((3 * ns,))],
        name=f"reduce_chips_{tag}", compiler_params=pltpu.CompilerParams(has_side_effects=True))(*words)


def chips_start(words, specs, tag):
    ns = len(specs)
    slots = [lax.empty(s, F32) for s in _slot_shapes(words)]

    def body(*refs):
        ssem, rsem = refs[2 * ns:2 * ns + 2]
        for cp in _chip_copies(refs[:ns], refs[ns:2 * ns], ssem, rsem):
            cp.start()
        refs[-1][...] = jnp.zeros_like(refs[-1])

    outs = pl.pallas_call(
        body, in_specs=[HBM] * (2 * ns),
        out_specs=[SEM, SEM] + [HBM] * (2 * ns) + [pl.BlockSpec(memory_space=pltpu.VMEM)],
        out_shape=[pltpu.SemaphoreType.DMA((3 * ns,)), pltpu.SemaphoreType.DMA((3 * ns,))]
        + [pltpu.HBM(a.shape, a.dtype) for a in list(words) + slots] + [jax.ShapeDtypeStruct((8, 128), F32)],
        input_output_aliases={i: 2 + i for i in range(2 * ns)},
        name=f"chips_start_{tag}", compiler_params=pltpu.CompilerParams(has_side_effects=EFFECT))(
            *_in_hbm(list(words) + slots))
    return outs[0], outs[1], outs[2:2 + ns], outs[2 + ns:2 + 2 * ns], outs[-1]


def chips_wait(ssem, rsem, words, slots, specs, after, tag):
    ns = len(specs)

    def body(*refs):
        for cp in _chip_copies(refs[:ns], refs[ns:2 * ns], refs[2 * ns], refs[2 * ns + 1]):
            cp.wait_send()
            cp.wait_recv()

    outs = pl.pallas_call(
        body, in_specs=[HBM] * (2 * ns) + [SEM, SEM, ANY], out_specs=[HBM] * (2 * ns),
        out_shape=[pltpu.HBM(a.shape, a.dtype) for a in list(words) + list(slots)],
        input_output_aliases={i: i for i in range(2 * ns)},
        name=f"chips_wait_{tag}", compiler_params=pltpu.CompilerParams(has_side_effects=EFFECT))(
            *words, *slots, ssem, rsem, after)
    return outs[ns:]


def sum_chips(own, r2, sp, tag):
    def body(own_ref, r_ref, o_ref):
        o_ref[...] = ((own_ref[...] + sp.from_words(r_ref[0])) + sp.from_words(r_ref[1])) + sp.from_words(r_ref[2])

    blk = pl.BlockSpec((None, sp.rows, sp.cols), lambda l: (l, 0, 0))
    return pl.pallas_call(
        body, grid=(sp.layers,), in_specs=[blk, pl.BlockSpec((3, None, sp.rows, sp.wcols), lambda l: (0, l, 0, 0))],
        out_specs=blk, out_shape=jax.ShapeDtypeStruct((sp.layers, sp.rows, sp.cols), F32),
        name=f"sum_chips_{tag}", compiler_params=_params(("arbitrary",)))(own, r2)


def reduce_scatter_start(grads, specs, names, dep, tag):
    theirs = reduce_pair(grads, specs, tag, dep)
    sums = [pair_sum(g, r1, sp, f"{tag}_{n}") for g, r1, sp, n in zip(grads, theirs, specs, names)]
    ssem, rsem, words, slots, token = chips_start([s[1] for s in sums], specs, tag)
    return (ssem, rsem, words, slots, [s[0] for s in sums]), token


def reduce_scatter_finish(state, after, specs, tag):
    ssem, rsem, words, slots, own = state
    return list(zip(own, chips_wait(ssem, rsem, words, slots, specs, after, tag)))


def reduce_scatter(grads, specs, names, tag, dep=None):
    theirs = reduce_pair(grads, specs, tag, dep)
    sums = [pair_sum(g, r1, sp, f"{tag}_{n}") for g, r1, sp, n in zip(grads, theirs, specs, names)]
    return list(zip([s[0] for s in sums], reduce_chips([s[1] for s in sums], specs, tag)))


def _adamw_math(w, g, m, v):
    c1 = 1.0 - ADAM_B1 ** ADAM_STEP
    c2 = 1.0 - ADAM_B2 ** ADAM_STEP
    m2 = ADAM_B1 * m + (1.0 - ADAM_B1) * g
    v2 = ADAM_B2 * v + (1.0 - ADAM_B2) * (g * g)
    return -ADAM_LR * ((m2 / c1) / (jnp.sqrt(v2 / c2) + ADAM_EPS) + ADAM_WD * w), m2, v2


def adamw_small(ws, gs, ms, vs):
    n = len(ws)
    flat = lambda a: a.reshape(math.prod(a.shape[:-1]), a.shape[-1])

    def body(*refs):
        ins, outs = refs[:4 * n], refs[4 * n:]
        for i in range(n):
            w_ref, g_ref, m_ref, v_ref = ins[4 * i:4 * i + 4]
            outs[3 * i][...], outs[3 * i + 1][...], outs[3 * i + 2][...] = _adamw_math(
                w_ref[...], g_ref[...], m_ref[...], v_ref[...])

    args = [flat(a) for quad in zip(ws, gs, ms, vs) for a in quad]
    res = pl.pallas_call(
        body, out_shape=[jax.ShapeDtypeStruct(flat(w).shape, F32) for w in ws for _ in range(3)],
        name="adamw_small", compiler_params=_params())(*args)
    return [[res[3 * i + k].reshape(ws[i].shape) for k in range(3)] for i in range(n)]


def adamw_layer(w, sums, m, v, sp, l, prev, dep, name):
    _, rows, cols = w.shape
    tc = _col_tile(cols)
    twc = tc // 2 if sp.packed else tc
    extra = ([] if prev is None else list(prev)) + ([] if dep is None else [dep])

    def body(w_ref, own_ref, r_ref, m_ref, v_ref, *rest):
        g_ref, d_ref, nm_ref, nv_ref = rest[-4:]
        g = ((own_ref[...] + sp.from_words(r_ref[0])) + sp.from_words(r_ref[1])) + sp.from_words(r_ref[2])
        g_ref[...] = g
        d_ref[...], nm_ref[...], nv_ref[...] = _adamw_math(w_ref[...], g, m_ref[...], v_ref[...])

    blk = pl.BlockSpec((None, rows, tc), lambda n: (l, 0, n))
    return pl.pallas_call(
        body, grid=(cols // tc,),
        in_specs=[blk, pl.BlockSpec((None, rows, tc), lambda n: (0, 0, n)),
                  pl.BlockSpec((3, None, rows, twc), lambda n: (0, 0, 0, n)), blk, blk] + [ANY] * len(extra),
        out_specs=[blk] * 4, out_shape=[jax.ShapeDtypeStruct(w.shape, F32)] * 4,
        input_output_aliases={} if prev is None else {5 + i: i for i in range(4)},
        name=f"adamw_{name}_l{l}", compiler_params=_params(("arbitrary",)))(w, sums[0], sums[1], m, v, *extra)


WEIGHTS = ("pre_norm_g", "w_in", "gm_ln_g", "gm_ln_b", "gm_ws", "gm_bs", "mla_q_norm_g", "mla_w_uq", "mla_kv_norm_g",
           "mla_w_ukv", "lru_conv_w", "lru_conv_b", "lru_w_a", "lru_b_a", "lru_w_x", "lru_b_x", "lru_lambda",
           "w_proj_a", "w_proj_b", "w_proj_c", "w_out", "post_norm_g")
SHARDED = ("w_in", "mla_w_uq", "mla_w_ukv", "lru_conv_w", "w_proj_a", "w_proj_b", "w_proj_c", "w_out")
REPLICATED = tuple(n for n in WEIGHTS if n not in SHARDED)


def _step(x, target, wts, ms, vs):
    t12 = lambda a: jnp.swapaxes(a, 1, 2)
    names = list(WEIGHT_SPECS)
    specs = [WEIGHT_SPECS[n] for n in names]
    tabs = _rope_tables()
    own = {"w_in_t": t12(wts["w_in"]), "wq": t12(wts["mla_w_uq"]), "wkv": t12(wts["mla_w_ukv"]),
           "conv": jnp.pad(t12(wts["lru_conv_w"]), ((0, 0), (0, 0), (0, 124))),
           "w_proj_a": wts["w_proj_a"], "w_proj_b": wts["w_proj_b"], "w_proj_c": wts["w_proj_c"], "w_out": wts["w_out"]}
    first, rest = ["w_in_t"], [n for n in names if n != "w_in_t"]
    sfirst, srest = [WEIGHT_SPECS[n] for n in first], [WEIGHT_SPECS[n] for n in rest]

    w = {n: wts[n] for n in REPLICATED}
    w["kv_g384"] = jnp.concatenate([wts["mla_kv_norm_g"], jnp.ones((L, 128), F32)], axis=1)

    def layer_weights(ns, words):
        gw = dict(zip(ns, words))
        gw["wq"] = gw["wq"].reshape(HEADS, 192, 384)
        gw["wkv"] = gw["wkv"].reshape(HEADS, 256, 128)
        gw["conv"] = gw["conv"][0, :, :4].T
        return gw

    place = lambda l, dep: {n: pack_place(own[n], WEIGHT_SPECS[n], l, f"w{l}_{n}", dep) for n in names}
    placed = [place(0, None)]
    words_of = lambda l, ns: [placed[l][n][0] for n in ns]
    bufs_of = lambda l, ns: [placed[l][n][1] for n in ns]
    later = {}

    ssem_a, rsem_a, wthru_a, fthru_a, token_a = gather_start(words_of(0, first), bufs_of(0, first), sfirst, None, "w0a")
    placed.append(place(1, token_a))
    win0 = gather_forward(gather_wait(ssem_a, rsem_a, wthru_a, fthru_a, sfirst, placed[1]["w_in_t"][0], "w0a"), sfirst, "w0a")
    ssem_b, rsem_b, wthru_b, fthru_b, token_b = gather_start(words_of(0, rest), bufs_of(0, rest), srest, win0[0], "w0b")
    ssem1, rsem1, wthru1, fthru1, token1 = gather_start(words_of(1, names), bufs_of(1, names), specs, token_b, "w1")

    def fwd0_mid(ya):
        rest0 = gather_forward(gather_wait(ssem_b, rsem_b, wthru_b, fthru_b, srest, ya, "w0b"), srest, "w0b")
        later["gw0"] = layer_weights(first + rest, list(win0) + list(rest0))
        return later["gw0"], None

    x1, saved0 = _layer_fwd(x, 0, w, {"w_in_t": win0[0]}, tabs, dep=token1, mid=fwd0_mid)
    words1 = gather_forward(gather_wait(ssem1, rsem1, wthru1, fthru1, specs, x1, "w1"), specs, "w1")
    gw0, gw1 = later["gw0"], layer_weights(names, words1)
    x2, saved1 = _layer_fwd(x1, 1, w, gw1, tabs)
    loss, dx2 = loss_head(x2, target)

    def bwd1_mid(gg, last):
        later["p1b"] = pair_start([gg[n] for n in rest], srest, last, "g1b")
        return later["p1b"][4]

    dx1, gg1, g1 = _layer_bwd(dx2, 1, w, gw1, tabs, saved1, mid=bwd1_mid)
    grads1b, theirs1b = pair_wait(*later["p1b"][:4], srest, dx1, "g1b")
    p1a = pair_start([gg1["w_in_t"]], sfirst, theirs1b[0], "g1a")

    def bwd0_early(last):
        grads1a, theirs1a = pair_wait(*p1a[:4], sfirst, last, "g1a")
        mine = dict(zip(first + rest, list(grads1a) + list(grads1b)))
        theirs = dict(zip(first + rest, list(theirs1a) + list(theirs1b)))
        sums = [pair_sum(mine[n], theirs[n], WEIGHT_SPECS[n], f"g1_{n}") for n in names]
        ssem, rsem, words, slots, token = chips_start([s[1] for s in sums], specs, "g1")
        later["g1"] = (ssem, rsem, words, slots, [s[0] for s in sums])
        return token

    def bwd0_mid(gg, last):
        later["g0b"], token = reduce_scatter_start([gg[n] for n in rest], srest, rest, last, "g0b")
        return token

    def bwd0_late(g_win):
        later["p0a"] = pair_start([g_win], sfirst, None, "g0a")
        return later["p0a"][4]

    dx0, gg0, g0 = _layer_bwd(dx1, 0, w, gw0, tabs, saved0, dep=p1a[4], early=bwd0_early, mid=bwd0_mid, late=bwd0_late)
    s1 = dict(zip(names, reduce_scatter_finish(later["g1"], dx0, specs, "g1")))
    s0 = dict(zip(rest, reduce_scatter_finish(later["g0b"], dx0, srest, "g0b")))

    grads0a, theirs0a = pair_wait(*later["p0a"][:4], sfirst, dx0, "g0a")
    own0a, words0a = pair_sum(grads0a[0], theirs0a[0], sfirst[0], "g0a_w_in_t")
    ssem_g, rsem_g, wthru_g, slots_g, token_g = chips_start([words0a], sfirst, "g0a")

    keys = {"w_in": "w_in_t", "mla_w_uq": "wq", "mla_w_ukv": "wkv",
            "w_proj_a": "w_proj_a", "w_proj_b": "w_proj_b", "w_proj_c": "w_proj_c", "w_out": "w_out"}
    transposed = ("w_in", "mla_w_uq", "mla_w_ukv")
    state_of = lambda n: [own[keys[n]], t12(ms[n]), t12(vs[n])] if n in transposed else [wts[n], ms[n], vs[n]]

    def update(n, l, sums, prev, dep):
        wl, ml, vl = state_of(n)
        return adamw_layer(wl, sums[keys[n]], ml, vl, WEIGHT_SPECS[keys[n]], l, prev, dep, n)

    upd = {n: update(n, 1, s1, None, token_g) for n in keys}
    for n in keys:
        if n != "w_in":
            upd[n] = update(n, 0, s0, upd[n], None)
    rep_flat = jnp.concatenate([jnp.stack([g0[n], g1[n]]).reshape(-1) for n in REPLICATED] + [loss[None]])
    rep_flat = jnp.pad(rep_flat, (0, REP_ROWS * NDEV * D - rep_flat.shape[0])).reshape(1, REP_ROWS * NDEV, D)
    rep_parts = reduce_scatter([rep_flat], [REP_SPEC], ["rep"], "rep", upd["w_out"][0])[0]
    rep_sum = sum_chips(*rep_parts, REP_SPEC, "rep")
    rep_full = all_gather([rep_sum], 0, [REP_SPEC], ["rep"], "rep")[0].reshape(-1)

    out = {}
    conv_sp = WEIGHT_SPECS["conv"]
    g_conv = t12(jnp.concatenate([sum_chips(*s0["conv"], conv_sp, "conv0"), sum_chips(*s1["conv"], conv_sp, "conv1")])[:, :, :4])
    small = {"lru_conv_w": g_conv}
    at = 0
    for n in REPLICATED:
        size = math.prod(wts[n].shape)
        small[n] = rep_full[at:at + size].reshape(wts[n].shape)
        at += size
    updates = adamw_small([wts[n] for n in small], list(small.values()), [ms[n] for n in small], [vs[n] for n in small])
    for n, u in zip(small, updates):
        out[n] = [small[n]] + u

    landed = chips_wait(ssem_g, rsem_g, wthru_g, slots_g, sfirst, out[REPLICATED[-1]][1], "g0a")
    s0["w_in_t"] = (own0a, landed[0])
    upd["w_in"] = update("w_in", 0, s0, upd["w_in"], None)
    out.update({n: [t12(r) for r in upd[n]] if n in transposed else upd[n] for n in keys})

    return (rep_full[at], dx0[None], *[out[n][k] for k in range(4) for n in WEIGHTS])


def kernel(x, pre_norm_g, w_in, gm_ln_g, gm_ln_b, gm_ws, gm_bs, mla_q_norm_g, mla_w_uq, mla_kv_norm_g, mla_w_ukv, lru_conv_w, lru_conv_b, lru_w_a, lru_b_a, lru_w_x, lru_b_x, lru_lambda, w_proj_a, w_proj_b, w_proj_c, w_out, post_norm_g, loss_target, m_pre_norm_g, m_w_in, m_gm_ln_g, m_gm_ln_b, m_gm_ws, m_gm_bs, m_mla_q_norm_g, m_mla_w_uq, m_mla_kv_norm_g, m_mla_w_ukv, m_lru_conv_w, m_lru_conv_b, m_lru_w_a, m_lru_b_a, m_lru_w_x, m_lru_b_x, m_lru_lambda, m_w_proj_a, m_w_proj_b, m_w_proj_c, m_w_out, m_post_norm_g, v_pre_norm_g, v_w_in, v_gm_ln_g, v_gm_ln_b, v_gm_ws, v_gm_bs, v_mla_q_norm_g, v_mla_w_uq, v_mla_kv_norm_g, v_mla_w_ukv, v_lru_conv_w, v_lru_conv_b, v_lru_w_a, v_lru_b_a, v_lru_w_x, v_lru_b_x, v_lru_lambda, v_w_proj_a, v_w_proj_b, v_w_proj_c, v_w_out, v_post_norm_g):
    wts = dict(zip(WEIGHTS, (pre_norm_g, w_in, gm_ln_g, gm_ln_b, gm_ws, gm_bs, mla_q_norm_g, mla_w_uq, mla_kv_norm_g,
                             mla_w_ukv, lru_conv_w, lru_conv_b, lru_w_a, lru_b_a, lru_w_x, lru_b_x, lru_lambda,
                             w_proj_a, w_proj_b, w_proj_c, w_out, post_norm_g)))
    ms = dict(zip(WEIGHTS, (m_pre_norm_g, m_w_in, m_gm_ln_g, m_gm_ln_b, m_gm_ws, m_gm_bs, m_mla_q_norm_g, m_mla_w_uq,
                            m_mla_kv_norm_g, m_mla_w_ukv, m_lru_conv_w, m_lru_conv_b, m_lru_w_a, m_lru_b_a, m_lru_w_x,
                            m_lru_b_x, m_lru_lambda, m_w_proj_a, m_w_proj_b, m_w_proj_c, m_w_out, m_post_norm_g)))
    vs = dict(zip(WEIGHTS, (v_pre_norm_g, v_w_in, v_gm_ln_g, v_gm_ln_b, v_gm_ws, v_gm_bs, v_mla_q_norm_g, v_mla_w_uq,
                            v_mla_kv_norm_g, v_mla_w_ukv, v_lru_conv_w, v_lru_conv_b, v_lru_w_a, v_lru_b_a, v_lru_w_x,
                            v_lru_b_x, v_lru_lambda, v_w_proj_a, v_w_proj_b, v_w_proj_c, v_w_out, v_post_norm_g)))
    return _step(x[0], loss_target[0], wts, ms, vs)
```

```python
import functools
import math

import jax
import jax.numpy as jnp
from jax import lax
from jax.experimental import pallas as pl
from jax.experimental.pallas import tpu as pltpu

F32 = jnp.float32
BF16 = jnp.bfloat16

T = 2048
D = 1024
L = 2
NDEV = 8
EPS = 1e-6
CHUNK_SHIFT = 6
HEADS = 8
QK = 192
LRU_W = 1280
LRU_TILE = 640
N_IN = 10432
SHARD = N_IN // NDEV
OFF_U, OFF_V, OFF_ZA, OFF_CQ, OFF_CKV, OFF_ZB = 0, 1024, 2048, 3072, 3456, 3840
OFF_XC, OFF_ZC, OFF_GA, OFF_GB, OFF_GC = 5120, 6400, 7680, 8704, 9728
NPAD = 10752
PAD1_AT, PAD1 = 3776, 64
PAD2_AT, PAD2 = 4800, 256
WIN_PIECES = ((0, 888), (888, 280), (1168, 136))
VMEM_LIMIT = 60 * 1024 * 1024

ADAM_LR, ADAM_B1, ADAM_B2, ADAM_EPS, ADAM_WD, ADAM_STEP = 0.001, 0.9, 0.999, 1e-08, 0.01, 10

_NN = (((1,), (0,)), ((), ()))
_NT = (((1,), (1,)), ((), ()))
_TN = (((0,), (0,)), ((), ()))


def _dg(a, b, dims):
    return lax.dot_general(a.astype(BF16), b.astype(BF16), dims, preferred_element_type=F32)


@jax.custom_vjp
def dot_nn(a, b):
    return _dg(a, b, _NN)


def _nn_fwd(a, b):
    return _dg(a, b, _NN), (a, b)


def _nn_bwd(res, g):
    a, b = res
    return _dg(g, b, _NT).astype(a.dtype), _dg(a, g, _TN).astype(b.dtype)


dot_nn.defvjp(_nn_fwd, _nn_bwd)


@jax.custom_vjp
def dot_nt(a, b):
    return _dg(a, b, _NT)


def _nt_fwd(a, b):
    return _dg(a, b, _NT), (a, b)


def _nt_bwd(res, g):
    a, b = res
    return _dg(g, b, _NN).astype(a.dtype), _dg(g, a, _TN).astype(b.dtype)


dot_nt.defvjp(_nt_fwd, _nt_bwd)


def _params(sem=None):
    return pltpu.CompilerParams(dimension_semantics=sem, vmem_limit_bytes=VMEM_LIMIT)


def _sigmoid(x):
    return 1.0 / (1.0 + jnp.exp(-x))


def _silu(x):
    return x * _sigmoid(x)


def _rms(x, g):
    ms = jnp.mean(x * x, axis=-1, keepdims=True)
    return x * lax.rsqrt(ms + EPS) * g


def _acc(ref, val, first):
    @pl.when(first)
    def _():
        ref[...] = val

    @pl.when(jnp.logical_not(first))
    def _():
        ref[...] += val


ANY = pl.BlockSpec(memory_space=pl.ANY)


INPROJ_TN = 768


def inproj_fwd(x, g, wt, l, dep=None):
    tn = INPROJ_TN

    def body(x_ref, g_ref, w_ref, *rest):
        proj_ref, h_ref = rest[-2:]

        @pl.when(pl.program_id(0) == 0)
        def _():
            h_ref[...] = _rms(x_ref[...], g_ref[...]).astype(BF16)

        proj_ref[...] = lax.dot_general(h_ref[...], _unpack(w_ref[...]).astype(BF16), _NT, preferred_element_type=F32)

    deps = [] if dep is None else [dep]
    return pl.pallas_call(
        body, grid=(NPAD // tn,),
        in_specs=[pl.BlockSpec((T, D), lambda j: (0, 0)), pl.BlockSpec((1, D), lambda j: (0, 0)),
                  pl.BlockSpec((None, tn, D // 2), lambda j: (0, j, 0))] + [ANY] * len(deps),
        out_specs=[pl.BlockSpec((T, tn), lambda j: (0, j)), pl.BlockSpec((T, D), lambda j: (0, 0))],
        out_shape=[jax.ShapeDtypeStruct((T, NPAD), F32), jax.ShapeDtypeStruct((T, D), BF16)],
        name=f"inproj_fwd_l{l}", compiler_params=_params(("arbitrary",)))(x, g, wt, *deps)


def inproj_bwd(dproj, h, wt, l, dep=None):
    tn = INPROJ_TN
    deps = [] if dep is None else [dep]

    def body(dp_ref, h_ref, w_ref, *rest):
        dwt_ref, dh_ref = rest[-2:]
        dp = dp_ref[...]
        dwt_ref[...] = lax.dot_general(dp, h_ref[...], _TN, preferred_element_type=F32)
        contrib = lax.dot_general(dp, _unpack(w_ref[...]).astype(BF16), _NN, preferred_element_type=F32)
        _acc(dh_ref, contrib, pl.program_id(0) == 0)

    return pl.pallas_call(
        body, grid=(NPAD // tn,),
        in_specs=[pl.BlockSpec((T, tn), lambda j: (0, j)), pl.BlockSpec((T, D), lambda j: (0, 0)),
                  pl.BlockSpec((None, tn, D // 2), lambda j: (0, j, 0))] + [ANY] * len(deps),
        out_specs=[pl.BlockSpec((None, tn, D), lambda j: (0, j, 0)), pl.BlockSpec((T, D), lambda j: (0, 0))],
        out_shape=[jax.ShapeDtypeStruct((1, NPAD, D), F32), jax.ShapeDtypeStruct((T, D), F32)],
        name=f"inproj_bwd_l{l}", compiler_params=_params(("arbitrary",)))(dproj, h, wt, *deps)


def prenorm_bwd(x, g, dh, dxn, l, dep=None):
    tm = 512
    deps = [] if dep is None else [dep]

    def body(x_ref, g_ref, dh_ref, dxn_ref, *rest):
        dx_ref, dg_ref = rest[-2:]
        _, vjp = jax.vjp(_rms, x_ref[...], g_ref[...])
        dx, dg = vjp(dh_ref[...])
        dx_ref[...] = dx + dxn_ref[...]
        _acc(dg_ref, dg, pl.program_id(0) == 0)

    tok = pl.BlockSpec((tm, D), lambda i: (i, 0))
    vec = pl.BlockSpec((1, D), lambda i: (0, 0))
    return pl.pallas_call(
        body, grid=(T // tm,), in_specs=[tok, vec, tok, tok] + [ANY] * len(deps), out_specs=[tok, vec],
        out_shape=[jax.ShapeDtypeStruct((T, D), F32), jax.ShapeDtypeStruct((1, D), F32)],
        name=f"prenorm_bwd_l{l}", compiler_params=_params(("arbitrary",)))(x, g, dh, dxn, *deps)


def _gmlp_tile(u, v, z, ln_g, ln_b, ws, bs):
    mu = jnp.mean(v, axis=-1, keepdims=True)
    vc = v - mu
    var = jnp.mean(vc * vc, axis=-1, keepdims=True)
    vn = vc * lax.rsqrt(var + EPS) * ln_g + ln_b
    qi = lax.broadcasted_iota(jnp.int32, (128, 128), 0) >> CHUNK_SHIFT
    kj = lax.broadcasted_iota(jnp.int32, (128, 128), 1) >> CHUNK_SHIFT
    mask = kj <= qi
    outs = []
    for g in range(4):
        wm = jnp.where(mask, ws[g], 0.0)
        outs.append(dot_nn(wm, vn[:, 256 * g:256 * (g + 1)]) + bs[g])
    sv = jnp.concatenate(outs, axis=1)
    return u * sv * _silu(z)


GMLP_ROWS = 512


def _gmlp_specs():
    blk = lambda c: pl.BlockSpec((GMLP_ROWS, 1024), lambda n, c=c: (n, c))
    vec = pl.BlockSpec((1, 1024), lambda n: (0, 0))
    return [blk(0), blk(1), blk(2), vec, vec,
            pl.BlockSpec((4, 128, 128), lambda n: (0, 0, 0)), pl.BlockSpec((4, 128, 1), lambda n: (0, 0, 0))]


def gmlp_fwd(proj, ln_g, ln_b, ws, bs, l):
    def body(u_ref, v_ref, z_ref, g_ref, b_ref, ws_ref, bs_ref, y_ref):
        for r in range(0, GMLP_ROWS, 128):
            rows = slice(r, r + 128)
            y_ref[rows, :] = _gmlp_tile(u_ref[rows, :], v_ref[rows, :], z_ref[rows, :], g_ref[...], b_ref[...],
                                        [ws_ref[g] for g in range(4)], [bs_ref[g] for g in range(4)])

    return pl.pallas_call(
        body, grid=(T // GMLP_ROWS,), in_specs=_gmlp_specs(),
        out_specs=pl.BlockSpec((GMLP_ROWS, 1024), lambda n: (n, 0)),
        out_shape=jax.ShapeDtypeStruct((T, 1024), F32),
        name=f"gmlp_fwd_l{l}", compiler_params=_params(("arbitrary",)))(proj, proj, proj, ln_g, ln_b, ws, bs)


def gmlp_bwd(proj, ln_g, ln_b, ws, bs, dy, dproj, l):
    def body(u_ref, v_ref, z_ref, g_ref, b_ref, ws_ref, bs_ref, dy_ref, _, dseg_ref, dg_ref, db_ref, dws_ref, dbs_ref):
        for r in range(0, GMLP_ROWS, 128):
            rows = slice(r, r + 128)
            first = jnp.logical_and(pl.program_id(0) == 0, r == 0)
            _, vjp = jax.vjp(_gmlp_tile, u_ref[rows, :], v_ref[rows, :], z_ref[rows, :], g_ref[...], b_ref[...],
                             [ws_ref[g] for g in range(4)], [bs_ref[g] for g in range(4)])
            du, dv, dz, dg, db, dws, dbs = vjp(dy_ref[rows, :])
            dseg_ref[rows, 0:1024] = du.astype(BF16)
            dseg_ref[rows, 1024:2048] = dv.astype(BF16)
            dseg_ref[rows, 2048:3072] = dz.astype(BF16)
            _acc(dg_ref, dg, first)
            _acc(db_ref, db, first)
            for g in range(4):
                _acc(dws_ref.at[g], dws[g], first)
                _acc(dbs_ref.at[g], dbs[g], first)

    vec = pl.BlockSpec((1, 1024), lambda n: (0, 0))
    return pl.pallas_call(
        body, grid=(T // GMLP_ROWS,),
        in_specs=_gmlp_specs() + [pl.BlockSpec((GMLP_ROWS, 1024), lambda n: (n, 0)), ANY],
        out_specs=[pl.BlockSpec((GMLP_ROWS, 3072), lambda n: (n, OFF_U // 3072)), vec, vec,
                   pl.BlockSpec((4, 128, 128), lambda n: (0, 0, 0)), pl.BlockSpec((4, 128, 1), lambda n: (0, 0, 0))],
        out_shape=[jax.ShapeDtypeStruct((T, NPAD), BF16), jax.ShapeDtypeStruct((1, 1024), F32),
                   jax.ShapeDtypeStruct((1, 1024), F32), jax.ShapeDtypeStruct((4, 128, 128), F32),
                   jax.ShapeDtypeStruct((4, 128, 1), F32)],
        input_output_aliases={8: 0},
        name=f"gmlp_bwd_l{l}", compiler_params=_params(("arbitrary",)))(proj, proj, proj, ln_g, ln_b, ws, bs, dy, dproj)


QKV_TM = 512


def _qkv_tile(cq, ckvr, qg, kvg, wq, wkv, ctab, stab):
    tm = cq.shape[0]
    cqn = _rms(cq, qg)
    lane = lax.broadcasted_iota(jnp.int32, ckvr.shape, 1)
    iskv = lane < 256
    ms = jnp.sum(jnp.where(iskv, ckvr * ckvr, 0.0), axis=-1, keepdims=True) * (1.0 / 256)
    lm = jnp.where(iskv, ckvr * lax.rsqrt(ms + EPS) * kvg, ckvr)
    r = lax.broadcasted_iota(jnp.int32, (64, 128), 0)
    c = lax.broadcasted_iota(jnp.int32, (64, 128), 1)
    eye = jnp.where(c == r, 1.0, 0.0)
    eye_sw = jnp.where(c == ((r + 32) & 63), 1.0, 0.0)
    z64 = jnp.zeros((64, 256), F32)
    z128 = jnp.zeros((128, 128), F32)
    rk_rope = jnp.concatenate([z64, eye], axis=1)
    rk_sw = jnp.concatenate([jnp.zeros((128, 384), F32), jnp.concatenate([z64, eye_sw], axis=1)], axis=0)
    k_sw = dot_nt(lm, rk_sw) * stab
    qs, ks, vs = [], [], []
    for h in range(HEADS):
        wn, w1, w2 = wq[h]
        wk, wv = wkv[h]
        wq_h = jnp.concatenate([wn, w1, w2], axis=0)
        wq_sw = jnp.concatenate([jnp.zeros((128, 384), F32), w2, w1], axis=0)
        qs.append(dot_nt(cqn, wq_h) * ctab + dot_nt(cqn, wq_sw) * stab)
        rk_h = jnp.concatenate([jnp.concatenate([wk, z128], axis=1), rk_rope], axis=0)
        ks.append(dot_nt(lm, rk_h) * ctab + k_sw)
        vs.append(dot_nt(lm, jnp.concatenate([wv, z128], axis=1)))
    return qs, ks, vs


def _qkv_in_specs():
    tm = QKV_TM
    return [pl.BlockSpec((tm, 384), lambda i: (i, OFF_CQ // 384)), pl.BlockSpec((tm, 384), lambda i: (i, OFF_CKV // 384)),
            pl.BlockSpec((1, 384), lambda i: (0, 0)), pl.BlockSpec((1, 384), lambda i: (0, 0)),
            pl.BlockSpec((HEADS, 192, 384), lambda i: (0, 0, 0)), pl.BlockSpec((HEADS, 256, 128), lambda i: (0, 0, 0)),
            pl.BlockSpec((tm, 192), lambda i: (i, 0)), pl.BlockSpec((tm, 192), lambda i: (i, 0))]


def _qkv_weights(wq_ref, wkv_ref):
    wq = [(wq_ref[h, 0:128, :], wq_ref[h, 128:160, :], wq_ref[h, 160:192, :]) for h in range(HEADS)]
    wkv = [(_unpack(wkv_ref[h, 0:128, :]), _unpack(wkv_ref[h, 128:256, :])) for h in range(HEADS)]
    return wq, wkv


def qkv_fwd(proj, qg, kvg, wq, wkv, ctab, stab, l, dep=None):
    tm = QKV_TM
    deps = [] if dep is None else [dep]

    def body(cq_ref, ckvr_ref, qg_ref, kvg_ref, wq_ref, wkv_ref, c_ref, s_ref, *rest):
        q_ref, k_ref, v_ref = rest[-3:]
        wq_l, wkv_l = _qkv_weights(wq_ref, wkv_ref)
        qs, ks, vs = _qkv_tile(cq_ref[...], ckvr_ref[...], qg_ref[...], kvg_ref[...], wq_l, wkv_l, c_ref[...], s_ref[...])
        for h in range(HEADS):
            q_ref[h] = qs[h]
            k_ref[h] = ks[h]
            v_ref[h] = vs[h]

    return pl.pallas_call(
        body, grid=(T // tm,), in_specs=_qkv_in_specs() + [ANY] * len(deps),
        out_specs=[pl.BlockSpec((HEADS, tm, QK), lambda i: (0, i, 0)), pl.BlockSpec((HEADS, tm, QK), lambda i: (0, i, 0)),
                   pl.BlockSpec((HEADS, tm, 128), lambda i: (0, i, 0))],
        out_shape=[jax.ShapeDtypeStruct((HEADS, T, QK), F32), jax.ShapeDtypeStruct((HEADS, T, QK), F32),
                   jax.ShapeDtypeStruct((HEADS, T, 128), F32)],
        name=f"qkv_fwd_l{l}", compiler_params=_params(("arbitrary",)))(proj, proj, qg, kvg, wq, wkv, ctab, stab, *deps)


def qkv_bwd(proj, qg, kvg, wq, wkv, ctab, stab, dq, dk, dv, dproj, l):
    tm = QKV_TM

    def body(cq_ref, ckvr_ref, qg_ref, kvg_ref, wq_ref, wkv_ref, c_ref, s_ref, dq_ref, dk_ref, dv_ref, _,
             dseg_ref, dqg_ref, dkvg_ref, dwq_ref, dwkv_ref):
        first = pl.program_id(0) == 0
        wq_l, wkv_l = _qkv_weights(wq_ref, wkv_ref)
        c_tab, s_tab = c_ref[...], s_ref[...]
        fn = lambda cq, ckvr, qg_, kvg_, wq_, wkv_: _qkv_tile(cq, ckvr, qg_, kvg_, wq_, wkv_, c_tab, s_tab)
        _, vjp = jax.vjp(fn, cq_ref[...], ckvr_ref[...], qg_ref[...], kvg_ref[...], wq_l, wkv_l)
        cts = ([dq_ref[h] for h in range(HEADS)], [dk_ref[h] for h in range(HEADS)], [dv_ref[h] for h in range(HEADS)])
        dcq, dckvr, dqg, dkvg, dwq, dwkv = vjp(cts)
        dseg_ref[:, 0:384] = dcq.astype(BF16)
        dseg_ref[:, 384:768] = dckvr.astype(BF16)
        _acc(dqg_ref, dqg, first)
        _acc(dkvg_ref, dkvg, first)
        for h in range(HEADS):
            _acc(dwq_ref.at[h, 0:128, :], dwq[h][0], first)
            _acc(dwq_ref.at[h, 128:160, :], dwq[h][1], first)
            _acc(dwq_ref.at[h, 160:192, :], dwq[h][2], first)
            _acc(dwkv_ref.at[h, 0:128, :], dwkv[h][0], first)
            _acc(dwkv_ref.at[h, 128:256, :], dwkv[h][1], first)

    hq = pl.BlockSpec((HEADS, tm, QK), lambda i: (0, i, 0))
    return pl.pallas_call(
        body, grid=(T // tm,),
        in_specs=_qkv_in_specs() + [hq, hq, pl.BlockSpec((HEADS, tm, 128), lambda i: (0, i, 0)), ANY],
        out_specs=[pl.BlockSpec((tm, 768), lambda i: (i, OFF_CQ // 768)), pl.BlockSpec((1, 384), lambda i: (0, 0)),
                   pl.BlockSpec((1, 384), lambda i: (0, 0)), pl.BlockSpec((HEADS, 192, 384), lambda i: (0, 0, 0)),
                   pl.BlockSpec((HEADS, 256, 256), lambda i: (0, 0, 0))],
        out_shape=[jax.ShapeDtypeStruct((T, NPAD), BF16), jax.ShapeDtypeStruct((1, 384), F32),
                   jax.ShapeDtypeStruct((1, 384), F32), jax.ShapeDtypeStruct((HEADS, 192, 384), F32),
                   jax.ShapeDtypeStruct((HEADS, 256, 256), F32)],
        input_output_aliases={11: 0},
        name=f"qkv_bwd_l{l}", compiler_params=_params(("arbitrary",)))(
            proj, proj, qg, kvg, wq, wkv, ctab, stab, dq, dk, dv, dproj)


ATT_TQ_FWD = 256
ATT_TQ_BWD = 512


def _attn_tile(q, kv_past, k, v, zb):
    q = q * (1.0 / math.sqrt(QK))
    s = dot_nt(q, k)
    qc = lax.broadcasted_iota(jnp.int32, s.shape, 0) >> CHUNK_SHIFT
    kc = lax.broadcasted_iota(jnp.int32, s.shape, 1) >> CHUNK_SHIFT
    s = jnp.where(kc <= qc, s, -1e30)
    m = jnp.max(s, axis=-1, keepdims=True)
    if kv_past is not None:
        sp = dot_nt(q, kv_past[0])
        m = jnp.maximum(m, jnp.max(sp, axis=-1, keepdims=True))
    m = lax.stop_gradient(m)
    p = jnp.exp(s - m)
    denom = jnp.sum(p, axis=-1, keepdims=True)
    o = dot_nn(p, v)
    if kv_past is not None:
        pp = jnp.exp(sp - m)
        denom = denom + jnp.sum(pp, axis=-1, keepdims=True)
        o = o + dot_nn(pp, kv_past[1])
    return o * (1.0 / denom) * _silu(zb)


def _attn_operands(k_ref, v_ref, g, tq):
    n = tq * g
    past = (k_ref[0:n, :], v_ref[0:n, :]) if g else None
    return past, k_ref[n:n + tq, :], v_ref[n:n + tq, :]


def _attn_in_specs(tq):
    return [pl.BlockSpec((None, tq, QK), lambda h, i: (h, i, 0)), pl.BlockSpec((None, T, QK), lambda h, i: (h, 0, 0)),
            pl.BlockSpec((None, T, 128), lambda h, i: (h, 0, 0)),
            pl.BlockSpec((tq, 128), lambda h, i: (i, OFF_ZB // 128 + h))]


def attn_fwd(q, k, v, proj, l):
    tq = ATT_TQ_FWD

    def body(q_ref, k_ref, v_ref, z_ref, y_ref):
        for g in range(T // tq):
            @pl.when(pl.program_id(1) == g)
            def _(g=g):
                past, k, v = _attn_operands(k_ref, v_ref, g, tq)
                y_ref[...] = _attn_tile(q_ref[...], past, k, v, z_ref[...])

    return pl.pallas_call(
        body, grid=(HEADS, T // tq), in_specs=_attn_in_specs(tq),
        out_specs=pl.BlockSpec((tq, 128), lambda h, i: (i, h)),
        out_shape=jax.ShapeDtypeStruct((T, 1024), F32),
        name=f"attn_fwd_l{l}", compiler_params=_params(("arbitrary", "arbitrary")))(q, k, v, proj)


def attn_bwd(q, k, v, proj, dy, dproj, l):
    tq = ATT_TQ_BWD

    def body(q_ref, k_ref, v_ref, z_ref, dy_ref, _, dq_ref, dk_ref, dv_ref, dz_ref):
        @pl.when(pl.program_id(1) == 0)
        def _():
            dk_ref[...] = jnp.zeros_like(dk_ref)
            dv_ref[...] = jnp.zeros_like(dv_ref)

        for g in range(T // tq):
            @pl.when(pl.program_id(1) == g)
            def _(g=g):
                n = tq * g
                past, k, v = _attn_operands(k_ref, v_ref, g, tq)
                _, vjp = jax.vjp(_attn_tile, q_ref[...], past, k, v, z_ref[...])
                dq, dpast, dk, dv, dz = vjp(dy_ref[...])
                dq_ref[...] = dq
                dz_ref[...] = dz.astype(BF16)
                dk_ref[n:n + tq, :] += dk
                dv_ref[n:n + tq, :] += dv
                if g:
                    dk_ref[0:n, :] += dpast[0]
                    dv_ref[0:n, :] += dpast[1]

    return pl.pallas_call(
        body, grid=(HEADS, T // tq),
        in_specs=_attn_in_specs(tq) + [pl.BlockSpec((tq, 128), lambda h, i: (i, h)), ANY],
        out_specs=[pl.BlockSpec((None, tq, QK), lambda h, i: (h, i, 0)), pl.BlockSpec((None, T, QK), lambda h, i: (h, 0, 0)),
                   pl.BlockSpec((None, T, 128), lambda h, i: (h, 0, 0)),
                   pl.BlockSpec((tq, 128), lambda h, i: (i, OFF_ZB // 128 + h))],
        out_shape=[jax.ShapeDtypeStruct((HEADS, T, QK), F32), jax.ShapeDtypeStruct((HEADS, T, QK), F32),
                   jax.ShapeDtypeStruct((HEADS, T, 128), F32), jax.ShapeDtypeStruct((T, NPAD), BF16)],
        input_output_aliases={5: 3},
        name=f"attn_bwd_l{l}", compiler_params=_params(("arbitrary", "arbitrary")))(q, k, v, proj, dy, dproj)


LRU_TT = 512


def _lru_gates(xc, wa, wx, ba, bx, lam):
    r = _sigmoid(dot_nn(xc, wa) + ba)
    i = _sigmoid(dot_nn(xc, wx) + bx)
    sp = jnp.maximum(-lam, 0.0) + jnp.log1p(jnp.exp(-jnp.abs(lam)))
    log_a = -8.0 * r * sp
    a = jnp.exp(log_a)
    mult = jnp.sqrt(jnp.maximum(1.0 - jnp.exp(2.0 * log_a), 0.0))
    return a, mult * (i * xc)


def _shift_down(x, s, halo):
    n, c = x.shape
    r = pltpu.roll(x.reshape(n // 8, 8, c), s, 1)
    before = jnp.concatenate([pltpu.roll(halo, s, 0)[None], r[:-1]], axis=0)
    sub = lax.broadcasted_iota(jnp.int32, r.shape, 1)
    return jnp.where(sub >= s, r, before).reshape(n, c)


def _shift_up(x, s, halo):
    n, c = x.shape
    r = pltpu.roll(x.reshape(n // 8, 8, c), 8 - s, 1)
    after = jnp.concatenate([r[1:], pltpu.roll(halo, 8 - s, 0)[None]], axis=0)
    sub = lax.broadcasted_iota(jnp.int32, r.shape, 1)
    return jnp.where(sub < 8 - s, r, after).reshape(n, c)


def _conv(x, halo, w_ref, b):
    return (w_ref[3:4, :] * x + w_ref[2:3, :] * _shift_down(x, 1, halo) + w_ref[1:2, :] * _shift_down(x, 2, halo)
            + w_ref[0:1, :] * _shift_down(x, 3, halo) + b)


def _scan(a, b, reverse, carry):
    n, c = a.shape
    a, b = a.reshape(n // 8, 8, c), b.reshape(n // 8, 8, c)
    sub = lax.broadcasted_iota(jnp.int32, a.shape, 1)
    for d in (1, 2, 4):
        keep = sub < 8 - d if reverse else sub >= d
        shift = 8 - d if reverse else d
        a_sh = jnp.where(keep, pltpu.roll(a, shift, 1), 1.0)
        b_sh = jnp.where(keep, pltpu.roll(b, shift, 1), 0.0)
        b = a * b_sh + b
        a = a * a_sh
    a, b = a.reshape(n, c), b.reshape(n, c)
    groups = [None] * (n // 8)
    for g in (reversed(range(n // 8)) if reverse else range(n // 8)):
        h = a[8 * g:8 * g + 8] * carry + b[8 * g:8 * g + 8]
        groups[g] = h
        carry = h[0:1] if reverse else h[7:8]
    return jnp.concatenate(groups, axis=0), carry


def _lru_param_specs(l):
    ct = LRU_TILE
    vec = pl.BlockSpec((1, ct), lambda n, i: (0, n))
    mat = pl.BlockSpec((None, 8, 80, 80), lambda n, i: (l, n, 0, 0))
    return [pl.BlockSpec((4, ct), lambda n, i: (0, n)), vec, mat, mat, vec, vec, vec]


def _blocks_to_dense(w_ref, dense):
    dense[...] = jnp.zeros_like(dense)
    for b in range(8):
        dense[80 * b:80 * b + 80, 80 * b:80 * b + 80] = w_ref[b]


def _dense_to_blocks(dense, w_ref):
    for b in range(8):
        w_ref[b] = dense[80 * b:80 * b + 80, 80 * b:80 * b + 80]


def lru_fwd(proj, conv_w, conv_b, wa, wx, ba, bx, lam, l):
    tt, ct = LRU_TT, LRU_TILE

    def body(x_ref, z_ref, cw_ref, cb_ref, wa_ref, wx_ref, ba_ref, bx_ref, lam_ref, h_ref, y_ref, halo, hcar, wa, wx):
        @pl.when(pl.program_id(1) == 0)
        def _():
            halo[...] = jnp.zeros_like(halo)
            hcar[...] = jnp.zeros_like(hcar)
            _blocks_to_dense(wa_ref, wa)
            _blocks_to_dense(wx_ref, wx)

        x = x_ref[...]
        xc = _conv(x, halo[...], cw_ref, cb_ref[...])
        halo[...] = x[tt - 8:tt]
        a, b = _lru_gates(xc, wa[...], wx[...], ba_ref[...], bx_ref[...], lam_ref[...])
        h, hcar[...] = _scan(a, b, False, hcar[...])
        h_ref[...] = h
        y_ref[...] = h * _silu(z_ref[...])

    seq = pl.BlockSpec((tt, ct), lambda n, i: (i, n))
    return pl.pallas_call(
        body, grid=(LRU_W // ct, T // tt),
        in_specs=[pl.BlockSpec((tt, ct), lambda n, i: (i, OFF_XC // ct + n)),
                  pl.BlockSpec((tt, ct), lambda n, i: (i, OFF_ZC // ct + n))] + _lru_param_specs(l),
        out_specs=[seq, seq],
        out_shape=[jax.ShapeDtypeStruct((T, LRU_W), F32), jax.ShapeDtypeStruct((T, LRU_W), F32)],
        scratch_shapes=[pltpu.VMEM((8, ct), F32), pltpu.VMEM((1, ct), F32), pltpu.VMEM((ct, ct), F32),
                        pltpu.VMEM((ct, ct), F32)],
        name=f"lru_fwd_l{l}", compiler_params=_params(("arbitrary", "arbitrary")))(
            proj, proj, conv_w, conv_b, wa, wx, ba, bx, lam)


def lru_bwd(proj, hseq, dy, conv_w, conv_b, wa, wx, ba, bx, lam, dproj, l):
    tt, ct = LRU_TT, LRU_TILE
    nt = T // tt
    rev = lambda i: nt - 1 - i
    prev8 = lambda i: jnp.maximum(rev(i) * (tt // 8) - 1, 0)

    def body(x_ref, xh_ref, z_ref, h_ref, hh_ref, dy_ref, cw_ref, cb_ref, wa_ref, wx_ref, ba_ref, bx_ref, lam_ref, _,
             dx_ref, dcw_ref, dcb_ref, dwa_ref, dwx_ref, dba_ref, dbx_ref, dlam_ref, gcar, dhalo,
             wa, wx, dwa_acc, dwx_acc):
        i = pl.program_id(1)
        first = i == 0

        @pl.when(first)
        def _():
            gcar[...] = jnp.zeros_like(gcar)
            dhalo[...] = jnp.zeros_like(dhalo)
            _blocks_to_dense(wa_ref, wa)
            _blocks_to_dense(wx_ref, wx)

        at_start = rev(i) == 0
        x = x_ref[...]
        xhalo = jnp.where(at_start, 0.0, xh_ref[...])
        sh = [x, _shift_down(x, 1, xhalo), _shift_down(x, 2, xhalo), _shift_down(x, 3, xhalo)]
        xc = (cw_ref[3:4, :] * sh[0] + cw_ref[2:3, :] * sh[1] + cw_ref[1:2, :] * sh[2] + cw_ref[0:1, :] * sh[3]
              + cb_ref[...])
        (a, b), vjp = jax.vjp(_lru_gates, xc, wa[...], wx[...], ba_ref[...], bx_ref[...], lam_ref[...])
        hs = h_ref[...]
        hprev = _shift_down(hs, 1, jnp.where(at_start, 0.0, hh_ref[...]))
        dh = dy_ref[...] * _silu(z_ref[...])
        a_next = _shift_up(a, 1, jnp.ones((8, ct), F32))
        g, _ = _scan(a_next, dh, True, gcar[...])
        dxc, dwa, dwx, dba, dbx, dlam = vjp((g * hprev, g))
        dx = (cw_ref[3:4, :] * dxc + cw_ref[2:3, :] * _shift_up(dxc, 1, dhalo[...])
              + cw_ref[1:2, :] * _shift_up(dxc, 2, dhalo[...]) + cw_ref[0:1, :] * _shift_up(dxc, 3, dhalo[...]))
        dx_ref[...] = dx.astype(BF16)
        dhalo[...] = dxc[0:8]
        ag = a * g
        gcar[...] = ag[0:1]
        dcw = jnp.concatenate([jnp.sum(dxc * sh[3 - j], axis=0, keepdims=True) for j in range(4)], axis=0)
        _acc(dcw_ref, dcw, first)
        _acc(dcb_ref, jnp.sum(dxc, axis=0, keepdims=True), first)
        _acc(dwa_acc, dwa, first)
        _acc(dwx_acc, dwx, first)

        @pl.when(i == nt - 1)
        def _():
            _dense_to_blocks(dwa_acc, dwa_ref)
            _dense_to_blocks(dwx_acc, dwx_ref)

        _acc(dba_ref, dba, first)
        _acc(dbx_ref, dbx, first)
        _acc(dlam_ref, dlam, first)

    xcol = OFF_XC // ct
    zcol = OFF_ZC // ct
    vec = pl.BlockSpec((1, ct), lambda n, i: (0, n))
    mat = pl.BlockSpec((8, 80, 80), lambda n, i: (n, 0, 0))
    seq = pl.BlockSpec((tt, ct), lambda n, i: (rev(i), n))
    return pl.pallas_call(
        body, grid=(LRU_W // ct, nt),
        in_specs=[pl.BlockSpec((tt, ct), lambda n, i: (rev(i), xcol + n)),
                  pl.BlockSpec((8, ct), lambda n, i: (prev8(i), xcol + n)),
                  pl.BlockSpec((tt, ct), lambda n, i: (rev(i), zcol + n)),
                  seq, pl.BlockSpec((8, ct), lambda n, i: (prev8(i), n)), seq] + _lru_param_specs(l) + [ANY],
        out_specs=[pl.BlockSpec((tt, ct), lambda n, i: (rev(i), xcol + n)),
                   pl.BlockSpec((4, ct), lambda n, i: (0, n)), vec, mat, mat, vec, vec, vec],
        out_shape=[jax.ShapeDtypeStruct((T, NPAD), BF16),
                   jax.ShapeDtypeStruct((4, LRU_W), F32), jax.ShapeDtypeStruct((1, LRU_W), F32),
                   jax.ShapeDtypeStruct((16, 80, 80), F32), jax.ShapeDtypeStruct((16, 80, 80), F32),
                   jax.ShapeDtypeStruct((1, LRU_W), F32), jax.ShapeDtypeStruct((1, LRU_W), F32),
                   jax.ShapeDtypeStruct((1, LRU_W), F32)],
        scratch_shapes=[pltpu.VMEM((1, ct), F32), pltpu.VMEM((8, ct), F32)] + [pltpu.VMEM((ct, ct), F32)] * 4,
        input_output_aliases={13: 0},
        name=f"lru_bwd_l{l}", compiler_params=_params(("arbitrary", "arbitrary")))(
            proj, proj, proj, hseq, hseq, dy, conv_w, conv_b, wa, wx, ba, bx, lam, dproj)


def proj_bwd(y, dp, w, l, tag, dep=None, dproj=None, gate=None):
    tm = 512
    k = y.shape[1]
    extra = [] if dep is None else [dep]
    in_specs = [pl.BlockSpec((tm, k), lambda i: (i, 0)), pl.BlockSpec((tm, D), lambda i: (i, 0)),
                pl.BlockSpec((None, k, D // 2), lambda i: (0, 0, 0))]
    out_specs = [pl.BlockSpec((tm, k), lambda i: (i, 0)), pl.BlockSpec((None, k, D), lambda i: (0, 0, 0))]
    out_shape = [jax.ShapeDtypeStruct((T, k), F32), jax.ShapeDtypeStruct((1, k, D), F32)]
    aliases = {}
    if gate is not None:
        in_specs += [pl.BlockSpec((tm, k), lambda i: (i, 0)), pl.BlockSpec((tm, k), lambda i: (i, OFF_ZC // k))]
        extra = list(gate) + extra
    if dproj is not None:
        width = k if gate is not None else PAD2
        at = OFF_ZC if gate is not None else OFF_XC - PAD2
        aliases = {3 + len(extra): 2}
        extra = extra + [dproj]
        out_specs.append(pl.BlockSpec((tm, width), lambda i: (i, at // width)))
        out_shape.append(jax.ShapeDtypeStruct((T, NPAD), BF16))
    in_specs += [ANY] * (3 + len(extra) - len(in_specs))

    def body(y_ref, dp_ref, w_ref, *rest):
        dy_ref, dw_ref = rest[len(extra):len(extra) + 2]
        dp = dp_ref[...]
        dy = _dg(dp, _unpack(w_ref[...]), _NT)
        dy_ref[...] = dy
        _acc(dw_ref, _dg(y_ref[...], dp, _TN), pl.program_id(0) == 0)
        if gate is not None:
            z = rest[1][...]
            sg = _sigmoid(z)
            rest[len(extra) + 2][...] = (dy * rest[0][...] * (sg * (1.0 + z * (1.0 - sg)))).astype(BF16)
        elif dproj is not None:
            rest[len(extra) + 2][...] = jnp.zeros((tm, PAD2), BF16)

    return pl.pallas_call(
        body, grid=(T // tm,), in_specs=in_specs, out_specs=out_specs, out_shape=out_shape,
        input_output_aliases=aliases,
        name=f"proj_{tag}_bwd_l{l}", compiler_params=_params(("arbitrary",)))(y, dp, w, *extra)


OUT_TM = 256


def _out_tile(pa, pb, pc, ga, gb, gc, wout, post_g):
    merged = _sigmoid(ga) * pa + _sigmoid(gb) * pb + _sigmoid(gc) * pc
    return _rms(dot_nn(merged, wout), post_g)


def _out_in_specs():
    tm = OUT_TM
    tok = pl.BlockSpec((tm, D), lambda i: (i, 0))
    gate = lambda off: pl.BlockSpec((tm, 512), lambda i, off=off: (i, off // 512))
    return [tok, tok, tok, gate(OFF_GA), gate(OFF_GA + 512), gate(OFF_GB), gate(OFF_GB + 512), gate(OFF_GC),
            gate(OFF_GC + 512), pl.BlockSpec((None, D, D // 2), lambda i: (0, 0, 0)), pl.BlockSpec((1, D), lambda i: (0, 0))]


def _gates(refs):
    return [jnp.concatenate([refs[2 * j][...], refs[2 * j + 1][...]], axis=1) for j in range(3)]


def out_fwd(x, ya, yb, yc, proj, wpa, wpb, wpc, wout, post_g, l):
    tm = OUT_TM

    def body(ya_ref, yb_ref, yc_ref, g0, g1, g2, g3, g4, g5, wo_ref, pg_ref, x_ref, wa_ref, wb_ref, wc_ref,
             o_ref, pa_ref, pb_ref, pc_ref, wa, wb, wc, wo):
        @pl.when(pl.program_id(0) == 0)
        def _():
            for dst, src in ((wa, wa_ref), (wb, wb_ref), (wc, wc_ref), (wo, wo_ref)):
                dst[...] = _unpack(src[...]).astype(BF16)

        pa = _dg(ya_ref[...], wa[...], _NN)
        pb = _dg(yb_ref[...], wb[...], _NN)
        pc = _dg(yc_ref[...], wc[...], _NN)
        ga, gb, gc = _gates([g0, g1, g2, g3, g4, g5])
        o_ref[...] = x_ref[...] + _out_tile(pa, pb, pc, ga, gb, gc, wo[...], pg_ref[...])
        pa_ref[...] = pa.astype(BF16)
        pb_ref[...] = pb.astype(BF16)
        pc_ref[...] = pc.astype(BF16)

    tok = pl.BlockSpec((tm, D), lambda i: (i, 0))
    words = lambda k: pl.BlockSpec((None, k, D // 2), lambda i: (0, 0, 0))
    specs = _out_in_specs()
    specs[2] = pl.BlockSpec((tm, LRU_W), lambda i: (i, 0))
    return pl.pallas_call(
        body, grid=(T // tm,), in_specs=specs + [tok, words(D), words(D), words(LRU_W)], out_specs=[tok] * 4,
        out_shape=[jax.ShapeDtypeStruct((T, D), F32)] + [jax.ShapeDtypeStruct((T, D), BF16)] * 3,
        scratch_shapes=[pltpu.VMEM((D, D), BF16), pltpu.VMEM((D, D), BF16), pltpu.VMEM((LRU_W, D), BF16),
                        pltpu.VMEM((D, D), BF16)],
        name=f"out_fwd_l{l}", compiler_params=_params(("arbitrary",)))(
            ya, yb, yc, proj, proj, proj, proj, proj, proj, wout, post_g, x, wpa, wpb, wpc)


def out_bwd(pa, pb, pc, proj, wout, post_g, dxn, l, dep=None):
    tm = OUT_TM
    nsteps = T // tm

    def body(pa_ref, pb_ref, pc_ref, g0, g1, g2, g3, g4, g5, w_ref, pg_ref, dxn_ref, *rest):
        dpa_ref, dpb_ref, dpc_ref, dproj_ref, dw_ref, dpg_ref, gbuf, sem = rest[-8:]
        i = pl.program_id(0)
        first = i == 0
        slot = i % 2
        ga, gb, gc = _gates([g0, g1, g2, g3, g4, g5])
        _, vjp = jax.vjp(_out_tile, pa_ref[...], pb_ref[...], pc_ref[...], ga, gb, gc, _unpack(w_ref[...]), pg_ref[...])
        dpa, dpb, dpc, dga, dgb, dgc, dw, dpg = vjp(dxn_ref[...])
        dpa_ref[...] = dpa.astype(BF16)
        dpb_ref[...] = dpb.astype(BF16)
        dpc_ref[...] = dpc.astype(BF16)
        _acc(dw_ref, dw, first)
        _acc(dpg_ref, dpg, first)

        def writeback(step, s):
            rows = pl.ds(pl.multiple_of(step * tm, tm), tm)
            return pltpu.make_async_copy(gbuf.at[s], dproj_ref.at[rows, pl.ds(OFF_GA, 3072)], sem.at[s])

        gbuf[slot, :, 0:1024] = dga.astype(BF16)
        gbuf[slot, :, 1024:2048] = dgb.astype(BF16)
        gbuf[slot, :, 2048:3072] = dgc.astype(BF16)
        writeback(i, slot).start()

        @pl.when(i > 0)
        def _():
            writeback(i - 1, 1 - slot).wait()

        @pl.when(i == nsteps - 1)
        def _():
            writeback(i, slot).wait()

    tok = pl.BlockSpec((tm, D), lambda i: (i, 0))
    deps = [] if dep is None else [dep]
    return pl.pallas_call(
        body, grid=(nsteps,), in_specs=_out_in_specs() + [tok] + [ANY] * len(deps),
        out_specs=[tok, tok, tok, ANY, pl.BlockSpec((None, D, D), lambda i: (0, 0, 0)), pl.BlockSpec((1, D), lambda i: (0, 0))],
        out_shape=[jax.ShapeDtypeStruct((T, D), BF16)] * 3 + [jax.ShapeDtypeStruct((T, NPAD), BF16),
                                                            jax.ShapeDtypeStruct((1, D, D), F32), jax.ShapeDtypeStruct((1, D), F32)],
        scratch_shapes=[pltpu.VMEM((2, tm, 3072), BF16), pltpu.SemaphoreType.DMA((2,))],
        name=f"out_bwd_l{l}", compiler_params=_params(("arbitrary",)))(
            pa, pb, pc, proj, proj, proj, proj, proj, proj, wout, post_g, dxn, *deps)


def loss_head(y, target):
    tm = 256

    def body(y_ref, t_ref, loss_ref, dy_ref):
        e = y_ref[...] - t_ref[...]
        dy_ref[...] = e * (1.0 / D)
        val = 0.5 * jnp.sum(jnp.mean(e * e, axis=-1, keepdims=True), axis=0, keepdims=True)
        _acc(loss_ref, jnp.broadcast_to(val, (8, 128)), pl.program_id(0) == 0)

    tok = pl.BlockSpec((tm, D), lambda i: (i, 0))
    total, dy = pl.pallas_call(
        body, grid=(T // tm,), in_specs=[tok, tok],
        out_specs=[pl.BlockSpec((8, 128), lambda i: (0, 0)), tok],
        out_shape=[jax.ShapeDtypeStruct((8, 128), F32), jax.ShapeDtypeStruct((T, D), F32)],
        name="loss_head", compiler_params=_params(("arbitrary",)))(y, target)
    return total[0, 0], dy


def _rope_tables():
    pos = jnp.arange(T, dtype=F32)
    inv_freq = 10000.0 ** (-jnp.arange(0, 64, 2, dtype=F32) / 64)
    ang = pos[:, None] * inv_freq[None, :]
    cos, sin = jnp.cos(ang), jnp.sin(ang)
    ctab = jnp.concatenate([jnp.ones((T, 128), F32), cos, cos], axis=1)
    stab = jnp.concatenate([jnp.zeros((T, 128), F32), -sin, sin], axis=1)
    return ctab, stab


def _layer_fwd(x, l, w, gw, tabs, dep=None, mid=None):
    row = lambda a: a[l][None]
    proj, h = inproj_fwd(x, row(w["pre_norm_g"]), gw["w_in_t"], l, dep)
    ya = gmlp_fwd(proj, row(w["gm_ln_g"]), row(w["gm_ln_b"]), w["gm_ws"][l], w["gm_bs"][l][..., None], l)
    dep2 = None
    if mid is not None:
        gw, dep2 = mid(ya)
    q, k, v = qkv_fwd(proj, row(w["mla_q_norm_g"]), row(w["kv_g384"]), gw["wq"], gw["wkv"], tabs[0], tabs[1], l, dep2)
    yb = attn_fwd(q, k, v, proj, l)
    hseq, yc = lru_fwd(proj, gw["conv"], row(w["lru_conv_b"]), w["lru_w_a"], w["lru_w_x"],
                       row(w["lru_b_a"]), row(w["lru_b_x"]), row(w["lru_lambda"]), l)
    xn, pa, pb, pc = out_fwd(x, ya, yb, yc, proj, gw["w_proj_a"], gw["w_proj_b"], gw["w_proj_c"], gw["w_out"],
                             row(w["post_norm_g"]), l)
    return xn, (x, proj, h, ya, q, k, v, yb, hseq, yc, pa, pb, pc)


def _layer_bwd(dxn, l, w, gw, tabs, saved, dep=None, early=None, mid=None, late=None):
    x, proj, h, ya, q, k, v, yb, hseq, yc, pa, pb, pc = saved
    row = lambda a: a[l][None]
    g, gg = {}, {}
    dpa, dpb, dpc, dproj, gg["w_out"], dpost = out_bwd(pa, pb, pc, proj, gw["w_out"], row(w["post_norm_g"]), dxn, l, dep)
    g["post_norm_g"] = dpost[0]
    dep1 = early(dpa) if early is not None else None
    dya, gg["w_proj_a"], dproj = proj_bwd(ya, dpa, gw["w_proj_a"], l, "a", dep1, dproj)
    dyb, gg["w_proj_b"] = proj_bwd(yb, dpb, gw["w_proj_b"], l, "b")
    dyc, gg["w_proj_c"], dproj = proj_bwd(yc, dpc, gw["w_proj_c"], l, "c", None, dproj, (hseq, proj))
    dproj, dln_g, dln_b, g["gm_ws"], dbs = gmlp_bwd(proj, row(w["gm_ln_g"]), row(w["gm_ln_b"]), w["gm_ws"][l],
                                                   w["gm_bs"][l][..., None], dya, dproj, l)
    g["gm_ln_g"], g["gm_ln_b"], g["gm_bs"] = dln_g[0], dln_b[0], dbs[..., 0]
    dq, dk, dv, dproj = attn_bwd(q, k, v, proj, dyb, dproj, l)
    dproj, dqg, dkvg, dwq, dwkv = qkv_bwd(proj, row(w["mla_q_norm_g"]), row(w["kv_g384"]), gw["wq"], gw["wkv"],
                                          tabs[0], tabs[1], dq, dk, dv, dproj, l)
    gg["wq"], gg["wkv"] = dwq.reshape(1, 1536, 384), dwkv.reshape(1, 2048, 256)
    g["mla_q_norm_g"], g["mla_kv_norm_g"] = dqg[0], dkvg[0, :256]
    dproj, dcw, dcb, dwa, dwx, dba, dbx, dlam = lru_bwd(
        proj, hseq, dyc, gw["conv"], row(w["lru_conv_b"]), w["lru_w_a"], w["lru_w_x"],
        row(w["lru_b_a"]), row(w["lru_b_x"]), row(w["lru_lambda"]), dproj, l)
    gg["conv"] = jnp.pad(dcw.T, ((0, 0), (0, 124)))[None]
    g["lru_conv_b"], g["lru_b_a"], g["lru_b_x"], g["lru_lambda"] = dcb[0], dba[0], dbx[0], dlam[0]
    g["lru_w_a"], g["lru_w_x"] = dwa, dwx
    dep2 = mid(gg, dproj) if mid is not None else None
    gg["w_in_t"], dh = inproj_bwd(dproj, h, gw["w_in_t"], l, dep2)
    dep3 = late(gg["w_in_t"]) if late is not None else None
    dx, dpre = prenorm_bwd(x, row(w["pre_norm_g"]), dh, dxn, l, dep3)
    g["pre_norm_g"] = dpre[0]
    return dx, gg, g


MESH = pl.DeviceIdType.MESH
HBM = pl.BlockSpec(memory_space=pltpu.HBM)
SEM = pl.BlockSpec(memory_space=pltpu.SEMAPHORE)
EFFECT = pltpu.SideEffectType.DATAFLOW_SIDE_EFFECTING
FLIPS = ((1, 0), (0, 1), (1, 1))


def _win_off(k, s):
    g = SHARD * k + s
    return g + jnp.where(g >= PAD1_AT, PAD1, 0) + jnp.where(g >= PAD2_AT, PAD2, 0)


def _plain_off(rows):
    return lambda k, s: rows * k + s


class Spec:
    def __init__(self, rows, cols, full_rows, pieces=None, off=None, layers=1, packed=None):
        self.rows, self.cols, self.full_rows, self.layers = rows, cols, full_rows, layers
        self.pieces = pieces or ((0, rows),)
        self.off = off or _plain_off(rows)
        self.packed = cols % 256 == 0 if packed is None else packed
        self.wcols = cols // 2 if self.packed else cols

    def to_words(self, a):
        return _pack(a) if self.packed else a

    def from_words(self, p):
        return _unpack(p) if self.packed else p


def _pack(a):
    def bits(v):
        u = lax.bitcast_convert_type(v, jnp.uint32)
        return u + jnp.uint32(0x7FFF) + ((u >> 16) & jnp.uint32(1))

    words = [(bits(a[:, g:g + 128]) >> 16) | (bits(a[:, g + 128:g + 256]) & jnp.uint32(0xFFFF0000))
             for g in range(0, a.shape[-1], 256)]
    return lax.bitcast_convert_type(jnp.concatenate(words, axis=-1) if len(words) > 1 else words[0], F32)


def _unpack(p):
    w = lax.bitcast_convert_type(p, jnp.uint32)
    lo = lax.bitcast_convert_type(w << 16, F32)
    hi = lax.bitcast_convert_type(w & jnp.uint32(0xFFFF0000), F32)
    return jnp.concatenate([h[:, g:g + 128] for g in range(0, p.shape[-1], 128) for h in (lo, hi)], axis=-1)


WEIGHT_SPECS = {
    "w_in_t": Spec(SHARD, D, NPAD, WIN_PIECES, _win_off),
    "wq": Spec(192, 384, 1536),
    "wkv": Spec(256, 256, 2048),
    "conv": Spec(160, 128, 1280),
    "w_proj_a": Spec(128, D, 1024),
    "w_proj_b": Spec(128, D, 1024),
    "w_proj_c": Spec(160, D, 1280),
    "w_out": Spec(128, D, 1024),
}
REP_ROWS = 72
REP_SPEC = Spec(REP_ROWS, D, REP_ROWS * NDEV, packed=False)


def _coords():
    return lax.axis_index("x"), lax.axis_index("y"), lax.axis_index("c")


def _rows(ref, start, n):
    if not isinstance(start, int):
        start = pl.multiple_of(start, 8)
    return ref.at[:, pl.ds(start, n), :]


def _col_tile(cols):
    return 256 if cols % 256 == 0 else cols


def _n_pieces(specs):
    return sum(len(sp.pieces) for sp in specs)


def pack_place(shard, sp, layer, tag, dep=None):
    gaps = ((PAD1_AT, PAD1), (PAD2_AT + PAD1, PAD2)) if sp.off is _win_off else ()
    npc = len(sp.pieces)
    deps = [] if dep is None else [dep]

    def body(s_ref, *rest):
        words_ref, full_ref, buf, zbuf, sem = rest[-5:]
        l = 0
        x, y, c = _coords()
        me = 4 * x + 2 * y + c
        words = sp.to_words(s_ref[...])
        words_ref[...] = words
        buf[...] = words
        copies = [pltpu.make_async_copy(buf.at[pl.ds(s, n), :],
                                        full_ref.at[l, pl.ds(pl.multiple_of(sp.off(me, s), 8), n), :], sem.at[i])
                  for i, (s, n) in enumerate(sp.pieces)]
        if gaps:
            zbuf[...] = jnp.zeros_like(zbuf)
            copies += [pltpu.make_async_copy(zbuf.at[pl.ds(0, n), :], full_ref.at[l, pl.ds(at, n), :], sem.at[npc + i])
                       for i, (at, n) in enumerate(gaps)]
        for cp in copies:
            cp.start()
        for cp in copies:
            cp.wait()

    return pl.pallas_call(
        body, grid=(1,), in_specs=[pl.BlockSpec((None, sp.rows, sp.cols), lambda i: (layer, 0, 0))] + [ANY] * len(deps),
        out_specs=[pl.BlockSpec((None, sp.rows, sp.wcols), lambda i: (0, 0, 0)), ANY],
        out_shape=[jax.ShapeDtypeStruct((sp.layers, sp.rows, sp.wcols), F32),
                   jax.ShapeDtypeStruct((sp.layers, sp.full_rows, sp.wcols), F32)],
        scratch_shapes=[pltpu.VMEM((sp.rows, sp.wcols), F32), pltpu.VMEM((PAD2 if gaps else 8, sp.wcols), F32),
                        pltpu.SemaphoreType.DMA((npc + len(gaps),))],
        name=f"pack_place_{tag}", compiler_params=_params(("arbitrary",)))(shard, *deps)


def _gather_copies(srcs, bufs, specs, ssem, rsem, landing):
    x, y, c = _coords()
    me = 4 * x + 2 * y + c
    targets = [(x, y, 1 - c)] + [(x ^ fx, y ^ fy, c) for fx, fy in FLIPS]
    copies = []
    p = 0
    for src, buf, sp in zip(srcs, bufs, specs):
        for s, n in sp.pieces:
            for t, (tx, ty, tc) in enumerate(targets):
                owner = 4 * tx + 2 * ty + tc if landing else me
                copies.append(pltpu.make_async_remote_copy(_rows(src, s, n), _rows(buf, sp.off(owner, s), n),
                                                           ssem.at[4 * p + t], rsem.at[4 * p + t],
                                                           device_id=(tx, ty, tc), device_id_type=MESH))
            p += 1
    return copies


def gather_send(words, fulls, specs, tag):
    ns, npc = len(specs), _n_pieces(specs)

    def body(*refs):
        srcs, bufs, sems = refs[:ns], refs[2 * ns:3 * ns], refs[3 * ns:]
        for cp in _gather_copies(srcs, bufs, specs, *sems, False):
            cp.start()
        for cp in _gather_copies(srcs, bufs, specs, *sems, False):
            cp.wait_send()
        for cp in _gather_copies(srcs, bufs, specs, *sems, True):
            cp.wait_recv()

    return pl.pallas_call(
        body, in_specs=[ANY] * (2 * ns), out_specs=[ANY] * ns,
        out_shape=[jax.ShapeDtypeStruct(f.shape, f.dtype) for f in fulls],
        input_output_aliases={ns + i: i for i in range(ns)},
        scratch_shapes=[pltpu.SemaphoreType.DMA((4 * npc,)), pltpu.SemaphoreType.DMA((4 * npc,))],
        name=f"gather_send_{tag}", compiler_params=pltpu.CompilerParams(has_side_effects=True))(*words, *fulls)


def _in_hbm(arrays):
    return [pltpu.with_memory_space_constraint(a, pltpu.HBM) for a in arrays]


def gather_start(words, fulls, specs, dep, tag):
    ns, npc = len(specs), _n_pieces(specs)
    deps = [] if dep is None else [dep]

    def body(*refs):
        ssem, rsem = refs[2 * ns + len(deps):2 * ns + len(deps) + 2]
        for cp in _gather_copies(refs[:ns], refs[ns:2 * ns], specs, ssem, rsem, False):
            cp.start()
        refs[-1][...] = jnp.zeros_like(refs[-1])

    outs = pl.pallas_call(
        body, in_specs=[HBM] * (2 * ns) + [ANY] * len(deps),
        out_specs=[SEM, SEM] + [HBM] * (2 * ns) + [pl.BlockSpec(memory_space=pltpu.VMEM)],
        out_shape=[pltpu.SemaphoreType.DMA((4 * npc,)), pltpu.SemaphoreType.DMA((4 * npc,))]
        + [pltpu.HBM(a.shape, a.dtype) for a in list(words) + list(fulls)] + [jax.ShapeDtypeStruct((8, 128), F32)],
        input_output_aliases={i: 2 + i for i in range(2 * ns)},
        name=f"gather_start_{tag}", compiler_params=pltpu.CompilerParams(has_side_effects=EFFECT))(
            *_in_hbm(list(words) + list(fulls)), *deps)
    return outs[0], outs[1], outs[2:2 + ns], outs[2 + ns:2 + 2 * ns], outs[-1]


def gather_wait(ssem, rsem, words, fulls, specs, after, tag):
    ns = len(specs)

    def body(*refs):
        srcs, bufs, ssem, rsem = refs[:ns], refs[ns:2 * ns], refs[2 * ns], refs[2 * ns + 1]
        for cp in _gather_copies(srcs, bufs, specs, ssem, rsem, False):
            cp.wait_send()
        for cp in _gather_copies(srcs, bufs, specs, ssem, rsem, True):
            cp.wait_recv()

    outs = pl.pallas_call(
        body, in_specs=[HBM] * (2 * ns) + [SEM, SEM, ANY], out_specs=[HBM] * (2 * ns),
        out_shape=[pltpu.HBM(a.shape, a.dtype) for a in list(words) + list(fulls)],
        input_output_aliases={i: i for i in range(2 * ns)},
        name=f"gather_wait_{tag}", compiler_params=pltpu.CompilerParams(has_side_effects=EFFECT))(
            *words, *fulls, ssem, rsem, after)
    return outs[ns:]


def gather_forward(fulls, specs, tag):
    ns, npc = len(specs), _n_pieces(specs)

    def body(*refs):
        bufs = refs[ns:2 * ns]
        ssem, rsem = refs[2 * ns:]
        x, y, c = _coords()
        sibling = (x, y, 1 - c)
        waits = []
        p = 0
        for buf, sp in zip(bufs, specs):
            for s, n in sp.pieces:
                for t, (fx, fy) in enumerate(FLIPS):
                    chip = 4 * (x ^ fx) + 2 * (y ^ fy)
                    here = _rows(buf, sp.off(chip + c, s), n)
                    send = pltpu.make_async_remote_copy(here, here, ssem.at[t, p], rsem.at[t, p],
                                                        device_id=sibling, device_id_type=MESH)
                    send.start()
                    waits.append(send.wait_send)
                    there = _rows(buf, sp.off(chip + 1 - c, s), n)
                    waits.append(pltpu.make_async_remote_copy(here, there, ssem.at[t, p], rsem.at[t, p],
                                                              device_id=sibling, device_id_type=MESH).wait_recv)
                p += 1
        for w in waits:
            w()

    return pl.pallas_call(
        body, in_specs=[ANY] * ns, out_specs=[ANY] * ns,
        out_shape=[jax.ShapeDtypeStruct(f.shape, f.dtype) for f in fulls],
        input_output_aliases={i: i for i in range(ns)},
        scratch_shapes=[pltpu.SemaphoreType.DMA((3, npc)), pltpu.SemaphoreType.DMA((3, npc))],
        name=f"gather_forward_{tag}", compiler_params=pltpu.CompilerParams(has_side_effects=True))(*fulls)


def all_gather(shards, layer, specs, names, tag):
    placed = [pack_place(s, sp, layer, f"{tag}_{n}") for s, sp, n in zip(shards, specs, names)]
    fulls = gather_send([p[0] for p in placed], [p[1] for p in placed], specs, tag)
    return gather_forward(fulls, specs, tag)


def _pair_copies(srcs, theirs, specs, ssem, rsem):
    x, y, c = _coords()
    copies = []
    p = 0
    for src, their, sp in zip(srcs, theirs, specs):
        for s, n in sp.pieces:
            for j in range(4):
                copies.append(pltpu.make_async_remote_copy(_rows(src, sp.off(2 * j + 1 - c, s), n), _rows(their.at[j], s, n),
                                                           ssem.at[4 * p + j], rsem.at[4 * p + j],
                                                           device_id=(x, y, 1 - c), device_id_type=MESH))
            p += 1
    return copies


def _pair_shapes(specs):
    return [(4, sp.layers, sp.rows, sp.cols) for sp in specs]


def reduce_pair(grads, specs, tag, dep=None):
    ns, npc = len(specs), _n_pieces(specs)
    deps = [] if dep is None else [dep]

    def body(*refs):
        copies = _pair_copies(refs[:ns], refs[ns + len(deps):2 * ns + len(deps)], specs, *refs[2 * ns + len(deps):])
        for cp in copies:
            cp.start()
        for cp in copies:
            cp.wait()

    return pl.pallas_call(
        body, in_specs=[ANY] * (ns + len(deps)), out_specs=[ANY] * ns,
        out_shape=[jax.ShapeDtypeStruct(s, F32) for s in _pair_shapes(specs)],
        scratch_shapes=[pltpu.SemaphoreType.DMA((4 * npc,)), pltpu.SemaphoreType.DMA((4 * npc,))],
        name=f"reduce_pair_{tag}", compiler_params=pltpu.CompilerParams(has_side_effects=True))(*grads, *deps)


def pair_start(grads, specs, dep, tag):
    ns, npc = len(specs), _n_pieces(specs)
    slots = [lax.empty(s, F32) for s in _pair_shapes(specs)]
    deps = [] if dep is None else [dep]

    def body(*refs):
        ssem, rsem = refs[2 * ns + len(deps):2 * ns + len(deps) + 2]
        for cp in _pair_copies(refs[:ns], refs[ns:2 * ns], specs, ssem, rsem):
            cp.start()
        refs[-1][...] = jnp.zeros_like(refs[-1])

    outs = pl.pallas_call(
        body, in_specs=[HBM] * (2 * ns) + [ANY] * len(deps),
        out_specs=[SEM, SEM] + [HBM] * (2 * ns) + [pl.BlockSpec(memory_space=pltpu.VMEM)],
        out_shape=[pltpu.SemaphoreType.DMA((4 * npc,)), pltpu.SemaphoreType.DMA((4 * npc,))]
        + [pltpu.HBM(a.shape, a.dtype) for a in list(grads) + slots] + [jax.ShapeDtypeStruct((8, 128), F32)],
        input_output_aliases={i: 2 + i for i in range(2 * ns)},
        name=f"pair_start_{tag}", compiler_params=pltpu.CompilerParams(has_side_effects=EFFECT))(
            *_in_hbm(list(grads) + slots), *deps)
    return outs[0], outs[1], outs[2:2 + ns], outs[2 + ns:2 + 2 * ns], outs[-1]


def pair_wait(ssem, rsem, grads, slots, specs, after, tag):
    ns = len(specs)

    def body(*refs):
        for cp in _pair_copies(refs[:ns], refs[ns:2 * ns], specs, refs[2 * ns], refs[2 * ns + 1]):
            cp.wait_send()
            cp.wait_recv()

    outs = pl.pallas_call(
        body, in_specs=[HBM] * (2 * ns) + [SEM, SEM, ANY], out_specs=[HBM] * (2 * ns),
        out_shape=[pltpu.HBM(a.shape, a.dtype) for a in list(grads) + list(slots)],
        input_output_aliases={i: i for i in range(2 * ns)},
        name=f"pair_wait_{tag}", compiler_params=pltpu.CompilerParams(has_side_effects=EFFECT))(
            *grads, *slots, ssem, rsem, after)
    return outs[:ns], outs[ns:]


def pair_sum(g, r1, sp, tag):
    npc = len(sp.pieces)
    fetch_all = 4 * sp.rows * sp.cols * 4 <= (8 << 20)

    def body(g_ref, r_ref, own_ref, words_ref, gbuf, sem):
        l, j = pl.program_id(0), pl.program_id(1)
        x, y, c = _coords()

        def copies(chip, slot):
            return [pltpu.make_async_copy(g_ref.at[l, pl.ds(pl.multiple_of(sp.off(2 * chip + c, s), 8), n), :],
                                          gbuf.at[slot, pl.ds(s, n), :], sem.at[slot, i])
                    for i, (s, n) in enumerate(sp.pieces)]

        def fetch(chip, slot):
            for cp in copies(chip, slot):
                cp.start()

        def arrived(chip, slot):
            for cp in copies(chip, slot):
                cp.wait()

        if fetch_all:
            @pl.when(j == 0)
            def _():
                for chip in range(4):
                    fetch(chip, chip)
                for chip in range(4):
                    arrived(chip, chip)

            mine = gbuf[j]
        else:
            @pl.when(j == 0)
            def _():
                fetch(0, 0)

            @pl.when(j < 3)
            def _():
                fetch(j + 1, (j + 1) % 2)

            arrived(j, j % 2)
            mine = gbuf[j % 2]
        p = mine + r_ref[...]
        words_ref[...] = sp.to_words(p)

        @pl.when(j == 2 * x + y)
        def _():
            own_ref[...] = p

    return pl.pallas_call(
        body, grid=(sp.layers, 4),
        in_specs=[ANY, pl.BlockSpec((None, None, sp.rows, sp.cols), lambda l, j: (j, l, 0, 0))],
        out_specs=[pl.BlockSpec((None, sp.rows, sp.cols), lambda l, j: (l, 0, 0)),
                   pl.BlockSpec((None, None, sp.rows, sp.wcols), lambda l, j: (j, l, 0, 0))],
        out_shape=[jax.ShapeDtypeStruct((sp.layers, sp.rows, sp.cols), F32),
                   jax.ShapeDtypeStruct((4, sp.layers, sp.rows, sp.wcols), F32)],
        scratch_shapes=[pltpu.VMEM((4 if fetch_all else 2, sp.rows, sp.cols), F32), pltpu.SemaphoreType.DMA((4, npc))],
        name=f"pair_sum_{tag}", compiler_params=_params(("arbitrary", "arbitrary")))(g, r1)


def _chip_copies(srcs, dsts, ssem, rsem):
    x, y, c = _coords()
    copies = []
    for i, (src, dst) in enumerate(zip(srcs, dsts)):
        for t, (fx, fy) in enumerate(FLIPS):
            tx, ty = x ^ fx, y ^ fy
            copies.append(pltpu.make_async_remote_copy(src.at[2 * tx + ty], dst.at[t], ssem.at[3 * i + t], rsem.at[3 * i + t],
                                                       device_id=(tx, ty, c), device_id_type=MESH))
    return copies


def _slot_shapes(words):
    return [(3,) + w.shape[1:] for w in words]


def reduce_chips(words, specs, tag):
    ns = len(specs)

    def body(*refs):
        copies = _chip_copies(refs[:ns], refs[ns:2 * ns], *refs[2 * ns:])
        for cp in copies:
            cp.start()
        for cp in copies:
            cp.wait()

    return pl.pallas_call(
        body, in_specs=[ANY] * ns, out_specs=[ANY] * ns,
        out_shape=[jax.ShapeDtypeStruct(s, F32) for s in _slot_shapes(words)],
        scratch_shapes=[pltpu.SemaphoreType.DMA((3 * ns,)), pltpu.SemaphoreType.DMA((3 * ns,))],
        name=f"reduce_chips_{tag}", compiler_params=pltpu.CompilerParams(has_side_effects=True))(*words)


def chips_start(words, specs, tag):
    ns = len(specs)
    slots = [lax.empty(s, F32) for s in _slot_shapes(words)]

    def body(*refs):
        ssem, rsem = refs[2 * ns:2 * ns + 2]
        for cp in _chip_copies(refs[:ns], refs[ns:2 * ns], ssem, rsem):
            cp.start()
        refs[-1][...] = jnp.zeros_like(refs[-1])

    outs = pl.pallas_call(
        body, in_specs=[HBM] * (2 * ns),
        out_specs=[SEM, SEM] + [HBM] * (2 * ns) + [pl.BlockSpec(memory_space=pltpu.VMEM)],
        out_shape=[pltpu.SemaphoreType.DMA((3 * ns,)), pltpu.SemaphoreType.DMA((3 * ns,))]
        + [pltpu.HBM(a.shape, a.dtype) for a in list(words) + slots] + [jax.ShapeDtypeStruct((8, 128), F32)],
        input_output_aliases={i: 2 + i for i in range(2 * ns)},
        name=f"chips_start_{tag}", compiler_params=pltpu.CompilerParams(has_side_effects=EFFECT))(
            *_in_hbm(list(words) + slots))
    return outs[0], outs[1], outs[2:2 + ns], outs[2 + ns:2 + 2 * ns], outs[-1]


def chips_wait(ssem, rsem, words, slots, specs, after, tag):
    ns = len(specs)

    def body(*refs):
        for cp in _chip_copies(refs[:ns], refs[ns:2 * ns], refs[2 * ns], refs[2 * ns + 1]):
            cp.wait_send()
            cp.wait_recv()

    outs = pl.pallas_call(
        body, in_specs=[HBM] * (2 * ns) + [SEM, SEM, ANY], out_specs=[HBM] * (2 * ns),
        out_shape=[pltpu.HBM(a.shape, a.dtype) for a in list(words) + list(slots)],
        input_output_aliases={i: i for i in range(2 * ns)},
        name=f"chips_wait_{tag}", compiler_params=pltpu.CompilerParams(has_side_effects=EFFECT))(
            *words, *slots, ssem, rsem, after)
    return outs[ns:]


def sum_chips(own, r2, sp, tag):
    def body(own_ref, r_ref, o_ref):
        o_ref[...] = ((own_ref[...] + sp.from_words(r_ref[0])) + sp.from_words(r_ref[1])) + sp.from_words(r_ref[2])

    blk = pl.BlockSpec((None, sp.rows, sp.cols), lambda l: (l, 0, 0))
    return pl.pallas_call(
        body, grid=(sp.layers,), in_specs=[blk, pl.BlockSpec((3, None, sp.rows, sp.wcols), lambda l: (0, l, 0, 0))],
        out_specs=blk, out_shape=jax.ShapeDtypeStruct((sp.layers, sp.rows, sp.cols), F32),
        name=f"sum_chips_{tag}", compiler_params=_params(("arbitrary",)))(own, r2)


def reduce_scatter_start(grads, specs, names, dep, tag):
    theirs = reduce_pair(grads, specs, tag, dep)
    sums = [pair_sum(g, r1, sp, f"{tag}_{n}") for g, r1, sp, n in zip(grads, theirs, specs, names)]
    ssem, rsem, words, slots, token = chips_start([s[1] for s in sums], specs, tag)
    return (ssem, rsem, words, slots, [s[0] for s in sums]), token


def reduce_scatter_finish(state, after, specs, tag):
    ssem, rsem, words, slots, own = state
    return list(zip(own, chips_wait(ssem, rsem, words, slots, specs, after, tag)))


def reduce_scatter(grads, specs, names, tag, dep=None):
    theirs = reduce_pair(grads, specs, tag, dep)
    sums = [pair_sum(g, r1, sp, f"{tag}_{n}") for g, r1, sp, n in zip(grads, theirs, specs, names)]
    return list(zip([s[0] for s in sums], reduce_chips([s[1] for s in sums], specs, tag)))


def _adamw_math(w, g, m, v):
    c1 = 1.0 - ADAM_B1 ** ADAM_STEP
    c2 = 1.0 - ADAM_B2 ** ADAM_STEP
    m2 = ADAM_B1 * m + (1.0 - ADAM_B1) * g
    v2 = ADAM_B2 * v + (1.0 - ADAM_B2) * (g * g)
    return -ADAM_LR * ((m2 / c1) / (jnp.sqrt(v2 / c2) + ADAM_EPS) + ADAM_WD * w), m2, v2


def adamw_small(ws, gs, ms, vs):
    n = len(ws)
    flat = lambda a: a.reshape(math.prod(a.shape[:-1]), a.shape[-1])

    def body(*refs):
        ins, outs = refs[:4 * n], refs[4 * n:]
        for i in range(n):
            w_ref, g_ref, m_ref, v_ref = ins[4 * i:4 * i + 4]
            outs[3 * i][...], outs[3 * i + 1][...], outs[3 * i + 2][...] = _adamw_math(
                w_ref[...], g_ref[...], m_ref[...], v_ref[...])

    args = [flat(a) for quad in zip(ws, gs, ms, vs) for a in quad]
    res = pl.pallas_call(
        body, out_shape=[jax.ShapeDtypeStruct(flat(w).shape, F32) for w in ws for _ in range(3)],
        name="adamw_small", compiler_params=_params())(*args)
    return [[res[3 * i + k].reshape(ws[i].shape) for k in range(3)] for i in range(n)]


def adamw_layer(w, sums, m, v, sp, l, prev, dep, name):
    _, rows, cols = w.shape
    tc = _col_tile(cols)
    twc = tc // 2 if sp.packed else tc
    extra = ([] if prev is None else list(prev)) + ([] if dep is None else [dep])

    def body(w_ref, own_ref, r_ref, m_ref, v_ref, *rest):
        g_ref, d_ref, nm_ref, nv_ref = rest[-4:]
        g = ((own_ref[...] + sp.from_words(r_ref[0])) + sp.from_words(r_ref[1])) + sp.from_words(r_ref[2])
        g_ref[...] = g
        d_ref[...], nm_ref[...], nv_ref[...] = _adamw_math(w_ref[...], g, m_ref[...], v_ref[...])

    blk = pl.BlockSpec((None, rows, tc), lambda n: (l, 0, n))
    return pl.pallas_call(
        body, grid=(cols // tc,),
        in_specs=[blk, pl.BlockSpec((None, rows, tc), lambda n: (0, 0, n)),
                  pl.BlockSpec((3, None, rows, twc), lambda n: (0, 0, 0, n)), blk, blk] + [ANY] * len(extra),
        out_specs=[blk] * 4, out_shape=[jax.ShapeDtypeStruct(w.shape, F32)] * 4,
        input_output_aliases={} if prev is None else {5 + i: i for i in range(4)},
        name=f"adamw_{name}_l{l}", compiler_params=_params(("arbitrary",)))(w, sums[0], sums[1], m, v, *extra)


WEIGHTS = ("pre_norm_g", "w_in", "gm_ln_g", "gm_ln_b", "gm_ws", "gm_bs", "mla_q_norm_g", "mla_w_uq", "mla_kv_norm_g",
           "mla_w_ukv", "lru_conv_w", "lru_conv_b", "lru_w_a", "lru_b_a", "lru_w_x", "lru_b_x", "lru_lambda",
           "w_proj_a", "w_proj_b", "w_proj_c", "w_out", "post_norm_g")
SHARDED = ("w_in", "mla_w_uq", "mla_w_ukv", "lru_conv_w", "w_proj_a", "w_proj_b", "w_proj_c", "w_out")
REPLICATED = tuple(n for n in WEIGHTS if n not in SHARDED)


def _step(x, target, wts, ms, vs):
    t12 = lambda a: jnp.swapaxes(a, 1, 2)
    names = list(WEIGHT_SPECS)
    specs = [WEIGHT_SPECS[n] for n in names]
    tabs = _rope_tables()
    own = {"w_in_t": t12(wts["w_in"]), "wq": t12(wts["mla_w_uq"]), "wkv": t12(wts["mla_w_ukv"]),
           "conv": jnp.pad(t12(wts["lru_conv_w"]), ((0, 0), (0, 0), (0, 124))),
           "w_proj_a": wts["w_proj_a"], "w_proj_b": wts["w_proj_b"], "w_proj_c": wts["w_proj_c"], "w_out": wts["w_out"]}
    first, rest = ["w_in_t"], [n for n in names if n != "w_in_t"]
    sfirst, srest = [WEIGHT_SPECS[n] for n in first], [WEIGHT_SPECS[n] for n in rest]

    w = {n: wts[n] for n in REPLICATED}
    w["kv_g384"] = jnp.concatenate([wts["mla_kv_norm_g"], jnp.ones((L, 128), F32)], axis=1)

    def layer_weights(ns, words):
        gw = dict(zip(ns, words))
        gw["wq"] = gw["wq"].reshape(HEADS, 192, 384)
        gw["wkv"] = gw["wkv"].reshape(HEADS, 256, 128)
        gw["conv"] = gw["conv"][0, :, :4].T
        return gw

    place = lambda l, dep: {n: pack_place(own[n], WEIGHT_SPECS[n], l, f"w{l}_{n}", dep) for n in names}
    placed = [place(0, None)]
    words_of = lambda l, ns: [placed[l][n][0] for n in ns]
    bufs_of = lambda l, ns: [placed[l][n][1] for n in ns]
    later = {}

    ssem_a, rsem_a, wthru_a, fthru_a, token_a = gather_start(words_of(0, first), bufs_of(0, first), sfirst, None, "w0a")
    placed.append(place(1, token_a))
    win0 = gather_forward(gather_wait(ssem_a, rsem_a, wthru_a, fthru_a, sfirst, placed[1]["w_in_t"][0], "w0a"), sfirst, "w0a")
    ssem_b, rsem_b, wthru_b, fthru_b, token_b = gather_start(words_of(0, rest), bufs_of(0, rest), srest, win0[0], "w0b")
    ssem1, rsem1, wthru1, fthru1, token1 = gather_start(words_of(1, names), bufs_of(1, names), specs, token_b, "w1")

    def fwd0_mid(ya):
        rest0 = gather_forward(gather_wait(ssem_b, rsem_b, wthru_b, fthru_b, srest, ya, "w0b"), srest, "w0b")
        later["gw0"] = layer_weights(first + rest, list(win0) + list(rest0))
        return later["gw0"], None

    x1, saved0 = _layer_fwd(x, 0, w, {"w_in_t": win0[0]}, tabs, dep=token1, mid=fwd0_mid)
    words1 = gather_forward(gather_wait(ssem1, rsem1, wthru1, fthru1, specs, x1, "w1"), specs, "w1")
    gw0, gw1 = later["gw0"], layer_weights(names, words1)
    x2, saved1 = _layer_fwd(x1, 1, w, gw1, tabs)
    loss, dx2 = loss_head(x2, target)

    def bwd1_mid(gg, last):
        later["p1b"] = pair_start([gg[n] for n in rest], srest, last, "g1b")
        return later["p1b"][4]

    dx1, gg1, g1 = _layer_bwd(dx2, 1, w, gw1, tabs, saved1, mid=bwd1_mid)
    grads1b, theirs1b = pair_wait(*later["p1b"][:4], srest, dx1, "g1b")
    p1a = pair_start([gg1["w_in_t"]], sfirst, theirs1b[0], "g1a")

    def bwd0_early(last):
        grads1a, theirs1a = pair_wait(*p1a[:4], sfirst, last, "g1a")
        mine = dict(zip(first + rest, list(grads1a) + list(grads1b)))
        theirs = dict(zip(first + rest, list(theirs1a) + list(theirs1b)))
        sums = [pair_sum(mine[n], theirs[n], WEIGHT_SPECS[n], f"g1_{n}") for n in names]
        ssem, rsem, words, slots, token = chips_start([s[1] for s in sums], specs, "g1")
        later["g1"] = (ssem, rsem, words, slots, [s[0] for s in sums])
        return token

    def bwd0_mid(gg, last):
        later["g0b"], token = reduce_scatter_start([gg[n] for n in rest], srest, rest, last, "g0b")
        return token

    def bwd0_late(g_win):
        later["p0a"] = pair_start([g_win], sfirst, None, "g0a")
        return later["p0a"][4]

    dx0, gg0, g0 = _layer_bwd(dx1, 0, w, gw0, tabs, saved0, dep=p1a[4], early=bwd0_early, mid=bwd0_mid, late=bwd0_late)
    s1 = dict(zip(names, reduce_scatter_finish(later["g1"], dx0, specs, "g1")))
    s0 = dict(zip(rest, reduce_scatter_finish(later["g0b"], dx0, srest, "g0b")))

    grads0a, theirs0a = pair_wait(*later["p0a"][:4], sfirst, dx0, "g0a")
    own0a, words0a = pair_sum(grads0a[0], theirs0a[0], sfirst[0], "g0a_w_in_t")
    ssem_g, rsem_g, wthru_g, slots_g, token_g = chips_start([words0a], sfirst, "g0a")

    keys = {"w_in": "w_in_t", "mla_w_uq": "wq", "mla_w_ukv": "wkv",
            "w_proj_a": "w_proj_a", "w_proj_b": "w_proj_b", "w_proj_c": "w_proj_c", "w_out": "w_out"}
    transposed = ("w_in", "mla_w_uq", "mla_w_ukv")
    state_of = lambda n: [own[keys[n]], t12(ms[n]), t12(vs[n])] if n in transposed else [wts[n], ms[n], vs[n]]

    def update(n, l, sums, prev, dep):
        wl, ml, vl = state_of(n)
        return adamw_layer(wl, sums[keys[n]], ml, vl, WEIGHT_SPECS[keys[n]], l, prev, dep, n)

    upd = {n: update(n, 1, s1, None, token_g) for n in keys}
    for n in keys:
        if n != "w_in":
            upd[n] = update(n, 0, s0, upd[n], None)
    rep_flat = jnp.concatenate([jnp.stack([g0[n], g1[n]]).reshape(-1) for n in REPLICATED] + [loss[None]])
    rep_flat = jnp.pad(rep_flat, (0, REP_ROWS * NDEV * D - rep_flat.shape[0])).reshape(1, REP_ROWS * NDEV, D)
    rep_parts = reduce_scatter([rep_flat], [REP_SPEC], ["rep"], "rep", upd["w_out"][0])[0]
    rep_sum = sum_chips(*rep_parts, REP_SPEC, "rep")
    rep_full = all_gather([rep_sum], 0, [REP_SPEC], ["rep"], "rep")[0].reshape(-1)

    out = {}
    conv_sp = WEIGHT_SPECS["conv"]
    g_conv = t12(jnp.concatenate([sum_chips(*s0["conv"], conv_sp, "conv0"), sum_chips(*s1["conv"], conv_sp, "conv1")])[:, :, :4])
    small = {"lru_conv_w": g_conv}
    at = 0
    for n in REPLICATED:
        size = math.prod(wts[n].shape)
        small[n] = rep_full[at:at + size].reshape(wts[n].shape)
        at += size
    updates = adamw_small([wts[n] for n in small], list(small.values()), [ms[n] for n in small], [vs[n] for n in small])
    for n, u in zip(small, updates):
        out[n] = [small[n]] + u

    landed = chips_wait(ssem_g, rsem_g, wthru_g, slots_g, sfirst, out[REPLICATED[-1]][1], "g0a")
    s0["w_in_t"] = (own0a, landed[0])
    upd["w_in"] = update("w_in", 0, s0, upd["w_in"], None)
    out.update({n: [t12(r) for r in upd[n]] if n in transposed else upd[n] for n in keys})

    return (rep_full[at], dx0[None], *[out[n][k] for k in range(4) for n in WEIGHTS])


def kernel(x, pre_norm_g, w_in, gm_ln_g, gm_ln_b, gm_ws, gm_bs, mla_q_norm_g, mla_w_uq, mla_kv_norm_g, mla_w_ukv, lru_conv_w, lru_conv_b, lru_w_a, lru_b_a, lru_w_x, lru_b_x, lru_lambda, w_proj_a, w_proj_b, w_proj_c, w_out, post_norm_g, loss_target, m_pre_norm_g, m_w_in, m_gm_ln_g, m_gm_ln_b, m_gm_ws, m_gm_bs, m_mla_q_norm_g, m_mla_w_uq, m_mla_kv_norm_g, m_mla_w_ukv, m_lru_conv_w, m_lru_conv_b, m_lru_w_a, m_lru_b_a, m_lru_w_x, m_lru_b_x, m_lru_lambda, m_w_proj_a, m_w_proj_b, m_w_proj_c, m_w_out, m_post_norm_g, v_pre_norm_g, v_w_in, v_gm_ln_g, v_gm_ln_b, v_gm_ws, v_gm_bs, v_mla_q_norm_g, v_mla_w_uq, v_mla_kv_norm_g, v_mla_w_ukv, v_lru_conv_w, v_lru_conv_b, v_lru_w_a, v_lru_b_a, v_lru_w_x, v_lru_b_x, v_lru_lambda, v_w_proj_a, v_w_proj_b, v_w_proj_c, v_w_out, v_post_norm_g):
    wts = dict(zip(WEIGHTS, (pre_norm_g, w_in, gm_ln_g, gm_ln_b, gm_ws, gm_bs, mla_q_norm_g, mla_w_uq, mla_kv_norm_g,
                             mla_w_ukv, lru_conv_w, lru_conv_b, lru_w_a, lru_b_a, lru_w_x, lru_b_x, lru_lambda,
                             w_proj_a, w_proj_b, w_proj_c, w_out, post_norm_g)))
    ms = dict(zip(WEIGHTS, (m_pre_norm_g, m_w_in, m_gm_ln_g, m_gm_ln_b, m_gm_ws, m_gm_bs, m_mla_q_norm_g, m_mla_w_uq,
                            m_mla_kv_norm_g, m_mla_w_ukv, m_lru_conv_w, m_lru_conv_b, m_lru_w_a, m_lru_b_a, m_lru_w_x,
                            m_lru_b_x, m_lru_lambda, m_w_proj_a, m_w_proj_b, m_w_proj_c, m_w_out, m_post_norm_g)))
    vs = dict(zip(WEIGHTS, (v_pre_norm_g, v_w_in, v_gm_ln_g, v_gm_ln_b, v_gm_ws, v_gm_bs, v_mla_q_norm_g, v_mla_w_uq,
                            v_mla_kv_norm_g, v_mla_w_ukv, v_lru_conv_w, v_lru_conv_b, v_lru_w_a, v_lru_b_a, v_lru_w_x,
                            v_lru_b_x, v_lru_lambda, v_w_proj_a, v_w_proj_b, v_w_proj_c, v_w_out, v_post_norm_g)))
    return _step(x[0], loss_target[0], wts, ms, vs)
```

```python
import functools
import math

import jax
import jax.numpy as jnp
from jax import lax
from jax.experimental import pallas as pl
from jax.experimental.pallas import tpu as pltpu

F32 = jnp.float32
BF16 = jnp.bfloat16

T = 2048
D = 1024
L = 2
NDEV = 8
EPS = 1e-6
CHUNK_SHIFT = 6
HEADS = 8
QK = 192
LRU_W = 1280
LRU_TILE = 640
N_IN = 10432
SHARD = N_IN // NDEV
OFF_U, OFF_V, OFF_ZA, OFF_CQ, OFF_CKV, OFF_ZB = 0, 1024, 2048, 3072, 3456, 3840
OFF_XC, OFF_ZC, OFF_GA, OFF_GB, OFF_GC = 5120, 6400, 7680, 8704, 9728
NPAD = 10752
PAD1_AT, PAD1 = 3776, 64
PAD2_AT, PAD2 = 4800, 256
WIN_PIECES = ((0, 888), (888, 280), (1168, 136))
VMEM_LIMIT = 60 * 1024 * 1024

ADAM_LR, ADAM_B1, ADAM_B2, ADAM_EPS, ADAM_WD, ADAM_STEP = 0.001, 0.9, 0.999, 1e-08, 0.01, 10

_NN = (((1,), (0,)), ((), ()))
_NT = (((1,), (1,)), ((), ()))
_TN = (((0,), (0,)), ((), ()))


def _dg(a, b, dims):
    return lax.dot_general(a.astype(BF16), b.astype(BF16), dims, preferred_element_type=F32)


@jax.custom_vjp
def dot_nn(a, b):
    return _dg(a, b, _NN)


def _nn_fwd(a, b):
    return _dg(a, b, _NN), (a, b)


def _nn_bwd(res, g):
    a, b = res
    return _dg(g, b, _NT).astype(a.dtype), _dg(a, g, _TN).astype(b.dtype)


dot_nn.defvjp(_nn_fwd, _nn_bwd)


@jax.custom_vjp
def dot_nt(a, b):
    return _dg(a, b, _NT)


def _nt_fwd(a, b):
    return _dg(a, b, _NT), (a, b)


def _nt_bwd(res, g):
    a, b = res
    return _dg(g, b, _NN).astype(a.dtype), _dg(g, a, _TN).astype(b.dtype)


dot_nt.defvjp(_nt_fwd, _nt_bwd)


def _params(sem=None):
    return pltpu.CompilerParams(dimension_semantics=sem, vmem_limit_bytes=VMEM_LIMIT)


def _sigmoid(x):
    return 1.0 / (1.0 + jnp.exp(-x))


def _silu(x):
    return x * _sigmoid(x)


def _rms(x, g):
    ms = jnp.mean(x * x, axis=-1, keepdims=True)
    return x * lax.rsqrt(ms + EPS) * g


def _acc(ref, val, first):
    @pl.when(first)
    def _():
        ref[...] = val

    @pl.when(jnp.logical_not(first))
    def _():
        ref[...] += val


ANY = pl.BlockSpec(memory_space=pl.ANY)


INPROJ_TN = 768


def inproj_fwd(x, g, wt, l, dep=None):
    tn = INPROJ_TN

    def body(x_ref, g_ref, w_ref, *rest):
        proj_ref, h_ref = rest[-2:]

        @pl.when(pl.program_id(0) == 0)
        def _():
            h_ref[...] = _rms(x_ref[...], g_ref[...]).astype(BF16)

        proj_ref[...] = lax.dot_general(h_ref[...], _unpack(w_ref[...]).astype(BF16), _NT, preferred_element_type=F32)

    deps = [] if dep is None else [dep]
    return pl.pallas_call(
        body, grid=(NPAD // tn,),
        in_specs=[pl.BlockSpec((T, D), lambda j: (0, 0)), pl.BlockSpec((1, D), lambda j: (0, 0)),
                  pl.BlockSpec((None, tn, D // 2), lambda j: (0, j, 0))] + [ANY] * len(deps),
        out_specs=[pl.BlockSpec((T, tn), lambda j: (0, j)), pl.BlockSpec((T, D), lambda j: (0, 0))],
        out_shape=[jax.ShapeDtypeStruct((T, NPAD), F32), jax.ShapeDtypeStruct((T, D), BF16)],
        name=f"inproj_fwd_l{l}", compiler_params=_params(("arbitrary",)))(x, g, wt, *deps)


def inproj_bwd(dproj, h, wt, l, dep=None):
    tn = INPROJ_TN
    deps = [] if dep is None else [dep]

    def body(dp_ref, h_ref, w_ref, *rest):
        dwt_ref, dh_ref = rest[-2:]
        dp = dp_ref[...]
        dwt_ref[...] = lax.dot_general(dp, h_ref[...], _TN, preferred_element_type=F32)
        contrib = lax.dot_general(dp, _unpack(w_ref[...]).astype(BF16), _NN, preferred_element_type=F32)
        _acc(dh_ref, contrib, pl.program_id(0) == 0)

    return pl.pallas_call(
        body, grid=(NPAD // tn,),
        in_specs=[pl.BlockSpec((T, tn), lambda j: (0, j)), pl.BlockSpec((T, D), lambda j: (0, 0)),
                  pl.BlockSpec((None, tn, D // 2), lambda j: (0, j, 0))] + [ANY] * len(deps),
        out_specs=[pl.BlockSpec((None, tn, D), lambda j: (0, j, 0)), pl.BlockSpec((T, D), lambda j: (0, 0))],
        out_shape=[jax.ShapeDtypeStruct((1, NPAD, D), F32), jax.ShapeDtypeStruct((T, D), F32)],
        name=f"inproj_bwd_l{l}", compiler_params=_params(("arbitrary",)))(dproj, h, wt, *deps)


def prenorm_bwd(x, g, dh, dxn, l, dep=None):
    tm = 512
    deps = [] if dep is None else [dep]

    def body(x_ref, g_ref, dh_ref, dxn_ref, *rest):
        dx_ref, dg_ref = rest[-2:]
        _, vjp = jax.vjp(_rms, x_ref[...], g_ref[...])
        dx, dg = vjp(dh_ref[...])
        dx_ref[...] = dx + dxn_ref[...]
        _acc(dg_ref, dg, pl.program_id(0) == 0)

    tok = pl.BlockSpec((tm, D), lambda i: (i, 0))
    vec = pl.BlockSpec((1, D), lambda i: (0, 0))
    return pl.pallas_call(
        body, grid=(T // tm,), in_specs=[tok, vec, tok, tok] + [ANY] * len(deps), out_specs=[tok, vec],
        out_shape=[jax.ShapeDtypeStruct((T, D), F32), jax.ShapeDtypeStruct((1, D), F32)],
        name=f"prenorm_bwd_l{l}", compiler_params=_params(("arbitrary",)))(x, g, dh, dxn, *deps)


def _gmlp_tile(u, v, z, ln_g, ln_b, ws, bs):
    mu = jnp.mean(v, axis=-1, keepdims=True)
    vc = v - mu
    var = jnp.mean(vc * vc, axis=-1, keepdims=True)
    vn = vc * lax.rsqrt(var + EPS) * ln_g + ln_b
    qi = lax.broadcasted_iota(jnp.int32, (128, 128), 0) >> CHUNK_SHIFT
    kj = lax.broadcasted_iota(jnp.int32, (128, 128), 1) >> CHUNK_SHIFT
    mask = kj <= qi
    outs = []
    for g in range(4):
        wm = jnp.where(mask, ws[g], 0.0)
        outs.append(dot_nn(wm, vn[:, 256 * g:256 * (g + 1)]) + bs[g])
    sv = jnp.concatenate(outs, axis=1)
    return u * sv * _silu(z)


GMLP_ROWS = 512


def _gmlp_specs():
    blk = lambda c: pl.BlockSpec((GMLP_ROWS, 1024), lambda n, c=c: (n, c))
    vec = pl.BlockSpec((1, 1024), lambda n: (0, 0))
    return [blk(0), blk(1), blk(2), vec, vec,
            pl.BlockSpec((4, 128, 128), lambda n: (0, 0, 0)), pl.BlockSpec((4, 128, 1), lambda n: (0, 0, 0))]


def gmlp_fwd(proj, ln_g, ln_b, ws, bs, l):
    def body(u_ref, v_ref, z_ref, g_ref, b_ref, ws_ref, bs_ref, y_ref):
        for r in range(0, GMLP_ROWS, 128):
            rows = slice(r, r + 128)
            y_ref[rows, :] = _gmlp_tile(u_ref[rows, :], v_ref[rows, :], z_ref[rows, :], g_ref[...], b_ref[...],
                                        [ws_ref[g] for g in range(4)], [bs_ref[g] for g in range(4)])

    return pl.pallas_call(
        body, grid=(T // GMLP_ROWS,), in_specs=_gmlp_specs(),
        out_specs=pl.BlockSpec((GMLP_ROWS, 1024), lambda n: (n, 0)),
        out_shape=jax.ShapeDtypeStruct((T, 1024), F32),
        name=f"gmlp_fwd_l{l}", compiler_params=_params(("arbitrary",)))(proj, proj, proj, ln_g, ln_b, ws, bs)


def gmlp_bwd(proj, ln_g, ln_b, ws, bs, dy, dproj, l):
    def body(u_ref, v_ref, z_ref, g_ref, b_ref, ws_ref, bs_ref, dy_ref, _, dseg_ref, dg_ref, db_ref, dws_ref, dbs_ref):
        for r in range(0, GMLP_ROWS, 128):
            rows = slice(r, r + 128)
            first = jnp.logical_and(pl.program_id(0) == 0, r == 0)
            _, vjp = jax.vjp(_gmlp_tile, u_ref[rows, :], v_ref[rows, :], z_ref[rows, :], g_ref[...], b_ref[...],
                             [ws_ref[g] for g in range(4)], [bs_ref[g] for g in range(4)])
            du, dv, dz, dg, db, dws, dbs = vjp(dy_ref[rows, :])
            dseg_ref[rows, 0:1024] = du.astype(BF16)
            dseg_ref[rows, 1024:2048] = dv.astype(BF16)
            dseg_ref[rows, 2048:3072] = dz.astype(BF16)
            _acc(dg_ref, dg, first)
            _acc(db_ref, db, first)
            for g in range(4):
                _acc(dws_ref.at[g], dws[g], first)
                _acc(dbs_ref.at[g], dbs[g], first)

    vec = pl.BlockSpec((1, 1024), lambda n: (0, 0))
    return pl.pallas_call(
        body, grid=(T // GMLP_ROWS,),
        in_specs=_gmlp_specs() + [pl.BlockSpec((GMLP_ROWS, 1024), lambda n: (n, 0)), ANY],
        out_specs=[pl.BlockSpec((GMLP_ROWS, 3072), lambda n: (n, OFF_U // 3072)), vec, vec,
                   pl.BlockSpec((4, 128, 128), lambda n: (0, 0, 0)), pl.BlockSpec((4, 128, 1), lambda n: (0, 0, 0))],
        out_shape=[jax.ShapeDtypeStruct((T, NPAD), BF16), jax.ShapeDtypeStruct((1, 1024), F32),
                   jax.ShapeDtypeStruct((1, 1024), F32), jax.ShapeDtypeStruct((4, 128, 128), F32),
                   jax.ShapeDtypeStruct((4, 128, 1), F32)],
        input_output_aliases={8: 0},
        name=f"gmlp_bwd_l{l}", compiler_params=_params(("arbitrary",)))(proj, proj, proj, ln_g, ln_b, ws, bs, dy, dproj)


QKV_TM = 512


def _qkv_tile(cq, ckvr, qg, kvg, wq, wkv, ctab, stab):
    tm = cq.shape[0]
    cqn = _rms(cq, qg)
    lane = lax.broadcasted_iota(jnp.int32, ckvr.shape, 1)
    iskv = lane < 256
    ms = jnp.sum(jnp.where(iskv, ckvr * ckvr, 0.0), axis=-1, keepdims=True) * (1.0 / 256)
    lm = jnp.where(iskv, ckvr * lax.rsqrt(ms + EPS) * kvg, ckvr)
    r = lax.broadcasted_iota(jnp.int32, (64, 128), 0)
    c = lax.broadcasted_iota(jnp.int32, (64, 128), 1)
    eye = jnp.where(c == r, 1.0, 0.0)
    eye_sw = jnp.where(c == ((r + 32) & 63), 1.0, 0.0)
    z64 = jnp.zeros((64, 256), F32)
    z128 = jnp.zeros((128, 128), F32)
    rk_rope = jnp.concatenate([z64, eye], axis=1)
    rk_sw = jnp.concatenate([jnp.zeros((128, 384), F32), jnp.concatenate([z64, eye_sw], axis=1)], axis=0)
    k_sw = dot_nt(lm, rk_sw) * stab
    qs, ks, vs = [], [], []
    for h in range(HEADS):
        wn, w1, w2 = wq[h]
        wk, wv = wkv[h]
        wq_h = jnp.concatenate([wn, w1, w2], axis=0)
        wq_sw = jnp.concatenate([jnp.zeros((128, 384), F32), w2, w1], axis=0)
        qs.append(dot_nt(cqn, wq_h) * ctab + dot_nt(cqn, wq_sw) * stab)
        rk_h = jnp.concatenate([jnp.concatenate([wk, z128], axis=1), rk_rope], axis=0)
        ks.append(dot_nt(lm, rk_h) * ctab + k_sw)
        vs.append(dot_nt(lm, jnp.concatenate([wv, z128], axis=1)))
    return qs, ks, vs


def _qkv_in_specs():
    tm = QKV_TM
    return [pl.BlockSpec((tm, 384), lambda i: (i, OFF_CQ // 384)), pl.BlockSpec((tm, 384), lambda i: (i, OFF_CKV // 384)),
            pl.BlockSpec((1, 384), lambda i: (0, 0)), pl.BlockSpec((1, 384), lambda i: (0, 0)),
            pl.BlockSpec((HEADS, 192, 384), lambda i: (0, 0, 0)), pl.BlockSpec((HEADS, 256, 128), lambda i: (0, 0, 0)),
            pl.BlockSpec((tm, 192), lambda i: (i, 0)), pl.BlockSpec((tm, 192), lambda i: (i, 0))]


def _qkv_weights(wq_ref, wkv_ref):
    wq = [(wq_ref[h, 0:128, :], wq_ref[h, 128:160, :], wq_ref[h, 160:192, :]) for h in range(HEADS)]
    wkv = [(_unpack(wkv_ref[h, 0:128, :]), _unpack(wkv_ref[h, 128:256, :])) for h in range(HEADS)]
    return wq, wkv


def qkv_fwd(proj, qg, kvg, wq, wkv, ctab, stab, l, dep=None):
    tm = QKV_TM
    deps = [] if dep is None else [dep]

    def body(cq_ref, ckvr_ref, qg_ref, kvg_ref, wq_ref, wkv_ref, c_ref, s_ref, *rest):
        q_ref, k_ref, v_ref = rest[-3:]
        wq_l, wkv_l = _qkv_weights(wq_ref, wkv_ref)
        qs, ks, vs = _qkv_tile(cq_ref[...], ckvr_ref[...], qg_ref[...], kvg_ref[...], wq_l, wkv_l, c_ref[...], s_ref[...])
        for h in range(HEADS):
            q_ref[h] = qs[h]
            k_ref[h] = ks[h]
            v_ref[h] = vs[h]

    return pl.pallas_call(
        body, grid=(T // tm,), in_specs=_qkv_in_specs() + [ANY] * len(deps),
        out_specs=[pl.BlockSpec((HEADS, tm, QK), lambda i: (0, i, 0)), pl.BlockSpec((HEADS, tm, QK), lambda i: (0, i, 0)),
                   pl.BlockSpec((HEADS, tm, 128), lambda i: (0, i, 0))],
        out_shape=[jax.ShapeDtypeStruct((HEADS, T, QK), F32), jax.ShapeDtypeStruct((HEADS, T, QK), F32),
                   jax.ShapeDtypeStruct((HEADS, T, 128), F32)],
        name=f"qkv_fwd_l{l}", compiler_params=_params(("arbitrary",)))(proj, proj, qg, kvg, wq, wkv, ctab, stab, *deps)


def qkv_bwd(proj, qg, kvg, wq, wkv, ctab, stab, dq, dk, dv, dproj, l):
    tm = QKV_TM

    def body(cq_ref, ckvr_ref, qg_ref, kvg_ref, wq_ref, wkv_ref, c_ref, s_ref, dq_ref, dk_ref, dv_ref, _,
             dseg_ref, dqg_ref, dkvg_ref, dwq_ref, dwkv_ref):
        first = pl.program_id(0) == 0
        wq_l, wkv_l = _qkv_weights(wq_ref, wkv_ref)
        c_tab, s_tab = c_ref[...], s_ref[...]
        fn = lambda cq, ckvr, qg_, kvg_, wq_, wkv_: _qkv_tile(cq, ckvr, qg_, kvg_, wq_, wkv_, c_tab, s_tab)
        _, vjp = jax.vjp(fn, cq_ref[...], ckvr_ref[...], qg_ref[...], kvg_ref[...], wq_l, wkv_l)
        cts = ([dq_ref[h] for h in range(HEADS)], [dk_ref[h] for h in range(HEADS)], [dv_ref[h] for h in range(HEADS)])
        dcq, dckvr, dqg, dkvg, dwq, dwkv = vjp(cts)
        dseg_ref[:, 0:384] = dcq.astype(BF16)
        dseg_ref[:, 384:768] = dckvr.astype(BF16)
        _acc(dqg_ref, dqg, first)
        _acc(dkvg_ref, dkvg, first)
        for h in range(HEADS):
            _acc(dwq_ref.at[h, 0:128, :], dwq[h][0], first)
            _acc(dwq_ref.at[h, 128:160, :], dwq[h][1], first)
            _acc(dwq_ref.at[h, 160:192, :], dwq[h][2], first)
            _acc(dwkv_ref.at[h, 0:128, :], dwkv[h][0], first)
            _acc(dwkv_ref.at[h, 128:256, :], dwkv[h][1], first)

    hq = pl.BlockSpec((HEADS, tm, QK), lambda i: (0, i, 0))
    return pl.pallas_call(
        body, grid=(T // tm,),
        in_specs=_qkv_in_specs() + [hq, hq, pl.BlockSpec((HEADS, tm, 128), lambda i: (0, i, 0)), ANY],
        out_specs=[pl.BlockSpec((tm, 768), lambda i: (i, OFF_CQ // 768)), pl.BlockSpec((1, 384), lambda i: (0, 0)),
                   pl.BlockSpec((1, 384), lambda i: (0, 0)), pl.BlockSpec((HEADS, 192, 384), lambda i: (0, 0, 0)),
                   pl.BlockSpec((HEADS, 256, 256), lambda i: (0, 0, 0))],
        out_shape=[jax.ShapeDtypeStruct((T, NPAD), BF16), jax.ShapeDtypeStruct((1, 384), F32),
                   jax.ShapeDtypeStruct((1, 384), F32), jax.ShapeDtypeStruct((HEADS, 192, 384), F32),
                   jax.ShapeDtypeStruct((HEADS, 256, 256), F32)],
        input_output_aliases={11: 0},
        name=f"qkv_bwd_l{l}", compiler_params=_params(("arbitrary",)))(
            proj, proj, qg, kvg, wq, wkv, ctab, stab, dq, dk, dv, dproj)


ATT_TQ_FWD = 256
ATT_TQ_BWD = 512


def _attn_tile(q, kv_past, k, v, zb):
    q = q * (1.0 / math.sqrt(QK))
    s = dot_nt(q, k)
    qc = lax.broadcasted_iota(jnp.int32, s.shape, 0) >> CHUNK_SHIFT
    kc = lax.broadcasted_iota(jnp.int32, s.shape, 1) >> CHUNK_SHIFT
    s = jnp.where(kc <= qc, s, -1e30)
    m = jnp.max(s, axis=-1, keepdims=True)
    if kv_past is not None:
        sp = dot_nt(q, kv_past[0])
        m = jnp.maximum(m, jnp.max(sp, axis=-1, keepdims=True))
    m = lax.stop_gradient(m)
    p = jnp.exp(s - m)
    denom = jnp.sum(p, axis=-1, keepdims=True)
    o = dot_nn(p, v)
    if kv_past is not None:
        pp = jnp.exp(sp - m)
        denom = denom + jnp.sum(pp, axis=-1, keepdims=True)
        o = o + dot_nn(pp, kv_past[1])
    return o * (1.0 / denom) * _silu(zb)


def _attn_operands(k_ref, v_ref, g, tq):
    n = tq * g
    past = (k_ref[0:n, :], v_ref[0:n, :]) if g else None
    return past, k_ref[n:n + tq, :], v_ref[n:n + tq, :]


def _attn_in_specs(tq):
    return [pl.BlockSpec((None, tq, QK), lambda h, i: (h, i, 0)), pl.BlockSpec((None, T, QK), lambda h, i: (h, 0, 0)),
            pl.BlockSpec((None, T, 128), lambda h, i: (h, 0, 0)),
            pl.BlockSpec((tq, 128), lambda h, i: (i, OFF_ZB // 128 + h))]


def attn_fwd(q, k, v, proj, l):
    tq = ATT_TQ_FWD

    def body(q_ref, k_ref, v_ref, z_ref, y_ref):
        for g in range(T // tq):
            @pl.when(pl.program_id(1) == g)
            def _(g=g):
                past, k, v = _attn_operands(k_ref, v_ref, g, tq)
                y_ref[...] = _attn_tile(q_ref[...], past, k, v, z_ref[...])

    return pl.pallas_call(
        body, grid=(HEADS, T // tq), in_specs=_attn_in_specs(tq),
        out_specs=pl.BlockSpec((tq, 128), lambda h, i: (i, h)),
        out_shape=jax.ShapeDtypeStruct((T, 1024), F32),
        name=f"attn_fwd_l{l}", compiler_params=_params(("arbitrary", "arbitrary")))(q, k, v, proj)


def attn_bwd(q, k, v, proj, dy, dproj, l):
    tq = ATT_TQ_BWD

    def body(q_ref, k_ref, v_ref, z_ref, dy_ref, _, dq_ref, dk_ref, dv_ref, dz_ref):
        @pl.when(pl.program_id(1) == 0)
        def _():
            dk_ref[...] = jnp.zeros_like(dk_ref)
            dv_ref[...] = jnp.zeros_like(dv_ref)

        for g in range(T // tq):
            @pl.when(pl.program_id(1) == g)
            def _(g=g):
                n = tq * g
                past, k, v = _attn_operands(k_ref, v_ref, g, tq)
                _, vjp = jax.vjp(_attn_tile, q_ref[...], past, k, v, z_ref[...])
                dq, dpast, dk, dv, dz = vjp(dy_ref[...])
                dq_ref[...] = dq
                dz_ref[...] = dz.astype(BF16)
                dk_ref[n:n + tq, :] += dk
                dv_ref[n:n + tq, :] += dv
                if g:
                    dk_ref[0:n, :] += dpast[0]
                    dv_ref[0:n, :] += dpast[1]

    return pl.pallas_call(
        body, grid=(HEADS, T // tq),
        in_specs=_attn_in_specs(tq) + [pl.BlockSpec((tq, 128), lambda h, i: (i, h)), ANY],
        out_specs=[pl.BlockSpec((None, tq, QK), lambda h, i: (h, i, 0)), pl.BlockSpec((None, T, QK), lambda h, i: (h, 0, 0)),
                   pl.BlockSpec((None, T, 128), lambda h, i: (h, 0, 0)),
                   pl.BlockSpec((tq, 128), lambda h, i: (i, OFF_ZB // 128 + h))],
        out_shape=[jax.ShapeDtypeStruct((HEADS, T, QK), F32), jax.ShapeDtypeStruct((HEADS, T, QK), F32),
                   jax.ShapeDtypeStruct((HEADS, T, 128), F32), jax.ShapeDtypeStruct((T, NPAD), BF16)],
        input_output_aliases={5: 3},
        name=f"attn_bwd_l{l}", compiler_params=_params(("arbitrary", "arbitrary")))(q, k, v, proj, dy, dproj)


LRU_TT = 512


def _lru_gates(xc, wa, wx, ba, bx, lam):
    r = _sigmoid(dot_nn(xc, wa) + ba)
    i = _sigmoid(dot_nn(xc, wx) + bx)
    sp = jnp.maximum(-lam, 0.0) + jnp.log1p(jnp.exp(-jnp.abs(lam)))
    log_a = -8.0 * r * sp
    a = jnp.exp(log_a)
    mult = jnp.sqrt(jnp.maximum(1.0 - jnp.exp(2.0 * log_a), 0.0))
    return a, mult * (i * xc)


def _shift_down(x, s, halo):
    n, c = x.shape
    r = pltpu.roll(x.reshape(n // 8, 8, c), s, 1)
    before = jnp.concatenate([pltpu.roll(halo, s, 0)[None], r[:-1]], axis=0)
    sub = lax.broadcasted_iota(jnp.int32, r.shape, 1)
    return jnp.where(sub >= s, r, before).reshape(n, c)


def _shift_up(x, s, halo):
    n, c = x.shape
    r = pltpu.roll(x.reshape(n // 8, 8, c), 8 - s, 1)
    after = jnp.concatenate([r[1:], pltpu.roll(halo, 8 - s, 0)[None]], axis=0)
    sub = lax.broadcasted_iota(jnp.int32, r.shape, 1)
    return jnp.where(sub < 8 - s, r, after).reshape(n, c)


def _conv(x, halo, w_ref, b):
    return (w_ref[3:4, :] * x + w_ref[2:3, :] * _shift_down(x, 1, halo) + w_ref[1:2, :] * _shift_down(x, 2, halo)
            + w_ref[0:1, :] * _shift_down(x, 3, halo) + b)


def _scan(a, b, reverse, carry):
    n, c = a.shape
    a, b = a.reshape(n // 8, 8, c), b.reshape(n // 8, 8, c)
    sub = lax.broadcasted_iota(jnp.int32, a.shape, 1)
    for d in (1, 2, 4):
        keep = sub < 8 - d if reverse else sub >= d
        shift = 8 - d if reverse else d
        a_sh = jnp.where(keep, pltpu.roll(a, shift, 1), 1.0)
        b_sh = jnp.where(keep, pltpu.roll(b, shift, 1), 0.0)
        b = a * b_sh + b
        a = a * a_sh
    a, b = a.reshape(n, c), b.reshape(n, c)
    groups = [None] * (n // 8)
    for g in (reversed(range(n // 8)) if reverse else range(n // 8)):
        h = a[8 * g:8 * g + 8] * carry + b[8 * g:8 * g + 8]
        groups[g] = h
        carry = h[0:1] if reverse else h[7:8]
    return jnp.concatenate(groups, axis=0), carry


def _lru_param_specs(l):
    ct = LRU_TILE
    vec = pl.BlockSpec((1, ct), lambda n, i: (0, n))
    mat = pl.BlockSpec((None, 8, 80, 80), lambda n, i: (l, n, 0, 0))
    return [pl.BlockSpec((4, ct), lambda n, i: (0, n)), vec, mat, mat, vec, vec, vec]


def _blocks_to_dense(w_ref, dense):
    dense[...] = jnp.zeros_like(dense)
    for b in range(8):
        dense[80 * b:80 * b + 80, 80 * b:80 * b + 80] = w_ref[b]


def _dense_to_blocks(dense, w_ref):
    for b in range(8):
        w_ref[b] = dense[80 * b:80 * b + 80, 80 * b:80 * b + 80]


def lru_fwd(proj, conv_w, conv_b, wa, wx, ba, bx, lam, l):
    tt, ct = LRU_TT, LRU_TILE

    def body(x_ref, z_ref, cw_ref, cb_ref, wa_ref, wx_ref, ba_ref, bx_ref, lam_ref, h_ref, y_ref, halo, hcar, wa, wx):
        @pl.when(pl.program_id(1) == 0)
        def _():
            halo[...] = jnp.zeros_like(halo)
            hcar[...] = jnp.zeros_like(hcar)
            _blocks_to_dense(wa_ref, wa)
            _blocks_to_dense(wx_ref, wx)

        x = x_ref[...]
        xc = _conv(x, halo[...], cw_ref, cb_ref[...])
        halo[...] = x[tt - 8:tt]
        a, b = _lru_gates(xc, wa[...], wx[...], ba_ref[...], bx_ref[...], lam_ref[...])
        h, hcar[...] = _scan(a, b, False, hcar[...])
        h_ref[...] = h
        y_ref[...] = h * _silu(z_ref[...])

    seq = pl.BlockSpec((tt, ct), lambda n, i: (i, n))
    return pl.pallas_call(
        body, grid=(LRU_W // ct, T // tt),
        in_specs=[pl.BlockSpec((tt, ct), lambda n, i: (i, OFF_XC // ct + n)),
                  pl.BlockSpec((tt, ct), lambda n, i: (i, OFF_ZC // ct + n))] + _lru_param_specs(l),
        out_specs=[seq, seq],
        out_shape=[jax.ShapeDtypeStruct((T, LRU_W), F32), jax.ShapeDtypeStruct((T, LRU_W), F32)],
        scratch_shapes=[pltpu.VMEM((8, ct), F32), pltpu.VMEM((1, ct), F32), pltpu.VMEM((ct, ct), F32),
                        pltpu.VMEM((ct, ct), F32)],
        name=f"lru_fwd_l{l}", compiler_params=_params(("arbitrary", "arbitrary")))(
            proj, proj, conv_w, conv_b, wa, wx, ba, bx, lam)


def lru_bwd(proj, hseq, dy, conv_w, conv_b, wa, wx, ba, bx, lam, dproj, l):
    tt, ct = LRU_TT, LRU_TILE
    nt = T // tt
    rev = lambda i: nt - 1 - i
    prev8 = lambda i: jnp.maximum(rev(i) * (tt // 8) - 1, 0)

    def body(x_ref, xh_ref, z_ref, h_ref, hh_ref, dy_ref, cw_ref, cb_ref, wa_ref, wx_ref, ba_ref, bx_ref, lam_ref, _,
             dx_ref, dcw_ref, dcb_ref, dwa_ref, dwx_ref, dba_ref, dbx_ref, dlam_ref, gcar, dhalo,
             wa, wx, dwa_acc, dwx_acc):
        i = pl.program_id(1)
        first = i == 0

        @pl.when(first)
        def _():
            gcar[...] = jnp.zeros_like(gcar)
            dhalo[...] = jnp.zeros_like(dhalo)
            _blocks_to_dense(wa_ref, wa)
            _blocks_to_dense(wx_ref, wx)

        at_start = rev(i) == 0
        x = x_ref[...]
        xhalo = jnp.where(at_start, 0.0, xh_ref[...])
        sh = [x, _shift_down(x, 1, xhalo), _shift_down(x, 2, xhalo), _shift_down(x, 3, xhalo)]
        xc = (cw_ref[3:4, :] * sh[0] + cw_ref[2:3, :] * sh[1] + cw_ref[1:2, :] * sh[2] + cw_ref[0:1, :] * sh[3]
              + cb_ref[...])
        (a, b), vjp = jax.vjp(_lru_gates, xc, wa[...], wx[...], ba_ref[...], bx_ref[...], lam_ref[...])
        hs = h_ref[...]
        hprev = _shift_down(hs, 1, jnp.where(at_start, 0.0, hh_ref[...]))
        dh = dy_ref[...] * _silu(z_ref[...])
        a_next = _shift_up(a, 1, jnp.ones((8, ct), F32))
        g, _ = _scan(a_next, dh, True, gcar[...])
        dxc, dwa, dwx, dba, dbx, dlam = vjp((g * hprev, g))
        dx = (cw_ref[3:4, :] * dxc + cw_ref[2:3, :] * _shift_up(dxc, 1, dhalo[...])
              + cw_ref[1:2, :] * _shift_up(dxc, 2, dhalo[...]) + cw_ref[0:1, :] * _shift_up(dxc, 3, dhalo[...]))
        dx_ref[...] = dx.astype(BF16)
        dhalo[...] = dxc[0:8]
        ag = a * g
        gcar[...] = ag[0:1]
        dcw = jnp.concatenate([jnp.sum(dxc * sh[3 - j], axis=0, keepdims=True) for j in range(4)], axis=0)
        _acc(dcw_ref, dcw, first)
        _acc(dcb_ref, jnp.sum(dxc, axis=0, keepdims=True), first)
        _acc(dwa_acc, dwa, first)
        _acc(dwx_acc, dwx, first)

        @pl.when(i == nt - 1)
        def _():
            _dense_to_blocks(dwa_acc, dwa_ref)
            _dense_to_blocks(dwx_acc, dwx_ref)

        _acc(dba_ref, dba, first)
        _acc(dbx_ref, dbx, first)
        _acc(dlam_ref, dlam, first)

    xcol = OFF_XC // ct
    zcol = OFF_ZC // ct
    vec = pl.BlockSpec((1, ct), lambda n, i: (0, n))
    mat = pl.BlockSpec((8, 80, 80), lambda n, i: (n, 0, 0))
    seq = pl.BlockSpec((tt, ct), lambda n, i: (rev(i), n))
    return pl.pallas_call(
        body, grid=(LRU_W // ct, nt),
        in_specs=[pl.BlockSpec((tt, ct), lambda n, i: (rev(i), xcol + n)),
                  pl.BlockSpec((8, ct), lambda n, i: (prev8(i), xcol + n)),
                  pl.BlockSpec((tt, ct), lambda n, i: (rev(i), zcol + n)),
                  seq, pl.BlockSpec((8, ct), lambda n, i: (prev8(i), n)), seq] + _lru_param_specs(l) + [ANY],
        out_specs=[pl.BlockSpec((tt, ct), lambda n, i: (rev(i), xcol + n)),
                   pl.BlockSpec((4, ct), lambda n, i: (0, n)), vec, mat, mat, vec, vec, vec],
        out_shape=[jax.ShapeDtypeStruct((T, NPAD), BF16),
                   jax.ShapeDtypeStruct((4, LRU_W), F32), jax.ShapeDtypeStruct((1, LRU_W), F32),
                   jax.ShapeDtypeStruct((16, 80, 80), F32), jax.ShapeDtypeStruct((16, 80, 80), F32),
                   jax.ShapeDtypeStruct((1, LRU_W), F32), jax.ShapeDtypeStruct((1, LRU_W), F32),
                   jax.ShapeDtypeStruct((1, LRU_W), F32)],
        scratch_shapes=[pltpu.VMEM((1, ct), F32), pltpu.VMEM((8, ct), F32)] + [pltpu.VMEM((ct, ct), F32)] * 4,
        input_output_aliases={13: 0},
        name=f"lru_bwd_l{l}", compiler_params=_params(("arbitrary", "arbitrary")))(
            proj, proj, proj, hseq, hseq, dy, conv_w, conv_b, wa, wx, ba, bx, lam, dproj)


def proj_bwd(y, dp, w, l, tag, dep=None, dproj=None, gate=None):
    tm = 512
    k = y.shape[1]
    extra = [] if dep is None else [dep]
    in_specs = [pl.BlockSpec((tm, k), lambda i: (i, 0)), pl.BlockSpec((tm, D), lambda i: (i, 0)),
                pl.BlockSpec((None, k, D // 2), lambda i: (0, 0, 0))]
    out_specs = [pl.BlockSpec((tm, k), lambda i: (i, 0)), pl.BlockSpec((None, k, D), lambda i: (0, 0, 0))]
    out_shape = [jax.ShapeDtypeStruct((T, k), F32), jax.ShapeDtypeStruct((1, k, D), F32)]
    aliases = {}
    if gate is not None:
        in_specs += [pl.BlockSpec((tm, k), lambda i: (i, 0)), pl.BlockSpec((tm, k), lambda i: (i, OFF_ZC // k))]
        extra = list(gate) + extra
    if dproj is not None:
        width = k if gate is not None else PAD2
        at = OFF_ZC if gate is not None else OFF_XC - PAD2
        aliases = {3 + len(extra): 2}
        extra = extra + [dproj]
        out_specs.append(pl.BlockSpec((tm, width), lambda i: (i, at // width)))
        out_shape.append(jax.ShapeDtypeStruct((T, NPAD), BF16))
    in_specs += [ANY] * (3 + len(extra) - len(in_specs))

    def body(y_ref, dp_ref, w_ref, *rest):
        dy_ref, dw_ref = rest[len(extra):len(extra) + 2]
        dp = dp_ref[...]
        dy = _dg(dp, _unpack(w_ref[...]), _NT)
        dy_ref[...] = dy
        _acc(dw_ref, _dg(y_ref[...], dp, _TN), pl.program_id(0) == 0)
        if gate is not None:
            z = rest[1][...]
            sg = _sigmoid(z)
            rest[len(extra) + 2][...] = (dy * rest[0][...] * (sg * (1.0 + z * (1.0 - sg)))).astype(BF16)
        elif dproj is not None:
            rest[len(extra) + 2][...] = jnp.zeros((tm, PAD2), BF16)

    return pl.pallas_call(
        body, grid=(T // tm,), in_specs=in_specs, out_specs=out_specs, out_shape=out_shape,
        input_output_aliases=aliases,
        name=f"proj_{tag}_bwd_l{l}", compiler_params=_params(("arbitrary",)))(y, dp, w, *extra)


OUT_TM = 256


def _out_tile(pa, pb, pc, ga, gb, gc, wout, post_g):
    merged = _sigmoid(ga) * pa + _sigmoid(gb) * pb + _sigmoid(gc) * pc
    return _rms(dot_nn(merged, wout), post_g)


def _out_in_specs(tm=OUT_TM):
    tok = pl.BlockSpec((tm, D), lambda i: (i, 0))
    gate = lambda off: pl.BlockSpec((tm, 512), lambda i, off=off: (i, off // 512))
    return [tok, tok, tok, gate(OFF_GA), gate(OFF_GA + 512), gate(OFF_GB), gate(OFF_GB + 512), gate(OFF_GC),
            gate(OFF_GC + 512), pl.BlockSpec((None, D, D // 2), lambda i: (0, 0, 0)), pl.BlockSpec((1, D), lambda i: (0, 0))]


def _gates(refs):
    return [jnp.concatenate([refs[2 * j][...], refs[2 * j + 1][...]], axis=1) for j in range(3)]


def out_fwd(x, ya, yb, yc, proj, wpa, wpb, wpc, wout, post_g, l):
    tm = 2 * OUT_TM

    def body(ya_ref, yb_ref, yc_ref, g0, g1, g2, g3, g4, g5, wo_ref, pg_ref, x_ref, wa_ref, wb_ref, wc_ref,
             o_ref, pa_ref, pb_ref, pc_ref, wa, wb, wc, wo):
        @pl.when(pl.program_id(0) == 0)
        def _():
            for dst, src in ((wa, wa_ref), (wb, wb_ref), (wc, wc_ref), (wo, wo_ref)):
                dst[...] = _unpack(src[...]).astype(BF16)

        pa = _dg(ya_ref[...], wa[...], _NN)
        pb = _dg(yb_ref[...], wb[...], _NN)
        pc = _dg(yc_ref[...], wc[...], _NN)
        ga, gb, gc = _gates([g0, g1, g2, g3, g4, g5])
        o_ref[...] = x_ref[...] + _out_tile(pa, pb, pc, ga, gb, gc, wo[...], pg_ref[...])
        pa_ref[...] = pa.astype(BF16)
        pb_ref[...] = pb.astype(BF16)
        pc_ref[...] = pc.astype(BF16)

    tok = pl.BlockSpec((tm, D), lambda i: (i, 0))
    words = lambda k: pl.BlockSpec((None, k, D // 2), lambda i: (0, 0, 0), pipeline_mode=pl.Buffered(1))
    specs = _out_in_specs(tm)
    specs[2] = pl.BlockSpec((tm, LRU_W), lambda i: (i, 0))
    specs[9] = words(D)
    return pl.pallas_call(
        body, grid=(T // tm,), in_specs=specs + [tok, words(D), words(D), words(LRU_W)], out_specs=[tok] * 4,
        out_shape=[jax.ShapeDtypeStruct((T, D), F32)] + [jax.ShapeDtypeStruct((T, D), BF16)] * 3,
        scratch_shapes=[pltpu.VMEM((D, D), BF16), pltpu.VMEM((D, D), BF16), pltpu.VMEM((LRU_W, D), BF16),
                        pltpu.VMEM((D, D), BF16)],
        name=f"out_fwd_l{l}", compiler_params=_params(("arbitrary",)))(
            ya, yb, yc, proj, proj, proj, proj, proj, proj, wout, post_g, x, wpa, wpb, wpc)


def out_bwd(pa, pb, pc, proj, wout, post_g, dxn, l, dep=None):
    tm = OUT_TM
    nsteps = T // tm

    def body(pa_ref, pb_ref, pc_ref, g0, g1, g2, g3, g4, g5, w_ref, pg_ref, dxn_ref, *rest):
        dpa_ref, dpb_ref, dpc_ref, dproj_ref, dw_ref, dpg_ref, gbuf, sem = rest[-8:]
        i = pl.program_id(0)
        first = i == 0
        slot = i % 2
        ga, gb, gc = _gates([g0, g1, g2, g3, g4, g5])
        _, vjp = jax.vjp(_out_tile, pa_ref[...], pb_ref[...], pc_ref[...], ga, gb, gc, _unpack(w_ref[...]), pg_ref[...])
        dpa, dpb, dpc, dga, dgb, dgc, dw, dpg = vjp(dxn_ref[...])
        dpa_ref[...] = dpa.astype(BF16)
        dpb_ref[...] = dpb.astype(BF16)
        dpc_ref[...] = dpc.astype(BF16)
        _acc(dw_ref, dw, first)
        _acc(dpg_ref, dpg, first)

        def writeback(step, s):
            rows = pl.ds(pl.multiple_of(step * tm, tm), tm)
            return pltpu.make_async_copy(gbuf.at[s], dproj_ref.at[rows, pl.ds(OFF_GA, 3072)], sem.at[s])

        gbuf[slot, :, 0:1024] = dga.astype(BF16)
        gbuf[slot, :, 1024:2048] = dgb.astype(BF16)
        gbuf[slot, :, 2048:3072] = dgc.astype(BF16)
        writeback(i, slot).start()

        @pl.when(i > 0)
        def _():
            writeback(i - 1, 1 - slot).wait()

        @pl.when(i == nsteps - 1)
        def _():
            writeback(i, slot).wait()

    tok = pl.BlockSpec((tm, D), lambda i: (i, 0))
    deps = [] if dep is None else [dep]
    return pl.pallas_call(
        body, grid=(nsteps,), in_specs=_out_in_specs() + [tok] + [ANY] * len(deps),
        out_specs=[tok, tok, tok, ANY, pl.BlockSpec((None, D, D), lambda i: (0, 0, 0)), pl.BlockSpec((1, D), lambda i: (0, 0))],
        out_shape=[jax.ShapeDtypeStruct((T, D), BF16)] * 3 + [jax.ShapeDtypeStruct((T, NPAD), BF16),
                                                            jax.ShapeDtypeStruct((1, D, D), F32), jax.ShapeDtypeStruct((1, D), F32)],
        scratch_shapes=[pltpu.VMEM((2, tm, 3072), BF16), pltpu.SemaphoreType.DMA((2,))],
        name=f"out_bwd_l{l}", compiler_params=_params(("arbitrary",)))(
            pa, pb, pc, proj, proj, proj, proj, proj, proj, wout, post_g, dxn, *deps)


def loss_head(y, target):
    tm = 256

    def body(y_ref, t_ref, loss_ref, dy_ref):
        e = y_ref[...] - t_ref[...]
        dy_ref[...] = e * (1.0 / D)
        val = 0.5 * jnp.sum(jnp.mean(e * e, axis=-1, keepdims=True), axis=0, keepdims=True)
        _acc(loss_ref, jnp.broadcast_to(val, (8, 128)), pl.program_id(0) == 0)

    tok = pl.BlockSpec((tm, D), lambda i: (i, 0))
    total, dy = pl.pallas_call(
        body, grid=(T // tm,), in_specs=[tok, tok],
        out_specs=[pl.BlockSpec((8, 128), lambda i: (0, 0)), tok],
        out_shape=[jax.ShapeDtypeStruct((8, 128), F32), jax.ShapeDtypeStruct((T, D), F32)],
        name="loss_head", compiler_params=_params(("arbitrary",)))(y, target)
    return total[0, 0], dy


def _rope_tables():
    pos = jnp.arange(T, dtype=F32)
    inv_freq = 10000.0 ** (-jnp.arange(0, 64, 2, dtype=F32) / 64)
    ang = pos[:, None] * inv_freq[None, :]
    cos, sin = jnp.cos(ang), jnp.sin(ang)
    ctab = jnp.concatenate([jnp.ones((T, 128), F32), cos, cos], axis=1)
    stab = jnp.concatenate([jnp.zeros((T, 128), F32), -sin, sin], axis=1)
    return ctab, stab


def _layer_fwd(x, l, w, gw, tabs, dep=None, mid=None):
    row = lambda a: a[l][None]
    proj, h = inproj_fwd(x, row(w["pre_norm_g"]), gw["w_in_t"], l, dep)
    ya = gmlp_fwd(proj, row(w["gm_ln_g"]), row(w["gm_ln_b"]), w["gm_ws"][l], w["gm_bs"][l][..., None], l)
    dep2 = None
    if mid is not None:
        gw, dep2 = mid(ya)
    q, k, v = qkv_fwd(proj, row(w["mla_q_norm_g"]), row(w["kv_g384"]), gw["wq"], gw["wkv"], tabs[0], tabs[1], l, dep2)
    yb = attn_fwd(q, k, v, proj, l)
    hseq, yc = lru_fwd(proj, gw["conv"], row(w["lru_conv_b"]), w["lru_w_a"], w["lru_w_x"],
                       row(w["lru_b_a"]), row(w["lru_b_x"]), row(w["lru_lambda"]), l)
    xn, pa, pb, pc = out_fwd(x, ya, yb, yc, proj, gw["w_proj_a"], gw["w_proj_b"], gw["w_proj_c"], gw["w_out"],
                             row(w["post_norm_g"]), l)
    return xn, (x, proj, h, ya, q, k, v, yb, hseq, yc, pa, pb, pc)


def _layer_bwd(dxn, l, w, gw, tabs, saved, dep=None, early=None, mid=None, late=None):
    x, proj, h, ya, q, k, v, yb, hseq, yc, pa, pb, pc = saved
    row = lambda a: a[l][None]
    g, gg = {}, {}
    dpa, dpb, dpc, dproj, gg["w_out"], dpost = out_bwd(pa, pb, pc, proj, gw["w_out"], row(w["post_norm_g"]), dxn, l, dep)
    g["post_norm_g"] = dpost[0]
    dep1 = early(dpa) if early is not None else None
    dya, gg["w_proj_a"], dproj = proj_bwd(ya, dpa, gw["w_proj_a"], l, "a", dep1, dproj)
    dyb, gg["w_proj_b"] = proj_bwd(yb, dpb, gw["w_proj_b"], l, "b")
    dyc, gg["w_proj_c"], dproj = proj_bwd(yc, dpc, gw["w_proj_c"], l, "c", None, dproj, (hseq, proj))
    dproj, dln_g, dln_b, g["gm_ws"], dbs = gmlp_bwd(proj, row(w["gm_ln_g"]), row(w["gm_ln_b"]), w["gm_ws"][l],
                                                   w["gm_bs"][l][..., None], dya, dproj, l)
    g["gm_ln_g"], g["gm_ln_b"], g["gm_bs"] = dln_g[0], dln_b[0], dbs[..., 0]
    dq, dk, dv, dproj = attn_bwd(q, k, v, proj, dyb, dproj, l)
    dproj, dqg, dkvg, dwq, dwkv = qkv_bwd(proj, row(w["mla_q_norm_g"]), row(w["kv_g384"]), gw["wq"], gw["wkv"],
                                          tabs[0], tabs[1], dq, dk, dv, dproj, l)
    gg["wq"], gg["wkv"] = dwq.reshape(1, 1536, 384), dwkv.reshape(1, 2048, 256)
    g["mla_q_norm_g"], g["mla_kv_norm_g"] = dqg[0], dkvg[0, :256]
    dproj, dcw, dcb, dwa, dwx, dba, dbx, dlam = lru_bwd(
        proj, hseq, dyc, gw["conv"], row(w["lru_conv_b"]), w["lru_w_a"], w["lru_w_x"],
        row(w["lru_b_a"]), row(w["lru_b_x"]), row(w["lru_lambda"]), dproj, l)
    gg["conv"] = jnp.pad(dcw.T, ((0, 0), (0, 124)))[None]
    g["lru_conv_b"], g["lru_b_a"], g["lru_b_x"], g["lru_lambda"] = dcb[0], dba[0], dbx[0], dlam[0]
    g["lru_w_a"], g["lru_w_x"] = dwa, dwx
    dep2 = mid(gg, dproj) if mid is not None else None
    gg["w_in_t"], dh = inproj_bwd(dproj, h, gw["w_in_t"], l, dep2)
    dep3 = late(gg["w_in_t"]) if late is not None else None
    dx, dpre = prenorm_bwd(x, row(w["pre_norm_g"]), dh, dxn, l, dep3)
    g["pre_norm_g"] = dpre[0]
    return dx, gg, g


MESH = pl.DeviceIdType.MESH
HBM = pl.BlockSpec(memory_space=pltpu.HBM)
SEM = pl.BlockSpec(memory_space=pltpu.SEMAPHORE)
EFFECT = pltpu.SideEffectType.DATAFLOW_SIDE_EFFECTING
FLIPS = ((1, 0), (0, 1), (1, 1))


def _win_off(k, s):
    g = SHARD * k + s
    return g + jnp.where(g >= PAD1_AT, PAD1, 0) + jnp.where(g >= PAD2_AT, PAD2, 0)


def _plain_off(rows):
    return lambda k, s: rows * k + s


class Spec:
    def __init__(self, rows, cols, full_rows, pieces=None, off=None, layers=1, packed=None):
        self.rows, self.cols, self.full_rows, self.layers = rows, cols, full_rows, layers
        self.pieces = pieces or ((0, rows),)
        self.off = off or _plain_off(rows)
        self.packed = cols % 256 == 0 if packed is None else packed
        self.wcols = cols // 2 if self.packed else cols

    def to_words(self, a):
        return _pack(a) if self.packed else a

    def from_words(self, p):
        return _unpack(p) if self.packed else p


def _pack(a):
    def bits(v):
        u = lax.bitcast_convert_type(v, jnp.uint32)
        return u + jnp.uint32(0x7FFF) + ((u >> 16) & jnp.uint32(1))

    words = [(bits(a[:, g:g + 128]) >> 16) | (bits(a[:, g + 128:g + 256]) & jnp.uint32(0xFFFF0000))
             for g in range(0, a.shape[-1], 256)]
    return lax.bitcast_convert_type(jnp.concatenate(words, axis=-1) if len(words) > 1 else words[0], F32)


def _unpack(p):
    w = lax.bitcast_convert_type(p, jnp.uint32)
    lo = lax.bitcast_convert_type(w << 16, F32)
    hi = lax.bitcast_convert_type(w & jnp.uint32(0xFFFF0000), F32)
    return jnp.concatenate([h[:, g:g + 128] for g in range(0, p.shape[-1], 128) for h in (lo, hi)], axis=-1)


WEIGHT_SPECS = {
    "w_in_t": Spec(SHARD, D, NPAD, WIN_PIECES, _win_off),
    "wq": Spec(192, 384, 1536),
    "wkv": Spec(256, 256, 2048),
    "conv": Spec(160, 128, 1280),
    "w_proj_a": Spec(128, D, 1024),
    "w_proj_b": Spec(128, D, 1024),
    "w_proj_c": Spec(160, D, 1280),
    "w_out": Spec(128, D, 1024),
}
REP_ROWS = 72
REP_SPEC = Spec(REP_ROWS, D, REP_ROWS * NDEV, packed=False)


def _coords():
    return lax.axis_index("x"), lax.axis_index("y"), lax.axis_index("c")


def _rows(ref, start, n):
    if not isinstance(start, int):
        start = pl.multiple_of(start, 8)
    return ref.at[:, pl.ds(start, n), :]


def _col_tile(cols):
    return 256 if cols % 256 == 0 else cols


def _n_pieces(specs):
    return sum(len(sp.pieces) for sp in specs)


def pack_place(shard, sp, layer, tag, dep=None):
    gaps = ((PAD1_AT, PAD1), (PAD2_AT + PAD1, PAD2)) if sp.off is _win_off else ()
    npc = len(sp.pieces)
    deps = [] if dep is None else [dep]

    def body(s_ref, *rest):
        words_ref, full_ref, buf, zbuf, sem = rest[-5:]
        l = 0
        x, y, c = _coords()
        me = 4 * x + 2 * y + c
        words = sp.to_words(s_ref[...])
        words_ref[...] = words
        buf[...] = words
        copies = [pltpu.make_async_copy(buf.at[pl.ds(s, n), :],
                                        full_ref.at[l, pl.ds(pl.multiple_of(sp.off(me, s), 8), n), :], sem.at[i])
                  for i, (s, n) in enumerate(sp.pieces)]
        if gaps:
            zbuf[...] = jnp.zeros_like(zbuf)
            copies += [pltpu.make_async_copy(zbuf.at[pl.ds(0, n), :], full_ref.at[l, pl.ds(at, n), :], sem.at[npc + i])
                       for i, (at, n) in enumerate(gaps)]
        for cp in copies:
            cp.start()
        for cp in copies:
            cp.wait()

    return pl.pallas_call(
        body, grid=(1,), in_specs=[pl.BlockSpec((None, sp.rows, sp.cols), lambda i: (layer, 0, 0))] + [ANY] * len(deps),
        out_specs=[pl.BlockSpec((None, sp.rows, sp.wcols), lambda i: (0, 0, 0)), ANY],
        out_shape=[jax.ShapeDtypeStruct((sp.layers, sp.rows, sp.wcols), F32),
                   jax.ShapeDtypeStruct((sp.layers, sp.full_rows, sp.wcols), F32)],
        scratch_shapes=[pltpu.VMEM((sp.rows, sp.wcols), F32), pltpu.VMEM((PAD2 if gaps else 8, sp.wcols), F32),
                        pltpu.SemaphoreType.DMA((npc + len(gaps),))],
        name=f"pack_place_{tag}", compiler_params=_params(("arbitrary",)))(shard, *deps)


def _gather_copies(srcs, bufs, specs, ssem, rsem, landing):
    x, y, c = _coords()
    me = 4 * x + 2 * y + c
    targets = [(x, y, 1 - c)] + [(x ^ fx, y ^ fy, c) for fx, fy in FLIPS]
    copies = []
    p = 0
    for src, buf, sp in zip(srcs, bufs, specs):
        for s, n in sp.pieces:
            for t, (tx, ty, tc) in enumerate(targets):
                owner = 4 * tx + 2 * ty + tc if landing else me
                copies.append(pltpu.make_async_remote_copy(_rows(src, s, n), _rows(buf, sp.off(owner, s), n),
                                                           ssem.at[4 * p + t], rsem.at[4 * p + t],
                                                           device_id=(tx, ty, tc), device_id_type=MESH))
            p += 1
    return copies


def gather_send(words, fulls, specs, tag):
    ns, npc = len(specs), _n_pieces(specs)

    def body(*refs):
        srcs, bufs, sems = refs[:ns], refs[2 * ns:3 * ns], refs[3 * ns:]
        for cp in _gather_copies(srcs, bufs, specs, *sems, False):
            cp.start()
        for cp in _gather_copies(srcs, bufs, specs, *sems, False):
            cp.wait_send()
        for cp in _gather_copies(srcs, bufs, specs, *sems, True):
            cp.wait_recv()

    return pl.pallas_call(
        body, in_specs=[ANY] * (2 * ns), out_specs=[ANY] * ns,
        out_shape=[jax.ShapeDtypeStruct(f.shape, f.dtype) for f in fulls],
        input_output_aliases={ns + i: i for i in range(ns)},
        scratch_shapes=[pltpu.SemaphoreType.DMA((4 * npc,)), pltpu.SemaphoreType.DMA((4 * npc,))],
        name=f"gather_send_{tag}", compiler_params=pltpu.CompilerParams(has_side_effects=True))(*words, *fulls)


def _in_hbm(arrays):
    return [pltpu.with_memory_space_constraint(a, pltpu.HBM) for a in arrays]


def gather_start(words, fulls, specs, dep, tag):
    ns, npc = len(specs), _n_pieces(specs)
    deps = [] if dep is None else [dep]

    def body(*refs):
        ssem, rsem = refs[2 * ns + len(deps):2 * ns + len(deps) + 2]
        for cp in _gather_copies(refs[:ns], refs[ns:2 * ns], specs, ssem, rsem, False):
            cp.start()
        refs[-1][...] = jnp.zeros_like(refs[-1])

    outs = pl.pallas_call(
        body, in_specs=[HBM] * (2 * ns) + [ANY] * len(deps),
        out_specs=[SEM, SEM] + [HBM] * (2 * ns) + [pl.BlockSpec(memory_space=pltpu.VMEM)],
        out_shape=[pltpu.SemaphoreType.DMA((4 * npc,)), pltpu.SemaphoreType.DMA((4 * npc,))]
        + [pltpu.HBM(a.shape, a.dtype) for a in list(words) + list(fulls)] + [jax.ShapeDtypeStruct((8, 128), F32)],
        input_output_aliases={i: 2 + i for i in range(2 * ns)},
        name=f"gather_start_{tag}", compiler_params=pltpu.CompilerParams(has_side_effects=EFFECT))(
            *_in_hbm(list(words) + list(fulls)), *deps)
    return outs[0], outs[1], outs[2:2 + ns], outs[2 + ns:2 + 2 * ns], outs[-1]


def gather_wait(ssem, rsem, words, fulls, specs, after, tag):
    ns = len(specs)

    def body(*refs):
        srcs, bufs, ssem, rsem = refs[:ns], refs[ns:2 * ns], refs[2 * ns], refs[2 * ns + 1]
        for cp in _gather_copies(srcs, bufs, specs, ssem, rsem, False):
            cp.wait_send()
        for cp in _gather_copies(srcs, bufs, specs, ssem, rsem, True):
            cp.wait_recv()

    outs = pl.pallas_call(
        body, in_specs=[HBM] * (2 * ns) + [SEM, SEM, ANY], out_specs=[HBM] * (2 * ns),
        out_shape=[pltpu.HBM(a.shape, a.dtype) for a in list(words) + list(fulls)],
        input_output_aliases={i: i for i in range(2 * ns)},
        name=f"gather_wait_{tag}", compiler_params=pltpu.CompilerParams(has_side_effects=EFFECT))(
            *words, *fulls, ssem, rsem, after)
    return outs[ns:]


def gather_forward(fulls, specs, tag):
    ns, npc = len(specs), _n_pieces(specs)

    def body(*refs):
        bufs = refs[ns:2 * ns]
        ssem, rsem = refs[2 * ns:]
        x, y, c = _coords()
        sibling = (x, y, 1 - c)
        waits = []
        p = 0
        for buf, sp in zip(bufs, specs):
            for s, n in sp.pieces:
                for t, (fx, fy) in enumerate(FLIPS):
                    chip = 4 * (x ^ fx) + 2 * (y ^ fy)
                    here = _rows(buf, sp.off(chip + c, s), n)
                    send = pltpu.make_async_remote_copy(here, here, ssem.at[t, p], rsem.at[t, p],
                                                        device_id=sibling, device_id_type=MESH)
                    send.start()
                    waits.append(send.wait_send)
                    there = _rows(buf, sp.off(chip + 1 - c, s), n)
                    waits.append(pltpu.make_async_remote_copy(here, there, ssem.at[t, p], rsem.at[t, p],
                                                              device_id=sibling, device_id_type=MESH).wait_recv)
                p += 1
        for w in waits:
            w()

    return pl.pallas_call(
        body, in_specs=[ANY] * ns, out_specs=[ANY] * ns,
        out_shape=[jax.ShapeDtypeStruct(f.shape, f.dtype) for f in fulls],
        input_output_aliases={i: i for i in range(ns)},
        scratch_shapes=[pltpu.SemaphoreType.DMA((3, npc)), pltpu.SemaphoreType.DMA((3, npc))],
        name=f"gather_forward_{tag}", compiler_params=pltpu.CompilerParams(has_side_effects=True))(*fulls)


def all_gather(shards, layer, specs, names, tag):
    placed = [pack_place(s, sp, layer, f"{tag}_{n}") for s, sp, n in zip(shards, specs, names)]
    fulls = gather_send([p[0] for p in placed], [p[1] for p in placed], specs, tag)
    return gather_forward(fulls, specs, tag)


def _pair_copies(srcs, theirs, specs, ssem, rsem):
    x, y, c = _coords()
    copies = []
    p = 0
    for src, their, sp in zip(srcs, theirs, specs):
        for s, n in sp.pieces:
            for j in range(4):
                copies.append(pltpu.make_async_remote_copy(_rows(src, sp.off(2 * j + 1 - c, s), n), _rows(their.at[j], s, n),
                                                           ssem.at[4 * p + j], rsem.at[4 * p + j],
                                                           device_id=(x, y, 1 - c), device_id_type=MESH))
            p += 1
    return copies


def _pair_shapes(specs):
    return [(4, sp.layers, sp.rows, sp.cols) for sp in specs]


def reduce_pair(grads, specs, tag, dep=None):
    ns, npc = len(specs), _n_pieces(specs)
    deps = [] if dep is None else [dep]

    def body(*refs):
        copies = _pair_copies(refs[:ns], refs[ns + len(deps):2 * ns + len(deps)], specs, *refs[2 * ns + len(deps):])
        for cp in copies:
            cp.start()
        for cp in copies:
            cp.wait()

    return pl.pallas_call(
        body, in_specs=[ANY] * (ns + len(deps)), out_specs=[ANY] * ns,
        out_shape=[jax.ShapeDtypeStruct(s, F32) for s in _pair_shapes(specs)],
        scratch_shapes=[pltpu.SemaphoreType.DMA((4 * npc,)), pltpu.SemaphoreType.DMA((4 * npc,))],
        name=f"reduce_pair_{tag}", compiler_params=pltpu.CompilerParams(has_side_effects=True))(*grads, *deps)


def pair_start(grads, specs, dep, tag):
    ns, npc = len(specs), _n_pieces(specs)
    slots = [lax.empty(s, F32) for s in _pair_shapes(specs)]
    deps = [] if dep is None else [dep]

    def body(*refs):
        ssem, rsem = refs[2 * ns + len(deps):2 * ns + len(deps) + 2]
        for cp in _pair_copies(refs[:ns], refs[ns:2 * ns], specs, ssem, rsem):
            cp.start()
        refs[-1][...] = jnp.zeros_like(refs[-1])

    outs = pl.pallas_call(
        body, in_specs=[HBM] * (2 * ns) + [ANY] * len(deps),
        out_specs=[SEM, SEM] + [HBM] * (2 * ns) + [pl.BlockSpec(memory_space=pltpu.VMEM)],
        out_shape=[pltpu.SemaphoreType.DMA((4 * npc,)), pltpu.SemaphoreType.DMA((4 * npc,))]
        + [pltpu.HBM(a.shape, a.dtype) for a in list(grads) + slots] + [jax.ShapeDtypeStruct((8, 128), F32)],
        input_output_aliases={i: 2 + i for i in range(2 * ns)},
        name=f"pair_start_{tag}", compiler_params=pltpu.CompilerParams(has_side_effects=EFFECT))(
            *_in_hbm(list(grads) + slots), *deps)
    return outs[0], outs[1], outs[2:2 + ns], outs[2 + ns:2 + 2 * ns], outs[-1]


def pair_wait(ssem, rsem, grads, slots, specs, after, tag):
    ns = len(specs)

    def body(*refs):
        for cp in _pair_copies(refs[:ns], refs[ns:2 * ns], specs, refs[2 * ns], refs[2 * ns + 1]):
            cp.wait_send()
            cp.wait_recv()

    outs = pl.pallas_call(
        body, in_specs=[HBM] * (2 * ns) + [SEM, SEM, ANY], out_specs=[HBM] * (2 * ns),
        out_shape=[pltpu.HBM(a.shape, a.dtype) for a in list(grads) + list(slots)],
        input_output_aliases={i: i for i in range(2 * ns)},
        name=f"pair_wait_{tag}", compiler_params=pltpu.CompilerParams(has_side_effects=EFFECT))(
            *grads, *slots, ssem, rsem, after)
    return outs[:ns], outs[ns:]


def pair_sum(g, r1, sp, tag):
    npc = len(sp.pieces)
    fetch_all = 4 * sp.rows * sp.cols * 4 <= (8 << 20)

    def body(g_ref, r_ref, own_ref, words_ref, gbuf, sem):
        l, j = pl.program_id(0), pl.program_id(1)
        x, y, c = _coords()

        def copies(chip, slot):
            return [pltpu.make_async_copy(g_ref.at[l, pl.ds(pl.multiple_of(sp.off(2 * chip + c, s), 8), n), :],
                                          gbuf.at[slot, pl.ds(s, n), :], sem.at[slot, i])
                    for i, (s, n) in enumerate(sp.pieces)]

        def fetch(chip, slot):
            for cp in copies(chip, slot):
                cp.start()

        def arrived(chip, slot):
            for cp in copies(chip, slot):
                cp.wait()

        if fetch_all:
            @pl.when(j == 0)
            def _():
                for chip in range(4):
                    fetch(chip, chip)
                for chip in range(4):
                    arrived(chip, chip)

            mine = gbuf[j]
        else:
            @pl.when(j == 0)
            def _():
                fetch(0, 0)

            @pl.when(j < 3)
            def _():
                fetch(j + 1, (j + 1) % 2)

            arrived(j, j % 2)
            mine = gbuf[j % 2]
        p = mine + r_ref[...]
        words_ref[...] = sp.to_words(p)

        @pl.when(j == 2 * x + y)
        def _():
            own_ref[...] = p

    return pl.pallas_call(
        body, grid=(sp.layers, 4),
        in_specs=[ANY, pl.BlockSpec((None, None, sp.rows, sp.cols), lambda l, j: (j, l, 0, 0))],
        out_specs=[pl.BlockSpec((None, sp.rows, sp.cols), lambda l, j: (l, 0, 0)),
                   pl.BlockSpec((None, None, sp.rows, sp.wcols), lambda l, j: (j, l, 0, 0))],
        out_shape=[jax.ShapeDtypeStruct((sp.layers, sp.rows, sp.cols), F32),
                   jax.ShapeDtypeStruct((4, sp.layers, sp.rows, sp.wcols), F32)],
        scratch_shapes=[pltpu.VMEM((4 if fetch_all else 2, sp.rows, sp.cols), F32), pltpu.SemaphoreType.DMA((4, npc))],
        name=f"pair_sum_{tag}", compiler_params=_params(("arbitrary", "arbitrary")))(g, r1)


def _chip_copies(srcs, dsts, ssem, rsem):
    x, y, c = _coords()
    copies = []
    for i, (src, dst) in enumerate(zip(srcs, dsts)):
        for t, (fx, fy) in enumerate(FLIPS):
            tx, ty = x ^ fx, y ^ fy
            copies.append(pltpu.make_async_remote_copy(src.at[2 * tx + ty], dst.at[t], ssem.at[3 * i + t], rsem.at[3 * i + t],
                                                       device_id=(tx, ty, c), device_id_type=MESH))
    return copies


def _slot_shapes(words):
    return [(3,) + w.shape[1:] for w in words]


def reduce_chips(words, specs, tag):
    ns = len(specs)

    def body(*refs):
        copies = _chip_copies(refs[:ns], refs[ns:2 * ns], *refs[2 * ns:])
        for cp in copies:
            cp.start()
        for cp in copies:
            cp.wait()

    return pl.pallas_call(
        body, in_specs=[ANY] * ns, out_specs=[ANY] * ns,
        out_shape=[jax.ShapeDtypeStruct(s, F32) for s in _slot_shapes(words)],
        scratch_shapes=[pltpu.SemaphoreType.DMA((3 * ns,)), pltpu.SemaphoreType.DMA((3 * ns,))],
        name=f"reduce_chips_{tag}", compiler_params=pltpu.CompilerParams(has_side_effects=True))(*words)


def chips_start(words, specs, tag):
    ns = len(specs)
    slots = [lax.empty(s, F32) for s in _slot_shapes(words)]

    def body(*refs):
        ssem, rsem = refs[2 * ns:2 * ns + 2]
        for cp in _chip_copies(refs[:ns], refs[ns:2 * ns], ssem, rsem):
            cp.start()
        refs[-1][...] = jnp.zeros_like(refs[-1])

    outs = pl.pallas_call(
        body, in_specs=[HBM] * (2 * ns),
        out_specs=[SEM, SEM] + [HBM] * (2 * ns) + [pl.BlockSpec(memory_space=pltpu.VMEM)],
        out_shape=[pltpu.SemaphoreType.DMA((3 * ns,)), pltpu.SemaphoreType.DMA((3 * ns,))]
        + [pltpu.HBM(a.shape, a.dtype) for a in list(words) + slots] + [jax.ShapeDtypeStruct((8, 128), F32)],
        input_output_aliases={i: 2 + i for i in range(2 * ns)},
        name=f"chips_start_{tag}", compiler_params=pltpu.CompilerParams(has_side_effects=EFFECT))(
            *_in_hbm(list(words) + slots))
    return outs[0], outs[1], outs[2:2 + ns], outs[2 + ns:2 + 2 * ns], outs[-1]


def chips_wait(ssem, rsem, words, slots, specs, after, tag):
    ns = len(specs)

    def body(*refs):
        for cp in _chip_copies(refs[:ns], refs[ns:2 * ns], refs[2 * ns], refs[2 * ns + 1]):
            cp.wait_send()
            cp.wait_recv()

    outs = pl.pallas_call(
        body, in_specs=[HBM] * (2 * ns) + [SEM, SEM, ANY], out_specs=[HBM] * (2 * ns),
        out_shape=[pltpu.HBM(a.shape, a.dtype) for a in list(words) + list(slots)],
        input_output_aliases={i: i for i in range(2 * ns)},
        name=f"chips_wait_{tag}", compiler_params=pltpu.CompilerParams(has_side_effects=EFFECT))(
            *words, *slots, ssem, rsem, after)
    return outs[ns:]


def sum_chips(own, r2, sp, tag):
    def body(own_ref, r_ref, o_ref):
        o_ref[...] = ((own_ref[...] + sp.from_words(r_ref[0])) + sp.from_words(r_ref[1])) + sp.from_words(r_ref[2])

    blk = pl.BlockSpec((None, sp.rows, sp.cols), lambda l: (l, 0, 0))
    return pl.pallas_call(
        body, grid=(sp.layers,), in_specs=[blk, pl.BlockSpec((3, None, sp.rows, sp.wcols), lambda l: (0, l, 0, 0))],
        out_specs=blk, out_shape=jax.ShapeDtypeStruct((sp.layers, sp.rows, sp.cols), F32),
        name=f"sum_chips_{tag}", compiler_params=_params(("arbitrary",)))(own, r2)


def reduce_scatter_start(grads, specs, names, dep, tag):
    theirs = reduce_pair(grads, specs, tag, dep)
    sums = [pair_sum(g, r1, sp, f"{tag}_{n}") for g, r1, sp, n in zip(grads, theirs, specs, names)]
    ssem, rsem, words, slots, token = chips_start([s[1] for s in sums], specs, tag)
    return (ssem, rsem, words, slots, [s[0] for s in sums]), token


def reduce_scatter_finish(state, after, specs, tag):
    ssem, rsem, words, slots, own = state
    return list(zip(own, chips_wait(ssem, rsem, words, slots, specs, after, tag)))


def reduce_scatter(grads, specs, names, tag, dep=None):
    theirs = reduce_pair(grads, specs, tag, dep)
    sums = [pair_sum(g, r1, sp, f"{tag}_{n}") for g, r1, sp, n in zip(grads, theirs, specs, names)]
    return list(zip([s[0] for s in sums], reduce_chips([s[1] for s in sums], specs, tag)))


def _adamw_math(w, g, m, v):
    c1 = 1.0 - ADAM_B1 ** ADAM_STEP
    c2 = 1.0 - ADAM_B2 ** ADAM_STEP
    m2 = ADAM_B1 * m + (1.0 - ADAM_B1) * g
    v2 = ADAM_B2 * v + (1.0 - ADAM_B2) * (g * g)
    return -ADAM_LR * ((m2 / c1) / (jnp.sqrt(v2 / c2) + ADAM_EPS) + ADAM_WD * w), m2, v2


def adamw_small(ws, gs, ms, vs):
    n = len(ws)
    flat = lambda a: a.reshape(math.prod(a.shape[:-1]), a.shape[-1])

    def body(*refs):
        ins, outs = refs[:4 * n], refs[4 * n:]
        for i in range(n):
            w_ref, g_ref, m_ref, v_ref = ins[4 * i:4 * i + 4]
            outs[3 * i][...], outs[3 * i + 1][...], outs[3 * i + 2][...] = _adamw_math(
                w_ref[...], g_ref[...], m_ref[...], v_ref[...])

    args = [flat(a) for quad in zip(ws, gs, ms, vs) for a in quad]
    res = pl.pallas_call(
        body, out_shape=[jax.ShapeDtypeStruct(flat(w).shape, F32) for w in ws for _ in range(3)],
        name="adamw_small", compiler_params=_params())(*args)
    return [[res[3 * i + k].reshape(ws[i].shape) for k in range(3)] for i in range(n)]


def adamw_layer(w, sums, m, v, sp, l, prev, dep, name):
    _, rows, cols = w.shape
    tc = _col_tile(cols)
    twc = tc // 2 if sp.packed else tc
    extra = ([] if prev is None else list(prev)) + ([] if dep is None else [dep])

    def body(w_ref, own_ref, r_ref, m_ref, v_ref, *rest):
        g_ref, d_ref, nm_ref, nv_ref = rest[-4:]
        g = ((own_ref[...] + sp.from_words(r_ref[0])) + sp.from_words(r_ref[1])) + sp.from_words(r_ref[2])
        g_ref[...] = g
        d_ref[...], nm_ref[...], nv_ref[...] = _adamw_math(w_ref[...], g, m_ref[...], v_ref[...])

    blk = pl.BlockSpec((None, rows, tc), lambda n: (l, 0, n))
    return pl.pallas_call(
        body, grid=(cols // tc,),
        in_specs=[blk, pl.BlockSpec((None, rows, tc), lambda n: (0, 0, n)),
                  pl.BlockSpec((3, None, rows, twc), lambda n: (0, 0, 0, n)), blk, blk] + [ANY] * len(extra),
        out_specs=[blk] * 4, out_shape=[jax.ShapeDtypeStruct(w.shape, F32)] * 4,
        input_output_aliases={} if prev is None else {5 + i: i for i in range(4)},
        name=f"adamw_{name}_l{l}", compiler_params=_params(("arbitrary",)))(w, sums[0], sums[1], m, v, *extra)


WEIGHTS = ("pre_norm_g", "w_in", "gm_ln_g", "gm_ln_b", "gm_ws", "gm_bs", "mla_q_norm_g", "mla_w_uq", "mla_kv_norm_g",
           "mla_w_ukv", "lru_conv_w", "lru_conv_b", "lru_w_a", "lru_b_a", "lru_w_x", "lru_b_x", "lru_lambda",
           "w_proj_a", "w_proj_b", "w_proj_c", "w_out", "post_norm_g")
SHARDED = ("w_in", "mla_w_uq", "mla_w_ukv", "lru_conv_w", "w_proj_a", "w_proj_b", "w_proj_c", "w_out")
REPLICATED = tuple(n for n in WEIGHTS if n not in SHARDED)


def _step(x, target, wts, ms, vs):
    t12 = lambda a: jnp.swapaxes(a, 1, 2)
    names = list(WEIGHT_SPECS)
    specs = [WEIGHT_SPECS[n] for n in names]
    tabs = _rope_tables()
    own = {"w_in_t": t12(wts["w_in"]), "wq": t12(wts["mla_w_uq"]), "wkv": t12(wts["mla_w_ukv"]),
           "conv": jnp.pad(t12(wts["lru_conv_w"]), ((0, 0), (0, 0), (0, 124))),
           "w_proj_a": wts["w_proj_a"], "w_proj_b": wts["w_proj_b"], "w_proj_c": wts["w_proj_c"], "w_out": wts["w_out"]}
    first, rest = ["w_in_t"], [n for n in names if n != "w_in_t"]
    sfirst, srest = [WEIGHT_SPECS[n] for n in first], [WEIGHT_SPECS[n] for n in rest]

    w = {n: wts[n] for n in REPLICATED}
    w["kv_g384"] = jnp.concatenate([wts["mla_kv_norm_g"], jnp.ones((L, 128), F32)], axis=1)

    def layer_weights(ns, words):
        gw = dict(zip(ns, words))
        gw["wq"] = gw["wq"].reshape(HEADS, 192, 384)
        gw["wkv"] = gw["wkv"].reshape(HEADS, 256, 128)
        gw["conv"] = gw["conv"][0, :, :4].T
        return gw

    place = lambda l, dep: {n: pack_place(own[n], WEIGHT_SPECS[n], l, f"w{l}_{n}", dep) for n in names}
    placed = [place(0, None)]
    words_of = lambda l, ns: [placed[l][n][0] for n in ns]
    bufs_of = lambda l, ns: [placed[l][n][1] for n in ns]
    later = {}

    ssem_a, rsem_a, wthru_a, fthru_a, token_a = gather_start(words_of(0, first), bufs_of(0, first), sfirst, None, "w0a")
    placed.append(place(1, token_a))
    win0 = gather_forward(gather_wait(ssem_a, rsem_a, wthru_a, fthru_a, sfirst, placed[1]["w_in_t"][0], "w0a"), sfirst, "w0a")
    ssem_b, rsem_b, wthru_b, fthru_b, token_b = gather_start(words_of(0, rest), bufs_of(0, rest), srest, win0[0], "w0b")
    ssem1, rsem1, wthru1, fthru1, token1 = gather_start(words_of(1, names), bufs_of(1, names), specs, token_b, "w1")

    def fwd0_mid(ya):
        rest0 = gather_forward(gather_wait(ssem_b, rsem_b, wthru_b, fthru_b, srest, ya, "w0b"), srest, "w0b")
        later["gw0"] = layer_weights(first + rest, list(win0) + list(rest0))
        return later["gw0"], None

    x1, saved0 = _layer_fwd(x, 0, w, {"w_in_t": win0[0]}, tabs, dep=token1, mid=fwd0_mid)
    words1 = gather_forward(gather_wait(ssem1, rsem1, wthru1, fthru1, specs, x1, "w1"), specs, "w1")
    gw0, gw1 = later["gw0"], layer_weights(names, words1)
    x2, saved1 = _layer_fwd(x1, 1, w, gw1, tabs)
    loss, dx2 = loss_head(x2, target)

    def bwd1_mid(gg, last):
        later["p1b"] = pair_start([gg[n] for n in rest], srest, last, "g1b")
        return later["p1b"][4]

    dx1, gg1, g1 = _layer_bwd(dx2, 1, w, gw1, tabs, saved1, mid=bwd1_mid)
    grads1b, theirs1b = pair_wait(*later["p1b"][:4], srest, dx1, "g1b")
    p1a = pair_start([gg1["w_in_t"]], sfirst, theirs1b[0], "g1a")

    def bwd0_early(last):
        grads1a, theirs1a = pair_wait(*p1a[:4], sfirst, last, "g1a")
        mine = dict(zip(first + rest, list(grads1a) + list(grads1b)))
        theirs = dict(zip(first + rest, list(theirs1a) + list(theirs1b)))
        sums = [pair_sum(mine[n], theirs[n], WEIGHT_SPECS[n], f"g1_{n}") for n in names]
        ssem, rsem, words, slots, token = chips_start([s[1] for s in sums], specs, "g1")
        later["g1"] = (ssem, rsem, words, slots, [s[0] for s in sums])
        return token

    def bwd0_mid(gg, last):
        later["g0b"], token = reduce_scatter_start([gg[n] for n in rest], srest, rest, last, "g0b")
        return token

    def bwd0_late(g_win):
        later["p0a"] = pair_start([g_win], sfirst, None, "g0a")
        return later["p0a"][4]

    dx0, gg0, g0 = _layer_bwd(dx1, 0, w, gw0, tabs, saved0, dep=p1a[4], early=bwd0_early, mid=bwd0_mid, late=bwd0_late)
    s1 = dict(zip(names, reduce_scatter_finish(later["g1"], dx0, specs, "g1")))
    s0 = dict(zip(rest, reduce_scatter_finish(later["g0b"], dx0, srest, "g0b")))

    grads0a, theirs0a = pair_wait(*later["p0a"][:4], sfirst, dx0, "g0a")
    own0a, words0a = pair_sum(grads0a[0], theirs0a[0], sfirst[0], "g0a_w_in_t")
    ssem_g, rsem_g, wthru_g, slots_g, token_g = chips_start([words0a], sfirst, "g0a")

    keys = {"w_in": "w_in_t", "mla_w_uq": "wq", "mla_w_ukv": "wkv",
            "w_proj_a": "w_proj_a", "w_proj_b": "w_proj_b", "w_proj_c": "w_proj_c", "w_out": "w_out"}
    transposed = ("w_in", "mla_w_uq", "mla_w_ukv")
    state_of = lambda n: [own[keys[n]], t12(ms[n]), t12(vs[n])] if n in transposed else [wts[n], ms[n], vs[n]]

    def update(n, l, sums, prev, dep):
        wl, ml, vl = state_of(n)
        return adamw_layer(wl, sums[keys[n]], ml, vl, WEIGHT_SPECS[keys[n]], l, prev, dep, n)

    upd = {n: update(n, 1, s1, None, token_g) for n in keys}
    for n in keys:
        if n != "w_in":
            upd[n] = update(n, 0, s0, upd[n], None)
    rep_flat = jnp.concatenate([jnp.stack([g0[n], g1[n]]).reshape(-1) for n in REPLICATED] + [loss[None]])
    rep_flat = jnp.pad(rep_flat, (0, REP_ROWS * NDEV * D - rep_flat.shape[0])).reshape(1, REP_ROWS * NDEV, D)
    rep_parts = reduce_scatter([rep_flat], [REP_SPEC], ["rep"], "rep", upd["w_out"][0])[0]
    rep_sum = sum_chips(*rep_parts, REP_SPEC, "rep")
    rep_full = all_gather([rep_sum], 0, [REP_SPEC], ["rep"], "rep")[0].reshape(-1)

    out = {}
    conv_sp = WEIGHT_SPECS["conv"]
    g_conv = t12(jnp.concatenate([sum_chips(*s0["conv"], conv_sp, "conv0"), sum_chips(*s1["conv"], conv_sp, "conv1")])[:, :, :4])
    small = {"lru_conv_w": g_conv}
    at = 0
    for n in REPLICATED:
        size = math.prod(wts[n].shape)
        small[n] = rep_full[at:at + size].reshape(wts[n].shape)
        at += size
    updates = adamw_small([wts[n] for n in small], list(small.values()), [ms[n] for n in small], [vs[n] for n in small])
    for n, u in zip(small, updates):
        out[n] = [small[n]] + u

    landed = chips_wait(ssem_g, rsem_g, wthru_g, slots_g, sfirst, out[REPLICATED[-1]][1], "g0a")
    s0["w_in_t"] = (own0a, landed[0])
    upd["w_in"] = update("w_in", 0, s0, upd["w_in"], None)
    out.update({n: [t12(r) for r in upd[n]] if n in transposed else upd[n] for n in keys})

    return (rep_full[at], dx0[None], *[out[n][k] for k in range(4) for n in WEIGHTS])


def kernel(x, pre_norm_g, w_in, gm_ln_g, gm_ln_b, gm_ws, gm_bs, mla_q_norm_g, mla_w_uq, mla_kv_norm_g, mla_w_ukv, lru_conv_w, lru_conv_b, lru_w_a, lru_b_a, lru_w_x, lru_b_x, lru_lambda, w_proj_a, w_proj_b, w_proj_c, w_out, post_norm_g, loss_target, m_pre_norm_g, m_w_in, m_gm_ln_g, m_gm_ln_b, m_gm_ws, m_gm_bs, m_mla_q_norm_g, m_mla_w_uq, m_mla_kv_norm_g, m_mla_w_ukv, m_lru_conv_w, m_lru_conv_b, m_lru_w_a, m_lru_b_a, m_lru_w_x, m_lru_b_x, m_lru_lambda, m_w_proj_a, m_w_proj_b, m_w_proj_c, m_w_out, m_post_norm_g, v_pre_norm_g, v_w_in, v_gm_ln_g, v_gm_ln_b, v_gm_ws, v_gm_bs, v_mla_q_norm_g, v_mla_w_uq, v_mla_kv_norm_g, v_mla_w_ukv, v_lru_conv_w, v_lru_conv_b, v_lru_w_a, v_lru_b_a, v_lru_w_x, v_lru_b_x, v_lru_lambda, v_w_proj_a, v_w_proj_b, v_w_proj_c, v_w_out, v_post_norm_g):
    wts = dict(zip(WEIGHTS, (pre_norm_g, w_in, gm_ln_g, gm_ln_b, gm_ws, gm_bs, mla_q_norm_g, mla_w_uq, mla_kv_norm_g,
                             mla_w_ukv, lru_conv_w, lru_conv_b, lru_w_a, lru_b_a, lru_w_x, lru_b_x, lru_lambda,
                             w_proj_a, w_proj_b, w_proj_c, w_out, post_norm_g)))
    ms = dict(zip(WEIGHTS, (m_pre_norm_g, m_w_in, m_gm_ln_g, m_gm_ln_b, m_gm_ws, m_gm_bs, m_mla_q_norm_g, m_mla_w_uq,
                            m_mla_kv_norm_g, m_mla_w_ukv, m_lru_conv_w, m_lru_conv_b, m_lru_w_a, m_lru_b_a, m_lru_w_x,
                            m_lru_b_x, m_lru_lambda, m_w_proj_a, m_w_proj_b, m_w_proj_c, m_w_out, m_post_norm_g)))
    vs = dict(zip(WEIGHTS, (v_pre_norm_g, v_w_in, v_gm_ln_g, v_gm_ln_b, v_gm_ws, v_gm_bs, v_mla_q_norm_g, v_mla_w_uq,
                            v_mla_kv_norm_g, v_mla_w_ukv, v_lru_conv_w, v_lru_conv_b, v_lru_w_a, v_lru_b_a, v_lru_w_x,
                            v_lru_b_x, v_lru_lambda, v_w_proj_a, v_w_proj_b, v_w_proj_c, v_w_out, v_post_norm_g)))
    return _step(x[0], loss_target[0], wts, ms, vs)
```
